```python
import math
import jax, jax.numpy as jnp
from jax import lax
import numpy as np

D_MODEL = 2048
BATCH = 8
SEQ = 2048
DEPTH = 4

N_MIXERS = 3
N_A = (DEPTH + 2) // 3
N_B = (DEPTH + 1) // 3
N_C = DEPTH // 3

HG_EXPAND = 128
HG_HEADS = D_MODEL // HG_EXPAND
HG_KEY_DIM = HG_EXPAND
HG_VAL_DIM = D_MODEL // HG_HEADS
HG_KEY_WIDTH = HG_HEADS * HG_KEY_DIM
HG_VAL_WIDTH = HG_HEADS * HG_VAL_DIM
HG_IN_WIDTH = 2 * HG_KEY_WIDTH + 2 * HG_VAL_WIDTH
HG_CHUNK = 64

SB_HEADS = 16
SB_HEAD_DIM = D_MODEL // SB_HEADS
SB_WIDTH = SB_HEADS * SB_HEAD_DIM
SB_BLOCK = 128

POOL_WINDOWS = (2, 4, 8, 16)
N_POOL = len(POOL_WINDOWS)
POOL_GROUP = D_MODEL // N_POOL
MAX_WIN = max(POOL_WINDOWS)

D_FF = 5632
CONV_WIDTH = 3
NORM_EPS = 1e-6

kernel_name = "hybrid_hgrn2_stickbreak_pool_trunk"


def rms_norm(x, g):
    xf = x.astype(jnp.float32)
    y = xf * lax.rsqrt(jnp.mean(xf * xf, axis=-1, keepdims=True) + NORM_EPS)
    return (y * g.astype(jnp.float32)).astype(x.dtype)


def hgrn2_mixer(h, w_in, onorm_g, w_out, lb):
    B, T, _ = h.shape
    nc = T // HG_CHUNK
    proj = h @ w_in
    q, f_pre, i_val, g_pre = jnp.split(
        proj, [HG_KEY_WIDTH, 2 * HG_KEY_WIDTH, 2 * HG_KEY_WIDTH + HG_VAL_WIDTH], axis=-1)
    q = jax.nn.silu(q.astype(jnp.float32))
    lbf = lb.astype(jnp.float32)
    log_f = jnp.logaddexp(jnp.log(lbf), jnp.log1p(-lbf) + jax.nn.log_sigmoid(f_pre.astype(jnp.float32)))
    k = -jnp.expm1(log_f)
    v = i_val.astype(jnp.float32)

    def to_chunks(a, d):
        return a.reshape(B, nc, HG_CHUNK, HG_HEADS, d).transpose(1, 0, 3, 2, 4)

    qc_all = to_chunks(q, HG_KEY_DIM)
    kc_all = to_chunks(k, HG_KEY_DIM)
    vc_all = to_chunks(v, HG_VAL_DIM)
    bc_all = lax.cumsum(to_chunks(log_f, HG_KEY_DIM), axis=3)
    incl = jnp.tril(jnp.ones((HG_CHUNK, HG_CHUNK), dtype=bool))

    def step(S, inp):
        qc, kc, vc, bc = inp
        b_last = bc[:, :, -1:, :]
        o_inter = jnp.einsum('bhck,bhkv->bhcv', qc * jnp.exp(bc), S)
        diff = bc[:, :, :, None, :] - bc[:, :, None, :, :]
        decay = jnp.exp(jnp.where(incl[:, :, None], diff, -jnp.inf))
        scores = jnp.einsum('bhtk,bhsk,bhtsk->bhts', qc, kc, decay)
        o_intra = jnp.einsum('bhts,bhsv->bhtv', scores, vc)
        k_dec = kc * jnp.exp(b_last - bc)
        S_new = jnp.exp(b_last[:, :, 0, :])[..., None] * S + jnp.einsum('bhck,bhcv->bhkv', k_dec, vc)
        return S_new, o_inter + o_intra

    S0 = jnp.zeros((B, HG_HEADS, HG_KEY_DIM, HG_VAL_DIM), jnp.float32)
    _, o = lax.scan(step, S0, (qc_all, kc_all, vc_all, bc_all))
    o = o.transpose(1, 0, 3, 2, 4).reshape(B, T, HG_HEADS, HG_VAL_DIM)
    o = o * lax.rsqrt(jnp.mean(o * o, axis=-1, keepdims=True) + NORM_EPS)
    o = o * onorm_g.astype(jnp.float32).reshape(HG_HEADS, HG_VAL_DIM)
    o = o.reshape(B, T, HG_VAL_WIDTH) * jax.nn.silu(g_pre.astype(jnp.float32))
    return o.astype(h.dtype) @ w_out


def stick_breaking_mixer(h, w_qkv, w_out):
    B, T, _ = h.shape
    qkv = (h @ w_qkv).reshape(B, T, 3, SB_HEADS, SB_HEAD_DIM).astype(jnp.float32)
    q = qkv[:, :, 0] * (SB_HEAD_DIM ** -0.5)
    k = qkv[:, :, 1]
    v = qkv[:, :, 2]
    outs = []
    for blk in range(T // SB_BLOCK):
        t0 = blk * SB_BLOCK
        t1 = t0 + SB_BLOCK
        z = jnp.einsum('bqhd,bshd->bhqs', q[:, t0:t1], k[:, :t1])
        strict = jnp.arange(t1)[None, :] < (t0 + jnp.arange(SB_BLOCK))[:, None]
        sp = jnp.where(strict, jax.nn.softplus(z), 0.0)
        rem = lax.cumsum(sp, axis=3, reverse=True) - sp
        a = jnp.where(strict, jnp.exp(jax.nn.log_sigmoid(z) - rem), 0.0)
        outs.append(jnp.einsum('bhqs,bshd->bqhd', a, v[:, :t1]))
    o = jnp.concatenate(outs, axis=1).reshape(B, T, SB_WIDTH)
    return o.astype(h.dtype) @ w_out


def multiscale_pool_mixer(h, pool_w, pool_scale):
    B, T, _ = h.shape
    hf = h.astype(jnp.float32)
    cs = jnp.pad(lax.cumsum(hf, axis=1), ((0, 0), (MAX_WIN, 0), (0, 0)))
    count_pos = jnp.arange(T)[None, :, None] + 1
    groups = []
    for g, w in enumerate(POOL_WINDOWS):
        c0, c1 = g * POOL_GROUP, (g + 1) * POOL_GROUP
        window_sum = cs[:, MAX_WIN:, c0:c1] - cs[:, MAX_WIN - w:MAX_WIN - w + T, c0:c1]
        count = jnp.minimum(count_pos, w).astype(jnp.float32)
        groups.append(window_sum / count - hf[:, :, c0:c1])
    p = jnp.stack(groups, axis=2)
    y = jnp.einsum('btgc,gcd->btgd', p, pool_w.astype(jnp.float32)).reshape(B, T, D_MODEL)
    return (y * pool_scale.astype(jnp.float32)).astype(h.dtype)


def conv_gated_ffn(h, w_up, conv_w, conv_b, w_down):
    u = h @ w_up
    c = u.shape[-1]
    uc = lax.conv_general_dilated(
        u, conv_w[:, None, :].astype(u.dtype), window_strides=(1,),
        padding=[(CONV_WIDTH - 1, 0)], dimension_numbers=('NWC', 'WIO', 'NWC'),
        feature_group_count=c) + conv_b
    gate, val = jnp.split(uc, 2, axis=-1)
    return (jax.nn.silu(gate) * val) @ w_down


def _fwd_setup_inputs(seed: int = 0) -> dict:
    key = jax.random.key(seed)
    ks = jax.random.split(key, 14)
    f32 = jnp.float32

    def nrm(k, shape, scale):
        return jax.random.normal(k, shape, f32) * scale

    return {
        "x": nrm(ks[0], (BATCH, SEQ, D_MODEL), 1.0),
        "norm_g": 1.0 + nrm(ks[1], (DEPTH, 4, D_MODEL), 0.05),
        "hgrn_lb_logits": nrm(ks[2], (DEPTH, HG_KEY_WIDTH), 0.5),
        "hgrn_w_in": nrm(ks[3], (N_A, D_MODEL, HG_IN_WIDTH), D_MODEL ** -0.5),
        "hgrn_onorm_g": 1.0 + nrm(ks[4], (N_A, HG_VAL_WIDTH), 0.05),
        "hgrn_w_out": nrm(ks[5], (N_A, HG_VAL_WIDTH, D_MODEL), HG_VAL_WIDTH ** -0.5),
        "sba_w_qkv": nrm(ks[6], (N_B, D_MODEL, 3 * SB_WIDTH), D_MODEL ** -0.5),
        "sba_w_out": nrm(ks[7], (N_B, SB_WIDTH, D_MODEL), SB_WIDTH ** -0.5),
        "pool_w": nrm(ks[8], (N_C, N_POOL, POOL_GROUP, POOL_GROUP), POOL_GROUP ** -0.5),
        "pool_scale": 1.0 + nrm(ks[9], (N_C, D_MODEL), 0.1),
        "ffn_w_up": nrm(ks[10], (DEPTH, D_MODEL, 2 * D_FF), D_MODEL ** -0.5),
        "ffn_conv_w": nrm(ks[11], (DEPTH, CONV_WIDTH, 2 * D_FF), CONV_WIDTH ** -0.5),
        "ffn_conv_b": nrm(ks[12], (DEPTH, 2 * D_FF), 0.02),
        "ffn_w_down": nrm(ks[13], (DEPTH, D_FF, D_MODEL), D_FF ** -0.5),
    }


def _fwd_reference(x, norm_g, hgrn_lb_logits, hgrn_w_in, hgrn_onorm_g, hgrn_w_out,
              sba_w_qkv, sba_w_out, pool_w, pool_scale,
              ffn_w_up, ffn_conv_w, ffn_conv_b, ffn_w_down):
    lb_soft = jax.nn.softmax(hgrn_lb_logits.astype(jnp.float32), axis=0)
    lower_bounds = jnp.concatenate(
        [jnp.zeros_like(lb_soft[:1]), lax.cumsum(lb_soft[1:], axis=0)], axis=0)
    h = x
    for i in range(DEPTH):
        kind, j = i % N_MIXERS, i // N_MIXERS
        u = rms_norm(h, norm_g[i, 0])
        if kind == 0:
            m = hgrn2_mixer(u, hgrn_w_in[j], hgrn_onorm_g[j], hgrn_w_out[j], lower_bounds[i])
        elif kind == 1:
            m = stick_breaking_mixer(u, sba_w_qkv[j], sba_w_out[j])
        else:
            m = multiscale_pool_mixer(u, pool_w[j], pool_scale[j])
        h = h + rms_norm(m, norm_g[i, 1])
        u = rms_norm(h, norm_g[i, 2])
        f = conv_gated_ffn(u, ffn_w_up[i], ffn_conv_w[i], ffn_conv_b[i], ffn_w_down[i])
        h = h + rms_norm(f, norm_g[i, 3])
    return h


import jax as _jax
import jax.numpy as _jnp

TWIN_FORMAT = 'train_step'
FWD_PARAMS = ['x', 'norm_g', 'hgrn_lb_logits', 'hgrn_w_in', 'hgrn_onorm_g', 'hgrn_w_out', 'sba_w_qkv', 'sba_w_out', 'pool_w', 'pool_scale', 'ffn_w_up', 'ffn_conv_w', 'ffn_conv_b', 'ffn_w_down']
TWIN_WEIGHTS = ['norm_g', 'hgrn_lb_logits', 'hgrn_w_in', 'hgrn_onorm_g', 'hgrn_w_out', 'sba_w_qkv', 'sba_w_out', 'pool_w', 'pool_scale', 'ffn_w_up', 'ffn_conv_w', 'ffn_conv_b', 'ffn_w_down']
TWIN_DIFF_INPUT = 'x'
TWIN_INPUTS = ['x', 'norm_g', 'hgrn_lb_logits', 'hgrn_w_in', 'hgrn_onorm_g', 'hgrn_w_out', 'sba_w_qkv', 'sba_w_out', 'pool_w', 'pool_scale', 'ffn_w_up', 'ffn_conv_w', 'ffn_conv_b', 'ffn_w_down', 'loss_target', 'm_norm_g', 'm_hgrn_lb_logits', 'm_hgrn_w_in', 'm_hgrn_onorm_g', 'm_hgrn_w_out', 'm_sba_w_qkv', 'm_sba_w_out', 'm_pool_w', 'm_pool_scale', 'm_ffn_w_up', 'm_ffn_conv_w', 'm_ffn_conv_b', 'm_ffn_w_down', 'v_norm_g', 'v_hgrn_lb_logits', 'v_hgrn_w_in', 'v_hgrn_onorm_g', 'v_hgrn_w_out', 'v_sba_w_qkv', 'v_sba_w_out', 'v_pool_w', 'v_pool_scale', 'v_ffn_w_up', 'v_ffn_conv_w', 'v_ffn_conv_b', 'v_ffn_w_down']
TWIN_OUTPUTS = ['loss', 'grad_x', 'grad_norm_g', 'grad_hgrn_lb_logits', 'grad_hgrn_w_in', 'grad_hgrn_onorm_g', 'grad_hgrn_w_out', 'grad_sba_w_qkv', 'grad_sba_w_out', 'grad_pool_w', 'grad_pool_scale', 'grad_ffn_w_up', 'grad_ffn_conv_w', 'grad_ffn_conv_b', 'grad_ffn_w_down', 'delta_norm_g', 'delta_hgrn_lb_logits', 'delta_hgrn_w_in', 'delta_hgrn_onorm_g', 'delta_hgrn_w_out', 'delta_sba_w_qkv', 'delta_sba_w_out', 'delta_pool_w', 'delta_pool_scale', 'delta_ffn_w_up', 'delta_ffn_conv_w', 'delta_ffn_conv_b', 'delta_ffn_w_down', 'new_m_norm_g', 'new_m_hgrn_lb_logits', 'new_m_hgrn_w_in', 'new_m_hgrn_onorm_g', 'new_m_hgrn_w_out', 'new_m_sba_w_qkv', 'new_m_sba_w_out', 'new_m_pool_w', 'new_m_pool_scale', 'new_m_ffn_w_up', 'new_m_ffn_conv_w', 'new_m_ffn_conv_b', 'new_m_ffn_w_down', 'new_v_norm_g', 'new_v_hgrn_lb_logits', 'new_v_hgrn_w_in', 'new_v_hgrn_onorm_g', 'new_v_hgrn_w_out', 'new_v_sba_w_qkv', 'new_v_sba_w_out', 'new_v_pool_w', 'new_v_pool_scale', 'new_v_ffn_w_up', 'new_v_ffn_conv_w', 'new_v_ffn_conv_b', 'new_v_ffn_w_down']
TWIN_LEAF_KINDS = {'loss': 'loss', 'grad_x': 'grad_x', 'grad_norm_g': 'grad_w', 'grad_hgrn_lb_logits': 'grad_w', 'grad_hgrn_w_in': 'grad_w', 'grad_hgrn_onorm_g': 'grad_w', 'grad_hgrn_w_out': 'grad_w', 'grad_sba_w_qkv': 'grad_w', 'grad_sba_w_out': 'grad_w', 'grad_pool_w': 'grad_w', 'grad_pool_scale': 'grad_w', 'grad_ffn_w_up': 'grad_w', 'grad_ffn_conv_w': 'grad_w', 'grad_ffn_conv_b': 'grad_w', 'grad_ffn_w_down': 'grad_w', 'delta_norm_g': 'delta_w', 'delta_hgrn_lb_logits': 'delta_w', 'delta_hgrn_w_in': 'delta_w', 'delta_hgrn_onorm_g': 'delta_w', 'delta_hgrn_w_out': 'delta_w', 'delta_sba_w_qkv': 'delta_w', 'delta_sba_w_out': 'delta_w', 'delta_pool_w': 'delta_w', 'delta_pool_scale': 'delta_w', 'delta_ffn_w_up': 'delta_w', 'delta_ffn_conv_w': 'delta_w', 'delta_ffn_conv_b': 'delta_w', 'delta_ffn_w_down': 'delta_w', 'new_m_norm_g': 'new_m', 'new_m_hgrn_lb_logits': 'new_m', 'new_m_hgrn_w_in': 'new_m', 'new_m_hgrn_onorm_g': 'new_m', 'new_m_hgrn_w_out': 'new_m', 'new_m_sba_w_qkv': 'new_m', 'new_m_sba_w_out': 'new_m', 'new_m_pool_w': 'new_m', 'new_m_pool_scale': 'new_m', 'new_m_ffn_w_up': 'new_m', 'new_m_ffn_conv_w': 'new_m', 'new_m_ffn_conv_b': 'new_m', 'new_m_ffn_w_down': 'new_m', 'new_v_norm_g': 'new_v', 'new_v_hgrn_lb_logits': 'new_v', 'new_v_hgrn_w_in': 'new_v', 'new_v_hgrn_onorm_g': 'new_v', 'new_v_hgrn_w_out': 'new_v', 'new_v_sba_w_qkv': 'new_v', 'new_v_sba_w_out': 'new_v', 'new_v_pool_w': 'new_v', 'new_v_pool_scale': 'new_v', 'new_v_ffn_w_up': 'new_v', 'new_v_ffn_conv_w': 'new_v', 'new_v_ffn_conv_b': 'new_v', 'new_v_ffn_w_down': 'new_v'}


def _forward(args):
    return _fwd_reference(*[args[k] for k in FWD_PARAMS])


def _output_shape():
    out = _jax.eval_shape(lambda: _forward(_fwd_setup_inputs(0)))
    return out.shape, out.dtype

N_MICROBATCH = 1
ADAM_LR = 0.001
ADAM_B1 = 0.9
ADAM_B2 = 0.999
ADAM_EPS = 1e-08
ADAM_WD = 0.01
ADAM_STEP = 10
PER_EXAMPLE_BATCH_AXIS = {'x': 0, 'loss_target': 0}
SHARED_INPUTS = []
_WEIGHT_DTYPES = {'norm_g': _jnp.float32, 'hgrn_lb_logits': _jnp.float32, 'hgrn_w_in': _jnp.float32, 'hgrn_onorm_g': _jnp.float32, 'hgrn_w_out': _jnp.float32, 'sba_w_qkv': _jnp.float32, 'sba_w_out': _jnp.float32, 'pool_w': _jnp.float32, 'pool_scale': _jnp.float32, 'ffn_w_up': _jnp.float32, 'ffn_conv_w': _jnp.float32, 'ffn_conv_b': _jnp.float32, 'ffn_w_down': _jnp.float32}
MOMENT_SCALE = {'norm_g': 5.596024e+00, 'hgrn_lb_logits': 1.741760e-02, 'hgrn_w_in': 3.537351e-01, 'hgrn_onorm_g': 4.881298e-01, 'hgrn_w_out': 4.953086e-01, 'sba_w_qkv': 2.684144e-01, 'sba_w_out': 4.033218e-01, 'pool_w': 6.291383e-01, 'pool_scale': 1.004562e+00, 'ffn_w_up': 1.949517e-01, 'ffn_conv_w': 1.981028e-01, 'ffn_conv_b': 2.406064e-01, 'ffn_w_down': 3.294878e-01}


def _to_microbatches(a, axis):
    t = _jnp.moveaxis(a, axis, 0)
    t = t.reshape((N_MICROBATCH, t.shape[0] // N_MICROBATCH) + t.shape[1:])
    return _jnp.moveaxis(t, 1, axis + 1)


def setup_inputs(seed: int = 0) -> dict:
    inp = _fwd_setup_inputs(seed)
    key = _jax.random.fold_in(_jax.random.key(seed), 7919)
    shape, _ = _output_shape()
    out = dict(inp)
    out["loss_target"] = _jax.random.normal(_jax.random.fold_in(key, 0), shape, _jnp.float32)
    for i, name in enumerate(TWIN_WEIGHTS):
        w = inp[name].astype(_jnp.float32)
        if MOMENT_SCALE is None:
            s = _jnp.sqrt(_jnp.mean(_jnp.square(w)) + 1e-30)
        else:
            s = MOMENT_SCALE[name]
        km, kv = _jax.random.split(_jax.random.fold_in(key, i + 1))
        out[name] = w
        out["m_" + name] = s * _jax.random.normal(km, w.shape, _jnp.float32)
        out["v_" + name] = (s * s) * _jax.random.uniform(kv, w.shape, _jnp.float32, 0.5, 1.5)
    if N_MICROBATCH > 1:
        for name, axis in PER_EXAMPLE_BATCH_AXIS.items():
            out[name] = _to_microbatches(out[name], axis)
    return {'x': out['x'], 'norm_g': out['norm_g'], 'hgrn_lb_logits': out['hgrn_lb_logits'], 'hgrn_w_in': out['hgrn_w_in'], 'hgrn_onorm_g': out['hgrn_onorm_g'], 'hgrn_w_out': out['hgrn_w_out'], 'sba_w_qkv': out['sba_w_qkv'], 'sba_w_out': out['sba_w_out'], 'pool_w': out['pool_w'], 'pool_scale': out['pool_scale'], 'ffn_w_up': out['ffn_w_up'], 'ffn_conv_w': out['ffn_conv_w'], 'ffn_conv_b': out['ffn_conv_b'], 'ffn_w_down': out['ffn_w_down'], 'loss_target': out['loss_target'], 'm_norm_g': out['m_norm_g'], 'm_hgrn_lb_logits': out['m_hgrn_lb_logits'], 'm_hgrn_w_in': out['m_hgrn_w_in'], 'm_hgrn_onorm_g': out['m_hgrn_onorm_g'], 'm_hgrn_w_out': out['m_hgrn_w_out'], 'm_sba_w_qkv': out['m_sba_w_qkv'], 'm_sba_w_out': out['m_sba_w_out'], 'm_pool_w': out['m_pool_w'], 'm_pool_scale': out['m_pool_scale'], 'm_ffn_w_up': out['m_ffn_w_up'], 'm_ffn_conv_w': out['m_ffn_conv_w'], 'm_ffn_conv_b': out['m_ffn_conv_b'], 'm_ffn_w_down': out['m_ffn_w_down'], 'v_norm_g': out['v_norm_g'], 'v_hgrn_lb_logits': out['v_hgrn_lb_logits'], 'v_hgrn_w_in': out['v_hgrn_w_in'], 'v_hgrn_onorm_g': out['v_hgrn_onorm_g'], 'v_hgrn_w_out': out['v_hgrn_w_out'], 'v_sba_w_qkv': out['v_sba_w_qkv'], 'v_sba_w_out': out['v_sba_w_out'], 'v_pool_w': out['v_pool_w'], 'v_pool_scale': out['v_pool_scale'], 'v_ffn_w_up': out['v_ffn_w_up'], 'v_ffn_conv_w': out['v_ffn_conv_w'], 'v_ffn_conv_b': out['v_ffn_conv_b'], 'v_ffn_w_down': out['v_ffn_w_down']}


def _loss(weights, diff, rest, loss_target):
    with _jax.named_scope("forward"):
        args = {**rest, TWIN_DIFF_INPUT: diff, **{k: w.astype(_WEIGHT_DTYPES[k]) for k, w in weights.items()}}
        y = _forward(args)
    with _jax.named_scope("loss_head"):
        err = _jnp.square(y.astype(_jnp.float32) - loss_target)
        return 0.5 * _jnp.sum(_jnp.mean(err, axis=-1)) if err.ndim else 0.5 * err


def _adamw(w, g, m, v):
    m = ADAM_B1 * m + (1.0 - ADAM_B1) * g
    v = ADAM_B2 * v + (1.0 - ADAM_B2) * _jnp.square(g)
    m_hat = m / (1.0 - ADAM_B1 ** ADAM_STEP)
    v_hat = v / (1.0 - ADAM_B2 ** ADAM_STEP)
    delta = -ADAM_LR * (m_hat / (_jnp.sqrt(v_hat) + ADAM_EPS) + ADAM_WD * w)
    return delta, m, v


def reference(x, norm_g, hgrn_lb_logits, hgrn_w_in, hgrn_onorm_g, hgrn_w_out, sba_w_qkv, sba_w_out, pool_w, pool_scale, ffn_w_up, ffn_conv_w, ffn_conv_b, ffn_w_down, loss_target, m_norm_g, m_hgrn_lb_logits, m_hgrn_w_in, m_hgrn_onorm_g, m_hgrn_w_out, m_sba_w_qkv, m_sba_w_out, m_pool_w, m_pool_scale, m_ffn_w_up, m_ffn_conv_w, m_ffn_conv_b, m_ffn_w_down, v_norm_g, v_hgrn_lb_logits, v_hgrn_w_in, v_hgrn_onorm_g, v_hgrn_w_out, v_sba_w_qkv, v_sba_w_out, v_pool_w, v_pool_scale, v_ffn_w_up, v_ffn_conv_w, v_ffn_conv_b, v_ffn_w_down):
    given = dict(x=x, norm_g=norm_g, hgrn_lb_logits=hgrn_lb_logits, hgrn_w_in=hgrn_w_in, hgrn_onorm_g=hgrn_onorm_g, hgrn_w_out=hgrn_w_out, sba_w_qkv=sba_w_qkv, sba_w_out=sba_w_out, pool_w=pool_w, pool_scale=pool_scale, ffn_w_up=ffn_w_up, ffn_conv_w=ffn_conv_w, ffn_conv_b=ffn_conv_b, ffn_w_down=ffn_w_down, loss_target=loss_target, m_norm_g=m_norm_g, m_hgrn_lb_logits=m_hgrn_lb_logits, m_hgrn_w_in=m_hgrn_w_in, m_hgrn_onorm_g=m_hgrn_onorm_g, m_hgrn_w_out=m_hgrn_w_out, m_sba_w_qkv=m_sba_w_qkv, m_sba_w_out=m_sba_w_out, m_pool_w=m_pool_w, m_pool_scale=m_pool_scale, m_ffn_w_up=m_ffn_w_up, m_ffn_conv_w=m_ffn_conv_w, m_ffn_conv_b=m_ffn_conv_b, m_ffn_w_down=m_ffn_w_down, v_norm_g=v_norm_g, v_hgrn_lb_logits=v_hgrn_lb_logits, v_hgrn_w_in=v_hgrn_w_in, v_hgrn_onorm_g=v_hgrn_onorm_g, v_hgrn_w_out=v_hgrn_w_out, v_sba_w_qkv=v_sba_w_qkv, v_sba_w_out=v_sba_w_out, v_pool_w=v_pool_w, v_pool_scale=v_pool_scale, v_ffn_w_up=v_ffn_w_up, v_ffn_conv_w=v_ffn_conv_w, v_ffn_conv_b=v_ffn_conv_b, v_ffn_w_down=v_ffn_w_down)
    weights = {n: given[n] for n in TWIN_WEIGHTS}
    shared = {n: given[n] for n in SHARED_INPUTS}
    per_example = {n: given[n] for n in ['x']}
    grad_fn = _jax.value_and_grad(_loss, argnums=(0, 1))

    def one_microbatch(ex, loss_target):
        ex = dict(ex)
        diff = ex.pop(TWIN_DIFF_INPUT)
        return grad_fn(weights, diff, {**shared, **ex}, loss_target)

    if N_MICROBATCH == 1:
        loss, (grad_w, grad_x) = one_microbatch(per_example, given["loss_target"])
    else:
        def body(carry, xs):
            loss_sum, grad_sum = carry
            l_k, (gw_k, gx_k) = one_microbatch(xs[0], xs[1])
            with _jax.named_scope("update"):
                return (loss_sum + l_k, _jax.tree.map(_jnp.add, grad_sum, gw_k)), gx_k

        init = (_jnp.zeros((), _jnp.float32), _jax.tree.map(_jnp.zeros_like, weights))
        (loss, grad_w), grad_x = _jax.lax.scan(body, init, (per_example, given["loss_target"]))
    with _jax.named_scope("update"):
        delta_w, new_m, new_v = {}, {}, {}
        for n in TWIN_WEIGHTS:
            delta_w[n], new_m[n], new_v[n] = _adamw(weights[n], grad_w[n], given["m_" + n], given["v_" + n])
    return (loss, grad_x, *[grad_w[n] for n in TWIN_WEIGHTS], *[delta_w[n] for n in TWIN_WEIGHTS],
            *[new_m[n] for n in TWIN_WEIGHTS], *[new_v[n] for n in TWIN_WEIGHTS])
```

```python
import functools
import math

import jax
import jax.numpy as jnp
from jax import lax
from jax.experimental import pallas as pl
from jax.experimental.pallas import tpu as pltpu

F32 = jnp.float32
BF16 = jnp.bfloat16
HI = lax.Precision.HIGHEST
MESH = pl.DeviceIdType.MESH
ANY = pl.BlockSpec(memory_space=pl.ANY)
VMEM_FULL = pl.BlockSpec(memory_space=pltpu.VMEM)

NORM_EPS = 1e-6
HEAD = 128
HG_CHUNK = 128
HG_SUB = 16
SB_BLOCK = 128
POOL_WINDOWS = (2, 4, 8, 16)
CONV_WIDTH = 3
ROW_TILE = 256
N_DEV = 8
N_CHIP = 4

ADAM_LR = 0.001
ADAM_B1 = 0.9
ADAM_B2 = 0.999
ADAM_EPS = 1e-08
ADAM_WD = 0.01
ADAM_STEP = 10

VMEM_LIMIT = 48 * 1024 * 1024
LANES = 128
ELEMWISE_BLOCK_ELEMS = 256 * 1024


def _params(sem=None, vmem=VMEM_LIMIT):
    return pltpu.CompilerParams(dimension_semantics=sem, vmem_limit_bytes=vmem)


def _tile(n, prefs=(1024, 512, 256, 128)):
    for p in prefs:
        if n % p == 0:
            return p
    return n


def _dot(a, b, prec=None):
    return jnp.dot(a, b, precision=prec, preferred_element_type=F32)


def _dot_nt(a, b, prec=None):
    return lax.dot_general(a, b, (((1,), (1,)), ((), ())), precision=prec, preferred_element_type=F32)


def _dot_tn(a, b, prec=None):
    return lax.dot_general(a, b, (((0,), (0,)), ((), ())), precision=prec, preferred_element_type=F32)


def _sigmoid(x):
    return jax.nn.sigmoid(x)


def _softplus(x):
    return jnp.maximum(x, 0.0) + jnp.log1p(jnp.exp(-jnp.abs(x)))


class Layer:
    def __init__(self, arr, l):
        self.arr, self.l = arr, l
        self.shape = arr.shape[1:]
        self.part = None


class Split:
    def __init__(self, arr):
        self.arr = arr
        self.shape = (arr.shape[1], arr.shape[0] * arr.shape[2])
        self.part = arr.shape[2]


class Plain:
    def __init__(self, arr):
        self.arr = arr
        self.shape = arr.shape
        self.part = None


def _wrap(op):
    return op if isinstance(op, (Layer, Split, Plain)) else Plain(op)


def _op_spec(op, br, bc, rc_of_grid):
    if isinstance(op, Layer):
        l = op.l
        return pl.BlockSpec((None, br, bc), lambda i, j, k: (l, *rc_of_grid(i, j, k)))
    if isinstance(op, Split):
        per = op.part // bc

        def imap(i, j, k):
            r, c = rc_of_grid(i, j, k)
            return (lax.div(c, per), r, lax.rem(c, per))
        return pl.BlockSpec((None, br, bc), imap)
    return pl.BlockSpec((br, bc), rc_of_grid)


def matmul(a, b, mode, out_dtype, name, out_stack=None):
    a, b = _wrap(a), _wrap(b)
    if mode == "nn":
        (m, kd), (kd2, n) = a.shape, b.shape
    elif mode == "nt":
        (m, kd), (n, kd2) = a.shape, b.shape
    else:
        (kd, m), (kd2, n) = a.shape, b.shape
    assert kd == kd2, (mode, a.shape, b.shape)

    def dim_tile(full, *ops_on_cols):
        base = full
        for op in ops_on_cols:
            if op.part is not None:
                base = math.gcd(base, op.part)
        return _tile(base)

    tm = dim_tile(m, *([a] if mode == "tn" else []))
    tn = dim_tile(n, *([b] if mode in ("nn", "tn") else []))
    tk = dim_tile(kd, *(([a] if mode in ("nn", "nt") else []) + ([b] if mode == "nt" else [])))
    tk = min(tk, 512)
    nk = kd // tk

    if mode == "nn":
        a_spec = _op_spec(a, tm, tk, lambda i, j, k: (i, k))
        b_spec = _op_spec(b, tk, tn, lambda i, j, k: (k, j))
        dot = _dot
    elif mode == "nt":
        a_spec = _op_spec(a, tm, tk, lambda i, j, k: (i, k))
        b_spec = _op_spec(b, tn, tk, lambda i, j, k: (j, k))
        dot = _dot_nt
    else:
        a_spec = _op_spec(a, tk, tm, lambda i, j, k: (k, i))
        b_spec = _op_spec(b, tk, tn, lambda i, j, k: (k, j))
        dot = _dot_tn

    def body(*refs):
        a_ref, b_ref = refs[0], refs[1]
        o_ref, acc_ref = refs[-2], refs[-1]
        k = pl.program_id(2)

        @pl.when(k == 0)
        def _():
            acc_ref[...] = jnp.zeros_like(acc_ref)

        acc_ref[...] += dot(a_ref[...].astype(BF16), b_ref[...].astype(BF16))

        @pl.when(k == nk - 1)
        def _():
            o_ref[...] = acc_ref[...].astype(o_ref.dtype)

    in_specs = [a_spec, b_spec]
    args = [a.arr, b.arr]
    aliases = {}
    if out_stack is None:
        out_shape = jax.ShapeDtypeStruct((m, n), out_dtype)
        out_spec = pl.BlockSpec((tm, tn), lambda i, j, k: (i, j))
    else:
        buf, l = out_stack
        assert buf.shape[1:] == (m, n) and buf.dtype == out_dtype
        out_shape = jax.ShapeDtypeStruct(buf.shape, buf.dtype)
        out_spec = pl.BlockSpec((None, tm, tn), lambda i, j, k: (l, i, j))
        in_specs.append(ANY)
        args.append(buf)
        aliases = {2: 0}
    return pl.pallas_call(
        body, out_shape=out_shape, grid=(m // tm, n // tn, nk),
        in_specs=in_specs, out_specs=out_spec,
        scratch_shapes=[pltpu.VMEM((tm, tn), F32)],
        input_output_aliases=aliases, name=name,
        compiler_params=_params(("parallel", "parallel", "arbitrary")),
    )(*args)


def _rms(x, g):
    r = lax.rsqrt(jnp.mean(x * x, axis=-1, keepdims=True) + NORM_EPS)
    return x * r * g


def res_norm(h, m, g_a, g_b, name, want_f32=False):
    t, d = h.shape
    tr = min(ROW_TILE, t)
    has_m, has_u = m is not None, g_b is not None
    row = pl.BlockSpec((tr, d), lambda i: (i, 0))
    vec = pl.BlockSpec((1, d), lambda i: (0, 0))

    def body(*refs):
        refs = list(refs)
        h_ref = refs.pop(0)
        hn = h_ref[...]
        if has_m:
            m_ref, ga_ref = refs.pop(0), refs.pop(0)
            hn = hn + _rms(m_ref[...], ga_ref[...])
        if has_u:
            gb_ref = refs.pop(0)
        if has_m:
            refs.pop(0)[...] = hn
        if has_u:
            u = _rms(hn, gb_ref[...])
            refs.pop(0)[...] = u.astype(BF16)
            if want_f32:
                refs.pop(0)[...] = u

    args, in_specs, out_shape, out_specs = [h], [row], [], []
    if has_m:
        args += [m, g_a]
        in_specs += [row, vec]
        out_shape.append(jax.ShapeDtypeStruct((t, d), F32))
        out_specs.append(row)
    if has_u:
        args.append(g_b)
        in_specs.append(vec)
        out_shape.append(jax.ShapeDtypeStruct((t, d), BF16))
        out_specs.append(row)
        if want_f32:
            out_shape.append(jax.ShapeDtypeStruct((t, d), F32))
            out_specs.append(row)
    outs = list(pl.pallas_call(
        body, out_shape=out_shape, grid=(t // tr,), in_specs=in_specs, out_specs=out_specs,
        name=name, compiler_params=_params(("parallel",)),
    )(*args))
    h_new = outs.pop(0) if has_m else None
    u16 = outs.pop(0) if has_u else None
    u32 = outs.pop(0) if (has_u and want_f32) else None
    return h_new, u16, u32


def norm_bwd(x, g, dy, add, out_dtype, name):
    t, d = x.shape
    tr = min(ROW_TILE, t)
    has_add = add is not None
    row = pl.BlockSpec((tr, d), lambda i: (i, 0))
    vec = pl.BlockSpec((1, d), lambda i: (0, 0))

    def body(*refs):
        if has_add:
            x_ref, g_ref, dy_ref, add_ref, dx_ref, dg_ref = refs
        else:
            x_ref, g_ref, dy_ref, dx_ref, dg_ref = refs
        xv = x_ref[...]
        dyv = dy_ref[...].astype(F32)
        r = lax.rsqrt(jnp.mean(xv * xv, axis=-1, keepdims=True) + NORM_EPS)
        gy = dyv * g_ref[...]
        dx = r * gy - xv * (r * r * r * jnp.mean(gy * xv, axis=-1, keepdims=True))
        if has_add:
            dx = dx + add_ref[...]
        dx_ref[...] = dx.astype(dx_ref.dtype)

        @pl.when(pl.program_id(0) == 0)
        def _():
            dg_ref[...] = jnp.zeros_like(dg_ref)

        dg_ref[...] += jnp.sum(dyv * xv * r, axis=0, keepdims=True)

    args = [x, g, dy] + ([add] if has_add else [])
    in_specs = [row, vec, row] + ([row] if has_add else [])
    return pl.pallas_call(
        body, out_shape=[jax.ShapeDtypeStruct((t, d), out_dtype), jax.ShapeDtypeStruct((1, d), F32)],
        grid=(t // tr,), in_specs=in_specs, out_specs=[row, vec],
        name=name, compiler_params=_params(("arbitrary",)),
    )(*args)


def loss_head(y, target, name):
    t, d = y.shape
    tr = min(ROW_TILE, t)
    row = pl.BlockSpec((tr, d), lambda i: (i, 0))
    acc = pl.BlockSpec((8, LANES), lambda i: (0, 0))

    def body(y_ref, t_ref, loss_ref, dy_ref):
        e = y_ref[...] - t_ref[...]
        dy_ref[...] = e * (1.0 / d)

        @pl.when(pl.program_id(0) == 0)
        def _():
            loss_ref[...] = jnp.zeros_like(loss_ref)

        loss_ref[...] += jnp.sum(e * e) * (0.5 / d)

    return pl.pallas_call(
        body, out_shape=[jax.ShapeDtypeStruct((8, LANES), F32), jax.ShapeDtypeStruct((t, d), F32)],
        grid=(t // tr,), in_specs=[row, row], out_specs=[acc, row],
        name=name, compiler_params=_params(("arbitrary",)),
    )(y, target)


def _depth_softmax(ref, depth):
    rows = [ref[i:i + 1, :] for i in range(depth)]
    mx = functools.reduce(jnp.maximum, rows)
    ex = [jnp.exp(r - mx) for r in rows]
    tot = functools.reduce(lambda p, q: p + q, ex)
    return [e / tot for e in ex]


def lower_bounds_fwd(logits):
    depth, kw = logits.shape

    def body(l_ref, o_ref):
        s = _depth_softmax(l_ref, depth)
        run = jnp.zeros_like(s[0])
        o_ref[0:1, :] = run
        for i in range(1, depth):
            run = run + s[i]
            o_ref[i:i + 1, :] = run

    return pl.pallas_call(body, out_shape=jax.ShapeDtypeStruct((depth, kw), F32), name="lb_fwd")(logits)


def lower_bounds_bwd(logits, dlb):
    depth, kw = logits.shape

    def body(l_ref, d_ref, o_ref):
        s = _depth_softmax(l_ref, depth)
        ds = [jnp.zeros_like(s[0]) for _ in range(depth)]
        run = jnp.zeros_like(s[0])
        for j in range(depth - 1, 0, -1):
            run = run + d_ref[j:j + 1, :]
            ds[j] = run
        dot = functools.reduce(lambda p, q: p + q, [s[j] * ds[j] for j in range(depth)])
        for j in range(depth):
            o_ref[j:j + 1, :] = s[j] * (ds[j] - dot)

    return pl.pallas_call(body, out_shape=jax.ShapeDtypeStruct((depth, kw), F32), name="lb_bwd")(logits, dlb)


def _hg_gates(qp, fp, lb_row, has_lb):
    sig = _sigmoid(fp)
    nsig = _sigmoid(-fp)
    ls = jnp.minimum(fp, 0.0) - jnp.log1p(jnp.exp(-jnp.abs(fp)))
    if has_lb:
        a = jnp.log(lb_row)
        bb = jnp.log1p(-lb_row) + ls
        g = jnp.maximum(a, bb) + jnp.log1p(jnp.exp(-jnp.abs(a - bb)))
        w = jnp.exp(bb - g)
        k = (1.0 - lb_row) * nsig
    else:
        g, w, k = ls, None, nsig
    q = qp * _sigmoid(qp)
    return q, k, g, sig, nsig, w


def _hg_masks():
    c = HG_CHUNK
    row = lax.broadcasted_iota(jnp.int32, (c, c), 0)
    col = lax.broadcasted_iota(jnp.int32, (c, c), 1)
    lower = (col <= row).astype(F32)
    upper = (col >= row).astype(F32)
    krow = lax.broadcasted_iota(jnp.int32, (c, HEAD), 0)
    arow = lax.broadcasted_iota(jnp.int32, (HG_SUB, c), 0)
    acol = lax.broadcasted_iota(jnp.int32, (HG_SUB, c), 1)
    return lower, upper, krow, arow, acol


def _hg_sub(i, q, k, b, b_ref, krow, arow, acol):
    r0 = i * HG_SUB
    m = b_ref[r0:r0 + 1, :]
    ebq = jnp.exp(b[r0:r0 + HG_SUB, :] - m)
    qh = q[r0:r0 + HG_SUB, :] * ebq
    ek = jnp.exp(jnp.where(krow < r0 + HG_SUB, m - b, 0.0))
    kh = k * ek
    mask = acol <= arow + r0
    amat = jnp.where(mask, _dot_nt(qh, kh, HI), 0.0)
    return ebq, qh, ek, kh, mask, amat


def hgrn_fwd(proj, lb, onorm_g, has_lb, name):
    t, w4 = proj.shape
    kw = w4 // 4
    nh, nc, c = kw // HEAD, t // HG_CHUNK, HG_CHUNK

    def body(qp_ref, fp_ref, iv_ref, gp_ref, lb_ref, on_ref, y_ref, st_ref, b_ref):
        lower, _, krow, arow, acol = _hg_masks()
        lb_row, gam = lb_ref[...], on_ref[...]
        st_ref[...] = jnp.zeros_like(st_ref)

        def chunk(ci, carry):
            rs = pl.ds(pl.multiple_of(ci * c, c), c)
            v = iv_ref[rs, :]
            gp = gp_ref[rs, :]
            q, k, g, _, _, _ = _hg_gates(qp_ref[rs, :], fp_ref[rs, :], lb_row, has_lb)
            b = _dot(lower, g, HI)
            b_ref[...] = b
            st = st_ref[...]
            o = _dot_nt(q * jnp.exp(b), st, HI)
            parts = []
            for i in range(c // HG_SUB):
                amat = _hg_sub(i, q, k, b, b_ref, krow, arow, acol)[-1]
                parts.append(_dot(amat, v, HI))
            o = o + jnp.concatenate(parts, axis=0)
            bl = b_ref[c - 1:c, :]
            st_ref[...] = jnp.exp(bl) * st + _dot_tn(v, k * jnp.exp(bl - b), HI)
            r = lax.rsqrt(jnp.mean(o * o, axis=-1, keepdims=True) + NORM_EPS)
            y_ref[rs, :] = (o * r * gam * (gp * _sigmoid(gp))).astype(BF16)
            return carry

        lax.fori_loop(0, nc, chunk, 0)

    col = lambda p: pl.BlockSpec((t, HEAD), lambda h: (0, p * nh + h))
    vec = pl.BlockSpec((1, HEAD), lambda h: (0, h))
    return pl.pallas_call(
        body, out_shape=jax.ShapeDtypeStruct((t, kw), BF16), grid=(nh,),
        in_specs=[col(0), col(1), col(2), col(3), vec, vec],
        out_specs=pl.BlockSpec((t, HEAD), lambda h: (0, h)),
        scratch_shapes=[pltpu.VMEM((HEAD, HEAD), F32), pltpu.VMEM((c, HEAD), F32)],
        name=name, compiler_params=_params(("parallel",)),
    )(proj, proj, proj, proj, lb, onorm_g)


def hgrn_bwd(proj, lb, onorm_g, dy, has_lb, name):
    t, w4 = proj.shape
    kw = w4 // 4
    nh, nc, c = kw // HEAD, t // HG_CHUNK, HG_CHUNK

    def body(qp_ref, fp_ref, iv_ref, gp_ref, lb_ref, on_ref, dy_ref,
             dp_ref, dgam_ref, dlb_ref, sst_ref, o_ref, b_ref, dst_ref, car_ref):
        lower, upper, krow, arow, acol = _hg_masks()
        lb_row, gam = lb_ref[...], on_ref[...]

        def recompute(ci):
            rs = pl.ds(pl.multiple_of(ci * c, c), c)
            gates = _hg_gates(qp_ref[rs, :], fp_ref[rs, :], lb_row, has_lb)
            b = _dot(lower, gates[2], HI)
            b_ref[...] = b
            return rs, gates, b

        def fwd_chunk(ci, carry):
            rs, (q, k, g, _, _, _), b = recompute(ci)
            v = iv_ref[rs, :]
            st = carry
            sst_ref[ci] = st
            o = _dot_nt(q * jnp.exp(b), st, HI)
            parts = []
            for i in range(c // HG_SUB):
                amat = _hg_sub(i, q, k, b, b_ref, krow, arow, acol)[-1]
                parts.append(_dot(amat, v, HI))
            o_ref[rs, :] = o + jnp.concatenate(parts, axis=0)
            bl = b_ref[c - 1:c, :]
            return jnp.exp(bl) * st + _dot_tn(v, k * jnp.exp(bl - b), HI)

        lax.fori_loop(0, nc, fwd_chunk, jnp.zeros((HEAD, HEAD), F32))

        dst_ref[...] = jnp.zeros_like(dst_ref)
        car_ref[...] = jnp.zeros_like(car_ref)
        dgam_ref[...] = jnp.zeros_like(dgam_ref)
        dlb_ref[...] = jnp.zeros_like(dlb_ref)

        def bwd_chunk(step, carry):
            ci = nc - 1 - step
            rs, (q, k, g, sig, nsig, w), b = recompute(ci)
            qp, gp, v = qp_ref[rs, :], gp_ref[rs, :], iv_ref[rs, :]
            dyv, o = dy_ref[rs, :], o_ref[rs, :]
            st = sst_ref[ci]
            dst_new = dst_ref[...]
            r = lax.rsqrt(jnp.mean(o * o, axis=-1, keepdims=True) + NORM_EPS)
            on = o * r
            sgm = _sigmoid(gp)
            sg = gp * sgm
            dgam_ref[...] += jnp.sum(dyv * sg * on, axis=0, keepdims=True)
            dgp = dyv * on * gam * (sgm * (1.0 + gp * (1.0 - sgm)))
            dn = dyv * gam * sg
            do = r * dn - o * (r * r * r * jnp.mean(dn * o, axis=-1, keepdims=True))
            eb = jnp.exp(b)
            bl = b_ref[c - 1:c, :]
            ekd = jnp.exp(bl - b)
            qe, kd = q * eb, k * ekd
            dq = _dot(do, st, HI) * eb
            dv = _dot_nt(kd, dst_new, HI)
            dk = _dot(v, dst_new, HI) * ekd
            dst_ref[...] = jnp.exp(bl) * dst_new + _dot_tn(do, qe, HI)
            dq_parts = []
            for i in range(c // HG_SUB):
                r0 = i * HG_SUB
                ebq, qh, ek, kh, mask, amat = _hg_sub(i, q, k, b, b_ref, krow, arow, acol)
                do_i = do[r0:r0 + HG_SUB, :]
                da = jnp.where(mask, _dot_nt(do_i, v, HI), 0.0)
                dq_parts.append(_dot(da, kh, HI) * ebq)
                dk = dk + _dot_tn(da, qh, HI) * ek
                dv = dv + _dot_tn(amat, do_i, HI)
            dq = dq + jnp.concatenate(dq_parts, axis=0)
            db = q * dq - k * dk
            dg = car_ref[...] + _dot(upper, db, HI)
            car_ref[...] += jnp.sum(db, axis=0, keepdims=True)
            if has_lb:
                dfp = dg * nsig * w - dk * ((1.0 - lb_row) * sig * nsig)
                dlb_ref[...] += jnp.sum(dg * nsig * jnp.exp(-g) - dk * nsig, axis=0, keepdims=True)
            else:
                dfp = dg * nsig - dk * (sig * nsig)
            sq = _sigmoid(qp)
            dp_ref[0, rs, :] = (dq * (sq * (1.0 + qp * (1.0 - sq)))).astype(BF16)
            dp_ref[1, rs, :] = dfp.astype(BF16)
            dp_ref[2, rs, :] = dv.astype(BF16)
            dp_ref[3, rs, :] = dgp.astype(BF16)
            return carry

        lax.fori_loop(0, nc, bwd_chunk, 0)

    col = lambda p: pl.BlockSpec((t, HEAD), lambda h: (0, p * nh + h))
    vec = pl.BlockSpec((1, HEAD), lambda h: (0, h))
    return pl.pallas_call(
        body,
        out_shape=[jax.ShapeDtypeStruct((4, t, kw), BF16), jax.ShapeDtypeStruct((1, kw), F32),
                   jax.ShapeDtypeStruct((1, kw), F32)],
        grid=(nh,),
        in_specs=[col(0), col(1), col(2), col(3), vec, vec, pl.BlockSpec((t, HEAD), lambda h: (0, h))],
        out_specs=[pl.BlockSpec((4, t, HEAD), lambda h: (0, 0, h)), vec, vec],
        scratch_shapes=[pltpu.VMEM((nc, HEAD, HEAD), F32), pltpu.VMEM((t, HEAD), F32),
                        pltpu.VMEM((c, HEAD), F32), pltpu.VMEM((HEAD, HEAD), F32), pltpu.VMEM((1, HEAD), F32)],
        name=name, compiler_params=_params(("parallel",)),
    )(proj, proj, proj, proj, lb, onorm_g, dy)


def _sb_masks():
    n = SB_BLOCK
    row = lax.broadcasted_iota(jnp.int32, (n, n), 0)
    col = lax.broadcasted_iota(jnp.int32, (n, n), 1)
    after = (row > col).astype(F32)
    from_ = (row >= col).astype(F32)
    return row, col, after, from_


def sba_fwd(qkv, name):
    t, w3 = qkv.shape
    wd = w3 // 3
    nh, nq, n = wd // HEAD, t // SB_BLOCK, SB_BLOCK
    scale = HEAD ** -0.5

    def body(q_ref, k_ref, v_ref, o_ref):
        row, col, after, _ = _sb_masks()

        def qblock(qi, carry):
            qs = pl.ds(pl.multiple_of(qi * n, n), n)
            q = q_ref[qs, :] * scale

            def kblock(step, state):
                acc, rem0 = state
                kj = qi - step
                ks = pl.ds(pl.multiple_of(kj * n, n), n)
                z = _dot_nt(q, k_ref[ks, :], HI)
                strict = (col + kj * n) < (row + qi * n)
                sp = jnp.where(strict, _softplus(z), 0.0)
                rem = rem0 + _dot(sp, after, HI)
                a = jnp.where(strict, jnp.exp(z - _softplus(z) - rem), 0.0)
                acc = acc + _dot(a, v_ref[ks, :], HI)
                return acc, rem0 + jnp.sum(sp, axis=1, keepdims=True)

            acc, _ = lax.fori_loop(0, qi + 1, kblock,
                                   (jnp.zeros((n, HEAD), F32), jnp.zeros((n, 1), F32)))
            o_ref[qs, :] = acc
            return carry

        lax.fori_loop(0, nq, qblock, 0)

    col_spec = lambda p: pl.BlockSpec((t, HEAD), lambda h: (0, p * nh + h))
    return pl.pallas_call(
        body, out_shape=jax.ShapeDtypeStruct((t, wd), F32), grid=(nh,),
        in_specs=[col_spec(0), col_spec(1), col_spec(2)],
        out_specs=pl.BlockSpec((t, HEAD), lambda h: (0, h)),
        name=name, compiler_params=_params(("parallel",)),
    )(qkv, qkv, qkv)


def sba_bwd(qkv, o, do, name):
    t, w3 = qkv.shape
    wd = w3 // 3
    nh, nq, n = wd // HEAD, t // SB_BLOCK, SB_BLOCK
    scale = HEAD ** -0.5

    def body(q_ref, k_ref, v_ref, o_ref, do_ref, d_ref, dk_ref, dv_ref):
        row, col, after, from_ = _sb_masks()
        dk_ref[...] = jnp.zeros_like(dk_ref)
        dv_ref[...] = jnp.zeros_like(dv_ref)

        def qblock(qi, carry):
            qs = pl.ds(pl.multiple_of(qi * n, n), n)
            q = q_ref[qs, :] * scale
            dov = do_ref[qs, :]
            dsum = jnp.sum(dov * o_ref[qs, :], axis=1, keepdims=True)

            def kblock(step, state):
                dq, rem0, e0 = state
                kj = qi - step
                ks = pl.ds(pl.multiple_of(kj * n, n), n)
                kv, vv = k_ref[ks, :], v_ref[ks, :]
                z = _dot_nt(q, kv, HI)
                strict = (col + kj * n) < (row + qi * n)
                spz = _softplus(z)
                sp = jnp.where(strict, spz, 0.0)
                rem = rem0 + _dot(sp, after, HI)
                a = jnp.where(strict, jnp.exp(z - spz - rem), 0.0)
                e = a * _dot_nt(dov, vv, HI)
                left = dsum - (e0 + _dot(e, from_, HI))
                sgz = _sigmoid(z)
                dz = jnp.where(strict, e * (1.0 - sgz) - sgz * left, 0.0)
                dq = dq + _dot(dz, kv, HI)
                dk_ref[ks, :] += _dot_tn(dz, q, HI)
                dv_ref[ks, :] += _dot_tn(a, dov, HI)
                return (dq, rem0 + jnp.sum(sp, axis=1, keepdims=True),
                        e0 + jnp.sum(e, axis=1, keepdims=True))

            zero1 = jnp.zeros((n, 1), F32)
            dq, _, _ = lax.fori_loop(0, qi + 1, kblock, (jnp.zeros((n, HEAD), F32), zero1, zero1))
            d_ref[0, qs, :] = (dq * scale).astype(BF16)
            return carry

        lax.fori_loop(0, nq, qblock, 0)
        d_ref[1, :, :] = dk_ref[...].astype(BF16)
        d_ref[2, :, :] = dv_ref[...].astype(BF16)

    col_spec = lambda p: pl.BlockSpec((t, HEAD), lambda h: (0, p * nh + h))
    head = pl.BlockSpec((t, HEAD), lambda h: (0, h))
    return pl.pallas_call(
        body, out_shape=jax.ShapeDtypeStruct((3, t, wd), BF16), grid=(nh,),
        in_specs=[col_spec(0), col_spec(1), col_spec(2), head, head],
        out_specs=pl.BlockSpec((3, t, HEAD), lambda h: (0, 0, h)),
        scratch_shapes=[pltpu.VMEM((t, HEAD), F32), pltpu.VMEM((t, HEAD), F32)],
        name=name, compiler_params=_params(("parallel",)),
    )(qkv, qkv, qkv, o, do)


def _pool_band(i_out, i_in, tr, win, transpose):
    r = lax.broadcasted_iota(jnp.int32, (tr, tr), 0) + i_out * tr
    c = lax.broadcasted_iota(jnp.int32, (tr, tr), 1) + i_in * tr
    if transpose:
        return ((r <= c) & (r > c - win)).astype(F32)
    return ((c <= r) & (c > r - win)).astype(F32)


def _pool_p(u_ref, i, tr, win):
    cur = u_ref[i * tr:(i + 1) * tr, :]
    ws = _dot(_pool_band(i, i, tr, win, False), cur, HI)
    if i > 0:
        ws = ws + _dot(_pool_band(i, i - 1, tr, win, False), u_ref[(i - 1) * tr:i * tr, :], HI)
    pos = lax.broadcasted_iota(jnp.int32, (tr, 1), 0) + (i * tr + 1)
    count = jnp.minimum(pos, win).astype(F32)
    return ws / count - cur, count


def pool_fwd(u, pool_w, pool_scale, name):
    t, d = u.shape
    ng = len(POOL_WINDOWS)
    gs = d // ng
    tr = min(ROW_TILE, t)

    def body(u_ref, w_ref, s_ref, y_ref):
        win = jnp.left_shift(2, pl.program_id(0))
        for i in range(t // tr):
            p, _ = _pool_p(u_ref, i, tr, win)
            y_ref[i * tr:(i + 1) * tr, :] = _dot(p.astype(BF16), w_ref[...]) * s_ref[...]

    grp = pl.BlockSpec((t, gs), lambda g: (0, g))
    return pl.pallas_call(
        body, out_shape=jax.ShapeDtypeStruct((t, d), F32), grid=(ng,),
        in_specs=[grp, pl.BlockSpec((None, gs, gs), lambda g: (g, 0, 0)), pl.BlockSpec((1, gs), lambda g: (0, g))],
        out_specs=grp, name=name, compiler_params=_params(("parallel",)),
    )(u, pool_w, pool_scale)


def pool_bwd(u, pool_w, pool_scale, dy, dw_buf, name):
    t, d = u.shape
    ng = len(POOL_WINDOWS)
    gs = d // ng
    tr = min(ROW_TILE, t)
    nt = t // tr

    def body(u_ref, w_ref, s_ref, dy_ref, buf_ref, du_ref, dw_ref, ds_ref, dpc_ref, dp_ref):
        del buf_ref
        win = jnp.left_shift(2, pl.program_id(0))
        wv = w_ref[...]
        dw = jnp.zeros((gs, gs), F32)
        dsc = jnp.zeros((1, gs), F32)
        for i in range(nt):
            rows = slice(i * tr, (i + 1) * tr)
            p, count = _pool_p(u_ref, i, tr, win)
            p16 = p.astype(BF16)
            dyv = dy_ref[rows, :]
            dsc = dsc + jnp.sum(dyv * _dot(p16, wv), axis=0, keepdims=True)
            dyp = (dyv * s_ref[...]).astype(BF16)
            dw = dw + _dot_tn(p16, dyp)
            dp = _dot_nt(dyp, wv)
            dp_ref[rows, :] = dp
            dpc_ref[rows, :] = dp / count
        dw_ref[...] = dw.astype(BF16)
        ds_ref[...] = dsc
        for i in range(nt):
            rows = slice(i * tr, (i + 1) * tr)
            acc = _dot(_pool_band(i, i, tr, win, True), dpc_ref[rows, :], HI)
            if i + 1 < nt:
                acc = acc + _dot(_pool_band(i, i + 1, tr, win, True), dpc_ref[(i + 1) * tr:(i + 2) * tr, :], HI)
            du_ref[rows, :] = acc - dp_ref[rows, :]

    grp = pl.BlockSpec((t, gs), lambda g: (0, g))
    wspec = pl.BlockSpec((None, gs, gs), lambda g: (g, 0, 0))
    vec = pl.BlockSpec((1, gs), lambda g: (0, g))
    return pl.pallas_call(
        body,
        out_shape=[jax.ShapeDtypeStruct((t, d), F32), jax.ShapeDtypeStruct(dw_buf.shape, BF16),
                   jax.ShapeDtypeStruct((1, d), F32)],
        grid=(ng,), in_specs=[grp, wspec, vec, grp, ANY], out_specs=[grp, wspec, vec],
        scratch_shapes=[pltpu.VMEM((t, gs), F32), pltpu.VMEM((t, gs), F32)],
        input_output_aliases={4: 1}, name=name, compiler_params=_params(("parallel",)),
    )(u, pool_w, pool_scale, dy, dw_buf)


CONV_COLS = 256
HALO = 8


def _conv_taps(ref, r0, tr):
    x = ref[r0:r0 + tr, :]
    prev = ref[r0 - HALO:r0, :] if r0 > 0 else jnp.zeros((HALO, x.shape[1]), F32)
    xx = jnp.concatenate([prev, x], axis=0)
    return x, pltpu.roll(xx, 1, 0)[HALO:, :], pltpu.roll(xx, 2, 0)[HALO:, :]


def _conv_out(taps, w_ref, b_ref):
    x, s1, s2 = taps
    return w_ref[0:1, :] * s2 + w_ref[1:2, :] * s1 + w_ref[2:3, :] * x + b_ref[...]


def conv_glu_fwd(up, conv_w, conv_b, name):
    t, f2 = up.shape
    f = f2 // 2
    tc = min(CONV_COLS, f)
    nj = f // tc
    tr = min(ROW_TILE, t)

    def body(ug_ref, uv_ref, wg_ref, wv_ref, bg_ref, bv_ref, o_ref):
        for i in range(t // tr):
            r0 = i * tr
            gate = _conv_out(_conv_taps(ug_ref, r0, tr), wg_ref, bg_ref)
            val = _conv_out(_conv_taps(uv_ref, r0, tr), wv_ref, bv_ref)
            o_ref[r0:r0 + tr, :] = (gate * _sigmoid(gate) * val).astype(BF16)

    blk = lambda rows, half: pl.BlockSpec((rows, tc), lambda j: (0, half * nj + j))
    return pl.pallas_call(
        body, out_shape=jax.ShapeDtypeStruct((t, f), BF16), grid=(nj,),
        in_specs=[blk(t, 0), blk(t, 1), blk(CONV_WIDTH, 0), blk(CONV_WIDTH, 1), blk(1, 0), blk(1, 1)],
        out_specs=pl.BlockSpec((t, tc), lambda j: (0, j)),
        name=name, compiler_params=_params(("parallel",)),
    )(up, up, conv_w, conv_w, conv_b, conv_b)


def conv_glu_bwd(up, conv_w, conv_b, dact, name):
    t, f2 = up.shape
    f = f2 // 2
    tc = min(CONV_COLS, f)
    nj = f // tc
    tr = min(ROW_TILE, t)
    nt = t // tr

    def body(ug_ref, uv_ref, wg_ref, wv_ref, bg_ref, bv_ref, da_ref, du_ref, dw_ref, db_ref, dg_ref, dv_ref):
        dwg = [jnp.zeros((1, tc), F32) for _ in range(CONV_WIDTH)]
        dwv = [jnp.zeros((1, tc), F32) for _ in range(CONV_WIDTH)]
        dbg = jnp.zeros((1, tc), F32)
        dbv = jnp.zeros((1, tc), F32)
        for i in range(nt):
            r0 = i * tr
            tg, tv = _conv_taps(ug_ref, r0, tr), _conv_taps(uv_ref, r0, tr)
            gate, val = _conv_out(tg, wg_ref, bg_ref), _conv_out(tv, wv_ref, bv_ref)
            sg = _sigmoid(gate)
            da = da_ref[r0:r0 + tr, :]
            d_gate = da * val * (sg * (1.0 + gate * (1.0 - sg)))
            d_val = da * (gate * sg)
            dg_ref[r0:r0 + tr, :] = d_gate
            dv_ref[r0:r0 + tr, :] = d_val
            dbg = dbg + jnp.sum(d_gate, axis=0, keepdims=True)
            dbv = dbv + jnp.sum(d_val, axis=0, keepdims=True)
            for tap in range(CONV_WIDTH):
                dwg[tap] = dwg[tap] + jnp.sum(d_gate * tg[2 - tap], axis=0, keepdims=True)
                dwv[tap] = dwv[tap] + jnp.sum(d_val * tv[2 - tap], axis=0, keepdims=True)
        for tap in range(CONV_WIDTH):
            dw_ref[0, tap:tap + 1, :] = dwg[tap]
            dw_ref[1, tap:tap + 1, :] = dwv[tap]
        db_ref[0, :, :] = dbg
        db_ref[1, :, :] = dbv
        for half, (d_ref, w_ref) in enumerate(((dg_ref, wg_ref), (dv_ref, wv_ref))):
            for i in range(nt):
                r0 = i * tr
                x = d_ref[r0:r0 + tr, :]
                nxt = d_ref[r0 + tr:r0 + tr + HALO, :] if i + 1 < nt else jnp.zeros((HALO, tc), F32)
                xx = jnp.concatenate([x, nxt], axis=0)
                up1 = pltpu.roll(xx, tr + HALO - 1, 0)[:tr, :]
                up2 = pltpu.roll(xx, tr + HALO - 2, 0)[:tr, :]
                du = w_ref[2:3, :] * x + w_ref[1:2, :] * up1 + w_ref[0:1, :] * up2
                du_ref[half, r0:r0 + tr, :] = du.astype(BF16)

    blk = lambda rows, half: pl.BlockSpec((rows, tc), lambda j: (0, half * nj + j))
    return pl.pallas_call(
        body,
        out_shape=[jax.ShapeDtypeStruct((2, t, f), BF16), jax.ShapeDtypeStruct((2, CONV_WIDTH, f), F32),
                   jax.ShapeDtypeStruct((2, 1, f), F32)],
        grid=(nj,),
        in_specs=[blk(t, 0), blk(t, 1), blk(CONV_WIDTH, 0), blk(CONV_WIDTH, 1), blk(1, 0), blk(1, 1),
                  pl.BlockSpec((t, tc), lambda j: (0, j))],
        out_specs=[pl.BlockSpec((2, t, tc), lambda j: (0, 0, j)),
                   pl.BlockSpec((2, CONV_WIDTH, tc), lambda j: (0, 0, j)),
                   pl.BlockSpec((2, 1, tc), lambda j: (0, 0, j))],
        scratch_shapes=[pltpu.VMEM((t, tc), F32), pltpu.VMEM((t, tc), F32)],
        name=name, compiler_params=_params(("parallel",)),
    )(up, up, conv_w, conv_w, conv_b, conv_b, dact)


def _place():
    x, y, c = lax.axis_index("x"), lax.axis_index("y"), lax.axis_index("c")
    others = [(1 - x, y), (x, 1 - y), (1 - x, 1 - y)]
    return x, y, c, others


def _window(ref, axis, b, n):
    if axis == 1:
        return ref.at[:, pl.ds(b * n, n), :]
    return ref.at[:, :, pl.ds(b * n, n)]


def _remote(src, dst, send_sems, recv_sems, k, to):
    return pltpu.make_async_remote_copy(src_ref=src, dst_ref=dst, send_sem=send_sems.at[k],
                                        recv_sem=recv_sems.at[k], device_id=to, device_id_type=MESH)


def gather_weights(shards, axes):
    na = len(shards)
    widths = [s.shape[ax] for s, ax in zip(shards, axes)]
    out_shape = []
    for s, ax in zip(shards, axes):
        shp = list(s.shape)
        shp[ax] *= N_DEV
        out_shape.append(jax.ShapeDtypeStruct(tuple(shp), s.dtype))

    def body(*refs):
        ins, outs = refs[:na], refs[na:2 * na]
        send_sems, recv_sems, local_sems = refs[2 * na:]
        x, y, c, others = _place()
        sibling = (x, y, 1 - c)

        def win(a, px, py, pc):
            return _window(outs[a], axes[a], 4 * px + 2 * py + pc, widths[a])

        def copy(a, k, block, to, src=None):
            w = win(a, *block)
            return _remote(w if src is None else src, w, send_sems, recv_sems, 7 * a + k, to)

        started = []
        for a in range(na):
            mine = pltpu.make_async_copy(ins[a], win(a, x, y, c), local_sems.at[a])
            mine.start()
            started.append(mine)
        sends = []
        for a in range(na):
            first = [copy(a, 0, (x, y, c), sibling, src=ins[a])]
            first += [copy(a, 1 + j, (x, y, c), (*chip, c), src=ins[a]) for j, chip in enumerate(others)]
            for cp in first:
                cp.start()
            sends += first
        for a in range(na):
            for j, chip in enumerate(others):
                copy(a, 1 + j, (*chip, c), (x, y, c)).wait_recv()
                fwd = copy(a, 4 + j, (*chip, c), sibling)
                fwd.start()
                sends.append(fwd)
        for a in range(na):
            copy(a, 0, (x, y, 1 - c), (x, y, c)).wait_recv()
            for j, chip in enumerate(others):
                copy(a, 4 + j, (*chip, 1 - c), (x, y, c)).wait_recv()
        for cp in sends:
            cp.wait_send()
        for cp in started:
            cp.wait()

    return pl.pallas_call(
        body, out_shape=out_shape, in_specs=[ANY] * na, out_specs=[ANY] * na,
        scratch_shapes=[pltpu.SemaphoreType.DMA((7 * na,)), pltpu.SemaphoreType.DMA((7 * na,)),
                        pltpu.SemaphoreType.DMA((na,))],
        name="gather_weights",
    )(*shards)


def sibling_exchange(grads, axes):
    na = len(grads)
    widths = [g.shape[ax] // N_DEV for g, ax in zip(grads, axes)]
    out_shape = []
    for g, ax, n in zip(grads, axes, widths):
        shp = list(g.shape)
        shp[ax] = n
        out_shape.append(jax.ShapeDtypeStruct((N_CHIP, *shp), g.dtype))

    def body(*refs):
        ins, outs = refs[:na], refs[na:2 * na]
        send_sems, recv_sems = refs[2 * na:]
        x, y, c, _ = _place()
        copies = []
        for a in range(na):
            for q in range(N_CHIP):
                src = _window(ins[a], axes[a], 2 * q + (1 - c), widths[a])
                cp = _remote(src, outs[a].at[q], send_sems, recv_sems, N_CHIP * a + q, (x, y, 1 - c))
                cp.start()
                copies.append(cp)
        for cp in copies:
            cp.wait_recv()
        for cp in copies:
            cp.wait_send()

    return pl.pallas_call(
        body, out_shape=out_shape, in_specs=[ANY] * na, out_specs=[ANY] * na,
        scratch_shapes=[pltpu.SemaphoreType.DMA((N_CHIP * na,)), pltpu.SemaphoreType.DMA((N_CHIP * na,))],
        name="sibling_exchange",
    )(*grads)


def chip_exchange(partials):
    na = len(partials)

    def body(*refs):
        ins, outs = refs[:na], refs[na:2 * na]
        send_sems, recv_sems, local_sems = refs[2 * na:]
        x, y, c, others = _place()
        me = 2 * x + y
        local, sends = [], []
        for a in range(na):
            cp = pltpu.make_async_copy(ins[a].at[me], outs[a].at[me], local_sems.at[a])
            cp.start()
            local.append(cp)
            for j, (px, py) in enumerate(others):
                cp = _remote(ins[a].at[2 * px + py], outs[a].at[me], send_sems, recv_sems, 3 * a + j, (px, py, c))
                cp.start()
                sends.append(cp)
        for a in range(na):
            for j, (px, py) in enumerate(others):
                slot = outs[a].at[2 * px + py]
                _remote(slot, slot, send_sems, recv_sems, 3 * a + j, (px, py, c)).wait_recv()
        for cp in sends:
            cp.wait_send()
        for cp in local:
            cp.wait()

    return pl.pallas_call(
        body, out_shape=[jax.ShapeDtypeStruct(p.shape, p.dtype) for p in partials],
        in_specs=[ANY] * na, out_specs=[ANY] * na,
        scratch_shapes=[pltpu.SemaphoreType.DMA((3 * na,)), pltpu.SemaphoreType.DMA((3 * na,)),
                        pltpu.SemaphoreType.DMA((na,))],
        name="chip_exchange",
    )(*partials)


def _peer_of(k, x, y, c):
    return (1 - x if k & 4 else x, 1 - y if k & 2 else y, 1 - c if k & 1 else c)


def small_exchange(vec, reduce, name):
    r = vec.shape[0]

    def body(v_ref, o_ref, *scratch):
        if reduce:
            buf, send_sems, recv_sems = scratch
        else:
            buf, (send_sems, recv_sems) = o_ref, scratch
        x, y, c, _ = _place()
        me = 4 * x + 2 * y + c
        copies = []
        for k in range(1, N_DEV):
            cp = _remote(v_ref, buf.at[me], send_sems, recv_sems, k - 1, _peer_of(k, x, y, c))
            cp.start()
            copies.append(cp)
        buf[me] = v_ref[...]
        for k in range(1, N_DEV):
            px, py, pc = _peer_of(k, x, y, c)
            slot = buf.at[4 * px + 2 * py + pc]
            _remote(slot, slot, send_sems, recv_sems, k - 1, (px, py, pc)).wait_recv()
        for cp in copies:
            cp.wait_send()
        if reduce:
            tot = buf[0]
            for b in range(1, N_DEV):
                tot = tot + buf[b]
            o_ref[...] = tot

    sems = [pltpu.SemaphoreType.DMA((N_DEV - 1,)), pltpu.SemaphoreType.DMA((N_DEV - 1,))]
    if reduce:
        out_shape = jax.ShapeDtypeStruct((r, LANES), F32)
        scratch = [pltpu.VMEM((N_DEV, r, LANES), F32)] + sems
    else:
        out_shape = jax.ShapeDtypeStruct((N_DEV, r, LANES), F32)
        scratch = sems
    return pl.pallas_call(
        body, out_shape=out_shape, in_specs=[VMEM_FULL], out_specs=VMEM_FULL,
        scratch_shapes=scratch, name=name, compiler_params=_params(None),
    )(vec)


def _row_tile(rows, cols):
    best = None
    for tb in range(16, rows + 1, 16):
        if rows % tb == 0 and tb * cols <= ELEMWISE_BLOCK_ELEMS:
            best = tb
    return best if best is not None else rows


def add_sibling(grad, recv, axis, core):
    nl = grad.shape[0]
    _, _, r, cc = recv.shape
    tb = _row_tile(r, cc)
    per = r // tb

    def body(c_ref, g_ref, r_ref, o_ref):
        del c_ref
        o_ref[...] = (g_ref[...].astype(F32) + r_ref[...].astype(F32)).astype(BF16)

    if axis == 2:
        g_spec = pl.BlockSpec((None, tb, cc), lambda q, l, i, c_ref: (l, i, 2 * q + c_ref[0]))
    else:
        g_spec = pl.BlockSpec((None, tb, cc), lambda q, l, i, c_ref: (l, (2 * q + c_ref[0]) * per + i, 0))
    slot = pl.BlockSpec((None, None, tb, cc), lambda q, l, i, c_ref: (q, l, i, 0))
    return pl.pallas_call(
        body, out_shape=jax.ShapeDtypeStruct(recv.shape, BF16),
        grid_spec=pltpu.PrefetchScalarGridSpec(
            num_scalar_prefetch=1, grid=(N_CHIP, nl, per), in_specs=[g_spec, slot], out_specs=slot),
        name="add_sibling", compiler_params=_params(("parallel", "parallel", "parallel")),
    )(core, grad, recv)


def _adamw(w, g, m, v):
    m = ADAM_B1 * m + (1.0 - ADAM_B1) * g
    v = ADAM_B2 * v + (1.0 - ADAM_B2) * (g * g)
    m_hat = m / (1.0 - ADAM_B1 ** ADAM_STEP)
    v_hat = v / (1.0 - ADAM_B2 ** ADAM_STEP)
    delta = -ADAM_LR * (m_hat / (jnp.sqrt(v_hat) + ADAM_EPS) + ADAM_WD * w)
    return delta, m, v


def adam_from_partials(recv, w, m, v):
    nl, r, cc = w.shape
    tb = _row_tile(r, cc)

    def body(p0, p1, p2, p3, w_ref, m_ref, v_ref, g_out, d_out, m_out, v_out):
        g = p0[...].astype(F32) + p1[...].astype(F32) + p2[...].astype(F32) + p3[...].astype(F32)
        d, mn, vn = _adamw(w_ref[...], g, m_ref[...], v_ref[...])
        g_out[...], d_out[...], m_out[...], v_out[...] = g, d, mn, vn

    slot = lambda q: pl.BlockSpec((None, None, tb, cc), lambda l, i: (q, l, i, 0))
    blk = pl.BlockSpec((None, tb, cc), lambda l, i: (l, i, 0))
    shp = jax.ShapeDtypeStruct(w.shape, F32)
    return pl.pallas_call(
        body, out_shape=[shp] * 4, grid=(nl, r // tb),
        in_specs=[slot(0), slot(1), slot(2), slot(3), blk, blk, blk], out_specs=[blk] * 4,
        name="adam_big", compiler_params=_params(("parallel", "parallel")),
    )(recv, recv, recv, recv, w, m, v)


def adam_small(w, g, m, v):
    def body(w_ref, g_ref, m_ref, v_ref, d_out, m_out, v_out):
        d_out[...], m_out[...], v_out[...] = _adamw(w_ref[...], g_ref[...], m_ref[...], v_ref[...])

    shp = jax.ShapeDtypeStruct(w.shape, F32)
    return pl.pallas_call(body, out_shape=[shp] * 3, name="adam_small")(w, g, m, v)


def _pack(arrays, multiple=8 * LANES):
    flat = jnp.concatenate([a.reshape(-1) for a in arrays])
    pad = (-flat.shape[0]) % multiple
    if pad:
        flat = jnp.concatenate([flat, jnp.zeros((pad,), flat.dtype)])
    return flat.reshape(-1, LANES)


def _unpack(packed, shapes):
    flat = packed.reshape(packed.shape[:-2] + (-1,))
    out, off = [], 0
    for shp in shapes:
        n = math.prod(shp)
        out.append(flat[..., off:off + n].reshape(packed.shape[:-2] + tuple(shp)))
        off += n
    return out


def _unshard_last(stacked):
    moved = jnp.moveaxis(stacked, 0, -2)
    return moved.reshape(moved.shape[:-2] + (-1,))


def _shard_last(full, block):
    n = full.shape[-1] // N_DEV
    return lax.dynamic_slice_in_dim(full, block * n, n, axis=full.ndim - 1)


def local_step(x, target, norm_g, lb_logits, onorm_g, pool_scale, conv_w, conv_b,
               w_in, w_out, w_qkv, w_so, w_pool, w_up, w_down):
    t, d = x.shape
    depth = norm_g.shape[0]
    ng = lambda i, j: norm_g[i, j].reshape(1, d)
    lbs = lower_bounds_fwd(lb_logits)
    saved = []
    h = x
    _, u16, u32 = res_norm(h, None, None, ng(0, 0), "norm_in")
    for i in range(depth):
        kind, j = i % 3, i // 3
        s = {"h_in": h, "u1": u16}
        if kind == 0:
            s["proj"] = matmul(u16, Layer(w_in, j), "nn", F32, f"hg_in_{i}")
            s["y"] = hgrn_fwd(s["proj"], lbs[i].reshape(1, -1), onorm_g[j].reshape(1, -1), i > 0, f"hgrn_fwd_{i}")
            mix = matmul(s["y"], Layer(w_out, j), "nn", F32, f"hg_out_{i}")
        elif kind == 1:
            s["qkv"] = matmul(u16, Layer(w_qkv, j), "nn", F32, f"sb_qkv_{i}")
            s["o"] = sba_fwd(s["qkv"], f"sba_fwd_{i}")
            mix = matmul(s["o"], Layer(w_so, j), "nn", F32, f"sb_out_{i}")
        else:
            s["u1f"] = u32
            mix = pool_fwd(u32, w_pool[4 * j:4 * j + 4], pool_scale[j].reshape(1, d), f"pool_fwd_{i}")
        s["mix"] = mix
        h_mid, u2, _ = res_norm(h, mix, ng(i, 1), ng(i, 2), f"norm_mid_{i}")
        s["h_mid"], s["u2"] = h_mid, u2
        s["up"] = matmul(u2, Layer(w_up, i), "nn", F32, f"ffn_up_{i}")
        s["act"] = conv_glu_fwd(s["up"], conv_w[i], conv_b[i].reshape(1, -1), f"glu_fwd_{i}")
        s["f"] = matmul(s["act"], Layer(w_down, i), "nn", F32, f"ffn_down_{i}")
        nxt = ng(i + 1, 0) if i + 1 < depth else None
        h, u16, u32 = res_norm(h_mid, s["f"], ng(i, 3), nxt, f"norm_out_{i}",
                               want_f32=(nxt is not None and (i + 1) % 3 == 2))
        saved.append(s)

    loss_acc, dh = loss_head(h, target, "loss_head")

    d_norm = [[None] * 4 for _ in range(depth)]
    d_lbs = jnp.zeros_like(lbs)
    d_onorm = [None] * onorm_g.shape[0]
    d_pscale = [None] * pool_scale.shape[0]
    d_cw, d_cb = [None] * depth, [None] * depth
    g_in, g_out = lax.empty(w_in.shape, BF16), lax.empty(w_out.shape, BF16)
    g_qkv, g_so = lax.empty(w_qkv.shape, BF16), lax.empty(w_so.shape, BF16)
    g_pool = lax.empty(w_pool.shape, BF16)
    g_up, g_down = lax.empty(w_up.shape, BF16), lax.empty(w_down.shape, BF16)

    for i in reversed(range(depth)):
        kind, j = i % 3, i // 3
        s = saved[i]
        df, d_norm[i][3] = norm_bwd(s["f"], ng(i, 3), dh, None, BF16, f"nb_out_{i}")
        dact = matmul(df, Layer(w_down, i), "nt", F32, f"d_act_{i}")
        g_down = matmul(s["act"], df, "tn", BF16, f"dw_down_{i}", out_stack=(g_down, i))
        dup, dcw, dcb = conv_glu_bwd(s["up"], conv_w[i], conv_b[i].reshape(1, -1), dact, f"glu_bwd_{i}")
        d_cw[i] = jnp.moveaxis(dcw, 0, 1).reshape(CONV_WIDTH, -1)
        d_cb[i] = dcb.reshape(-1)
        du2 = matmul(Split(dup), Layer(w_up, i), "nt", F32, f"d_u2_{i}")
        g_up = matmul(s["u2"], Split(dup), "tn", BF16, f"dw_up_{i}", out_stack=(g_up, i))
        dh_mid, d_norm[i][2] = norm_bwd(s["h_mid"], ng(i, 2), du2, dh, F32, f"nb_mid_{i}")
        dm, d_norm[i][1] = norm_bwd(s["mix"], ng(i, 1), dh_mid, None, F32 if kind == 2 else BF16, f"nb_mix_{i}")
        if kind == 0:
            dy = matmul(dm, Layer(w_out, j), "nt", F32, f"d_y_{i}")
            g_out = matmul(s["y"], dm, "tn", BF16, f"dw_hgout_{i}", out_stack=(g_out, j))
            dproj, d_onorm[j], dlb = hgrn_bwd(s["proj"], lbs[i].reshape(1, -1), onorm_g[j].reshape(1, -1), dy,
                                              i > 0, f"hgrn_bwd_{i}")
            d_lbs = d_lbs.at[i].set(dlb[0])
            du1 = matmul(Split(dproj), Layer(w_in, j), "nt", F32, f"d_u1_{i}")
            g_in = matmul(s["u1"], Split(dproj), "tn", BF16, f"dw_hgin_{i}", out_stack=(g_in, j))
        elif kind == 1:
            do = matmul(dm, Layer(w_so, j), "nt", F32, f"d_o_{i}")
            g_so = matmul(s["o"], dm, "tn", BF16, f"dw_sbout_{i}", out_stack=(g_so, j))
            dqkv = sba_bwd(s["qkv"], s["o"], do, f"sba_bwd_{i}")
            du1 = matmul(Split(dqkv), Layer(w_qkv, j), "nt", F32, f"d_u1_{i}")
            g_qkv = matmul(s["u1"], Split(dqkv), "tn", BF16, f"dw_sbqkv_{i}", out_stack=(g_qkv, j))
        else:
            du1, g_pool, d_pscale[j] = pool_bwd(s["u1f"], w_pool[4 * j:4 * j + 4], pool_scale[j].reshape(1, d),
                                                dm, g_pool, f"pool_bwd_{i}")
        dh, d_norm[i][0] = norm_bwd(s["h_in"], ng(i, 0), du1, dh_mid, F32, f"nb_in_{i}")

    small = {
        "norm_g": jnp.stack([jnp.stack([v.reshape(d) for v in row]) for row in d_norm]),
        "lb_logits": lower_bounds_bwd(lb_logits, d_lbs),
        "onorm_g": jnp.stack([v.reshape(-1) for v in d_onorm]),
        "pool_scale": jnp.stack([v.reshape(-1) for v in d_pscale]),
        "conv_w": jnp.stack(d_cw),
        "conv_b": jnp.stack(d_cb),
    }
    big = [g_in, g_out, g_qkv, g_so, g_pool, g_up, g_down]
    return loss_acc, dh, small, big


BIG_AXES = (2, 1, 2, 1, 1, 2, 1)
SMALL_SHARDED = ("norm_g", "onorm_g", "pool_scale", "conv_w")
SMALL_ORDER = ("norm_g", "lb_logits", "onorm_g", "pool_scale", "conv_w", "conv_b")


def kernel(x, norm_g, hgrn_lb_logits, hgrn_w_in, hgrn_onorm_g, hgrn_w_out, sba_w_qkv, sba_w_out, pool_w, pool_scale, ffn_w_up, ffn_conv_w, ffn_conv_b, ffn_w_down, loss_target, m_norm_g, m_hgrn_lb_logits, m_hgrn_w_in, m_hgrn_onorm_g, m_hgrn_w_out, m_sba_w_qkv, m_sba_w_out, m_pool_w, m_pool_scale, m_ffn_w_up, m_ffn_conv_w, m_ffn_conv_b, m_ffn_w_down, v_norm_g, v_hgrn_lb_logits, v_hgrn_w_in, v_hgrn_onorm_g, v_hgrn_w_out, v_sba_w_qkv, v_sba_w_out, v_pool_w, v_pool_scale, v_ffn_w_up, v_ffn_conv_w, v_ffn_conv_b, v_ffn_w_down):
    cx, cy, cc = lax.axis_index("x"), lax.axis_index("y"), lax.axis_index("c")
    block = 4 * cx + 2 * cy + cc
    core = cc.astype(jnp.int32).reshape(1)

    pool3 = lambda a: a.reshape(a.shape[0] * a.shape[1], a.shape[2], a.shape[3])
    big_w = [hgrn_w_in, hgrn_w_out, sba_w_qkv, sba_w_out, pool3(pool_w), ffn_w_up, ffn_w_down]
    big_m = [m_hgrn_w_in, m_hgrn_w_out, m_sba_w_qkv, m_sba_w_out, pool3(m_pool_w), m_ffn_w_up, m_ffn_w_down]
    big_v = [v_hgrn_w_in, v_hgrn_w_out, v_sba_w_qkv, v_sba_w_out, pool3(v_pool_w), v_ffn_w_up, v_ffn_w_down]

    sharded = {"norm_g": norm_g, "onorm_g": hgrn_onorm_g, "pool_scale": pool_scale, "conv_w": ffn_conv_w}
    gathered = small_exchange(_pack([sharded[n] for n in SMALL_SHARDED]), False, "gather_small")
    parts = _unpack(gathered, [sharded[n].shape for n in SMALL_SHARDED])
    full = {n: _unshard_last(p) for n, p in zip(SMALL_SHARDED, parts)}

    full_w = gather_weights([w.astype(BF16) for w in big_w], BIG_AXES)

    loss_acc, grad_x, small_g, big_g = local_step(
        x[0], loss_target[0], full["norm_g"], hgrn_lb_logits, full["onorm_g"], full["pool_scale"],
        full["conv_w"], ffn_conv_b, *full_w)
    loss = lax.psum(loss_acc[0, 0], ("x", "y", "c"))

    shapes = [small_g[n].shape for n in SMALL_ORDER]
    summed = _unpack(small_exchange(_pack([small_g[n] for n in SMALL_ORDER]), True, "reduce_small"), shapes)
    sg = {n: (_shard_last(g, block) if n in SMALL_SHARDED else g) for n, g in zip(SMALL_ORDER, summed)}
    sw = {"norm_g": norm_g, "lb_logits": hgrn_lb_logits, "onorm_g": hgrn_onorm_g, "pool_scale": pool_scale,
          "conv_w": ffn_conv_w, "conv_b": ffn_conv_b}
    sm = {"norm_g": m_norm_g, "lb_logits": m_hgrn_lb_logits, "onorm_g": m_hgrn_onorm_g, "pool_scale": m_pool_scale,
          "conv_w": m_ffn_conv_w, "conv_b": m_ffn_conv_b}
    sv = {"norm_g": v_norm_g, "lb_logits": v_hgrn_lb_logits, "onorm_g": v_hgrn_onorm_g, "pool_scale": v_pool_scale,
          "conv_w": v_ffn_conv_w, "conv_b": v_ffn_conv_b}
    sshapes = [sw[n].shape for n in SMALL_ORDER]
    packed = [_pack([dct[n] for n in SMALL_ORDER]) for dct in (sw, sg, sm, sv)]
    s_delta, s_m, s_v = [dict(zip(SMALL_ORDER, _unpack(p, sshapes))) for p in adam_small(*packed)]

    from_sibling = sibling_exchange(big_g, BIG_AXES)
    partials = [add_sibling(g, r, ax, core) for g, r, ax in zip(big_g, from_sibling, BIG_AXES)]
    from_chips = chip_exchange(partials)
    upd = [adam_from_partials(r, w, m, v) for r, w, m, v in zip(from_chips, big_w, big_m, big_v)]
    b_grad, b_delta, b_m, b_v = [[u[k] for u in upd] for k in range(4)]

    def tree(small, bigs):
        bg = list(bigs)
        bg[4] = bg[4].reshape(pool_w.shape)
        return (small["norm_g"], small["lb_logits"], bg[0], small["onorm_g"], bg[1], bg[2], bg[3], bg[4],
                small["pool_scale"], bg[5], small["conv_w"], small["conv_b"], bg[6])

    return (loss, grad_x[None], *tree(sg, b_grad), *tree(s_delta, b_delta), *tree(s_m, b_m), *tree(s_v, b_v))
```

```python
import functools
import math

import jax
import jax.numpy as jnp
from jax import lax
from jax.experimental import pallas as pl
from jax.experimental.pallas import tpu as pltpu

F32 = jnp.float32
BF16 = jnp.bfloat16
HI = lax.Precision.HIGHEST
MESH = pl.DeviceIdType.MESH
ANY = pl.BlockSpec(memory_space=pl.ANY)
VMEM_FULL = pl.BlockSpec(memory_space=pltpu.VMEM)

NORM_EPS = 1e-6
HEAD = 128
HG_CHUNK = 128
HG_SUB = 16
SB_BLOCK = 128
SB_QROWS = 256
POOL_WINDOWS = (2, 4, 8, 16)
CONV_WIDTH = 3
ROW_TILE = 256
N_DEV = 8
N_CHIP = 4

ADAM_LR = 0.001
ADAM_B1 = 0.9
ADAM_B2 = 0.999
ADAM_EPS = 1e-08
ADAM_WD = 0.01
ADAM_STEP = 10

VMEM_LIMIT = 48 * 1024 * 1024
LANES = 128
ELEMWISE_BLOCK_ELEMS = 256 * 1024


def _params(sem=None, vmem=VMEM_LIMIT):
    return pltpu.CompilerParams(dimension_semantics=sem, vmem_limit_bytes=vmem)


def _tile(n, prefs=(1024, 512, 256, 128)):
    for p in prefs:
        if n % p == 0:
            return p
    return n


def _dot(a, b, prec=None):
    return jnp.dot(a, b, precision=prec, preferred_element_type=F32)


def _dot_nt(a, b, prec=None):
    return lax.dot_general(a, b, (((1,), (1,)), ((), ())), precision=prec, preferred_element_type=F32)


def _dot_tn(a, b, prec=None):
    return lax.dot_general(a, b, (((0,), (0,)), ((), ())), precision=prec, preferred_element_type=F32)


def _split_dot(x, tri, parts, left=False):
    tot, rest = None, x
    for p in range(parts):
        h = rest.astype(BF16)
        d = _dot(tri, h) if left else _dot(h, tri)
        tot = d if tot is None else tot + d
        if p + 1 < parts:
            rest = rest - h.astype(F32)
    return tot


def _dot1(a, b, fn):
    return fn(a.astype(BF16), b.astype(BF16))


def _dot3(a, b, fn):
    ah, bh = a.astype(BF16), b.astype(BF16)
    al, bl = (a - ah.astype(F32)).astype(BF16), (b - bh.astype(F32)).astype(BF16)
    return fn(ah, bh) + fn(ah, bl) + fn(al, bh)


def _sigmoid(x):
    return jax.nn.sigmoid(x)


def _softplus(x):
    return jnp.maximum(x, 0.0) + jnp.log1p(jnp.exp(-jnp.abs(x)))


class Layer:
    def __init__(self, arr, l):
        self.arr, self.l = arr, l
        self.shape = arr.shape[1:]
        self.part = None


class Split:
    def __init__(self, arr):
        self.arr = arr
        self.shape = (arr.shape[1], arr.shape[0] * arr.shape[2])
        self.part = arr.shape[2]


class Plain:
    def __init__(self, arr):
        self.arr = arr
        self.shape = arr.shape
        self.part = None


def _wrap(op):
    return op if isinstance(op, (Layer, Split, Plain)) else Plain(op)


def _op_spec(op, br, bc, rc_of_grid):
    if isinstance(op, Layer):
        l = op.l
        return pl.BlockSpec((None, br, bc), lambda i, j, k: (l, *rc_of_grid(i, j, k)))
    if isinstance(op, Split):
        per = op.part // bc

        def imap(i, j, k):
            r, c = rc_of_grid(i, j, k)
            return (lax.div(c, per), r, lax.rem(c, per))
        return pl.BlockSpec((None, br, bc), imap)
    return pl.BlockSpec((br, bc), rc_of_grid)


def matmul(a, b, mode, out_dtype, name, out_stack=None):
    a, b = _wrap(a), _wrap(b)
    if mode == "nn":
        (m, kd), (kd2, n) = a.shape, b.shape
    elif mode == "nt":
        (m, kd), (n, kd2) = a.shape, b.shape
    else:
        (kd, m), (kd2, n) = a.shape, b.shape
    assert kd == kd2, (mode, a.shape, b.shape)

    def dim_tile(full, *ops_on_cols):
        base = full
        for op in ops_on_cols:
            if op.part is not None:
                base = math.gcd(base, op.part)
        return _tile(base)

    tm = dim_tile(m, *([a] if mode == "tn" else []))
    tn = dim_tile(n, *([b] if mode in ("nn", "tn") else []))
    tk = dim_tile(kd, *(([a] if mode in ("nn", "nt") else []) + ([b] if mode == "nt" else [])))
    tk = min(tk, 512)
    nk = kd // tk

    if mode == "nn":
        a_spec = _op_spec(a, tm, tk, lambda i, j, k: (i, k))
        b_spec = _op_spec(b, tk, tn, lambda i, j, k: (k, j))
        dot = _dot
    elif mode == "nt":
        a_spec = _op_spec(a, tm, tk, lambda i, j, k: (i, k))
        b_spec = _op_spec(b, tn, tk, lambda i, j, k: (j, k))
        dot = _dot_nt
    else:
        a_spec = _op_spec(a, tk, tm, lambda i, j, k: (k, i))
        b_spec = _op_spec(b, tk, tn, lambda i, j, k: (k, j))
        dot = _dot_tn

    def body(*refs):
        a_ref, b_ref = refs[0], refs[1]
        o_ref, acc_ref = refs[-2], refs[-1]
        k = pl.program_id(2)

        @pl.when(k == 0)
        def _():
            acc_ref[...] = jnp.zeros_like(acc_ref)

        acc_ref[...] += dot(a_ref[...].astype(BF16), b_ref[...].astype(BF16))

        @pl.when(k == nk - 1)
        def _():
            o_ref[...] = acc_ref[...].astype(o_ref.dtype)

    in_specs = [a_spec, b_spec]
    args = [a.arr, b.arr]
    aliases = {}
    if out_stack is None:
        out_shape = jax.ShapeDtypeStruct((m, n), out_dtype)
        out_spec = pl.BlockSpec((tm, tn), lambda i, j, k: (i, j))
    else:
        buf, l = out_stack
        assert buf.shape[1:] == (m, n) and buf.dtype == out_dtype
        out_shape = jax.ShapeDtypeStruct(buf.shape, buf.dtype)
        out_spec = pl.BlockSpec((None, tm, tn), lambda i, j, k: (l, i, j))
        in_specs.append(ANY)
        args.append(buf)
        aliases = {2: 0}
    return pl.pallas_call(
        body, out_shape=out_shape, grid=(m // tm, n // tn, nk),
        in_specs=in_specs, out_specs=out_spec,
        scratch_shapes=[pltpu.VMEM((tm, tn), F32)],
        input_output_aliases=aliases, name=name,
        compiler_params=_params(("parallel", "parallel", "arbitrary")),
    )(*args)


def _rms(x, g):
    r = lax.rsqrt(jnp.mean(x * x, axis=-1, keepdims=True) + NORM_EPS)
    return x * r * g


def res_norm(h, m, g_a, g_b, name, want_f32=False):
    t, d = h.shape
    tr = min(ROW_TILE, t)
    has_m, has_u = m is not None, g_b is not None
    row = pl.BlockSpec((tr, d), lambda i: (i, 0))
    vec = pl.BlockSpec((1, d), lambda i: (0, 0))

    def body(*refs):
        refs = list(refs)
        h_ref = refs.pop(0)
        hn = h_ref[...]
        if has_m:
            m_ref, ga_ref = refs.pop(0), refs.pop(0)
            hn = hn + _rms(m_ref[...], ga_ref[...])
        if has_u:
            gb_ref = refs.pop(0)
        if has_m:
            refs.pop(0)[...] = hn
        if has_u:
            u = _rms(hn, gb_ref[...])
            refs.pop(0)[...] = u.astype(BF16)
            if want_f32:
                refs.pop(0)[...] = u

    args, in_specs, out_shape, out_specs = [h], [row], [], []
    if has_m:
        args += [m, g_a]
        in_specs += [row, vec]
        out_shape.append(jax.ShapeDtypeStruct((t, d), F32))
        out_specs.append(row)
    if has_u:
        args.append(g_b)
        in_specs.append(vec)
        out_shape.append(jax.ShapeDtypeStruct((t, d), BF16))
        out_specs.append(row)
        if want_f32:
            out_shape.append(jax.ShapeDtypeStruct((t, d), F32))
            out_specs.append(row)
    outs = list(pl.pallas_call(
        body, out_shape=out_shape, grid=(t // tr,), in_specs=in_specs, out_specs=out_specs,
        name=name, compiler_params=_params(("parallel",)),
    )(*args))
    h_new = outs.pop(0) if has_m else None
    u16 = outs.pop(0) if has_u else None
    u32 = outs.pop(0) if (has_u and want_f32) else None
    return h_new, u16, u32


def norm_bwd(x, g, dy, add, out_dtype, name):
    t, d = x.shape
    tr = min(ROW_TILE, t)
    has_add = add is not None
    row = pl.BlockSpec((tr, d), lambda i: (i, 0))
    vec = pl.BlockSpec((1, d), lambda i: (0, 0))

    def body(*refs):
        if has_add:
            x_ref, g_ref, dy_ref, add_ref, dx_ref, dg_ref = refs
        else:
            x_ref, g_ref, dy_ref, dx_ref, dg_ref = refs
        xv = x_ref[...]
        dyv = dy_ref[...].astype(F32)
        r = lax.rsqrt(jnp.mean(xv * xv, axis=-1, keepdims=True) + NORM_EPS)
        gy = dyv * g_ref[...]
        dx = r * gy - xv * (r * r * r * jnp.mean(gy * xv, axis=-1, keepdims=True))
        if has_add:
            dx = dx + add_ref[...]
        dx_ref[...] = dx.astype(dx_ref.dtype)

        @pl.when(pl.program_id(0) == 0)
        def _():
            dg_ref[...] = jnp.zeros_like(dg_ref)

        dg_ref[...] += jnp.sum(dyv * xv * r, axis=0, keepdims=True)

    args = [x, g, dy] + ([add] if has_add else [])
    in_specs = [row, vec, row] + ([row] if has_add else [])
    return pl.pallas_call(
        body, out_shape=[jax.ShapeDtypeStruct((t, d), out_dtype), jax.ShapeDtypeStruct((1, d), F32)],
        grid=(t // tr,), in_specs=in_specs, out_specs=[row, vec],
        name=name, compiler_params=_params(("arbitrary",)),
    )(*args)


def loss_head(y, target, name):
    t, d = y.shape
    tr = min(ROW_TILE, t)
    row = pl.BlockSpec((tr, d), lambda i: (i, 0))
    acc = pl.BlockSpec((8, LANES), lambda i: (0, 0))

    def body(y_ref, t_ref, loss_ref, dy_ref):
        e = y_ref[...] - t_ref[...]
        dy_ref[...] = e * (1.0 / d)

        @pl.when(pl.program_id(0) == 0)
        def _():
            loss_ref[...] = jnp.zeros_like(loss_ref)

        loss_ref[...] += jnp.sum(e * e) * (0.5 / d)

    return pl.pallas_call(
        body, out_shape=[jax.ShapeDtypeStruct((8, LANES), F32), jax.ShapeDtypeStruct((t, d), F32)],
        grid=(t // tr,), in_specs=[row, row], out_specs=[acc, row],
        name=name, compiler_params=_params(("arbitrary",)),
    )(y, target)


def _depth_softmax(ref, depth):
    rows = [ref[i:i + 1, :] for i in range(depth)]
    mx = functools.reduce(jnp.maximum, rows)
    ex = [jnp.exp(r - mx) for r in rows]
    tot = functools.reduce(lambda p, q: p + q, ex)
    return [e / tot for e in ex]


def lower_bounds_fwd(logits):
    depth, kw = logits.shape

    def body(l_ref, o_ref):
        s = _depth_softmax(l_ref, depth)
        run = jnp.zeros_like(s[0])
        o_ref[0:1, :] = run
        for i in range(1, depth):
            run = run + s[i]
            o_ref[i:i + 1, :] = run

    return pl.pallas_call(body, out_shape=jax.ShapeDtypeStruct((depth, kw), F32), name="lb_fwd")(logits)


def lower_bounds_bwd(logits, dlb):
    depth, kw = logits.shape

    def body(l_ref, d_ref, o_ref):
        s = _depth_softmax(l_ref, depth)
        ds = [jnp.zeros_like(s[0]) for _ in range(depth)]
        run = jnp.zeros_like(s[0])
        for j in range(depth - 1, 0, -1):
            run = run + d_ref[j:j + 1, :]
            ds[j] = run
        dot = functools.reduce(lambda p, q: p + q, [s[j] * ds[j] for j in range(depth)])
        for j in range(depth):
            o_ref[j:j + 1, :] = s[j] * (ds[j] - dot)

    return pl.pallas_call(body, out_shape=jax.ShapeDtypeStruct((depth, kw), F32), name="lb_bwd")(logits, dlb)


def _hg_gates(qp, fp, lb_row, has_lb):
    sig = _sigmoid(fp)
    nsig = _sigmoid(-fp)
    ls = jnp.minimum(fp, 0.0) - jnp.log1p(jnp.exp(-jnp.abs(fp)))
    if has_lb:
        a = jnp.log(lb_row)
        bb = jnp.log1p(-lb_row) + ls
        g = jnp.maximum(a, bb) + jnp.log1p(jnp.exp(-jnp.abs(a - bb)))
        w = jnp.exp(bb - g)
        k = (1.0 - lb_row) * nsig
    else:
        g, w, k = ls, None, nsig
    q = qp * _sigmoid(qp)
    return q, k, g, sig, nsig, w


def _hg_masks():
    c = HG_CHUNK
    row = lax.broadcasted_iota(jnp.int32, (c, c), 0)
    col = lax.broadcasted_iota(jnp.int32, (c, c), 1)
    lower = (col <= row).astype(BF16)
    upper = (col >= row).astype(BF16)
    krow = lax.broadcasted_iota(jnp.int32, (c, HEAD), 0)
    arow = lax.broadcasted_iota(jnp.int32, (HG_SUB, c), 0)
    acol = lax.broadcasted_iota(jnp.int32, (HG_SUB, c), 1)
    return lower, upper, krow, arow, acol


def _hg_sub(i, q, k, b, b_ref, krow, arow, acol):
    r0 = i * HG_SUB
    m = b_ref[r0:r0 + 1, :]
    ebq = jnp.exp(b[r0:r0 + HG_SUB, :] - m)
    qh = (q[r0:r0 + HG_SUB, :] * ebq).astype(BF16)
    ek = jnp.exp(jnp.where(krow < r0 + HG_SUB, m - b, 0.0))
    kh = (k * ek).astype(BF16)
    mask = acol <= arow + r0
    amat = jnp.where(mask, _dot_nt(qh, kh), 0.0).astype(BF16)
    return ebq, qh, ek, kh, mask, amat


def hgrn_fwd(proj, lb, onorm_g, has_lb, name):
    t, w4 = proj.shape
    kw = w4 // 4
    nh, nc, c = kw // HEAD, t // HG_CHUNK, HG_CHUNK

    def body(qp_ref, fp_ref, iv_ref, gp_ref, lb_ref, on_ref, y_ref, st_ref, b_ref):
        lower, _, krow, arow, acol = _hg_masks()
        lb_row, gam = lb_ref[...], on_ref[...]
        st_ref[...] = jnp.zeros_like(st_ref)

        def chunk(ci, carry):
            rs = pl.ds(pl.multiple_of(ci * c, c), c)
            v = iv_ref[rs, :].astype(BF16)
            gp = gp_ref[rs, :]
            q, k, g, _, _, _ = _hg_gates(qp_ref[rs, :], fp_ref[rs, :], lb_row, has_lb)
            b = _split_dot(g, lower, 3, left=True)
            b_ref[...] = b
            st = st_ref[...]
            o = _dot_nt((q * jnp.exp(b)).astype(BF16), st.astype(BF16))
            parts = []
            for i in range(c // HG_SUB):
                amat = _hg_sub(i, q, k, b, b_ref, krow, arow, acol)[-1]
                parts.append(_dot(amat, v))
            o = o + jnp.concatenate(parts, axis=0)
            bl = b_ref[c - 1:c, :]
            st_ref[...] = jnp.exp(bl) * st + _dot_tn(v, (k * jnp.exp(bl - b)).astype(BF16))
            r = lax.rsqrt(jnp.mean(o * o, axis=-1, keepdims=True) + NORM_EPS)
            y_ref[rs, :] = (o * r * gam * (gp * _sigmoid(gp))).astype(BF16)
            return carry

        lax.fori_loop(0, nc, chunk, 0)

    col = lambda p: pl.BlockSpec((t, HEAD), lambda h: (0, p * nh + h))
    vec = pl.BlockSpec((1, HEAD), lambda h: (0, h))
    return pl.pallas_call(
        body, out_shape=jax.ShapeDtypeStruct((t, kw), BF16), grid=(nh,),
        in_specs=[col(0), col(1), col(2), col(3), vec, vec],
        out_specs=pl.BlockSpec((t, HEAD), lambda h: (0, h)),
        scratch_shapes=[pltpu.VMEM((HEAD, HEAD), F32), pltpu.VMEM((c, HEAD), F32)],
        name=name, compiler_params=_params(("parallel",)),
    )(proj, proj, proj, proj, lb, onorm_g)


def hgrn_bwd(proj, lb, onorm_g, dy, has_lb, name):
    t, w4 = proj.shape
    kw = w4 // 4
    nh, nc, c = kw // HEAD, t // HG_CHUNK, HG_CHUNK

    def body(qp_ref, fp_ref, iv_ref, gp_ref, lb_ref, on_ref, dy_ref,
             dp_ref, dgam_ref, dlb_ref, sst_ref, o_ref, b_ref, dst_ref, car_ref):
        lower, upper, krow, arow, acol = _hg_masks()
        lb_row, gam = lb_ref[...], on_ref[...]
        mm = _dot3 if has_lb else _dot1

        def recompute(ci):
            rs = pl.ds(pl.multiple_of(ci * c, c), c)
            gates = _hg_gates(qp_ref[rs, :], fp_ref[rs, :], lb_row, has_lb)
            b = _split_dot(gates[2], lower, 3, left=True)
            b_ref[...] = b
            return rs, gates, b

        def fwd_chunk(ci, carry):
            rs, (q, k, g, _, _, _), b = recompute(ci)
            v = iv_ref[rs, :].astype(BF16)
            st = carry
            sst_ref[ci] = st
            o = _dot_nt((q * jnp.exp(b)).astype(BF16), st.astype(BF16))
            parts = []
            for i in range(c // HG_SUB):
                amat = _hg_sub(i, q, k, b, b_ref, krow, arow, acol)[-1]
                parts.append(_dot(amat, v))
            o_ref[rs, :] = o + jnp.concatenate(parts, axis=0)
            bl = b_ref[c - 1:c, :]
            return jnp.exp(bl) * st + mm(iv_ref[rs, :], k * jnp.exp(bl - b), _dot_tn)

        lax.fori_loop(0, nc, fwd_chunk, jnp.zeros((HEAD, HEAD), F32))

        dst_ref[...] = jnp.zeros_like(dst_ref)
        car_ref[...] = jnp.zeros_like(car_ref)
        dgam_ref[...] = jnp.zeros_like(dgam_ref)
        dlb_ref[...] = jnp.zeros_like(dlb_ref)

        def bwd_chunk(step, carry):
            ci = nc - 1 - step
            rs, (q, k, g, sig, nsig, w), b = recompute(ci)
            qp, gp, v32 = qp_ref[rs, :], gp_ref[rs, :], iv_ref[rs, :]
            v = v32.astype(BF16)
            dyv, o = dy_ref[rs, :], o_ref[rs, :]
            st = sst_ref[ci]
            dst_new = dst_ref[...]
            r = lax.rsqrt(jnp.mean(o * o, axis=-1, keepdims=True) + NORM_EPS)
            on = o * r
            sgm = _sigmoid(gp)
            sg = gp * sgm
            dgam_ref[...] += jnp.sum(dyv * sg * on, axis=0, keepdims=True)
            dgp = dyv * on * gam * (sgm * (1.0 + gp * (1.0 - sgm)))
            dn = dyv * gam * sg
            do = r * dn - o * (r * r * r * jnp.mean(dn * o, axis=-1, keepdims=True))
            eb = jnp.exp(b)
            bl = b_ref[c - 1:c, :]
            ekd = jnp.exp(bl - b)
            do16 = do.astype(BF16)
            dq = mm(do, st, _dot) * eb
            dv = _dot1(k * ekd, dst_new, _dot_nt)
            dk = mm(v32, dst_new, _dot) * ekd
            dst_ref[...] = jnp.exp(bl) * dst_new + mm(do, q * eb, _dot_tn)
            dq_parts = []
            for i in range(c // HG_SUB):
                r0 = i * HG_SUB
                ebq, _, ek, _, mask, amat = _hg_sub(i, q, k, b, b_ref, krow, arow, acol)
                da = jnp.where(mask, mm(do[r0:r0 + HG_SUB, :], v32, _dot_nt), 0.0)
                dq_parts.append(mm(da, k * ek, _dot) * ebq)
                dk = dk + mm(da, q[r0:r0 + HG_SUB, :] * ebq, _dot_tn) * ek
                dv = dv + _dot_tn(amat, do16[r0:r0 + HG_SUB, :])
            dq = dq + jnp.concatenate(dq_parts, axis=0)
            db = q * dq - k * dk
            dg = car_ref[...] + _split_dot(db, upper, 3, left=True)
            car_ref[...] += jnp.sum(db, axis=0, keepdims=True)
            if has_lb:
                dfp = dg * nsig * w - dk * ((1.0 - lb_row) * sig * nsig)
                dlb_ref[...] += jnp.sum(dg * nsig * jnp.exp(-g) - dk * nsig, axis=0, keepdims=True)
            else:
                dfp = dg * nsig - dk * (sig * nsig)
            sq = _sigmoid(qp)
            dp_ref[0, rs, :] = (dq * (sq * (1.0 + qp * (1.0 - sq)))).astype(BF16)
            dp_ref[1, rs, :] = dfp.astype(BF16)
            dp_ref[2, rs, :] = dv.astype(BF16)
            dp_ref[3, rs, :] = dgp.astype(BF16)
            return carry

        lax.fori_loop(0, nc, bwd_chunk, 0)

    col = lambda p: pl.BlockSpec((t, HEAD), lambda h: (0, p * nh + h))
    vec = pl.BlockSpec((1, HEAD), lambda h: (0, h))
    return pl.pallas_call(
        body,
        out_shape=[jax.ShapeDtypeStruct((4, t, kw), BF16), jax.ShapeDtypeStruct((1, kw), F32),
                   jax.ShapeDtypeStruct((1, kw), F32)],
        grid=(nh,),
        in_specs=[col(0), col(1), col(2), col(3), vec, vec, pl.BlockSpec((t, HEAD), lambda h: (0, h))],
        out_specs=[pl.BlockSpec((4, t, HEAD), lambda h: (0, 0, h)), vec, vec],
        scratch_shapes=[pltpu.VMEM((nc, HEAD, HEAD), F32), pltpu.VMEM((t, HEAD), F32),
                        pltpu.VMEM((c, HEAD), F32), pltpu.VMEM((HEAD, HEAD), F32), pltpu.VMEM((1, HEAD), F32)],
        name=name, compiler_params=_params(("parallel",)),
    )(proj, proj, proj, proj, lb, onorm_g, dy)


def _sb_masks():
    m, n = SB_QROWS, SB_BLOCK
    row = lax.broadcasted_iota(jnp.int32, (m, n), 0)
    col = lax.broadcasted_iota(jnp.int32, (m, n), 1)
    r2 = lax.broadcasted_iota(jnp.int32, (n, n), 0)
    c2 = lax.broadcasted_iota(jnp.int32, (n, n), 1)
    after = (r2 > c2).astype(BF16)
    from_ = (r2 >= c2).astype(BF16)
    return row, col, after, from_


def sba_fwd(qkv, name):
    t, w3 = qkv.shape
    wd = w3 // 3
    m, n = SB_QROWS, SB_BLOCK
    nh, nq, per = wd // HEAD, t // m, m // n
    scale = HEAD ** -0.5

    def body(q_ref, k_ref, v_ref, o_ref, q16, k16, v16):
        row, col, after, _ = _sb_masks()
        q16[...] = (q_ref[...] * scale).astype(BF16)
        k16[...] = k_ref[...].astype(BF16)
        v16[...] = v_ref[...].astype(BF16)

        def qblock(qi, carry):
            qs = pl.ds(pl.multiple_of(qi * m, m), m)
            q = q16[qs, :]
            last = (qi + 1) * per - 1

            def kblock(step, state):
                acc, rem0 = state
                kj = last - step
                ks = pl.ds(pl.multiple_of(kj * n, n), n)
                z = _dot_nt(q, k16[ks, :])
                strict = (col + kj * n) < (row + qi * m)
                spz = _softplus(z)
                sp = jnp.where(strict, spz, 0.0)
                rem = rem0 + _split_dot(sp, after, 2)
                a = jnp.where(strict, jnp.exp(z - spz - rem), 0.0)
                acc = acc + _dot(a.astype(BF16), v16[ks, :])
                return acc, rem0 + jnp.sum(sp, axis=1, keepdims=True)

            acc, _ = lax.fori_loop(0, last + 1, kblock,
                                   (jnp.zeros((m, HEAD), F32), jnp.zeros((m, 1), F32)))
            o_ref[qs, :] = acc
            return carry

        lax.fori_loop(0, nq, qblock, 0)

    col_spec = lambda p: pl.BlockSpec((t, HEAD), lambda h: (0, p * nh + h))
    return pl.pallas_call(
        body, out_shape=jax.ShapeDtypeStruct((t, wd), F32), grid=(nh,),
        in_specs=[col_spec(0), col_spec(1), col_spec(2)],
        out_specs=pl.BlockSpec((t, HEAD), lambda h: (0, h)),
        scratch_shapes=[pltpu.VMEM((t, HEAD), BF16)] * 3,
        name=name, compiler_params=_params(("parallel",)),
    )(qkv, qkv, qkv)


def sba_bwd(qkv, o, do, name):
    t, w3 = qkv.shape
    wd = w3 // 3
    m, n = SB_QROWS, SB_BLOCK
    nh, nq, per = wd // HEAD, t // m, m // n
    scale = HEAD ** -0.5

    def body(q_ref, k_ref, v_ref, o_ref, do_ref, d_ref, dk_ref, dv_ref, q16, k16, v16):
        row, col, after, from_ = _sb_masks()
        dk_ref[...] = jnp.zeros_like(dk_ref)
        dv_ref[...] = jnp.zeros_like(dv_ref)
        q16[...] = (q_ref[...] * scale).astype(BF16)
        k16[...] = k_ref[...].astype(BF16)
        v16[...] = v_ref[...].astype(BF16)

        def qblock(qi, carry):
            qs = pl.ds(pl.multiple_of(qi * m, m), m)
            q = q16[qs, :]
            dov = do_ref[qs, :]
            do16 = dov.astype(BF16)
            dsum = jnp.sum(do16.astype(F32) * o_ref[qs, :], axis=1, keepdims=True)
            last = (qi + 1) * per - 1

            def kblock(step, state):
                dq, rem0, e0 = state
                kj = last - step
                ks = pl.ds(pl.multiple_of(kj * n, n), n)
                kv, vv = k16[ks, :], v16[ks, :]
                z = _dot_nt(q, kv)
                strict = (col + kj * n) < (row + qi * m)
                spz = _softplus(z)
                sp = jnp.where(strict, spz, 0.0)
                rem = rem0 + _split_dot(sp, after, 2)
                sgz = jnp.exp(z - spz)
                a = jnp.where(strict, sgz * jnp.exp(-rem), 0.0).astype(BF16)
                e = a.astype(F32) * _dot_nt(do16, vv)
                left = dsum - (e0 + _split_dot(e, from_, 2))
                dz = jnp.where(strict, e * (1.0 - sgz) - sgz * left, 0.0).astype(BF16)
                dq = dq + _dot(dz, kv)
                dk_ref[ks, :] += _dot_tn(dz, q)
                dv_ref[ks, :] += _dot_tn(a, do16)
                return (dq, rem0 + jnp.sum(sp, axis=1, keepdims=True),
                        e0 + jnp.sum(e, axis=1, keepdims=True))

            zero1 = jnp.zeros((m, 1), F32)
            dq, _, _ = lax.fori_loop(0, last + 1, kblock, (jnp.zeros((m, HEAD), F32), zero1, zero1))
            d_ref[0, qs, :] = (dq * scale).astype(BF16)
            return carry

        lax.fori_loop(0, nq, qblock, 0)
        d_ref[1, :, :] = dk_ref[...].astype(BF16)
        d_ref[2, :, :] = dv_ref[...].astype(BF16)

    col_spec = lambda p: pl.BlockSpec((t, HEAD), lambda h: (0, p * nh + h))
    head = pl.BlockSpec((t, HEAD), lambda h: (0, h))
    return pl.pallas_call(
        body, out_shape=jax.ShapeDtypeStruct((3, t, wd), BF16), grid=(nh,),
        in_specs=[col_spec(0), col_spec(1), col_spec(2), head, head],
        out_specs=pl.BlockSpec((3, t, HEAD), lambda h: (0, 0, h)),
        scratch_shapes=[pltpu.VMEM((t, HEAD), F32)] * 2 + [pltpu.VMEM((t, HEAD), BF16)] * 3,
        name=name, compiler_params=_params(("parallel",)),
    )(qkv, qkv, qkv, o, do)


def _pool_band(i_out, i_in, tr, win, transpose):
    r = lax.broadcasted_iota(jnp.int32, (tr, tr), 0) + i_out * tr
    c = lax.broadcasted_iota(jnp.int32, (tr, tr), 1) + i_in * tr
    if transpose:
        return ((r <= c) & (r > c - win)).astype(F32)
    return ((c <= r) & (c > r - win)).astype(F32)


def _pool_p(u_ref, i, tr, win):
    cur = u_ref[i * tr:(i + 1) * tr, :]
    ws = _dot(_pool_band(i, i, tr, win, False), cur, HI)
    if i > 0:
        ws = ws + _dot(_pool_band(i, i - 1, tr, win, False), u_ref[(i - 1) * tr:i * tr, :], HI)
    pos = lax.broadcasted_iota(jnp.int32, (tr, 1), 0) + (i * tr + 1)
    count = jnp.minimum(pos, win).astype(F32)
    return ws / count - cur, count


def pool_fwd(u, pool_w, pool_scale, name):
    t, d = u.shape
    ng = len(POOL_WINDOWS)
    gs = d // ng
    tr = min(ROW_TILE, t)

    def body(u_ref, w_ref, s_ref, y_ref):
        win = jnp.left_shift(2, pl.program_id(0))
        for i in range(t // tr):
            p, _ = _pool_p(u_ref, i, tr, win)
            y_ref[i * tr:(i + 1) * tr, :] = _dot(p.astype(BF16), w_ref[...]) * s_ref[...]

    grp = pl.BlockSpec((t, gs), lambda g: (0, g))
    return pl.pallas_call(
        body, out_shape=jax.ShapeDtypeStruct((t, d), F32), grid=(ng,),
        in_specs=[grp, pl.BlockSpec((None, gs, gs), lambda g: (g, 0, 0)), pl.BlockSpec((1, gs), lambda g: (0, g))],
        out_specs=grp, name=name, compiler_params=_params(("parallel",)),
    )(u, pool_w, pool_scale)


def pool_bwd(u, pool_w, pool_scale, dy, dw_buf, name):
    t, d = u.shape
    ng = len(POOL_WINDOWS)
    gs = d // ng
    tr = min(ROW_TILE, t)
    nt = t // tr

    def body(u_ref, w_ref, s_ref, dy_ref, buf_ref, du_ref, dw_ref, ds_ref, dpc_ref, dp_ref):
        del buf_ref
        win = jnp.left_shift(2, pl.program_id(0))
        wv = w_ref[...]
        dw = jnp.zeros((gs, gs), F32)
        dsc = jnp.zeros((1, gs), F32)
        for i in range(nt):
            rows = slice(i * tr, (i + 1) * tr)
            p, count = _pool_p(u_ref, i, tr, win)
            p16 = p.astype(BF16)
            dyv = dy_ref[rows, :]
            dsc = dsc + jnp.sum(dyv * _dot(p16, wv), axis=0, keepdims=True)
            dyp = (dyv * s_ref[...]).astype(BF16)
            dw = dw + _dot_tn(p16, dyp)
            dp = _dot_nt(dyp, wv)
            dp_ref[rows, :] = dp
            dpc_ref[rows, :] = dp / count
        dw_ref[...] = dw.astype(BF16)
        ds_ref[...] = dsc
        for i in range(nt):
            rows = slice(i * tr, (i + 1) * tr)
            acc = _dot(_pool_band(i, i, tr, win, True), dpc_ref[rows, :], HI)
            if i + 1 < nt:
                acc = acc + _dot(_pool_band(i, i + 1, tr, win, True), dpc_ref[(i + 1) * tr:(i + 2) * tr, :], HI)
            du_ref[rows, :] = acc - dp_ref[rows, :]

    grp = pl.BlockSpec((t, gs), lambda g: (0, g))
    wspec = pl.BlockSpec((None, gs, gs), lambda g: (g, 0, 0))
    vec = pl.BlockSpec((1, gs), lambda g: (0, g))
    return pl.pallas_call(
        body,
        out_shape=[jax.ShapeDtypeStruct((t, d), F32), jax.ShapeDtypeStruct(dw_buf.shape, BF16),
                   jax.ShapeDtypeStruct((1, d), F32)],
        grid=(ng,), in_specs=[grp, wspec, vec, grp, ANY], out_specs=[grp, wspec, vec],
        scratch_shapes=[pltpu.VMEM((t, gs), F32), pltpu.VMEM((t, gs), F32)],
        input_output_aliases={4: 1}, name=name, compiler_params=_params(("parallel",)),
    )(u, pool_w, pool_scale, dy, dw_buf)


CONV_COLS = 256
HALO = 8


def _conv_taps(ref, r0, tr):
    x = ref[r0:r0 + tr, :]
    prev = ref[r0 - HALO:r0, :] if r0 > 0 else jnp.zeros((HALO, x.shape[1]), F32)
    xx = jnp.concatenate([prev, x], axis=0)
    return x, pltpu.roll(xx, 1, 0)[HALO:, :], pltpu.roll(xx, 2, 0)[HALO:, :]


def _conv_out(taps, w_ref, b_ref):
    x, s1, s2 = taps
    return w_ref[0:1, :] * s2 + w_ref[1:2, :] * s1 + w_ref[2:3, :] * x + b_ref[...]


def conv_glu_fwd(up, conv_w, conv_b, name):
    t, f2 = up.shape
    f = f2 // 2
    tc = min(CONV_COLS, f)
    nj = f // tc
    tr = min(ROW_TILE, t)

    def body(ug_ref, uv_ref, wg_ref, wv_ref, bg_ref, bv_ref, o_ref):
        for i in range(t // tr):
            r0 = i * tr
            gate = _conv_out(_conv_taps(ug_ref, r0, tr), wg_ref, bg_ref)
            val = _conv_out(_conv_taps(uv_ref, r0, tr), wv_ref, bv_ref)
            o_ref[r0:r0 + tr, :] = (gate * _sigmoid(gate) * val).astype(BF16)

    blk = lambda rows, half: pl.BlockSpec((rows, tc), lambda j: (0, half * nj + j))
    return pl.pallas_call(
        body, out_shape=jax.ShapeDtypeStruct((t, f), BF16), grid=(nj,),
        in_specs=[blk(t, 0), blk(t, 1), blk(CONV_WIDTH, 0), blk(CONV_WIDTH, 1), blk(1, 0), blk(1, 1)],
        out_specs=pl.BlockSpec((t, tc), lambda j: (0, j)),
        name=name, compiler_params=_params(("parallel",)),
    )(up, up, conv_w, conv_w, conv_b, conv_b)


def conv_glu_bwd(up, conv_w, conv_b, dact, name):
    t, f2 = up.shape
    f = f2 // 2
    tc = min(CONV_COLS, f)
    nj = f // tc
    tr = min(ROW_TILE, t)
    nt = t // tr

    def body(ug_ref, uv_ref, wg_ref, wv_ref, bg_ref, bv_ref, da_ref, du_ref, dw_ref, db_ref, dg_ref, dv_ref):
        dwg = [jnp.zeros((1, tc), F32) for _ in range(CONV_WIDTH)]
        dwv = [jnp.zeros((1, tc), F32) for _ in range(CONV_WIDTH)]
        dbg = jnp.zeros((1, tc), F32)
        dbv = jnp.zeros((1, tc), F32)
        for i in range(nt):
            r0 = i * tr
            tg, tv = _conv_taps(ug_ref, r0, tr), _conv_taps(uv_ref, r0, tr)
            gate, val = _conv_out(tg, wg_ref, bg_ref), _conv_out(tv, wv_ref, bv_ref)
            sg = _sigmoid(gate)
            da = da_ref[r0:r0 + tr, :]
            d_gate = da * val * (sg * (1.0 + gate * (1.0 - sg)))
            d_val = da * (gate * sg)
            dg_ref[r0:r0 + tr, :] = d_gate
            dv_ref[r0:r0 + tr, :] = d_val
            dbg = dbg + jnp.sum(d_gate, axis=0, keepdims=True)
            dbv = dbv + jnp.sum(d_val, axis=0, keepdims=True)
            for tap in range(CONV_WIDTH):
                dwg[tap] = dwg[tap] + jnp.sum(d_gate * tg[2 - tap], axis=0, keepdims=True)
                dwv[tap] = dwv[tap] + jnp.sum(d_val * tv[2 - tap], axis=0, keepdims=True)
        for tap in range(CONV_WIDTH):
            dw_ref[0, tap:tap + 1, :] = dwg[tap]
            dw_ref[1, tap:tap + 1, :] = dwv[tap]
        db_ref[0, :, :] = dbg
        db_ref[1, :, :] = dbv
        for half, (d_ref, w_ref) in enumerate(((dg_ref, wg_ref), (dv_ref, wv_ref))):
            for i in range(nt):
                r0 = i * tr
                x = d_ref[r0:r0 + tr, :]
                nxt = d_ref[r0 + tr:r0 + tr + HALO, :] if i + 1 < nt else jnp.zeros((HALO, tc), F32)
                xx = jnp.concatenate([x, nxt], axis=0)
                up1 = pltpu.roll(xx, tr + HALO - 1, 0)[:tr, :]
                up2 = pltpu.roll(xx, tr + HALO - 2, 0)[:tr, :]
                du = w_ref[2:3, :] * x + w_ref[1:2, :] * up1 + w_ref[0:1, :] * up2
                du_ref[half, r0:r0 + tr, :] = du.astype(BF16)

    blk = lambda rows, half: pl.BlockSpec((rows, tc), lambda j: (0, half * nj + j))
    return pl.pallas_call(
        body,
        out_shape=[jax.ShapeDtypeStruct((2, t, f), BF16), jax.ShapeDtypeStruct((2, CONV_WIDTH, f), F32),
                   jax.ShapeDtypeStruct((2, 1, f), F32)],
        grid=(nj,),
        in_specs=[blk(t, 0), blk(t, 1), blk(CONV_WIDTH, 0), blk(CONV_WIDTH, 1), blk(1, 0), blk(1, 1),
                  pl.BlockSpec((t, tc), lambda j: (0, j))],
        out_specs=[pl.BlockSpec((2, t, tc), lambda j: (0, 0, j)),
                   pl.BlockSpec((2, CONV_WIDTH, tc), lambda j: (0, 0, j)),
                   pl.BlockSpec((2, 1, tc), lambda j: (0, 0, j))],
        scratch_shapes=[pltpu.VMEM((t, tc), F32), pltpu.VMEM((t, tc), F32)],
        name=name, compiler_params=_params(("parallel",)),
    )(up, up, conv_w, conv_w, conv_b, conv_b, dact)


def _place():
    x, y, c = lax.axis_index("x"), lax.axis_index("y"), lax.axis_index("c")
    others = [(1 - x, y), (x, 1 - y), (1 - x, 1 - y)]
    return x, y, c, others


def _window(ref, axis, b, n):
    if axis == 1:
        return ref.at[:, pl.ds(b * n, n), :]
    return ref.at[:, :, pl.ds(b * n, n)]


def _remote(src, dst, send_sems, recv_sems, k, to):
    return pltpu.make_async_remote_copy(src_ref=src, dst_ref=dst, send_sem=send_sems.at[k],
                                        recv_sem=recv_sems.at[k], device_id=to, device_id_type=MESH)


def gather_weights(shards, axes):
    na = len(shards)
    widths = [s.shape[ax] for s, ax in zip(shards, axes)]
    out_shape = []
    for s, ax in zip(shards, axes):
        shp = list(s.shape)
        shp[ax] *= N_DEV
        out_shape.append(jax.ShapeDtypeStruct(tuple(shp), s.dtype))

    def body(*refs):
        ins, outs = refs[:na], refs[na:2 * na]
        send_sems, recv_sems, local_sems = refs[2 * na:]
        x, y, c, others = _place()
        sibling = (x, y, 1 - c)

        def win(a, px, py, pc):
            return _window(outs[a], axes[a], 4 * px + 2 * py + pc, widths[a])

        def copy(a, k, block, to, src=None):
            w = win(a, *block)
            return _remote(w if src is None else src, w, send_sems, recv_sems, 7 * a + k, to)

        started = []
        for a in range(na):
            mine = pltpu.make_async_copy(ins[a], win(a, x, y, c), local_sems.at[a])
            mine.start()
            started.append(mine)
        sends = []
        for a in range(na):
            first = [copy(a, 0, (x, y, c), sibling, src=ins[a])]
            first += [copy(a, 1 + j, (x, y, c), (*chip, c), src=ins[a]) for j, chip in enumerate(others)]
            for cp in first:
                cp.start()
            sends += first
        for a in range(na):
            for j, chip in enumerate(others):
                copy(a, 1 + j, (*chip, c), (x, y, c)).wait_recv()
                fwd = copy(a, 4 + j, (*chip, c), sibling)
                fwd.start()
                sends.append(fwd)
        for a in range(na):
            copy(a, 0, (x, y, 1 - c), (x, y, c)).wait_recv()
            for j, chip in enumerate(others):
                copy(a, 4 + j, (*chip, 1 - c), (x, y, c)).wait_recv()
        for cp in sends:
            cp.wait_send()
        for cp in started:
            cp.wait()

    return pl.pallas_call(
        body, out_shape=out_shape, in_specs=[ANY] * na, out_specs=[ANY] * na,
        scratch_shapes=[pltpu.SemaphoreType.DMA((7 * na,)), pltpu.SemaphoreType.DMA((7 * na,)),
                        pltpu.SemaphoreType.DMA((na,))],
        name="gather_weights",
    )(*shards)


def sibling_exchange(grads, axes):
    na = len(grads)
    widths = [g.shape[ax] // N_DEV for g, ax in zip(grads, axes)]
    out_shape = []
    for g, ax, n in zip(grads, axes, widths):
        shp = list(g.shape)
        shp[ax] = n
        out_shape.append(jax.ShapeDtypeStruct((N_CHIP, *shp), g.dtype))

    def body(*refs):
        ins, outs = refs[:na], refs[na:2 * na]
        send_sems, recv_sems = refs[2 * na:]
        x, y, c, _ = _place()
        copies = []
        for a in range(na):
            for q in range(N_CHIP):
                src = _window(ins[a], axes[a], 2 * q + (1 - c), widths[a])
                cp = _remote(src, outs[a].at[q], send_sems, recv_sems, N_CHIP * a + q, (x, y, 1 - c))
                cp.start()
                copies.append(cp)
        for cp in copies:
            cp.wait_recv()
        for cp in copies:
            cp.wait_send()

    return pl.pallas_call(
        body, out_shape=out_shape, in_specs=[ANY] * na, out_specs=[ANY] * na,
        scratch_shapes=[pltpu.SemaphoreType.DMA((N_CHIP * na,)), pltpu.SemaphoreType.DMA((N_CHIP * na,))],
        name="sibling_exchange",
    )(*grads)


def chip_exchange(partials):
    na = len(partials)

    def body(*refs):
        ins, outs = refs[:na], refs[na:2 * na]
        send_sems, recv_sems, local_sems = refs[2 * na:]
        x, y, c, others = _place()
        me = 2 * x + y
        local, sends = [], []
        for a in range(na):
            cp = pltpu.make_async_copy(ins[a].at[me], outs[a].at[me], local_sems.at[a])
            cp.start()
            local.append(cp)
            for j, (px, py) in enumerate(others):
                cp = _remote(ins[a].at[2 * px + py], outs[a].at[me], send_sems, recv_sems, 3 * a + j, (px, py, c))
                cp.start()
                sends.append(cp)
        for a in range(na):
            for j, (px, py) in enumerate(others):
                slot = outs[a].at[2 * px + py]
                _remote(slot, slot, send_sems, recv_sems, 3 * a + j, (px, py, c)).wait_recv()
        for cp in sends:
            cp.wait_send()
        for cp in local:
            cp.wait()

    return pl.pallas_call(
        body, out_shape=[jax.ShapeDtypeStruct(p.shape, p.dtype) for p in partials],
        in_specs=[ANY] * na, out_specs=[ANY] * na,
        scratch_shapes=[pltpu.SemaphoreType.DMA((3 * na,)), pltpu.SemaphoreType.DMA((3 * na,)),
                        pltpu.SemaphoreType.DMA((na,))],
        name="chip_exchange",
    )(*partials)


def _peer_of(k, x, y, c):
    return (1 - x if k & 4 else x, 1 - y if k & 2 else y, 1 - c if k & 1 else c)


def small_exchange(vec, reduce, name):
    r = vec.shape[0]

    def body(v_ref, o_ref, *scratch):
        if reduce:
            buf, send_sems, recv_sems = scratch
        else:
            buf, (send_sems, recv_sems) = o_ref, scratch
        x, y, c, _ = _place()
        me = 4 * x + 2 * y + c
        copies = []
        for k in range(1, N_DEV):
            cp = _remote(v_ref, buf.at[me], send_sems, recv_sems, k - 1, _peer_of(k, x, y, c))
            cp.start()
            copies.append(cp)
        buf[me] = v_ref[...]
        for k in range(1, N_DEV):
            px, py, pc = _peer_of(k, x, y, c)
            slot = buf.at[4 * px + 2 * py + pc]
            _remote(slot, slot, send_sems, recv_sems, k - 1, (px, py, pc)).wait_recv()
        for cp in copies:
            cp.wait_send()
        if reduce:
            tot = buf[0]
            for b in range(1, N_DEV):
                tot = tot + buf[b]
            o_ref[...] = tot

    sems = [pltpu.SemaphoreType.DMA((N_DEV - 1,)), pltpu.SemaphoreType.DMA((N_DEV - 1,))]
    if reduce:
        out_shape = jax.ShapeDtypeStruct((r, LANES), F32)
        scratch = [pltpu.VMEM((N_DEV, r, LANES), F32)] + sems
    else:
        out_shape = jax.ShapeDtypeStruct((N_DEV, r, LANES), F32)
        scratch = sems
    return pl.pallas_call(
        body, out_shape=out_shape, in_specs=[VMEM_FULL], out_specs=VMEM_FULL,
        scratch_shapes=scratch, name=name, compiler_params=_params(None),
    )(vec)


def _row_tile(rows, cols):
    best = None
    for tb in range(16, rows + 1, 16):
        if rows % tb == 0 and tb * cols <= ELEMWISE_BLOCK_ELEMS:
            best = tb
    return best if best is not None else rows


def add_sibling(grad, recv, axis, core):
    nl = grad.shape[0]
    _, _, r, cc = recv.shape
    tb = _row_tile(r, cc)
    per = r // tb

    def body(c_ref, g_ref, r_ref, o_ref):
        del c_ref
        o_ref[...] = (g_ref[...].astype(F32) + r_ref[...].astype(F32)).astype(BF16)

    if axis == 2:
        g_spec = pl.BlockSpec((None, tb, cc), lambda q, l, i, c_ref: (l, i, 2 * q + c_ref[0]))
    else:
        g_spec = pl.BlockSpec((None, tb, cc), lambda q, l, i, c_ref: (l, (2 * q + c_ref[0]) * per + i, 0))
    slot = pl.BlockSpec((None, None, tb, cc), lambda q, l, i, c_ref: (q, l, i, 0))
    return pl.pallas_call(
        body, out_shape=jax.ShapeDtypeStruct(recv.shape, BF16),
        grid_spec=pltpu.PrefetchScalarGridSpec(
            num_scalar_prefetch=1, grid=(N_CHIP, nl, per), in_specs=[g_spec, slot], out_specs=slot),
        name="add_sibling", compiler_params=_params(("parallel", "parallel", "parallel")),
    )(core, grad, recv)


def _adamw(w, g, m, v):
    m = ADAM_B1 * m + (1.0 - ADAM_B1) * g
    v = ADAM_B2 * v + (1.0 - ADAM_B2) * (g * g)
    m_hat = m / (1.0 - ADAM_B1 ** ADAM_STEP)
    v_hat = v / (1.0 - ADAM_B2 ** ADAM_STEP)
    delta = -ADAM_LR * (m_hat / (jnp.sqrt(v_hat) + ADAM_EPS) + ADAM_WD * w)
    return delta, m, v


def adam_from_partials(recv, w, m, v):
    nl, r, cc = w.shape
    tb = _row_tile(r, cc)

    def body(p0, p1, p2, p3, w_ref, m_ref, v_ref, g_out, d_out, m_out, v_out):
        g = p0[...].astype(F32) + p1[...].astype(F32) + p2[...].astype(F32) + p3[...].astype(F32)
        d, mn, vn = _adamw(w_ref[...], g, m_ref[...], v_ref[...])
        g_out[...], d_out[...], m_out[...], v_out[...] = g, d, mn, vn

    slot = lambda q: pl.BlockSpec((None, None, tb, cc), lambda l, i: (q, l, i, 0))
    blk = pl.BlockSpec((None, tb, cc), lambda l, i: (l, i, 0))
    shp = jax.ShapeDtypeStruct(w.shape, F32)
    return pl.pallas_call(
        body, out_shape=[shp] * 4, grid=(nl, r // tb),
        in_specs=[slot(0), slot(1), slot(2), slot(3), blk, blk, blk], out_specs=[blk] * 4,
        name="adam_big", compiler_params=_params(("parallel", "parallel")),
    )(recv, recv, recv, recv, w, m, v)


def adam_small(w, g, m, v):
    def body(w_ref, g_ref, m_ref, v_ref, d_out, m_out, v_out):
        d_out[...], m_out[...], v_out[...] = _adamw(w_ref[...], g_ref[...], m_ref[...], v_ref[...])

    shp = jax.ShapeDtypeStruct(w.shape, F32)
    return pl.pallas_call(body, out_shape=[shp] * 3, name="adam_small")(w, g, m, v)


def _pack(arrays, multiple=8 * LANES):
    flat = jnp.concatenate([a.reshape(-1) for a in arrays])
    pad = (-flat.shape[0]) % multiple
    if pad:
        flat = jnp.concatenate([flat, jnp.zeros((pad,), flat.dtype)])
    return flat.reshape(-1, LANES)


def _unpack(packed, shapes):
    flat = packed.reshape(packed.shape[:-2] + (-1,))
    out, off = [], 0
    for shp in shapes:
        n = math.prod(shp)
        out.append(flat[..., off:off + n].reshape(packed.shape[:-2] + tuple(shp)))
        off += n
    return out


def _unshard_last(stacked):
    moved = jnp.moveaxis(stacked, 0, -2)
    return moved.reshape(moved.shape[:-2] + (-1,))


def _shard_last(full, block):
    n = full.shape[-1] // N_DEV
    return lax.dynamic_slice_in_dim(full, block * n, n, axis=full.ndim - 1)


def local_step(x, target, norm_g, lb_logits, onorm_g, pool_scale, conv_w, conv_b,
               w_in, w_out, w_qkv, w_so, w_pool, w_up, w_down):
    t, d = x.shape
    depth = norm_g.shape[0]
    ng = lambda i, j: norm_g[i, j].reshape(1, d)
    lbs = lower_bounds_fwd(lb_logits)
    saved = []
    h = x
    _, u16, u32 = res_norm(h, None, None, ng(0, 0), "norm_in")
    for i in range(depth):
        kind, j = i % 3, i // 3
        s = {"h_in": h, "u1": u16}
        if kind == 0:
            s["proj"] = matmul(u16, Layer(w_in, j), "nn", F32, f"hg_in_{i}")
            s["y"] = hgrn_fwd(s["proj"], lbs[i].reshape(1, -1), onorm_g[j].reshape(1, -1), i > 0, f"hgrn_fwd_{i}")
            mix = matmul(s["y"], Layer(w_out, j), "nn", F32, f"hg_out_{i}")
        elif kind == 1:
            s["qkv"] = matmul(u16, Layer(w_qkv, j), "nn", F32, f"sb_qkv_{i}")
            s["o"] = sba_fwd(s["qkv"], f"sba_fwd_{i}")
            mix = matmul(s["o"], Layer(w_so, j), "nn", F32, f"sb_out_{i}")
        else:
            s["u1f"] = u32
            mix = pool_fwd(u32, w_pool[4 * j:4 * j + 4], pool_scale[j].reshape(1, d), f"pool_fwd_{i}")
        s["mix"] = mix
        h_mid, u2, _ = res_norm(h, mix, ng(i, 1), ng(i, 2), f"norm_mid_{i}")
        s["h_mid"], s["u2"] = h_mid, u2
        s["up"] = matmul(u2, Layer(w_up, i), "nn", F32, f"ffn_up_{i}")
        s["act"] = conv_glu_fwd(s["up"], conv_w[i], conv_b[i].reshape(1, -1), f"glu_fwd_{i}")
        s["f"] = matmul(s["act"], Layer(w_down, i), "nn", F32, f"ffn_down_{i}")
        nxt = ng(i + 1, 0) if i + 1 < depth else None
        h, u16, u32 = res_norm(h_mid, s["f"], ng(i, 3), nxt, f"norm_out_{i}",
                               want_f32=(nxt is not None and (i + 1) % 3 == 2))
        saved.append(s)

    loss_acc, dh = loss_head(h, target, "loss_head")

    d_norm = [[None] * 4 for _ in range(depth)]
    d_lbs = jnp.zeros_like(lbs)
    d_onorm = [None] * onorm_g.shape[0]
    d_pscale = [None] * pool_scale.shape[0]
    d_cw, d_cb = [None] * depth, [None] * depth
    g_in, g_out = lax.empty(w_in.shape, BF16), lax.empty(w_out.shape, BF16)
    g_qkv, g_so = lax.empty(w_qkv.shape, BF16), lax.empty(w_so.shape, BF16)
    g_pool = lax.empty(w_pool.shape, BF16)
    g_up, g_down = lax.empty(w_up.shape, BF16), lax.empty(w_down.shape, BF16)

    for i in reversed(range(depth)):
        kind, j = i % 3, i // 3
        s = saved[i]
        df, d_norm[i][3] = norm_bwd(s["f"], ng(i, 3), dh, None, BF16, f"nb_out_{i}")
        dact = matmul(df, Layer(w_down, i), "nt", F32, f"d_act_{i}")
        g_down = matmul(s["act"], df, "tn", BF16, f"dw_down_{i}", out_stack=(g_down, i))
        dup, dcw, dcb = conv_glu_bwd(s["up"], conv_w[i], conv_b[i].reshape(1, -1), dact, f"glu_bwd_{i}")
        d_cw[i] = jnp.moveaxis(dcw, 0, 1).reshape(CONV_WIDTH, -1)
        d_cb[i] = dcb.reshape(-1)
        du2 = matmul(Split(dup), Layer(w_up, i), "nt", F32, f"d_u2_{i}")
        g_up = matmul(s["u2"], Split(dup), "tn", BF16, f"dw_up_{i}", out_stack=(g_up, i))
        dh_mid, d_norm[i][2] = norm_bwd(s["h_mid"], ng(i, 2), du2, dh, F32, f"nb_mid_{i}")
        dm, d_norm[i][1] = norm_bwd(s["mix"], ng(i, 1), dh_mid, None, F32 if kind == 2 else BF16, f"nb_mix_{i}")
        if kind == 0:
            dy = matmul(dm, Layer(w_out, j), "nt", F32, f"d_y_{i}")
            g_out = matmul(s["y"], dm, "tn", BF16, f"dw_hgout_{i}", out_stack=(g_out, j))
            dproj, d_onorm[j], dlb = hgrn_bwd(s["proj"], lbs[i].reshape(1, -1), onorm_g[j].reshape(1, -1), dy,
                                              i > 0, f"hgrn_bwd_{i}")
            d_lbs = d_lbs.at[i].set(dlb[0])
            du1 = matmul(Split(dproj), Layer(w_in, j), "nt", F32, f"d_u1_{i}")
            g_in = matmul(s["u1"], Split(dproj), "tn", BF16, f"dw_hgin_{i}", out_stack=(g_in, j))
        elif kind == 1:
            do = matmul(dm, Layer(w_so, j), "nt", F32, f"d_o_{i}")
            g_so = matmul(s["o"], dm, "tn", BF16, f"dw_sbout_{i}", out_stack=(g_so, j))
            dqkv = sba_bwd(s["qkv"], s["o"], do, f"sba_bwd_{i}")
            du1 = matmul(Split(dqkv), Layer(w_qkv, j), "nt", F32, f"d_u1_{i}")
            g_qkv = matmul(s["u1"], Split(dqkv), "tn", BF16, f"dw_sbqkv_{i}", out_stack=(g_qkv, j))
        else:
            du1, g_pool, d_pscale[j] = pool_bwd(s["u1f"], w_pool[4 * j:4 * j + 4], pool_scale[j].reshape(1, d),
                                                dm, g_pool, f"pool_bwd_{i}")
        dh, d_norm[i][0] = norm_bwd(s["h_in"], ng(i, 0), du1, dh_mid, F32, f"nb_in_{i}")

    small = {
        "norm_g": jnp.stack([jnp.stack([v.reshape(d) for v in row]) for row in d_norm]),
        "lb_logits": lower_bounds_bwd(lb_logits, d_lbs),
        "onorm_g": jnp.stack([v.reshape(-1) for v in d_onorm]),
        "pool_scale": jnp.stack([v.reshape(-1) for v in d_pscale]),
        "conv_w": jnp.stack(d_cw),
        "conv_b": jnp.stack(d_cb),
    }
    big = [g_in, g_out, g_qkv, g_so, g_pool, g_up, g_down]
    return loss_acc, dh, small, big


BIG_AXES = (2, 1, 2, 1, 1, 2, 1)
SMALL_SHARDED = ("norm_g", "onorm_g", "pool_scale", "conv_w")
SMALL_ORDER = ("norm_g", "lb_logits", "onorm_g", "pool_scale", "conv_w", "conv_b")


def kernel(x, norm_g, hgrn_lb_logits, hgrn_w_in, hgrn_onorm_g, hgrn_w_out, sba_w_qkv, sba_w_out, pool_w, pool_scale, ffn_w_up, ffn_conv_w, ffn_conv_b, ffn_w_down, loss_target, m_norm_g, m_hgrn_lb_logits, m_hgrn_w_in, m_hgrn_onorm_g, m_hgrn_w_out, m_sba_w_qkv, m_sba_w_out, m_pool_w, m_pool_scale, m_ffn_w_up, m_ffn_conv_w, m_ffn_conv_b, m_ffn_w_down, v_norm_g, v_hgrn_lb_logits, v_hgrn_w_in, v_hgrn_onorm_g, v_hgrn_w_out, v_sba_w_qkv, v_sba_w_out, v_pool_w, v_pool_scale, v_ffn_w_up, v_ffn_conv_w, v_ffn_conv_b, v_ffn_w_down):
    cx, cy, cc = lax.axis_index("x"), lax.axis_index("y"), lax.axis_index("c")
    block = 4 * cx + 2 * cy + cc
    core = cc.astype(jnp.int32).reshape(1)

    pool3 = lambda a: a.reshape(a.shape[0] * a.shape[1], a.shape[2], a.shape[3])
    big_w = [hgrn_w_in, hgrn_w_out, sba_w_qkv, sba_w_out, pool3(pool_w), ffn_w_up, ffn_w_down]
    big_m = [m_hgrn_w_in, m_hgrn_w_out, m_sba_w_qkv, m_sba_w_out, pool3(m_pool_w), m_ffn_w_up, m_ffn_w_down]
    big_v = [v_hgrn_w_in, v_hgrn_w_out, v_sba_w_qkv, v_sba_w_out, pool3(v_pool_w), v_ffn_w_up, v_ffn_w_down]

    sharded = {"norm_g": norm_g, "onorm_g": hgrn_onorm_g, "pool_scale": pool_scale, "conv_w": ffn_conv_w}
    gathered = small_exchange(_pack([sharded[n] for n in SMALL_SHARDED]), False, "gather_small")
    parts = _unpack(gathered, [sharded[n].shape for n in SMALL_SHARDED])
    full = {n: _unshard_last(p) for n, p in zip(SMALL_SHARDED, parts)}

    full_w = gather_weights([w.astype(BF16) for w in big_w], BIG_AXES)

    loss_acc, grad_x, small_g, big_g = local_step(
        x[0], loss_target[0], full["norm_g"], hgrn_lb_logits, full["onorm_g"], full["pool_scale"],
        full["conv_w"], ffn_conv_b, *full_w)
    loss = lax.psum(loss_acc[0, 0], ("x", "y", "c"))

    shapes = [small_g[n].shape for n in SMALL_ORDER]
    summed = _unpack(small_exchange(_pack([small_g[n] for n in SMALL_ORDER]), True, "reduce_small"), shapes)
    sg = {n: (_shard_last(g, block) if n in SMALL_SHARDED else g) for n, g in zip(SMALL_ORDER, summed)}
    sw = {"norm_g": norm_g, "lb_logits": hgrn_lb_logits, "onorm_g": hgrn_onorm_g, "pool_scale": pool_scale,
          "conv_w": ffn_conv_w, "conv_b": ffn_conv_b}
    sm = {"norm_g": m_norm_g, "lb_logits": m_hgrn_lb_logits, "onorm_g": m_hgrn_onorm_g, "pool_scale": m_pool_scale,
          "conv_w": m_ffn_conv_w, "conv_b": m_ffn_conv_b}
    sv = {"norm_g": v_norm_g, "lb_logits": v_hgrn_lb_logits, "onorm_g": v_hgrn_onorm_g, "pool_scale": v_pool_scale,
          "conv_w": v_ffn_conv_w, "conv_b": v_ffn_conv_b}
    sshapes = [sw[n].shape for n in SMALL_ORDER]
    packed = [_pack([dct[n] for n in SMALL_ORDER]) for dct in (sw, sg, sm, sv)]
    s_delta, s_m, s_v = [dict(zip(SMALL_ORDER, _unpack(p, sshapes))) for p in adam_small(*packed)]

    from_sibling = sibling_exchange(big_g, BIG_AXES)
    partials = [add_sibling(g, r, ax, core) for g, r, ax in zip(big_g, from_sibling, BIG_AXES)]
    from_chips = chip_exchange(partials)
    upd = [adam_from_partials(r, w, m, v) for r, w, m, v in zip(from_chips, big_w, big_m, big_v)]
    b_grad, b_delta, b_m, b_v = [[u[k] for u in upd] for k in range(4)]

    def tree(small, bigs):
        bg = list(bigs)
        bg[4] = bg[4].reshape(pool_w.shape)
        return (small["norm_g"], small["lb_logits"], bg[0], small["onorm_g"], bg[1], bg[2], bg[3], bg[4],
                small["pool_scale"], bg[5], small["conv_w"], small["conv_b"], bg[6])

    return (loss, grad_x[None], *tree(sg, b_grad), *tree(s_delta, b_delta), *tree(s_m, b_m), *tree(s_v, b_v))
```

```python
import functools
import math

import jax
import jax.numpy as jnp
from jax import lax
from jax.experimental import pallas as pl
from jax.experimental.pallas import tpu as pltpu

F32 = jnp.float32
BF16 = jnp.bfloat16
HI = lax.Precision.HIGHEST
MESH = pl.DeviceIdType.MESH
ANY = pl.BlockSpec(memory_space=pl.ANY)
VMEM_FULL = pl.BlockSpec(memory_space=pltpu.VMEM)

NORM_EPS = 1e-6
HEAD = 128
HG_CHUNK = 128
HG_SUB = 16
SB_BLOCK = 128
SB_QROWS = 256
POOL_WINDOWS = (2, 4, 8, 16)
CONV_WIDTH = 3
ROW_TILE = 256
N_DEV = 8
N_CHIP = 4

ADAM_LR = 0.001
ADAM_B1 = 0.9
ADAM_B2 = 0.999
ADAM_EPS = 1e-08
ADAM_WD = 0.01
ADAM_STEP = 10

VMEM_LIMIT = 48 * 1024 * 1024
LANES = 128
ELEMWISE_BLOCK_ELEMS = 256 * 1024


def _params(sem=None, vmem=VMEM_LIMIT):
    return pltpu.CompilerParams(dimension_semantics=sem, vmem_limit_bytes=vmem)


def _tile(n, prefs=(1024, 512, 256, 128)):
    for p in prefs:
        if n % p == 0:
            return p
    return n


def _dot(a, b, prec=None):
    return jnp.dot(a, b, precision=prec, preferred_element_type=F32)


def _dot_nt(a, b, prec=None):
    return lax.dot_general(a, b, (((1,), (1,)), ((), ())), precision=prec, preferred_element_type=F32)


def _dot_tn(a, b, prec=None):
    return lax.dot_general(a, b, (((0,), (0,)), ((), ())), precision=prec, preferred_element_type=F32)


def _split_dot(x, tri, parts, left=False):
    tot, rest = None, x
    for p in range(parts):
        h = rest.astype(BF16)
        d = _dot(tri, h) if left else _dot(h, tri)
        tot = d if tot is None else tot + d
        if p + 1 < parts:
            rest = rest - h.astype(F32)
    return tot


def _dot1(a, b, fn):
    return fn(a.astype(BF16), b.astype(BF16))


def _dot3(a, b, fn):
    ah, bh = a.astype(BF16), b.astype(BF16)
    al, bl = (a - ah.astype(F32)).astype(BF16), (b - bh.astype(F32)).astype(BF16)
    return fn(ah, bh) + fn(ah, bl) + fn(al, bh)


def _sigmoid(x):
    return jax.nn.sigmoid(x)


def _softplus(x):
    return jnp.maximum(x, 0.0) + jnp.log1p(jnp.exp(-jnp.abs(x)))


class Layer:
    def __init__(self, arr, l):
        self.arr, self.l = arr, l
        self.shape = arr.shape[1:]
        self.part = None


class Split:
    def __init__(self, arr):
        self.arr = arr
        self.shape = (arr.shape[1], arr.shape[0] * arr.shape[2])
        self.part = arr.shape[2]


class Plain:
    def __init__(self, arr):
        self.arr = arr
        self.shape = arr.shape
        self.part = None


def _wrap(op):
    return op if isinstance(op, (Layer, Split, Plain)) else Plain(op)


def _op_spec(op, br, bc, rc_of_grid):
    if isinstance(op, Layer):
        l = op.l
        return pl.BlockSpec((None, br, bc), lambda i, j, k: (l, *rc_of_grid(i, j, k)))
    if isinstance(op, Split):
        per = op.part // bc

        def imap(i, j, k):
            r, c = rc_of_grid(i, j, k)
            return (lax.div(c, per), r, lax.rem(c, per))
        return pl.BlockSpec((None, br, bc), imap)
    return pl.BlockSpec((br, bc), rc_of_grid)


def _hosted(body, n_in, n_out, plan, step_of_grid):
    if plan is None:
        return body
    pi, po, ps = len(plan.args), len(plan.out_shape), len(plan.scratch)

    def wrapped(*refs):
        refs = list(refs)
        ins, pins = refs[:n_in], refs[n_in:n_in + pi]
        outs = refs[n_in + pi:n_in + pi + n_out]
        pouts = refs[n_in + pi + n_out:n_in + pi + n_out + po]
        scr, pscr = refs[n_in + pi + n_out + po:len(refs) - ps], refs[len(refs) - ps:]
        step, nsteps = step_of_grid()

        @pl.when(step == 0)
        def _():
            plan.start(pins, pouts, pscr)

        if plan.has_mid:
            @pl.when(step == min((3 * nsteps) // 4, nsteps - 1))
            def _():
                plan.mid(pins, pouts, pscr)

        body(*ins, *outs, *scr)

        @pl.when(step == nsteps - 1)
        def _():
            plan.finish(pins, pouts, pscr)

    return wrapped


def _host_call(body, n_in, plan, step_of_grid, *, out_shape, grid, in_specs, out_specs, scratch_shapes, name, sem, args):
    single = not isinstance(out_shape, (list, tuple))
    out_shape = [out_shape] if single else list(out_shape)
    out_specs = [out_specs] if single else list(out_specs)
    n_out = len(out_shape)
    in_specs, scratch_shapes, args = list(in_specs), list(scratch_shapes), list(args)
    if plan is not None:
        in_specs += [ANY] * len(plan.args)
        args += plan.args
        out_shape += plan.out_shape
        out_specs += [ANY] * len(plan.out_shape)
        scratch_shapes += plan.scratch
        sem = ("arbitrary",) * len(grid)
    outs = pl.pallas_call(
        _hosted(body, n_in, n_out, plan, step_of_grid), out_shape=out_shape, grid=grid,
        in_specs=in_specs, out_specs=out_specs, scratch_shapes=scratch_shapes,
        name=name, compiler_params=_params(sem),
    )(*args)
    host = outs[0] if single else list(outs[:n_out])
    return host, list(outs[n_out:])


def matmul(a, b, mode, out_dtype, name, plan=None):
    a, b = _wrap(a), _wrap(b)
    if mode == "nn":
        (m, kd), (kd2, n) = a.shape, b.shape
    elif mode == "nt":
        (m, kd), (n, kd2) = a.shape, b.shape
    else:
        (kd, m), (kd2, n) = a.shape, b.shape
    assert kd == kd2, (mode, a.shape, b.shape)

    def dim_tile(full, *ops_on_cols):
        base = full
        for op in ops_on_cols:
            if op.part is not None:
                base = math.gcd(base, op.part)
        return _tile(base)

    tm = dim_tile(m, *([a] if mode == "tn" else []))
    tn = dim_tile(n, *([b] if mode in ("nn", "tn") else []))
    tk = dim_tile(kd, *(([a] if mode in ("nn", "nt") else []) + ([b] if mode == "nt" else [])))
    tk = min(tk, 512)
    nk = kd // tk

    if mode == "nn":
        a_spec = _op_spec(a, tm, tk, lambda i, j, k: (i, k))
        b_spec = _op_spec(b, tk, tn, lambda i, j, k: (k, j))
        dot = _dot
    elif mode == "nt":
        a_spec = _op_spec(a, tm, tk, lambda i, j, k: (i, k))
        b_spec = _op_spec(b, tn, tk, lambda i, j, k: (j, k))
        dot = _dot_nt
    else:
        a_spec = _op_spec(a, tk, tm, lambda i, j, k: (k, i))
        b_spec = _op_spec(b, tk, tn, lambda i, j, k: (k, j))
        dot = _dot_tn

    def body(a_ref, b_ref, o_ref, acc_ref):
        k = pl.program_id(2)

        @pl.when(k == 0)
        def _():
            acc_ref[...] = jnp.zeros_like(acc_ref)

        acc_ref[...] += dot(a_ref[...].astype(BF16), b_ref[...].astype(BF16))

        @pl.when(k == nk - 1)
        def _():
            o_ref[...] = acc_ref[...].astype(o_ref.dtype)

    gi, gj = m // tm, n // tn

    def step_of_grid():
        return (pl.program_id(0) * gj + pl.program_id(1)) * nk + pl.program_id(2), gi * gj * nk

    out, extra = _host_call(
        body, 2, plan, step_of_grid, out_shape=jax.ShapeDtypeStruct((m, n), out_dtype), grid=(gi, gj, nk),
        in_specs=[a_spec, b_spec], out_specs=pl.BlockSpec((tm, tn), lambda i, j, k: (i, j)),
        scratch_shapes=[pltpu.VMEM((tm, tn), F32)], name=name,
        sem=("parallel", "parallel", "arbitrary"), args=[a.arr, b.arr])
    return out if plan is None else (out, extra)


def _rms(x, g):
    r = lax.rsqrt(jnp.mean(x * x, axis=-1, keepdims=True) + NORM_EPS)
    return x * r * g


def res_norm(h, m, g_a, g_b, name, want_f32=False):
    t, d = h.shape
    tr = min(ROW_TILE, t)
    has_m, has_u = m is not None, g_b is not None
    row = pl.BlockSpec((tr, d), lambda i: (i, 0))
    vec = pl.BlockSpec((1, d), lambda i: (0, 0))

    def body(*refs):
        refs = list(refs)
        h_ref = refs.pop(0)
        hn = h_ref[...]
        if has_m:
            m_ref, ga_ref = refs.pop(0), refs.pop(0)
            hn = hn + _rms(m_ref[...], ga_ref[...])
        if has_u:
            gb_ref = refs.pop(0)
        if has_m:
            refs.pop(0)[...] = hn
        if has_u:
            u = _rms(hn, gb_ref[...])
            refs.pop(0)[...] = u.astype(BF16)
            if want_f32:
                refs.pop(0)[...] = u

    args, in_specs, out_shape, out_specs = [h], [row], [], []
    if has_m:
        args += [m, g_a]
        in_specs += [row, vec]
        out_shape.append(jax.ShapeDtypeStruct((t, d), F32))
        out_specs.append(row)
    if has_u:
        args.append(g_b)
        in_specs.append(vec)
        out_shape.append(jax.ShapeDtypeStruct((t, d), BF16))
        out_specs.append(row)
        if want_f32:
            out_shape.append(jax.ShapeDtypeStruct((t, d), F32))
            out_specs.append(row)
    outs = list(pl.pallas_call(
        body, out_shape=out_shape, grid=(t // tr,), in_specs=in_specs, out_specs=out_specs,
        name=name, compiler_params=_params(("parallel",)),
    )(*args))
    h_new = outs.pop(0) if has_m else None
    u16 = outs.pop(0) if has_u else None
    u32 = outs.pop(0) if (has_u and want_f32) else None
    return h_new, u16, u32


def norm_bwd(x, g, dy, add, out_dtype, name):
    t, d = x.shape
    tr = min(ROW_TILE, t)
    has_add = add is not None
    row = pl.BlockSpec((tr, d), lambda i: (i, 0))
    vec = pl.BlockSpec((1, d), lambda i: (0, 0))

    def body(*refs):
        if has_add:
            x_ref, g_ref, dy_ref, add_ref, dx_ref, dg_ref = refs
        else:
            x_ref, g_ref, dy_ref, dx_ref, dg_ref = refs
        xv = x_ref[...]
        dyv = dy_ref[...].astype(F32)
        r = lax.rsqrt(jnp.mean(xv * xv, axis=-1, keepdims=True) + NORM_EPS)
        gy = dyv * g_ref[...]
        dx = r * gy - xv * (r * r * r * jnp.mean(gy * xv, axis=-1, keepdims=True))
        if has_add:
            dx = dx + add_ref[...]
        dx_ref[...] = dx.astype(dx_ref.dtype)

        @pl.when(pl.program_id(0) == 0)
        def _():
            dg_ref[...] = jnp.zeros_like(dg_ref)

        dg_ref[...] += jnp.sum(dyv * xv * r, axis=0, keepdims=True)

    args = [x, g, dy] + ([add] if has_add else [])
    in_specs = [row, vec, row] + ([row] if has_add else [])
    return pl.pallas_call(
        body, out_shape=[jax.ShapeDtypeStruct((t, d), out_dtype), jax.ShapeDtypeStruct((1, d), F32)],
        grid=(t // tr,), in_specs=in_specs, out_specs=[row, vec],
        name=name, compiler_params=_params(("arbitrary",)),
    )(*args)


def loss_head(y, target, name):
    t, d = y.shape
    tr = min(ROW_TILE, t)
    row = pl.BlockSpec((tr, d), lambda i: (i, 0))
    acc = pl.BlockSpec((8, LANES), lambda i: (0, 0))

    def body(y_ref, t_ref, loss_ref, dy_ref):
        e = y_ref[...] - t_ref[...]
        dy_ref[...] = e * (1.0 / d)

        @pl.when(pl.program_id(0) == 0)
        def _():
            loss_ref[...] = jnp.zeros_like(loss_ref)

        loss_ref[...] += jnp.sum(e * e) * (0.5 / d)

    return pl.pallas_call(
        body, out_shape=[jax.ShapeDtypeStruct((8, LANES), F32), jax.ShapeDtypeStruct((t, d), F32)],
        grid=(t // tr,), in_specs=[row, row], out_specs=[acc, row],
        name=name, compiler_params=_params(("arbitrary",)),
    )(y, target)


def _depth_softmax(ref, depth):
    rows = [ref[i:i + 1, :] for i in range(depth)]
    mx = functools.reduce(jnp.maximum, rows)
    ex = [jnp.exp(r - mx) for r in rows]
    tot = functools.reduce(lambda p, q: p + q, ex)
    return [e / tot for e in ex]


def lower_bounds_fwd(logits):
    depth, kw = logits.shape

    def body(l_ref, o_ref):
        s = _depth_softmax(l_ref, depth)
        run = jnp.zeros_like(s[0])
        o_ref[0:1, :] = run
        for i in range(1, depth):
            run = run + s[i]
            o_ref[i:i + 1, :] = run

    return pl.pallas_call(body, out_shape=jax.ShapeDtypeStruct((depth, kw), F32), name="lb_fwd")(logits)


def lower_bounds_bwd(logits, dlb):
    depth, kw = logits.shape

    def body(l_ref, d_ref, o_ref):
        s = _depth_softmax(l_ref, depth)
        ds = [jnp.zeros_like(s[0]) for _ in range(depth)]
        run = jnp.zeros_like(s[0])
        for j in range(depth - 1, 0, -1):
            run = run + d_ref[j:j + 1, :]
            ds[j] = run
        dot = functools.reduce(lambda p, q: p + q, [s[j] * ds[j] for j in range(depth)])
        for j in range(depth):
            o_ref[j:j + 1, :] = s[j] * (ds[j] - dot)

    return pl.pallas_call(body, out_shape=jax.ShapeDtypeStruct((depth, kw), F32), name="lb_bwd")(logits, dlb)


def _hg_gates(qp, fp, lb_row, has_lb):
    sig = _sigmoid(fp)
    nsig = _sigmoid(-fp)
    ls = jnp.minimum(fp, 0.0) - jnp.log1p(jnp.exp(-jnp.abs(fp)))
    if has_lb:
        a = jnp.log(lb_row)
        bb = jnp.log1p(-lb_row) + ls
        g = jnp.maximum(a, bb) + jnp.log1p(jnp.exp(-jnp.abs(a - bb)))
        w = jnp.exp(bb - g)
        k = (1.0 - lb_row) * nsig
    else:
        g, w, k = ls, None, nsig
    q = qp * _sigmoid(qp)
    return q, k, g, sig, nsig, w


def _hg_masks():
    c = HG_CHUNK
    row = lax.broadcasted_iota(jnp.int32, (c, c), 0)
    col = lax.broadcasted_iota(jnp.int32, (c, c), 1)
    lower = (col <= row).astype(BF16)
    upper = (col >= row).astype(BF16)
    krow = lax.broadcasted_iota(jnp.int32, (c, HEAD), 0)
    arow = lax.broadcasted_iota(jnp.int32, (HG_SUB, c), 0)
    acol = lax.broadcasted_iota(jnp.int32, (HG_SUB, c), 1)
    return lower, upper, krow, arow, acol


def _hg_sub(i, q, k, b, b_ref, krow, arow, acol):
    r0 = i * HG_SUB
    m = b_ref[r0:r0 + 1, :]
    ebq = jnp.exp(b[r0:r0 + HG_SUB, :] - m)
    qh = (q[r0:r0 + HG_SUB, :] * ebq).astype(BF16)
    ek = jnp.exp(jnp.where(krow < r0 + HG_SUB, m - b, 0.0))
    kh = (k * ek).astype(BF16)
    mask = acol <= arow + r0
    amat = jnp.where(mask, _dot_nt(qh, kh), 0.0).astype(BF16)
    return ebq, qh, ek, kh, mask, amat


def hgrn_fwd(proj, lb, onorm_g, has_lb, name, plan=None):
    t, w4 = proj.shape
    kw = w4 // 4
    nh, nc, c = kw // HEAD, t // HG_CHUNK, HG_CHUNK

    def body(qp_ref, fp_ref, iv_ref, gp_ref, lb_ref, on_ref, y_ref, st_ref, b_ref):
        lower, _, krow, arow, acol = _hg_masks()
        lb_row, gam = lb_ref[...], on_ref[...]
        st_ref[...] = jnp.zeros_like(st_ref)

        def chunk(ci, carry):
            rs = pl.ds(pl.multiple_of(ci * c, c), c)
            v = iv_ref[rs, :].astype(BF16)
            gp = gp_ref[rs, :]
            q, k, g, _, _, _ = _hg_gates(qp_ref[rs, :], fp_ref[rs, :], lb_row, has_lb)
            b = _split_dot(g, lower, 3, left=True)
            b_ref[...] = b
            st = st_ref[...]
            o = _dot_nt((q * jnp.exp(b)).astype(BF16), st.astype(BF16))
            parts = []
            for i in range(c // HG_SUB):
                amat = _hg_sub(i, q, k, b, b_ref, krow, arow, acol)[-1]
                parts.append(_dot(amat, v))
            o = o + jnp.concatenate(parts, axis=0)
            bl = b_ref[c - 1:c, :]
            st_ref[...] = jnp.exp(bl) * st + _dot_tn(v, (k * jnp.exp(bl - b)).astype(BF16))
            r = lax.rsqrt(jnp.mean(o * o, axis=-1, keepdims=True) + NORM_EPS)
            y_ref[rs, :] = (o * r * gam * (gp * _sigmoid(gp))).astype(BF16)
            return carry

        lax.fori_loop(0, nc, chunk, 0)

    col = lambda p: pl.BlockSpec((t, HEAD), lambda h: (0, p * nh + h))
    vec = pl.BlockSpec((1, HEAD), lambda h: (0, h))
    return _host_call(
        body, 6, plan, lambda: (pl.program_id(0), nh),
        out_shape=jax.ShapeDtypeStruct((t, kw), BF16), grid=(nh,),
        in_specs=[col(0), col(1), col(2), col(3), vec, vec],
        out_specs=pl.BlockSpec((t, HEAD), lambda h: (0, h)),
        scratch_shapes=[pltpu.VMEM((HEAD, HEAD), F32), pltpu.VMEM((c, HEAD), F32)],
        name=name, sem=("parallel",), args=[proj, proj, proj, proj, lb, onorm_g])


def hgrn_bwd(proj, lb, onorm_g, dy, has_lb, name, plan=None):
    t, w4 = proj.shape
    kw = w4 // 4
    nh, nc, c = kw // HEAD, t // HG_CHUNK, HG_CHUNK

    def body(qp_ref, fp_ref, iv_ref, gp_ref, lb_ref, on_ref, dy_ref,
             dp_ref, dgam_ref, dlb_ref, sst_ref, o_ref, b_ref, dst_ref, car_ref):
        lower, upper, krow, arow, acol = _hg_masks()
        lb_row, gam = lb_ref[...], on_ref[...]
        mm = _dot3 if has_lb else _dot1

        def recompute(ci):
            rs = pl.ds(pl.multiple_of(ci * c, c), c)
            gates = _hg_gates(qp_ref[rs, :], fp_ref[rs, :], lb_row, has_lb)
            b = _split_dot(gates[2], lower, 3, left=True)
            b_ref[...] = b
            return rs, gates, b

        def fwd_chunk(ci, carry):
            rs, (q, k, g, _, _, _), b = recompute(ci)
            v = iv_ref[rs, :].astype(BF16)
            st = carry
            sst_ref[ci] = st
            o = _dot_nt((q * jnp.exp(b)).astype(BF16), st.astype(BF16))
            parts = []
            for i in range(c // HG_SUB):
                amat = _hg_sub(i, q, k, b, b_ref, krow, arow, acol)[-1]
                parts.append(_dot(amat, v))
            o_ref[rs, :] = o + jnp.concatenate(parts, axis=0)
            bl = b_ref[c - 1:c, :]
            return jnp.exp(bl) * st + mm(iv_ref[rs, :], k * jnp.exp(bl - b), _dot_tn)

        lax.fori_loop(0, nc, fwd_chunk, jnp.zeros((HEAD, HEAD), F32))

        dst_ref[...] = jnp.zeros_like(dst_ref)
        car_ref[...] = jnp.zeros_like(car_ref)
        dgam_ref[...] = jnp.zeros_like(dgam_ref)
        dlb_ref[...] = jnp.zeros_like(dlb_ref)

        def bwd_chunk(step, carry):
            ci = nc - 1 - step
            rs, (q, k, g, sig, nsig, w), b = recompute(ci)
            qp, gp, v32 = qp_ref[rs, :], gp_ref[rs, :], iv_ref[rs, :]
            v = v32.astype(BF16)
            dyv, o = dy_ref[rs, :], o_ref[rs, :]
            st = sst_ref[ci]
            dst_new = dst_ref[...]
            r = lax.rsqrt(jnp.mean(o * o, axis=-1, keepdims=True) + NORM_EPS)
            on = o * r
            sgm = _sigmoid(gp)
            sg = gp * sgm
            dgam_ref[...] += jnp.sum(dyv * sg * on, axis=0, keepdims=True)
            dgp = dyv * on * gam * (sgm * (1.0 + gp * (1.0 - sgm)))
            dn = dyv * gam * sg
            do = r * dn - o * (r * r * r * jnp.mean(dn * o, axis=-1, keepdims=True))
            eb = jnp.exp(b)
            bl = b_ref[c - 1:c, :]
            ekd = jnp.exp(bl - b)
            do16 = do.astype(BF16)
            dq = mm(do, st, _dot) * eb
            dv = _dot1(k * ekd, dst_new, _dot_nt)
            dk = mm(v32, dst_new, _dot) * ekd
            dst_ref[...] = jnp.exp(bl) * dst_new + mm(do, q * eb, _dot_tn)
            dq_parts = []
            for i in range(c // HG_SUB):
                r0 = i * HG_SUB
                ebq, _, ek, _, mask, amat = _hg_sub(i, q, k, b, b_ref, krow, arow, acol)
                da = jnp.where(mask, mm(do[r0:r0 + HG_SUB, :], v32, _dot_nt), 0.0)
                dq_parts.append(mm(da, k * ek, _dot) * ebq)
                dk = dk + mm(da, q[r0:r0 + HG_SUB, :] * ebq, _dot_tn) * ek
                dv = dv + _dot_tn(amat, do16[r0:r0 + HG_SUB, :])
            dq = dq + jnp.concatenate(dq_parts, axis=0)
            db = q * dq - k * dk
            dg = car_ref[...] + _split_dot(db, upper, 3, left=True)
            car_ref[...] += jnp.sum(db, axis=0, keepdims=True)
            if has_lb:
                dfp = dg * nsig * w - dk * ((1.0 - lb_row) * sig * nsig)
                dlb_ref[...] += jnp.sum(dg * nsig * jnp.exp(-g) - dk * nsig, axis=0, keepdims=True)
            else:
                dfp = dg * nsig - dk * (sig * nsig)
            sq = _sigmoid(qp)
            dp_ref[0, rs, :] = (dq * (sq * (1.0 + qp * (1.0 - sq)))).astype(BF16)
            dp_ref[1, rs, :] = dfp.astype(BF16)
            dp_ref[2, rs, :] = dv.astype(BF16)
            dp_ref[3, rs, :] = dgp.astype(BF16)
            return carry

        lax.fori_loop(0, nc, bwd_chunk, 0)

    col = lambda p: pl.BlockSpec((t, HEAD), lambda h: (0, p * nh + h))
    vec = pl.BlockSpec((1, HEAD), lambda h: (0, h))
    return _host_call(
        body, 7, plan, lambda: (pl.program_id(0), nh),
        out_shape=[jax.ShapeDtypeStruct((4, t, kw), BF16), jax.ShapeDtypeStruct((1, kw), F32),
                   jax.ShapeDtypeStruct((1, kw), F32)],
        grid=(nh,),
        in_specs=[col(0), col(1), col(2), col(3), vec, vec, pl.BlockSpec((t, HEAD), lambda h: (0, h))],
        out_specs=[pl.BlockSpec((4, t, HEAD), lambda h: (0, 0, h)), vec, vec],
        scratch_shapes=[pltpu.VMEM((nc, HEAD, HEAD), F32), pltpu.VMEM((t, HEAD), F32),
                        pltpu.VMEM((c, HEAD), F32), pltpu.VMEM((HEAD, HEAD), F32), pltpu.VMEM((1, HEAD), F32)],
        name=name, sem=("parallel",), args=[proj, proj, proj, proj, lb, onorm_g, dy])


def _sb_masks():
    m, n = SB_QROWS, SB_BLOCK
    row = lax.broadcasted_iota(jnp.int32, (m, n), 0)
    col = lax.broadcasted_iota(jnp.int32, (m, n), 1)
    r2 = lax.broadcasted_iota(jnp.int32, (n, n), 0)
    c2 = lax.broadcasted_iota(jnp.int32, (n, n), 1)
    after = (r2 > c2).astype(BF16)
    from_ = (r2 >= c2).astype(BF16)
    return row, col, after, from_


def sba_fwd(qkv, name, plan=None):
    t, w3 = qkv.shape
    wd = w3 // 3
    m, n = SB_QROWS, SB_BLOCK
    nh, nq, per = wd // HEAD, t // m, m // n
    scale = HEAD ** -0.5

    def body(q_ref, k_ref, v_ref, o_ref, q16, k16, v16):
        row, col, after, _ = _sb_masks()
        q16[...] = (q_ref[...] * scale).astype(BF16)
        k16[...] = k_ref[...].astype(BF16)
        v16[...] = v_ref[...].astype(BF16)

        def qblock(qi, carry):
            qs = pl.ds(pl.multiple_of(qi * m, m), m)
            q = q16[qs, :]
            last = (qi + 1) * per - 1

            def kblock(step, state):
                acc, rem0 = state
                kj = last - step
                ks = pl.ds(pl.multiple_of(kj * n, n), n)
                z = _dot_nt(q, k16[ks, :])
                strict = (col + kj * n) < (row + qi * m)
                spz = _softplus(z)
                sp = jnp.where(strict, spz, 0.0)
                rem = rem0 + _split_dot(sp, after, 2)
                a = jnp.where(strict, jnp.exp(z - spz - rem), 0.0)
                acc = acc + _dot(a.astype(BF16), v16[ks, :])
                return acc, rem0 + jnp.sum(sp, axis=1, keepdims=True)

            acc, _ = lax.fori_loop(0, last + 1, kblock,
                                   (jnp.zeros((m, HEAD), F32), jnp.zeros((m, 1), F32)))
            o_ref[qs, :] = acc
            return carry

        lax.fori_loop(0, nq, qblock, 0)

    col_spec = lambda p: pl.BlockSpec((t, HEAD), lambda h: (0, p * nh + h))
    return _host_call(
        body, 3, plan, lambda: (pl.program_id(0), nh),
        out_shape=jax.ShapeDtypeStruct((t, wd), F32), grid=(nh,),
        in_specs=[col_spec(0), col_spec(1), col_spec(2)],
        out_specs=pl.BlockSpec((t, HEAD), lambda h: (0, h)),
        scratch_shapes=[pltpu.VMEM((t, HEAD), BF16)] * 3,
        name=name, sem=("parallel",), args=[qkv, qkv, qkv])


def sba_bwd(qkv, o, do, name, plan=None):
    t, w3 = qkv.shape
    wd = w3 // 3
    m, n = SB_QROWS, SB_BLOCK
    nh, nq, per = wd // HEAD, t // m, m // n
    scale = HEAD ** -0.5

    def body(q_ref, k_ref, v_ref, o_ref, do_ref, d_ref, dk_ref, dv_ref, q16, k16, v16):
        row, col, after, from_ = _sb_masks()
        dk_ref[...] = jnp.zeros_like(dk_ref)
        dv_ref[...] = jnp.zeros_like(dv_ref)
        q16[...] = (q_ref[...] * scale).astype(BF16)
        k16[...] = k_ref[...].astype(BF16)
        v16[...] = v_ref[...].astype(BF16)

        def qblock(qi, carry):
            qs = pl.ds(pl.multiple_of(qi * m, m), m)
            q = q16[qs, :]
            dov = do_ref[qs, :]
            do16 = dov.astype(BF16)
            dsum = jnp.sum(do16.astype(F32) * o_ref[qs, :], axis=1, keepdims=True)
            last = (qi + 1) * per - 1

            def kblock(step, state):
                dq, rem0, e0 = state
                kj = last - step
                ks = pl.ds(pl.multiple_of(kj * n, n), n)
                kv, vv = k16[ks, :], v16[ks, :]
                z = _dot_nt(q, kv)
                strict = (col + kj * n) < (row + qi * m)
                spz = _softplus(z)
                sp = jnp.where(strict, spz, 0.0)
                rem = rem0 + _split_dot(sp, after, 2)
                sgz = jnp.exp(z - spz)
                a = jnp.where(strict, sgz * jnp.exp(-rem), 0.0).astype(BF16)
                e = a.astype(F32) * _dot_nt(do16, vv)
                left = dsum - (e0 + _split_dot(e, from_, 2))
                dz = jnp.where(strict, e * (1.0 - sgz) - sgz * left, 0.0).astype(BF16)
                dq = dq + _dot(dz, kv)
                dk_ref[ks, :] += _dot_tn(dz, q)
                dv_ref[ks, :] += _dot_tn(a, do16)
                return (dq, rem0 + jnp.sum(sp, axis=1, keepdims=True),
                        e0 + jnp.sum(e, axis=1, keepdims=True))

            zero1 = jnp.zeros((m, 1), F32)
            dq, _, _ = lax.fori_loop(0, last + 1, kblock, (jnp.zeros((m, HEAD), F32), zero1, zero1))
            d_ref[0, qs, :] = (dq * scale).astype(BF16)
            return carry

        lax.fori_loop(0, nq, qblock, 0)
        d_ref[1, :, :] = dk_ref[...].astype(BF16)
        d_ref[2, :, :] = dv_ref[...].astype(BF16)

    col_spec = lambda p: pl.BlockSpec((t, HEAD), lambda h: (0, p * nh + h))
    head = pl.BlockSpec((t, HEAD), lambda h: (0, h))
    return _host_call(
        body, 5, plan, lambda: (pl.program_id(0), nh),
        out_shape=jax.ShapeDtypeStruct((3, t, wd), BF16), grid=(nh,),
        in_specs=[col_spec(0), col_spec(1), col_spec(2), head, head],
        out_specs=pl.BlockSpec((3, t, HEAD), lambda h: (0, 0, h)),
        scratch_shapes=[pltpu.VMEM((t, HEAD), F32)] * 2 + [pltpu.VMEM((t, HEAD), BF16)] * 3,
        name=name, sem=("parallel",), args=[qkv, qkv, qkv, o, do])


def _pool_band(i_out, i_in, tr, win, transpose):
    r = lax.broadcasted_iota(jnp.int32, (tr, tr), 0) + i_out * tr
    c = lax.broadcasted_iota(jnp.int32, (tr, tr), 1) + i_in * tr
    if transpose:
        return ((r <= c) & (r > c - win)).astype(F32)
    return ((c <= r) & (c > r - win)).astype(F32)


def _pool_p(u_ref, i, tr, win):
    cur = u_ref[i * tr:(i + 1) * tr, :]
    ws = _dot(_pool_band(i, i, tr, win, False), cur, HI)
    if i > 0:
        ws = ws + _dot(_pool_band(i, i - 1, tr, win, False), u_ref[(i - 1) * tr:i * tr, :], HI)
    pos = lax.broadcasted_iota(jnp.int32, (tr, 1), 0) + (i * tr + 1)
    count = jnp.minimum(pos, win).astype(F32)
    return ws / count - cur, count


def pool_fwd(u, pool_w, pool_scale, name):
    t, d = u.shape
    ng = len(POOL_WINDOWS)
    gs = d // ng
    tr = min(ROW_TILE, t)

    def body(u_ref, w_ref, s_ref, y_ref):
        win = jnp.left_shift(2, pl.program_id(0))
        for i in range(t // tr):
            p, _ = _pool_p(u_ref, i, tr, win)
            y_ref[i * tr:(i + 1) * tr, :] = _dot(p.astype(BF16), w_ref[...]) * s_ref[...]

    grp = pl.BlockSpec((t, gs), lambda g: (0, g))
    return pl.pallas_call(
        body, out_shape=jax.ShapeDtypeStruct((t, d), F32), grid=(ng,),
        in_specs=[grp, pl.BlockSpec((None, gs, gs), lambda g: (g, 0, 0)), pl.BlockSpec((1, gs), lambda g: (0, g))],
        out_specs=grp, name=name, compiler_params=_params(("parallel",)),
    )(u, pool_w, pool_scale)


def pool_bwd(u, pool_w, pool_scale, dy, name):
    t, d = u.shape
    ng = len(POOL_WINDOWS)
    gs = d // ng
    tr = min(ROW_TILE, t)
    nt = t // tr

    def body(u_ref, w_ref, s_ref, dy_ref, du_ref, dw_ref, ds_ref, dpc_ref, dp_ref):
        win = jnp.left_shift(2, pl.program_id(0))
        wv = w_ref[...]
        dw = jnp.zeros((gs, gs), F32)
        dsc = jnp.zeros((1, gs), F32)
        for i in range(nt):
            rows = slice(i * tr, (i + 1) * tr)
            p, count = _pool_p(u_ref, i, tr, win)
            p16 = p.astype(BF16)
            dyv = dy_ref[rows, :]
            dsc = dsc + jnp.sum(dyv * _dot(p16, wv), axis=0, keepdims=True)
            dyp = (dyv * s_ref[...]).astype(BF16)
            dw = dw + _dot_tn(p16, dyp)
            dp = _dot_nt(dyp, wv)
            dp_ref[rows, :] = dp
            dpc_ref[rows, :] = dp / count
        dw_ref[...] = dw.astype(BF16)
        ds_ref[...] = dsc
        for i in range(nt):
            rows = slice(i * tr, (i + 1) * tr)
            acc = _dot(_pool_band(i, i, tr, win, True), dpc_ref[rows, :], HI)
            if i + 1 < nt:
                acc = acc + _dot(_pool_band(i, i + 1, tr, win, True), dpc_ref[(i + 1) * tr:(i + 2) * tr, :], HI)
            du_ref[rows, :] = acc - dp_ref[rows, :]

    grp = pl.BlockSpec((t, gs), lambda g: (0, g))
    wspec = pl.BlockSpec((None, gs, gs), lambda g: (g, 0, 0))
    vec = pl.BlockSpec((1, gs), lambda g: (0, g))
    return pl.pallas_call(
        body,
        out_shape=[jax.ShapeDtypeStruct((t, d), F32), jax.ShapeDtypeStruct((ng, gs, gs), BF16),
                   jax.ShapeDtypeStruct((1, d), F32)],
        grid=(ng,), in_specs=[grp, wspec, vec, grp], out_specs=[grp, wspec, vec],
        scratch_shapes=[pltpu.VMEM((t, gs), F32), pltpu.VMEM((t, gs), F32)],
        name=name, compiler_params=_params(("parallel",)),
    )(u, pool_w, pool_scale, dy)


CONV_COLS = 256
HALO = 8


def _conv_taps(ref, r0, tr):
    x = ref[r0:r0 + tr, :]
    prev = ref[r0 - HALO:r0, :] if r0 > 0 else jnp.zeros((HALO, x.shape[1]), F32)
    xx = jnp.concatenate([prev, x], axis=0)
    return x, pltpu.roll(xx, 1, 0)[HALO:, :], pltpu.roll(xx, 2, 0)[HALO:, :]


def _conv_out(taps, w_ref, b_ref):
    x, s1, s2 = taps
    return w_ref[0:1, :] * s2 + w_ref[1:2, :] * s1 + w_ref[2:3, :] * x + b_ref[...]


def conv_glu_fwd(up, conv_w, conv_b, name):
    t, f2 = up.shape
    f = f2 // 2
    tc = min(CONV_COLS, f)
    nj = f // tc
    tr = min(ROW_TILE, t)

    def body(ug_ref, uv_ref, wg_ref, wv_ref, bg_ref, bv_ref, o_ref):
        for i in range(t // tr):
            r0 = i * tr
            gate = _conv_out(_conv_taps(ug_ref, r0, tr), wg_ref, bg_ref)
            val = _conv_out(_conv_taps(uv_ref, r0, tr), wv_ref, bv_ref)
            o_ref[r0:r0 + tr, :] = (gate * _sigmoid(gate) * val).astype(BF16)

    blk = lambda rows, half: pl.BlockSpec((rows, tc), lambda j: (0, half * nj + j))
    return pl.pallas_call(
        body, out_shape=jax.ShapeDtypeStruct((t, f), BF16), grid=(nj,),
        in_specs=[blk(t, 0), blk(t, 1), blk(CONV_WIDTH, 0), blk(CONV_WIDTH, 1), blk(1, 0), blk(1, 1)],
        out_specs=pl.BlockSpec((t, tc), lambda j: (0, j)),
        name=name, compiler_params=_params(("parallel",)),
    )(up, up, conv_w, conv_w, conv_b, conv_b)


def conv_glu_bwd(up, conv_w, conv_b, dact, name):
    t, f2 = up.shape
    f = f2 // 2
    tc = min(CONV_COLS, f)
    nj = f // tc
    tr = min(ROW_TILE, t)
    nt = t // tr

    def body(ug_ref, uv_ref, wg_ref, wv_ref, bg_ref, bv_ref, da_ref, du_ref, dw_ref, db_ref, dg_ref, dv_ref):
        dwg = [jnp.zeros((1, tc), F32) for _ in range(CONV_WIDTH)]
        dwv = [jnp.zeros((1, tc), F32) for _ in range(CONV_WIDTH)]
        dbg = jnp.zeros((1, tc), F32)
        dbv = jnp.zeros((1, tc), F32)
        for i in range(nt):
            r0 = i * tr
            tg, tv = _conv_taps(ug_ref, r0, tr), _conv_taps(uv_ref, r0, tr)
            gate, val = _conv_out(tg, wg_ref, bg_ref), _conv_out(tv, wv_ref, bv_ref)
            sg = _sigmoid(gate)
            da = da_ref[r0:r0 + tr, :]
            d_gate = da * val * (sg * (1.0 + gate * (1.0 - sg)))
            d_val = da * (gate * sg)
            dg_ref[r0:r0 + tr, :] = d_gate
            dv_ref[r0:r0 + tr, :] = d_val
            dbg = dbg + jnp.sum(d_gate, axis=0, keepdims=True)
            dbv = dbv + jnp.sum(d_val, axis=0, keepdims=True)
            for tap in range(CONV_WIDTH):
                dwg[tap] = dwg[tap] + jnp.sum(d_gate * tg[2 - tap], axis=0, keepdims=True)
                dwv[tap] = dwv[tap] + jnp.sum(d_val * tv[2 - tap], axis=0, keepdims=True)
        for tap in range(CONV_WIDTH):
            dw_ref[0, tap:tap + 1, :] = dwg[tap]
            dw_ref[1, tap:tap + 1, :] = dwv[tap]
        db_ref[0, :, :] = dbg
        db_ref[1, :, :] = dbv
        for half, (d_ref, w_ref) in enumerate(((dg_ref, wg_ref), (dv_ref, wv_ref))):
            for i in range(nt):
                r0 = i * tr
                x = d_ref[r0:r0 + tr, :]
                nxt = d_ref[r0 + tr:r0 + tr + HALO, :] if i + 1 < nt else jnp.zeros((HALO, tc), F32)
                xx = jnp.concatenate([x, nxt], axis=0)
                up1 = pltpu.roll(xx, tr + HALO - 1, 0)[:tr, :]
                up2 = pltpu.roll(xx, tr + HALO - 2, 0)[:tr, :]
                du = w_ref[2:3, :] * x + w_ref[1:2, :] * up1 + w_ref[0:1, :] * up2
                du_ref[half, r0:r0 + tr, :] = du.astype(BF16)

    blk = lambda rows, half: pl.BlockSpec((rows, tc), lambda j: (0, half * nj + j))
    return pl.pallas_call(
        body,
        out_shape=[jax.ShapeDtypeStruct((2, t, f), BF16), jax.ShapeDtypeStruct((2, CONV_WIDTH, f), F32),
                   jax.ShapeDtypeStruct((2, 1, f), F32)],
        grid=(nj,),
        in_specs=[blk(t, 0), blk(t, 1), blk(CONV_WIDTH, 0), blk(CONV_WIDTH, 1), blk(1, 0), blk(1, 1),
                  pl.BlockSpec((t, tc), lambda j: (0, j))],
        out_specs=[pl.BlockSpec((2, t, tc), lambda j: (0, 0, j)),
                   pl.BlockSpec((2, CONV_WIDTH, tc), lambda j: (0, 0, j)),
                   pl.BlockSpec((2, 1, tc), lambda j: (0, 0, j))],
        scratch_shapes=[pltpu.VMEM((t, tc), F32), pltpu.VMEM((t, tc), F32)],
        name=name, compiler_params=_params(("parallel",)),
    )(up, up, conv_w, conv_w, conv_b, conv_b, dact)


def _place():
    x, y, c = lax.axis_index("x"), lax.axis_index("y"), lax.axis_index("c")
    others = [(1 - x, y), (x, 1 - y), (1 - x, 1 - y)]
    return x, y, c, others


def _window(ref, axis, b, n):
    if axis == 1:
        return ref.at[:, pl.ds(b * n, n), :]
    return ref.at[:, :, pl.ds(b * n, n)]


def _remote(src, dst, send_sems, recv_sems, k, to):
    return pltpu.make_async_remote_copy(src_ref=src, dst_ref=dst, send_sem=send_sems.at[k],
                                        recv_sem=recv_sems.at[k], device_id=to, device_id_type=MESH)


class GatherPlan:
    has_mid = True

    def __init__(self, items):
        self.items = items
        self.args = [s for s, _, _, _ in items]
        self.out_shape = []
        for s, _, nl, ax in items:
            shp = [nl, s.shape[1], s.shape[2]]
            shp[ax] *= N_DEV
            self.out_shape.append(jax.ShapeDtypeStruct(tuple(shp), s.dtype))
        n = len(items)
        self.scratch = [pltpu.SemaphoreType.DMA((7 * n,)), pltpu.SemaphoreType.DMA((7 * n,)),
                        pltpu.SemaphoreType.DMA((n,))]

    def _mine(self, ins, a):
        _, l0, nl, _ = self.items[a]
        return ins[a].at[pl.ds(l0, nl)]

    def _copy(self, ins, outs, sems, a, k, block, to, own=False):
        s, _, _, ax = self.items[a]
        px, py, pc = block
        w = _window(outs[a], ax, 4 * px + 2 * py + pc, s.shape[ax])
        return _remote(self._mine(ins, a) if own else w, w, sems[0], sems[1], 7 * a + k, to)

    def _local(self, ins, outs, sems, a, x, y, c):
        s, _, _, ax = self.items[a]
        return pltpu.make_async_copy(self._mine(ins, a), _window(outs[a], ax, 4 * x + 2 * y + c, s.shape[ax]),
                                     sems[2].at[a])

    def _first(self, ins, outs, sems, a, x, y, c, others):
        me = (x, y, c)
        return [self._copy(ins, outs, sems, a, 0, me, (x, y, 1 - c), own=True)] + [
            self._copy(ins, outs, sems, a, 1 + j, me, (*chip, c), own=True) for j, chip in enumerate(others)]

    def start(self, ins, outs, sems):
        x, y, c, others = _place()
        for a in range(len(self.items)):
            self._local(ins, outs, sems, a, x, y, c).start()
        for a in range(len(self.items)):
            for cp in self._first(ins, outs, sems, a, x, y, c, others):
                cp.start()

    def mid(self, ins, outs, sems):
        x, y, c, others = _place()
        for a in range(len(self.items)):
            for j, chip in enumerate(others):
                self._copy(ins, outs, sems, a, 1 + j, (*chip, c), (x, y, c)).wait_recv()
                self._copy(ins, outs, sems, a, 4 + j, (*chip, c), (x, y, 1 - c)).start()

    def finish(self, ins, outs, sems):
        x, y, c, others = _place()
        for a in range(len(self.items)):
            self._copy(ins, outs, sems, a, 0, (x, y, 1 - c), (x, y, c)).wait_recv()
            for j, chip in enumerate(others):
                self._copy(ins, outs, sems, a, 4 + j, (*chip, 1 - c), (x, y, c)).wait_recv()
        for a in range(len(self.items)):
            for cp in self._first(ins, outs, sems, a, x, y, c, others):
                cp.wait_send()
            for j, chip in enumerate(others):
                self._copy(ins, outs, sems, a, 4 + j, (*chip, c), (x, y, 1 - c)).wait_send()
            self._local(ins, outs, sems, a, x, y, c).wait()


class ExchangePlan:
    has_mid = False

    def __init__(self, partials):
        self.args = list(partials)
        self.out_shape = [jax.ShapeDtypeStruct(p.shape, p.dtype) for p in partials]
        n = len(partials)
        self.scratch = [pltpu.SemaphoreType.DMA((3 * n,)), pltpu.SemaphoreType.DMA((3 * n,)),
                        pltpu.SemaphoreType.DMA((n,))]

    def _copies(self, ins, outs, sems):
        x, y, c, others = _place()
        me = 2 * x + y
        local, sends, recvs = [], [], []
        for a in range(len(self.args)):
            local.append(pltpu.make_async_copy(ins[a].at[me], outs[a].at[me], sems[2].at[a]))
            for j, (px, py) in enumerate(others):
                sends.append(_remote(ins[a].at[2 * px + py], outs[a].at[me], sems[0], sems[1], 3 * a + j, (px, py, c)))
                slot = outs[a].at[2 * px + py]
                recvs.append(_remote(slot, slot, sems[0], sems[1], 3 * a + j, (px, py, c)))
        return local, sends, recvs

    def start(self, ins, outs, sems):
        local, sends, _ = self._copies(ins, outs, sems)
        for cp in local + sends:
            cp.start()

    def finish(self, ins, outs, sems):
        local, sends, recvs = self._copies(ins, outs, sems)
        for cp in recvs:
            cp.wait_recv()
        for cp in sends:
            cp.wait_send()
        for cp in local:
            cp.wait()


def run_plan(plan, name):
    ni, no = len(plan.args), len(plan.out_shape)

    def body(*refs):
        ins, outs, sems = refs[:ni], refs[ni:ni + no], refs[ni + no:]
        plan.start(ins, outs, sems)
        if plan.has_mid:
            plan.mid(ins, outs, sems)
        plan.finish(ins, outs, sems)

    return pl.pallas_call(
        body, out_shape=plan.out_shape, in_specs=[ANY] * ni, out_specs=[ANY] * no,
        scratch_shapes=plan.scratch, name=name,
    )(*plan.args)


def sibling_exchange(grads, axes):
    na = len(grads)
    widths = [g.shape[ax] // N_DEV for g, ax in zip(grads, axes)]
    out_shape = []
    for g, ax, n in zip(grads, axes, widths):
        shp = list(g.shape)
        shp[ax] = n
        out_shape.append(jax.ShapeDtypeStruct((N_CHIP, *shp), g.dtype))

    def body(*refs):
        ins, outs = refs[:na], refs[na:2 * na]
        send_sems, recv_sems = refs[2 * na:]
        x, y, c, _ = _place()
        copies = []
        for a in range(na):
            for q in range(N_CHIP):
                src = _window(ins[a], axes[a], 2 * q + (1 - c), widths[a])
                cp = _remote(src, outs[a].at[q], send_sems, recv_sems, N_CHIP * a + q, (x, y, 1 - c))
                cp.start()
                copies.append(cp)
        for cp in copies:
            cp.wait_recv()
        for cp in copies:
            cp.wait_send()

    return pl.pallas_call(
        body, out_shape=out_shape, in_specs=[ANY] * na, out_specs=[ANY] * na,
        scratch_shapes=[pltpu.SemaphoreType.DMA((N_CHIP * na,)), pltpu.SemaphoreType.DMA((N_CHIP * na,))],
        name="sibling_exchange",
    )(*grads)


def _peer_of(k, x, y, c):
    return (1 - x if k & 4 else x, 1 - y if k & 2 else y, 1 - c if k & 1 else c)


def small_exchange(vec, reduce, name):
    r = vec.shape[0]

    def body(v_ref, o_ref, *scratch):
        if reduce:
            buf, send_sems, recv_sems = scratch
        else:
            buf, (send_sems, recv_sems) = o_ref, scratch
        x, y, c, _ = _place()
        me = 4 * x + 2 * y + c
        copies = []
        for k in range(1, N_DEV):
            cp = _remote(v_ref, buf.at[me], send_sems, recv_sems, k - 1, _peer_of(k, x, y, c))
            cp.start()
            copies.append(cp)
        buf[me] = v_ref[...]
        for k in range(1, N_DEV):
            px, py, pc = _peer_of(k, x, y, c)
            slot = buf.at[4 * px + 2 * py + pc]
            _remote(slot, slot, send_sems, recv_sems, k - 1, (px, py, pc)).wait_recv()
        for cp in copies:
            cp.wait_send()
        if reduce:
            tot = buf[0]
            for b in range(1, N_DEV):
                tot = tot + buf[b]
            o_ref[...] = tot

    sems = [pltpu.SemaphoreType.DMA((N_DEV - 1,)), pltpu.SemaphoreType.DMA((N_DEV - 1,))]
    if reduce:
        out_shape = jax.ShapeDtypeStruct((r, LANES), F32)
        scratch = [pltpu.VMEM((N_DEV, r, LANES), F32)] + sems
    else:
        out_shape = jax.ShapeDtypeStruct((N_DEV, r, LANES), F32)
        scratch = sems
    return pl.pallas_call(
        body, out_shape=out_shape, in_specs=[VMEM_FULL], out_specs=VMEM_FULL,
        scratch_shapes=scratch, name=name, compiler_params=_params(None),
    )(vec)


def _row_tile(rows, cols):
    best = None
    for tb in range(16, rows + 1, 16):
        if rows % tb == 0 and tb * cols <= ELEMWISE_BLOCK_ELEMS:
            best = tb
    return best if best is not None else rows


def add_sibling(grad, recv, axis, core):
    nl = grad.shape[0]
    _, _, r, cc = recv.shape
    tb = _row_tile(r, cc)
    per = r // tb

    def body(c_ref, g_ref, r_ref, o_ref):
        del c_ref
        o_ref[...] = (g_ref[...].astype(F32) + r_ref[...].astype(F32)).astype(BF16)

    if axis == 2:
        g_spec = pl.BlockSpec((None, tb, cc), lambda q, l, i, c_ref: (l, i, 2 * q + c_ref[0]))
    else:
        g_spec = pl.BlockSpec((None, tb, cc), lambda q, l, i, c_ref: (l, (2 * q + c_ref[0]) * per + i, 0))
    slot = pl.BlockSpec((None, None, tb, cc), lambda q, l, i, c_ref: (q, l, i, 0))
    return pl.pallas_call(
        body, out_shape=jax.ShapeDtypeStruct(recv.shape, BF16),
        grid_spec=pltpu.PrefetchScalarGridSpec(
            num_scalar_prefetch=1, grid=(N_CHIP, nl, per), in_specs=[g_spec, slot], out_specs=slot),
        name="add_sibling", compiler_params=_params(("parallel", "parallel", "parallel")),
    )(core, grad, recv)


def _adamw(w, g, m, v):
    m = ADAM_B1 * m + (1.0 - ADAM_B1) * g
    v = ADAM_B2 * v + (1.0 - ADAM_B2) * (g * g)
    m_hat = m / (1.0 - ADAM_B1 ** ADAM_STEP)
    v_hat = v / (1.0 - ADAM_B2 ** ADAM_STEP)
    delta = -ADAM_LR * (m_hat / (jnp.sqrt(v_hat) + ADAM_EPS) + ADAM_WD * w)
    return delta, m, v


def adam_from_partials(recv, w, m, v, l0, bufs):
    nl = recv.shape[1]
    _, r, cc = w.shape
    tb = _row_tile(r, cc)

    def body(p0, p1, p2, p3, w_ref, m_ref, v_ref, b0, b1, b2, b3, g_out, d_out, m_out, v_out):
        del b0, b1, b2, b3
        g = p0[...].astype(F32) + p1[...].astype(F32) + p2[...].astype(F32) + p3[...].astype(F32)
        d, mn, vn = _adamw(w_ref[...], g, m_ref[...], v_ref[...])
        g_out[...], d_out[...], m_out[...], v_out[...] = g, d, mn, vn

    slot = lambda q: pl.BlockSpec((None, None, tb, cc), lambda l, i: (q, l, i, 0))
    blk = pl.BlockSpec((None, tb, cc), lambda l, i: (l0 + l, i, 0))
    shp = jax.ShapeDtypeStruct(w.shape, F32)
    return pl.pallas_call(
        body, out_shape=[shp] * 4, grid=(nl, r // tb),
        in_specs=[slot(0), slot(1), slot(2), slot(3), blk, blk, blk] + [ANY] * 4, out_specs=[blk] * 4,
        input_output_aliases={7: 0, 8: 1, 9: 2, 10: 3},
        name="adam_big", compiler_params=_params(("parallel", "parallel")),
    )(recv, recv, recv, recv, w, m, v, *bufs)


def adam_small(w, g, m, v):
    def body(w_ref, g_ref, m_ref, v_ref, d_out, m_out, v_out):
        d_out[...], m_out[...], v_out[...] = _adamw(w_ref[...], g_ref[...], m_ref[...], v_ref[...])

    shp = jax.ShapeDtypeStruct(w.shape, F32)
    return pl.pallas_call(body, out_shape=[shp] * 3, name="adam_small")(w, g, m, v)


def _pack(arrays, multiple=8 * LANES):
    flat = jnp.concatenate([a.reshape(-1) for a in arrays])
    pad = (-flat.shape[0]) % multiple
    if pad:
        flat = jnp.concatenate([flat, jnp.zeros((pad,), flat.dtype)])
    return flat.reshape(-1, LANES)


def _unpack(packed, shapes):
    flat = packed.reshape(packed.shape[:-2] + (-1,))
    out, off = [], 0
    for shp in shapes:
        n = math.prod(shp)
        out.append(flat[..., off:off + n].reshape(packed.shape[:-2] + tuple(shp)))
        off += n
    return out


def _unshard_last(stacked):
    moved = jnp.moveaxis(stacked, 0, -2)
    return moved.reshape(moved.shape[:-2] + (-1,))


def _shard_last(full, block):
    n = full.shape[-1] // N_DEV
    return lax.dynamic_slice_in_dim(full, block * n, n, axis=full.ndim - 1)


BIG_AXIS = {"w_in": 2, "w_out": 1, "w_qkv": 2, "w_so": 1, "w_pool": 1, "w_up": 2, "w_down": 1}
BIG_ORDER = ("w_in", "w_out", "w_qkv", "w_so", "w_pool", "w_up", "w_down")

GATHER_HOSTS = {
    "hgrn_fwd_0": (("w_out", 0), ("w_up", 0), ("w_down", 0)),
    "ffn_up_0": (("w_qkv", 0),),
    "sba_fwd_1": (("w_so", 0), ("w_up", 1), ("w_down", 1), ("w_pool", 0), ("w_up", 2), ("w_down", 2)),
    "ffn_up_2": (("w_in", 1),),
    "hgrn_fwd_3": (("w_out", 1), ("w_up", 3), ("w_down", 3)),
}


class LocalWeights:
    def __init__(self, full):
        self.full, self.grads = full, {}

    def gather_plan(self, host):
        return None

    def gathered(self, outs):
        pass

    def weight(self, kind, l):
        return self.full[kind][l] if kind != "w_pool" else self.full[kind]

    def grad(self, kind, l, g):
        self.grads[(kind, l)] = g

    def exchange_plan(self):
        return None

    def exchanged(self, outs):
        pass


class MeshWeights:
    def __init__(self, shards16, w, m, v, core):
        self.shards, self.w, self.m, self.v, self.core = shards16, w, m, v, core
        self.full, self.pending, self.flying, self.keys = {}, [], [], None
        self.out = {k: [lax.empty(w[k].shape, F32) for _ in range(4)] for k in BIG_ORDER}

    def _items(self, keys):
        return [(self.shards[k], 0 if k == "w_pool" else l, 4 if k == "w_pool" else 1, BIG_AXIS[k]) for k, l in keys]

    def gather_plan(self, host):
        self.keys = [key for key in GATHER_HOSTS.get(host, ()) if key[0] in self.shards
                     and key[1] < (1 if key[0] == "w_pool" else self.shards[key[0]].shape[0])]
        return GatherPlan(self._items(self.keys)) if self.keys else None

    def gathered(self, outs):
        for (k, l), o in zip(self.keys, outs):
            self.full[(k, l)] = o if k == "w_pool" else o[0]
        self.keys = None

    def weight(self, kind, l):
        if (kind, l) not in self.full:
            (out,) = run_plan(GatherPlan(self._items([(kind, l)])), f"gather_{kind}_{l}")
            self.full[(kind, l)] = out if kind == "w_pool" else out[0]
        return self.full[(kind, l)]

    def grad(self, kind, l, g):
        self.pending.append((kind, l, g if g.ndim == 3 else g[None]))

    def exchange_plan(self):
        if not self.pending:
            return None
        self.flying, self.pending = self.pending, []
        grads = [g for _, _, g in self.flying]
        axes = [BIG_AXIS[k] for k, _, _ in self.flying]
        recv = sibling_exchange(grads, axes)
        return ExchangePlan([add_sibling(g, r, ax, self.core) for g, r, ax in zip(grads, recv, axes)])

    def exchanged(self, outs):
        for (k, l, _), r in zip(self.flying, outs):
            self.out[k] = adam_from_partials(r, self.w[k], self.m[k], self.v[k], l, self.out[k])
        self.flying = []

    def finish(self):
        plan = self.exchange_plan()
        if plan is not None:
            self.exchanged(run_plan(plan, "chip_exchange_tail"))
        return self.out


def train_step(x, target, norm_g, lb_logits, onorm_g, pool_scale, conv_w, conv_b, wts):
    t, d = x.shape
    depth = norm_g.shape[0]
    ng = lambda i, j: norm_g[i, j].reshape(1, d)
    lbs = lower_bounds_fwd(lb_logits)
    saved = []
    h = x
    _, u16, u32 = res_norm(h, None, None, ng(0, 0), "norm_in")

    def hosted(call, *args, **kw):
        plan = wts.gather_plan(kw["name"])
        out, extra = call(*args, plan=plan, **kw)
        if plan is not None:
            wts.gathered(extra)
        return out

    for i in range(depth):
        kind, j = i % 3, i // 3
        s = {"h_in": h, "u1": u16}
        if kind == 0:
            s["proj"] = matmul(u16, wts.weight("w_in", j), "nn", F32, f"hg_in_{i}")
            s["y"] = hosted(hgrn_fwd, s["proj"], lbs[i].reshape(1, -1), onorm_g[j].reshape(1, -1), i > 0,
                            name=f"hgrn_fwd_{i}")
            mix = matmul(s["y"], wts.weight("w_out", j), "nn", F32, f"hg_out_{i}")
        elif kind == 1:
            s["qkv"] = matmul(u16, wts.weight("w_qkv", j), "nn", F32, f"sb_qkv_{i}")
            s["o"] = hosted(sba_fwd, s["qkv"], name=f"sba_fwd_{i}")
            mix = matmul(s["o"], wts.weight("w_so", j), "nn", F32, f"sb_out_{i}")
        else:
            s["u1f"] = u32
            mix = pool_fwd(u32, wts.weight("w_pool", j), pool_scale[j].reshape(1, d), f"pool_fwd_{i}")
        s["mix"] = mix
        h_mid, u2, _ = res_norm(h, mix, ng(i, 1), ng(i, 2), f"norm_mid_{i}")
        s["h_mid"], s["u2"] = h_mid, u2
        up_plan = wts.gather_plan(f"ffn_up_{i}")
        if up_plan is None:
            s["up"] = matmul(u2, wts.weight("w_up", i), "nn", F32, f"ffn_up_{i}")
        else:
            s["up"], extra = matmul(u2, wts.weight("w_up", i), "nn", F32, f"ffn_up_{i}", plan=up_plan)
            wts.gathered(extra)
        s["act"] = conv_glu_fwd(s["up"], conv_w[i], conv_b[i].reshape(1, -1), f"glu_fwd_{i}")
        s["f"] = matmul(s["act"], wts.weight("w_down", i), "nn", F32, f"ffn_down_{i}")
        nxt = ng(i + 1, 0) if i + 1 < depth else None
        h, u16, u32 = res_norm(h_mid, s["f"], ng(i, 3), nxt, f"norm_out_{i}",
                               want_f32=(nxt is not None and (i + 1) % 3 == 2))
        saved.append(s)

    loss_acc, dh = loss_head(h, target, "loss_head")

    d_norm = [[None] * 4 for _ in range(depth)]
    d_lbs = jnp.zeros_like(lbs)
    d_onorm = [None] * onorm_g.shape[0]
    d_pscale = [None] * pool_scale.shape[0]
    d_cw, d_cb = [None] * depth, [None] * depth

    def exchanging(call, *args, **kw):
        plan = wts.exchange_plan()
        out, extra = call(*args, plan=plan, **kw)
        if plan is not None:
            wts.exchanged(extra)
        return out

    for i in reversed(range(depth)):
        kind, j = i % 3, i // 3
        s = saved[i]
        df, d_norm[i][3] = norm_bwd(s["f"], ng(i, 3), dh, None, BF16, f"nb_out_{i}")
        dact = matmul(df, wts.weight("w_down", i), "nt", F32, f"d_act_{i}")
        wts.grad("w_down", i, matmul(s["act"], df, "tn", BF16, f"dw_down_{i}"))
        dup, dcw, dcb = conv_glu_bwd(s["up"], conv_w[i], conv_b[i].reshape(1, -1), dact, f"glu_bwd_{i}")
        d_cw[i] = jnp.moveaxis(dcw, 0, 1).reshape(CONV_WIDTH, -1)
        d_cb[i] = dcb.reshape(-1)
        du2 = matmul(Split(dup), wts.weight("w_up", i), "nt", F32, f"d_u2_{i}")
        wts.grad("w_up", i, matmul(s["u2"], Split(dup), "tn", BF16, f"dw_up_{i}"))
        dh_mid, d_norm[i][2] = norm_bwd(s["h_mid"], ng(i, 2), du2, dh, F32, f"nb_mid_{i}")
        dm, d_norm[i][1] = norm_bwd(s["mix"], ng(i, 1), dh_mid, None, F32 if kind == 2 else BF16, f"nb_mix_{i}")
        if kind == 0:
            dy = matmul(dm, wts.weight("w_out", j), "nt", F32, f"d_y_{i}")
            wts.grad("w_out", j, matmul(s["y"], dm, "tn", BF16, f"dw_hgout_{i}"))
            dproj, d_onorm[j], dlb = exchanging(hgrn_bwd, s["proj"], lbs[i].reshape(1, -1),
                                                onorm_g[j].reshape(1, -1), dy, i > 0, name=f"hgrn_bwd_{i}")
            d_lbs = d_lbs.at[i].set(dlb[0])
            du1 = matmul(Split(dproj), wts.weight("w_in", j), "nt", F32, f"d_u1_{i}")
            wts.grad("w_in", j, matmul(s["u1"], Split(dproj), "tn", BF16, f"dw_hgin_{i}"))
        elif kind == 1:
            do = matmul(dm, wts.weight("w_so", j), "nt", F32, f"d_o_{i}")
            wts.grad("w_so", j, matmul(s["o"], dm, "tn", BF16, f"dw_sbout_{i}"))
            dqkv = exchanging(sba_bwd, s["qkv"], s["o"], do, name=f"sba_bwd_{i}")
            du1 = matmul(Split(dqkv), wts.weight("w_qkv", j), "nt", F32, f"d_u1_{i}")
            wts.grad("w_qkv", j, matmul(s["u1"], Split(dqkv), "tn", BF16, f"dw_sbqkv_{i}"))
        else:
            du1, g_pool, d_pscale[j] = pool_bwd(s["u1f"], wts.weight("w_pool", j), pool_scale[j].reshape(1, d),
                                                dm, f"pool_bwd_{i}")
            wts.grad("w_pool", j, g_pool)
        dh, d_norm[i][0] = norm_bwd(s["h_in"], ng(i, 0), du1, dh_mid, F32, f"nb_in_{i}")

    small = {
        "norm_g": jnp.stack([jnp.stack([v.reshape(d) for v in row]) for row in d_norm]),
        "lb_logits": lower_bounds_bwd(lb_logits, d_lbs),
        "onorm_g": jnp.stack([v.reshape(-1) for v in d_onorm]),
        "pool_scale": jnp.stack([v.reshape(-1) for v in d_pscale]),
        "conv_w": jnp.stack(d_cw),
        "conv_b": jnp.stack(d_cb),
    }
    return loss_acc, dh, small


SMALL_SHARDED = ("norm_g", "onorm_g", "pool_scale", "conv_w")
SMALL_ORDER = ("norm_g", "lb_logits", "onorm_g", "pool_scale", "conv_w", "conv_b")


def kernel(x, norm_g, hgrn_lb_logits, hgrn_w_in, hgrn_onorm_g, hgrn_w_out, sba_w_qkv, sba_w_out, pool_w, pool_scale, ffn_w_up, ffn_conv_w, ffn_conv_b, ffn_w_down, loss_target, m_norm_g, m_hgrn_lb_logits, m_hgrn_w_in, m_hgrn_onorm_g, m_hgrn_w_out, m_sba_w_qkv, m_sba_w_out, m_pool_w, m_pool_scale, m_ffn_w_up, m_ffn_conv_w, m_ffn_conv_b, m_ffn_w_down, v_norm_g, v_hgrn_lb_logits, v_hgrn_w_in, v_hgrn_onorm_g, v_hgrn_w_out, v_sba_w_qkv, v_sba_w_out, v_pool_w, v_pool_scale, v_ffn_w_up, v_ffn_conv_w, v_ffn_conv_b, v_ffn_w_down):
    cx, cy, cc = lax.axis_index("x"), lax.axis_index("y"), lax.axis_index("c")
    block = 4 * cx + 2 * cy + cc
    core = cc.astype(jnp.int32).reshape(1)

    pool3 = lambda a: a.reshape(a.shape[0] * a.shape[1], a.shape[2], a.shape[3])
    big_w = dict(zip(BIG_ORDER, [hgrn_w_in, hgrn_w_out, sba_w_qkv, sba_w_out, pool3(pool_w), ffn_w_up, ffn_w_down]))
    big_m = dict(zip(BIG_ORDER, [m_hgrn_w_in, m_hgrn_w_out, m_sba_w_qkv, m_sba_w_out, pool3(m_pool_w), m_ffn_w_up,
                                 m_ffn_w_down]))
    big_v = dict(zip(BIG_ORDER, [v_hgrn_w_in, v_hgrn_w_out, v_sba_w_qkv, v_sba_w_out, pool3(v_pool_w), v_ffn_w_up,
                                 v_ffn_w_down]))

    sharded = {"norm_g": norm_g, "onorm_g": hgrn_onorm_g, "pool_scale": pool_scale, "conv_w": ffn_conv_w}
    gathered = small_exchange(_pack([sharded[n] for n in SMALL_SHARDED]), False, "gather_small")
    parts = _unpack(gathered, [sharded[n].shape for n in SMALL_SHARDED])
    full = {n: _unshard_last(p) for n, p in zip(SMALL_SHARDED, parts)}

    wts = MeshWeights({k: w.astype(BF16) for k, w in big_w.items()}, big_w, big_m, big_v, core)
    loss_acc, grad_x, small_g = train_step(
        x[0], loss_target[0], full["norm_g"], hgrn_lb_logits, full["onorm_g"], full["pool_scale"],
        full["conv_w"], ffn_conv_b, wts)
    loss = lax.psum(loss_acc[0, 0], ("x", "y", "c"))

    shapes = [small_g[n].shape for n in SMALL_ORDER]
    summed = _unpack(small_exchange(_pack([small_g[n] for n in SMALL_ORDER]), True, "reduce_small"), shapes)
    sg = {n: (_shard_last(g, block) if n in SMALL_SHARDED else g) for n, g in zip(SMALL_ORDER, summed)}
    sw = {"norm_g": norm_g, "lb_logits": hgrn_lb_logits, "onorm_g": hgrn_onorm_g, "pool_scale": pool_scale,
          "conv_w": ffn_conv_w, "conv_b": ffn_conv_b}
    sm = {"norm_g": m_norm_g, "lb_logits": m_hgrn_lb_logits, "onorm_g": m_hgrn_onorm_g, "pool_scale": m_pool_scale,
          "conv_w": m_ffn_conv_w, "conv_b": m_ffn_conv_b}
    sv = {"norm_g": v_norm_g, "lb_logits": v_hgrn_lb_logits, "onorm_g": v_hgrn_onorm_g, "pool_scale": v_pool_scale,
          "conv_w": v_ffn_conv_w, "conv_b": v_ffn_conv_b}
    sshapes = [sw[n].shape for n in SMALL_ORDER]
    packed = [_pack([dct[n] for n in SMALL_ORDER]) for dct in (sw, sg, sm, sv)]
    s_delta, s_m, s_v = [dict(zip(SMALL_ORDER, _unpack(p, sshapes))) for p in adam_small(*packed)]

    upd = wts.finish()
    b_grad, b_delta, b_m, b_v = [[upd[k][n] for k in BIG_ORDER] for n in range(4)]

    def tree(small, bigs):
        bg = list(bigs)
        bg[4] = bg[4].reshape(pool_w.shape)
        return (small["norm_g"], small["lb_logits"], bg[0], small["onorm_g"], bg[1], bg[2], bg[3], bg[4],
                small["pool_scale"], bg[5], small["conv_w"], small["conv_b"], bg[6])

    return (loss, grad_x[None], *tree(sg, b_grad), *tree(s_delta, b_delta), *tree(s_m, b_m), *tree(s_v, b_v))
```

```python
import functools
import math

import jax
import jax.numpy as jnp
from jax import lax
from jax.experimental import pallas as pl
from jax.experimental.pallas import tpu as pltpu

F32 = jnp.float32
BF16 = jnp.bfloat16
HI = lax.Precision.HIGHEST
MESH = pl.DeviceIdType.MESH
ANY = pl.BlockSpec(memory_space=pl.ANY)
VMEM_FULL = pl.BlockSpec(memory_space=pltpu.VMEM)

NORM_EPS = 1e-6
HEAD = 128
HG_CHUNK = 128
HG_SUB = 16
SB_BLOCK = 128
SB_QROWS = 256
POOL_WINDOWS = (2, 4, 8, 16)
CONV_WIDTH = 3
ROW_TILE = 256
N_DEV = 8
N_CHIP = 4

ADAM_LR = 0.001
ADAM_B1 = 0.9
ADAM_B2 = 0.999
ADAM_EPS = 1e-08
ADAM_WD = 0.01
ADAM_STEP = 10

VMEM_LIMIT = 48 * 1024 * 1024
LANES = 128
ELEMWISE_BLOCK_ELEMS = 256 * 1024


def _params(sem=None, vmem=VMEM_LIMIT):
    return pltpu.CompilerParams(dimension_semantics=sem, vmem_limit_bytes=vmem)


def _tile(n, prefs=(1024, 512, 256, 128)):
    for p in prefs:
        if n % p == 0:
            return p
    return n


def _dot(a, b, prec=None):
    return jnp.dot(a, b, precision=prec, preferred_element_type=F32)


def _dot_nt(a, b, prec=None):
    return lax.dot_general(a, b, (((1,), (1,)), ((), ())), precision=prec, preferred_element_type=F32)


def _dot_tn(a, b, prec=None):
    return lax.dot_general(a, b, (((0,), (0,)), ((), ())), precision=prec, preferred_element_type=F32)


def _split_dot(x, tri, parts, left=False):
    tot, rest = None, x
    for p in range(parts):
        h = rest.astype(BF16)
        d = _dot(tri, h) if left else _dot(h, tri)
        tot = d if tot is None else tot + d
        if p + 1 < parts:
            rest = rest - h.astype(F32)
    return tot


def _dot1(a, b, fn):
    return fn(a.astype(BF16), b.astype(BF16))


def _dot3(a, b, fn):
    ah, bh = a.astype(BF16), b.astype(BF16)
    al, bl = (a - ah.astype(F32)).astype(BF16), (b - bh.astype(F32)).astype(BF16)
    return fn(ah, bh) + fn(ah, bl) + fn(al, bh)


def _sigmoid(x):
    return jax.nn.sigmoid(x)


def _softplus(x):
    return jnp.maximum(x, 0.0) + jnp.log1p(jnp.exp(-jnp.abs(x)))


class Layer:
    def __init__(self, arr, l):
        self.arr, self.l = arr, l
        self.shape = arr.shape[1:]
        self.part = None


class Split:
    def __init__(self, arr):
        self.arr = arr
        self.shape = (arr.shape[1], arr.shape[0] * arr.shape[2])
        self.part = arr.shape[2]


class Plain:
    def __init__(self, arr):
        self.arr = arr
        self.shape = arr.shape
        self.part = None


def _wrap(op):
    return op if isinstance(op, (Layer, Split, Plain)) else Plain(op)


def _op_spec(op, br, bc, rc_of_grid):
    if isinstance(op, Layer):
        l = op.l
        return pl.BlockSpec((None, br, bc), lambda i, j, k: (l, *rc_of_grid(i, j, k)))
    if isinstance(op, Split):
        per = op.part // bc

        def imap(i, j, k):
            r, c = rc_of_grid(i, j, k)
            return (lax.div(c, per), r, lax.rem(c, per))
        return pl.BlockSpec((None, br, bc), imap)
    return pl.BlockSpec((br, bc), rc_of_grid)


def _hosted(body, n_in, n_out, plan, step_of_grid):
    if plan is None:
        return body
    pi, po, ps = len(plan.args), len(plan.out_shape), len(plan.scratch)

    def wrapped(*refs):
        refs = list(refs)
        ins, pins = refs[:n_in], refs[n_in:n_in + pi]
        outs = refs[n_in + pi:n_in + pi + n_out]
        pouts = refs[n_in + pi + n_out:n_in + pi + n_out + po]
        scr, pscr = refs[n_in + pi + n_out + po:len(refs) - ps], refs[len(refs) - ps:]
        step, nsteps = step_of_grid()

        @pl.when(step == 0)
        def _():
            plan.start(pins, pouts, pscr)

        if plan.has_mid:
            @pl.when(step == min((3 * nsteps) // 4, nsteps - 1))
            def _():
                plan.mid(pins, pouts, pscr)

        body(*ins, *outs, *scr)

        @pl.when(step == nsteps - 1)
        def _():
            plan.finish(pins, pouts, pscr)

    return wrapped


def _host_call(body, n_in, plan, step_of_grid, *, out_shape, grid, in_specs, out_specs, scratch_shapes, name, sem, args):
    single = not isinstance(out_shape, (list, tuple))
    out_shape = [out_shape] if single else list(out_shape)
    out_specs = [out_specs] if single else list(out_specs)
    n_out = len(out_shape)
    in_specs, scratch_shapes, args = list(in_specs), list(scratch_shapes), list(args)
    if plan is not None:
        in_specs += [ANY] * len(plan.args)
        args += plan.args
        out_shape += plan.out_shape
        out_specs += [ANY] * len(plan.out_shape)
        scratch_shapes += plan.scratch
        sem = ("arbitrary",) * len(grid)
    outs = pl.pallas_call(
        _hosted(body, n_in, n_out, plan, step_of_grid), out_shape=out_shape, grid=grid,
        in_specs=in_specs, out_specs=out_specs, scratch_shapes=scratch_shapes,
        name=name, compiler_params=_params(sem),
    )(*args)
    host = outs[0] if single else list(outs[:n_out])
    return host, list(outs[n_out:])


def matmul(a, b, mode, out_dtype, name, plan=None):
    a, b = _wrap(a), _wrap(b)
    if mode == "nn":
        (m, kd), (kd2, n) = a.shape, b.shape
    elif mode == "nt":
        (m, kd), (n, kd2) = a.shape, b.shape
    else:
        (kd, m), (kd2, n) = a.shape, b.shape
    assert kd == kd2, (mode, a.shape, b.shape)

    def dim_tile(full, ops_on_cols, prefs=(1024, 512, 256, 128)):
        base = full
        for op in ops_on_cols:
            if op.part is not None:
                base = math.gcd(base, op.part)
        return _tile(base, prefs)

    tm = dim_tile(m, [a] if mode == "tn" else [])
    tn = dim_tile(n, [b] if mode in ("nn", "tn") else [])
    tk = dim_tile(kd, ([a] if mode in ("nn", "nt") else []) + ([b] if mode == "nt" else []),
                  prefs=(2048, 2816, 1024, 512, 256, 128))
    nk = kd // tk

    if mode == "nn":
        a_spec = _op_spec(a, tm, tk, lambda i, j, k: (i, k))
        b_spec = _op_spec(b, tk, tn, lambda i, j, k: (k, j))
        dot = _dot
    elif mode == "nt":
        a_spec = _op_spec(a, tm, tk, lambda i, j, k: (i, k))
        b_spec = _op_spec(b, tn, tk, lambda i, j, k: (j, k))
        dot = _dot_nt
    else:
        a_spec = _op_spec(a, tk, tm, lambda i, j, k: (k, i))
        b_spec = _op_spec(b, tk, tn, lambda i, j, k: (k, j))
        dot = _dot_tn

    def body(a_ref, b_ref, o_ref, *acc):
        part = dot(a_ref[...].astype(BF16), b_ref[...].astype(BF16))
        if nk == 1:
            o_ref[...] = part.astype(o_ref.dtype)
            return
        (acc_ref,) = acc
        k = pl.program_id(2)

        @pl.when(k == 0)
        def _():
            acc_ref[...] = part

        @pl.when(k > 0)
        def _():
            acc_ref[...] += part

        @pl.when(k == nk - 1)
        def _():
            o_ref[...] = acc_ref[...].astype(o_ref.dtype)

    gi, gj = m // tm, n // tn

    def step_of_grid():
        return (pl.program_id(0) * gj + pl.program_id(1)) * nk + pl.program_id(2), gi * gj * nk

    out, extra = _host_call(
        body, 2, plan, step_of_grid, out_shape=jax.ShapeDtypeStruct((m, n), out_dtype), grid=(gi, gj, nk),
        in_specs=[a_spec, b_spec], out_specs=pl.BlockSpec((tm, tn), lambda i, j, k: (i, j)),
        scratch_shapes=[pltpu.VMEM((tm, tn), F32)] if nk > 1 else [], name=name,
        sem=("parallel", "parallel", "arbitrary"), args=[a.arr, b.arr])
    return out if plan is None else (out, extra)


def _rms(x, g):
    r = lax.rsqrt(jnp.mean(x * x, axis=-1, keepdims=True) + NORM_EPS)
    return x * r * g


def res_norm(h, m, g_a, g_b, name, want_f32=False):
    t, d = h.shape
    tr = min(ROW_TILE, t)
    has_m, has_u = m is not None, g_b is not None
    row = pl.BlockSpec((tr, d), lambda i: (i, 0))
    vec = pl.BlockSpec((1, d), lambda i: (0, 0))

    def body(*refs):
        refs = list(refs)
        h_ref = refs.pop(0)
        hn = h_ref[...]
        if has_m:
            m_ref, ga_ref = refs.pop(0), refs.pop(0)
            hn = hn + _rms(m_ref[...], ga_ref[...])
        if has_u:
            gb_ref = refs.pop(0)
        if has_m:
            refs.pop(0)[...] = hn
        if has_u:
            u = _rms(hn, gb_ref[...])
            refs.pop(0)[...] = u.astype(BF16)
            if want_f32:
                refs.pop(0)[...] = u

    args, in_specs, out_shape, out_specs = [h], [row], [], []
    if has_m:
        args += [m, g_a]
        in_specs += [row, vec]
        out_shape.append(jax.ShapeDtypeStruct((t, d), F32))
        out_specs.append(row)
    if has_u:
        args.append(g_b)
        in_specs.append(vec)
        out_shape.append(jax.ShapeDtypeStruct((t, d), BF16))
        out_specs.append(row)
        if want_f32:
            out_shape.append(jax.ShapeDtypeStruct((t, d), F32))
            out_specs.append(row)
    outs = list(pl.pallas_call(
        body, out_shape=out_shape, grid=(t // tr,), in_specs=in_specs, out_specs=out_specs,
        name=name, compiler_params=_params(("parallel",)),
    )(*args))
    h_new = outs.pop(0) if has_m else None
    u16 = outs.pop(0) if has_u else None
    u32 = outs.pop(0) if (has_u and want_f32) else None
    return h_new, u16, u32


def norm_bwd(x, g, dy, add, out_dtype, name):
    t, d = x.shape
    tr = min(ROW_TILE, t)
    has_add = add is not None
    row = pl.BlockSpec((tr, d), lambda i: (i, 0))
    vec = pl.BlockSpec((1, d), lambda i: (0, 0))

    def body(*refs):
        if has_add:
            x_ref, g_ref, dy_ref, add_ref, dx_ref, dg_ref = refs
        else:
            x_ref, g_ref, dy_ref, dx_ref, dg_ref = refs
        xv = x_ref[...]
        dyv = dy_ref[...].astype(F32)
        r = lax.rsqrt(jnp.mean(xv * xv, axis=-1, keepdims=True) + NORM_EPS)
        gy = dyv * g_ref[...]
        dx = r * gy - xv * (r * r * r * jnp.mean(gy * xv, axis=-1, keepdims=True))
        if has_add:
            dx = dx + add_ref[...]
        dx_ref[...] = dx.astype(dx_ref.dtype)

        @pl.when(pl.program_id(0) == 0)
        def _():
            dg_ref[...] = jnp.zeros_like(dg_ref)

        dg_ref[...] += jnp.sum(dyv * xv * r, axis=0, keepdims=True)

    args = [x, g, dy] + ([add] if has_add else [])
    in_specs = [row, vec, row] + ([row] if has_add else [])
    return pl.pallas_call(
        body, out_shape=[jax.ShapeDtypeStruct((t, d), out_dtype), jax.ShapeDtypeStruct((1, d), F32)],
        grid=(t // tr,), in_specs=in_specs, out_specs=[row, vec],
        name=name, compiler_params=_params(("arbitrary",)),
    )(*args)


def loss_head(y, target, name):
    t, d = y.shape
    tr = min(ROW_TILE, t)
    row = pl.BlockSpec((tr, d), lambda i: (i, 0))
    acc = pl.BlockSpec((8, LANES), lambda i: (0, 0))

    def body(y_ref, t_ref, loss_ref, dy_ref):
        e = y_ref[...] - t_ref[...]
        dy_ref[...] = e * (1.0 / d)

        @pl.when(pl.program_id(0) == 0)
        def _():
            loss_ref[...] = jnp.zeros_like(loss_ref)

        loss_ref[...] += jnp.sum(e * e) * (0.5 / d)

    return pl.pallas_call(
        body, out_shape=[jax.ShapeDtypeStruct((8, LANES), F32), jax.ShapeDtypeStruct((t, d), F32)],
        grid=(t // tr,), in_specs=[row, row], out_specs=[acc, row],
        name=name, compiler_params=_params(("arbitrary",)),
    )(y, target)


def _depth_softmax(ref, depth):
    rows = [ref[i:i + 1, :] for i in range(depth)]
    mx = functools.reduce(jnp.maximum, rows)
    ex = [jnp.exp(r - mx) for r in rows]
    tot = functools.reduce(lambda p, q: p + q, ex)
    return [e / tot for e in ex]


def lower_bounds_fwd(logits):
    depth, kw = logits.shape

    def body(l_ref, o_ref):
        s = _depth_softmax(l_ref, depth)
        run = jnp.zeros_like(s[0])
        o_ref[0:1, :] = run
        for i in range(1, depth):
            run = run + s[i]
            o_ref[i:i + 1, :] = run

    return pl.pallas_call(body, out_shape=jax.ShapeDtypeStruct((depth, kw), F32), name="lb_fwd")(logits)


def lower_bounds_bwd(logits, dlb):
    depth, kw = logits.shape

    def body(l_ref, d_ref, o_ref):
        s = _depth_softmax(l_ref, depth)
        ds = [jnp.zeros_like(s[0]) for _ in range(depth)]
        run = jnp.zeros_like(s[0])
        for j in range(depth - 1, 0, -1):
            run = run + d_ref[j:j + 1, :]
            ds[j] = run
        dot = functools.reduce(lambda p, q: p + q, [s[j] * ds[j] for j in range(depth)])
        for j in range(depth):
            o_ref[j:j + 1, :] = s[j] * (ds[j] - dot)

    return pl.pallas_call(body, out_shape=jax.ShapeDtypeStruct((depth, kw), F32), name="lb_bwd")(logits, dlb)


def _hg_gates(qp, fp, lb_row, has_lb):
    sig = _sigmoid(fp)
    nsig = _sigmoid(-fp)
    ls = jnp.minimum(fp, 0.0) - jnp.log1p(jnp.exp(-jnp.abs(fp)))
    if has_lb:
        a = jnp.log(lb_row)
        bb = jnp.log1p(-lb_row) + ls
        g = jnp.maximum(a, bb) + jnp.log1p(jnp.exp(-jnp.abs(a - bb)))
        w = jnp.exp(bb - g)
        k = (1.0 - lb_row) * nsig
    else:
        g, w, k = ls, None, nsig
    q = qp * _sigmoid(qp)
    return q, k, g, sig, nsig, w


HG_HEADS_PER_STEP = 2


def _heads_per_step(nh):
    return HG_HEADS_PER_STEP if nh % HG_HEADS_PER_STEP == 0 else 1


def _head_views(refs, hh, lanes, lead):
    cs = pl.ds(hh * HEAD, HEAD)
    out = []
    for i, r in enumerate(refs):
        if i in lead:
            out.append(r.at[hh])
        elif i in lanes:
            out.append(r.at[(slice(None),) * (len(r.shape) - 1) + (cs,)])
        else:
            out.append(r)
    return out


def _hg_masks():
    c = HG_CHUNK
    row = lax.broadcasted_iota(jnp.int32, (c, c), 0)
    col = lax.broadcasted_iota(jnp.int32, (c, c), 1)
    lower = (col <= row).astype(BF16)
    upper = (col >= row).astype(BF16)
    krow = lax.broadcasted_iota(jnp.int32, (c, HEAD), 0)
    arow = lax.broadcasted_iota(jnp.int32, (HG_SUB, c), 0)
    acol = lax.broadcasted_iota(jnp.int32, (HG_SUB, c), 1)
    return lower, upper, krow, arow, acol


def _hg_sub(i, q, k, b, b_ref, krow, arow, acol):
    r0 = i * HG_SUB
    m = b_ref[r0:r0 + 1, :]
    ebq = jnp.exp(b[r0:r0 + HG_SUB, :] - m)
    qh = (q[r0:r0 + HG_SUB, :] * ebq).astype(BF16)
    ek = jnp.exp(jnp.where(krow < r0 + HG_SUB, m - b, 0.0))
    kh = (k * ek).astype(BF16)
    mask = acol <= arow + r0
    amat = jnp.where(mask, _dot_nt(qh, kh), 0.0).astype(BF16)
    return ebq, qh, ek, kh, mask, amat


def hgrn_fwd(proj, lb, onorm_g, has_lb, name, plan=None):
    t, w4 = proj.shape
    kw = w4 // 4
    nh, nc, c = kw // HEAD, t // HG_CHUNK, HG_CHUNK
    hpb = _heads_per_step(nh)
    wide = hpb * HEAD

    def body(*refs):
        refs[7][...] = jnp.zeros_like(refs[7])
        heads = [one_head(*_head_views(refs, hh, lanes=(0, 1, 2, 3, 4, 5, 6), lead=(7, 8))) for hh in range(hpb)]

        def chunks(ci, carry):
            for chunk in heads:
                chunk(ci)
            return carry

        lax.fori_loop(0, nc, chunks, 0)

    def one_head(qp_ref, fp_ref, iv_ref, gp_ref, lb_ref, on_ref, y_ref, st_ref, b_ref):
        lower, _, krow, arow, acol = _hg_masks()
        lb_row, gam = lb_ref[...], on_ref[...]

        def chunk(ci):
            rs = pl.ds(pl.multiple_of(ci * c, c), c)
            v = iv_ref[rs, :].astype(BF16)
            gp = gp_ref[rs, :]
            q, k, g, _, _, _ = _hg_gates(qp_ref[rs, :], fp_ref[rs, :], lb_row, has_lb)
            b = _split_dot(g, lower, 3, left=True)
            b_ref[...] = b
            st = st_ref[...]
            o = _dot_nt((q * jnp.exp(b)).astype(BF16), st.astype(BF16))
            parts = []
            for i in range(c // HG_SUB):
                amat = _hg_sub(i, q, k, b, b_ref, krow, arow, acol)[-1]
                parts.append(_dot(amat, v))
            o = o + jnp.concatenate(parts, axis=0)
            bl = b_ref[c - 1:c, :]
            st_ref[...] = jnp.exp(bl) * st + _dot_tn(v, (k * jnp.exp(bl - b)).astype(BF16))
            r = lax.rsqrt(jnp.mean(o * o, axis=-1, keepdims=True) + NORM_EPS)
            y_ref[rs, :] = (o * r * gam * (gp * _sigmoid(gp))).astype(BF16)

        return chunk

    ns = nh // hpb
    col = lambda p: pl.BlockSpec((t, wide), lambda h: (0, p * ns + h))
    vec = pl.BlockSpec((1, wide), lambda h: (0, h))
    return _host_call(
        body, 6, plan, lambda: (pl.program_id(0), ns),
        out_shape=jax.ShapeDtypeStruct((t, kw), BF16), grid=(ns,),
        in_specs=[col(0), col(1), col(2), col(3), vec, vec],
        out_specs=pl.BlockSpec((t, wide), lambda h: (0, h)),
        scratch_shapes=[pltpu.VMEM((hpb, HEAD, HEAD), F32), pltpu.VMEM((hpb, c, HEAD), F32)],
        name=name, sem=("parallel",), args=[proj, proj, proj, proj, lb, onorm_g])


def hgrn_bwd(proj, lb, onorm_g, dy, has_lb, name, plan=None):
    t, w4 = proj.shape
    kw = w4 // 4
    nh, nc, c = kw // HEAD, t // HG_CHUNK, HG_CHUNK
    hpb = _heads_per_step(nh)
    wide = hpb * HEAD

    def body(*refs):
        for i in (8, 9, 13, 14):
            refs[i][...] = jnp.zeros_like(refs[i])
        heads = [one_head(*_head_views(refs, hh, lanes=(0, 1, 2, 3, 4, 5, 6, 7, 8, 9, 11), lead=(10, 12, 13, 14)))
                 for hh in range(hpb)]

        def fwd_chunks(ci, states):
            return tuple(fwd(ci, st) for (fwd, _), st in zip(heads, states))

        lax.fori_loop(0, nc, fwd_chunks, tuple(jnp.zeros((HEAD, HEAD), F32) for _ in heads))

        def bwd_chunks(step, carry):
            for _, bwd in heads:
                bwd(nc - 1 - step)
            return carry

        lax.fori_loop(0, nc, bwd_chunks, 0)

    def one_head(qp_ref, fp_ref, iv_ref, gp_ref, lb_ref, on_ref, dy_ref,
                 dp_ref, dgam_ref, dlb_ref, sst_ref, o_ref, b_ref, dst_ref, car_ref):
        lower, upper, krow, arow, acol = _hg_masks()
        lb_row, gam = lb_ref[...], on_ref[...]
        mm = _dot3 if has_lb else _dot1

        def recompute(ci):
            rs = pl.ds(pl.multiple_of(ci * c, c), c)
            gates = _hg_gates(qp_ref[rs, :], fp_ref[rs, :], lb_row, has_lb)
            b = _split_dot(gates[2], lower, 3, left=True)
            b_ref[...] = b
            return rs, gates, b

        def fwd_chunk(ci, carry):
            rs, (q, k, g, _, _, _), b = recompute(ci)
            v = iv_ref[rs, :].astype(BF16)
            st = carry
            sst_ref[ci] = st
            o = _dot_nt((q * jnp.exp(b)).astype(BF16), st.astype(BF16))
            parts = []
            for i in range(c // HG_SUB):
                amat = _hg_sub(i, q, k, b, b_ref, krow, arow, acol)[-1]
                parts.append(_dot(amat, v))
            o_ref[rs, :] = o + jnp.concatenate(parts, axis=0)
            bl = b_ref[c - 1:c, :]
            return jnp.exp(bl) * st + mm(iv_ref[rs, :], k * jnp.exp(bl - b), _dot_tn)

        def bwd_chunk(ci):
            rs, (q, k, g, sig, nsig, w), b = recompute(ci)
            qp, gp, v32 = qp_ref[rs, :], gp_ref[rs, :], iv_ref[rs, :]
            v = v32.astype(BF16)
            dyv, o = dy_ref[rs, :], o_ref[rs, :]
            st = sst_ref[ci]
            dst_new = dst_ref[...]
            r = lax.rsqrt(jnp.mean(o * o, axis=-1, keepdims=True) + NORM_EPS)
            on = o * r
            sgm = _sigmoid(gp)
            sg = gp * sgm
            dgam_ref[...] += jnp.sum(dyv * sg * on, axis=0, keepdims=True)
            dgp = dyv * on * gam * (sgm * (1.0 + gp * (1.0 - sgm)))
            dn = dyv * gam * sg
            do = r * dn - o * (r * r * r * jnp.mean(dn * o, axis=-1, keepdims=True))
            eb = jnp.exp(b)
            bl = b_ref[c - 1:c, :]
            ekd = jnp.exp(bl - b)
            do16 = do.astype(BF16)
            dq = mm(do, st, _dot) * eb
            dv = _dot1(k * ekd, dst_new, _dot_nt)
            dk = mm(v32, dst_new, _dot) * ekd
            dst_ref[...] = jnp.exp(bl) * dst_new + mm(do, q * eb, _dot_tn)
            dq_parts = []
            for i in range(c // HG_SUB):
                r0 = i * HG_SUB
                ebq, _, ek, _, mask, amat = _hg_sub(i, q, k, b, b_ref, krow, arow, acol)
                da = jnp.where(mask, mm(do[r0:r0 + HG_SUB, :], v32, _dot_nt), 0.0)
                dq_parts.append(mm(da, k * ek, _dot) * ebq)
                dk = dk + mm(da, q[r0:r0 + HG_SUB, :] * ebq, _dot_tn) * ek
                dv = dv + _dot_tn(amat, do16[r0:r0 + HG_SUB, :])
            dq = dq + jnp.concatenate(dq_parts, axis=0)
            db = q * dq - k * dk
            dg = car_ref[...] + _split_dot(db, upper, 3, left=True)
            car_ref[...] += jnp.sum(db, axis=0, keepdims=True)
            if has_lb:
                dfp = dg * nsig * w - dk * ((1.0 - lb_row) * sig * nsig)
                dlb_ref[...] += jnp.sum(dg * nsig * jnp.exp(-g) - dk * nsig, axis=0, keepdims=True)
            else:
                dfp = dg * nsig - dk * (sig * nsig)
            sq = _sigmoid(qp)
            dp_ref[0, rs, :] = (dq * (sq * (1.0 + qp * (1.0 - sq)))).astype(BF16)
            dp_ref[1, rs, :] = dfp.astype(BF16)
            dp_ref[2, rs, :] = dv.astype(BF16)
            dp_ref[3, rs, :] = dgp.astype(BF16)

        return fwd_chunk, bwd_chunk

    ns = nh // hpb
    col = lambda p: pl.BlockSpec((t, wide), lambda h: (0, p * ns + h))
    vec = pl.BlockSpec((1, wide), lambda h: (0, h))
    return _host_call(
        body, 7, plan, lambda: (pl.program_id(0), ns),
        out_shape=[jax.ShapeDtypeStruct((4, t, kw), BF16), jax.ShapeDtypeStruct((1, kw), F32),
                   jax.ShapeDtypeStruct((1, kw), F32)],
        grid=(ns,),
        in_specs=[col(0), col(1), col(2), col(3), vec, vec, pl.BlockSpec((t, wide), lambda h: (0, h))],
        out_specs=[pl.BlockSpec((4, t, wide), lambda h: (0, 0, h)), vec, vec],
        scratch_shapes=[pltpu.VMEM((hpb, nc, HEAD, HEAD), F32), pltpu.VMEM((t, wide), F32),
                        pltpu.VMEM((hpb, c, HEAD), F32), pltpu.VMEM((hpb, HEAD, HEAD), F32),
                        pltpu.VMEM((hpb, 1, HEAD), F32)],
        name=name, sem=("parallel",), args=[proj, proj, proj, proj, lb, onorm_g, dy])


def _sb_masks():
    m, n = SB_QROWS, SB_BLOCK
    row = lax.broadcasted_iota(jnp.int32, (m, n), 0)
    col = lax.broadcasted_iota(jnp.int32, (m, n), 1)
    r2 = lax.broadcasted_iota(jnp.int32, (n, n), 0)
    c2 = lax.broadcasted_iota(jnp.int32, (n, n), 1)
    after = (r2 > c2).astype(BF16)
    from_ = (r2 >= c2).astype(BF16)
    return row, col, after, from_


def sba_fwd(qkv, name, plan=None):
    t, w3 = qkv.shape
    wd = w3 // 3
    m, n = SB_QROWS, SB_BLOCK
    nh, nq, per = wd // HEAD, t // m, m // n
    scale = HEAD ** -0.5

    def body(q_ref, k_ref, v_ref, o_ref, q16, k16, v16):
        row, col, after, _ = _sb_masks()
        q16[...] = (q_ref[...] * scale).astype(BF16)
        k16[...] = k_ref[...].astype(BF16)
        v16[...] = v_ref[...].astype(BF16)

        def qblock(qi, carry):
            qs = pl.ds(pl.multiple_of(qi * m, m), m)
            q = q16[qs, :]
            last = (qi + 1) * per - 1

            def kblock(step, state):
                acc, rem0 = state
                kj = last - step
                ks = pl.ds(pl.multiple_of(kj * n, n), n)
                z = _dot_nt(q, k16[ks, :])
                strict = (col + kj * n) < (row + qi * m)
                spz = _softplus(z)
                sp = jnp.where(strict, spz, 0.0)
                rem = rem0 + _split_dot(sp, after, 2)
                a = jnp.where(strict, jnp.exp(z - spz - rem), 0.0)
                acc = acc + _dot(a.astype(BF16), v16[ks, :])
                return acc, rem0 + jnp.sum(sp, axis=1, keepdims=True)

            acc, _ = lax.fori_loop(0, last + 1, kblock,
                                   (jnp.zeros((m, HEAD), F32), jnp.zeros((m, 1), F32)))
            o_ref[qs, :] = acc
            return carry

        lax.fori_loop(0, nq, qblock, 0)

    col_spec = lambda p: pl.BlockSpec((t, HEAD), lambda h: (0, p * nh + h))
    return _host_call(
        body, 3, plan, lambda: (pl.program_id(0), nh),
        out_shape=jax.ShapeDtypeStruct((t, wd), F32), grid=(nh,),
        in_specs=[col_spec(0), col_spec(1), col_spec(2)],
        out_specs=pl.BlockSpec((t, HEAD), lambda h: (0, h)),
        scratch_shapes=[pltpu.VMEM((t, HEAD), BF16)] * 3,
        name=name, sem=("parallel",), args=[qkv, qkv, qkv])


def sba_bwd(qkv, o, do, name, plan=None):
    t, w3 = qkv.shape
    wd = w3 // 3
    m, n = SB_QROWS, SB_BLOCK
    nh, nq, per = wd // HEAD, t // m, m // n
    scale = HEAD ** -0.5

    def body(q_ref, k_ref, v_ref, o_ref, do_ref, d_ref, dk_ref, dv_ref, q16, k16, v16):
        row, col, after, from_ = _sb_masks()
        dk_ref[...] = jnp.zeros_like(dk_ref)
        dv_ref[...] = jnp.zeros_like(dv_ref)
        q16[...] = (q_ref[...] * scale).astype(BF16)
        k16[...] = k_ref[...].astype(BF16)
        v16[...] = v_ref[...].astype(BF16)

        def qblock(qi, carry):
            qs = pl.ds(pl.multiple_of(qi * m, m), m)
            q = q16[qs, :]
            dov = do_ref[qs, :]
            do16 = dov.astype(BF16)
            dsum = jnp.sum(do16.astype(F32) * o_ref[qs, :], axis=1, keepdims=True)
            last = (qi + 1) * per - 1

            def kblock(step, state):
                dq, rem0, e0 = state
                kj = last - step
                ks = pl.ds(pl.multiple_of(kj * n, n), n)
                kv, vv = k16[ks, :], v16[ks, :]
                z = _dot_nt(q, kv)
                strict = (col + kj * n) < (row + qi * m)
                spz = _softplus(z)
                sp = jnp.where(strict, spz, 0.0)
                rem = rem0 + _split_dot(sp, after, 2)
                sgz = jnp.exp(z - spz)
                a = jnp.where(strict, sgz * jnp.exp(-rem), 0.0).astype(BF16)
                e = a.astype(F32) * _dot_nt(do16, vv)
                left = dsum - (e0 + _split_dot(e, from_, 2))
                dz = jnp.where(strict, e * (1.0 - sgz) - sgz * left, 0.0).astype(BF16)
                dq = dq + _dot(dz, kv)
                dk_ref[ks, :] += _dot_tn(dz, q)
                dv_ref[ks, :] += _dot_tn(a, do16)
                return (dq, rem0 + jnp.sum(sp, axis=1, keepdims=True),
                        e0 + jnp.sum(e, axis=1, keepdims=True))

            zero1 = jnp.zeros((m, 1), F32)
            dq, _, _ = lax.fori_loop(0, last + 1, kblock, (jnp.zeros((m, HEAD), F32), zero1, zero1))
            d_ref[0, qs, :] = (dq * scale).astype(BF16)
            return carry

        lax.fori_loop(0, nq, qblock, 0)
        d_ref[1, :, :] = dk_ref[...].astype(BF16)
        d_ref[2, :, :] = dv_ref[...].astype(BF16)

    col_spec = lambda p: pl.BlockSpec((t, HEAD), lambda h: (0, p * nh + h))
    head = pl.BlockSpec((t, HEAD), lambda h: (0, h))
    return _host_call(
        body, 5, plan, lambda: (pl.program_id(0), nh),
        out_shape=jax.ShapeDtypeStruct((3, t, wd), BF16), grid=(nh,),
        in_specs=[col_spec(0), col_spec(1), col_spec(2), head, head],
        out_specs=pl.BlockSpec((3, t, HEAD), lambda h: (0, 0, h)),
        scratch_shapes=[pltpu.VMEM((t, HEAD), F32)] * 2 + [pltpu.VMEM((t, HEAD), BF16)] * 3,
        name=name, sem=("parallel",), args=[qkv, qkv, qkv, o, do])


def _pool_band(i_out, i_in, tr, win, transpose):
    r = lax.broadcasted_iota(jnp.int32, (tr, tr), 0) + i_out * tr
    c = lax.broadcasted_iota(jnp.int32, (tr, tr), 1) + i_in * tr
    if transpose:
        return ((r <= c) & (r > c - win)).astype(F32)
    return ((c <= r) & (c > r - win)).astype(F32)


def _pool_p(u_ref, i, tr, win):
    cur = u_ref[i * tr:(i + 1) * tr, :]
    ws = _dot(_pool_band(i, i, tr, win, False), cur, HI)
    if i > 0:
        ws = ws + _dot(_pool_band(i, i - 1, tr, win, False), u_ref[(i - 1) * tr:i * tr, :], HI)
    pos = lax.broadcasted_iota(jnp.int32, (tr, 1), 0) + (i * tr + 1)
    count = jnp.minimum(pos, win).astype(F32)
    return ws / count - cur, count


def pool_fwd(u, pool_w, pool_scale, name):
    t, d = u.shape
    ng = len(POOL_WINDOWS)
    gs = d // ng
    tr = min(ROW_TILE, t)

    def body(u_ref, w_ref, s_ref, y_ref):
        win = jnp.left_shift(2, pl.program_id(0))
        for i in range(t // tr):
            p, _ = _pool_p(u_ref, i, tr, win)
            y_ref[i * tr:(i + 1) * tr, :] = _dot(p.astype(BF16), w_ref[...]) * s_ref[...]

    grp = pl.BlockSpec((t, gs), lambda g: (0, g))
    return pl.pallas_call(
        body, out_shape=jax.ShapeDtypeStruct((t, d), F32), grid=(ng,),
        in_specs=[grp, pl.BlockSpec((None, gs, gs), lambda g: (g, 0, 0)), pl.BlockSpec((1, gs), lambda g: (0, g))],
        out_specs=grp, name=name, compiler_params=_params(("parallel",)),
    )(u, pool_w, pool_scale)


def pool_bwd(u, pool_w, pool_scale, dy, name):
    t, d = u.shape
    ng = len(POOL_WINDOWS)
    gs = d // ng
    tr = min(ROW_TILE, t)
    nt = t // tr

    def body(u_ref, w_ref, s_ref, dy_ref, du_ref, dw_ref, ds_ref, dpc_ref, dp_ref):
        win = jnp.left_shift(2, pl.program_id(0))
        wv = w_ref[...]
        dw = jnp.zeros((gs, gs), F32)
        dsc = jnp.zeros((1, gs), F32)
        for i in range(nt):
            rows = slice(i * tr, (i + 1) * tr)
            p, count = _pool_p(u_ref, i, tr, win)
            p16 = p.astype(BF16)
            dyv = dy_ref[rows, :]
            dsc = dsc + jnp.sum(dyv * _dot(p16, wv), axis=0, keepdims=True)
            dyp = (dyv * s_ref[...]).astype(BF16)
            dw = dw + _dot_tn(p16, dyp)
            dp = _dot_nt(dyp, wv)
            dp_ref[rows, :] = dp
            dpc_ref[rows, :] = dp / count
        dw_ref[...] = dw.astype(BF16)
        ds_ref[...] = dsc
        for i in range(nt):
            rows = slice(i * tr, (i + 1) * tr)
            acc = _dot(_pool_band(i, i, tr, win, True), dpc_ref[rows, :], HI)
            if i + 1 < nt:
                acc = acc + _dot(_pool_band(i, i + 1, tr, win, True), dpc_ref[(i + 1) * tr:(i + 2) * tr, :], HI)
            du_ref[rows, :] = acc - dp_ref[rows, :]

    grp = pl.BlockSpec((t, gs), lambda g: (0, g))
    wspec = pl.BlockSpec((None, gs, gs), lambda g: (g, 0, 0))
    vec = pl.BlockSpec((1, gs), lambda g: (0, g))
    return pl.pallas_call(
        body,
        out_shape=[jax.ShapeDtypeStruct((t, d), F32), jax.ShapeDtypeStruct((ng, gs, gs), BF16),
                   jax.ShapeDtypeStruct((1, d), F32)],
        grid=(ng,), in_specs=[grp, wspec, vec, grp], out_specs=[grp, wspec, vec],
        scratch_shapes=[pltpu.VMEM((t, gs), F32), pltpu.VMEM((t, gs), F32)],
        name=name, compiler_params=_params(("parallel",)),
    )(u, pool_w, pool_scale, dy)


CONV_COLS = 256
HALO = 8


def _conv_taps(ref, r0, tr):
    x = ref[r0:r0 + tr, :]
    prev = ref[r0 - HALO:r0, :] if r0 > 0 else jnp.zeros((HALO, x.shape[1]), F32)
    xx = jnp.concatenate([prev, x], axis=0)
    return x, pltpu.roll(xx, 1, 0)[HALO:, :], pltpu.roll(xx, 2, 0)[HALO:, :]


def _conv_out(taps, w_ref, b_ref):
    x, s1, s2 = taps
    return w_ref[0:1, :] * s2 + w_ref[1:2, :] * s1 + w_ref[2:3, :] * x + b_ref[...]


def conv_glu_fwd(up, conv_w, conv_b, name):
    t, f2 = up.shape
    f = f2 // 2
    tc = min(CONV_COLS, f)
    nj = f // tc
    tr = min(ROW_TILE, t)

    def body(ug_ref, uv_ref, wg_ref, wv_ref, bg_ref, bv_ref, o_ref):
        for i in range(t // tr):
            r0 = i * tr
            gate = _conv_out(_conv_taps(ug_ref, r0, tr), wg_ref, bg_ref)
            val = _conv_out(_conv_taps(uv_ref, r0, tr), wv_ref, bv_ref)
            o_ref[r0:r0 + tr, :] = (gate * _sigmoid(gate) * val).astype(BF16)

    blk = lambda rows, half: pl.BlockSpec((rows, tc), lambda j: (0, half * nj + j))
    return pl.pallas_call(
        body, out_shape=jax.ShapeDtypeStruct((t, f), BF16), grid=(nj,),
        in_specs=[blk(t, 0), blk(t, 1), blk(CONV_WIDTH, 0), blk(CONV_WIDTH, 1), blk(1, 0), blk(1, 1)],
        out_specs=pl.BlockSpec((t, tc), lambda j: (0, j)),
        name=name, compiler_params=_params(("parallel",)),
    )(up, up, conv_w, conv_w, conv_b, conv_b)


def conv_glu_bwd(up, conv_w, conv_b, dact, name):
    t, f2 = up.shape
    f = f2 // 2
    tc = min(CONV_COLS, f)
    nj = f // tc
    tr = min(ROW_TILE, t)
    nt = t // tr

    def body(ug_ref, uv_ref, wg_ref, wv_ref, bg_ref, bv_ref, da_ref, du_ref, dw_ref, db_ref, dg_ref, dv_ref):
        dwg = [jnp.zeros((1, tc), F32) for _ in range(CONV_WIDTH)]
        dwv = [jnp.zeros((1, tc), F32) for _ in range(CONV_WIDTH)]
        dbg = jnp.zeros((1, tc), F32)
        dbv = jnp.zeros((1, tc), F32)
        for i in range(nt):
            r0 = i * tr
            tg, tv = _conv_taps(ug_ref, r0, tr), _conv_taps(uv_ref, r0, tr)
            gate, val = _conv_out(tg, wg_ref, bg_ref), _conv_out(tv, wv_ref, bv_ref)
            sg = _sigmoid(gate)
            da = da_ref[r0:r0 + tr, :]
            d_gate = da * val * (sg * (1.0 + gate * (1.0 - sg)))
            d_val = da * (gate * sg)
            dg_ref[r0:r0 + tr, :] = d_gate
            dv_ref[r0:r0 + tr, :] = d_val
            dbg = dbg + jnp.sum(d_gate, axis=0, keepdims=True)
            dbv = dbv + jnp.sum(d_val, axis=0, keepdims=True)
            for tap in range(CONV_WIDTH):
                dwg[tap] = dwg[tap] + jnp.sum(d_gate * tg[2 - tap], axis=0, keepdims=True)
                dwv[tap] = dwv[tap] + jnp.sum(d_val * tv[2 - tap], axis=0, keepdims=True)
        for tap in range(CONV_WIDTH):
            dw_ref[0, tap:tap + 1, :] = dwg[tap]
            dw_ref[1, tap:tap + 1, :] = dwv[tap]
        db_ref[0, :, :] = dbg
        db_ref[1, :, :] = dbv
        for half, (d_ref, w_ref) in enumerate(((dg_ref, wg_ref), (dv_ref, wv_ref))):
            for i in range(nt):
                r0 = i * tr
                x = d_ref[r0:r0 + tr, :]
                nxt = d_ref[r0 + tr:r0 + tr + HALO, :] if i + 1 < nt else jnp.zeros((HALO, tc), F32)
                xx = jnp.concatenate([x, nxt], axis=0)
                up1 = pltpu.roll(xx, tr + HALO - 1, 0)[:tr, :]
                up2 = pltpu.roll(xx, tr + HALO - 2, 0)[:tr, :]
                du = w_ref[2:3, :] * x + w_ref[1:2, :] * up1 + w_ref[0:1, :] * up2
                du_ref[half, r0:r0 + tr, :] = du.astype(BF16)

    blk = lambda rows, half: pl.BlockSpec((rows, tc), lambda j: (0, half * nj + j))
    return pl.pallas_call(
        body,
        out_shape=[jax.ShapeDtypeStruct((2, t, f), BF16), jax.ShapeDtypeStruct((2, CONV_WIDTH, f), F32),
                   jax.ShapeDtypeStruct((2, 1, f), F32)],
        grid=(nj,),
        in_specs=[blk(t, 0), blk(t, 1), blk(CONV_WIDTH, 0), blk(CONV_WIDTH, 1), blk(1, 0), blk(1, 1),
                  pl.BlockSpec((t, tc), lambda j: (0, j))],
        out_specs=[pl.BlockSpec((2, t, tc), lambda j: (0, 0, j)),
                   pl.BlockSpec((2, CONV_WIDTH, tc), lambda j: (0, 0, j)),
                   pl.BlockSpec((2, 1, tc), lambda j: (0, 0, j))],
        scratch_shapes=[pltpu.VMEM((t, tc), F32), pltpu.VMEM((t, tc), F32)],
        name=name, compiler_params=_params(("parallel",)),
    )(up, up, conv_w, conv_w, conv_b, conv_b, dact)


def _place():
    x, y, c = lax.axis_index("x"), lax.axis_index("y"), lax.axis_index("c")
    others = [(1 - x, y), (x, 1 - y), (1 - x, 1 - y)]
    return x, y, c, others


def _window(ref, axis, b, n):
    if axis == 1:
        return ref.at[:, pl.ds(b * n, n), :]
    return ref.at[:, :, pl.ds(b * n, n)]


def _remote(src, dst, send_sems, recv_sems, k, to):
    return pltpu.make_async_remote_copy(src_ref=src, dst_ref=dst, send_sem=send_sems.at[k],
                                        recv_sem=recv_sems.at[k], device_id=to, device_id_type=MESH)


class GatherPlan:
    has_mid = True

    def __init__(self, items):
        self.items = items
        self.args = [s for s, _, _, _ in items]
        self.out_shape = []
        for s, _, nl, ax in items:
            shp = [nl, s.shape[1], s.shape[2]]
            shp[ax] *= N_DEV
            self.out_shape.append(jax.ShapeDtypeStruct(tuple(shp), s.dtype))
        n = len(items)
        self.scratch = [pltpu.SemaphoreType.DMA((7 * n,)), pltpu.SemaphoreType.DMA((7 * n,)),
                        pltpu.SemaphoreType.DMA((n,))]

    def _mine(self, ins, a):
        _, l0, nl, _ = self.items[a]
        return ins[a].at[pl.ds(l0, nl)]

    def _copy(self, ins, outs, sems, a, k, block, to, own=False):
        s, _, _, ax = self.items[a]
        px, py, pc = block
        w = _window(outs[a], ax, 4 * px + 2 * py + pc, s.shape[ax])
        return _remote(self._mine(ins, a) if own else w, w, sems[0], sems[1], 7 * a + k, to)

    def _local(self, ins, outs, sems, a, x, y, c):
        s, _, _, ax = self.items[a]
        return pltpu.make_async_copy(self._mine(ins, a), _window(outs[a], ax, 4 * x + 2 * y + c, s.shape[ax]),
                                     sems[2].at[a])

    def _first(self, ins, outs, sems, a, x, y, c, others):
        me = (x, y, c)
        return [self._copy(ins, outs, sems, a, 0, me, (x, y, 1 - c), own=True)] + [
            self._copy(ins, outs, sems, a, 1 + j, me, (*chip, c), own=True) for j, chip in enumerate(others)]

    def start(self, ins, outs, sems):
        x, y, c, others = _place()
        for a in range(len(self.items)):
            self._local(ins, outs, sems, a, x, y, c).start()
        for a in range(len(self.items)):
            for cp in self._first(ins, outs, sems, a, x, y, c, others):
                cp.start()

    def mid(self, ins, outs, sems):
        x, y, c, others = _place()
        for a in range(len(self.items)):
            for j, chip in enumerate(others):
                self._copy(ins, outs, sems, a, 1 + j, (*chip, c), (x, y, c)).wait_recv()
                self._copy(ins, outs, sems, a, 4 + j, (*chip, c), (x, y, 1 - c)).start()

    def finish(self, ins, outs, sems):
        x, y, c, others = _place()
        for a in range(len(self.items)):
            self._copy(ins, outs, sems, a, 0, (x, y, 1 - c), (x, y, c)).wait_recv()
            for j, chip in enumerate(others):
                self._copy(ins, outs, sems, a, 4 + j, (*chip, 1 - c), (x, y, c)).wait_recv()
        for a in range(len(self.items)):
            for cp in self._first(ins, outs, sems, a, x, y, c, others):
                cp.wait_send()
            for j, chip in enumerate(others):
                self._copy(ins, outs, sems, a, 4 + j, (*chip, c), (x, y, 1 - c)).wait_send()
            self._local(ins, outs, sems, a, x, y, c).wait()


class ExchangePlan:
    has_mid = False

    def __init__(self, partials):
        self.args = list(partials)
        self.out_shape = [jax.ShapeDtypeStruct(p.shape, p.dtype) for p in partials]
        n = len(partials)
        self.scratch = [pltpu.SemaphoreType.DMA((3 * n,)), pltpu.SemaphoreType.DMA((3 * n,)),
                        pltpu.SemaphoreType.DMA((n,))]

    def _copies(self, ins, outs, sems):
        x, y, c, others = _place()
        me = 2 * x + y
        local, sends, recvs = [], [], []
        for a in range(len(self.args)):
            local.append(pltpu.make_async_copy(ins[a].at[me], outs[a].at[me], sems[2].at[a]))
            for j, (px, py) in enumerate(others):
                sends.append(_remote(ins[a].at[2 * px + py], outs[a].at[me], sems[0], sems[1], 3 * a + j, (px, py, c)))
                slot = outs[a].at[2 * px + py]
                recvs.append(_remote(slot, slot, sems[0], sems[1], 3 * a + j, (px, py, c)))
        return local, sends, recvs

    def start(self, ins, outs, sems):
        local, sends, _ = self._copies(ins, outs, sems)
        for cp in local + sends:
            cp.start()

    def finish(self, ins, outs, sems):
        local, sends, recvs = self._copies(ins, outs, sems)
        for cp in recvs:
            cp.wait_recv()
        for cp in sends:
            cp.wait_send()
        for cp in local:
            cp.wait()


def run_plan(plan, name):
    ni, no = len(plan.args), len(plan.out_shape)

    def body(*refs):
        ins, outs, sems = refs[:ni], refs[ni:ni + no], refs[ni + no:]
        plan.start(ins, outs, sems)
        if plan.has_mid:
            plan.mid(ins, outs, sems)
        plan.finish(ins, outs, sems)

    return pl.pallas_call(
        body, out_shape=plan.out_shape, in_specs=[ANY] * ni, out_specs=[ANY] * no,
        scratch_shapes=plan.scratch, name=name,
    )(*plan.args)


def sibling_exchange(grads, axes):
    na = len(grads)
    widths = [g.shape[ax] // N_DEV for g, ax in zip(grads, axes)]
    out_shape = []
    for g, ax, n in zip(grads, axes, widths):
        shp = list(g.shape)
        shp[ax] = n
        out_shape.append(jax.ShapeDtypeStruct((N_CHIP, *shp), g.dtype))

    def body(*refs):
        ins, outs = refs[:na], refs[na:2 * na]
        send_sems, recv_sems = refs[2 * na:]
        x, y, c, _ = _place()
        copies = []
        for a in range(na):
            for q in range(N_CHIP):
                src = _window(ins[a], axes[a], 2 * q + (1 - c), widths[a])
                cp = _remote(src, outs[a].at[q], send_sems, recv_sems, N_CHIP * a + q, (x, y, 1 - c))
                cp.start()
                copies.append(cp)
        for cp in copies:
            cp.wait_recv()
        for cp in copies:
            cp.wait_send()

    return pl.pallas_call(
        body, out_shape=out_shape, in_specs=[ANY] * na, out_specs=[ANY] * na,
        scratch_shapes=[pltpu.SemaphoreType.DMA((N_CHIP * na,)), pltpu.SemaphoreType.DMA((N_CHIP * na,))],
        name="sibling_exchange",
    )(*grads)


def _peer_of(k, x, y, c):
    return (1 - x if k & 4 else x, 1 - y if k & 2 else y, 1 - c if k & 1 else c)


def small_exchange(vec, reduce, name):
    r = vec.shape[0]

    def body(v_ref, o_ref, *scratch):
        if reduce:
            buf, send_sems, recv_sems = scratch
        else:
            buf, (send_sems, recv_sems) = o_ref, scratch
        x, y, c, _ = _place()
        me = 4 * x + 2 * y + c
        copies = []
        for k in range(1, N_DEV):
            cp = _remote(v_ref, buf.at[me], send_sems, recv_sems, k - 1, _peer_of(k, x, y, c))
            cp.start()
            copies.append(cp)
        buf[me] = v_ref[...]
        for k in range(1, N_DEV):
            px, py, pc = _peer_of(k, x, y, c)
            slot = buf.at[4 * px + 2 * py + pc]
            _remote(slot, slot, send_sems, recv_sems, k - 1, (px, py, pc)).wait_recv()
        for cp in copies:
            cp.wait_send()
        if reduce:
            tot = buf[0]
            for b in range(1, N_DEV):
                tot = tot + buf[b]
            o_ref[...] = tot

    sems = [pltpu.SemaphoreType.DMA((N_DEV - 1,)), pltpu.SemaphoreType.DMA((N_DEV - 1,))]
    if reduce:
        out_shape = jax.ShapeDtypeStruct((r, LANES), F32)
        scratch = [pltpu.VMEM((N_DEV, r, LANES), F32)] + sems
    else:
        out_shape = jax.ShapeDtypeStruct((N_DEV, r, LANES), F32)
        scratch = sems
    return pl.pallas_call(
        body, out_shape=out_shape, in_specs=[VMEM_FULL], out_specs=VMEM_FULL,
        scratch_shapes=scratch, name=name, compiler_params=_params(None),
    )(vec)


def _row_tile(rows, cols):
    best = None
    for tb in range(16, rows + 1, 16):
        if rows % tb == 0 and tb * cols <= ELEMWISE_BLOCK_ELEMS:
            best = tb
    return best if best is not None else rows


def add_sibling(grad, recv, axis, core):
    nl = grad.shape[0]
    _, _, r, cc = recv.shape
    tb = _row_tile(r, cc)
    per = r // tb

    def body(c_ref, g_ref, r_ref, o_ref):
        del c_ref
        o_ref[...] = (g_ref[...].astype(F32) + r_ref[...].astype(F32)).astype(BF16)

    if axis == 2:
        g_spec = pl.BlockSpec((None, tb, cc), lambda q, l, i, c_ref: (l, i, 2 * q + c_ref[0]))
    else:
        g_spec = pl.BlockSpec((None, tb, cc), lambda q, l, i, c_ref: (l, (2 * q + c_ref[0]) * per + i, 0))
    slot = pl.BlockSpec((None, None, tb, cc), lambda q, l, i, c_ref: (q, l, i, 0))
    return pl.pallas_call(
        body, out_shape=jax.ShapeDtypeStruct(recv.shape, BF16),
        grid_spec=pltpu.PrefetchScalarGridSpec(
            num_scalar_prefetch=1, grid=(N_CHIP, nl, per), in_specs=[g_spec, slot], out_specs=slot),
        name="add_sibling", compiler_params=_params(("parallel", "parallel", "parallel")),
    )(core, grad, recv)


def _adamw(w, g, m, v):
    m = ADAM_B1 * m + (1.0 - ADAM_B1) * g
    v = ADAM_B2 * v + (1.0 - ADAM_B2) * (g * g)
    m_hat = m / (1.0 - ADAM_B1 ** ADAM_STEP)
    v_hat = v / (1.0 - ADAM_B2 ** ADAM_STEP)
    delta = -ADAM_LR * (m_hat / (jnp.sqrt(v_hat) + ADAM_EPS) + ADAM_WD * w)
    return delta, m, v


def adam_from_partials(recv, w, m, v, l0, bufs):
    nl = recv.shape[1]
    _, r, cc = w.shape
    tb = _row_tile(r, cc)

    def body(p0, p1, p2, p3, w_ref, m_ref, v_ref, b0, b1, b2, b3, g_out, d_out, m_out, v_out):
        del b0, b1, b2, b3
        g = p0[...].astype(F32) + p1[...].astype(F32) + p2[...].astype(F32) + p3[...].astype(F32)
        d, mn, vn = _adamw(w_ref[...], g, m_ref[...], v_ref[...])
        g_out[...], d_out[...], m_out[...], v_out[...] = g, d, mn, vn

    slot = lambda q: pl.BlockSpec((None, None, tb, cc), lambda l, i: (q, l, i, 0))
    blk = pl.BlockSpec((None, tb, cc), lambda l, i: (l0 + l, i, 0))
    shp = jax.ShapeDtypeStruct(w.shape, F32)
    return pl.pallas_call(
        body, out_shape=[shp] * 4, grid=(nl, r // tb),
        in_specs=[slot(0), slot(1), slot(2), slot(3), blk, blk, blk] + [ANY] * 4, out_specs=[blk] * 4,
        input_output_aliases={7: 0, 8: 1, 9: 2, 10: 3},
        name="adam_big", compiler_params=_params(("parallel", "parallel")),
    )(recv, recv, recv, recv, w, m, v, *bufs)


def adam_small(w, g, m, v):
    def body(w_ref, g_ref, m_ref, v_ref, d_out, m_out, v_out):
        d_out[...], m_out[...], v_out[...] = _adamw(w_ref[...], g_ref[...], m_ref[...], v_ref[...])

    shp = jax.ShapeDtypeStruct(w.shape, F32)
    return pl.pallas_call(body, out_shape=[shp] * 3, name="adam_small")(w, g, m, v)


def _pack(arrays, multiple=8 * LANES):
    flat = jnp.concatenate([a.reshape(-1) for a in arrays])
    pad = (-flat.shape[0]) % multiple
    if pad:
        flat = jnp.concatenate([flat, jnp.zeros((pad,), flat.dtype)])
    return flat.reshape(-1, LANES)


def _unpack(packed, shapes):
    flat = packed.reshape(packed.shape[:-2] + (-1,))
    out, off = [], 0
    for shp in shapes:
        n = math.prod(shp)
        out.append(flat[..., off:off + n].reshape(packed.shape[:-2] + tuple(shp)))
        off += n
    return out


def _unshard_last(stacked):
    moved = jnp.moveaxis(stacked, 0, -2)
    return moved.reshape(moved.shape[:-2] + (-1,))


def _shard_last(full, block):
    n = full.shape[-1] // N_DEV
    return lax.dynamic_slice_in_dim(full, block * n, n, axis=full.ndim - 1)


BIG_AXIS = {"w_in": 2, "w_out": 1, "w_qkv": 2, "w_so": 1, "w_pool": 1, "w_up": 2, "w_down": 1}
BIG_ORDER = ("w_in", "w_out", "w_qkv", "w_so", "w_pool", "w_up", "w_down")

GATHER_HOSTS = {
    "hgrn_fwd_0": (("w_out", 0), ("w_up", 0), ("w_down", 0)),
    "ffn_up_0": (("w_qkv", 0),),
    "sba_fwd_1": (("w_so", 0), ("w_up", 1), ("w_down", 1), ("w_pool", 0), ("w_up", 2), ("w_down", 2)),
    "ffn_up_2": (("w_in", 1),),
    "hgrn_fwd_3": (("w_out", 1), ("w_up", 3), ("w_down", 3)),
}


class LocalWeights:
    def __init__(self, full):
        self.full, self.grads = full, {}

    def gather_plan(self, host):
        return None

    def gathered(self, outs):
        pass

    def weight(self, kind, l):
        return self.full[kind][l] if kind != "w_pool" else self.full[kind]

    def grad(self, kind, l, g):
        self.grads[(kind, l)] = g

    def exchange_plan(self):
        return None

    def exchanged(self, outs):
        pass


class MeshWeights:
    def __init__(self, shards16, w, m, v, core):
        self.shards, self.w, self.m, self.v, self.core = shards16, w, m, v, core
        self.full, self.pending, self.flying, self.keys = {}, [], [], None
        self.out = {k: [lax.empty(w[k].shape, F32) for _ in range(4)] for k in BIG_ORDER}

    def _items(self, keys):
        return [(self.shards[k], 0 if k == "w_pool" else l, 4 if k == "w_pool" else 1, BIG_AXIS[k]) for k, l in keys]

    def gather_plan(self, host):
        self.keys = [key for key in GATHER_HOSTS.get(host, ()) if key[0] in self.shards
                     and key[1] < (1 if key[0] == "w_pool" else self.shards[key[0]].shape[0])]
        return GatherPlan(self._items(self.keys)) if self.keys else None

    def gathered(self, outs):
        for (k, l), o in zip(self.keys, outs):
            self.full[(k, l)] = o if k == "w_pool" else o[0]
        self.keys = None

    def weight(self, kind, l):
        if (kind, l) not in self.full:
            (out,) = run_plan(GatherPlan(self._items([(kind, l)])), f"gather_{kind}_{l}")
            self.full[(kind, l)] = out if kind == "w_pool" else out[0]
        return self.full[(kind, l)]

    def grad(self, kind, l, g):
        self.pending.append((kind, l, g if g.ndim == 3 else g[None]))

    def exchange_plan(self):
        if not self.pending:
            return None
        self.flying, self.pending = self.pending, []
        grads = [g for _, _, g in self.flying]
        axes = [BIG_AXIS[k] for k, _, _ in self.flying]
        recv = sibling_exchange(grads, axes)
        return ExchangePlan([add_sibling(g, r, ax, self.core) for g, r, ax in zip(grads, recv, axes)])

    def exchanged(self, outs):
        for (k, l, _), r in zip(self.flying, outs):
            self.out[k] = adam_from_partials(r, self.w[k], self.m[k], self.v[k], l, self.out[k])
        self.flying = []

    def finish(self):
        plan = self.exchange_plan()
        if plan is not None:
            self.exchanged(run_plan(plan, "chip_exchange_tail"))
        return self.out


def train_step(x, target, norm_g, lb_logits, onorm_g, pool_scale, conv_w, conv_b, wts):
    t, d = x.shape
    depth = norm_g.shape[0]
    ng = lambda i, j: norm_g[i, j].reshape(1, d)
    lbs = lower_bounds_fwd(lb_logits)
    saved = []
    h = x
    _, u16, u32 = res_norm(h, None, None, ng(0, 0), "norm_in")

    def hosted(call, *args, **kw):
        plan = wts.gather_plan(kw["name"])
        out, extra = call(*args, plan=plan, **kw)
        if plan is not None:
            wts.gathered(extra)
        return out

    for i in range(depth):
        kind, j = i % 3, i // 3
        s = {"h_in": h, "u1": u16}
        if kind == 0:
            s["proj"] = matmul(u16, wts.weight("w_in", j), "nn", F32, f"hg_in_{i}")
            s["y"] = hosted(hgrn_fwd, s["proj"], lbs[i].reshape(1, -1), onorm_g[j].reshape(1, -1), i > 0,
                            name=f"hgrn_fwd_{i}")
            mix = matmul(s["y"], wts.weight("w_out", j), "nn", F32, f"hg_out_{i}")
        elif kind == 1:
            s["qkv"] = matmul(u16, wts.weight("w_qkv", j), "nn", F32, f"sb_qkv_{i}")
            s["o"] = hosted(sba_fwd, s["qkv"], name=f"sba_fwd_{i}")
            mix = matmul(s["o"], wts.weight("w_so", j), "nn", F32, f"sb_out_{i}")
        else:
            s["u1f"] = u32
            mix = pool_fwd(u32, wts.weight("w_pool", j), pool_scale[j].reshape(1, d), f"pool_fwd_{i}")
        s["mix"] = mix
        h_mid, u2, _ = res_norm(h, mix, ng(i, 1), ng(i, 2), f"norm_mid_{i}")
        s["h_mid"], s["u2"] = h_mid, u2
        up_plan = wts.gather_plan(f"ffn_up_{i}")
        if up_plan is None:
            s["up"] = matmul(u2, wts.weight("w_up", i), "nn", F32, f"ffn_up_{i}")
        else:
            s["up"], extra = matmul(u2, wts.weight("w_up", i), "nn", F32, f"ffn_up_{i}", plan=up_plan)
            wts.gathered(extra)
        s["act"] = conv_glu_fwd(s["up"], conv_w[i], conv_b[i].reshape(1, -1), f"glu_fwd_{i}")
        s["f"] = matmul(s["act"], wts.weight("w_down", i), "nn", F32, f"ffn_down_{i}")
        nxt = ng(i + 1, 0) if i + 1 < depth else None
        h, u16, u32 = res_norm(h_mid, s["f"], ng(i, 3), nxt, f"norm_out_{i}",
                               want_f32=(nxt is not None and (i + 1) % 3 == 2))
        saved.append(s)

    loss_acc, dh = loss_head(h, target, "loss_head")

    d_norm = [[None] * 4 for _ in range(depth)]
    d_lbs = jnp.zeros_like(lbs)
    d_onorm = [None] * onorm_g.shape[0]
    d_pscale = [None] * pool_scale.shape[0]
    d_cw, d_cb = [None] * depth, [None] * depth

    def exchanging(call, *args, **kw):
        plan = wts.exchange_plan()
        out, extra = call(*args, plan=plan, **kw)
        if plan is not None:
            wts.exchanged(extra)
        return out

    for i in reversed(range(depth)):
        kind, j = i % 3, i // 3
        s = saved[i]
        df, d_norm[i][3] = norm_bwd(s["f"], ng(i, 3), dh, None, BF16, f"nb_out_{i}")
        dact = matmul(df, wts.weight("w_down", i), "nt", F32, f"d_act_{i}")
        wts.grad("w_down", i, matmul(s["act"], df, "tn", BF16, f"dw_down_{i}"))
        dup, dcw, dcb = conv_glu_bwd(s["up"], conv_w[i], conv_b[i].reshape(1, -1), dact, f"glu_bwd_{i}")
        d_cw[i] = jnp.moveaxis(dcw, 0, 1).reshape(CONV_WIDTH, -1)
        d_cb[i] = dcb.reshape(-1)
        du2 = matmul(Split(dup), wts.weight("w_up", i), "nt", F32, f"d_u2_{i}")
        wts.grad("w_up", i, matmul(s["u2"], Split(dup), "tn", BF16, f"dw_up_{i}"))
        dh_mid, d_norm[i][2] = norm_bwd(s["h_mid"], ng(i, 2), du2, dh, F32, f"nb_mid_{i}")
        dm, d_norm[i][1] = norm_bwd(s["mix"], ng(i, 1), dh_mid, None, F32 if kind == 2 else BF16, f"nb_mix_{i}")
        if kind == 0:
            dy = matmul(dm, wts.weight("w_out", j), "nt", F32, f"d_y_{i}")
            wts.grad("w_out", j, matmul(s["y"], dm, "tn", BF16, f"dw_hgout_{i}"))
            dproj, d_onorm[j], dlb = exchanging(hgrn_bwd, s["proj"], lbs[i].reshape(1, -1),
                                                onorm_g[j].reshape(1, -1), dy, i > 0, name=f"hgrn_bwd_{i}")
            d_lbs = d_lbs.at[i].set(dlb[0])
            du1 = matmul(Split(dproj), wts.weight("w_in", j), "nt", F32, f"d_u1_{i}")
            wts.grad("w_in", j, matmul(s["u1"], Split(dproj), "tn", BF16, f"dw_hgin_{i}"))
        elif kind == 1:
            do = matmul(dm, wts.weight("w_so", j), "nt", F32, f"d_o_{i}")
            wts.grad("w_so", j, matmul(s["o"], dm, "tn", BF16, f"dw_sbout_{i}"))
            dqkv = exchanging(sba_bwd, s["qkv"], s["o"], do, name=f"sba_bwd_{i}")
            du1 = matmul(Split(dqkv), wts.weight("w_qkv", j), "nt", F32, f"d_u1_{i}")
            wts.grad("w_qkv", j, matmul(s["u1"], Split(dqkv), "tn", BF16, f"dw_sbqkv_{i}"))
        else:
            du1, g_pool, d_pscale[j] = pool_bwd(s["u1f"], wts.weight("w_pool", j), pool_scale[j].reshape(1, d),
                                                dm, f"pool_bwd_{i}")
            wts.grad("w_pool", j, g_pool)
        dh, d_norm[i][0] = norm_bwd(s["h_in"], ng(i, 0), du1, dh_mid, F32, f"nb_in_{i}")

    small = {
        "norm_g": jnp.stack([jnp.stack([v.reshape(d) for v in row]) for row in d_norm]),
        "lb_logits": lower_bounds_bwd(lb_logits, d_lbs),
        "onorm_g": jnp.stack([v.reshape(-1) for v in d_onorm]),
        "pool_scale": jnp.stack([v.reshape(-1) for v in d_pscale]),
        "conv_w": jnp.stack(d_cw),
        "conv_b": jnp.stack(d_cb),
    }
    return loss_acc, dh, small


SMALL_SHARDED = ("norm_g", "onorm_g", "pool_scale", "conv_w")
SMALL_ORDER = ("norm_g", "lb_logits", "onorm_g", "pool_scale", "conv_w", "conv_b")


def kernel(x, norm_g, hgrn_lb_logits, hgrn_w_in, hgrn_onorm_g, hgrn_w_out, sba_w_qkv, sba_w_out, pool_w, pool_scale, ffn_w_up, ffn_conv_w, ffn_conv_b, ffn_w_down, loss_target, m_norm_g, m_hgrn_lb_logits, m_hgrn_w_in, m_hgrn_onorm_g, m_hgrn_w_out, m_sba_w_qkv, m_sba_w_out, m_pool_w, m_pool_scale, m_ffn_w_up, m_ffn_conv_w, m_ffn_conv_b, m_ffn_w_down, v_norm_g, v_hgrn_lb_logits, v_hgrn_w_in, v_hgrn_onorm_g, v_hgrn_w_out, v_sba_w_qkv, v_sba_w_out, v_pool_w, v_pool_scale, v_ffn_w_up, v_ffn_conv_w, v_ffn_conv_b, v_ffn_w_down):
    cx, cy, cc = lax.axis_index("x"), lax.axis_index("y"), lax.axis_index("c")
    block = 4 * cx + 2 * cy + cc
    core = cc.astype(jnp.int32).reshape(1)

    pool3 = lambda a: a.reshape(a.shape[0] * a.shape[1], a.shape[2], a.shape[3])
    big_w = dict(zip(BIG_ORDER, [hgrn_w_in, hgrn_w_out, sba_w_qkv, sba_w_out, pool3(pool_w), ffn_w_up, ffn_w_down]))
    big_m = dict(zip(BIG_ORDER, [m_hgrn_w_in, m_hgrn_w_out, m_sba_w_qkv, m_sba_w_out, pool3(m_pool_w), m_ffn_w_up,
                                 m_ffn_w_down]))
    big_v = dict(zip(BIG_ORDER, [v_hgrn_w_in, v_hgrn_w_out, v_sba_w_qkv, v_sba_w_out, pool3(v_pool_w), v_ffn_w_up,
                                 v_ffn_w_down]))

    sharded = {"norm_g": norm_g, "onorm_g": hgrn_onorm_g, "pool_scale": pool_scale, "conv_w": ffn_conv_w}
    gathered = small_exchange(_pack([sharded[n] for n in SMALL_SHARDED]), False, "gather_small")
    parts = _unpack(gathered, [sharded[n].shape for n in SMALL_SHARDED])
    full = {n: _unshard_last(p) for n, p in zip(SMALL_SHARDED, parts)}

    wts = MeshWeights({k: w.astype(BF16) for k, w in big_w.items()}, big_w, big_m, big_v, core)
    loss_acc, grad_x, small_g = train_step(
        x[0], loss_target[0], full["norm_g"], hgrn_lb_logits, full["onorm_g"], full["pool_scale"],
        full["conv_w"], ffn_conv_b, wts)
    loss = lax.psum(loss_acc[0, 0], ("x", "y", "c"))

    shapes = [small_g[n].shape for n in SMALL_ORDER]
    summed = _unpack(small_exchange(_pack([small_g[n] for n in SMALL_ORDER]), True, "reduce_small"), shapes)
    sg = {n: (_shard_last(g, block) if n in SMALL_SHARDED else g) for n, g in zip(SMALL_ORDER, summed)}
    sw = {"norm_g": norm_g, "lb_logits": hgrn_lb_logits, "onorm_g": hgrn_onorm_g, "pool_scale": pool_scale,
          "conv_w": ffn_conv_w, "conv_b": ffn_conv_b}
    sm = {"norm_g": m_norm_g, "lb_logits": m_hgrn_lb_logits, "onorm_g": m_hgrn_onorm_g, "pool_scale": m_pool_scale,
          "conv_w": m_ffn_conv_w, "conv_b": m_ffn_conv_b}
    sv = {"norm_g": v_norm_g, "lb_logits": v_hgrn_lb_logits, "onorm_g": v_hgrn_onorm_g, "pool_scale": v_pool_scale,
          "conv_w": v_ffn_conv_w, "conv_b": v_ffn_conv_b}
    sshapes = [sw[n].shape for n in SMALL_ORDER]
    packed = [_pack([dct[n] for n in SMALL_ORDER]) for dct in (sw, sg, sm, sv)]
    s_delta, s_m, s_v = [dict(zip(SMALL_ORDER, _unpack(p, sshapes))) for p in adam_small(*packed)]

    upd = wts.finish()
    b_grad, b_delta, b_m, b_v = [[upd[k][n] for k in BIG_ORDER] for n in range(4)]

    def tree(small, bigs):
        bg = list(bigs)
        bg[4] = bg[4].reshape(pool_w.shape)
        return (small["norm_g"], small["lb_logits"], bg[0], small["onorm_g"], bg[1], bg[2], bg[3], bg[4],
                small["pool_scale"], bg[5], small["conv_w"], small["conv_b"], bg[6])

    return (loss, grad_x[None], *tree(sg, b_grad), *tree(s_delta, b_delta), *tree(s_m, b_m), *tree(s_v, b_v))
```

```python
import functools
import math

import jax
import jax.numpy as jnp
from jax import lax
from jax.experimental import pallas as pl
from jax.experimental.pallas import tpu as pltpu

F32 = jnp.float32
BF16 = jnp.bfloat16
HI = lax.Precision.HIGHEST
MESH = pl.DeviceIdType.MESH
ANY = pl.BlockSpec(memory_space=pl.ANY)
VMEM_FULL = pl.BlockSpec(memory_space=pltpu.VMEM)

NORM_EPS = 1e-6
HEAD = 128
HG_CHUNK = 128
HG_SUB = 32
HG_MAX_EXPONENT = 80.0
SB_BLOCK = 128
SB_QROWS = 256
POOL_WINDOWS = (2, 4, 8, 16)
CONV_WIDTH = 3
ROW_TILE = 256
N_DEV = 8
N_CHIP = 4

ADAM_LR = 0.001
ADAM_B1 = 0.9
ADAM_B2 = 0.999
ADAM_EPS = 1e-08
ADAM_WD = 0.01
ADAM_STEP = 10

VMEM_LIMIT = 48 * 1024 * 1024
LANES = 128
ELEMWISE_BLOCK_ELEMS = 256 * 1024


def _params(sem=None, vmem=VMEM_LIMIT):
    return pltpu.CompilerParams(dimension_semantics=sem, vmem_limit_bytes=vmem)


def _tile(n, prefs=(1024, 512, 256, 128)):
    for p in prefs:
        if n % p == 0:
            return p
    return n


def _dot(a, b, prec=None):
    return jnp.dot(a, b, precision=prec, preferred_element_type=F32)


def _dot_nt(a, b, prec=None):
    return lax.dot_general(a, b, (((1,), (1,)), ((), ())), precision=prec, preferred_element_type=F32)


def _dot_tn(a, b, prec=None):
    return lax.dot_general(a, b, (((0,), (0,)), ((), ())), precision=prec, preferred_element_type=F32)


def _split_dot(x, tri, parts, left=False):
    tot, rest = None, x
    for p in range(parts):
        h = rest.astype(BF16)
        d = _dot(tri, h) if left else _dot(h, tri)
        tot = d if tot is None else tot + d
        if p + 1 < parts:
            rest = rest - h.astype(F32)
    return tot


def _dot1(a, b, fn):
    return fn(a.astype(BF16), b.astype(BF16))


def _dot3(a, b, fn):
    ah, bh = a.astype(BF16), b.astype(BF16)
    al, bl = (a - ah.astype(F32)).astype(BF16), (b - bh.astype(F32)).astype(BF16)
    return fn(ah, bh) + fn(ah, bl) + fn(al, bh)


def _sigmoid(x):
    return jax.nn.sigmoid(x)


def _softplus(x):
    return jnp.maximum(x, 0.0) + jnp.log1p(jnp.exp(-jnp.abs(x)))


class Layer:
    def __init__(self, arr, l):
        self.arr, self.l = arr, l
        self.shape = arr.shape[1:]
        self.part = None


class Split:
    def __init__(self, arr):
        self.arr = arr
        self.shape = (arr.shape[1], arr.shape[0] * arr.shape[2])
        self.part = arr.shape[2]


class Plain:
    def __init__(self, arr):
        self.arr = arr
        self.shape = arr.shape
        self.part = None


def _wrap(op):
    return op if isinstance(op, (Layer, Split, Plain)) else Plain(op)


def _op_spec(op, br, bc, rc_of_grid):
    if isinstance(op, Layer):
        l = op.l
        return pl.BlockSpec((None, br, bc), lambda i, j, k: (l, *rc_of_grid(i, j, k)))
    if isinstance(op, Split):
        per = op.part // bc

        def imap(i, j, k):
            r, c = rc_of_grid(i, j, k)
            return (lax.div(c, per), r, lax.rem(c, per))
        return pl.BlockSpec((None, br, bc), imap)
    return pl.BlockSpec((br, bc), rc_of_grid)


def _hosted(body, n_in, n_out, plan, step_of_grid):
    if plan is None:
        return body
    pi, po, ps = len(plan.args), len(plan.out_shape), len(plan.scratch)

    def wrapped(*refs):
        refs = list(refs)
        ins, pins = refs[:n_in], refs[n_in:n_in + pi]
        outs = refs[n_in + pi:n_in + pi + n_out]
        pouts = refs[n_in + pi + n_out:n_in + pi + n_out + po]
        scr, pscr = refs[n_in + pi + n_out + po:len(refs) - ps], refs[len(refs) - ps:]
        step, nsteps = step_of_grid()

        @pl.when(step == 0)
        def _():
            plan.start(pins, pouts, pscr)

        if plan.has_mid:
            @pl.when(step == min((3 * nsteps) // 4, nsteps - 1))
            def _():
                plan.mid(pins, pouts, pscr)

        body(*ins, *outs, *scr)

        @pl.when(step == nsteps - 1)
        def _():
            plan.finish(pins, pouts, pscr)

    return wrapped


def _host_call(body, n_in, plan, step_of_grid, *, out_shape, grid, in_specs, out_specs, scratch_shapes, name, sem, args):
    single = not isinstance(out_shape, (list, tuple))
    out_shape = [out_shape] if single else list(out_shape)
    out_specs = [out_specs] if single else list(out_specs)
    n_out = len(out_shape)
    in_specs, scratch_shapes, args = list(in_specs), list(scratch_shapes), list(args)
    if plan is not None:
        in_specs += [ANY] * len(plan.args)
        args += plan.args
        out_shape += plan.out_shape
        out_specs += [ANY] * len(plan.out_shape)
        scratch_shapes += plan.scratch
        sem = ("arbitrary",) * len(grid)
    outs = pl.pallas_call(
        _hosted(body, n_in, n_out, plan, step_of_grid), out_shape=out_shape, grid=grid,
        in_specs=in_specs, out_specs=out_specs, scratch_shapes=scratch_shapes,
        name=name, compiler_params=_params(sem),
    )(*args)
    host = outs[0] if single else list(outs[:n_out])
    return host, list(outs[n_out:])


def matmul(a, b, mode, out_dtype, name, plan=None):
    a, b = _wrap(a), _wrap(b)
    if mode == "nn":
        (m, kd), (kd2, n) = a.shape, b.shape
    elif mode == "nt":
        (m, kd), (n, kd2) = a.shape, b.shape
    else:
        (kd, m), (kd2, n) = a.shape, b.shape
    assert kd == kd2, (mode, a.shape, b.shape)

    def dim_tile(full, ops_on_cols, prefs=(1024, 512, 256, 128)):
        base = full
        for op in ops_on_cols:
            if op.part is not None:
                base = math.gcd(base, op.part)
        return _tile(base, prefs)

    tm = dim_tile(m, [a] if mode == "tn" else [])
    tn = dim_tile(n, [b] if mode in ("nn", "tn") else [])
    tk = dim_tile(kd, ([a] if mode in ("nn", "nt") else []) + ([b] if mode == "nt" else []),
                  prefs=(2048, 2816, 1024, 512, 256, 128))
    nk = kd // tk

    if mode == "nn":
        a_spec = _op_spec(a, tm, tk, lambda i, j, k: (i, k))
        b_spec = _op_spec(b, tk, tn, lambda i, j, k: (k, j))
        dot = _dot
    elif mode == "nt":
        a_spec = _op_spec(a, tm, tk, lambda i, j, k: (i, k))
        b_spec = _op_spec(b, tn, tk, lambda i, j, k: (j, k))
        dot = _dot_nt
    else:
        a_spec = _op_spec(a, tk, tm, lambda i, j, k: (k, i))
        b_spec = _op_spec(b, tk, tn, lambda i, j, k: (k, j))
        dot = _dot_tn

    def body(a_ref, b_ref, o_ref, *acc):
        part = dot(a_ref[...].astype(BF16), b_ref[...].astype(BF16))
        if nk == 1:
            o_ref[...] = part.astype(o_ref.dtype)
            return
        (acc_ref,) = acc
        k = pl.program_id(2)

        @pl.when(k == 0)
        def _():
            acc_ref[...] = part

        @pl.when(k > 0)
        def _():
            acc_ref[...] += part

        @pl.when(k == nk - 1)
        def _():
            o_ref[...] = acc_ref[...].astype(o_ref.dtype)

    gi, gj = m // tm, n // tn

    def step_of_grid():
        return (pl.program_id(0) * gj + pl.program_id(1)) * nk + pl.program_id(2), gi * gj * nk

    out, extra = _host_call(
        body, 2, plan, step_of_grid, out_shape=jax.ShapeDtypeStruct((m, n), out_dtype), grid=(gi, gj, nk),
        in_specs=[a_spec, b_spec], out_specs=pl.BlockSpec((tm, tn), lambda i, j, k: (i, j)),
        scratch_shapes=[pltpu.VMEM((tm, tn), F32)] if nk > 1 else [], name=name,
        sem=("parallel", "parallel", "arbitrary"), args=[a.arr, b.arr])
    return out if plan is None else (out, extra)


def _rms(x, g):
    r = lax.rsqrt(jnp.mean(x * x, axis=-1, keepdims=True) + NORM_EPS)
    return x * r * g


def res_norm(h, m, g_a, g_b, name, want_f32=False):
    t, d = h.shape
    tr = min(ROW_TILE, t)
    has_m, has_u = m is not None, g_b is not None
    row = pl.BlockSpec((tr, d), lambda i: (i, 0))
    vec = pl.BlockSpec((1, d), lambda i: (0, 0))

    def body(*refs):
        refs = list(refs)
        h_ref = refs.pop(0)
        hn = h_ref[...]
        if has_m:
            m_ref, ga_ref = refs.pop(0), refs.pop(0)
            hn = hn + _rms(m_ref[...], ga_ref[...])
        if has_u:
            gb_ref = refs.pop(0)
        if has_m:
            refs.pop(0)[...] = hn
        if has_u:
            u = _rms(hn, gb_ref[...])
            refs.pop(0)[...] = u.astype(BF16)
            if want_f32:
                refs.pop(0)[...] = u

    args, in_specs, out_shape, out_specs = [h], [row], [], []
    if has_m:
        args += [m, g_a]
        in_specs += [row, vec]
        out_shape.append(jax.ShapeDtypeStruct((t, d), F32))
        out_specs.append(row)
    if has_u:
        args.append(g_b)
        in_specs.append(vec)
        out_shape.append(jax.ShapeDtypeStruct((t, d), BF16))
        out_specs.append(row)
        if want_f32:
            out_shape.append(jax.ShapeDtypeStruct((t, d), F32))
            out_specs.append(row)
    outs = list(pl.pallas_call(
        body, out_shape=out_shape, grid=(t // tr,), in_specs=in_specs, out_specs=out_specs,
        name=name, compiler_params=_params(("parallel",)),
    )(*args))
    h_new = outs.pop(0) if has_m else None
    u16 = outs.pop(0) if has_u else None
    u32 = outs.pop(0) if (has_u and want_f32) else None
    return h_new, u16, u32


def norm_bwd(x, g, dy, add, out_dtype, name):
    t, d = x.shape
    tr = min(ROW_TILE, t)
    has_add = add is not None
    row = pl.BlockSpec((tr, d), lambda i: (i, 0))
    vec = pl.BlockSpec((1, d), lambda i: (0, 0))

    def body(*refs):
        if has_add:
            x_ref, g_ref, dy_ref, add_ref, dx_ref, dg_ref = refs
        else:
            x_ref, g_ref, dy_ref, dx_ref, dg_ref = refs
        xv = x_ref[...]
        dyv = dy_ref[...].astype(F32)
        r = lax.rsqrt(jnp.mean(xv * xv, axis=-1, keepdims=True) + NORM_EPS)
        gy = dyv * g_ref[...]
        dx = r * gy - xv * (r * r * r * jnp.mean(gy * xv, axis=-1, keepdims=True))
        if has_add:
            dx = dx + add_ref[...]
        dx_ref[...] = dx.astype(dx_ref.dtype)

        @pl.when(pl.program_id(0) == 0)
        def _():
            dg_ref[...] = jnp.zeros_like(dg_ref)

        dg_ref[...] += jnp.sum(dyv * xv * r, axis=0, keepdims=True)

    args = [x, g, dy] + ([add] if has_add else [])
    in_specs = [row, vec, row] + ([row] if has_add else [])
    return pl.pallas_call(
        body, out_shape=[jax.ShapeDtypeStruct((t, d), out_dtype), jax.ShapeDtypeStruct((1, d), F32)],
        grid=(t // tr,), in_specs=in_specs, out_specs=[row, vec],
        name=name, compiler_params=_params(("arbitrary",)),
    )(*args)


def loss_head(y, target, name):
    t, d = y.shape
    tr = min(ROW_TILE, t)
    row = pl.BlockSpec((tr, d), lambda i: (i, 0))
    acc = pl.BlockSpec((8, LANES), lambda i: (0, 0))

    def body(y_ref, t_ref, loss_ref, dy_ref):
        e = y_ref[...] - t_ref[...]
        dy_ref[...] = e * (1.0 / d)

        @pl.when(pl.program_id(0) == 0)
        def _():
            loss_ref[...] = jnp.zeros_like(loss_ref)

        loss_ref[...] += jnp.sum(e * e) * (0.5 / d)

    return pl.pallas_call(
        body, out_shape=[jax.ShapeDtypeStruct((8, LANES), F32), jax.ShapeDtypeStruct((t, d), F32)],
        grid=(t // tr,), in_specs=[row, row], out_specs=[acc, row],
        name=name, compiler_params=_params(("arbitrary",)),
    )(y, target)


def _depth_softmax(ref, depth):
    rows = [ref[i:i + 1, :] for i in range(depth)]
    mx = functools.reduce(jnp.maximum, rows)
    ex = [jnp.exp(r - mx) for r in rows]
    tot = functools.reduce(lambda p, q: p + q, ex)
    return [e / tot for e in ex]


def lower_bounds_fwd(logits):
    depth, kw = logits.shape

    def body(l_ref, o_ref):
        s = _depth_softmax(l_ref, depth)
        run = jnp.zeros_like(s[0])
        o_ref[0:1, :] = run
        for i in range(1, depth):
            run = run + s[i]
            o_ref[i:i + 1, :] = run

    return pl.pallas_call(body, out_shape=jax.ShapeDtypeStruct((depth, kw), F32), name="lb_fwd")(logits)


def lower_bounds_bwd(logits, dlb):
    depth, kw = logits.shape

    def body(l_ref, d_ref, o_ref):
        s = _depth_softmax(l_ref, depth)
        ds = [jnp.zeros_like(s[0]) for _ in range(depth)]
        run = jnp.zeros_like(s[0])
        for j in range(depth - 1, 0, -1):
            run = run + d_ref[j:j + 1, :]
            ds[j] = run
        dot = functools.reduce(lambda p, q: p + q, [s[j] * ds[j] for j in range(depth)])
        for j in range(depth):
            o_ref[j:j + 1, :] = s[j] * (ds[j] - dot)

    return pl.pallas_call(body, out_shape=jax.ShapeDtypeStruct((depth, kw), F32), name="lb_bwd")(logits, dlb)


def _hg_gates(qp, fp, lb_row, has_lb):
    sig = _sigmoid(fp)
    nsig = _sigmoid(-fp)
    ls = jnp.minimum(fp, 0.0) - jnp.log1p(jnp.exp(-jnp.abs(fp)))
    if has_lb:
        a = jnp.log(lb_row)
        bb = jnp.log1p(-lb_row) + ls
        g = jnp.maximum(a, bb) + jnp.log1p(jnp.exp(-jnp.abs(a - bb)))
        w = jnp.exp(bb - g)
        k = (1.0 - lb_row) * nsig
    else:
        g, w, k = ls, None, nsig
    q = qp * _sigmoid(qp)
    return q, k, g, sig, nsig, w


HG_HEADS_PER_STEP = 2


def _heads_per_step(nh):
    return HG_HEADS_PER_STEP if nh % HG_HEADS_PER_STEP == 0 else 1


def _head_views(refs, hh, lanes, lead):
    cs = pl.ds(hh * HEAD, HEAD)
    out = []
    for i, r in enumerate(refs):
        if i in lead:
            out.append(r.at[hh])
        elif i in lanes:
            out.append(r.at[(slice(None),) * (len(r.shape) - 1) + (cs,)])
        else:
            out.append(r)
    return out


def _hg_masks():
    c = HG_CHUNK
    row = lax.broadcasted_iota(jnp.int32, (c, c), 0)
    col = lax.broadcasted_iota(jnp.int32, (c, c), 1)
    lower = (col <= row).astype(BF16)
    upper = (col >= row).astype(BF16)
    krow = lax.broadcasted_iota(jnp.int32, (c, HEAD), 0)
    arow = lax.broadcasted_iota(jnp.int32, (HG_SUB, c), 0)
    acol = lax.broadcasted_iota(jnp.int32, (HG_SUB, c), 1)
    return lower, upper, krow, arow, acol


def _hg_sub(i, q, k, b, b_ref, krow, arow, acol):
    r0 = i * HG_SUB
    m = b_ref[r0:r0 + 1, :]
    ebq = jnp.exp(b[r0:r0 + HG_SUB, :] - m)
    qh = (q[r0:r0 + HG_SUB, :] * ebq).astype(BF16)
    ek = jnp.exp(jnp.where(krow < r0 + HG_SUB, jnp.minimum(m - b, HG_MAX_EXPONENT), 0.0))
    kh = (k * ek).astype(BF16)
    mask = acol <= arow + r0
    amat = jnp.where(mask, _dot_nt(qh, kh), 0.0).astype(BF16)
    return ebq, qh, ek, kh, mask, amat


def hgrn_fwd(proj, lb, onorm_g, has_lb, name, plan=None):
    t, w4 = proj.shape
    kw = w4 // 4
    nh, nc, c = kw // HEAD, t // HG_CHUNK, HG_CHUNK
    hpb = _heads_per_step(nh)
    wide = hpb * HEAD

    def body(*refs):
        refs[7][...] = jnp.zeros_like(refs[7])
        heads = [one_head(*_head_views(refs, hh, lanes=(0, 1, 2, 3, 4, 5, 6), lead=(7, 8))) for hh in range(hpb)]

        def chunks(ci, carry):
            for chunk in heads:
                chunk(ci)
            return carry

        lax.fori_loop(0, nc, chunks, 0)

    def one_head(qp_ref, fp_ref, iv_ref, gp_ref, lb_ref, on_ref, y_ref, st_ref, b_ref):
        lower, _, krow, arow, acol = _hg_masks()
        lb_row, gam = lb_ref[...], on_ref[...]

        def chunk(ci):
            rs = pl.ds(pl.multiple_of(ci * c, c), c)
            v = iv_ref[rs, :].astype(BF16)
            gp = gp_ref[rs, :]
            q, k, g, _, _, _ = _hg_gates(qp_ref[rs, :], fp_ref[rs, :], lb_row, has_lb)
            b = _split_dot(g, lower, 3, left=True)
            b_ref[...] = b
            st = st_ref[...]
            o = _dot_nt((q * jnp.exp(b)).astype(BF16), st.astype(BF16))
            amats = [_hg_sub(i, q, k, b, b_ref, krow, arow, acol)[-1] for i in range(c // HG_SUB)]
            o = o + _dot(jnp.concatenate(amats, axis=0), v)
            bl = b_ref[c - 1:c, :]
            st_ref[...] = jnp.exp(bl) * st + _dot_tn(v, (k * jnp.exp(bl - b)).astype(BF16))
            r = lax.rsqrt(jnp.mean(o * o, axis=-1, keepdims=True) + NORM_EPS)
            y_ref[rs, :] = (o * r * gam * (gp * _sigmoid(gp))).astype(BF16)

        return chunk

    ns = nh // hpb
    col = lambda p: pl.BlockSpec((t, wide), lambda h: (0, p * ns + h))
    vec = pl.BlockSpec((1, wide), lambda h: (0, h))
    return _host_call(
        body, 6, plan, lambda: (pl.program_id(0), ns),
        out_shape=jax.ShapeDtypeStruct((t, kw), BF16), grid=(ns,),
        in_specs=[col(0), col(1), col(2), col(3), vec, vec],
        out_specs=pl.BlockSpec((t, wide), lambda h: (0, h)),
        scratch_shapes=[pltpu.VMEM((hpb, HEAD, HEAD), F32), pltpu.VMEM((hpb, c, HEAD), F32)],
        name=name, sem=("parallel",), args=[proj, proj, proj, proj, lb, onorm_g])


def hgrn_bwd(proj, lb, onorm_g, dy, has_lb, name, plan=None):
    t, w4 = proj.shape
    kw = w4 // 4
    nh, nc, c = kw // HEAD, t // HG_CHUNK, HG_CHUNK
    hpb = _heads_per_step(nh)
    wide = hpb * HEAD

    def body(*refs):
        for i in (8, 9, 13, 14):
            refs[i][...] = jnp.zeros_like(refs[i])
        heads = [one_head(*_head_views(refs, hh, lanes=(0, 1, 2, 3, 4, 5, 6, 7, 8, 9, 11), lead=(10, 12, 13, 14)))
                 for hh in range(hpb)]

        def fwd_chunks(ci, states):
            return tuple(fwd(ci, st) for (fwd, _), st in zip(heads, states))

        lax.fori_loop(0, nc, fwd_chunks, tuple(jnp.zeros((HEAD, HEAD), F32) for _ in heads))

        def bwd_chunks(step, carry):
            for _, bwd in heads:
                bwd(nc - 1 - step)
            return carry

        lax.fori_loop(0, nc, bwd_chunks, 0)

    def one_head(qp_ref, fp_ref, iv_ref, gp_ref, lb_ref, on_ref, dy_ref,
                 dp_ref, dgam_ref, dlb_ref, sst_ref, o_ref, b_ref, dst_ref, car_ref):
        lower, upper, krow, arow, acol = _hg_masks()
        lb_row, gam = lb_ref[...], on_ref[...]
        mm = _dot3 if has_lb else _dot1

        def recompute(ci):
            rs = pl.ds(pl.multiple_of(ci * c, c), c)
            gates = _hg_gates(qp_ref[rs, :], fp_ref[rs, :], lb_row, has_lb)
            b = _split_dot(gates[2], lower, 3, left=True)
            b_ref[...] = b
            return rs, gates, b

        def fwd_chunk(ci, carry):
            rs, (q, k, g, _, _, _), b = recompute(ci)
            v = iv_ref[rs, :].astype(BF16)
            st = carry
            sst_ref[ci] = st
            o = _dot_nt((q * jnp.exp(b)).astype(BF16), st.astype(BF16))
            amats = [_hg_sub(i, q, k, b, b_ref, krow, arow, acol)[-1] for i in range(c // HG_SUB)]
            o_ref[rs, :] = o + _dot(jnp.concatenate(amats, axis=0), v)
            bl = b_ref[c - 1:c, :]
            return jnp.exp(bl) * st + mm(iv_ref[rs, :], k * jnp.exp(bl - b), _dot_tn)

        def bwd_chunk(ci):
            rs, (q, k, g, sig, nsig, w), b = recompute(ci)
            qp, gp, v32 = qp_ref[rs, :], gp_ref[rs, :], iv_ref[rs, :]
            v = v32.astype(BF16)
            dyv, o = dy_ref[rs, :], o_ref[rs, :]
            st = sst_ref[ci]
            dst_new = dst_ref[...]
            r = lax.rsqrt(jnp.mean(o * o, axis=-1, keepdims=True) + NORM_EPS)
            on = o * r
            sgm = _sigmoid(gp)
            sg = gp * sgm
            dgam_ref[...] += jnp.sum(dyv * sg * on, axis=0, keepdims=True)
            dgp = dyv * on * gam * (sgm * (1.0 + gp * (1.0 - sgm)))
            dn = dyv * gam * sg
            do = r * dn - o * (r * r * r * jnp.mean(dn * o, axis=-1, keepdims=True))
            eb = jnp.exp(b)
            bl = b_ref[c - 1:c, :]
            ekd = jnp.exp(bl - b)
            do16 = do.astype(BF16)
            dq = mm(do, st, _dot) * eb
            dv = _dot1(k * ekd, dst_new, _dot_nt)
            dk = mm(v32, dst_new, _dot) * ekd
            dst_ref[...] = jnp.exp(bl) * dst_new + mm(do, q * eb, _dot_tn)
            da_all = mm(do, v32, _dot_nt)
            dq_parts, amats = [], []
            for i in range(c // HG_SUB):
                r0 = i * HG_SUB
                ebq, _, ek, _, mask, amat = _hg_sub(i, q, k, b, b_ref, krow, arow, acol)
                da = jnp.where(mask, da_all[r0:r0 + HG_SUB, :], 0.0)
                dq_parts.append(mm(da, k * ek, _dot) * ebq)
                dk = dk + mm(da, q[r0:r0 + HG_SUB, :] * ebq, _dot_tn) * ek
                amats.append(amat)
            dq = dq + jnp.concatenate(dq_parts, axis=0)
            dv = dv + _dot_tn(jnp.concatenate(amats, axis=0), do16)
            db = q * dq - k * dk
            dg = car_ref[...] + _split_dot(db, upper, 3, left=True)
            car_ref[...] += jnp.sum(db, axis=0, keepdims=True)
            if has_lb:
                dfp = dg * nsig * w - dk * ((1.0 - lb_row) * sig * nsig)
                dlb_ref[...] += jnp.sum(dg * nsig * jnp.exp(-g) - dk * nsig, axis=0, keepdims=True)
            else:
                dfp = dg * nsig - dk * (sig * nsig)
            sq = _sigmoid(qp)
            dp_ref[0, rs, :] = (dq * (sq * (1.0 + qp * (1.0 - sq)))).astype(BF16)
            dp_ref[1, rs, :] = dfp.astype(BF16)
            dp_ref[2, rs, :] = dv.astype(BF16)
            dp_ref[3, rs, :] = dgp.astype(BF16)

        return fwd_chunk, bwd_chunk

    ns = nh // hpb
    col = lambda p: pl.BlockSpec((t, wide), lambda h: (0, p * ns + h))
    vec = pl.BlockSpec((1, wide), lambda h: (0, h))
    return _host_call(
        body, 7, plan, lambda: (pl.program_id(0), ns),
        out_shape=[jax.ShapeDtypeStruct((4, t, kw), BF16), jax.ShapeDtypeStruct((1, kw), F32),
                   jax.ShapeDtypeStruct((1, kw), F32)],
        grid=(ns,),
        in_specs=[col(0), col(1), col(2), col(3), vec, vec, pl.BlockSpec((t, wide), lambda h: (0, h))],
        out_specs=[pl.BlockSpec((4, t, wide), lambda h: (0, 0, h)), vec, vec],
        scratch_shapes=[pltpu.VMEM((hpb, nc, HEAD, HEAD), F32), pltpu.VMEM((t, wide), F32),
                        pltpu.VMEM((hpb, c, HEAD), F32), pltpu.VMEM((hpb, HEAD, HEAD), F32),
                        pltpu.VMEM((hpb, 1, HEAD), F32)],
        name=name, sem=("parallel",), args=[proj, proj, proj, proj, lb, onorm_g, dy])


def _sb_masks():
    m, n = SB_QROWS, SB_BLOCK
    row = lax.broadcasted_iota(jnp.int32, (m, n), 0)
    col = lax.broadcasted_iota(jnp.int32, (m, n), 1)
    r2 = lax.broadcasted_iota(jnp.int32, (n, n), 0)
    c2 = lax.broadcasted_iota(jnp.int32, (n, n), 1)
    after = (r2 > c2).astype(BF16)
    from_ = (r2 >= c2).astype(BF16)
    return row, col, after, from_


def sba_fwd(qkv, name, plan=None):
    t, w3 = qkv.shape
    wd = w3 // 3
    m, n = SB_QROWS, SB_BLOCK
    nh, nq, per = wd // HEAD, t // m, m // n
    scale = HEAD ** -0.5

    def body(q_ref, k_ref, v_ref, o_ref, q16, k16, v16):
        row, col, after, _ = _sb_masks()
        q16[...] = (q_ref[...] * scale).astype(BF16)
        k16[...] = k_ref[...].astype(BF16)
        v16[...] = v_ref[...].astype(BF16)

        def qblock(qi, carry):
            qs = pl.ds(pl.multiple_of(qi * m, m), m)
            q = q16[qs, :]
            last = (qi + 1) * per - 1

            def kblock(step, state):
                acc, rem0 = state
                kj = last - step
                ks = pl.ds(pl.multiple_of(kj * n, n), n)
                z = _dot_nt(q, k16[ks, :])
                strict = (col + kj * n) < (row + qi * m)
                spz = _softplus(z)
                sp = jnp.where(strict, spz, 0.0)
                rem = rem0 + _split_dot(sp, after, 2)
                a = jnp.where(strict, jnp.exp(z - spz - rem), 0.0)
                acc = acc + _dot(a.astype(BF16), v16[ks, :])
                return acc, rem0 + jnp.sum(sp, axis=1, keepdims=True)

            acc, _ = lax.fori_loop(0, last + 1, kblock,
                                   (jnp.zeros((m, HEAD), F32), jnp.zeros((m, 1), F32)))
            o_ref[qs, :] = acc
            return carry

        lax.fori_loop(0, nq, qblock, 0)

    col_spec = lambda p: pl.BlockSpec((t, HEAD), lambda h: (0, p * nh + h))
    return _host_call(
        body, 3, plan, lambda: (pl.program_id(0), nh),
        out_shape=jax.ShapeDtypeStruct((t, wd), F32), grid=(nh,),
        in_specs=[col_spec(0), col_spec(1), col_spec(2)],
        out_specs=pl.BlockSpec((t, HEAD), lambda h: (0, h)),
        scratch_shapes=[pltpu.VMEM((t, HEAD), BF16)] * 3,
        name=name, sem=("parallel",), args=[qkv, qkv, qkv])


def sba_bwd(qkv, o, do, name, plan=None):
    t, w3 = qkv.shape
    wd = w3 // 3
    m, n = SB_QROWS, SB_BLOCK
    nh, nq, per = wd // HEAD, t // m, m // n
    scale = HEAD ** -0.5

    def body(q_ref, k_ref, v_ref, o_ref, do_ref, d_ref, dk_ref, dv_ref, q16, k16, v16):
        row, col, after, from_ = _sb_masks()
        dk_ref[...] = jnp.zeros_like(dk_ref)
        dv_ref[...] = jnp.zeros_like(dv_ref)
        q16[...] = (q_ref[...] * scale).astype(BF16)
        k16[...] = k_ref[...].astype(BF16)
        v16[...] = v_ref[...].astype(BF16)

        def qblock(qi, carry):
            qs = pl.ds(pl.multiple_of(qi * m, m), m)
            q = q16[qs, :]
            dov = do_ref[qs, :]
            do16 = dov.astype(BF16)
            dsum = jnp.sum(do16.astype(F32) * o_ref[qs, :], axis=1, keepdims=True)
            last = (qi + 1) * per - 1

            def kblock(step, state):
                dq, rem0, e0 = state
                kj = last - step
                ks = pl.ds(pl.multiple_of(kj * n, n), n)
                kv, vv = k16[ks, :], v16[ks, :]
                z = _dot_nt(q, kv)
                strict = (col + kj * n) < (row + qi * m)
                spz = _softplus(z)
                sp = jnp.where(strict, spz, 0.0)
                rem = rem0 + _split_dot(sp, after, 2)
                sgz = jnp.exp(z - spz)
                a = jnp.where(strict, sgz * jnp.exp(-rem), 0.0).astype(BF16)
                e = a.astype(F32) * _dot_nt(do16, vv)
                left = dsum - (e0 + _split_dot(e, from_, 2))
                dz = jnp.where(strict, e * (1.0 - sgz) - sgz * left, 0.0).astype(BF16)
                dq = dq + _dot(dz, kv)
                dk_ref[ks, :] += _dot_tn(dz, q)
                dv_ref[ks, :] += _dot_tn(a, do16)
                return (dq, rem0 + jnp.sum(sp, axis=1, keepdims=True),
                        e0 + jnp.sum(e, axis=1, keepdims=True))

            zero1 = jnp.zeros((m, 1), F32)
            dq, _, _ = lax.fori_loop(0, last + 1, kblock, (jnp.zeros((m, HEAD), F32), zero1, zero1))
            d_ref[0, qs, :] = (dq * scale).astype(BF16)
            return carry

        lax.fori_loop(0, nq, qblock, 0)
        d_ref[1, :, :] = dk_ref[...].astype(BF16)
        d_ref[2, :, :] = dv_ref[...].astype(BF16)

    col_spec = lambda p: pl.BlockSpec((t, HEAD), lambda h: (0, p * nh + h))
    head = pl.BlockSpec((t, HEAD), lambda h: (0, h))
    return _host_call(
        body, 5, plan, lambda: (pl.program_id(0), nh),
        out_shape=jax.ShapeDtypeStruct((3, t, wd), BF16), grid=(nh,),
        in_specs=[col_spec(0), col_spec(1), col_spec(2), head, head],
        out_specs=pl.BlockSpec((3, t, HEAD), lambda h: (0, 0, h)),
        scratch_shapes=[pltpu.VMEM((t, HEAD), F32)] * 2 + [pltpu.VMEM((t, HEAD), BF16)] * 3,
        name=name, sem=("parallel",), args=[qkv, qkv, qkv, o, do])


def _pool_band(i_out, i_in, tr, win, transpose):
    r = lax.broadcasted_iota(jnp.int32, (tr, tr), 0) + i_out * tr
    c = lax.broadcasted_iota(jnp.int32, (tr, tr), 1) + i_in * tr
    if transpose:
        return ((r <= c) & (r > c - win)).astype(F32)
    return ((c <= r) & (c > r - win)).astype(F32)


def _pool_p(u_ref, i, tr, win):
    cur = u_ref[i * tr:(i + 1) * tr, :]
    ws = _dot(_pool_band(i, i, tr, win, False), cur, HI)
    if i > 0:
        ws = ws + _dot(_pool_band(i, i - 1, tr, win, False), u_ref[(i - 1) * tr:i * tr, :], HI)
    pos = lax.broadcasted_iota(jnp.int32, (tr, 1), 0) + (i * tr + 1)
    count = jnp.minimum(pos, win).astype(F32)
    return ws / count - cur, count


def pool_fwd(u, pool_w, pool_scale, name):
    t, d = u.shape
    ng = len(POOL_WINDOWS)
    gs = d // ng
    tr = min(ROW_TILE, t)

    def body(u_ref, w_ref, s_ref, y_ref):
        win = jnp.left_shift(2, pl.program_id(0))
        for i in range(t // tr):
            p, _ = _pool_p(u_ref, i, tr, win)
            y_ref[i * tr:(i + 1) * tr, :] = _dot(p.astype(BF16), w_ref[...]) * s_ref[...]

    grp = pl.BlockSpec((t, gs), lambda g: (0, g))
    return pl.pallas_call(
        body, out_shape=jax.ShapeDtypeStruct((t, d), F32), grid=(ng,),
        in_specs=[grp, pl.BlockSpec((None, gs, gs), lambda g: (g, 0, 0)), pl.BlockSpec((1, gs), lambda g: (0, g))],
        out_specs=grp, name=name, compiler_params=_params(("parallel",)),
    )(u, pool_w, pool_scale)


def pool_bwd(u, pool_w, pool_scale, dy, name):
    t, d = u.shape
    ng = len(POOL_WINDOWS)
    gs = d // ng
    tr = min(ROW_TILE, t)
    nt = t // tr

    def body(u_ref, w_ref, s_ref, dy_ref, du_ref, dw_ref, ds_ref, dpc_ref, dp_ref):
        win = jnp.left_shift(2, pl.program_id(0))
        wv = w_ref[...]
        dw = jnp.zeros((gs, gs), F32)
        dsc = jnp.zeros((1, gs), F32)
        for i in range(nt):
            rows = slice(i * tr, (i + 1) * tr)
            p, count = _pool_p(u_ref, i, tr, win)
            p16 = p.astype(BF16)
            dyv = dy_ref[rows, :]
            dsc = dsc + jnp.sum(dyv * _dot(p16, wv), axis=0, keepdims=True)
            dyp = (dyv * s_ref[...]).astype(BF16)
            dw = dw + _dot_tn(p16, dyp)
            dp = _dot_nt(dyp, wv)
            dp_ref[rows, :] = dp
            dpc_ref[rows, :] = dp / count
        dw_ref[...] = dw.astype(BF16)
        ds_ref[...] = dsc
        for i in range(nt):
            rows = slice(i * tr, (i + 1) * tr)
            acc = _dot(_pool_band(i, i, tr, win, True), dpc_ref[rows, :], HI)
            if i + 1 < nt:
                acc = acc + _dot(_pool_band(i, i + 1, tr, win, True), dpc_ref[(i + 1) * tr:(i + 2) * tr, :], HI)
            du_ref[rows, :] = acc - dp_ref[rows, :]

    grp = pl.BlockSpec((t, gs), lambda g: (0, g))
    wspec = pl.BlockSpec((None, gs, gs), lambda g: (g, 0, 0))
    vec = pl.BlockSpec((1, gs), lambda g: (0, g))
    return pl.pallas_call(
        body,
        out_shape=[jax.ShapeDtypeStruct((t, d), F32), jax.ShapeDtypeStruct((ng, gs, gs), BF16),
                   jax.ShapeDtypeStruct((1, d), F32)],
        grid=(ng,), in_specs=[grp, wspec, vec, grp], out_specs=[grp, wspec, vec],
        scratch_shapes=[pltpu.VMEM((t, gs), F32), pltpu.VMEM((t, gs), F32)],
        name=name, compiler_params=_params(("parallel",)),
    )(u, pool_w, pool_scale, dy)


CONV_COLS = 256
HALO = 8


def _conv_taps(ref, r0, tr):
    x = ref[r0:r0 + tr, :]
    prev = ref[r0 - HALO:r0, :] if r0 > 0 else jnp.zeros((HALO, x.shape[1]), F32)
    xx = jnp.concatenate([prev, x], axis=0)
    return x, pltpu.roll(xx, 1, 0)[HALO:, :], pltpu.roll(xx, 2, 0)[HALO:, :]


def _conv_out(taps, w_ref, b_ref):
    x, s1, s2 = taps
    return w_ref[0:1, :] * s2 + w_ref[1:2, :] * s1 + w_ref[2:3, :] * x + b_ref[...]


def conv_glu_fwd(up, conv_w, conv_b, name):
    t, f2 = up.shape
    f = f2 // 2
    tc = min(CONV_COLS, f)
    nj = f // tc
    tr = min(ROW_TILE, t)

    def body(ug_ref, uv_ref, wg_ref, wv_ref, bg_ref, bv_ref, o_ref):
        for i in range(t // tr):
            r0 = i * tr
            gate = _conv_out(_conv_taps(ug_ref, r0, tr), wg_ref, bg_ref)
            val = _conv_out(_conv_taps(uv_ref, r0, tr), wv_ref, bv_ref)
            o_ref[r0:r0 + tr, :] = (gate * _sigmoid(gate) * val).astype(BF16)

    blk = lambda rows, half: pl.BlockSpec((rows, tc), lambda j: (0, half * nj + j))
    return pl.pallas_call(
        body, out_shape=jax.ShapeDtypeStruct((t, f), BF16), grid=(nj,),
        in_specs=[blk(t, 0), blk(t, 1), blk(CONV_WIDTH, 0), blk(CONV_WIDTH, 1), blk(1, 0), blk(1, 1)],
        out_specs=pl.BlockSpec((t, tc), lambda j: (0, j)),
        name=name, compiler_params=_params(("parallel",)),
    )(up, up, conv_w, conv_w, conv_b, conv_b)


def conv_glu_bwd(up, conv_w, conv_b, dact, name):
    t, f2 = up.shape
    f = f2 // 2
    tc = min(CONV_COLS, f)
    nj = f // tc
    tr = min(ROW_TILE, t)
    nt = t // tr

    def body(ug_ref, uv_ref, wg_ref, wv_ref, bg_ref, bv_ref, da_ref, du_ref, dw_ref, db_ref, dg_ref, dv_ref):
        dwg = [jnp.zeros((1, tc), F32) for _ in range(CONV_WIDTH)]
        dwv = [jnp.zeros((1, tc), F32) for _ in range(CONV_WIDTH)]
        dbg = jnp.zeros((1, tc), F32)
        dbv = jnp.zeros((1, tc), F32)
        for i in range(nt):
            r0 = i * tr
            tg, tv = _conv_taps(ug_ref, r0, tr), _conv_taps(uv_ref, r0, tr)
            gate, val = _conv_out(tg, wg_ref, bg_ref), _conv_out(tv, wv_ref, bv_ref)
            sg = _sigmoid(gate)
            da = da_ref[r0:r0 + tr, :]
            d_gate = da * val * (sg * (1.0 + gate * (1.0 - sg)))
            d_val = da * (gate * sg)
            dg_ref[r0:r0 + tr, :] = d_gate
            dv_ref[r0:r0 + tr, :] = d_val
            dbg = dbg + jnp.sum(d_gate, axis=0, keepdims=True)
            dbv = dbv + jnp.sum(d_val, axis=0, keepdims=True)
            for tap in range(CONV_WIDTH):
                dwg[tap] = dwg[tap] + jnp.sum(d_gate * tg[2 - tap], axis=0, keepdims=True)
                dwv[tap] = dwv[tap] + jnp.sum(d_val * tv[2 - tap], axis=0, keepdims=True)
        for tap in range(CONV_WIDTH):
            dw_ref[0, tap:tap + 1, :] = dwg[tap]
            dw_ref[1, tap:tap + 1, :] = dwv[tap]
        db_ref[0, :, :] = dbg
        db_ref[1, :, :] = dbv
        for half, (d_ref, w_ref) in enumerate(((dg_ref, wg_ref), (dv_ref, wv_ref))):
            for i in range(nt):
                r0 = i * tr
                x = d_ref[r0:r0 + tr, :]
                nxt = d_ref[r0 + tr:r0 + tr + HALO, :] if i + 1 < nt else jnp.zeros((HALO, tc), F32)
                xx = jnp.concatenate([x, nxt], axis=0)
                up1 = pltpu.roll(xx, tr + HALO - 1, 0)[:tr, :]
                up2 = pltpu.roll(xx, tr + HALO - 2, 0)[:tr, :]
                du = w_ref[2:3, :] * x + w_ref[1:2, :] * up1 + w_ref[0:1, :] * up2
                du_ref[half, r0:r0 + tr, :] = du.astype(BF16)

    blk = lambda rows, half: pl.BlockSpec((rows, tc), lambda j: (0, half * nj + j))
    return pl.pallas_call(
        body,
        out_shape=[jax.ShapeDtypeStruct((2, t, f), BF16), jax.ShapeDtypeStruct((2, CONV_WIDTH, f), F32),
                   jax.ShapeDtypeStruct((2, 1, f), F32)],
        grid=(nj,),
        in_specs=[blk(t, 0), blk(t, 1), blk(CONV_WIDTH, 0), blk(CONV_WIDTH, 1), blk(1, 0), blk(1, 1),
                  pl.BlockSpec((t, tc), lambda j: (0, j))],
        out_specs=[pl.BlockSpec((2, t, tc), lambda j: (0, 0, j)),
                   pl.BlockSpec((2, CONV_WIDTH, tc), lambda j: (0, 0, j)),
                   pl.BlockSpec((2, 1, tc), lambda j: (0, 0, j))],
        scratch_shapes=[pltpu.VMEM((t, tc), F32), pltpu.VMEM((t, tc), F32)],
        name=name, compiler_params=_params(("parallel",)),
    )(up, up, conv_w, conv_w, conv_b, conv_b, dact)


def _place():
    x, y, c = lax.axis_index("x"), lax.axis_index("y"), lax.axis_index("c")
    others = [(1 - x, y), (x, 1 - y), (1 - x, 1 - y)]
    return x, y, c, others


def _window(ref, axis, b, n):
    if axis == 1:
        return ref.at[:, pl.ds(b * n, n), :]
    return ref.at[:, :, pl.ds(b * n, n)]


def _remote(src, dst, send_sems, recv_sems, k, to):
    return pltpu.make_async_remote_copy(src_ref=src, dst_ref=dst, send_sem=send_sems.at[k],
                                        recv_sem=recv_sems.at[k], device_id=to, device_id_type=MESH)


class GatherPlan:
    has_mid = True

    def __init__(self, items):
        self.items = items
        self.args = [s for s, _, _, _ in items]
        self.out_shape = []
        for s, _, nl, ax in items:
            shp = [nl, s.shape[1], s.shape[2]]
            shp[ax] *= N_DEV
            self.out_shape.append(jax.ShapeDtypeStruct(tuple(shp), s.dtype))
        n = len(items)
        self.scratch = [pltpu.SemaphoreType.DMA((7 * n,)), pltpu.SemaphoreType.DMA((7 * n,)),
                        pltpu.SemaphoreType.DMA((n,))]

    def _mine(self, ins, a):
        _, l0, nl, _ = self.items[a]
        return ins[a].at[pl.ds(l0, nl)]

    def _copy(self, ins, outs, sems, a, k, block, to, own=False):
        s, _, _, ax = self.items[a]
        px, py, pc = block
        w = _window(outs[a], ax, 4 * px + 2 * py + pc, s.shape[ax])
        return _remote(self._mine(ins, a) if own else w, w, sems[0], sems[1], 7 * a + k, to)

    def _local(self, ins, outs, sems, a, x, y, c):
        s, _, _, ax = self.items[a]
        return pltpu.make_async_copy(self._mine(ins, a), _window(outs[a], ax, 4 * x + 2 * y + c, s.shape[ax]),
                                     sems[2].at[a])

    def _first(self, ins, outs, sems, a, x, y, c, others):
        me = (x, y, c)
        return [self._copy(ins, outs, sems, a, 0, me, (x, y, 1 - c), own=True)] + [
            self._copy(ins, outs, sems, a, 1 + j, me, (*chip, c), own=True) for j, chip in enumerate(others)]

    def start(self, ins, outs, sems):
        x, y, c, others = _place()
        for a in range(len(self.items)):
            self._local(ins, outs, sems, a, x, y, c).start()
        for a in range(len(self.items)):
            for cp in self._first(ins, outs, sems, a, x, y, c, others):
                cp.start()

    def mid(self, ins, outs, sems):
        x, y, c, others = _place()
        for a in range(len(self.items)):
            for j, chip in enumerate(others):
                self._copy(ins, outs, sems, a, 1 + j, (*chip, c), (x, y, c)).wait_recv()
                self._copy(ins, outs, sems, a, 4 + j, (*chip, c), (x, y, 1 - c)).start()

    def finish(self, ins, outs, sems):
        x, y, c, others = _place()
        for a in range(len(self.items)):
            self._copy(ins, outs, sems, a, 0, (x, y, 1 - c), (x, y, c)).wait_recv()
            for j, chip in enumerate(others):
                self._copy(ins, outs, sems, a, 4 + j, (*chip, 1 - c), (x, y, c)).wait_recv()
        for a in range(len(self.items)):
            for cp in self._first(ins, outs, sems, a, x, y, c, others):
                cp.wait_send()
            for j, chip in enumerate(others):
                self._copy(ins, outs, sems, a, 4 + j, (*chip, c), (x, y, 1 - c)).wait_send()
            self._local(ins, outs, sems, a, x, y, c).wait()


class ExchangePlan:
    has_mid = False

    def __init__(self, partials):
        self.args = list(partials)
        self.out_shape = [jax.ShapeDtypeStruct(p.shape, p.dtype) for p in partials]
        n = len(partials)
        self.scratch = [pltpu.SemaphoreType.DMA((3 * n,)), pltpu.SemaphoreType.DMA((3 * n,)),
                        pltpu.SemaphoreType.DMA((n,))]

    def _copies(self, ins, outs, sems):
        x, y, c, others = _place()
        me = 2 * x + y
        local, sends, recvs = [], [], []
        for a in range(len(self.args)):
            local.append(pltpu.make_async_copy(ins[a].at[me], outs[a].at[me], sems[2].at[a]))
            for j, (px, py) in enumerate(others):
                sends.append(_remote(ins[a].at[2 * px + py], outs[a].at[me], sems[0], sems[1], 3 * a + j, (px, py, c)))
                slot = outs[a].at[2 * px + py]
                recvs.append(_remote(slot, slot, sems[0], sems[1], 3 * a + j, (px, py, c)))
        return local, sends, recvs

    def start(self, ins, outs, sems):
        local, sends, _ = self._copies(ins, outs, sems)
        for cp in local + sends:
            cp.start()

    def finish(self, ins, outs, sems):
        local, sends, recvs = self._copies(ins, outs, sems)
        for cp in recvs:
            cp.wait_recv()
        for cp in sends:
            cp.wait_send()
        for cp in local:
            cp.wait()


def run_plan(plan, name):
    ni, no = len(plan.args), len(plan.out_shape)

    def body(*refs):
        ins, outs, sems = refs[:ni], refs[ni:ni + no], refs[ni + no:]
        plan.start(ins, outs, sems)
        if plan.has_mid:
            plan.mid(ins, outs, sems)
        plan.finish(ins, outs, sems)

    return pl.pallas_call(
        body, out_shape=plan.out_shape, in_specs=[ANY] * ni, out_specs=[ANY] * no,
        scratch_shapes=plan.scratch, name=name,
    )(*plan.args)


def sibling_exchange(grads, axes):
    na = len(grads)
    widths = [g.shape[ax] // N_DEV for g, ax in zip(grads, axes)]
    out_shape = []
    for g, ax, n in zip(grads, axes, widths):
        shp = list(g.shape)
        shp[ax] = n
        out_shape.append(jax.ShapeDtypeStruct((N_CHIP, *shp), g.dtype))

    def body(*refs):
        ins, outs = refs[:na], refs[na:2 * na]
        send_sems, recv_sems = refs[2 * na:]
        x, y, c, _ = _place()
        copies = []
        for a in range(na):
            for q in range(N_CHIP):
                src = _window(ins[a], axes[a], 2 * q + (1 - c), widths[a])
                cp = _remote(src, outs[a].at[q], send_sems, recv_sems, N_CHIP * a + q, (x, y, 1 - c))
                cp.start()
                copies.append(cp)
        for cp in copies:
            cp.wait_recv()
        for cp in copies:
            cp.wait_send()

    return pl.pallas_call(
        body, out_shape=out_shape, in_specs=[ANY] * na, out_specs=[ANY] * na,
        scratch_shapes=[pltpu.SemaphoreType.DMA((N_CHIP * na,)), pltpu.SemaphoreType.DMA((N_CHIP * na,))],
        name="sibling_exchange",
    )(*grads)


def _peer_of(k, x, y, c):
    return (1 - x if k & 4 else x, 1 - y if k & 2 else y, 1 - c if k & 1 else c)


def small_exchange(vec, reduce, name):
    r = vec.shape[0]

    def body(v_ref, o_ref, *scratch):
        if reduce:
            buf, send_sems, recv_sems = scratch
        else:
            buf, (send_sems, recv_sems) = o_ref, scratch
        x, y, c, _ = _place()
        me = 4 * x + 2 * y + c
        copies = []
        for k in range(1, N_DEV):
            cp = _remote(v_ref, buf.at[me], send_sems, recv_sems, k - 1, _peer_of(k, x, y, c))
            cp.start()
            copies.append(cp)
        buf[me] = v_ref[...]
        for k in range(1, N_DEV):
            px, py, pc = _peer_of(k, x, y, c)
            slot = buf.at[4 * px + 2 * py + pc]
            _remote(slot, slot, send_sems, recv_sems, k - 1, (px, py, pc)).wait_recv()
        for cp in copies:
            cp.wait_send()
        if reduce:
            tot = buf[0]
            for b in range(1, N_DEV):
                tot = tot + buf[b]
            o_ref[...] = tot

    sems = [pltpu.SemaphoreType.DMA((N_DEV - 1,)), pltpu.SemaphoreType.DMA((N_DEV - 1,))]
    if reduce:
        out_shape = jax.ShapeDtypeStruct((r, LANES), F32)
        scratch = [pltpu.VMEM((N_DEV, r, LANES), F32)] + sems
    else:
        out_shape = jax.ShapeDtypeStruct((N_DEV, r, LANES), F32)
        scratch = sems
    return pl.pallas_call(
        body, out_shape=out_shape, in_specs=[VMEM_FULL], out_specs=VMEM_FULL,
        scratch_shapes=scratch, name=name, compiler_params=_params(None),
    )(vec)


def _row_tile(rows, cols):
    best = None
    for tb in range(16, rows + 1, 16):
        if rows % tb == 0 and tb * cols <= ELEMWISE_BLOCK_ELEMS:
            best = tb
    return best if best is not None else rows


def add_sibling(grad, recv, axis, core):
    nl = grad.shape[0]
    _, _, r, cc = recv.shape
    tb = _row_tile(r, cc)
    per = r // tb

    def body(c_ref, g_ref, r_ref, o_ref):
        del c_ref
        o_ref[...] = (g_ref[...].astype(F32) + r_ref[...].astype(F32)).astype(BF16)

    if axis == 2:
        g_spec = pl.BlockSpec((None, tb, cc), lambda q, l, i, c_ref: (l, i, 2 * q + c_ref[0]))
    else:
        g_spec = pl.BlockSpec((None, tb, cc), lambda q, l, i, c_ref: (l, (2 * q + c_ref[0]) * per + i, 0))
    slot = pl.BlockSpec((None, None, tb, cc), lambda q, l, i, c_ref: (q, l, i, 0))
    return pl.pallas_call(
        body, out_shape=jax.ShapeDtypeStruct(recv.shape, BF16),
        grid_spec=pltpu.PrefetchScalarGridSpec(
            num_scalar_prefetch=1, grid=(N_CHIP, nl, per), in_specs=[g_spec, slot], out_specs=slot),
        name="add_sibling", compiler_params=_params(("parallel", "parallel", "parallel")),
    )(core, grad, recv)


def _adamw(w, g, m, v):
    m = ADAM_B1 * m + (1.0 - ADAM_B1) * g
    v = ADAM_B2 * v + (1.0 - ADAM_B2) * (g * g)
    m_hat = m / (1.0 - ADAM_B1 ** ADAM_STEP)
    v_hat = v / (1.0 - ADAM_B2 ** ADAM_STEP)
    delta = -ADAM_LR * (m_hat / (jnp.sqrt(v_hat) + ADAM_EPS) + ADAM_WD * w)
    return delta, m, v


def adam_from_partials(recv, w, m, v, l0, bufs):
    nl = recv.shape[1]
    _, r, cc = w.shape
    tb = _row_tile(r, cc)

    def body(p0, p1, p2, p3, w_ref, m_ref, v_ref, b0, b1, b2, b3, g_out, d_out, m_out, v_out):
        del b0, b1, b2, b3
        g = p0[...].astype(F32) + p1[...].astype(F32) + p2[...].astype(F32) + p3[...].astype(F32)
        d, mn, vn = _adamw(w_ref[...], g, m_ref[...], v_ref[...])
        g_out[...], d_out[...], m_out[...], v_out[...] = g, d, mn, vn

    slot = lambda q: pl.BlockSpec((None, None, tb, cc), lambda l, i: (q, l, i, 0))
    blk = pl.BlockSpec((None, tb, cc), lambda l, i: (l0 + l, i, 0))
    shp = jax.ShapeDtypeStruct(w.shape, F32)
    return pl.pallas_call(
        body, out_shape=[shp] * 4, grid=(nl, r // tb),
        in_specs=[slot(0), slot(1), slot(2), slot(3), blk, blk, blk] + [ANY] * 4, out_specs=[blk] * 4,
        input_output_aliases={7: 0, 8: 1, 9: 2, 10: 3},
        name="adam_big", compiler_params=_params(("parallel", "parallel")),
    )(recv, recv, recv, recv, w, m, v, *bufs)


def adam_small(w, g, m, v):
    def body(w_ref, g_ref, m_ref, v_ref, d_out, m_out, v_out):
        d_out[...], m_out[...], v_out[...] = _adamw(w_ref[...], g_ref[...], m_ref[...], v_ref[...])

    shp = jax.ShapeDtypeStruct(w.shape, F32)
    return pl.pallas_call(body, out_shape=[shp] * 3, name="adam_small")(w, g, m, v)


def _pack(arrays, multiple=8 * LANES):
    flat = jnp.concatenate([a.reshape(-1) for a in arrays])
    pad = (-flat.shape[0]) % multiple
    if pad:
        flat = jnp.concatenate([flat, jnp.zeros((pad,), flat.dtype)])
    return flat.reshape(-1, LANES)


def _unpack(packed, shapes):
    flat = packed.reshape(packed.shape[:-2] + (-1,))
    out, off = [], 0
    for shp in shapes:
        n = math.prod(shp)
        out.append(flat[..., off:off + n].reshape(packed.shape[:-2] + tuple(shp)))
        off += n
    return out


def _unshard_last(stacked):
    moved = jnp.moveaxis(stacked, 0, -2)
    return moved.reshape(moved.shape[:-2] + (-1,))


def _shard_last(full, block):
    n = full.shape[-1] // N_DEV
    return lax.dynamic_slice_in_dim(full, block * n, n, axis=full.ndim - 1)


BIG_AXIS = {"w_in": 2, "w_out": 1, "w_qkv": 2, "w_so": 1, "w_pool": 1, "w_up": 2, "w_down": 1}
BIG_ORDER = ("w_in", "w_out", "w_qkv", "w_so", "w_pool", "w_up", "w_down")

GATHER_HOSTS = {
    "hg_in_0": (("w_out", 0),),
    "hgrn_fwd_0": (("w_up", 0),),
    "ffn_up_0": (("w_down", 0),),
    "ffn_down_0": (("w_qkv", 0),),
    "sb_qkv_1": (("w_so", 0),),
    "sba_fwd_1": (("w_up", 1), ("w_down", 1), ("w_pool", 0), ("w_up", 2)),
    "ffn_up_1": (("w_down", 2),),
    "ffn_up_2": (("w_in", 1),),
    "ffn_down_2": (("w_out", 1),),
    "hgrn_fwd_3": (("w_up", 3),),
    "ffn_up_3": (("w_down", 3),),
}


class LocalWeights:
    def __init__(self, full):
        self.full, self.grads = full, {}

    def gather_plan(self, host):
        return None

    def gathered(self, outs):
        pass

    def weight(self, kind, l):
        return self.full[kind][l] if kind != "w_pool" else self.full[kind]

    def grad(self, kind, l, g):
        self.grads[(kind, l)] = g

    def exchange_plan(self):
        return None

    def exchanged(self, outs):
        pass


class MeshWeights:
    def __init__(self, shards16, w, m, v, core):
        self.shards, self.w, self.m, self.v, self.core = shards16, w, m, v, core
        self.full, self.pending, self.flying, self.keys = {}, [], [], None
        self.out = {k: [lax.empty(w[k].shape, F32) for _ in range(4)] for k in BIG_ORDER}

    def _items(self, keys):
        return [(self.shards[k], 0 if k == "w_pool" else l, 4 if k == "w_pool" else 1, BIG_AXIS[k]) for k, l in keys]

    def gather_plan(self, host):
        self.keys = [key for key in GATHER_HOSTS.get(host, ()) if key[0] in self.shards
                     and key[1] < (1 if key[0] == "w_pool" else self.shards[key[0]].shape[0])]
        return GatherPlan(self._items(self.keys)) if self.keys else None

    def gathered(self, outs):
        for (k, l), o in zip(self.keys, outs):
            self.full[(k, l)] = o if k == "w_pool" else o[0]
        self.keys = None

    def weight(self, kind, l):
        if (kind, l) not in self.full:
            (out,) = run_plan(GatherPlan(self._items([(kind, l)])), f"gather_{kind}_{l}")
            self.full[(kind, l)] = out if kind == "w_pool" else out[0]
        return self.full[(kind, l)]

    def grad(self, kind, l, g):
        self.pending.append((kind, l, g if g.ndim == 3 else g[None]))

    def exchange_plan(self):
        if not self.pending:
            return None
        self.flying, self.pending = self.pending, []
        grads = [g for _, _, g in self.flying]
        axes = [BIG_AXIS[k] for k, _, _ in self.flying]
        recv = sibling_exchange(grads, axes)
        return ExchangePlan([add_sibling(g, r, ax, self.core) for g, r, ax in zip(grads, recv, axes)])

    def exchanged(self, outs):
        for (k, l, _), r in zip(self.flying, outs):
            self.out[k] = adam_from_partials(r, self.w[k], self.m[k], self.v[k], l, self.out[k])
        self.flying = []

    def finish(self):
        plan = self.exchange_plan()
        if plan is not None:
            self.exchanged(run_plan(plan, "chip_exchange_tail"))
        return self.out


def train_step(x, target, norm_g, lb_logits, onorm_g, pool_scale, conv_w, conv_b, wts):
    t, d = x.shape
    depth = norm_g.shape[0]
    ng = lambda i, j: norm_g[i, j].reshape(1, d)
    lbs = lower_bounds_fwd(lb_logits)
    saved = []
    h = x
    _, u16, u32 = res_norm(h, None, None, ng(0, 0), "norm_in")

    def hosted(call, *args, **kw):
        plan = wts.gather_plan(kw["name"])
        out, extra = call(*args, plan=plan, **kw)
        if plan is not None:
            wts.gathered(extra)
        return out

    def project(a, kind, l, name):
        plan = wts.gather_plan(name)
        if plan is None:
            return matmul(a, wts.weight(kind, l), "nn", F32, name)
        out, extra = matmul(a, wts.weight(kind, l), "nn", F32, name, plan=plan)
        wts.gathered(extra)
        return out

    for i in range(depth):
        kind, j = i % 3, i // 3
        s = {"h_in": h, "u1": u16}
        if kind == 0:
            s["proj"] = project(u16, "w_in", j, f"hg_in_{i}")
            s["y"] = hosted(hgrn_fwd, s["proj"], lbs[i].reshape(1, -1), onorm_g[j].reshape(1, -1), i > 0,
                            name=f"hgrn_fwd_{i}")
            mix = project(s["y"], "w_out", j, f"hg_out_{i}")
        elif kind == 1:
            s["qkv"] = project(u16, "w_qkv", j, f"sb_qkv_{i}")
            s["o"] = hosted(sba_fwd, s["qkv"], name=f"sba_fwd_{i}")
            mix = project(s["o"], "w_so", j, f"sb_out_{i}")
        else:
            s["u1f"] = u32
            mix = pool_fwd(u32, wts.weight("w_pool", j), pool_scale[j].reshape(1, d), f"pool_fwd_{i}")
        s["mix"] = mix
        h_mid, u2, _ = res_norm(h, mix, ng(i, 1), ng(i, 2), f"norm_mid_{i}")
        s["h_mid"], s["u2"] = h_mid, u2
        s["up"] = project(u2, "w_up", i, f"ffn_up_{i}")
        s["act"] = conv_glu_fwd(s["up"], conv_w[i], conv_b[i].reshape(1, -1), f"glu_fwd_{i}")
        s["f"] = project(s["act"], "w_down", i, f"ffn_down_{i}")
        nxt = ng(i + 1, 0) if i + 1 < depth else None
        h, u16, u32 = res_norm(h_mid, s["f"], ng(i, 3), nxt, f"norm_out_{i}",
                               want_f32=(nxt is not None and (i + 1) % 3 == 2))
        saved.append(s)

    loss_acc, dh = loss_head(h, target, "loss_head")

    d_norm = [[None] * 4 for _ in range(depth)]
    d_lbs = jnp.zeros_like(lbs)
    d_onorm = [None] * onorm_g.shape[0]
    d_pscale = [None] * pool_scale.shape[0]
    d_cw, d_cb = [None] * depth, [None] * depth

    def exchanging(call, *args, **kw):
        plan = wts.exchange_plan()
        out, extra = call(*args, plan=plan, **kw)
        if plan is not None:
            wts.exchanged(extra)
        return out

    for i in reversed(range(depth)):
        kind, j = i % 3, i // 3
        s = saved[i]
        df, d_norm[i][3] = norm_bwd(s["f"], ng(i, 3), dh, None, BF16, f"nb_out_{i}")
        dact = matmul(df, wts.weight("w_down", i), "nt", F32, f"d_act_{i}")
        wts.grad("w_down", i, matmul(s["act"], df, "tn", BF16, f"dw_down_{i}"))
        dup, dcw, dcb = conv_glu_bwd(s["up"], conv_w[i], conv_b[i].reshape(1, -1), dact, f"glu_bwd_{i}")
        d_cw[i] = jnp.moveaxis(dcw, 0, 1).reshape(CONV_WIDTH, -1)
        d_cb[i] = dcb.reshape(-1)
        du2 = matmul(Split(dup), wts.weight("w_up", i), "nt", F32, f"d_u2_{i}")
        wts.grad("w_up", i, matmul(s["u2"], Split(dup), "tn", BF16, f"dw_up_{i}"))
        dh_mid, d_norm[i][2] = norm_bwd(s["h_mid"], ng(i, 2), du2, dh, F32, f"nb_mid_{i}")
        dm, d_norm[i][1] = norm_bwd(s["mix"], ng(i, 1), dh_mid, None, F32 if kind == 2 else BF16, f"nb_mix_{i}")
        if kind == 0:
            dy = matmul(dm, wts.weight("w_out", j), "nt", F32, f"d_y_{i}")
            wts.grad("w_out", j, matmul(s["y"], dm, "tn", BF16, f"dw_hgout_{i}"))
            dproj, d_onorm[j], dlb = exchanging(hgrn_bwd, s["proj"], lbs[i].reshape(1, -1),
                                                onorm_g[j].reshape(1, -1), dy, i > 0, name=f"hgrn_bwd_{i}")
            d_lbs = d_lbs.at[i].set(dlb[0])
            du1 = matmul(Split(dproj), wts.weight("w_in", j), "nt", F32, f"d_u1_{i}")
            wts.grad("w_in", j, matmul(s["u1"], Split(dproj), "tn", BF16, f"dw_hgin_{i}"))
        elif kind == 1:
            do = matmul(dm, wts.weight("w_so", j), "nt", F32, f"d_o_{i}")
            wts.grad("w_so", j, matmul(s["o"], dm, "tn", BF16, f"dw_sbout_{i}"))
            dqkv = exchanging(sba_bwd, s["qkv"], s["o"], do, name=f"sba_bwd_{i}")
            du1 = matmul(Split(dqkv), wts.weight("w_qkv", j), "nt", F32, f"d_u1_{i}")
            wts.grad("w_qkv", j, matmul(s["u1"], Split(dqkv), "tn", BF16, f"dw_sbqkv_{i}"))
        else:
            du1, g_pool, d_pscale[j] = pool_bwd(s["u1f"], wts.weight("w_pool", j), pool_scale[j].reshape(1, d),
                                                dm, f"pool_bwd_{i}")
            wts.grad("w_pool", j, g_pool)
        dh, d_norm[i][0] = norm_bwd(s["h_in"], ng(i, 0), du1, dh_mid, F32, f"nb_in_{i}")

    small = {
        "norm_g": jnp.stack([jnp.stack([v.reshape(d) for v in row]) for row in d_norm]),
        "lb_logits": lower_bounds_bwd(lb_logits, d_lbs),
        "onorm_g": jnp.stack([v.reshape(-1) for v in d_onorm]),
        "pool_scale": jnp.stack([v.reshape(-1) for v in d_pscale]),
        "conv_w": jnp.stack(d_cw),
        "conv_b": jnp.stack(d_cb),
    }
    return loss_acc, dh, small


SMALL_SHARDED = ("norm_g", "onorm_g", "pool_scale", "conv_w")
SMALL_ORDER = ("norm_g", "lb_logits", "onorm_g", "pool_scale", "conv_w", "conv_b")


def kernel(x, norm_g, hgrn_lb_logits, hgrn_w_in, hgrn_onorm_g, hgrn_w_out, sba_w_qkv, sba_w_out, pool_w, pool_scale, ffn_w_up, ffn_conv_w, ffn_conv_b, ffn_w_down, loss_target, m_norm_g, m_hgrn_lb_logits, m_hgrn_w_in, m_hgrn_onorm_g, m_hgrn_w_out, m_sba_w_qkv, m_sba_w_out, m_pool_w, m_pool_scale, m_ffn_w_up, m_ffn_conv_w, m_ffn_conv_b, m_ffn_w_down, v_norm_g, v_hgrn_lb_logits, v_hgrn_w_in, v_hgrn_onorm_g, v_hgrn_w_out, v_sba_w_qkv, v_sba_w_out, v_pool_w, v_pool_scale, v_ffn_w_up, v_ffn_conv_w, v_ffn_conv_b, v_ffn_w_down):
    cx, cy, cc = lax.axis_index("x"), lax.axis_index("y"), lax.axis_index("c")
    block = 4 * cx + 2 * cy + cc
    core = cc.astype(jnp.int32).reshape(1)

    pool3 = lambda a: a.reshape(a.shape[0] * a.shape[1], a.shape[2], a.shape[3])
    big_w = dict(zip(BIG_ORDER, [hgrn_w_in, hgrn_w_out, sba_w_qkv, sba_w_out, pool3(pool_w), ffn_w_up, ffn_w_down]))
    big_m = dict(zip(BIG_ORDER, [m_hgrn_w_in, m_hgrn_w_out, m_sba_w_qkv, m_sba_w_out, pool3(m_pool_w), m_ffn_w_up,
                                 m_ffn_w_down]))
    big_v = dict(zip(BIG_ORDER, [v_hgrn_w_in, v_hgrn_w_out, v_sba_w_qkv, v_sba_w_out, pool3(v_pool_w), v_ffn_w_up,
                                 v_ffn_w_down]))

    sharded = {"norm_g": norm_g, "onorm_g": hgrn_onorm_g, "pool_scale": pool_scale, "conv_w": ffn_conv_w}
    gathered = small_exchange(_pack([sharded[n] for n in SMALL_SHARDED]), False, "gather_small")
    parts = _unpack(gathered, [sharded[n].shape for n in SMALL_SHARDED])
    full = {n: _unshard_last(p) for n, p in zip(SMALL_SHARDED, parts)}

    wts = MeshWeights({k: w.astype(BF16) for k, w in big_w.items()}, big_w, big_m, big_v, core)
    loss_acc, grad_x, small_g = train_step(
        x[0], loss_target[0], full["norm_g"], hgrn_lb_logits, full["onorm_g"], full["pool_scale"],
        full["conv_w"], ffn_conv_b, wts)
    loss = lax.psum(loss_acc[0, 0], ("x", "y", "c"))

    shapes = [small_g[n].shape for n in SMALL_ORDER]
    summed = _unpack(small_exchange(_pack([small_g[n] for n in SMALL_ORDER]), True, "reduce_small"), shapes)
    sg = {n: (_shard_last(g, block) if n in SMALL_SHARDED else g) for n, g in zip(SMALL_ORDER, summed)}
    sw = {"norm_g": norm_g, "lb_logits": hgrn_lb_logits, "onorm_g": hgrn_onorm_g, "pool_scale": pool_scale,
          "conv_w": ffn_conv_w, "conv_b": ffn_conv_b}
    sm = {"norm_g": m_norm_g, "lb_logits": m_hgrn_lb_logits, "onorm_g": m_hgrn_onorm_g, "pool_scale": m_pool_scale,
          "conv_w": m_ffn_conv_w, "conv_b": m_ffn_conv_b}
    sv = {"norm_g": v_norm_g, "lb_logits": v_hgrn_lb_logits, "onorm_g": v_hgrn_onorm_g, "pool_scale": v_pool_scale,
          "conv_w": v_ffn_conv_w, "conv_b": v_ffn_conv_b}
    sshapes = [sw[n].shape for n in SMALL_ORDER]
    packed = [_pack([dct[n] for n in SMALL_ORDER]) for dct in (sw, sg, sm, sv)]
    s_delta, s_m, s_v = [dict(zip(SMALL_ORDER, _unpack(p, sshapes))) for p in adam_small(*packed)]

    upd = wts.finish()
    b_grad, b_delta, b_m, b_v = [[upd[k][n] for k in BIG_ORDER] for n in range(4)]

    def tree(small, bigs):
        bg = list(bigs)
        bg[4] = bg[4].reshape(pool_w.shape)
        return (small["norm_g"], small["lb_logits"], bg[0], small["onorm_g"], bg[1], bg[2], bg[3], bg[4],
                small["pool_scale"], bg[5], small["conv_w"], small["conv_b"], bg[6])

    return (loss, grad_x[None], *tree(sg, b_grad), *tree(s_delta, b_delta), *tree(s_m, b_m), *tree(s_v, b_v))
```

```python
import functools
import math

import jax
import jax.numpy as jnp
from jax import lax
from jax.experimental import pallas as pl
from jax.experimental.pallas import tpu as pltpu

F32 = jnp.float32
BF16 = jnp.bfloat16
HI = lax.Precision.HIGHEST
MESH = pl.DeviceIdType.MESH
ANY = pl.BlockSpec(memory_space=pl.ANY)
VMEM_FULL = pl.BlockSpec(memory_space=pltpu.VMEM)

NORM_EPS = 1e-6
HEAD = 128
HG_CHUNK = 128
HG_SUB = 32
HG_MAX_EXPONENT = 80.0
SB_BLOCK = 256
SB_QROWS = 256
POOL_WINDOWS = (2, 4, 8, 16)
CONV_WIDTH = 3
ROW_TILE = 256
N_DEV = 8
N_CHIP = 4

ADAM_LR = 0.001
ADAM_B1 = 0.9
ADAM_B2 = 0.999
ADAM_EPS = 1e-08
ADAM_WD = 0.01
ADAM_STEP = 10

VMEM_LIMIT = 48 * 1024 * 1024
LANES = 128
ELEMWISE_BLOCK_ELEMS = 256 * 1024


def _params(sem=None, vmem=VMEM_LIMIT):
    return pltpu.CompilerParams(dimension_semantics=sem, vmem_limit_bytes=vmem)


def _tile(n, prefs=(1024, 512, 256, 128)):
    for p in prefs:
        if n % p == 0:
            return p
    return n


def _dot(a, b, prec=None):
    return jnp.dot(a, b, precision=prec, preferred_element_type=F32)


def _dot_nt(a, b, prec=None):
    return lax.dot_general(a, b, (((1,), (1,)), ((), ())), precision=prec, preferred_element_type=F32)


def _dot_tn(a, b, prec=None):
    return lax.dot_general(a, b, (((0,), (0,)), ((), ())), precision=prec, preferred_element_type=F32)


def _split_dot(x, tri, parts, left=False):
    tot, rest = None, x
    for p in range(parts):
        h = rest.astype(BF16)
        d = _dot(tri, h) if left else _dot(h, tri)
        tot = d if tot is None else tot + d
        if p + 1 < parts:
            rest = rest - h.astype(F32)
    return tot


def _dot1(a, b, fn):
    return fn(a.astype(BF16), b.astype(BF16))


def _dot3(a, b, fn):
    ah, bh = a.astype(BF16), b.astype(BF16)
    al, bl = (a - ah.astype(F32)).astype(BF16), (b - bh.astype(F32)).astype(BF16)
    return fn(ah, bh) + fn(ah, bl) + fn(al, bh)


def _sigmoid(x):
    return jax.nn.sigmoid(x)


def _softplus(x):
    return jnp.maximum(x, 0.0) + jnp.log1p(jnp.exp(-jnp.abs(x)))


class Layer:
    def __init__(self, arr, l):
        self.arr, self.l = arr, l
        self.shape = arr.shape[1:]
        self.part = None


class Split:
    def __init__(self, arr):
        self.arr = arr
        self.shape = (arr.shape[1], arr.shape[0] * arr.shape[2])
        self.part = arr.shape[2]


class Plain:
    def __init__(self, arr):
        self.arr = arr
        self.shape = arr.shape
        self.part = None


def _wrap(op):
    return op if isinstance(op, (Layer, Split, Plain)) else Plain(op)


def _op_spec(op, br, bc, rc_of_grid):
    if isinstance(op, Layer):
        l = op.l
        return pl.BlockSpec((None, br, bc), lambda i, j, k: (l, *rc_of_grid(i, j, k)))
    if isinstance(op, Split):
        per = op.part // bc

        def imap(i, j, k):
            r, c = rc_of_grid(i, j, k)
            return (lax.div(c, per), r, lax.rem(c, per))
        return pl.BlockSpec((None, br, bc), imap)
    return pl.BlockSpec((br, bc), rc_of_grid)


def _hosted(body, n_in, n_out, plan, step_of_grid):
    if plan is None:
        return body
    pi, po, ps = len(plan.args), len(plan.out_shape), len(plan.scratch)

    def wrapped(*refs):
        refs = list(refs)
        ins, pins = refs[:n_in], refs[n_in:n_in + pi]
        outs = refs[n_in + pi:n_in + pi + n_out]
        pouts = refs[n_in + pi + n_out:n_in + pi + n_out + po]
        scr, pscr = refs[n_in + pi + n_out + po:len(refs) - ps], refs[len(refs) - ps:]
        step, nsteps = step_of_grid()

        @pl.when(step == 0)
        def _():
            plan.start(pins, pouts, pscr)

        if plan.has_mid:
            @pl.when(step == min((3 * nsteps) // 4, nsteps - 1))
            def _():
                plan.mid(pins, pouts, pscr)

        body(*ins, *outs, *scr)

        @pl.when(step == nsteps - 1)
        def _():
            plan.finish(pins, pouts, pscr)

    return wrapped


def _host_call(body, n_in, plan, step_of_grid, *, out_shape, grid, in_specs, out_specs, scratch_shapes, name, sem, args):
    single = not isinstance(out_shape, (list, tuple))
    out_shape = [out_shape] if single else list(out_shape)
    out_specs = [out_specs] if single else list(out_specs)
    n_out = len(out_shape)
    in_specs, scratch_shapes, args = list(in_specs), list(scratch_shapes), list(args)
    if plan is not None:
        in_specs += [ANY] * len(plan.args)
        args += plan.args
        out_shape += plan.out_shape
        out_specs += [ANY] * len(plan.out_shape)
        scratch_shapes += plan.scratch
        sem = ("arbitrary",) * len(grid)
    outs = pl.pallas_call(
        _hosted(body, n_in, n_out, plan, step_of_grid), out_shape=out_shape, grid=grid,
        in_specs=in_specs, out_specs=out_specs, scratch_shapes=scratch_shapes,
        name=name, compiler_params=_params(sem),
    )(*args)
    host = outs[0] if single else list(outs[:n_out])
    return host, list(outs[n_out:])


def matmul(a, b, mode, out_dtype, name, plan=None):
    a, b = _wrap(a), _wrap(b)
    if mode == "nn":
        (m, kd), (kd2, n) = a.shape, b.shape
    elif mode == "nt":
        (m, kd), (n, kd2) = a.shape, b.shape
    else:
        (kd, m), (kd2, n) = a.shape, b.shape
    assert kd == kd2, (mode, a.shape, b.shape)

    def dim_tile(full, ops_on_cols, prefs=(1024, 512, 256, 128)):
        base = full
        for op in ops_on_cols:
            if op.part is not None:
                base = math.gcd(base, op.part)
        return _tile(base, prefs)

    tm = dim_tile(m, [a] if mode == "tn" else [])
    tn = dim_tile(n, [b] if mode in ("nn", "tn") else [])
    tk = dim_tile(kd, ([a] if mode in ("nn", "nt") else []) + ([b] if mode == "nt" else []),
                  prefs=(2048, 2816, 1024, 512, 256, 128))
    nk = kd // tk

    if mode == "nn":
        a_spec = _op_spec(a, tm, tk, lambda i, j, k: (i, k))
        b_spec = _op_spec(b, tk, tn, lambda i, j, k: (k, j))
        dot = _dot
    elif mode == "nt":
        a_spec = _op_spec(a, tm, tk, lambda i, j, k: (i, k))
        b_spec = _op_spec(b, tn, tk, lambda i, j, k: (j, k))
        dot = _dot_nt
    else:
        a_spec = _op_spec(a, tk, tm, lambda i, j, k: (k, i))
        b_spec = _op_spec(b, tk, tn, lambda i, j, k: (k, j))
        dot = _dot_tn

    def body(a_ref, b_ref, o_ref, *acc):
        part = dot(a_ref[...].astype(BF16), b_ref[...].astype(BF16))
        if nk == 1:
            o_ref[...] = part.astype(o_ref.dtype)
            return
        (acc_ref,) = acc
        k = pl.program_id(2)

        @pl.when(k == 0)
        def _():
            acc_ref[...] = part

        @pl.when(k > 0)
        def _():
            acc_ref[...] += part

        @pl.when(k == nk - 1)
        def _():
            o_ref[...] = acc_ref[...].astype(o_ref.dtype)

    gi, gj = m // tm, n // tn

    def step_of_grid():
        return (pl.program_id(0) * gj + pl.program_id(1)) * nk + pl.program_id(2), gi * gj * nk

    out, extra = _host_call(
        body, 2, plan, step_of_grid, out_shape=jax.ShapeDtypeStruct((m, n), out_dtype), grid=(gi, gj, nk),
        in_specs=[a_spec, b_spec], out_specs=pl.BlockSpec((tm, tn), lambda i, j, k: (i, j)),
        scratch_shapes=[pltpu.VMEM((tm, tn), F32)] if nk > 1 else [], name=name,
        sem=("parallel", "parallel", "arbitrary"), args=[a.arr, b.arr])
    return out if plan is None else (out, extra)


def _rms(x, g):
    r = lax.rsqrt(jnp.mean(x * x, axis=-1, keepdims=True) + NORM_EPS)
    return x * r * g


def res_norm(h, m, g_a, g_b, name, want_f32=False):
    t, d = h.shape
    tr = min(ROW_TILE, t)
    has_m, has_u = m is not None, g_b is not None
    row = pl.BlockSpec((tr, d), lambda i: (i, 0))
    vec = pl.BlockSpec((1, d), lambda i: (0, 0))

    def body(*refs):
        refs = list(refs)
        h_ref = refs.pop(0)
        hn = h_ref[...]
        if has_m:
            m_ref, ga_ref = refs.pop(0), refs.pop(0)
            hn = hn + _rms(m_ref[...], ga_ref[...])
        if has_u:
            gb_ref = refs.pop(0)
        if has_m:
            refs.pop(0)[...] = hn
        if has_u:
            u = _rms(hn, gb_ref[...])
            refs.pop(0)[...] = u.astype(BF16)
            if want_f32:
                refs.pop(0)[...] = u

    args, in_specs, out_shape, out_specs = [h], [row], [], []
    if has_m:
        args += [m, g_a]
        in_specs += [row, vec]
        out_shape.append(jax.ShapeDtypeStruct((t, d), F32))
        out_specs.append(row)
    if has_u:
        args.append(g_b)
        in_specs.append(vec)
        out_shape.append(jax.ShapeDtypeStruct((t, d), BF16))
        out_specs.append(row)
        if want_f32:
            out_shape.append(jax.ShapeDtypeStruct((t, d), F32))
            out_specs.append(row)
    outs = list(pl.pallas_call(
        body, out_shape=out_shape, grid=(t // tr,), in_specs=in_specs, out_specs=out_specs,
        name=name, compiler_params=_params(("parallel",)),
    )(*args))
    h_new = outs.pop(0) if has_m else None
    u16 = outs.pop(0) if has_u else None
    u32 = outs.pop(0) if (has_u and want_f32) else None
    return h_new, u16, u32


def norm_bwd(x, g, dy, add, out_dtype, name):
    t, d = x.shape
    tr = min(ROW_TILE, t)
    has_add = add is not None
    row = pl.BlockSpec((tr, d), lambda i: (i, 0))
    vec = pl.BlockSpec((1, d), lambda i: (0, 0))

    def body(*refs):
        if has_add:
            x_ref, g_ref, dy_ref, add_ref, dx_ref, dg_ref = refs
        else:
            x_ref, g_ref, dy_ref, dx_ref, dg_ref = refs
        xv = x_ref[...]
        dyv = dy_ref[...].astype(F32)
        r = lax.rsqrt(jnp.mean(xv * xv, axis=-1, keepdims=True) + NORM_EPS)
        gy = dyv * g_ref[...]
        dx = r * gy - xv * (r * r * r * jnp.mean(gy * xv, axis=-1, keepdims=True))
        if has_add:
            dx = dx + add_ref[...]
        dx_ref[...] = dx.astype(dx_ref.dtype)

        @pl.when(pl.program_id(0) == 0)
        def _():
            dg_ref[...] = jnp.zeros_like(dg_ref)

        dg_ref[...] += jnp.sum(dyv * xv * r, axis=0, keepdims=True)

    args = [x, g, dy] + ([add] if has_add else [])
    in_specs = [row, vec, row] + ([row] if has_add else [])
    return pl.pallas_call(
        body, out_shape=[jax.ShapeDtypeStruct((t, d), out_dtype), jax.ShapeDtypeStruct((1, d), F32)],
        grid=(t // tr,), in_specs=in_specs, out_specs=[row, vec],
        name=name, compiler_params=_params(("arbitrary",)),
    )(*args)


def loss_head(y, target, name):
    t, d = y.shape
    tr = min(ROW_TILE, t)
    row = pl.BlockSpec((tr, d), lambda i: (i, 0))
    acc = pl.BlockSpec((8, LANES), lambda i: (0, 0))

    def body(y_ref, t_ref, loss_ref, dy_ref):
        e = y_ref[...] - t_ref[...]
        dy_ref[...] = e * (1.0 / d)

        @pl.when(pl.program_id(0) == 0)
        def _():
            loss_ref[...] = jnp.zeros_like(loss_ref)

        loss_ref[...] += jnp.sum(e * e) * (0.5 / d)

    return pl.pallas_call(
        body, out_shape=[jax.ShapeDtypeStruct((8, LANES), F32), jax.ShapeDtypeStruct((t, d), F32)],
        grid=(t // tr,), in_specs=[row, row], out_specs=[acc, row],
        name=name, compiler_params=_params(("arbitrary",)),
    )(y, target)


def _depth_softmax(ref, depth):
    rows = [ref[i:i + 1, :] for i in range(depth)]
    mx = functools.reduce(jnp.maximum, rows)
    ex = [jnp.exp(r - mx) for r in rows]
    tot = functools.reduce(lambda p, q: p + q, ex)
    return [e / tot for e in ex]


def lower_bounds_fwd(logits):
    depth, kw = logits.shape

    def body(l_ref, o_ref):
        s = _depth_softmax(l_ref, depth)
        run = jnp.zeros_like(s[0])
        o_ref[0:1, :] = run
        for i in range(1, depth):
            run = run + s[i]
            o_ref[i:i + 1, :] = run

    return pl.pallas_call(body, out_shape=jax.ShapeDtypeStruct((depth, kw), F32), name="lb_fwd")(logits)


def lower_bounds_bwd(logits, dlb):
    depth, kw = logits.shape

    def body(l_ref, d_ref, o_ref):
        s = _depth_softmax(l_ref, depth)
        ds = [jnp.zeros_like(s[0]) for _ in range(depth)]
        run = jnp.zeros_like(s[0])
        for j in range(depth - 1, 0, -1):
            run = run + d_ref[j:j + 1, :]
            ds[j] = run
        dot = functools.reduce(lambda p, q: p + q, [s[j] * ds[j] for j in range(depth)])
        for j in range(depth):
            o_ref[j:j + 1, :] = s[j] * (ds[j] - dot)

    return pl.pallas_call(body, out_shape=jax.ShapeDtypeStruct((depth, kw), F32), name="lb_bwd")(logits, dlb)


def _hg_gates(qp, fp, lb_row, has_lb):
    sig = _sigmoid(fp)
    nsig = _sigmoid(-fp)
    ls = jnp.minimum(fp, 0.0) - jnp.log1p(jnp.exp(-jnp.abs(fp)))
    if has_lb:
        a = jnp.log(lb_row)
        bb = jnp.log1p(-lb_row) + ls
        g = jnp.maximum(a, bb) + jnp.log1p(jnp.exp(-jnp.abs(a - bb)))
        w = jnp.exp(bb - g)
        k = (1.0 - lb_row) * nsig
    else:
        g, w, k = ls, None, nsig
    q = qp * _sigmoid(qp)
    return q, k, g, sig, nsig, w


HG_HEADS_PER_STEP = 2


def _heads_per_step(nh):
    return HG_HEADS_PER_STEP if nh % HG_HEADS_PER_STEP == 0 else 1


def _head_views(refs, hh, lanes, lead):
    cs = pl.ds(hh * HEAD, HEAD)
    out = []
    for i, r in enumerate(refs):
        if i in lead:
            out.append(r.at[hh])
        elif i in lanes:
            out.append(r.at[(slice(None),) * (len(r.shape) - 1) + (cs,)])
        else:
            out.append(r)
    return out


def _hg_masks():
    c = HG_CHUNK
    row = lax.broadcasted_iota(jnp.int32, (c, c), 0)
    col = lax.broadcasted_iota(jnp.int32, (c, c), 1)
    lower = (col <= row).astype(BF16)
    upper = (col >= row).astype(BF16)
    krow = lax.broadcasted_iota(jnp.int32, (c, HEAD), 0)
    arow = lax.broadcasted_iota(jnp.int32, (HG_SUB, c), 0)
    acol = lax.broadcasted_iota(jnp.int32, (HG_SUB, c), 1)
    return lower, upper, krow, arow, acol


def _hg_sub(i, q, k, b, b_ref, krow, arow, acol):
    r0 = i * HG_SUB
    m = b_ref[r0:r0 + 1, :]
    ebq = jnp.exp(b[r0:r0 + HG_SUB, :] - m)
    qh = (q[r0:r0 + HG_SUB, :] * ebq).astype(BF16)
    ek = jnp.exp(jnp.where(krow < r0 + HG_SUB, jnp.minimum(m - b, HG_MAX_EXPONENT), 0.0))
    kh = (k * ek).astype(BF16)
    mask = acol <= arow + r0
    amat = jnp.where(mask, _dot_nt(qh, kh), 0.0).astype(BF16)
    return ebq, qh, ek, kh, mask, amat


def hgrn_fwd(proj, lb, onorm_g, has_lb, name, plan=None):
    t, w4 = proj.shape
    kw = w4 // 4
    nh, nc, c = kw // HEAD, t // HG_CHUNK, HG_CHUNK
    hpb = _heads_per_step(nh)
    wide = hpb * HEAD

    def body(*refs):
        refs[7][...] = jnp.zeros_like(refs[7])
        heads = [one_head(*_head_views(refs, hh, lanes=(0, 1, 2, 3, 4, 5, 6), lead=(7, 8))) for hh in range(hpb)]

        def chunks(ci, carry):
            for chunk in heads:
                chunk(ci)
            return carry

        lax.fori_loop(0, nc, chunks, 0)

    def one_head(qp_ref, fp_ref, iv_ref, gp_ref, lb_ref, on_ref, y_ref, st_ref, b_ref):
        lower, _, krow, arow, acol = _hg_masks()
        lb_row, gam = lb_ref[...], on_ref[...]

        def chunk(ci):
            rs = pl.ds(pl.multiple_of(ci * c, c), c)
            v = iv_ref[rs, :].astype(BF16)
            gp = gp_ref[rs, :]
            q, k, g, _, _, _ = _hg_gates(qp_ref[rs, :], fp_ref[rs, :], lb_row, has_lb)
            b = _split_dot(g, lower, 3, left=True)
            b_ref[...] = b
            st = st_ref[...]
            o = _dot_nt((q * jnp.exp(b)).astype(BF16), st.astype(BF16))
            amats = [_hg_sub(i, q, k, b, b_ref, krow, arow, acol)[-1] for i in range(c // HG_SUB)]
            o = o + _dot(jnp.concatenate(amats, axis=0), v)
            bl = b_ref[c - 1:c, :]
            st_ref[...] = jnp.exp(bl) * st + _dot_tn(v, (k * jnp.exp(bl - b)).astype(BF16))
            r = lax.rsqrt(jnp.mean(o * o, axis=-1, keepdims=True) + NORM_EPS)
            y_ref[rs, :] = (o * r * gam * (gp * _sigmoid(gp))).astype(BF16)

        return chunk

    ns = nh // hpb
    col = lambda p: pl.BlockSpec((t, wide), lambda h: (0, p * ns + h))
    vec = pl.BlockSpec((1, wide), lambda h: (0, h))
    return _host_call(
        body, 6, plan, lambda: (pl.program_id(0), ns),
        out_shape=jax.ShapeDtypeStruct((t, kw), BF16), grid=(ns,),
        in_specs=[col(0), col(1), col(2), col(3), vec, vec],
        out_specs=pl.BlockSpec((t, wide), lambda h: (0, h)),
        scratch_shapes=[pltpu.VMEM((hpb, HEAD, HEAD), F32), pltpu.VMEM((hpb, c, HEAD), F32)],
        name=name, sem=("parallel",), args=[proj, proj, proj, proj, lb, onorm_g])


def hgrn_bwd(proj, lb, onorm_g, dy, has_lb, name, plan=None):
    t, w4 = proj.shape
    kw = w4 // 4
    nh, nc, c = kw // HEAD, t // HG_CHUNK, HG_CHUNK
    hpb = _heads_per_step(nh)
    wide = hpb * HEAD

    def body(*refs):
        for i in (8, 9, 13, 14):
            refs[i][...] = jnp.zeros_like(refs[i])
        heads = [one_head(*_head_views(refs, hh, lanes=(0, 1, 2, 3, 4, 5, 6, 7, 8, 9, 11), lead=(10, 12, 13, 14)))
                 for hh in range(hpb)]

        def fwd_chunks(ci, states):
            return tuple(fwd(ci, st) for (fwd, _), st in zip(heads, states))

        lax.fori_loop(0, nc, fwd_chunks, tuple(jnp.zeros((HEAD, HEAD), F32) for _ in heads))

        def bwd_chunks(step, carry):
            for _, bwd in heads:
                bwd(nc - 1 - step)
            return carry

        lax.fori_loop(0, nc, bwd_chunks, 0)

    def one_head(qp_ref, fp_ref, iv_ref, gp_ref, lb_ref, on_ref, dy_ref,
                 dp_ref, dgam_ref, dlb_ref, sst_ref, o_ref, b_ref, dst_ref, car_ref):
        lower, upper, krow, arow, acol = _hg_masks()
        lb_row, gam = lb_ref[...], on_ref[...]
        mm = _dot3 if has_lb else _dot1

        def recompute(ci):
            rs = pl.ds(pl.multiple_of(ci * c, c), c)
            gates = _hg_gates(qp_ref[rs, :], fp_ref[rs, :], lb_row, has_lb)
            b = _split_dot(gates[2], lower, 3, left=True)
            b_ref[...] = b
            return rs, gates, b

        def fwd_chunk(ci, carry):
            rs, (q, k, g, _, _, _), b = recompute(ci)
            v = iv_ref[rs, :].astype(BF16)
            st = carry
            sst_ref[ci] = st
            o = _dot_nt((q * jnp.exp(b)).astype(BF16), st.astype(BF16))
            amats = [_hg_sub(i, q, k, b, b_ref, krow, arow, acol)[-1] for i in range(c // HG_SUB)]
            o_ref[rs, :] = o + _dot(jnp.concatenate(amats, axis=0), v)
            bl = b_ref[c - 1:c, :]
            return jnp.exp(bl) * st + mm(iv_ref[rs, :], k * jnp.exp(bl - b), _dot_tn)

        def bwd_chunk(ci):
            rs, (q, k, g, sig, nsig, w), b = recompute(ci)
            qp, gp, v32 = qp_ref[rs, :], gp_ref[rs, :], iv_ref[rs, :]
            v = v32.astype(BF16)
            dyv, o = dy_ref[rs, :], o_ref[rs, :]
            st = sst_ref[ci]
            dst_new = dst_ref[...]
            r = lax.rsqrt(jnp.mean(o * o, axis=-1, keepdims=True) + NORM_EPS)
            on = o * r
            sgm = _sigmoid(gp)
            sg = gp * sgm
            dgam_ref[...] += jnp.sum(dyv * sg * on, axis=0, keepdims=True)
            dgp = dyv * on * gam * (sgm * (1.0 + gp * (1.0 - sgm)))
            dn = dyv * gam * sg
            do = r * dn - o * (r * r * r * jnp.mean(dn * o, axis=-1, keepdims=True))
            eb = jnp.exp(b)
            bl = b_ref[c - 1:c, :]
            ekd = jnp.exp(bl - b)
            do16 = do.astype(BF16)
            dq = mm(do, st, _dot) * eb
            dv = _dot1(k * ekd, dst_new, _dot_nt)
            dk = mm(v32, dst_new, _dot) * ekd
            dst_ref[...] = jnp.exp(bl) * dst_new + mm(do, q * eb, _dot_tn)
            da_all = mm(do, v32, _dot_nt)
            dq_parts, amats = [], []
            for i in range(c // HG_SUB):
                r0 = i * HG_SUB
                ebq, _, ek, _, mask, amat = _hg_sub(i, q, k, b, b_ref, krow, arow, acol)
                da = jnp.where(mask, da_all[r0:r0 + HG_SUB, :], 0.0)
                dq_parts.append(mm(da, k * ek, _dot) * ebq)
                dk = dk + mm(da, q[r0:r0 + HG_SUB, :] * ebq, _dot_tn) * ek
                amats.append(amat)
            dq = dq + jnp.concatenate(dq_parts, axis=0)
            dv = dv + _dot_tn(jnp.concatenate(amats, axis=0), do16)
            db = q * dq - k * dk
            dg = car_ref[...] + _split_dot(db, upper, 3, left=True)
            car_ref[...] += jnp.sum(db, axis=0, keepdims=True)
            if has_lb:
                dfp = dg * nsig * w - dk * ((1.0 - lb_row) * sig * nsig)
                dlb_ref[...] += jnp.sum(dg * nsig * jnp.exp(-g) - dk * nsig, axis=0, keepdims=True)
            else:
                dfp = dg * nsig - dk * (sig * nsig)
            sq = _sigmoid(qp)
            dp_ref[0, rs, :] = (dq * (sq * (1.0 + qp * (1.0 - sq)))).astype(BF16)
            dp_ref[1, rs, :] = dfp.astype(BF16)
            dp_ref[2, rs, :] = dv.astype(BF16)
            dp_ref[3, rs, :] = dgp.astype(BF16)

        return fwd_chunk, bwd_chunk

    ns = nh // hpb
    col = lambda p: pl.BlockSpec((t, wide), lambda h: (0, p * ns + h))
    vec = pl.BlockSpec((1, wide), lambda h: (0, h))
    return _host_call(
        body, 7, plan, lambda: (pl.program_id(0), ns),
        out_shape=[jax.ShapeDtypeStruct((4, t, kw), BF16), jax.ShapeDtypeStruct((1, kw), F32),
                   jax.ShapeDtypeStruct((1, kw), F32)],
        grid=(ns,),
        in_specs=[col(0), col(1), col(2), col(3), vec, vec, pl.BlockSpec((t, wide), lambda h: (0, h))],
        out_specs=[pl.BlockSpec((4, t, wide), lambda h: (0, 0, h)), vec, vec],
        scratch_shapes=[pltpu.VMEM((hpb, nc, HEAD, HEAD), F32), pltpu.VMEM((t, wide), F32),
                        pltpu.VMEM((hpb, c, HEAD), F32), pltpu.VMEM((hpb, HEAD, HEAD), F32),
                        pltpu.VMEM((hpb, 1, HEAD), F32)],
        name=name, sem=("parallel",), args=[proj, proj, proj, proj, lb, onorm_g, dy])


def _sb_masks():
    m, n = SB_QROWS, SB_BLOCK
    row = lax.broadcasted_iota(jnp.int32, (m, n), 0)
    col = lax.broadcasted_iota(jnp.int32, (m, n), 1)
    r2 = lax.broadcasted_iota(jnp.int32, (n, n), 0)
    c2 = lax.broadcasted_iota(jnp.int32, (n, n), 1)
    after = (r2 > c2).astype(BF16)
    from_ = (r2 >= c2).astype(BF16)
    return row, col, after, from_


def sba_fwd(qkv, name, plan=None):
    t, w3 = qkv.shape
    wd = w3 // 3
    m, n = SB_QROWS, SB_BLOCK
    nh, nq, per = wd // HEAD, t // m, m // n
    scale = HEAD ** -0.5

    def body(q_ref, k_ref, v_ref, o_ref, q16, k16, v16):
        row, col, after, _ = _sb_masks()
        q16[...] = (q_ref[...] * scale).astype(BF16)
        k16[...] = k_ref[...].astype(BF16)
        v16[...] = v_ref[...].astype(BF16)

        def qblock(qi, carry):
            qs = pl.ds(pl.multiple_of(qi * m, m), m)
            q = q16[qs, :]
            last = (qi + 1) * per - 1

            def kblock(step, state):
                acc, rem0 = state
                kj = last - step
                ks = pl.ds(pl.multiple_of(kj * n, n), n)
                z = _dot_nt(q, k16[ks, :])
                strict = (col + kj * n) < (row + qi * m)
                spz = _softplus(z)
                sp = jnp.where(strict, spz, 0.0)
                rem = rem0 + _split_dot(sp, after, 2)
                a = jnp.where(strict, jnp.exp(z - spz - rem), 0.0)
                acc = acc + _dot(a.astype(BF16), v16[ks, :])
                return acc, rem0 + jnp.sum(sp, axis=1, keepdims=True)

            acc, _ = lax.fori_loop(0, last + 1, kblock,
                                   (jnp.zeros((m, HEAD), F32), jnp.zeros((m, 1), F32)))
            o_ref[qs, :] = acc
            return carry

        lax.fori_loop(0, nq, qblock, 0)

    col_spec = lambda p: pl.BlockSpec((t, HEAD), lambda h: (0, p * nh + h))
    return _host_call(
        body, 3, plan, lambda: (pl.program_id(0), nh),
        out_shape=jax.ShapeDtypeStruct((t, wd), F32), grid=(nh,),
        in_specs=[col_spec(0), col_spec(1), col_spec(2)],
        out_specs=pl.BlockSpec((t, HEAD), lambda h: (0, h)),
        scratch_shapes=[pltpu.VMEM((t, HEAD), BF16)] * 3,
        name=name, sem=("parallel",), args=[qkv, qkv, qkv])


def sba_bwd(qkv, o, do, name, plan=None):
    t, w3 = qkv.shape
    wd = w3 // 3
    m, n = SB_QROWS, SB_BLOCK
    nh, nq, per = wd // HEAD, t // m, m // n
    scale = HEAD ** -0.5

    def body(q_ref, k_ref, v_ref, o_ref, do_ref, d_ref, dk_ref, dv_ref, q16, k16, v16):
        row, col, after, from_ = _sb_masks()
        dk_ref[...] = jnp.zeros_like(dk_ref)
        dv_ref[...] = jnp.zeros_like(dv_ref)
        q16[...] = (q_ref[...] * scale).astype(BF16)
        k16[...] = k_ref[...].astype(BF16)
        v16[...] = v_ref[...].astype(BF16)

        def qblock(qi, carry):
            qs = pl.ds(pl.multiple_of(qi * m, m), m)
            q = q16[qs, :]
            dov = do_ref[qs, :]
            do16 = dov.astype(BF16)
            dsum = jnp.sum(do16.astype(F32) * o_ref[qs, :], axis=1, keepdims=True)
            last = (qi + 1) * per - 1

            def kblock(step, state):
                dq, rem0, e0 = state
                kj = last - step
                ks = pl.ds(pl.multiple_of(kj * n, n), n)
                kv, vv = k16[ks, :], v16[ks, :]
                z = _dot_nt(q, kv)
                strict = (col + kj * n) < (row + qi * m)
                spz = _softplus(z)
                sp = jnp.where(strict, spz, 0.0)
                rem = rem0 + _split_dot(sp, after, 2)
                sgz = jnp.exp(z - spz)
                a = jnp.where(strict, sgz * jnp.exp(-rem), 0.0).astype(BF16)
                e = a.astype(F32) * _dot_nt(do16, vv)
                left = dsum - (e0 + _split_dot(e, from_, 2))
                dz = jnp.where(strict, e * (1.0 - sgz) - sgz * left, 0.0).astype(BF16)
                dq = dq + _dot(dz, kv)
                dk_ref[ks, :] += _dot_tn(dz, q)
                dv_ref[ks, :] += _dot_tn(a, do16)
                return (dq, rem0 + jnp.sum(sp, axis=1, keepdims=True),
                        e0 + jnp.sum(e, axis=1, keepdims=True))

            zero1 = jnp.zeros((m, 1), F32)
            dq, _, _ = lax.fori_loop(0, last + 1, kblock, (jnp.zeros((m, HEAD), F32), zero1, zero1))
            d_ref[0, qs, :] = (dq * scale).astype(BF16)
            return carry

        lax.fori_loop(0, nq, qblock, 0)
        d_ref[1, :, :] = dk_ref[...].astype(BF16)
        d_ref[2, :, :] = dv_ref[...].astype(BF16)

    col_spec = lambda p: pl.BlockSpec((t, HEAD), lambda h: (0, p * nh + h))
    head = pl.BlockSpec((t, HEAD), lambda h: (0, h))
    return _host_call(
        body, 5, plan, lambda: (pl.program_id(0), nh),
        out_shape=jax.ShapeDtypeStruct((3, t, wd), BF16), grid=(nh,),
        in_specs=[col_spec(0), col_spec(1), col_spec(2), head, head],
        out_specs=pl.BlockSpec((3, t, HEAD), lambda h: (0, 0, h)),
        scratch_shapes=[pltpu.VMEM((t, HEAD), F32)] * 2 + [pltpu.VMEM((t, HEAD), BF16)] * 3,
        name=name, sem=("parallel",), args=[qkv, qkv, qkv, o, do])


def _pool_band(i_out, i_in, tr, win, transpose):
    r = lax.broadcasted_iota(jnp.int32, (tr, tr), 0) + i_out * tr
    c = lax.broadcasted_iota(jnp.int32, (tr, tr), 1) + i_in * tr
    if transpose:
        return ((r <= c) & (r > c - win)).astype(F32)
    return ((c <= r) & (c > r - win)).astype(F32)


def _pool_p(u_ref, i, tr, win):
    cur = u_ref[i * tr:(i + 1) * tr, :]
    ws = _dot(_pool_band(i, i, tr, win, False), cur, HI)
    if i > 0:
        ws = ws + _dot(_pool_band(i, i - 1, tr, win, False), u_ref[(i - 1) * tr:i * tr, :], HI)
    pos = lax.broadcasted_iota(jnp.int32, (tr, 1), 0) + (i * tr + 1)
    count = jnp.minimum(pos, win).astype(F32)
    return ws / count - cur, count


def pool_fwd(u, pool_w, pool_scale, name):
    t, d = u.shape
    ng = len(POOL_WINDOWS)
    gs = d // ng
    tr = min(ROW_TILE, t)

    def body(u_ref, w_ref, s_ref, y_ref):
        win = jnp.left_shift(2, pl.program_id(0))
        for i in range(t // tr):
            p, _ = _pool_p(u_ref, i, tr, win)
            y_ref[i * tr:(i + 1) * tr, :] = _dot(p.astype(BF16), w_ref[...]) * s_ref[...]

    grp = pl.BlockSpec((t, gs), lambda g: (0, g))
    return pl.pallas_call(
        body, out_shape=jax.ShapeDtypeStruct((t, d), F32), grid=(ng,),
        in_specs=[grp, pl.BlockSpec((None, gs, gs), lambda g: (g, 0, 0)), pl.BlockSpec((1, gs), lambda g: (0, g))],
        out_specs=grp, name=name, compiler_params=_params(("parallel",)),
    )(u, pool_w, pool_scale)


def pool_bwd(u, pool_w, pool_scale, dy, name):
    t, d = u.shape
    ng = len(POOL_WINDOWS)
    gs = d // ng
    tr = min(ROW_TILE, t)
    nt = t // tr

    def body(u_ref, w_ref, s_ref, dy_ref, du_ref, dw_ref, ds_ref, dpc_ref, dp_ref):
        win = jnp.left_shift(2, pl.program_id(0))
        wv = w_ref[...]
        dw = jnp.zeros((gs, gs), F32)
        dsc = jnp.zeros((1, gs), F32)
        for i in range(nt):
            rows = slice(i * tr, (i + 1) * tr)
            p, count = _pool_p(u_ref, i, tr, win)
            p16 = p.astype(BF16)
            dyv = dy_ref[rows, :]
            dsc = dsc + jnp.sum(dyv * _dot(p16, wv), axis=0, keepdims=True)
            dyp = (dyv * s_ref[...]).astype(BF16)
            dw = dw + _dot_tn(p16, dyp)
            dp = _dot_nt(dyp, wv)
            dp_ref[rows, :] = dp
            dpc_ref[rows, :] = dp / count
        dw_ref[...] = dw.astype(BF16)
        ds_ref[...] = dsc
        for i in range(nt):
            rows = slice(i * tr, (i + 1) * tr)
            acc = _dot(_pool_band(i, i, tr, win, True), dpc_ref[rows, :], HI)
            if i + 1 < nt:
                acc = acc + _dot(_pool_band(i, i + 1, tr, win, True), dpc_ref[(i + 1) * tr:(i + 2) * tr, :], HI)
            du_ref[rows, :] = acc - dp_ref[rows, :]

    grp = pl.BlockSpec((t, gs), lambda g: (0, g))
    wspec = pl.BlockSpec((None, gs, gs), lambda g: (g, 0, 0))
    vec = pl.BlockSpec((1, gs), lambda g: (0, g))
    return pl.pallas_call(
        body,
        out_shape=[jax.ShapeDtypeStruct((t, d), F32), jax.ShapeDtypeStruct((ng, gs, gs), BF16),
                   jax.ShapeDtypeStruct((1, d), F32)],
        grid=(ng,), in_specs=[grp, wspec, vec, grp], out_specs=[grp, wspec, vec],
        scratch_shapes=[pltpu.VMEM((t, gs), F32), pltpu.VMEM((t, gs), F32)],
        name=name, compiler_params=_params(("parallel",)),
    )(u, pool_w, pool_scale, dy)


CONV_COLS = 256
HALO = 8


def _conv_taps(ref, r0, tr):
    x = ref[r0:r0 + tr, :]
    prev = ref[r0 - HALO:r0, :] if r0 > 0 else jnp.zeros((HALO, x.shape[1]), F32)
    xx = jnp.concatenate([prev, x], axis=0)
    return x, pltpu.roll(xx, 1, 0)[HALO:, :], pltpu.roll(xx, 2, 0)[HALO:, :]


def _conv_out(taps, w_ref, b_ref):
    x, s1, s2 = taps
    return w_ref[0:1, :] * s2 + w_ref[1:2, :] * s1 + w_ref[2:3, :] * x + b_ref[...]


def conv_glu_fwd(up, conv_w, conv_b, name):
    t, f2 = up.shape
    f = f2 // 2
    tc = min(CONV_COLS, f)
    nj = f // tc
    tr = min(ROW_TILE, t)

    def body(ug_ref, uv_ref, wg_ref, wv_ref, bg_ref, bv_ref, o_ref):
        for i in range(t // tr):
            r0 = i * tr
            gate = _conv_out(_conv_taps(ug_ref, r0, tr), wg_ref, bg_ref)
            val = _conv_out(_conv_taps(uv_ref, r0, tr), wv_ref, bv_ref)
            o_ref[r0:r0 + tr, :] = (gate * _sigmoid(gate) * val).astype(BF16)

    blk = lambda rows, half: pl.BlockSpec((rows, tc), lambda j: (0, half * nj + j))
    return pl.pallas_call(
        body, out_shape=jax.ShapeDtypeStruct((t, f), BF16), grid=(nj,),
        in_specs=[blk(t, 0), blk(t, 1), blk(CONV_WIDTH, 0), blk(CONV_WIDTH, 1), blk(1, 0), blk(1, 1)],
        out_specs=pl.BlockSpec((t, tc), lambda j: (0, j)),
        name=name, compiler_params=_params(("parallel",)),
    )(up, up, conv_w, conv_w, conv_b, conv_b)


def conv_glu_bwd(up, conv_w, conv_b, dact, name):
    t, f2 = up.shape
    f = f2 // 2
    tc = min(CONV_COLS, f)
    nj = f // tc
    tr = min(ROW_TILE, t)
    nt = t // tr

    def body(ug_ref, uv_ref, wg_ref, wv_ref, bg_ref, bv_ref, da_ref, du_ref, dw_ref, db_ref, dg_ref, dv_ref):
        dwg = [jnp.zeros((1, tc), F32) for _ in range(CONV_WIDTH)]
        dwv = [jnp.zeros((1, tc), F32) for _ in range(CONV_WIDTH)]
        dbg = jnp.zeros((1, tc), F32)
        dbv = jnp.zeros((1, tc), F32)
        for i in range(nt):
            r0 = i * tr
            tg, tv = _conv_taps(ug_ref, r0, tr), _conv_taps(uv_ref, r0, tr)
            gate, val = _conv_out(tg, wg_ref, bg_ref), _conv_out(tv, wv_ref, bv_ref)
            sg = _sigmoid(gate)
            da = da_ref[r0:r0 + tr, :]
            d_gate = da * val * (sg * (1.0 + gate * (1.0 - sg)))
            d_val = da * (gate * sg)
            dg_ref[r0:r0 + tr, :] = d_gate
            dv_ref[r0:r0 + tr, :] = d_val
            dbg = dbg + jnp.sum(d_gate, axis=0, keepdims=True)
            dbv = dbv + jnp.sum(d_val, axis=0, keepdims=True)
            for tap in range(CONV_WIDTH):
                dwg[tap] = dwg[tap] + jnp.sum(d_gate * tg[2 - tap], axis=0, keepdims=True)
                dwv[tap] = dwv[tap] + jnp.sum(d_val * tv[2 - tap], axis=0, keepdims=True)
        for tap in range(CONV_WIDTH):
            dw_ref[0, tap:tap + 1, :] = dwg[tap]
            dw_ref[1, tap:tap + 1, :] = dwv[tap]
        db_ref[0, :, :] = dbg
        db_ref[1, :, :] = dbv
        for half, (d_ref, w_ref) in enumerate(((dg_ref, wg_ref), (dv_ref, wv_ref))):
            for i in range(nt):
                r0 = i * tr
                x = d_ref[r0:r0 + tr, :]
                nxt = d_ref[r0 + tr:r0 + tr + HALO, :] if i + 1 < nt else jnp.zeros((HALO, tc), F32)
                xx = jnp.concatenate([x, nxt], axis=0)
                up1 = pltpu.roll(xx, tr + HALO - 1, 0)[:tr, :]
                up2 = pltpu.roll(xx, tr + HALO - 2, 0)[:tr, :]
                du = w_ref[2:3, :] * x + w_ref[1:2, :] * up1 + w_ref[0:1, :] * up2
                du_ref[half, r0:r0 + tr, :] = du.astype(BF16)

    blk = lambda rows, half: pl.BlockSpec((rows, tc), lambda j: (0, half * nj + j))
    return pl.pallas_call(
        body,
        out_shape=[jax.ShapeDtypeStruct((2, t, f), BF16), jax.ShapeDtypeStruct((2, CONV_WIDTH, f), F32),
                   jax.ShapeDtypeStruct((2, 1, f), F32)],
        grid=(nj,),
        in_specs=[blk(t, 0), blk(t, 1), blk(CONV_WIDTH, 0), blk(CONV_WIDTH, 1), blk(1, 0), blk(1, 1),
                  pl.BlockSpec((t, tc), lambda j: (0, j))],
        out_specs=[pl.BlockSpec((2, t, tc), lambda j: (0, 0, j)),
                   pl.BlockSpec((2, CONV_WIDTH, tc), lambda j: (0, 0, j)),
                   pl.BlockSpec((2, 1, tc), lambda j: (0, 0, j))],
        scratch_shapes=[pltpu.VMEM((t, tc), F32), pltpu.VMEM((t, tc), F32)],
        name=name, compiler_params=_params(("parallel",)),
    )(up, up, conv_w, conv_w, conv_b, conv_b, dact)


def _place():
    x, y, c = lax.axis_index("x"), lax.axis_index("y"), lax.axis_index("c")
    others = [(1 - x, y), (x, 1 - y), (1 - x, 1 - y)]
    return x, y, c, others


def _window(ref, axis, b, n):
    if axis == 1:
        return ref.at[:, pl.ds(b * n, n), :]
    return ref.at[:, :, pl.ds(b * n, n)]


def _remote(src, dst, send_sems, recv_sems, k, to):
    return pltpu.make_async_remote_copy(src_ref=src, dst_ref=dst, send_sem=send_sems.at[k],
                                        recv_sem=recv_sems.at[k], device_id=to, device_id_type=MESH)


class GatherPlan:
    has_mid = True

    def __init__(self, items):
        self.items = items
        self.args = [s for s, _, _, _ in items]
        self.out_shape = []
        for s, _, nl, ax in items:
            shp = [nl, s.shape[1], s.shape[2]]
            shp[ax] *= N_DEV
            self.out_shape.append(jax.ShapeDtypeStruct(tuple(shp), s.dtype))
        n = len(items)
        self.scratch = [pltpu.SemaphoreType.DMA((7 * n,)), pltpu.SemaphoreType.DMA((7 * n,)),
                        pltpu.SemaphoreType.DMA((n,))]

    def _mine(self, ins, a):
        _, l0, nl, _ = self.items[a]
        return ins[a].at[pl.ds(l0, nl)]

    def _copy(self, ins, outs, sems, a, k, block, to, own=False):
        s, _, _, ax = self.items[a]
        px, py, pc = block
        w = _window(outs[a], ax, 4 * px + 2 * py + pc, s.shape[ax])
        return _remote(self._mine(ins, a) if own else w, w, sems[0], sems[1], 7 * a + k, to)

    def _local(self, ins, outs, sems, a, x, y, c):
        s, _, _, ax = self.items[a]
        return pltpu.make_async_copy(self._mine(ins, a), _window(outs[a], ax, 4 * x + 2 * y + c, s.shape[ax]),
                                     sems[2].at[a])

    def _first(self, ins, outs, sems, a, x, y, c, others):
        me = (x, y, c)
        return [self._copy(ins, outs, sems, a, 0, me, (x, y, 1 - c), own=True)] + [
            self._copy(ins, outs, sems, a, 1 + j, me, (*chip, c), own=True) for j, chip in enumerate(others)]

    def start(self, ins, outs, sems):
        x, y, c, others = _place()
        for a in range(len(self.items)):
            self._local(ins, outs, sems, a, x, y, c).start()
        for a in range(len(self.items)):
            for cp in self._first(ins, outs, sems, a, x, y, c, others):
                cp.start()

    def mid(self, ins, outs, sems):
        x, y, c, others = _place()
        for a in range(len(self.items)):
            for j, chip in enumerate(others):
                self._copy(ins, outs, sems, a, 1 + j, (*chip, c), (x, y, c)).wait_recv()
                self._copy(ins, outs, sems, a, 4 + j, (*chip, c), (x, y, 1 - c)).start()

    def finish(self, ins, outs, sems):
        x, y, c, others = _place()
        for a in range(len(self.items)):
            self._copy(ins, outs, sems, a, 0, (x, y, 1 - c), (x, y, c)).wait_recv()
            for j, chip in enumerate(others):
                self._copy(ins, outs, sems, a, 4 + j, (*chip, 1 - c), (x, y, c)).wait_recv()
        for a in range(len(self.items)):
            for cp in self._first(ins, outs, sems, a, x, y, c, others):
                cp.wait_send()
            for j, chip in enumerate(others):
                self._copy(ins, outs, sems, a, 4 + j, (*chip, c), (x, y, 1 - c)).wait_send()
            self._local(ins, outs, sems, a, x, y, c).wait()


class ExchangePlan:
    has_mid = False

    def __init__(self, partials):
        self.args = list(partials)
        self.out_shape = [jax.ShapeDtypeStruct(p.shape, p.dtype) for p in partials]
        n = len(partials)
        self.scratch = [pltpu.SemaphoreType.DMA((3 * n,)), pltpu.SemaphoreType.DMA((3 * n,)),
                        pltpu.SemaphoreType.DMA((n,))]

    def _copies(self, ins, outs, sems):
        x, y, c, others = _place()
        me = 2 * x + y
        local, sends, recvs = [], [], []
        for a in range(len(self.args)):
            local.append(pltpu.make_async_copy(ins[a].at[me], outs[a].at[me], sems[2].at[a]))
            for j, (px, py) in enumerate(others):
                sends.append(_remote(ins[a].at[2 * px + py], outs[a].at[me], sems[0], sems[1], 3 * a + j, (px, py, c)))
                slot = outs[a].at[2 * px + py]
                recvs.append(_remote(slot, slot, sems[0], sems[1], 3 * a + j, (px, py, c)))
        return local, sends, recvs

    def start(self, ins, outs, sems):
        local, sends, _ = self._copies(ins, outs, sems)
        for cp in local + sends:
            cp.start()

    def finish(self, ins, outs, sems):
        local, sends, recvs = self._copies(ins, outs, sems)
        for cp in recvs:
            cp.wait_recv()
        for cp in sends:
            cp.wait_send()
        for cp in local:
            cp.wait()


def run_plan(plan, name):
    ni, no = len(plan.args), len(plan.out_shape)

    def body(*refs):
        ins, outs, sems = refs[:ni], refs[ni:ni + no], refs[ni + no:]
        plan.start(ins, outs, sems)
        if plan.has_mid:
            plan.mid(ins, outs, sems)
        plan.finish(ins, outs, sems)

    return pl.pallas_call(
        body, out_shape=plan.out_shape, in_specs=[ANY] * ni, out_specs=[ANY] * no,
        scratch_shapes=plan.scratch, name=name,
    )(*plan.args)


def sibling_exchange(grads, axes):
    na = len(grads)
    widths = [g.shape[ax] // N_DEV for g, ax in zip(grads, axes)]
    out_shape = []
    for g, ax, n in zip(grads, axes, widths):
        shp = list(g.shape)
        shp[ax] = n
        out_shape.append(jax.ShapeDtypeStruct((N_CHIP, *shp), g.dtype))

    def body(*refs):
        ins, outs = refs[:na], refs[na:2 * na]
        send_sems, recv_sems = refs[2 * na:]
        x, y, c, _ = _place()
        copies = []
        for a in range(na):
            for q in range(N_CHIP):
                src = _window(ins[a], axes[a], 2 * q + (1 - c), widths[a])
                cp = _remote(src, outs[a].at[q], send_sems, recv_sems, N_CHIP * a + q, (x, y, 1 - c))
                cp.start()
                copies.append(cp)
        for cp in copies:
            cp.wait_recv()
        for cp in copies:
            cp.wait_send()

    return pl.pallas_call(
        body, out_shape=out_shape, in_specs=[ANY] * na, out_specs=[ANY] * na,
        scratch_shapes=[pltpu.SemaphoreType.DMA((N_CHIP * na,)), pltpu.SemaphoreType.DMA((N_CHIP * na,))],
        name="sibling_exchange",
    )(*grads)


def _peer_of(k, x, y, c):
    return (1 - x if k & 4 else x, 1 - y if k & 2 else y, 1 - c if k & 1 else c)


def small_exchange(vec, reduce, name):
    r = vec.shape[0]

    def body(v_ref, o_ref, *scratch):
        if reduce:
            buf, send_sems, recv_sems = scratch
        else:
            buf, (send_sems, recv_sems) = o_ref, scratch
        x, y, c, _ = _place()
        me = 4 * x + 2 * y + c
        copies = []
        for k in range(1, N_DEV):
            cp = _remote(v_ref, buf.at[me], send_sems, recv_sems, k - 1, _peer_of(k, x, y, c))
            cp.start()
            copies.append(cp)
        buf[me] = v_ref[...]
        for k in range(1, N_DEV):
            px, py, pc = _peer_of(k, x, y, c)
            slot = buf.at[4 * px + 2 * py + pc]
            _remote(slot, slot, send_sems, recv_sems, k - 1, (px, py, pc)).wait_recv()
        for cp in copies:
            cp.wait_send()
        if reduce:
            tot = buf[0]
            for b in range(1, N_DEV):
                tot = tot + buf[b]
            o_ref[...] = tot

    sems = [pltpu.SemaphoreType.DMA((N_DEV - 1,)), pltpu.SemaphoreType.DMA((N_DEV - 1,))]
    if reduce:
        out_shape = jax.ShapeDtypeStruct((r, LANES), F32)
        scratch = [pltpu.VMEM((N_DEV, r, LANES), F32)] + sems
    else:
        out_shape = jax.ShapeDtypeStruct((N_DEV, r, LANES), F32)
        scratch = sems
    return pl.pallas_call(
        body, out_shape=out_shape, in_specs=[VMEM_FULL], out_specs=VMEM_FULL,
        scratch_shapes=scratch, name=name, compiler_params=_params(None),
    )(vec)


def _row_tile(rows, cols, limit=ELEMWISE_BLOCK_ELEMS):
    best = None
    for tb in range(16, rows + 1, 16):
        if rows % tb == 0 and tb * cols <= limit:
            best = tb
    return best if best is not None else rows


def add_sibling(grad, recv, axis, core):
    nl = grad.shape[0]
    _, _, r, cc = recv.shape
    tb = _row_tile(r, cc, 4 * ELEMWISE_BLOCK_ELEMS)
    per = r // tb

    def body(c_ref, g_ref, r_ref, o_ref):
        del c_ref
        o_ref[...] = (g_ref[...].astype(F32) + r_ref[...].astype(F32)).astype(BF16)

    if axis == 2:
        g_spec = pl.BlockSpec((None, tb, cc), lambda q, l, i, c_ref: (l, i, 2 * q + c_ref[0]))
    else:
        g_spec = pl.BlockSpec((None, tb, cc), lambda q, l, i, c_ref: (l, (2 * q + c_ref[0]) * per + i, 0))
    slot = pl.BlockSpec((None, None, tb, cc), lambda q, l, i, c_ref: (q, l, i, 0))
    return pl.pallas_call(
        body, out_shape=jax.ShapeDtypeStruct(recv.shape, BF16),
        grid_spec=pltpu.PrefetchScalarGridSpec(
            num_scalar_prefetch=1, grid=(N_CHIP, nl, per), in_specs=[g_spec, slot], out_specs=slot),
        name="add_sibling", compiler_params=_params(("parallel", "parallel", "parallel")),
    )(core, grad, recv)


def _adamw(w, g, m, v):
    m = ADAM_B1 * m + (1.0 - ADAM_B1) * g
    v = ADAM_B2 * v + (1.0 - ADAM_B2) * (g * g)
    m_hat = m / (1.0 - ADAM_B1 ** ADAM_STEP)
    v_hat = v / (1.0 - ADAM_B2 ** ADAM_STEP)
    delta = -ADAM_LR * (m_hat / (jnp.sqrt(v_hat) + ADAM_EPS) + ADAM_WD * w)
    return delta, m, v


def adam_from_partials(recv, w, m, v, l0, bufs):
    nl = recv.shape[1]
    _, r, cc = w.shape
    tb = _row_tile(r, cc)

    def body(p0, p1, p2, p3, w_ref, m_ref, v_ref, b0, b1, b2, b3, g_out, d_out, m_out, v_out):
        del b0, b1, b2, b3
        g = p0[...].astype(F32) + p1[...].astype(F32) + p2[...].astype(F32) + p3[...].astype(F32)
        d, mn, vn = _adamw(w_ref[...], g, m_ref[...], v_ref[...])
        g_out[...], d_out[...], m_out[...], v_out[...] = g, d, mn, vn

    slot = lambda q: pl.BlockSpec((None, None, tb, cc), lambda l, i: (q, l, i, 0))
    blk = pl.BlockSpec((None, tb, cc), lambda l, i: (l0 + l, i, 0))
    shp = jax.ShapeDtypeStruct(w.shape, F32)
    return pl.pallas_call(
        body, out_shape=[shp] * 4, grid=(nl, r // tb),
        in_specs=[slot(0), slot(1), slot(2), slot(3), blk, blk, blk] + [ANY] * 4, out_specs=[blk] * 4,
        input_output_aliases={7: 0, 8: 1, 9: 2, 10: 3},
        name="adam_big", compiler_params=_params(("parallel", "parallel")),
    )(recv, recv, recv, recv, w, m, v, *bufs)


def adam_small(w, g, m, v):
    def body(w_ref, g_ref, m_ref, v_ref, d_out, m_out, v_out):
        d_out[...], m_out[...], v_out[...] = _adamw(w_ref[...], g_ref[...], m_ref[...], v_ref[...])

    shp = jax.ShapeDtypeStruct(w.shape, F32)
    return pl.pallas_call(body, out_shape=[shp] * 3, name="adam_small")(w, g, m, v)


def _pack(arrays, multiple=8 * LANES):
    flat = jnp.concatenate([a.reshape(-1) for a in arrays])
    pad = (-flat.shape[0]) % multiple
    if pad:
        flat = jnp.concatenate([flat, jnp.zeros((pad,), flat.dtype)])
    return flat.reshape(-1, LANES)


def _unpack(packed, shapes):
    flat = packed.reshape(packed.shape[:-2] + (-1,))
    out, off = [], 0
    for shp in shapes:
        n = math.prod(shp)
        out.append(flat[..., off:off + n].reshape(packed.shape[:-2] + tuple(shp)))
        off += n
    return out


def _unshard_last(stacked):
    moved = jnp.moveaxis(stacked, 0, -2)
    return moved.reshape(moved.shape[:-2] + (-1,))


def _shard_last(full, block):
    n = full.shape[-1] // N_DEV
    return lax.dynamic_slice_in_dim(full, block * n, n, axis=full.ndim - 1)


BIG_AXIS = {"w_in": 2, "w_out": 1, "w_qkv": 2, "w_so": 1, "w_pool": 1, "w_up": 2, "w_down": 1}
BIG_ORDER = ("w_in", "w_out", "w_qkv", "w_so", "w_pool", "w_up", "w_down")

GATHER_HOSTS = {
    "hg_in_0": (("w_out", 0),),
    "hgrn_fwd_0": (("w_up", 0),),
    "ffn_up_0": (("w_down", 0),),
    "ffn_down_0": (("w_qkv", 0),),
    "sb_qkv_1": (("w_so", 0),),
    "sba_fwd_1": (("w_up", 1), ("w_down", 1), ("w_pool", 0)),
    "ffn_up_1": (("w_up", 2),),
    "ffn_down_1": (("w_down", 2),),
    "ffn_up_2": (("w_in", 1),),
    "ffn_down_2": (("w_out", 1),),
    "hgrn_fwd_3": (("w_up", 3),),
    "ffn_up_3": (("w_down", 3),),
}


class LocalWeights:
    def __init__(self, full):
        self.full, self.grads = full, {}

    def gather_plan(self, host):
        return None

    def gathered(self, outs):
        pass

    def weight(self, kind, l):
        return self.full[kind][l] if kind != "w_pool" else self.full[kind]

    def grad(self, kind, l, g):
        self.grads[(kind, l)] = g

    def exchange_plan(self):
        return None

    def exchanged(self, outs):
        pass


class MeshWeights:
    def __init__(self, shards16, w, m, v, core):
        self.shards, self.w, self.m, self.v, self.core = shards16, w, m, v, core
        self.full, self.pending, self.flying, self.keys = {}, [], [], None
        self.out = {k: [lax.empty(w[k].shape, F32) for _ in range(4)] for k in BIG_ORDER}

    def _items(self, keys):
        return [(self.shards[k], 0 if k == "w_pool" else l, 4 if k == "w_pool" else 1, BIG_AXIS[k]) for k, l in keys]

    def gather_plan(self, host):
        self.keys = [key for key in GATHER_HOSTS.get(host, ()) if key[0] in self.shards
                     and key[1] < (1 if key[0] == "w_pool" else self.shards[key[0]].shape[0])]
        return GatherPlan(self._items(self.keys)) if self.keys else None

    def gathered(self, outs):
        for (k, l), o in zip(self.keys, outs):
            self.full[(k, l)] = o if k == "w_pool" else o[0]
        self.keys = None

    def weight(self, kind, l):
        if (kind, l) not in self.full:
            (out,) = run_plan(GatherPlan(self._items([(kind, l)])), f"gather_{kind}_{l}")
            self.full[(kind, l)] = out if kind == "w_pool" else out[0]
        return self.full[(kind, l)]

    def grad(self, kind, l, g):
        self.pending.append((kind, l, g if g.ndim == 3 else g[None]))

    def exchange_plan(self):
        if not self.pending:
            return None
        self.flying, self.pending = self.pending, []
        grads = [g for _, _, g in self.flying]
        axes = [BIG_AXIS[k] for k, _, _ in self.flying]
        recv = sibling_exchange(grads, axes)
        return ExchangePlan([add_sibling(g, r, ax, self.core) for g, r, ax in zip(grads, recv, axes)])

    def exchanged(self, outs):
        for (k, l, _), r in zip(self.flying, outs):
            self.out[k] = adam_from_partials(r, self.w[k], self.m[k], self.v[k], l, self.out[k])
        self.flying = []

    def finish(self):
        plan = self.exchange_plan()
        if plan is not None:
            self.exchanged(run_plan(plan, "chip_exchange_tail"))
        return self.out


def train_step(x, target, norm_g, lb_logits, onorm_g, pool_scale, conv_w, conv_b, wts):
    t, d = x.shape
    depth = norm_g.shape[0]
    ng = lambda i, j: norm_g[i, j].reshape(1, d)
    lbs = lower_bounds_fwd(lb_logits)
    saved = []
    h = x
    _, u16, u32 = res_norm(h, None, None, ng(0, 0), "norm_in")

    def hosted(call, *args, **kw):
        plan = wts.gather_plan(kw["name"])
        out, extra = call(*args, plan=plan, **kw)
        if plan is not None:
            wts.gathered(extra)
        return out

    def project(a, kind, l, name):
        plan = wts.gather_plan(name)
        if plan is None:
            return matmul(a, wts.weight(kind, l), "nn", F32, name)
        out, extra = matmul(a, wts.weight(kind, l), "nn", F32, name, plan=plan)
        wts.gathered(extra)
        return out

    for i in range(depth):
        kind, j = i % 3, i // 3
        s = {"h_in": h, "u1": u16}
        if kind == 0:
            s["proj"] = project(u16, "w_in", j, f"hg_in_{i}")
            s["y"] = hosted(hgrn_fwd, s["proj"], lbs[i].reshape(1, -1), onorm_g[j].reshape(1, -1), i > 0,
                            name=f"hgrn_fwd_{i}")
            mix = project(s["y"], "w_out", j, f"hg_out_{i}")
        elif kind == 1:
            s["qkv"] = project(u16, "w_qkv", j, f"sb_qkv_{i}")
            s["o"] = hosted(sba_fwd, s["qkv"], name=f"sba_fwd_{i}")
            mix = project(s["o"], "w_so", j, f"sb_out_{i}")
        else:
            s["u1f"] = u32
            mix = pool_fwd(u32, wts.weight("w_pool", j), pool_scale[j].reshape(1, d), f"pool_fwd_{i}")
        s["mix"] = mix
        h_mid, u2, _ = res_norm(h, mix, ng(i, 1), ng(i, 2), f"norm_mid_{i}")
        s["h_mid"], s["u2"] = h_mid, u2
        s["up"] = project(u2, "w_up", i, f"ffn_up_{i}")
        s["act"] = conv_glu_fwd(s["up"], conv_w[i], conv_b[i].reshape(1, -1), f"glu_fwd_{i}")
        s["f"] = project(s["act"], "w_down", i, f"ffn_down_{i}")
        nxt = ng(i + 1, 0) if i + 1 < depth else None
        h, u16, u32 = res_norm(h_mid, s["f"], ng(i, 3), nxt, f"norm_out_{i}",
                               want_f32=(nxt is not None and (i + 1) % 3 == 2))
        saved.append(s)

    loss_acc, dh = loss_head(h, target, "loss_head")

    d_norm = [[None] * 4 for _ in range(depth)]
    d_lbs = jnp.zeros_like(lbs)
    d_onorm = [None] * onorm_g.shape[0]
    d_pscale = [None] * pool_scale.shape[0]
    d_cw, d_cb = [None] * depth, [None] * depth

    def exchanging(call, *args, **kw):
        plan = wts.exchange_plan()
        out, extra = call(*args, plan=plan, **kw)
        if plan is not None:
            wts.exchanged(extra)
        return out

    for i in reversed(range(depth)):
        kind, j = i % 3, i // 3
        s = saved[i]
        df, d_norm[i][3] = norm_bwd(s["f"], ng(i, 3), dh, None, BF16, f"nb_out_{i}")
        dact = matmul(df, wts.weight("w_down", i), "nt", F32, f"d_act_{i}")
        wts.grad("w_down", i, matmul(s["act"], df, "tn", BF16, f"dw_down_{i}"))
        dup, dcw, dcb = conv_glu_bwd(s["up"], conv_w[i], conv_b[i].reshape(1, -1), dact, f"glu_bwd_{i}")
        d_cw[i] = jnp.moveaxis(dcw, 0, 1).reshape(CONV_WIDTH, -1)
        d_cb[i] = dcb.reshape(-1)
        du2 = matmul(Split(dup), wts.weight("w_up", i), "nt", F32, f"d_u2_{i}")
        wts.grad("w_up", i, matmul(s["u2"], Split(dup), "tn", BF16, f"dw_up_{i}"))
        dh_mid, d_norm[i][2] = norm_bwd(s["h_mid"], ng(i, 2), du2, dh, F32, f"nb_mid_{i}")
        dm, d_norm[i][1] = norm_bwd(s["mix"], ng(i, 1), dh_mid, None, F32 if kind == 2 else BF16, f"nb_mix_{i}")
        if kind == 0:
            dy = matmul(dm, wts.weight("w_out", j), "nt", F32, f"d_y_{i}")
            wts.grad("w_out", j, matmul(s["y"], dm, "tn", BF16, f"dw_hgout_{i}"))
            dproj, d_onorm[j], dlb = exchanging(hgrn_bwd, s["proj"], lbs[i].reshape(1, -1),
                                                onorm_g[j].reshape(1, -1), dy, i > 0, name=f"hgrn_bwd_{i}")
            d_lbs = d_lbs.at[i].set(dlb[0])
            du1 = matmul(Split(dproj), wts.weight("w_in", j), "nt", F32, f"d_u1_{i}")
            wts.grad("w_in", j, matmul(s["u1"], Split(dproj), "tn", BF16, f"dw_hgin_{i}"))
        elif kind == 1:
            do = matmul(dm, wts.weight("w_so", j), "nt", F32, f"d_o_{i}")
            wts.grad("w_so", j, matmul(s["o"], dm, "tn", BF16, f"dw_sbout_{i}"))
            dqkv = exchanging(sba_bwd, s["qkv"], s["o"], do, name=f"sba_bwd_{i}")
            du1 = matmul(Split(dqkv), wts.weight("w_qkv", j), "nt", F32, f"d_u1_{i}")
            wts.grad("w_qkv", j, matmul(s["u1"], Split(dqkv), "tn", BF16, f"dw_sbqkv_{i}"))
        else:
            du1, g_pool, d_pscale[j] = pool_bwd(s["u1f"], wts.weight("w_pool", j), pool_scale[j].reshape(1, d),
                                                dm, f"pool_bwd_{i}")
            wts.grad("w_pool", j, g_pool)
        dh, d_norm[i][0] = norm_bwd(s["h_in"], ng(i, 0), du1, dh_mid, F32, f"nb_in_{i}")

    small = {
        "norm_g": jnp.stack([jnp.stack([v.reshape(d) for v in row]) for row in d_norm]),
        "lb_logits": lower_bounds_bwd(lb_logits, d_lbs),
        "onorm_g": jnp.stack([v.reshape(-1) for v in d_onorm]),
        "pool_scale": jnp.stack([v.reshape(-1) for v in d_pscale]),
        "conv_w": jnp.stack(d_cw),
        "conv_b": jnp.stack(d_cb),
    }
    return loss_acc, dh, small


SMALL_SHARDED = ("norm_g", "onorm_g", "pool_scale", "conv_w")
SMALL_ORDER = ("norm_g", "lb_logits", "onorm_g", "pool_scale", "conv_w", "conv_b")


def kernel(x, norm_g, hgrn_lb_logits, hgrn_w_in, hgrn_onorm_g, hgrn_w_out, sba_w_qkv, sba_w_out, pool_w, pool_scale, ffn_w_up, ffn_conv_w, ffn_conv_b, ffn_w_down, loss_target, m_norm_g, m_hgrn_lb_logits, m_hgrn_w_in, m_hgrn_onorm_g, m_hgrn_w_out, m_sba_w_qkv, m_sba_w_out, m_pool_w, m_pool_scale, m_ffn_w_up, m_ffn_conv_w, m_ffn_conv_b, m_ffn_w_down, v_norm_g, v_hgrn_lb_logits, v_hgrn_w_in, v_hgrn_onorm_g, v_hgrn_w_out, v_sba_w_qkv, v_sba_w_out, v_pool_w, v_pool_scale, v_ffn_w_up, v_ffn_conv_w, v_ffn_conv_b, v_ffn_w_down):
    cx, cy, cc = lax.axis_index("x"), lax.axis_index("y"), lax.axis_index("c")
    block = 4 * cx + 2 * cy + cc
    core = cc.astype(jnp.int32).reshape(1)

    pool3 = lambda a: a.reshape(a.shape[0] * a.shape[1], a.shape[2], a.shape[3])
    big_w = dict(zip(BIG_ORDER, [hgrn_w_in, hgrn_w_out, sba_w_qkv, sba_w_out, pool3(pool_w), ffn_w_up, ffn_w_down]))
    big_m = dict(zip(BIG_ORDER, [m_hgrn_w_in, m_hgrn_w_out, m_sba_w_qkv, m_sba_w_out, pool3(m_pool_w), m_ffn_w_up,
                                 m_ffn_w_down]))
    big_v = dict(zip(BIG_ORDER, [v_hgrn_w_in, v_hgrn_w_out, v_sba_w_qkv, v_sba_w_out, pool3(v_pool_w), v_ffn_w_up,
                                 v_ffn_w_down]))

    sharded = {"norm_g": norm_g, "onorm_g": hgrn_onorm_g, "pool_scale": pool_scale, "conv_w": ffn_conv_w}
    gathered = small_exchange(_pack([sharded[n] for n in SMALL_SHARDED]), False, "gather_small")
    parts = _unpack(gathered, [sharded[n].shape for n in SMALL_SHARDED])
    full = {n: _unshard_last(p) for n, p in zip(SMALL_SHARDED, parts)}

    wts = MeshWeights({k: w.astype(BF16) for k, w in big_w.items()}, big_w, big_m, big_v, core)
    loss_acc, grad_x, small_g = train_step(
        x[0], loss_target[0], full["norm_g"], hgrn_lb_logits, full["onorm_g"], full["pool_scale"],
        full["conv_w"], ffn_conv_b, wts)
    loss = lax.psum(loss_acc[0, 0], ("x", "y", "c"))

    shapes = [small_g[n].shape for n in SMALL_ORDER]
    summed = _unpack(small_exchange(_pack([small_g[n] for n in SMALL_ORDER]), True, "reduce_small"), shapes)
    sg = {n: (_shard_last(g, block) if n in SMALL_SHARDED else g) for n, g in zip(SMALL_ORDER, summed)}
    sw = {"norm_g": norm_g, "lb_logits": hgrn_lb_logits, "onorm_g": hgrn_onorm_g, "pool_scale": pool_scale,
          "conv_w": ffn_conv_w, "conv_b": ffn_conv_b}
    sm = {"norm_g": m_norm_g, "lb_logits": m_hgrn_lb_logits, "onorm_g": m_hgrn_onorm_g, "pool_scale": m_pool_scale,
          "conv_w": m_ffn_conv_w, "conv_b": m_ffn_conv_b}
    sv = {"norm_g": v_norm_g, "lb_logits": v_hgrn_lb_logits, "onorm_g": v_hgrn_onorm_g, "pool_scale": v_pool_scale,
          "conv_w": v_ffn_conv_w, "conv_b": v_ffn_conv_b}
    sshapes = [sw[n].shape for n in SMALL_ORDER]
    packed = [_pack([dct[n] for n in SMALL_ORDER]) for dct in (sw, sg, sm, sv)]
    s_delta, s_m, s_v = [dict(zip(SMALL_ORDER, _unpack(p, sshapes))) for p in adam_small(*packed)]

    upd = wts.finish()
    b_grad, b_delta, b_m, b_v = [[upd[k][n] for k in BIG_ORDER] for n in range(4)]

    def tree(small, bigs):
        bg = list(bigs)
        bg[4] = bg[4].reshape(pool_w.shape)
        return (small["norm_g"], small["lb_logits"], bg[0], small["onorm_g"], bg[1], bg[2], bg[3], bg[4],
                small["pool_scale"], bg[5], small["conv_w"], small["conv_b"], bg[6])

    return (loss, grad_x[None], *tree(sg, b_grad), *tree(s_delta, b_delta), *tree(s_m, b_m), *tree(s_v, b_v))
```

```python
import functools
import math

import jax
import jax.numpy as jnp
from jax import lax
from jax.experimental import pallas as pl
from jax.experimental.pallas import tpu as pltpu

F32 = jnp.float32
BF16 = jnp.bfloat16
HI = lax.Precision.HIGHEST
MESH = pl.DeviceIdType.MESH
ANY = pl.BlockSpec(memory_space=pl.ANY)
VMEM_FULL = pl.BlockSpec(memory_space=pltpu.VMEM)

NORM_EPS = 1e-6
HEAD = 128
HG_CHUNK = 128
HG_SUB = 32
HG_MAX_EXPONENT = 80.0
SB_BLOCK = 256
SB_QROWS = 256
POOL_WINDOWS = (2, 4, 8, 16)
CONV_WIDTH = 3
ROW_TILE = 256
N_DEV = 8
N_CHIP = 4

ADAM_LR = 0.001
ADAM_B1 = 0.9
ADAM_B2 = 0.999
ADAM_EPS = 1e-08
ADAM_WD = 0.01
ADAM_STEP = 10

VMEM_LIMIT = 48 * 1024 * 1024
LANES = 128
ELEMWISE_BLOCK_ELEMS = 256 * 1024


def _params(sem=None, vmem=VMEM_LIMIT):
    return pltpu.CompilerParams(dimension_semantics=sem, vmem_limit_bytes=vmem)


def _tile(n, prefs=(1024, 512, 256, 128)):
    for p in prefs:
        if n % p == 0:
            return p
    return n


def _dot(a, b, prec=None):
    return jnp.dot(a, b, precision=prec, preferred_element_type=F32)


def _dot_nt(a, b, prec=None):
    return lax.dot_general(a, b, (((1,), (1,)), ((), ())), precision=prec, preferred_element_type=F32)


def _dot_tn(a, b, prec=None):
    return lax.dot_general(a, b, (((0,), (0,)), ((), ())), precision=prec, preferred_element_type=F32)


def _split_dot(x, tri, parts, left=False):
    tot, rest = None, x
    for p in range(parts):
        h = rest.astype(BF16)
        d = _dot(tri, h) if left else _dot(h, tri)
        tot = d if tot is None else tot + d
        if p + 1 < parts:
            rest = rest - h.astype(F32)
    return tot


def _dot1(a, b, fn):
    return fn(a.astype(BF16), b.astype(BF16))


def _dot3(a, b, fn):
    ah, bh = a.astype(BF16), b.astype(BF16)
    al, bl = (a - ah.astype(F32)).astype(BF16), (b - bh.astype(F32)).astype(BF16)
    return fn(ah, bh) + fn(ah, bl) + fn(al, bh)


def _sigmoid(x):
    return jax.nn.sigmoid(x)


def _softplus(x):
    return jnp.maximum(x, 0.0) + jnp.log1p(jnp.exp(-jnp.abs(x)))


class Layer:
    def __init__(self, arr, l):
        self.arr, self.l = arr, l
        self.shape = arr.shape[1:]
        self.part = None


class Split:
    def __init__(self, arr):
        self.arr = arr
        self.shape = (arr.shape[1], arr.shape[0] * arr.shape[2])
        self.part = arr.shape[2]


class Plain:
    def __init__(self, arr):
        self.arr = arr
        self.shape = arr.shape
        self.part = None


def _wrap(op):
    return op if isinstance(op, (Layer, Split, Plain)) else Plain(op)


def _op_spec(op, br, bc, rc_of_grid):
    if isinstance(op, Layer):
        l = op.l
        return pl.BlockSpec((None, br, bc), lambda i, j, k: (l, *rc_of_grid(i, j, k)))
    if isinstance(op, Split):
        per = op.part // bc

        def imap(i, j, k):
            r, c = rc_of_grid(i, j, k)
            return (lax.div(c, per), r, lax.rem(c, per))
        return pl.BlockSpec((None, br, bc), imap)
    return pl.BlockSpec((br, bc), rc_of_grid)


def _hosted(body, n_in, n_out, plan, step_of_grid):
    if plan is None:
        return body
    pi, po, ps = len(plan.args), len(plan.out_shape), len(plan.scratch)

    def wrapped(*refs):
        refs = list(refs)
        ins, pins = refs[:n_in], refs[n_in:n_in + pi]
        outs = refs[n_in + pi:n_in + pi + n_out]
        pouts = refs[n_in + pi + n_out:n_in + pi + n_out + po]
        scr, pscr = refs[n_in + pi + n_out + po:len(refs) - ps], refs[len(refs) - ps:]
        step, nsteps = step_of_grid()

        @pl.when(step == 0)
        def _():
            plan.start(pins, pouts, pscr)

        if plan.has_mid:
            @pl.when(step == min((3 * nsteps) // 4, nsteps - 1))
            def _():
                plan.mid(pins, pouts, pscr)

        body(*ins, *outs, *scr)

        @pl.when(step == nsteps - 1)
        def _():
            plan.finish(pins, pouts, pscr)

    return wrapped


def _host_call(body, n_in, plan, step_of_grid, *, out_shape, grid, in_specs, out_specs, scratch_shapes, name, sem, args):
    single = not isinstance(out_shape, (list, tuple))
    out_shape = [out_shape] if single else list(out_shape)
    out_specs = [out_specs] if single else list(out_specs)
    n_out = len(out_shape)
    in_specs, scratch_shapes, args = list(in_specs), list(scratch_shapes), list(args)
    if plan is not None:
        in_specs += [ANY] * len(plan.args)
        args += plan.args
        out_shape += plan.out_shape
        out_specs += [ANY] * len(plan.out_shape)
        scratch_shapes += plan.scratch
        sem = ("arbitrary",) * len(grid)
    outs = pl.pallas_call(
        _hosted(body, n_in, n_out, plan, step_of_grid), out_shape=out_shape, grid=grid,
        in_specs=in_specs, out_specs=out_specs, scratch_shapes=scratch_shapes,
        name=name, compiler_params=_params(sem),
    )(*args)
    host = outs[0] if single else list(outs[:n_out])
    return host, list(outs[n_out:])


def matmul(a, b, mode, out_dtype, name, plan=None):
    a, b = _wrap(a), _wrap(b)
    if mode == "nn":
        (m, kd), (kd2, n) = a.shape, b.shape
    elif mode == "nt":
        (m, kd), (n, kd2) = a.shape, b.shape
    else:
        (kd, m), (kd2, n) = a.shape, b.shape
    assert kd == kd2, (mode, a.shape, b.shape)

    def dim_tile(full, ops_on_cols, prefs=(1024, 512, 256, 128)):
        base = full
        for op in ops_on_cols:
            if op.part is not None:
                base = math.gcd(base, op.part)
        return _tile(base, prefs)

    tm = dim_tile(m, [a] if mode == "tn" else [])
    tn = dim_tile(n, [b] if mode in ("nn", "tn") else [])
    tk = dim_tile(kd, ([a] if mode in ("nn", "nt") else []) + ([b] if mode == "nt" else []),
                  prefs=(2048, 2816, 1024, 512, 256, 128))
    nk = kd // tk

    if mode == "nn":
        a_spec = _op_spec(a, tm, tk, lambda i, j, k: (i, k))
        b_spec = _op_spec(b, tk, tn, lambda i, j, k: (k, j))
        dot = _dot
    elif mode == "nt":
        a_spec = _op_spec(a, tm, tk, lambda i, j, k: (i, k))
        b_spec = _op_spec(b, tn, tk, lambda i, j, k: (j, k))
        dot = _dot_nt
    else:
        a_spec = _op_spec(a, tk, tm, lambda i, j, k: (k, i))
        b_spec = _op_spec(b, tk, tn, lambda i, j, k: (k, j))
        dot = _dot_tn

    def body(a_ref, b_ref, o_ref, *acc):
        part = dot(a_ref[...].astype(BF16), b_ref[...].astype(BF16))
        if nk == 1:
            o_ref[...] = part.astype(o_ref.dtype)
            return
        (acc_ref,) = acc
        k = pl.program_id(2)

        @pl.when(k == 0)
        def _():
            acc_ref[...] = part

        @pl.when(k > 0)
        def _():
            acc_ref[...] += part

        @pl.when(k == nk - 1)
        def _():
            o_ref[...] = acc_ref[...].astype(o_ref.dtype)

    gi, gj = m // tm, n // tn

    def step_of_grid():
        return (pl.program_id(0) * gj + pl.program_id(1)) * nk + pl.program_id(2), gi * gj * nk

    out, extra = _host_call(
        body, 2, plan, step_of_grid, out_shape=jax.ShapeDtypeStruct((m, n), out_dtype), grid=(gi, gj, nk),
        in_specs=[a_spec, b_spec], out_specs=pl.BlockSpec((tm, tn), lambda i, j, k: (i, j)),
        scratch_shapes=[pltpu.VMEM((tm, tn), F32)] if nk > 1 else [], name=name,
        sem=("parallel", "parallel", "arbitrary"), args=[a.arr, b.arr])
    return out if plan is None else (out, extra)


def _rms(x, g):
    r = lax.rsqrt(jnp.mean(x * x, axis=-1, keepdims=True) + NORM_EPS)
    return x * r * g


def res_norm(h, m, g_a, g_b, name, want_f32=False):
    t, d = h.shape
    tr = min(ROW_TILE, t)
    has_m, has_u = m is not None, g_b is not None
    row = pl.BlockSpec((tr, d), lambda i: (i, 0))
    vec = pl.BlockSpec((1, d), lambda i: (0, 0))

    def body(*refs):
        refs = list(refs)
        h_ref = refs.pop(0)
        hn = h_ref[...]
        if has_m:
            m_ref, ga_ref = refs.pop(0), refs.pop(0)
            hn = hn + _rms(m_ref[...], ga_ref[...])
        if has_u:
            gb_ref = refs.pop(0)
        if has_m:
            refs.pop(0)[...] = hn
        if has_u:
            u = _rms(hn, gb_ref[...])
            refs.pop(0)[...] = u.astype(BF16)
            if want_f32:
                refs.pop(0)[...] = u

    args, in_specs, out_shape, out_specs = [h], [row], [], []
    if has_m:
        args += [m, g_a]
        in_specs += [row, vec]
        out_shape.append(jax.ShapeDtypeStruct((t, d), F32))
        out_specs.append(row)
    if has_u:
        args.append(g_b)
        in_specs.append(vec)
        out_shape.append(jax.ShapeDtypeStruct((t, d), BF16))
        out_specs.append(row)
        if want_f32:
            out_shape.append(jax.ShapeDtypeStruct((t, d), F32))
            out_specs.append(row)
    outs = list(pl.pallas_call(
        body, out_shape=out_shape, grid=(t // tr,), in_specs=in_specs, out_specs=out_specs,
        name=name, compiler_params=_params(("parallel",)),
    )(*args))
    h_new = outs.pop(0) if has_m else None
    u16 = outs.pop(0) if has_u else None
    u32 = outs.pop(0) if (has_u and want_f32) else None
    return h_new, u16, u32


def norm_bwd(x, g, dy, add, out_dtype, name):
    t, d = x.shape
    tr = min(ROW_TILE, t)
    has_add = add is not None
    row = pl.BlockSpec((tr, d), lambda i: (i, 0))
    vec = pl.BlockSpec((1, d), lambda i: (0, 0))

    def body(*refs):
        if has_add:
            x_ref, g_ref, dy_ref, add_ref, dx_ref, dg_ref = refs
        else:
            x_ref, g_ref, dy_ref, dx_ref, dg_ref = refs
        xv = x_ref[...]
        dyv = dy_ref[...].astype(F32)
        r = lax.rsqrt(jnp.mean(xv * xv, axis=-1, keepdims=True) + NORM_EPS)
        gy = dyv * g_ref[...]
        dx = r * gy - xv * (r * r * r * jnp.mean(gy * xv, axis=-1, keepdims=True))
        if has_add:
            dx = dx + add_ref[...]
        dx_ref[...] = dx.astype(dx_ref.dtype)

        @pl.when(pl.program_id(0) == 0)
        def _():
            dg_ref[...] = jnp.zeros_like(dg_ref)

        dg_ref[...] += jnp.sum(dyv * xv * r, axis=0, keepdims=True)

    args = [x, g, dy] + ([add] if has_add else [])
    in_specs = [row, vec, row] + ([row] if has_add else [])
    return pl.pallas_call(
        body, out_shape=[jax.ShapeDtypeStruct((t, d), out_dtype), jax.ShapeDtypeStruct((1, d), F32)],
        grid=(t // tr,), in_specs=in_specs, out_specs=[row, vec],
        name=name, compiler_params=_params(("arbitrary",)),
    )(*args)


def loss_head(y, target, name):
    t, d = y.shape
    tr = min(ROW_TILE, t)
    row = pl.BlockSpec((tr, d), lambda i: (i, 0))
    acc = pl.BlockSpec((8, LANES), lambda i: (0, 0))

    def body(y_ref, t_ref, loss_ref, dy_ref):
        e = y_ref[...] - t_ref[...]
        dy_ref[...] = e * (1.0 / d)

        @pl.when(pl.program_id(0) == 0)
        def _():
            loss_ref[...] = jnp.zeros_like(loss_ref)

        loss_ref[...] += jnp.sum(e * e) * (0.5 / d)

    return pl.pallas_call(
        body, out_shape=[jax.ShapeDtypeStruct((8, LANES), F32), jax.ShapeDtypeStruct((t, d), F32)],
        grid=(t // tr,), in_specs=[row, row], out_specs=[acc, row],
        name=name, compiler_params=_params(("arbitrary",)),
    )(y, target)


def _depth_softmax(ref, depth):
    rows = [ref[i:i + 1, :] for i in range(depth)]
    mx = functools.reduce(jnp.maximum, rows)
    ex = [jnp.exp(r - mx) for r in rows]
    tot = functools.reduce(lambda p, q: p + q, ex)
    return [e / tot for e in ex]


def lower_bounds_fwd(logits):
    depth, kw = logits.shape

    def body(l_ref, o_ref):
        s = _depth_softmax(l_ref, depth)
        run = jnp.zeros_like(s[0])
        o_ref[0:1, :] = run
        for i in range(1, depth):
            run = run + s[i]
            o_ref[i:i + 1, :] = run

    return pl.pallas_call(body, out_shape=jax.ShapeDtypeStruct((depth, kw), F32), name="lb_fwd")(logits)


def lower_bounds_bwd(logits, dlb):
    depth, kw = logits.shape

    def body(l_ref, d_ref, o_ref):
        s = _depth_softmax(l_ref, depth)
        ds = [jnp.zeros_like(s[0]) for _ in range(depth)]
        run = jnp.zeros_like(s[0])
        for j in range(depth - 1, 0, -1):
            run = run + d_ref[j:j + 1, :]
            ds[j] = run
        dot = functools.reduce(lambda p, q: p + q, [s[j] * ds[j] for j in range(depth)])
        for j in range(depth):
            o_ref[j:j + 1, :] = s[j] * (ds[j] - dot)

    return pl.pallas_call(body, out_shape=jax.ShapeDtypeStruct((depth, kw), F32), name="lb_bwd")(logits, dlb)


def _hg_gates(qp, fp, lb_row, has_lb):
    sig = _sigmoid(fp)
    nsig = _sigmoid(-fp)
    ls = jnp.minimum(fp, 0.0) - jnp.log1p(jnp.exp(-jnp.abs(fp)))
    if has_lb:
        a = jnp.log(lb_row)
        bb = jnp.log1p(-lb_row) + ls
        g = jnp.maximum(a, bb) + jnp.log1p(jnp.exp(-jnp.abs(a - bb)))
        w = jnp.exp(bb - g)
        k = (1.0 - lb_row) * nsig
    else:
        g, w, k = ls, None, nsig
    q = qp * _sigmoid(qp)
    return q, k, g, sig, nsig, w


HG_HEADS_PER_STEP = 2


def _heads_per_step(nh):
    return HG_HEADS_PER_STEP if nh % HG_HEADS_PER_STEP == 0 else 1


def _head_views(refs, hh, lanes, lead):
    cs = pl.ds(hh * HEAD, HEAD)
    out = []
    for i, r in enumerate(refs):
        if i in lead:
            out.append(r.at[hh])
        elif i in lanes:
            out.append(r.at[(slice(None),) * (len(r.shape) - 1) + (cs,)])
        else:
            out.append(r)
    return out


def _hg_masks():
    c = HG_CHUNK
    row = lax.broadcasted_iota(jnp.int32, (c, c), 0)
    col = lax.broadcasted_iota(jnp.int32, (c, c), 1)
    lower = (col <= row).astype(BF16)
    upper = (col >= row).astype(BF16)
    krow = lax.broadcasted_iota(jnp.int32, (c, HEAD), 0)
    arow = lax.broadcasted_iota(jnp.int32, (HG_SUB, c), 0)
    acol = lax.broadcasted_iota(jnp.int32, (HG_SUB, c), 1)
    return lower, upper, krow, arow, acol


def _hg_sub(i, q, k, b, b_ref, krow, arow, acol):
    r0 = i * HG_SUB
    m = b_ref[r0:r0 + 1, :]
    ebq = jnp.exp(b[r0:r0 + HG_SUB, :] - m)
    qh = (q[r0:r0 + HG_SUB, :] * ebq).astype(BF16)
    ek = jnp.exp(jnp.where(krow < r0 + HG_SUB, jnp.minimum(m - b, HG_MAX_EXPONENT), 0.0))
    kh = (k * ek).astype(BF16)
    mask = acol <= arow + r0
    amat = jnp.where(mask, _dot_nt(qh, kh), 0.0).astype(BF16)
    return ebq, qh, ek, kh, mask, amat


def hgrn_fwd(proj, lb, onorm_g, has_lb, name, plan=None):
    t, w4 = proj.shape
    kw = w4 // 4
    nh, nc, c = kw // HEAD, t // HG_CHUNK, HG_CHUNK
    hpb = _heads_per_step(nh)
    wide = hpb * HEAD

    def body(*refs):
        refs[7][...] = jnp.zeros_like(refs[7])
        heads = [one_head(*_head_views(refs, hh, lanes=(0, 1, 2, 3, 4, 5, 6), lead=(7, 8))) for hh in range(hpb)]

        def chunks(ci, carry):
            for chunk in heads:
                chunk(ci)
            return carry

        lax.fori_loop(0, nc, chunks, 0)

    def one_head(qp_ref, fp_ref, iv_ref, gp_ref, lb_ref, on_ref, y_ref, st_ref, b_ref):
        lower, _, krow, arow, acol = _hg_masks()
        lb_row, gam = lb_ref[...], on_ref[...]

        def chunk(ci):
            rs = pl.ds(pl.multiple_of(ci * c, c), c)
            v = iv_ref[rs, :].astype(BF16)
            gp = gp_ref[rs, :]
            q, k, g, _, _, _ = _hg_gates(qp_ref[rs, :], fp_ref[rs, :], lb_row, has_lb)
            b = _split_dot(g, lower, 3, left=True)
            b_ref[...] = b
            st = st_ref[...]
            o = _dot_nt((q * jnp.exp(b)).astype(BF16), st.astype(BF16))
            amats = [_hg_sub(i, q, k, b, b_ref, krow, arow, acol)[-1] for i in range(c // HG_SUB)]
            o = o + _dot(jnp.concatenate(amats, axis=0), v)
            bl = b_ref[c - 1:c, :]
            st_ref[...] = jnp.exp(bl) * st + _dot_tn(v, (k * jnp.exp(bl - b)).astype(BF16))
            r = lax.rsqrt(jnp.mean(o * o, axis=-1, keepdims=True) + NORM_EPS)
            y_ref[rs, :] = (o * r * gam * (gp * _sigmoid(gp))).astype(BF16)

        return chunk

    ns = nh // hpb
    col = lambda p: pl.BlockSpec((t, wide), lambda h: (0, p * ns + h))
    vec = pl.BlockSpec((1, wide), lambda h: (0, h))
    return _host_call(
        body, 6, plan, lambda: (pl.program_id(0), ns),
        out_shape=jax.ShapeDtypeStruct((t, kw), BF16), grid=(ns,),
        in_specs=[col(0), col(1), col(2), col(3), vec, vec],
        out_specs=pl.BlockSpec((t, wide), lambda h: (0, h)),
        scratch_shapes=[pltpu.VMEM((hpb, HEAD, HEAD), F32), pltpu.VMEM((hpb, c, HEAD), F32)],
        name=name, sem=("parallel",), args=[proj, proj, proj, proj, lb, onorm_g])


def hgrn_bwd(proj, lb, onorm_g, dy, has_lb, name, plan=None):
    t, w4 = proj.shape
    kw = w4 // 4
    nh, nc, c = kw // HEAD, t // HG_CHUNK, HG_CHUNK
    hpb = _heads_per_step(nh)
    wide = hpb * HEAD

    def body(*refs):
        for i in (8, 9, 13, 14):
            refs[i][...] = jnp.zeros_like(refs[i])
        heads = [one_head(*_head_views(refs, hh, lanes=(0, 1, 2, 3, 4, 5, 6, 7, 8, 9, 11), lead=(10, 12, 13, 14)))
                 for hh in range(hpb)]

        def fwd_chunks(ci, states):
            return tuple(fwd(ci, st) for (fwd, _), st in zip(heads, states))

        lax.fori_loop(0, nc, fwd_chunks, tuple(jnp.zeros((HEAD, HEAD), F32) for _ in heads))

        def bwd_chunks(step, carry):
            for _, bwd in heads:
                bwd(nc - 1 - step)
            return carry

        lax.fori_loop(0, nc, bwd_chunks, 0)

    def one_head(qp_ref, fp_ref, iv_ref, gp_ref, lb_ref, on_ref, dy_ref,
                 dp_ref, dgam_ref, dlb_ref, sst_ref, o_ref, b_ref, dst_ref, car_ref):
        lower, upper, krow, arow, acol = _hg_masks()
        lb_row, gam = lb_ref[...], on_ref[...]
        mm = _dot3 if has_lb else _dot1

        def recompute(ci):
            rs = pl.ds(pl.multiple_of(ci * c, c), c)
            gates = _hg_gates(qp_ref[rs, :], fp_ref[rs, :], lb_row, has_lb)
            b = _split_dot(gates[2], lower, 3, left=True)
            b_ref[...] = b
            return rs, gates, b

        def fwd_chunk(ci, carry):
            rs, (q, k, g, _, _, _), b = recompute(ci)
            v = iv_ref[rs, :].astype(BF16)
            st = carry
            sst_ref[ci] = st
            o = _dot_nt((q * jnp.exp(b)).astype(BF16), st.astype(BF16))
            amats = [_hg_sub(i, q, k, b, b_ref, krow, arow, acol)[-1] for i in range(c // HG_SUB)]
            o_ref[rs, :] = o + _dot(jnp.concatenate(amats, axis=0), v)
            bl = b_ref[c - 1:c, :]
            return jnp.exp(bl) * st + mm(iv_ref[rs, :], k * jnp.exp(bl - b), _dot_tn)

        def bwd_chunk(ci):
            rs, (q, k, g, sig, nsig, w), b = recompute(ci)
            qp, gp, v32 = qp_ref[rs, :], gp_ref[rs, :], iv_ref[rs, :]
            v = v32.astype(BF16)
            dyv, o = dy_ref[rs, :], o_ref[rs, :]
            st = sst_ref[ci]
            dst_new = dst_ref[...]
            r = lax.rsqrt(jnp.mean(o * o, axis=-1, keepdims=True) + NORM_EPS)
            on = o * r
            sgm = _sigmoid(gp)
            sg = gp * sgm
            dgam_ref[...] += jnp.sum(dyv * sg * on, axis=0, keepdims=True)
            dgp = dyv * on * gam * (sgm * (1.0 + gp * (1.0 - sgm)))
            dn = dyv * gam * sg
            do = r * dn - o * (r * r * r * jnp.mean(dn * o, axis=-1, keepdims=True))
            eb = jnp.exp(b)
            bl = b_ref[c - 1:c, :]
            ekd = jnp.exp(bl - b)
            do16 = do.astype(BF16)
            dq = mm(do, st, _dot) * eb
            dv = _dot1(k * ekd, dst_new, _dot_nt)
            dk = mm(v32, dst_new, _dot) * ekd
            dst_ref[...] = jnp.exp(bl) * dst_new + mm(do, q * eb, _dot_tn)
            da_all = mm(do, v32, _dot_nt)
            dq_parts, amats = [], []
            for i in range(c // HG_SUB):
                r0 = i * HG_SUB
                ebq, _, ek, _, mask, amat = _hg_sub(i, q, k, b, b_ref, krow, arow, acol)
                da = jnp.where(mask, da_all[r0:r0 + HG_SUB, :], 0.0)
                dq_parts.append(mm(da, k * ek, _dot) * ebq)
                dk = dk + mm(da, q[r0:r0 + HG_SUB, :] * ebq, _dot_tn) * ek
                amats.append(amat)
            dq = dq + jnp.concatenate(dq_parts, axis=0)
            dv = dv + _dot_tn(jnp.concatenate(amats, axis=0), do16)
            db = q * dq - k * dk
            dg = car_ref[...] + _split_dot(db, upper, 3, left=True)
            car_ref[...] += jnp.sum(db, axis=0, keepdims=True)
            if has_lb:
                dfp = dg * nsig * w - dk * ((1.0 - lb_row) * sig * nsig)
                dlb_ref[...] += jnp.sum(dg * nsig * jnp.exp(-g) - dk * nsig, axis=0, keepdims=True)
            else:
                dfp = dg * nsig - dk * (sig * nsig)
            sq = _sigmoid(qp)
            dp_ref[0, rs, :] = (dq * (sq * (1.0 + qp * (1.0 - sq)))).astype(BF16)
            dp_ref[1, rs, :] = dfp.astype(BF16)
            dp_ref[2, rs, :] = dv.astype(BF16)
            dp_ref[3, rs, :] = dgp.astype(BF16)

        return fwd_chunk, bwd_chunk

    ns = nh // hpb
    col = lambda p: pl.BlockSpec((t, wide), lambda h: (0, p * ns + h))
    vec = pl.BlockSpec((1, wide), lambda h: (0, h))
    return _host_call(
        body, 7, plan, lambda: (pl.program_id(0), ns),
        out_shape=[jax.ShapeDtypeStruct((4, t, kw), BF16), jax.ShapeDtypeStruct((1, kw), F32),
                   jax.ShapeDtypeStruct((1, kw), F32)],
        grid=(ns,),
        in_specs=[col(0), col(1), col(2), col(3), vec, vec, pl.BlockSpec((t, wide), lambda h: (0, h))],
        out_specs=[pl.BlockSpec((4, t, wide), lambda h: (0, 0, h)), vec, vec],
        scratch_shapes=[pltpu.VMEM((hpb, nc, HEAD, HEAD), F32), pltpu.VMEM((t, wide), F32),
                        pltpu.VMEM((hpb, c, HEAD), F32), pltpu.VMEM((hpb, HEAD, HEAD), F32),
                        pltpu.VMEM((hpb, 1, HEAD), F32)],
        name=name, sem=("parallel",), args=[proj, proj, proj, proj, lb, onorm_g, dy])


def _sb_masks():
    m, n = SB_QROWS, SB_BLOCK
    row = lax.broadcasted_iota(jnp.int32, (m, n), 0)
    col = lax.broadcasted_iota(jnp.int32, (m, n), 1)
    r2 = lax.broadcasted_iota(jnp.int32, (n, n), 0)
    c2 = lax.broadcasted_iota(jnp.int32, (n, n), 1)
    after = (r2 > c2).astype(BF16)
    from_ = (r2 >= c2).astype(BF16)
    return row, col, after, from_


def sba_fwd(qkv, name, plan=None):
    t, w3 = qkv.shape
    wd = w3 // 3
    m, n = SB_QROWS, SB_BLOCK
    nh, nq, per = wd // HEAD, t // m, m // n
    scale = HEAD ** -0.5

    def body(q_ref, k_ref, v_ref, o_ref, q16, k16, v16):
        row, col, after, _ = _sb_masks()
        q16[...] = (q_ref[...] * scale).astype(BF16)
        k16[...] = k_ref[...].astype(BF16)
        v16[...] = v_ref[...].astype(BF16)

        def qblock(qi, carry):
            qs = pl.ds(pl.multiple_of(qi * m, m), m)
            q = q16[qs, :]
            last = (qi + 1) * per - 1

            def kblock(step, state):
                acc, rem0 = state
                kj = last - step
                ks = pl.ds(pl.multiple_of(kj * n, n), n)
                z = _dot_nt(q, k16[ks, :])
                strict = (col + kj * n) < (row + qi * m)
                spz = _softplus(z)
                sp = jnp.where(strict, spz, 0.0)
                rem = rem0 + _split_dot(sp, after, 2)
                a = jnp.where(strict, jnp.exp(z - spz - rem), 0.0)
                acc = acc + _dot(a.astype(BF16), v16[ks, :])
                return acc, rem0 + jnp.sum(sp, axis=1, keepdims=True)

            acc, _ = lax.fori_loop(0, last + 1, kblock,
                                   (jnp.zeros((m, HEAD), F32), jnp.zeros((m, 1), F32)))
            o_ref[qs, :] = acc
            return carry

        lax.fori_loop(0, nq, qblock, 0)

    col_spec = lambda p: pl.BlockSpec((t, HEAD), lambda h: (0, p * nh + h))
    return _host_call(
        body, 3, plan, lambda: (pl.program_id(0), nh),
        out_shape=jax.ShapeDtypeStruct((t, wd), F32), grid=(nh,),
        in_specs=[col_spec(0), col_spec(1), col_spec(2)],
        out_specs=pl.BlockSpec((t, HEAD), lambda h: (0, h)),
        scratch_shapes=[pltpu.VMEM((t, HEAD), BF16)] * 3,
        name=name, sem=("parallel",), args=[qkv, qkv, qkv])


def sba_bwd(qkv, o, do, name, plan=None):
    t, w3 = qkv.shape
    wd = w3 // 3
    m, n = SB_QROWS, SB_BLOCK
    nh, nq, per = wd // HEAD, t // m, m // n
    scale = HEAD ** -0.5

    def body(q_ref, k_ref, v_ref, o_ref, do_ref, d_ref, dk_ref, dv_ref, q16, k16, v16):
        row, col, after, from_ = _sb_masks()
        dk_ref[...] = jnp.zeros_like(dk_ref)
        dv_ref[...] = jnp.zeros_like(dv_ref)
        q16[...] = (q_ref[...] * scale).astype(BF16)
        k16[...] = k_ref[...].astype(BF16)
        v16[...] = v_ref[...].astype(BF16)

        def qblock(qi, carry):
            qs = pl.ds(pl.multiple_of(qi * m, m), m)
            q = q16[qs, :]
            dov = do_ref[qs, :]
            do16 = dov.astype(BF16)
            dsum = jnp.sum(do16.astype(F32) * o_ref[qs, :], axis=1, keepdims=True)
            last = (qi + 1) * per - 1

            def kblock(step, state):
                dq, rem0, e0 = state
                kj = last - step
                ks = pl.ds(pl.multiple_of(kj * n, n), n)
                kv, vv = k16[ks, :], v16[ks, :]
                z = _dot_nt(q, kv)
                strict = (col + kj * n) < (row + qi * m)
                spz = _softplus(z)
                sp = jnp.where(strict, spz, 0.0)
                rem = rem0 + _split_dot(sp, after, 2)
                sgz = jnp.exp(z - spz)
                a = jnp.where(strict, sgz * jnp.exp(-rem), 0.0).astype(BF16)
                e = a.astype(F32) * _dot_nt(do16, vv)
                left = dsum - (e0 + _split_dot(e, from_, 2))
                dz = jnp.where(strict, e * (1.0 - sgz) - sgz * left, 0.0).astype(BF16)
                dq = dq + _dot(dz, kv)
                dk_ref[ks, :] += _dot_tn(dz, q)
                dv_ref[ks, :] += _dot_tn(a, do16)
                return (dq, rem0 + jnp.sum(sp, axis=1, keepdims=True),
                        e0 + jnp.sum(e, axis=1, keepdims=True))

            zero1 = jnp.zeros((m, 1), F32)
            dq, _, _ = lax.fori_loop(0, last + 1, kblock, (jnp.zeros((m, HEAD), F32), zero1, zero1))
            d_ref[0, qs, :] = (dq * scale).astype(BF16)
            return carry

        lax.fori_loop(0, nq, qblock, 0)
        d_ref[1, :, :] = dk_ref[...].astype(BF16)
        d_ref[2, :, :] = dv_ref[...].astype(BF16)

    col_spec = lambda p: pl.BlockSpec((t, HEAD), lambda h: (0, p * nh + h))
    head = pl.BlockSpec((t, HEAD), lambda h: (0, h))
    return _host_call(
        body, 5, plan, lambda: (pl.program_id(0), nh),
        out_shape=jax.ShapeDtypeStruct((3, t, wd), BF16), grid=(nh,),
        in_specs=[col_spec(0), col_spec(1), col_spec(2), head, head],
        out_specs=pl.BlockSpec((3, t, HEAD), lambda h: (0, 0, h)),
        scratch_shapes=[pltpu.VMEM((t, HEAD), F32)] * 2 + [pltpu.VMEM((t, HEAD), BF16)] * 3,
        name=name, sem=("parallel",), args=[qkv, qkv, qkv, o, do])


def _pool_band(i_out, i_in, tr, win, transpose):
    r = lax.broadcasted_iota(jnp.int32, (tr, tr), 0) + i_out * tr
    c = lax.broadcasted_iota(jnp.int32, (tr, tr), 1) + i_in * tr
    if transpose:
        return ((r <= c) & (r > c - win)).astype(F32)
    return ((c <= r) & (c > r - win)).astype(F32)


def _pool_p(u_ref, i, tr, win):
    cur = u_ref[i * tr:(i + 1) * tr, :]
    ws = _dot(_pool_band(i, i, tr, win, False), cur, HI)
    if i > 0:
        ws = ws + _dot(_pool_band(i, i - 1, tr, win, False), u_ref[(i - 1) * tr:i * tr, :], HI)
    pos = lax.broadcasted_iota(jnp.int32, (tr, 1), 0) + (i * tr + 1)
    count = jnp.minimum(pos, win).astype(F32)
    return ws / count - cur, count


def pool_fwd(u, pool_w, pool_scale, name):
    t, d = u.shape
    ng = len(POOL_WINDOWS)
    gs = d // ng
    tr = min(ROW_TILE, t)

    def body(u_ref, w_ref, s_ref, y_ref):
        win = jnp.left_shift(2, pl.program_id(0))
        for i in range(t // tr):
            p, _ = _pool_p(u_ref, i, tr, win)
            y_ref[i * tr:(i + 1) * tr, :] = _dot(p.astype(BF16), w_ref[...]) * s_ref[...]

    grp = pl.BlockSpec((t, gs), lambda g: (0, g))
    return pl.pallas_call(
        body, out_shape=jax.ShapeDtypeStruct((t, d), F32), grid=(ng,),
        in_specs=[grp, pl.BlockSpec((None, gs, gs), lambda g: (g, 0, 0)), pl.BlockSpec((1, gs), lambda g: (0, g))],
        out_specs=grp, name=name, compiler_params=_params(("parallel",)),
    )(u, pool_w, pool_scale)


def pool_bwd(u, pool_w, pool_scale, dy, name):
    t, d = u.shape
    ng = len(POOL_WINDOWS)
    gs = d // ng
    tr = min(ROW_TILE, t)
    nt = t // tr

    def body(u_ref, w_ref, s_ref, dy_ref, du_ref, dw_ref, ds_ref, dpc_ref, dp_ref):
        win = jnp.left_shift(2, pl.program_id(0))
        wv = w_ref[...]
        dw = jnp.zeros((gs, gs), F32)
        dsc = jnp.zeros((1, gs), F32)
        for i in range(nt):
            rows = slice(i * tr, (i + 1) * tr)
            p, count = _pool_p(u_ref, i, tr, win)
            p16 = p.astype(BF16)
            dyv = dy_ref[rows, :]
            dsc = dsc + jnp.sum(dyv * _dot(p16, wv), axis=0, keepdims=True)
            dyp = (dyv * s_ref[...]).astype(BF16)
            dw = dw + _dot_tn(p16, dyp)
            dp = _dot_nt(dyp, wv)
            dp_ref[rows, :] = dp
            dpc_ref[rows, :] = dp / count
        dw_ref[...] = dw.astype(BF16)
        ds_ref[...] = dsc
        for i in range(nt):
            rows = slice(i * tr, (i + 1) * tr)
            acc = _dot(_pool_band(i, i, tr, win, True), dpc_ref[rows, :], HI)
            if i + 1 < nt:
                acc = acc + _dot(_pool_band(i, i + 1, tr, win, True), dpc_ref[(i + 1) * tr:(i + 2) * tr, :], HI)
            du_ref[rows, :] = acc - dp_ref[rows, :]

    grp = pl.BlockSpec((t, gs), lambda g: (0, g))
    wspec = pl.BlockSpec((None, gs, gs), lambda g: (g, 0, 0))
    vec = pl.BlockSpec((1, gs), lambda g: (0, g))
    return pl.pallas_call(
        body,
        out_shape=[jax.ShapeDtypeStruct((t, d), F32), jax.ShapeDtypeStruct((ng, gs, gs), BF16),
                   jax.ShapeDtypeStruct((1, d), F32)],
        grid=(ng,), in_specs=[grp, wspec, vec, grp], out_specs=[grp, wspec, vec],
        scratch_shapes=[pltpu.VMEM((t, gs), F32), pltpu.VMEM((t, gs), F32)],
        name=name, compiler_params=_params(("parallel",)),
    )(u, pool_w, pool_scale, dy)


CONV_COLS = 256
HALO = 8


def _conv_taps(ref, r0, tr):
    x = ref[r0:r0 + tr, :]
    prev = ref[r0 - HALO:r0, :] if r0 > 0 else jnp.zeros((HALO, x.shape[1]), F32)
    xx = jnp.concatenate([prev, x], axis=0)
    return x, pltpu.roll(xx, 1, 0)[HALO:, :], pltpu.roll(xx, 2, 0)[HALO:, :]


def _conv_out(taps, w_ref, b_ref):
    x, s1, s2 = taps
    return w_ref[0:1, :] * s2 + w_ref[1:2, :] * s1 + w_ref[2:3, :] * x + b_ref[...]


def conv_glu_fwd(up, conv_w, conv_b, name):
    t, f2 = up.shape
    f = f2 // 2
    tc = min(CONV_COLS, f)
    nj = f // tc
    tr = min(ROW_TILE, t)

    def body(ug_ref, uv_ref, wg_ref, wv_ref, bg_ref, bv_ref, o_ref):
        for i in range(t // tr):
            r0 = i * tr
            gate = _conv_out(_conv_taps(ug_ref, r0, tr), wg_ref, bg_ref)
            val = _conv_out(_conv_taps(uv_ref, r0, tr), wv_ref, bv_ref)
            o_ref[r0:r0 + tr, :] = (gate * _sigmoid(gate) * val).astype(BF16)

    blk = lambda rows, half: pl.BlockSpec((rows, tc), lambda j: (0, half * nj + j))
    return pl.pallas_call(
        body, out_shape=jax.ShapeDtypeStruct((t, f), BF16), grid=(nj,),
        in_specs=[blk(t, 0), blk(t, 1), blk(CONV_WIDTH, 0), blk(CONV_WIDTH, 1), blk(1, 0), blk(1, 1)],
        out_specs=pl.BlockSpec((t, tc), lambda j: (0, j)),
        name=name, compiler_params=_params(("parallel",)),
    )(up, up, conv_w, conv_w, conv_b, conv_b)


def conv_glu_bwd(up, conv_w, conv_b, dact, name):
    t, f2 = up.shape
    f = f2 // 2
    tc = min(CONV_COLS, f)
    nj = f // tc
    tr = min(ROW_TILE, t)
    nt = t // tr

    def body(ug_ref, uv_ref, wg_ref, wv_ref, bg_ref, bv_ref, da_ref, du_ref, dw_ref, db_ref, dg_ref, dv_ref):
        dwg = [jnp.zeros((1, tc), F32) for _ in range(CONV_WIDTH)]
        dwv = [jnp.zeros((1, tc), F32) for _ in range(CONV_WIDTH)]
        dbg = jnp.zeros((1, tc), F32)
        dbv = jnp.zeros((1, tc), F32)
        for i in range(nt):
            r0 = i * tr
            tg, tv = _conv_taps(ug_ref, r0, tr), _conv_taps(uv_ref, r0, tr)
            gate, val = _conv_out(tg, wg_ref, bg_ref), _conv_out(tv, wv_ref, bv_ref)
            sg = _sigmoid(gate)
            da = da_ref[r0:r0 + tr, :]
            d_gate = da * val * (sg * (1.0 + gate * (1.0 - sg)))
            d_val = da * (gate * sg)
            dg_ref[r0:r0 + tr, :] = d_gate
            dv_ref[r0:r0 + tr, :] = d_val
            dbg = dbg + jnp.sum(d_gate, axis=0, keepdims=True)
            dbv = dbv + jnp.sum(d_val, axis=0, keepdims=True)
            for tap in range(CONV_WIDTH):
                dwg[tap] = dwg[tap] + jnp.sum(d_gate * tg[2 - tap], axis=0, keepdims=True)
                dwv[tap] = dwv[tap] + jnp.sum(d_val * tv[2 - tap], axis=0, keepdims=True)
        for tap in range(CONV_WIDTH):
            dw_ref[0, tap:tap + 1, :] = dwg[tap]
            dw_ref[1, tap:tap + 1, :] = dwv[tap]
        db_ref[0, :, :] = dbg
        db_ref[1, :, :] = dbv
        for half, (d_ref, w_ref) in enumerate(((dg_ref, wg_ref), (dv_ref, wv_ref))):
            for i in range(nt):
                r0 = i * tr
                x = d_ref[r0:r0 + tr, :]
                nxt = d_ref[r0 + tr:r0 + tr + HALO, :] if i + 1 < nt else jnp.zeros((HALO, tc), F32)
                xx = jnp.concatenate([x, nxt], axis=0)
                up1 = pltpu.roll(xx, tr + HALO - 1, 0)[:tr, :]
                up2 = pltpu.roll(xx, tr + HALO - 2, 0)[:tr, :]
                du = w_ref[2:3, :] * x + w_ref[1:2, :] * up1 + w_ref[0:1, :] * up2
                du_ref[half, r0:r0 + tr, :] = du.astype(BF16)

    blk = lambda rows, half: pl.BlockSpec((rows, tc), lambda j: (0, half * nj + j))
    return pl.pallas_call(
        body,
        out_shape=[jax.ShapeDtypeStruct((2, t, f), BF16), jax.ShapeDtypeStruct((2, CONV_WIDTH, f), F32),
                   jax.ShapeDtypeStruct((2, 1, f), F32)],
        grid=(nj,),
        in_specs=[blk(t, 0), blk(t, 1), blk(CONV_WIDTH, 0), blk(CONV_WIDTH, 1), blk(1, 0), blk(1, 1),
                  pl.BlockSpec((t, tc), lambda j: (0, j))],
        out_specs=[pl.BlockSpec((2, t, tc), lambda j: (0, 0, j)),
                   pl.BlockSpec((2, CONV_WIDTH, tc), lambda j: (0, 0, j)),
                   pl.BlockSpec((2, 1, tc), lambda j: (0, 0, j))],
        scratch_shapes=[pltpu.VMEM((t, tc), F32), pltpu.VMEM((t, tc), F32)],
        name=name, compiler_params=_params(("parallel",)),
    )(up, up, conv_w, conv_w, conv_b, conv_b, dact)


def _place():
    x, y, c = lax.axis_index("x"), lax.axis_index("y"), lax.axis_index("c")
    others = [(1 - x, y), (x, 1 - y), (1 - x, 1 - y)]
    return x, y, c, others


def _window(ref, axis, b, n):
    if axis == 1:
        return ref.at[:, pl.ds(b * n, n), :]
    return ref.at[:, :, pl.ds(b * n, n)]


def _remote(src, dst, send_sems, recv_sems, k, to):
    return pltpu.make_async_remote_copy(src_ref=src, dst_ref=dst, send_sem=send_sems.at[k],
                                        recv_sem=recv_sems.at[k], device_id=to, device_id_type=MESH)


class GatherPlan:
    has_mid = True

    def __init__(self, items):
        self.items = items
        self.args = [s for s, _, _, _ in items]
        self.out_shape = []
        for s, _, nl, ax in items:
            shp = [nl, s.shape[1], s.shape[2]]
            shp[ax] *= N_DEV
            self.out_shape.append(jax.ShapeDtypeStruct(tuple(shp), s.dtype))
        n = len(items)
        self.scratch = [pltpu.SemaphoreType.DMA((7 * n,)), pltpu.SemaphoreType.DMA((7 * n,)),
                        pltpu.SemaphoreType.DMA((n,))]

    def _mine(self, ins, a):
        _, l0, nl, _ = self.items[a]
        return ins[a].at[pl.ds(l0, nl)]

    def _copy(self, ins, outs, sems, a, k, block, to, own=False):
        s, _, _, ax = self.items[a]
        px, py, pc = block
        w = _window(outs[a], ax, 4 * px + 2 * py + pc, s.shape[ax])
        return _remote(self._mine(ins, a) if own else w, w, sems[0], sems[1], 7 * a + k, to)

    def _local(self, ins, outs, sems, a, x, y, c):
        s, _, _, ax = self.items[a]
        return pltpu.make_async_copy(self._mine(ins, a), _window(outs[a], ax, 4 * x + 2 * y + c, s.shape[ax]),
                                     sems[2].at[a])

    def _first(self, ins, outs, sems, a, x, y, c, others):
        me = (x, y, c)
        return [self._copy(ins, outs, sems, a, 0, me, (x, y, 1 - c), own=True)] + [
            self._copy(ins, outs, sems, a, 1 + j, me, (*chip, c), own=True) for j, chip in enumerate(others)]

    def start(self, ins, outs, sems):
        x, y, c, others = _place()
        for a in range(len(self.items)):
            self._local(ins, outs, sems, a, x, y, c).start()
        for a in range(len(self.items)):
            for cp in self._first(ins, outs, sems, a, x, y, c, others):
                cp.start()

    def mid(self, ins, outs, sems):
        x, y, c, others = _place()
        for a in range(len(self.items)):
            for j, chip in enumerate(others):
                self._copy(ins, outs, sems, a, 1 + j, (*chip, c), (x, y, c)).wait_recv()
                self._copy(ins, outs, sems, a, 4 + j, (*chip, c), (x, y, 1 - c)).start()

    def finish(self, ins, outs, sems):
        x, y, c, others = _place()
        for a in range(len(self.items)):
            self._copy(ins, outs, sems, a, 0, (x, y, 1 - c), (x, y, c)).wait_recv()
            for j, chip in enumerate(others):
                self._copy(ins, outs, sems, a, 4 + j, (*chip, 1 - c), (x, y, c)).wait_recv()
        for a in range(len(self.items)):
            for cp in self._first(ins, outs, sems, a, x, y, c, others):
                cp.wait_send()
            for j, chip in enumerate(others):
                self._copy(ins, outs, sems, a, 4 + j, (*chip, c), (x, y, 1 - c)).wait_send()
            self._local(ins, outs, sems, a, x, y, c).wait()


class ExchangePlan:
    has_mid = False

    def __init__(self, partials):
        self.args = list(partials)
        self.out_shape = [jax.ShapeDtypeStruct(p.shape, p.dtype) for p in partials]
        n = len(partials)
        self.scratch = [pltpu.SemaphoreType.DMA((3 * n,)), pltpu.SemaphoreType.DMA((3 * n,)),
                        pltpu.SemaphoreType.DMA((n,))]

    def _copies(self, ins, outs, sems):
        x, y, c, others = _place()
        me = 2 * x + y
        local, sends, recvs = [], [], []
        for a in range(len(self.args)):
            local.append(pltpu.make_async_copy(ins[a].at[me], outs[a].at[me], sems[2].at[a]))
            for j, (px, py) in enumerate(others):
                sends.append(_remote(ins[a].at[2 * px + py], outs[a].at[me], sems[0], sems[1], 3 * a + j, (px, py, c)))
                slot = outs[a].at[2 * px + py]
                recvs.append(_remote(slot, slot, sems[0], sems[1], 3 * a + j, (px, py, c)))
        return local, sends, recvs

    def start(self, ins, outs, sems):
        local, sends, _ = self._copies(ins, outs, sems)
        for cp in local + sends:
            cp.start()

    def finish(self, ins, outs, sems):
        local, sends, recvs = self._copies(ins, outs, sems)
        for cp in recvs:
            cp.wait_recv()
        for cp in sends:
            cp.wait_send()
        for cp in local:
            cp.wait()


def run_plan(plan, name):
    ni, no = len(plan.args), len(plan.out_shape)

    def body(*refs):
        ins, outs, sems = refs[:ni], refs[ni:ni + no], refs[ni + no:]
        plan.start(ins, outs, sems)
        if plan.has_mid:
            plan.mid(ins, outs, sems)
        plan.finish(ins, outs, sems)

    return pl.pallas_call(
        body, out_shape=plan.out_shape, in_specs=[ANY] * ni, out_specs=[ANY] * no,
        scratch_shapes=plan.scratch, name=name,
    )(*plan.args)


class SiblingPlan:
    has_mid = False

    def __init__(self, grads, axes):
        self.args, self.axes = list(grads), list(axes)
        self.widths = [g.shape[ax] // N_DEV for g, ax in zip(grads, axes)]
        self.out_shape = []
        for g, ax, n in zip(grads, axes, self.widths):
            shp = list(g.shape)
            shp[ax] = n
            self.out_shape.append(jax.ShapeDtypeStruct((N_CHIP, *shp), g.dtype))
        n = len(grads)
        self.scratch = [pltpu.SemaphoreType.DMA((N_CHIP * n,)), pltpu.SemaphoreType.DMA((N_CHIP * n,))]

    def _copies(self, ins, outs, sems):
        x, y, c, _ = _place()
        copies = []
        for a in range(len(self.args)):
            for q in range(N_CHIP):
                src = _window(ins[a], self.axes[a], 2 * q + (1 - c), self.widths[a])
                copies.append(_remote(src, outs[a].at[q], sems[0], sems[1], N_CHIP * a + q, (x, y, 1 - c)))
        return copies

    def start(self, ins, outs, sems):
        for cp in self._copies(ins, outs, sems):
            cp.start()

    def finish(self, ins, outs, sems):
        copies = self._copies(ins, outs, sems)
        for cp in copies:
            cp.wait_recv()
        for cp in copies:
            cp.wait_send()


class JointPlan:
    def __init__(self, plans):
        self.plans = plans
        self.args = [a for p in plans for a in p.args]
        self.out_shape = [o for p in plans for o in p.out_shape]
        self.scratch = [s for p in plans for s in p.scratch]
        self.has_mid = any(p.has_mid for p in plans)

    def split(self, seq, field):
        out, off = [], 0
        for p in self.plans:
            n = len(getattr(p, field))
            out.append(seq[off:off + n])
            off += n
        return out

    def _each(self, ins, outs, sems):
        return zip(self.plans, self.split(ins, "args"), self.split(outs, "out_shape"), self.split(sems, "scratch"))

    def start(self, ins, outs, sems):
        for p, i, o, s in self._each(ins, outs, sems):
            p.start(i, o, s)

    def mid(self, ins, outs, sems):
        for p, i, o, s in self._each(ins, outs, sems):
            if p.has_mid:
                p.mid(i, o, s)

    def finish(self, ins, outs, sems):
        for p, i, o, s in self._each(ins, outs, sems):
            p.finish(i, o, s)


def _peer_of(k, x, y, c):
    return (1 - x if k & 4 else x, 1 - y if k & 2 else y, 1 - c if k & 1 else c)


def small_exchange(vec, reduce, name):
    r = vec.shape[0]

    def body(v_ref, o_ref, *scratch):
        if reduce:
            buf, send_sems, recv_sems = scratch
        else:
            buf, (send_sems, recv_sems) = o_ref, scratch
        x, y, c, _ = _place()
        me = 4 * x + 2 * y + c
        copies = []
        for k in range(1, N_DEV):
            cp = _remote(v_ref, buf.at[me], send_sems, recv_sems, k - 1, _peer_of(k, x, y, c))
            cp.start()
            copies.append(cp)
        buf[me] = v_ref[...]
        for k in range(1, N_DEV):
            px, py, pc = _peer_of(k, x, y, c)
            slot = buf.at[4 * px + 2 * py + pc]
            _remote(slot, slot, send_sems, recv_sems, k - 1, (px, py, pc)).wait_recv()
        for cp in copies:
            cp.wait_send()
        if reduce:
            tot = buf[0]
            for b in range(1, N_DEV):
                tot = tot + buf[b]
            o_ref[...] = tot

    sems = [pltpu.SemaphoreType.DMA((N_DEV - 1,)), pltpu.SemaphoreType.DMA((N_DEV - 1,))]
    if reduce:
        out_shape = jax.ShapeDtypeStruct((r, LANES), F32)
        scratch = [pltpu.VMEM((N_DEV, r, LANES), F32)] + sems
    else:
        out_shape = jax.ShapeDtypeStruct((N_DEV, r, LANES), F32)
        scratch = sems
    return pl.pallas_call(
        body, out_shape=out_shape, in_specs=[VMEM_FULL], out_specs=VMEM_FULL,
        scratch_shapes=scratch, name=name, compiler_params=_params(None),
    )(vec)


def _row_tile(rows, cols, limit=ELEMWISE_BLOCK_ELEMS):
    best = None
    for tb in range(16, rows + 1, 16):
        if rows % tb == 0 and tb * cols <= limit:
            best = tb
    return best if best is not None else rows


def add_sibling(grad, recv, axis, core):
    nl = grad.shape[0]
    _, _, r, cc = recv.shape
    tb = _row_tile(r, cc, 4 * ELEMWISE_BLOCK_ELEMS)
    per = r // tb

    def body(c_ref, g_ref, r_ref, o_ref):
        del c_ref
        o_ref[...] = (g_ref[...].astype(F32) + r_ref[...].astype(F32)).astype(BF16)

    if axis == 2:
        g_spec = pl.BlockSpec((None, tb, cc), lambda q, l, i, c_ref: (l, i, 2 * q + c_ref[0]))
    else:
        g_spec = pl.BlockSpec((None, tb, cc), lambda q, l, i, c_ref: (l, (2 * q + c_ref[0]) * per + i, 0))
    slot = pl.BlockSpec((None, None, tb, cc), lambda q, l, i, c_ref: (q, l, i, 0))
    return pl.pallas_call(
        body, out_shape=jax.ShapeDtypeStruct(recv.shape, BF16),
        grid_spec=pltpu.PrefetchScalarGridSpec(
            num_scalar_prefetch=1, grid=(N_CHIP, nl, per), in_specs=[g_spec, slot], out_specs=slot),
        name="add_sibling", compiler_params=_params(("parallel", "parallel", "parallel")),
    )(core, grad, recv)


def _adamw(w, g, m, v):
    m = ADAM_B1 * m + (1.0 - ADAM_B1) * g
    v = ADAM_B2 * v + (1.0 - ADAM_B2) * (g * g)
    m_hat = m / (1.0 - ADAM_B1 ** ADAM_STEP)
    v_hat = v / (1.0 - ADAM_B2 ** ADAM_STEP)
    delta = -ADAM_LR * (m_hat / (jnp.sqrt(v_hat) + ADAM_EPS) + ADAM_WD * w)
    return delta, m, v


def adam_from_partials(recv, w, m, v, l0, bufs):
    nl = recv.shape[1]
    _, r, cc = w.shape
    tb = _row_tile(r, cc)

    def body(p0, p1, p2, p3, w_ref, m_ref, v_ref, b0, b1, b2, b3, g_out, d_out, m_out, v_out):
        del b0, b1, b2, b3
        g = p0[...].astype(F32) + p1[...].astype(F32) + p2[...].astype(F32) + p3[...].astype(F32)
        d, mn, vn = _adamw(w_ref[...], g, m_ref[...], v_ref[...])
        g_out[...], d_out[...], m_out[...], v_out[...] = g, d, mn, vn

    slot = lambda q: pl.BlockSpec((None, None, tb, cc), lambda l, i: (q, l, i, 0))
    blk = pl.BlockSpec((None, tb, cc), lambda l, i: (l0 + l, i, 0))
    shp = jax.ShapeDtypeStruct(w.shape, F32)
    return pl.pallas_call(
        body, out_shape=[shp] * 4, grid=(nl, r // tb),
        in_specs=[slot(0), slot(1), slot(2), slot(3), blk, blk, blk] + [ANY] * 4, out_specs=[blk] * 4,
        input_output_aliases={7: 0, 8: 1, 9: 2, 10: 3},
        name="adam_big", compiler_params=_params(("parallel", "parallel")),
    )(recv, recv, recv, recv, w, m, v, *bufs)


def adam_small(w, g, m, v):
    def body(w_ref, g_ref, m_ref, v_ref, d_out, m_out, v_out):
        d_out[...], m_out[...], v_out[...] = _adamw(w_ref[...], g_ref[...], m_ref[...], v_ref[...])

    shp = jax.ShapeDtypeStruct(w.shape, F32)
    return pl.pallas_call(body, out_shape=[shp] * 3, name="adam_small")(w, g, m, v)


def _pack(arrays, multiple=8 * LANES):
    flat = jnp.concatenate([a.reshape(-1) for a in arrays])
    pad = (-flat.shape[0]) % multiple
    if pad:
        flat = jnp.concatenate([flat, jnp.zeros((pad,), flat.dtype)])
    return flat.reshape(-1, LANES)


def _unpack(packed, shapes):
    flat = packed.reshape(packed.shape[:-2] + (-1,))
    out, off = [], 0
    for shp in shapes:
        n = math.prod(shp)
        out.append(flat[..., off:off + n].reshape(packed.shape[:-2] + tuple(shp)))
        off += n
    return out


def _unshard_last(stacked):
    moved = jnp.moveaxis(stacked, 0, -2)
    return moved.reshape(moved.shape[:-2] + (-1,))


def _shard_last(full, block):
    n = full.shape[-1] // N_DEV
    return lax.dynamic_slice_in_dim(full, block * n, n, axis=full.ndim - 1)


BIG_AXIS = {"w_in": 2, "w_out": 1, "w_qkv": 2, "w_so": 1, "w_pool": 1, "w_up": 2, "w_down": 1}
BIG_ORDER = ("w_in", "w_out", "w_qkv", "w_so", "w_pool", "w_up", "w_down")

GATHER_HOSTS = {
    "hg_in_0": (("w_out", 0),),
    "hgrn_fwd_0": (("w_up", 0),),
    "ffn_up_0": (("w_down", 0),),
    "ffn_down_0": (("w_qkv", 0),),
    "sb_qkv_1": (("w_so", 0),),
    "sba_fwd_1": (("w_up", 1), ("w_down", 1), ("w_pool", 0)),
    "ffn_up_1": (("w_up", 2),),
    "ffn_down_1": (("w_down", 2),),
    "ffn_up_2": (("w_in", 1),),
    "ffn_down_2": (("w_out", 1),),
    "hgrn_fwd_3": (("w_up", 3),),
    "ffn_up_3": (("w_down", 3),),
}


class LocalWeights:
    def __init__(self, full):
        self.full, self.grads = full, {}

    def gather_plan(self, host):
        return None

    def gathered(self, outs):
        pass

    def weight(self, kind, l):
        return self.full[kind][l] if kind != "w_pool" else self.full[kind]

    def grad(self, kind, l, g):
        self.grads[(kind, l)] = g

    def exchange_plan(self):
        return None

    def exchanged(self, outs):
        pass


class MeshWeights:
    def __init__(self, shards16, w, m, v, core):
        self.shards, self.w, self.m, self.v, self.core = shards16, w, m, v, core
        self.full, self.pending, self.partials, self.flying, self.keys = {}, [], [], ([], []), None
        self.out = {k: [lax.empty(w[k].shape, F32) for _ in range(4)] for k in BIG_ORDER}

    def _items(self, keys):
        return [(self.shards[k], 0 if k == "w_pool" else l, 4 if k == "w_pool" else 1, BIG_AXIS[k]) for k, l in keys]

    def gather_plan(self, host):
        self.keys = [key for key in GATHER_HOSTS.get(host, ()) if key[0] in self.shards
                     and key[1] < (1 if key[0] == "w_pool" else self.shards[key[0]].shape[0])]
        return GatherPlan(self._items(self.keys)) if self.keys else None

    def gathered(self, outs):
        for (k, l), o in zip(self.keys, outs):
            self.full[(k, l)] = o if k == "w_pool" else o[0]
        self.keys = None

    def weight(self, kind, l):
        if (kind, l) not in self.full:
            (out,) = run_plan(GatherPlan(self._items([(kind, l)])), f"gather_{kind}_{l}")
            self.full[(kind, l)] = out if kind == "w_pool" else out[0]
        return self.full[(kind, l)]

    def grad(self, kind, l, g):
        self.pending.append((kind, l, g if g.ndim == 3 else g[None]))

    def exchange_plan(self):
        self.flying = (self.partials, self.pending)
        self.partials, self.pending = [], []
        plans = []
        if self.flying[0]:
            plans.append(ExchangePlan([p for _, _, p in self.flying[0]]))
        if self.flying[1]:
            plans.append(SiblingPlan([g for _, _, g in self.flying[1]], [BIG_AXIS[k] for k, _, _ in self.flying[1]]))
        return JointPlan(plans) if plans else None

    def exchanged(self, outs):
        chips, sibs = self.flying
        for (k, l, _), r in zip(chips, outs[:len(chips)]):
            self.out[k] = adam_from_partials(r, self.w[k], self.m[k], self.v[k], l, self.out[k])
        for (k, l, g), r in zip(sibs, outs[len(chips):]):
            self.partials.append((k, l, add_sibling(g, r, BIG_AXIS[k], self.core)))

    def finish(self):
        for tail in range(2):
            plan = self.exchange_plan()
            if plan is not None:
                self.exchanged(run_plan(plan, f"exchange_tail_{tail}"))
        return self.out


def train_step(x, target, norm_g, lb_logits, onorm_g, pool_scale, conv_w, conv_b, wts):
    t, d = x.shape
    depth = norm_g.shape[0]
    ng = lambda i, j: norm_g[i, j].reshape(1, d)
    lbs = lower_bounds_fwd(lb_logits)
    saved = []
    h = x
    _, u16, u32 = res_norm(h, None, None, ng(0, 0), "norm_in")

    def hosted(call, *args, **kw):
        plan = wts.gather_plan(kw["name"])
        out, extra = call(*args, plan=plan, **kw)
        if plan is not None:
            wts.gathered(extra)
        return out

    def project(a, kind, l, name):
        plan = wts.gather_plan(name)
        if plan is None:
            return matmul(a, wts.weight(kind, l), "nn", F32, name)
        out, extra = matmul(a, wts.weight(kind, l), "nn", F32, name, plan=plan)
        wts.gathered(extra)
        return out

    for i in range(depth):
        kind, j = i % 3, i // 3
        s = {"h_in": h, "u1": u16}
        if kind == 0:
            s["proj"] = project(u16, "w_in", j, f"hg_in_{i}")
            s["y"] = hosted(hgrn_fwd, s["proj"], lbs[i].reshape(1, -1), onorm_g[j].reshape(1, -1), i > 0,
                            name=f"hgrn_fwd_{i}")
            mix = project(s["y"], "w_out", j, f"hg_out_{i}")
        elif kind == 1:
            s["qkv"] = project(u16, "w_qkv", j, f"sb_qkv_{i}")
            s["o"] = hosted(sba_fwd, s["qkv"], name=f"sba_fwd_{i}")
            mix = project(s["o"], "w_so", j, f"sb_out_{i}")
        else:
            s["u1f"] = u32
            mix = pool_fwd(u32, wts.weight("w_pool", j), pool_scale[j].reshape(1, d), f"pool_fwd_{i}")
        s["mix"] = mix
        h_mid, u2, _ = res_norm(h, mix, ng(i, 1), ng(i, 2), f"norm_mid_{i}")
        s["h_mid"], s["u2"] = h_mid, u2
        s["up"] = project(u2, "w_up", i, f"ffn_up_{i}")
        s["act"] = conv_glu_fwd(s["up"], conv_w[i], conv_b[i].reshape(1, -1), f"glu_fwd_{i}")
        s["f"] = project(s["act"], "w_down", i, f"ffn_down_{i}")
        nxt = ng(i + 1, 0) if i + 1 < depth else None
        h, u16, u32 = res_norm(h_mid, s["f"], ng(i, 3), nxt, f"norm_out_{i}",
                               want_f32=(nxt is not None and (i + 1) % 3 == 2))
        saved.append(s)

    loss_acc, dh = loss_head(h, target, "loss_head")

    d_norm = [[None] * 4 for _ in range(depth)]
    d_lbs = jnp.zeros_like(lbs)
    d_onorm = [None] * onorm_g.shape[0]
    d_pscale = [None] * pool_scale.shape[0]
    d_cw, d_cb = [None] * depth, [None] * depth

    def exchanging(call, *args, **kw):
        plan = wts.exchange_plan()
        out, extra = call(*args, plan=plan, **kw)
        if plan is not None:
            wts.exchanged(extra)
        return out

    def bmm(a, b, mode, dtype, name):
        plan = wts.exchange_plan()
        if plan is None:
            return matmul(a, b, mode, dtype, name)
        out, extra = matmul(a, b, mode, dtype, name, plan=plan)
        wts.exchanged(extra)
        return out

    for i in reversed(range(depth)):
        kind, j = i % 3, i // 3
        s = saved[i]
        df, d_norm[i][3] = norm_bwd(s["f"], ng(i, 3), dh, None, BF16, f"nb_out_{i}")
        dact = bmm(df, wts.weight("w_down", i), "nt", F32, f"d_act_{i}")
        wts.grad("w_down", i, bmm(s["act"], df, "tn", BF16, f"dw_down_{i}"))
        dup, dcw, dcb = conv_glu_bwd(s["up"], conv_w[i], conv_b[i].reshape(1, -1), dact, f"glu_bwd_{i}")
        d_cw[i] = jnp.moveaxis(dcw, 0, 1).reshape(CONV_WIDTH, -1)
        d_cb[i] = dcb.reshape(-1)
        du2 = bmm(Split(dup), wts.weight("w_up", i), "nt", F32, f"d_u2_{i}")
        wts.grad("w_up", i, bmm(s["u2"], Split(dup), "tn", BF16, f"dw_up_{i}"))
        dh_mid, d_norm[i][2] = norm_bwd(s["h_mid"], ng(i, 2), du2, dh, F32, f"nb_mid_{i}")
        dm, d_norm[i][1] = norm_bwd(s["mix"], ng(i, 1), dh_mid, None, F32 if kind == 2 else BF16, f"nb_mix_{i}")
        if kind == 0:
            dy = matmul(dm, wts.weight("w_out", j), "nt", F32, f"d_y_{i}")
            wts.grad("w_out", j, matmul(s["y"], dm, "tn", BF16, f"dw_hgout_{i}"))
            dproj, d_onorm[j], dlb = exchanging(hgrn_bwd, s["proj"], lbs[i].reshape(1, -1),
                                                onorm_g[j].reshape(1, -1), dy, i > 0, name=f"hgrn_bwd_{i}")
            d_lbs = d_lbs.at[i].set(dlb[0])
            du1 = bmm(Split(dproj), wts.weight("w_in", j), "nt", F32, f"d_u1_{i}")
            wts.grad("w_in", j, bmm(s["u1"], Split(dproj), "tn", BF16, f"dw_hgin_{i}"))
        elif kind == 1:
            do = matmul(dm, wts.weight("w_so", j), "nt", F32, f"d_o_{i}")
            wts.grad("w_so", j, matmul(s["o"], dm, "tn", BF16, f"dw_sbout_{i}"))
            dqkv = exchanging(sba_bwd, s["qkv"], s["o"], do, name=f"sba_bwd_{i}")
            du1 = bmm(Split(dqkv), wts.weight("w_qkv", j), "nt", F32, f"d_u1_{i}")
            wts.grad("w_qkv", j, bmm(s["u1"], Split(dqkv), "tn", BF16, f"dw_sbqkv_{i}"))
        else:
            du1, g_pool, d_pscale[j] = pool_bwd(s["u1f"], wts.weight("w_pool", j), pool_scale[j].reshape(1, d),
                                                dm, f"pool_bwd_{i}")
            wts.grad("w_pool", j, g_pool)
        dh, d_norm[i][0] = norm_bwd(s["h_in"], ng(i, 0), du1, dh_mid, F32, f"nb_in_{i}")

    small = {
        "norm_g": jnp.stack([jnp.stack([v.reshape(d) for v in row]) for row in d_norm]),
        "lb_logits": lower_bounds_bwd(lb_logits, d_lbs),
        "onorm_g": jnp.stack([v.reshape(-1) for v in d_onorm]),
        "pool_scale": jnp.stack([v.reshape(-1) for v in d_pscale]),
        "conv_w": jnp.stack(d_cw),
        "conv_b": jnp.stack(d_cb),
    }
    return loss_acc, dh, small


SMALL_SHARDED = ("norm_g", "onorm_g", "pool_scale", "conv_w")
SMALL_ORDER = ("norm_g", "lb_logits", "onorm_g", "pool_scale", "conv_w", "conv_b")


def kernel(x, norm_g, hgrn_lb_logits, hgrn_w_in, hgrn_onorm_g, hgrn_w_out, sba_w_qkv, sba_w_out, pool_w, pool_scale, ffn_w_up, ffn_conv_w, ffn_conv_b, ffn_w_down, loss_target, m_norm_g, m_hgrn_lb_logits, m_hgrn_w_in, m_hgrn_onorm_g, m_hgrn_w_out, m_sba_w_qkv, m_sba_w_out, m_pool_w, m_pool_scale, m_ffn_w_up, m_ffn_conv_w, m_ffn_conv_b, m_ffn_w_down, v_norm_g, v_hgrn_lb_logits, v_hgrn_w_in, v_hgrn_onorm_g, v_hgrn_w_out, v_sba_w_qkv, v_sba_w_out, v_pool_w, v_pool_scale, v_ffn_w_up, v_ffn_conv_w, v_ffn_conv_b, v_ffn_w_down):
    cx, cy, cc = lax.axis_index("x"), lax.axis_index("y"), lax.axis_index("c")
    block = 4 * cx + 2 * cy + cc
    core = cc.astype(jnp.int32).reshape(1)

    pool3 = lambda a: a.reshape(a.shape[0] * a.shape[1], a.shape[2], a.shape[3])
    big_w = dict(zip(BIG_ORDER, [hgrn_w_in, hgrn_w_out, sba_w_qkv, sba_w_out, pool3(pool_w), ffn_w_up, ffn_w_down]))
    big_m = dict(zip(BIG_ORDER, [m_hgrn_w_in, m_hgrn_w_out, m_sba_w_qkv, m_sba_w_out, pool3(m_pool_w), m_ffn_w_up,
                                 m_ffn_w_down]))
    big_v = dict(zip(BIG_ORDER, [v_hgrn_w_in, v_hgrn_w_out, v_sba_w_qkv, v_sba_w_out, pool3(v_pool_w), v_ffn_w_up,
                                 v_ffn_w_down]))

    sharded = {"norm_g": norm_g, "onorm_g": hgrn_onorm_g, "pool_scale": pool_scale, "conv_w": ffn_conv_w}
    gathered = small_exchange(_pack([sharded[n] for n in SMALL_SHARDED]), False, "gather_small")
    parts = _unpack(gathered, [sharded[n].shape for n in SMALL_SHARDED])
    full = {n: _unshard_last(p) for n, p in zip(SMALL_SHARDED, parts)}

    wts = MeshWeights({k: w.astype(BF16) for k, w in big_w.items()}, big_w, big_m, big_v, core)
    loss_acc, grad_x, small_g = train_step(
        x[0], loss_target[0], full["norm_g"], hgrn_lb_logits, full["onorm_g"], full["pool_scale"],
        full["conv_w"], ffn_conv_b, wts)
    loss = lax.psum(loss_acc[0, 0], ("x", "y", "c"))

    shapes = [small_g[n].shape for n in SMALL_ORDER]
    summed = _unpack(small_exchange(_pack([small_g[n] for n in SMALL_ORDER]), True, "reduce_small"), shapes)
    sg = {n: (_shard_last(g, block) if n in SMALL_SHARDED else g) for n, g in zip(SMALL_ORDER, summed)}
    sw = {"norm_g": norm_g, "lb_logits": hgrn_lb_logits, "onorm_g": hgrn_onorm_g, "pool_scale": pool_scale,
          "conv_w": ffn_conv_w, "conv_b": ffn_conv_b}
    sm = {"norm_g": m_norm_g, "lb_logits": m_hgrn_lb_logits, "onorm_g": m_hgrn_onorm_g, "pool_scale": m_pool_scale,
          "conv_w": m_ffn_conv_w, "conv_b": m_ffn_conv_b}
    sv = {"norm_g": v_norm_g, "lb_logits": v_hgrn_lb_logits, "onorm_g": v_hgrn_onorm_g, "pool_scale": v_pool_scale,
          "conv_w": v_ffn_conv_w, "conv_b": v_ffn_conv_b}
    sshapes = [sw[n].shape for n in SMALL_ORDER]
    packed = [_pack([dct[n] for n in SMALL_ORDER]) for dct in (sw, sg, sm, sv)]
    s_delta, s_m, s_v = [dict(zip(SMALL_ORDER, _unpack(p, sshapes))) for p in adam_small(*packed)]

    upd = wts.finish()
    b_grad, b_delta, b_m, b_v = [[upd[k][n] for k in BIG_ORDER] for n in range(4)]

    def tree(small, bigs):
        bg = list(bigs)
        bg[4] = bg[4].reshape(pool_w.shape)
        return (small["norm_g"], small["lb_logits"], bg[0], small["onorm_g"], bg[1], bg[2], bg[3], bg[4],
                small["pool_scale"], bg[5], small["conv_w"], small["conv_b"], bg[6])

    return (loss, grad_x[None], *tree(sg, b_grad), *tree(s_delta, b_delta), *tree(s_m, b_m), *tree(s_v, b_v))
```

```python
import functools
import math

import jax
import jax.numpy as jnp
from jax import lax
from jax.experimental import pallas as pl
from jax.experimental.pallas import tpu as pltpu

F32 = jnp.float32
BF16 = jnp.bfloat16
HI = lax.Precision.HIGHEST
MESH = pl.DeviceIdType.MESH
ANY = pl.BlockSpec(memory_space=pl.ANY)
VMEM_FULL = pl.BlockSpec(memory_space=pltpu.VMEM)

NORM_EPS = 1e-6
HEAD = 128
HG_CHUNK = 128
HG_SUB = 32
HG_MAX_EXPONENT = 80.0
SB_BLOCK = 256
SB_QROWS = 256
POOL_WINDOWS = (2, 4, 8, 16)
CONV_WIDTH = 3
ROW_TILE = 256
N_DEV = 8
N_CHIP = 4

ADAM_LR = 0.001
ADAM_B1 = 0.9
ADAM_B2 = 0.999
ADAM_EPS = 1e-08
ADAM_WD = 0.01
ADAM_STEP = 10

VMEM_LIMIT = 48 * 1024 * 1024
LANES = 128
ELEMWISE_BLOCK_ELEMS = 256 * 1024


def _params(sem=None, vmem=VMEM_LIMIT):
    return pltpu.CompilerParams(dimension_semantics=sem, vmem_limit_bytes=vmem)


def _tile(n, prefs=(1024, 512, 256, 128)):
    for p in prefs:
        if n % p == 0:
            return p
    return n


def _dot(a, b, prec=None):
    return jnp.dot(a, b, precision=prec, preferred_element_type=F32)


def _dot_nt(a, b, prec=None):
    return lax.dot_general(a, b, (((1,), (1,)), ((), ())), precision=prec, preferred_element_type=F32)


def _dot_tn(a, b, prec=None):
    return lax.dot_general(a, b, (((0,), (0,)), ((), ())), precision=prec, preferred_element_type=F32)


def _split_dot(x, tri, parts, left=False):
    tot, rest = None, x
    for p in range(parts):
        h = rest.astype(BF16)
        d = _dot(tri, h) if left else _dot(h, tri)
        tot = d if tot is None else tot + d
        if p + 1 < parts:
            rest = rest - h.astype(F32)
    return tot


def _dot1(a, b, fn):
    return fn(a.astype(BF16), b.astype(BF16))


def _dot3(a, b, fn):
    ah, bh = a.astype(BF16), b.astype(BF16)
    al, bl = (a - ah.astype(F32)).astype(BF16), (b - bh.astype(F32)).astype(BF16)
    return fn(ah, bh) + fn(ah, bl) + fn(al, bh)


def _sigmoid(x):
    return jax.nn.sigmoid(x)


def _softplus(x):
    return jnp.maximum(x, 0.0) + jnp.log1p(jnp.exp(-jnp.abs(x)))


class Layer:
    def __init__(self, arr, l):
        self.arr, self.l = arr, l
        self.shape = arr.shape[1:]
        self.part = None


class Split:
    def __init__(self, arr):
        self.arr = arr
        self.shape = (arr.shape[1], arr.shape[0] * arr.shape[2])
        self.part = arr.shape[2]


class Plain:
    def __init__(self, arr):
        self.arr = arr
        self.shape = arr.shape
        self.part = None


def _wrap(op):
    return op if isinstance(op, (Layer, Split, Plain)) else Plain(op)


def _op_spec(op, br, bc, rc_of_grid):
    if isinstance(op, Layer):
        l = op.l
        return pl.BlockSpec((None, br, bc), lambda i, j, k: (l, *rc_of_grid(i, j, k)))
    if isinstance(op, Split):
        per = op.part // bc

        def imap(i, j, k):
            r, c = rc_of_grid(i, j, k)
            return (lax.div(c, per), r, lax.rem(c, per))
        return pl.BlockSpec((None, br, bc), imap)
    return pl.BlockSpec((br, bc), rc_of_grid)


def _hosted(body, n_in, n_out, plan, step_of_grid):
    if plan is None:
        return body
    pi, po, ps = len(plan.args), len(plan.out_shape), len(plan.scratch)

    def wrapped(*refs):
        refs = list(refs)
        ins, pins = refs[:n_in], refs[n_in:n_in + pi]
        outs = refs[n_in + pi:n_in + pi + n_out]
        pouts = refs[n_in + pi + n_out:n_in + pi + n_out + po]
        scr, pscr = refs[n_in + pi + n_out + po:len(refs) - ps], refs[len(refs) - ps:]
        step, nsteps = step_of_grid()

        @pl.when(step == 0)
        def _():
            plan.start(pins, pouts, pscr)

        if plan.has_mid:
            @pl.when(step == min((3 * nsteps) // 4, nsteps - 1))
            def _():
                plan.mid(pins, pouts, pscr)

        body(*ins, *outs, *scr)

        @pl.when(step == nsteps - 1)
        def _():
            plan.finish(pins, pouts, pscr)

    return wrapped


def _host_call(body, n_in, plan, step_of_grid, *, out_shape, grid, in_specs, out_specs, scratch_shapes, name, sem, args):
    single = not isinstance(out_shape, (list, tuple))
    out_shape = [out_shape] if single else list(out_shape)
    out_specs = [out_specs] if single else list(out_specs)
    n_out = len(out_shape)
    in_specs, scratch_shapes, args = list(in_specs), list(scratch_shapes), list(args)
    if plan is not None:
        in_specs += [ANY] * len(plan.args)
        args += plan.args
        out_shape += plan.out_shape
        out_specs += [ANY] * len(plan.out_shape)
        scratch_shapes += plan.scratch
        sem = ("arbitrary",) * len(grid)
    outs = pl.pallas_call(
        _hosted(body, n_in, n_out, plan, step_of_grid), out_shape=out_shape, grid=grid,
        in_specs=in_specs, out_specs=out_specs, scratch_shapes=scratch_shapes,
        name=name, compiler_params=_params(sem),
    )(*args)
    host = outs[0] if single else list(outs[:n_out])
    return host, list(outs[n_out:])


def matmul(a, b, mode, out_dtype, name, plan=None):
    a, b = _wrap(a), _wrap(b)
    if mode == "nn":
        (m, kd), (kd2, n) = a.shape, b.shape
    elif mode == "nt":
        (m, kd), (n, kd2) = a.shape, b.shape
    else:
        (kd, m), (kd2, n) = a.shape, b.shape
    assert kd == kd2, (mode, a.shape, b.shape)

    def dim_tile(full, ops_on_cols, prefs=(1024, 512, 256, 128)):
        base = full
        for op in ops_on_cols:
            if op.part is not None:
                base = math.gcd(base, op.part)
        return _tile(base, prefs)

    tm = dim_tile(m, [a] if mode == "tn" else [])
    tn = dim_tile(n, [b] if mode in ("nn", "tn") else [])
    tk = dim_tile(kd, ([a] if mode in ("nn", "nt") else []) + ([b] if mode == "nt" else []),
                  prefs=(2048, 2816, 1024, 512, 256, 128))
    nk = kd // tk

    if mode == "nn":
        a_spec = _op_spec(a, tm, tk, lambda i, j, k: (i, k))
        b_spec = _op_spec(b, tk, tn, lambda i, j, k: (k, j))
        dot = _dot
    elif mode == "nt":
        a_spec = _op_spec(a, tm, tk, lambda i, j, k: (i, k))
        b_spec = _op_spec(b, tn, tk, lambda i, j, k: (j, k))
        dot = _dot_nt
    else:
        a_spec = _op_spec(a, tk, tm, lambda i, j, k: (k, i))
        b_spec = _op_spec(b, tk, tn, lambda i, j, k: (k, j))
        dot = _dot_tn

    def body(a_ref, b_ref, o_ref, *acc):
        part = dot(a_ref[...].astype(BF16), b_ref[...].astype(BF16))
        if nk == 1:
            o_ref[...] = part.astype(o_ref.dtype)
            return
        (acc_ref,) = acc
        k = pl.program_id(2)

        @pl.when(k == 0)
        def _():
            acc_ref[...] = part

        @pl.when(k > 0)
        def _():
            acc_ref[...] += part

        @pl.when(k == nk - 1)
        def _():
            o_ref[...] = acc_ref[...].astype(o_ref.dtype)

    gi, gj = m // tm, n // tn

    def step_of_grid():
        return (pl.program_id(0) * gj + pl.program_id(1)) * nk + pl.program_id(2), gi * gj * nk

    out, extra = _host_call(
        body, 2, plan, step_of_grid, out_shape=jax.ShapeDtypeStruct((m, n), out_dtype), grid=(gi, gj, nk),
        in_specs=[a_spec, b_spec], out_specs=pl.BlockSpec((tm, tn), lambda i, j, k: (i, j)),
        scratch_shapes=[pltpu.VMEM((tm, tn), F32)] if nk > 1 else [], name=name,
        sem=("parallel", "parallel", "arbitrary"), args=[a.arr, b.arr])
    return out if plan is None else (out, extra)


def _rms(x, g):
    r = lax.rsqrt(jnp.mean(x * x, axis=-1, keepdims=True) + NORM_EPS)
    return x * r * g


def res_norm(h, m, g_a, g_b, name, want_f32=False):
    t, d = h.shape
    tr = min(ROW_TILE, t)
    has_m, has_u = m is not None, g_b is not None
    row = pl.BlockSpec((tr, d), lambda i: (i, 0))
    vec = pl.BlockSpec((1, d), lambda i: (0, 0))

    def body(*refs):
        refs = list(refs)
        h_ref = refs.pop(0)
        hn = h_ref[...]
        if has_m:
            m_ref, ga_ref = refs.pop(0), refs.pop(0)
            hn = hn + _rms(m_ref[...], ga_ref[...])
        if has_u:
            gb_ref = refs.pop(0)
        if has_m:
            refs.pop(0)[...] = hn
        if has_u:
            u = _rms(hn, gb_ref[...])
            refs.pop(0)[...] = u.astype(BF16)
            if want_f32:
                refs.pop(0)[...] = u

    args, in_specs, out_shape, out_specs = [h], [row], [], []
    if has_m:
        args += [m, g_a]
        in_specs += [row, vec]
        out_shape.append(jax.ShapeDtypeStruct((t, d), F32))
        out_specs.append(row)
    if has_u:
        args.append(g_b)
        in_specs.append(vec)
        out_shape.append(jax.ShapeDtypeStruct((t, d), BF16))
        out_specs.append(row)
        if want_f32:
            out_shape.append(jax.ShapeDtypeStruct((t, d), F32))
            out_specs.append(row)
    outs = list(pl.pallas_call(
        body, out_shape=out_shape, grid=(t // tr,), in_specs=in_specs, out_specs=out_specs,
        name=name, compiler_params=_params(("parallel",)),
    )(*args))
    h_new = outs.pop(0) if has_m else None
    u16 = outs.pop(0) if has_u else None
    u32 = outs.pop(0) if (has_u and want_f32) else None
    return h_new, u16, u32


def norm_bwd(x, g, dy, add, out_dtype, name):
    t, d = x.shape
    tr = min(ROW_TILE, t)
    has_add = add is not None
    row = pl.BlockSpec((tr, d), lambda i: (i, 0))
    vec = pl.BlockSpec((1, d), lambda i: (0, 0))

    def body(*refs):
        if has_add:
            x_ref, g_ref, dy_ref, add_ref, dx_ref, dg_ref = refs
        else:
            x_ref, g_ref, dy_ref, dx_ref, dg_ref = refs
        xv = x_ref[...]
        dyv = dy_ref[...].astype(F32)
        r = lax.rsqrt(jnp.mean(xv * xv, axis=-1, keepdims=True) + NORM_EPS)
        gy = dyv * g_ref[...]
        dx = r * gy - xv * (r * r * r * jnp.mean(gy * xv, axis=-1, keepdims=True))
        if has_add:
            dx = dx + add_ref[...]
        dx_ref[...] = dx.astype(dx_ref.dtype)

        @pl.when(pl.program_id(0) == 0)
        def _():
            dg_ref[...] = jnp.zeros_like(dg_ref)

        dg_ref[...] += jnp.sum(dyv * xv * r, axis=0, keepdims=True)

    args = [x, g, dy] + ([add] if has_add else [])
    in_specs = [row, vec, row] + ([row] if has_add else [])
    return pl.pallas_call(
        body, out_shape=[jax.ShapeDtypeStruct((t, d), out_dtype), jax.ShapeDtypeStruct((1, d), F32)],
        grid=(t // tr,), in_specs=in_specs, out_specs=[row, vec],
        name=name, compiler_params=_params(("arbitrary",)),
    )(*args)


def loss_head(y, target, name):
    t, d = y.shape
    tr = min(ROW_TILE, t)
    row = pl.BlockSpec((tr, d), lambda i: (i, 0))
    acc = pl.BlockSpec((8, LANES), lambda i: (0, 0))

    def body(y_ref, t_ref, loss_ref, dy_ref):
        e = y_ref[...] - t_ref[...]
        dy_ref[...] = e * (1.0 / d)

        @pl.when(pl.program_id(0) == 0)
        def _():
            loss_ref[...] = jnp.zeros_like(loss_ref)

        loss_ref[...] += jnp.sum(e * e) * (0.5 / d)

    return pl.pallas_call(
        body, out_shape=[jax.ShapeDtypeStruct((8, LANES), F32), jax.ShapeDtypeStruct((t, d), F32)],
        grid=(t // tr,), in_specs=[row, row], out_specs=[acc, row],
        name=name, compiler_params=_params(("arbitrary",)),
    )(y, target)


def _depth_softmax(ref, depth):
    rows = [ref[i:i + 1, :] for i in range(depth)]
    mx = functools.reduce(jnp.maximum, rows)
    ex = [jnp.exp(r - mx) for r in rows]
    tot = functools.reduce(lambda p, q: p + q, ex)
    return [e / tot for e in ex]


def lower_bounds_fwd(logits):
    depth, kw = logits.shape

    def body(l_ref, o_ref):
        s = _depth_softmax(l_ref, depth)
        run = jnp.zeros_like(s[0])
        o_ref[0:1, :] = run
        for i in range(1, depth):
            run = run + s[i]
            o_ref[i:i + 1, :] = run

    return pl.pallas_call(body, out_shape=jax.ShapeDtypeStruct((depth, kw), F32), name="lb_fwd")(logits)


def lower_bounds_bwd(logits, dlb):
    depth, kw = logits.shape

    def body(l_ref, d_ref, o_ref):
        s = _depth_softmax(l_ref, depth)
        ds = [jnp.zeros_like(s[0]) for _ in range(depth)]
        run = jnp.zeros_like(s[0])
        for j in range(depth - 1, 0, -1):
            run = run + d_ref[j:j + 1, :]
            ds[j] = run
        dot = functools.reduce(lambda p, q: p + q, [s[j] * ds[j] for j in range(depth)])
        for j in range(depth):
            o_ref[j:j + 1, :] = s[j] * (ds[j] - dot)

    return pl.pallas_call(body, out_shape=jax.ShapeDtypeStruct((depth, kw), F32), name="lb_bwd")(logits, dlb)


def _hg_gates(qp, fp, lb_row, has_lb):
    sig = _sigmoid(fp)
    nsig = _sigmoid(-fp)
    ls = jnp.minimum(fp, 0.0) - jnp.log1p(jnp.exp(-jnp.abs(fp)))
    if has_lb:
        a = jnp.log(lb_row)
        bb = jnp.log1p(-lb_row) + ls
        g = jnp.maximum(a, bb) + jnp.log1p(jnp.exp(-jnp.abs(a - bb)))
        w = jnp.exp(bb - g)
        k = (1.0 - lb_row) * nsig
    else:
        g, w, k = ls, None, nsig
    q = qp * _sigmoid(qp)
    return q, k, g, sig, nsig, w


HG_HEADS_PER_STEP = 2


def _heads_per_step(nh):
    return HG_HEADS_PER_STEP if nh % HG_HEADS_PER_STEP == 0 else 1


def _head_views(refs, hh, lanes, lead):
    cs = pl.ds(hh * HEAD, HEAD)
    out = []
    for i, r in enumerate(refs):
        if i in lead:
            out.append(r.at[hh])
        elif i in lanes:
            out.append(r.at[(slice(None),) * (len(r.shape) - 1) + (cs,)])
        else:
            out.append(r)
    return out


def _hg_masks():
    c = HG_CHUNK
    row = lax.broadcasted_iota(jnp.int32, (c, c), 0)
    col = lax.broadcasted_iota(jnp.int32, (c, c), 1)
    lower = (col <= row).astype(BF16)
    upper = (col >= row).astype(BF16)
    krow = lax.broadcasted_iota(jnp.int32, (c, HEAD), 0)
    arow = lax.broadcasted_iota(jnp.int32, (HG_SUB, c), 0)
    acol = lax.broadcasted_iota(jnp.int32, (HG_SUB, c), 1)
    return lower, upper, krow, arow, acol


def _hg_sub(i, q, k, b, b_ref, krow, arow, acol):
    r0 = i * HG_SUB
    m = b_ref[r0:r0 + 1, :]
    ebq = jnp.exp(b[r0:r0 + HG_SUB, :] - m)
    qh = (q[r0:r0 + HG_SUB, :] * ebq).astype(BF16)
    ek = jnp.exp(jnp.where(krow < r0 + HG_SUB, jnp.minimum(m - b, HG_MAX_EXPONENT), 0.0))
    kh = (k * ek).astype(BF16)
    mask = acol <= arow + r0
    amat = jnp.where(mask, _dot_nt(qh, kh), 0.0).astype(BF16)
    return ebq, qh, ek, kh, mask, amat


def hgrn_fwd(proj, lb, onorm_g, has_lb, name, plan=None):
    t, w4 = proj.shape
    kw = w4 // 4
    nh, nc, c = kw // HEAD, t // HG_CHUNK, HG_CHUNK
    hpb = _heads_per_step(nh)
    wide = hpb * HEAD

    def body(*refs):
        refs[7][...] = jnp.zeros_like(refs[7])
        heads = [one_head(*_head_views(refs, hh, lanes=(0, 1, 2, 3, 4, 5, 6), lead=(7, 8))) for hh in range(hpb)]

        def chunks(ci, carry):
            for chunk in heads:
                chunk(ci)
            return carry

        lax.fori_loop(0, nc, chunks, 0)

    def one_head(qp_ref, fp_ref, iv_ref, gp_ref, lb_ref, on_ref, y_ref, st_ref, b_ref):
        lower, _, krow, arow, acol = _hg_masks()
        lb_row, gam = lb_ref[...], on_ref[...]

        def chunk(ci):
            rs = pl.ds(pl.multiple_of(ci * c, c), c)
            v = iv_ref[rs, :].astype(BF16)
            gp = gp_ref[rs, :]
            q, k, g, _, _, _ = _hg_gates(qp_ref[rs, :], fp_ref[rs, :], lb_row, has_lb)
            b = _split_dot(g, lower, 3, left=True)
            b_ref[...] = b
            st = st_ref[...]
            o = _dot_nt((q * jnp.exp(b)).astype(BF16), st.astype(BF16))
            amats = [_hg_sub(i, q, k, b, b_ref, krow, arow, acol)[-1] for i in range(c // HG_SUB)]
            o = o + _dot(jnp.concatenate(amats, axis=0), v)
            bl = b_ref[c - 1:c, :]
            st_ref[...] = jnp.exp(bl) * st + _dot_tn(v, (k * jnp.exp(bl - b)).astype(BF16))
            r = lax.rsqrt(jnp.mean(o * o, axis=-1, keepdims=True) + NORM_EPS)
            y_ref[rs, :] = (o * r * gam * (gp * _sigmoid(gp))).astype(BF16)

        return chunk

    ns = nh // hpb
    col = lambda p: pl.BlockSpec((t, wide), lambda h: (0, p * ns + h))
    vec = pl.BlockSpec((1, wide), lambda h: (0, h))
    return _host_call(
        body, 6, plan, lambda: (pl.program_id(0), ns),
        out_shape=jax.ShapeDtypeStruct((t, kw), BF16), grid=(ns,),
        in_specs=[col(0), col(1), col(2), col(3), vec, vec],
        out_specs=pl.BlockSpec((t, wide), lambda h: (0, h)),
        scratch_shapes=[pltpu.VMEM((hpb, HEAD, HEAD), F32), pltpu.VMEM((hpb, c, HEAD), F32)],
        name=name, sem=("parallel",), args=[proj, proj, proj, proj, lb, onorm_g])


def hgrn_bwd(proj, lb, onorm_g, dy, has_lb, name, plan=None):
    t, w4 = proj.shape
    kw = w4 // 4
    nh, nc, c = kw // HEAD, t // HG_CHUNK, HG_CHUNK
    hpb = _heads_per_step(nh)
    wide = hpb * HEAD

    def body(*refs):
        for i in (8, 9, 13, 14):
            refs[i][...] = jnp.zeros_like(refs[i])
        heads = [one_head(*_head_views(refs, hh, lanes=(0, 1, 2, 3, 4, 5, 6, 7, 8, 9, 11), lead=(10, 12, 13, 14)))
                 for hh in range(hpb)]

        def fwd_chunks(ci, states):
            return tuple(fwd(ci, st) for (fwd, _), st in zip(heads, states))

        lax.fori_loop(0, nc, fwd_chunks, tuple(jnp.zeros((HEAD, HEAD), F32) for _ in heads))

        def bwd_chunks(step, carry):
            for _, bwd in heads:
                bwd(nc - 1 - step)
            return carry

        lax.fori_loop(0, nc, bwd_chunks, 0)

    def one_head(qp_ref, fp_ref, iv_ref, gp_ref, lb_ref, on_ref, dy_ref,
                 dp_ref, dgam_ref, dlb_ref, sst_ref, o_ref, b_ref, dst_ref, car_ref):
        lower, upper, krow, arow, acol = _hg_masks()
        lb_row, gam = lb_ref[...], on_ref[...]
        mm = _dot3 if has_lb else _dot1

        def recompute(ci):
            rs = pl.ds(pl.multiple_of(ci * c, c), c)
            gates = _hg_gates(qp_ref[rs, :], fp_ref[rs, :], lb_row, has_lb)
            b = _split_dot(gates[2], lower, 3, left=True)
            b_ref[...] = b
            return rs, gates, b

        def fwd_chunk(ci, carry):
            rs, (q, k, g, _, _, _), b = recompute(ci)
            v = iv_ref[rs, :].astype(BF16)
            st = carry
            sst_ref[ci] = st
            o = _dot_nt((q * jnp.exp(b)).astype(BF16), st.astype(BF16))
            amats = [_hg_sub(i, q, k, b, b_ref, krow, arow, acol)[-1] for i in range(c // HG_SUB)]
            o_ref[rs, :] = o + _dot(jnp.concatenate(amats, axis=0), v)
            bl = b_ref[c - 1:c, :]
            return jnp.exp(bl) * st + mm(iv_ref[rs, :], k * jnp.exp(bl - b), _dot_tn)

        def bwd_chunk(ci):
            rs, (q, k, g, sig, nsig, w), b = recompute(ci)
            qp, gp, v32 = qp_ref[rs, :], gp_ref[rs, :], iv_ref[rs, :]
            v = v32.astype(BF16)
            dyv, o = dy_ref[rs, :], o_ref[rs, :]
            st = sst_ref[ci]
            dst_new = dst_ref[...]
            r = lax.rsqrt(jnp.mean(o * o, axis=-1, keepdims=True) + NORM_EPS)
            on = o * r
            sgm = _sigmoid(gp)
            sg = gp * sgm
            dgam_ref[...] += jnp.sum(dyv * sg * on, axis=0, keepdims=True)
            dgp = dyv * on * gam * (sgm * (1.0 + gp * (1.0 - sgm)))
            dn = dyv * gam * sg
            do = r * dn - o * (r * r * r * jnp.mean(dn * o, axis=-1, keepdims=True))
            eb = jnp.exp(b)
            bl = b_ref[c - 1:c, :]
            ekd = jnp.exp(bl - b)
            do16 = do.astype(BF16)
            dq = mm(do, st, _dot) * eb
            dv = _dot1(k * ekd, dst_new, _dot_nt)
            dk = mm(v32, dst_new, _dot) * ekd
            dst_ref[...] = jnp.exp(bl) * dst_new + mm(do, q * eb, _dot_tn)
            da_all = mm(do, v32, _dot_nt)
            dq_parts, amats = [], []
            for i in range(c // HG_SUB):
                r0 = i * HG_SUB
                ebq, _, ek, _, mask, amat = _hg_sub(i, q, k, b, b_ref, krow, arow, acol)
                da = jnp.where(mask, da_all[r0:r0 + HG_SUB, :], 0.0)
                dq_parts.append(mm(da, k * ek, _dot) * ebq)
                dk = dk + mm(da, q[r0:r0 + HG_SUB, :] * ebq, _dot_tn) * ek
                amats.append(amat)
            dq = dq + jnp.concatenate(dq_parts, axis=0)
            dv = dv + _dot_tn(jnp.concatenate(amats, axis=0), do16)
            db = q * dq - k * dk
            dg = car_ref[...] + _split_dot(db, upper, 3, left=True)
            car_ref[...] += jnp.sum(db, axis=0, keepdims=True)
            if has_lb:
                dfp = dg * nsig * w - dk * ((1.0 - lb_row) * sig * nsig)
                dlb_ref[...] += jnp.sum(dg * nsig * jnp.exp(-g) - dk * nsig, axis=0, keepdims=True)
            else:
                dfp = dg * nsig - dk * (sig * nsig)
            sq = _sigmoid(qp)
            dp_ref[0, rs, :] = (dq * (sq * (1.0 + qp * (1.0 - sq)))).astype(BF16)
            dp_ref[1, rs, :] = dfp.astype(BF16)
            dp_ref[2, rs, :] = dv.astype(BF16)
            dp_ref[3, rs, :] = dgp.astype(BF16)

        return fwd_chunk, bwd_chunk

    ns = nh // hpb
    col = lambda p: pl.BlockSpec((t, wide), lambda h: (0, p * ns + h))
    vec = pl.BlockSpec((1, wide), lambda h: (0, h))
    return _host_call(
        body, 7, plan, lambda: (pl.program_id(0), ns),
        out_shape=[jax.ShapeDtypeStruct((4, t, kw), BF16), jax.ShapeDtypeStruct((1, kw), F32),
                   jax.ShapeDtypeStruct((1, kw), F32)],
        grid=(ns,),
        in_specs=[col(0), col(1), col(2), col(3), vec, vec, pl.BlockSpec((t, wide), lambda h: (0, h))],
        out_specs=[pl.BlockSpec((4, t, wide), lambda h: (0, 0, h)), vec, vec],
        scratch_shapes=[pltpu.VMEM((hpb, nc, HEAD, HEAD), F32), pltpu.VMEM((t, wide), F32),
                        pltpu.VMEM((hpb, c, HEAD), F32), pltpu.VMEM((hpb, HEAD, HEAD), F32),
                        pltpu.VMEM((hpb, 1, HEAD), F32)],
        name=name, sem=("parallel",), args=[proj, proj, proj, proj, lb, onorm_g, dy])


def _sb_masks():
    m, n = SB_QROWS, SB_BLOCK
    row = lax.broadcasted_iota(jnp.int32, (m, n), 0)
    col = lax.broadcasted_iota(jnp.int32, (m, n), 1)
    r2 = lax.broadcasted_iota(jnp.int32, (n, n), 0)
    c2 = lax.broadcasted_iota(jnp.int32, (n, n), 1)
    after = (r2 > c2).astype(BF16)
    from_ = (r2 >= c2).astype(BF16)
    return row, col, after, from_


def sba_fwd(qkv, name, plan=None):
    t, w3 = qkv.shape
    wd = w3 // 3
    m, n = SB_QROWS, SB_BLOCK
    nh, nq, per = wd // HEAD, t // m, m // n
    scale = HEAD ** -0.5

    def body(q_ref, k_ref, v_ref, o_ref, q16, k16, v16):
        row, col, after, _ = _sb_masks()
        q16[...] = (q_ref[...] * scale).astype(BF16)
        k16[...] = k_ref[...].astype(BF16)
        v16[...] = v_ref[...].astype(BF16)

        def qblock(qi, carry):
            qs = pl.ds(pl.multiple_of(qi * m, m), m)
            q = q16[qs, :]
            last = (qi + 1) * per - 1

            def kblock(step, state):
                acc, rem0 = state
                kj = last - step
                ks = pl.ds(pl.multiple_of(kj * n, n), n)
                z = _dot_nt(q, k16[ks, :])
                strict = (col + kj * n) < (row + qi * m)
                spz = _softplus(z)
                sp = jnp.where(strict, spz, 0.0)
                rem = rem0 + _split_dot(sp, after, 2)
                a = jnp.where(strict, jnp.exp(z - spz - rem), 0.0)
                acc = acc + _dot(a.astype(BF16), v16[ks, :])
                return acc, rem0 + jnp.sum(sp, axis=1, keepdims=True)

            acc, _ = lax.fori_loop(0, last + 1, kblock,
                                   (jnp.zeros((m, HEAD), F32), jnp.zeros((m, 1), F32)))
            o_ref[qs, :] = acc
            return carry

        lax.fori_loop(0, nq, qblock, 0)

    col_spec = lambda p: pl.BlockSpec((t, HEAD), lambda h: (0, p * nh + h))
    return _host_call(
        body, 3, plan, lambda: (pl.program_id(0), nh),
        out_shape=jax.ShapeDtypeStruct((t, wd), F32), grid=(nh,),
        in_specs=[col_spec(0), col_spec(1), col_spec(2)],
        out_specs=pl.BlockSpec((t, HEAD), lambda h: (0, h)),
        scratch_shapes=[pltpu.VMEM((t, HEAD), BF16)] * 3,
        name=name, sem=("parallel",), args=[qkv, qkv, qkv])


def sba_bwd(qkv, o, do, name, plan=None):
    t, w3 = qkv.shape
    wd = w3 // 3
    m, n = SB_QROWS, SB_BLOCK
    nh, nq, per = wd // HEAD, t // m, m // n
    scale = HEAD ** -0.5

    def body(q_ref, k_ref, v_ref, o_ref, do_ref, d_ref, dk_ref, dv_ref, q16, k16, v16):
        row, col, after, from_ = _sb_masks()
        dk_ref[...] = jnp.zeros_like(dk_ref)
        dv_ref[...] = jnp.zeros_like(dv_ref)
        q16[...] = (q_ref[...] * scale).astype(BF16)
        k16[...] = k_ref[...].astype(BF16)
        v16[...] = v_ref[...].astype(BF16)

        def qblock(qi, carry):
            qs = pl.ds(pl.multiple_of(qi * m, m), m)
            q = q16[qs, :]
            dov = do_ref[qs, :]
            do16 = dov.astype(BF16)
            dsum = jnp.sum(do16.astype(F32) * o_ref[qs, :], axis=1, keepdims=True)
            last = (qi + 1) * per - 1

            def kblock(step, state):
                dq, rem0, e0 = state
                kj = last - step
                ks = pl.ds(pl.multiple_of(kj * n, n), n)
                kv, vv = k16[ks, :], v16[ks, :]
                z = _dot_nt(q, kv)
                strict = (col + kj * n) < (row + qi * m)
                spz = _softplus(z)
                sp = jnp.where(strict, spz, 0.0)
                rem = rem0 + _split_dot(sp, after, 2)
                sgz = jnp.exp(z - spz)
                a = jnp.where(strict, sgz * jnp.exp(-rem), 0.0).astype(BF16)
                e = a.astype(F32) * _dot_nt(do16, vv)
                left = dsum - (e0 + _split_dot(e, from_, 2))
                dz = jnp.where(strict, e * (1.0 - sgz) - sgz * left, 0.0).astype(BF16)
                dq = dq + _dot(dz, kv)
                dk_ref[ks, :] += _dot_tn(dz, q)
                dv_ref[ks, :] += _dot_tn(a, do16)
                return (dq, rem0 + jnp.sum(sp, axis=1, keepdims=True),
                        e0 + jnp.sum(e, axis=1, keepdims=True))

            zero1 = jnp.zeros((m, 1), F32)
            dq, _, _ = lax.fori_loop(0, last + 1, kblock, (jnp.zeros((m, HEAD), F32), zero1, zero1))
            d_ref[0, qs, :] = (dq * scale).astype(BF16)
            return carry

        lax.fori_loop(0, nq, qblock, 0)
        d_ref[1, :, :] = dk_ref[...].astype(BF16)
        d_ref[2, :, :] = dv_ref[...].astype(BF16)

    col_spec = lambda p: pl.BlockSpec((t, HEAD), lambda h: (0, p * nh + h))
    head = pl.BlockSpec((t, HEAD), lambda h: (0, h))
    return _host_call(
        body, 5, plan, lambda: (pl.program_id(0), nh),
        out_shape=jax.ShapeDtypeStruct((3, t, wd), BF16), grid=(nh,),
        in_specs=[col_spec(0), col_spec(1), col_spec(2), head, head],
        out_specs=pl.BlockSpec((3, t, HEAD), lambda h: (0, 0, h)),
        scratch_shapes=[pltpu.VMEM((t, HEAD), F32)] * 2 + [pltpu.VMEM((t, HEAD), BF16)] * 3,
        name=name, sem=("parallel",), args=[qkv, qkv, qkv, o, do])


def _pool_band(i_out, i_in, tr, win, transpose):
    r = lax.broadcasted_iota(jnp.int32, (tr, tr), 0) + i_out * tr
    c = lax.broadcasted_iota(jnp.int32, (tr, tr), 1) + i_in * tr
    if transpose:
        return ((r <= c) & (r > c - win)).astype(F32)
    return ((c <= r) & (c > r - win)).astype(F32)


def _pool_p(u_ref, i, tr, win):
    cur = u_ref[i * tr:(i + 1) * tr, :]
    ws = _dot(_pool_band(i, i, tr, win, False), cur, HI)
    if i > 0:
        ws = ws + _dot(_pool_band(i, i - 1, tr, win, False), u_ref[(i - 1) * tr:i * tr, :], HI)
    pos = lax.broadcasted_iota(jnp.int32, (tr, 1), 0) + (i * tr + 1)
    count = jnp.minimum(pos, win).astype(F32)
    return ws / count - cur, count


def pool_fwd(u, pool_w, pool_scale, name):
    t, d = u.shape
    ng = len(POOL_WINDOWS)
    gs = d // ng
    tr = min(ROW_TILE, t)

    def body(u_ref, w_ref, s_ref, y_ref):
        win = jnp.left_shift(2, pl.program_id(0))
        for i in range(t // tr):
            p, _ = _pool_p(u_ref, i, tr, win)
            y_ref[i * tr:(i + 1) * tr, :] = _dot(p.astype(BF16), w_ref[...]) * s_ref[...]

    grp = pl.BlockSpec((t, gs), lambda g: (0, g))
    return pl.pallas_call(
        body, out_shape=jax.ShapeDtypeStruct((t, d), F32), grid=(ng,),
        in_specs=[grp, pl.BlockSpec((None, gs, gs), lambda g: (g, 0, 0)), pl.BlockSpec((1, gs), lambda g: (0, g))],
        out_specs=grp, name=name, compiler_params=_params(("parallel",)),
    )(u, pool_w, pool_scale)


def pool_bwd(u, pool_w, pool_scale, dy, name):
    t, d = u.shape
    ng = len(POOL_WINDOWS)
    gs = d // ng
    tr = min(ROW_TILE, t)
    nt = t // tr

    def body(u_ref, w_ref, s_ref, dy_ref, du_ref, dw_ref, ds_ref, dpc_ref, dp_ref):
        win = jnp.left_shift(2, pl.program_id(0))
        wv = w_ref[...]
        dw = jnp.zeros((gs, gs), F32)
        dsc = jnp.zeros((1, gs), F32)
        for i in range(nt):
            rows = slice(i * tr, (i + 1) * tr)
            p, count = _pool_p(u_ref, i, tr, win)
            p16 = p.astype(BF16)
            dyv = dy_ref[rows, :]
            dsc = dsc + jnp.sum(dyv * _dot(p16, wv), axis=0, keepdims=True)
            dyp = (dyv * s_ref[...]).astype(BF16)
            dw = dw + _dot_tn(p16, dyp)
            dp = _dot_nt(dyp, wv)
            dp_ref[rows, :] = dp
            dpc_ref[rows, :] = dp / count
        dw_ref[...] = dw.astype(BF16)
        ds_ref[...] = dsc
        for i in range(nt):
            rows = slice(i * tr, (i + 1) * tr)
            acc = _dot(_pool_band(i, i, tr, win, True), dpc_ref[rows, :], HI)
            if i + 1 < nt:
                acc = acc + _dot(_pool_band(i, i + 1, tr, win, True), dpc_ref[(i + 1) * tr:(i + 2) * tr, :], HI)
            du_ref[rows, :] = acc - dp_ref[rows, :]

    grp = pl.BlockSpec((t, gs), lambda g: (0, g))
    wspec = pl.BlockSpec((None, gs, gs), lambda g: (g, 0, 0))
    vec = pl.BlockSpec((1, gs), lambda g: (0, g))
    return pl.pallas_call(
        body,
        out_shape=[jax.ShapeDtypeStruct((t, d), F32), jax.ShapeDtypeStruct((ng, gs, gs), BF16),
                   jax.ShapeDtypeStruct((1, d), F32)],
        grid=(ng,), in_specs=[grp, wspec, vec, grp], out_specs=[grp, wspec, vec],
        scratch_shapes=[pltpu.VMEM((t, gs), F32), pltpu.VMEM((t, gs), F32)],
        name=name, compiler_params=_params(("parallel",)),
    )(u, pool_w, pool_scale, dy)


CONV_COLS = 256
HALO = 8


def _conv_taps(ref, r0, tr):
    x = ref[r0:r0 + tr, :]
    prev = ref[r0 - HALO:r0, :] if r0 > 0 else jnp.zeros((HALO, x.shape[1]), F32)
    xx = jnp.concatenate([prev, x], axis=0)
    return x, pltpu.roll(xx, 1, 0)[HALO:, :], pltpu.roll(xx, 2, 0)[HALO:, :]


def _conv_out(taps, w_ref, b_ref):
    x, s1, s2 = taps
    return w_ref[0:1, :] * s2 + w_ref[1:2, :] * s1 + w_ref[2:3, :] * x + b_ref[...]


def conv_glu_fwd(up, conv_w, conv_b, name):
    t, f2 = up.shape
    f = f2 // 2
    tc = min(CONV_COLS, f)
    nj = f // tc
    tr = min(ROW_TILE, t)

    def body(ug_ref, uv_ref, wg_ref, wv_ref, bg_ref, bv_ref, o_ref):
        for i in range(t // tr):
            r0 = i * tr
            gate = _conv_out(_conv_taps(ug_ref, r0, tr), wg_ref, bg_ref)
            val = _conv_out(_conv_taps(uv_ref, r0, tr), wv_ref, bv_ref)
            o_ref[r0:r0 + tr, :] = (gate * _sigmoid(gate) * val).astype(BF16)

    blk = lambda rows, half: pl.BlockSpec((rows, tc), lambda j: (0, half * nj + j))
    return pl.pallas_call(
        body, out_shape=jax.ShapeDtypeStruct((t, f), BF16), grid=(nj,),
        in_specs=[blk(t, 0), blk(t, 1), blk(CONV_WIDTH, 0), blk(CONV_WIDTH, 1), blk(1, 0), blk(1, 1)],
        out_specs=pl.BlockSpec((t, tc), lambda j: (0, j)),
        name=name, compiler_params=_params(("parallel",)),
    )(up, up, conv_w, conv_w, conv_b, conv_b)


def conv_glu_bwd(up, conv_w, conv_b, dact, name):
    t, f2 = up.shape
    f = f2 // 2
    tc = min(CONV_COLS, f)
    nj = f // tc
    tr = min(ROW_TILE, t)
    nt = t // tr

    def body(ug_ref, uv_ref, wg_ref, wv_ref, bg_ref, bv_ref, da_ref, du_ref, dw_ref, db_ref, dg_ref, dv_ref):
        dwg = [jnp.zeros((1, tc), F32) for _ in range(CONV_WIDTH)]
        dwv = [jnp.zeros((1, tc), F32) for _ in range(CONV_WIDTH)]
        dbg = jnp.zeros((1, tc), F32)
        dbv = jnp.zeros((1, tc), F32)
        for i in range(nt):
            r0 = i * tr
            tg, tv = _conv_taps(ug_ref, r0, tr), _conv_taps(uv_ref, r0, tr)
            gate, val = _conv_out(tg, wg_ref, bg_ref), _conv_out(tv, wv_ref, bv_ref)
            sg = _sigmoid(gate)
            da = da_ref[r0:r0 + tr, :]
            d_gate = da * val * (sg * (1.0 + gate * (1.0 - sg)))
            d_val = da * (gate * sg)
            dg_ref[r0:r0 + tr, :] = d_gate
            dv_ref[r0:r0 + tr, :] = d_val
            dbg = dbg + jnp.sum(d_gate, axis=0, keepdims=True)
            dbv = dbv + jnp.sum(d_val, axis=0, keepdims=True)
            for tap in range(CONV_WIDTH):
                dwg[tap] = dwg[tap] + jnp.sum(d_gate * tg[2 - tap], axis=0, keepdims=True)
                dwv[tap] = dwv[tap] + jnp.sum(d_val * tv[2 - tap], axis=0, keepdims=True)
        for tap in range(CONV_WIDTH):
            dw_ref[0, tap:tap + 1, :] = dwg[tap]
            dw_ref[1, tap:tap + 1, :] = dwv[tap]
        db_ref[0, :, :] = dbg
        db_ref[1, :, :] = dbv
        for half, (d_ref, w_ref) in enumerate(((dg_ref, wg_ref), (dv_ref, wv_ref))):
            for i in range(nt):
                r0 = i * tr
                x = d_ref[r0:r0 + tr, :]
                nxt = d_ref[r0 + tr:r0 + tr + HALO, :] if i + 1 < nt else jnp.zeros((HALO, tc), F32)
                xx = jnp.concatenate([x, nxt], axis=0)
                up1 = pltpu.roll(xx, tr + HALO - 1, 0)[:tr, :]
                up2 = pltpu.roll(xx, tr + HALO - 2, 0)[:tr, :]
                du = w_ref[2:3, :] * x + w_ref[1:2, :] * up1 + w_ref[0:1, :] * up2
                du_ref[half, r0:r0 + tr, :] = du.astype(BF16)

    blk = lambda rows, half: pl.BlockSpec((rows, tc), lambda j: (0, half * nj + j))
    return pl.pallas_call(
        body,
        out_shape=[jax.ShapeDtypeStruct((2, t, f), BF16), jax.ShapeDtypeStruct((2, CONV_WIDTH, f), F32),
                   jax.ShapeDtypeStruct((2, 1, f), F32)],
        grid=(nj,),
        in_specs=[blk(t, 0), blk(t, 1), blk(CONV_WIDTH, 0), blk(CONV_WIDTH, 1), blk(1, 0), blk(1, 1),
                  pl.BlockSpec((t, tc), lambda j: (0, j))],
        out_specs=[pl.BlockSpec((2, t, tc), lambda j: (0, 0, j)),
                   pl.BlockSpec((2, CONV_WIDTH, tc), lambda j: (0, 0, j)),
                   pl.BlockSpec((2, 1, tc), lambda j: (0, 0, j))],
        scratch_shapes=[pltpu.VMEM((t, tc), F32), pltpu.VMEM((t, tc), F32)],
        name=name, compiler_params=_params(("parallel",)),
    )(up, up, conv_w, conv_w, conv_b, conv_b, dact)


def _place():
    x, y, c = lax.axis_index("x"), lax.axis_index("y"), lax.axis_index("c")
    others = [(1 - x, y), (x, 1 - y), (1 - x, 1 - y)]
    return x, y, c, others


def _window(ref, axis, b, n):
    if axis == 1:
        return ref.at[:, pl.ds(b * n, n), :]
    return ref.at[:, :, pl.ds(b * n, n)]


def _remote(src, dst, send_sems, recv_sems, k, to):
    return pltpu.make_async_remote_copy(src_ref=src, dst_ref=dst, send_sem=send_sems.at[k],
                                        recv_sem=recv_sems.at[k], device_id=to, device_id_type=MESH)


class GatherPlan:
    has_mid = True

    def __init__(self, items):
        self.items = items
        self.args = [s for s, _, _, _ in items]
        self.out_shape = []
        for s, _, nl, ax in items:
            shp = [nl, s.shape[1], s.shape[2]]
            shp[ax] *= N_DEV
            self.out_shape.append(jax.ShapeDtypeStruct(tuple(shp), s.dtype))
        n = len(items)
        self.scratch = [pltpu.SemaphoreType.DMA((7 * n,)), pltpu.SemaphoreType.DMA((7 * n,)),
                        pltpu.SemaphoreType.DMA((n,))]

    def _mine(self, ins, a):
        _, l0, nl, _ = self.items[a]
        return ins[a].at[pl.ds(l0, nl)]

    def _copy(self, ins, outs, sems, a, k, block, to, own=False):
        s, _, _, ax = self.items[a]
        px, py, pc = block
        w = _window(outs[a], ax, 4 * px + 2 * py + pc, s.shape[ax])
        return _remote(self._mine(ins, a) if own else w, w, sems[0], sems[1], 7 * a + k, to)

    def _local(self, ins, outs, sems, a, x, y, c):
        s, _, _, ax = self.items[a]
        return pltpu.make_async_copy(self._mine(ins, a), _window(outs[a], ax, 4 * x + 2 * y + c, s.shape[ax]),
                                     sems[2].at[a])

    def _first(self, ins, outs, sems, a, x, y, c, others):
        me = (x, y, c)
        return [self._copy(ins, outs, sems, a, 0, me, (x, y, 1 - c), own=True)] + [
            self._copy(ins, outs, sems, a, 1 + j, me, (*chip, c), own=True) for j, chip in enumerate(others)]

    def start(self, ins, outs, sems):
        x, y, c, others = _place()
        for a in range(len(self.items)):
            self._local(ins, outs, sems, a, x, y, c).start()
        for a in range(len(self.items)):
            for cp in self._first(ins, outs, sems, a, x, y, c, others):
                cp.start()

    def mid(self, ins, outs, sems):
        x, y, c, others = _place()
        for a in range(len(self.items)):
            for j, chip in enumerate(others):
                self._copy(ins, outs, sems, a, 1 + j, (*chip, c), (x, y, c)).wait_recv()
                self._copy(ins, outs, sems, a, 4 + j, (*chip, c), (x, y, 1 - c)).start()

    def finish(self, ins, outs, sems):
        x, y, c, others = _place()
        for a in range(len(self.items)):
            self._copy(ins, outs, sems, a, 0, (x, y, 1 - c), (x, y, c)).wait_recv()
            for j, chip in enumerate(others):
                self._copy(ins, outs, sems, a, 4 + j, (*chip, 1 - c), (x, y, c)).wait_recv()
        for a in range(len(self.items)):
            for cp in self._first(ins, outs, sems, a, x, y, c, others):
                cp.wait_send()
            for j, chip in enumerate(others):
                self._copy(ins, outs, sems, a, 4 + j, (*chip, c), (x, y, 1 - c)).wait_send()
            self._local(ins, outs, sems, a, x, y, c).wait()


class ExchangePlan:
    has_mid = False

    def __init__(self, partials):
        self.args = list(partials)
        self.out_shape = [jax.ShapeDtypeStruct(p.shape, p.dtype) for p in partials]
        n = len(partials)
        self.scratch = [pltpu.SemaphoreType.DMA((3 * n,)), pltpu.SemaphoreType.DMA((3 * n,)),
                        pltpu.SemaphoreType.DMA((n,))]

    def _copies(self, ins, outs, sems):
        x, y, c, others = _place()
        me = 2 * x + y
        local, sends, recvs = [], [], []
        for a in range(len(self.args)):
            local.append(pltpu.make_async_copy(ins[a].at[me], outs[a].at[me], sems[2].at[a]))
            for j, (px, py) in enumerate(others):
                sends.append(_remote(ins[a].at[2 * px + py], outs[a].at[me], sems[0], sems[1], 3 * a + j, (px, py, c)))
                slot = outs[a].at[2 * px + py]
                recvs.append(_remote(slot, slot, sems[0], sems[1], 3 * a + j, (px, py, c)))
        return local, sends, recvs

    def start(self, ins, outs, sems):
        local, sends, _ = self._copies(ins, outs, sems)
        for cp in local + sends:
            cp.start()

    def finish(self, ins, outs, sems):
        local, sends, recvs = self._copies(ins, outs, sems)
        for cp in recvs:
            cp.wait_recv()
        for cp in sends:
            cp.wait_send()
        for cp in local:
            cp.wait()


def run_plan(plan, name):
    ni, no = len(plan.args), len(plan.out_shape)

    def body(*refs):
        ins, outs, sems = refs[:ni], refs[ni:ni + no], refs[ni + no:]
        plan.start(ins, outs, sems)
        if plan.has_mid:
            plan.mid(ins, outs, sems)
        plan.finish(ins, outs, sems)

    return pl.pallas_call(
        body, out_shape=plan.out_shape, in_specs=[ANY] * ni, out_specs=[ANY] * no,
        scratch_shapes=plan.scratch, name=name,
    )(*plan.args)


class SiblingPlan:
    has_mid = False

    def __init__(self, grads, axes):
        self.args, self.axes = list(grads), list(axes)
        self.widths = [g.shape[ax] // N_DEV for g, ax in zip(grads, axes)]
        self.out_shape = []
        for g, ax, n in zip(grads, axes, self.widths):
            shp = list(g.shape)
            shp[ax] = n
            self.out_shape.append(jax.ShapeDtypeStruct((N_CHIP, *shp), g.dtype))
        n = len(grads)
        self.scratch = [pltpu.SemaphoreType.DMA((N_CHIP * n,)), pltpu.SemaphoreType.DMA((N_CHIP * n,))]

    def _copies(self, ins, outs, sems):
        x, y, c, _ = _place()
        copies = []
        for a in range(len(self.args)):
            for q in range(N_CHIP):
                src = _window(ins[a], self.axes[a], 2 * q + (1 - c), self.widths[a])
                copies.append(_remote(src, outs[a].at[q], sems[0], sems[1], N_CHIP * a + q, (x, y, 1 - c)))
        return copies

    def start(self, ins, outs, sems):
        for cp in self._copies(ins, outs, sems):
            cp.start()

    def finish(self, ins, outs, sems):
        copies = self._copies(ins, outs, sems)
        for cp in copies:
            cp.wait_recv()
        for cp in copies:
            cp.wait_send()


class JointPlan:
    def __init__(self, plans):
        self.plans = plans
        self.args = [a for p in plans for a in p.args]
        self.out_shape = [o for p in plans for o in p.out_shape]
        self.scratch = [s for p in plans for s in p.scratch]
        self.has_mid = any(p.has_mid for p in plans)

    def split(self, seq, field):
        out, off = [], 0
        for p in self.plans:
            n = len(getattr(p, field))
            out.append(seq[off:off + n])
            off += n
        return out

    def _each(self, ins, outs, sems):
        return zip(self.plans, self.split(ins, "args"), self.split(outs, "out_shape"), self.split(sems, "scratch"))

    def start(self, ins, outs, sems):
        for p, i, o, s in self._each(ins, outs, sems):
            p.start(i, o, s)

    def mid(self, ins, outs, sems):
        for p, i, o, s in self._each(ins, outs, sems):
            if p.has_mid:
                p.mid(i, o, s)

    def finish(self, ins, outs, sems):
        for p, i, o, s in self._each(ins, outs, sems):
            p.finish(i, o, s)


def _peer_of(k, x, y, c):
    return (1 - x if k & 4 else x, 1 - y if k & 2 else y, 1 - c if k & 1 else c)


def small_exchange(vec, reduce, name):
    r = vec.shape[0]

    def body(v_ref, o_ref, *scratch):
        if reduce:
            buf, send_sems, recv_sems = scratch
        else:
            buf, (send_sems, recv_sems) = o_ref, scratch
        x, y, c, _ = _place()
        me = 4 * x + 2 * y + c
        copies = []
        for k in range(1, N_DEV):
            cp = _remote(v_ref, buf.at[me], send_sems, recv_sems, k - 1, _peer_of(k, x, y, c))
            cp.start()
            copies.append(cp)
        buf[me] = v_ref[...]
        for k in range(1, N_DEV):
            px, py, pc = _peer_of(k, x, y, c)
            slot = buf.at[4 * px + 2 * py + pc]
            _remote(slot, slot, send_sems, recv_sems, k - 1, (px, py, pc)).wait_recv()
        for cp in copies:
            cp.wait_send()
        if reduce:
            tot = buf[0]
            for b in range(1, N_DEV):
                tot = tot + buf[b]
            o_ref[...] = tot

    sems = [pltpu.SemaphoreType.DMA((N_DEV - 1,)), pltpu.SemaphoreType.DMA((N_DEV - 1,))]
    if reduce:
        out_shape = jax.ShapeDtypeStruct((r, LANES), F32)
        scratch = [pltpu.VMEM((N_DEV, r, LANES), F32)] + sems
    else:
        out_shape = jax.ShapeDtypeStruct((N_DEV, r, LANES), F32)
        scratch = sems
    return pl.pallas_call(
        body, out_shape=out_shape, in_specs=[VMEM_FULL], out_specs=VMEM_FULL,
        scratch_shapes=scratch, name=name, compiler_params=_params(None),
    )(vec)


def _row_tile(rows, cols, limit=ELEMWISE_BLOCK_ELEMS):
    best = None
    for tb in range(16, rows + 1, 16):
        if rows % tb == 0 and tb * cols <= limit:
            best = tb
    return best if best is not None else rows


def add_sibling(grad, recv, axis, core):
    nl = grad.shape[0]
    _, _, r, cc = recv.shape
    tb = _row_tile(r, cc, 4 * ELEMWISE_BLOCK_ELEMS)
    per = r // tb

    def body(c_ref, g_ref, r_ref, o_ref):
        del c_ref
        o_ref[...] = (g_ref[...].astype(F32) + r_ref[...].astype(F32)).astype(BF16)

    if axis == 2:
        g_spec = pl.BlockSpec((None, tb, cc), lambda q, l, i, c_ref: (l, i, 2 * q + c_ref[0]))
    else:
        g_spec = pl.BlockSpec((None, tb, cc), lambda q, l, i, c_ref: (l, (2 * q + c_ref[0]) * per + i, 0))
    slot = pl.BlockSpec((None, None, tb, cc), lambda q, l, i, c_ref: (q, l, i, 0))
    return pl.pallas_call(
        body, out_shape=jax.ShapeDtypeStruct(recv.shape, BF16),
        grid_spec=pltpu.PrefetchScalarGridSpec(
            num_scalar_prefetch=1, grid=(N_CHIP, nl, per), in_specs=[g_spec, slot], out_specs=slot),
        name="add_sibling", compiler_params=_params(("parallel", "parallel", "parallel")),
    )(core, grad, recv)


def _adamw(w, g, m, v):
    m = ADAM_B1 * m + (1.0 - ADAM_B1) * g
    v = ADAM_B2 * v + (1.0 - ADAM_B2) * (g * g)
    m_hat = m / (1.0 - ADAM_B1 ** ADAM_STEP)
    v_hat = v / (1.0 - ADAM_B2 ** ADAM_STEP)
    delta = -ADAM_LR * (m_hat / (jnp.sqrt(v_hat) + ADAM_EPS) + ADAM_WD * w)
    return delta, m, v


def adam_from_partials(recv, w, m, v, l0, bufs):
    nl = recv.shape[1]
    _, r, cc = w.shape
    tb = _row_tile(r, cc)

    def body(p0, p1, p2, p3, w_ref, m_ref, v_ref, b0, b1, b2, b3, g_out, d_out, m_out, v_out):
        del b0, b1, b2, b3
        g = p0[...].astype(F32) + p1[...].astype(F32) + p2[...].astype(F32) + p3[...].astype(F32)
        d, mn, vn = _adamw(w_ref[...], g, m_ref[...], v_ref[...])
        g_out[...], d_out[...], m_out[...], v_out[...] = g, d, mn, vn

    slot = lambda q: pl.BlockSpec((None, None, tb, cc), lambda l, i: (q, l, i, 0))
    blk = pl.BlockSpec((None, tb, cc), lambda l, i: (l0 + l, i, 0))
    shp = jax.ShapeDtypeStruct(w.shape, F32)
    return pl.pallas_call(
        body, out_shape=[shp] * 4, grid=(nl, r // tb),
        in_specs=[slot(0), slot(1), slot(2), slot(3), blk, blk, blk] + [ANY] * 4, out_specs=[blk] * 4,
        input_output_aliases={7: 0, 8: 1, 9: 2, 10: 3},
        name="adam_big", compiler_params=_params(("parallel", "parallel")),
    )(recv, recv, recv, recv, w, m, v, *bufs)


def adam_small(w, g, m, v):
    def body(w_ref, g_ref, m_ref, v_ref, d_out, m_out, v_out):
        d_out[...], m_out[...], v_out[...] = _adamw(w_ref[...], g_ref[...], m_ref[...], v_ref[...])

    shp = jax.ShapeDtypeStruct(w.shape, F32)
    return pl.pallas_call(body, out_shape=[shp] * 3, name="adam_small")(w, g, m, v)


def _pack(arrays, multiple=8 * LANES):
    flat = jnp.concatenate([a.reshape(-1) for a in arrays])
    pad = (-flat.shape[0]) % multiple
    if pad:
        flat = jnp.concatenate([flat, jnp.zeros((pad,), flat.dtype)])
    return flat.reshape(-1, LANES)


def _unpack(packed, shapes):
    flat = packed.reshape(packed.shape[:-2] + (-1,))
    out, off = [], 0
    for shp in shapes:
        n = math.prod(shp)
        out.append(flat[..., off:off + n].reshape(packed.shape[:-2] + tuple(shp)))
        off += n
    return out


def _unshard_last(stacked):
    moved = jnp.moveaxis(stacked, 0, -2)
    return moved.reshape(moved.shape[:-2] + (-1,))


def _shard_last(full, block):
    n = full.shape[-1] // N_DEV
    return lax.dynamic_slice_in_dim(full, block * n, n, axis=full.ndim - 1)


BIG_AXIS = {"w_in": 2, "w_out": 1, "w_qkv": 2, "w_so": 1, "w_pool": 1, "w_up": 2, "w_down": 1}
BIG_ORDER = ("w_in", "w_out", "w_qkv", "w_so", "w_pool", "w_up", "w_down")

GATHER_HOSTS = {
    "hg_in_0": (("w_out", 0),),
    "hgrn_fwd_0": (("w_up", 0),),
    "ffn_up_0": (("w_down", 0),),
    "ffn_down_0": (("w_qkv", 0),),
    "sb_qkv_1": (("w_so", 0),),
    "sba_fwd_1": (("w_up", 1), ("w_down", 1), ("w_pool", 0)),
    "ffn_up_1": (("w_up", 2),),
    "ffn_down_1": (("w_down", 2),),
    "ffn_up_2": (("w_in", 1),),
    "ffn_down_2": (("w_out", 1),),
    "hgrn_fwd_3": (("w_up", 3),),
    "ffn_up_3": (("w_down", 3),),
}


MEGA = 1 << 20
EXCHANGE_BUDGET = {"d_u2": 13 * MEGA, "dw_up": 17 * MEGA, "hgrn_bwd": 30 * MEGA, "sba_bwd": 70 * MEGA}


class LocalWeights:
    def __init__(self, full):
        self.full, self.grads = full, {}

    def gather_plan(self, host):
        return None

    def gathered(self, outs):
        pass

    def weight(self, kind, l):
        return self.full[kind][l] if kind != "w_pool" else self.full[kind]

    def grad(self, kind, l, g):
        self.grads[(kind, l)] = g

    def exchange_plan(self, host=None):
        return None

    def exchanged(self, outs):
        pass


class MeshWeights:
    def __init__(self, shards16, w, m, v, core):
        self.shards, self.w, self.m, self.v, self.core = shards16, w, m, v, core
        self.full, self.pending, self.flying, self.keys = {}, [], [], None
        self.out = {k: [lax.empty(w[k].shape, F32) for _ in range(4)] for k in BIG_ORDER}

    def _items(self, keys):
        return [(self.shards[k], 0 if k == "w_pool" else l, 4 if k == "w_pool" else 1, BIG_AXIS[k]) for k, l in keys]

    def gather_plan(self, host):
        self.keys = [key for key in GATHER_HOSTS.get(host, ()) if key[0] in self.shards
                     and key[1] < (1 if key[0] == "w_pool" else self.shards[key[0]].shape[0])]
        return GatherPlan(self._items(self.keys)) if self.keys else None

    def gathered(self, outs):
        for (k, l), o in zip(self.keys, outs):
            self.full[(k, l)] = o if k == "w_pool" else o[0]
        self.keys = None

    def weight(self, kind, l):
        if (kind, l) not in self.full:
            (out,) = run_plan(GatherPlan(self._items([(kind, l)])), f"gather_{kind}_{l}")
            self.full[(kind, l)] = out if kind == "w_pool" else out[0]
        return self.full[(kind, l)]

    def grad(self, kind, l, g):
        self.pending.append((kind, l, g if g.ndim == 3 else g[None]))

    def exchange_plan(self, host=None):
        budget = next((b for prefix, b in EXCHANGE_BUDGET.items() if host.startswith(prefix)), 0) if host else None
        take, keep, used = [], [], 0
        for item in self.pending:
            if budget is None or used + item[2].size <= budget:
                take.append(item)
                used += item[2].size
            else:
                keep.append(item)
        self.pending = keep
        if not take:
            return None
        self.flying = take
        grads = [g for _, _, g in take]
        axes = [BIG_AXIS[k] for k, _, _ in take]
        recv = run_plan(SiblingPlan(grads, axes), "sibling_exchange")
        return ExchangePlan([add_sibling(g, r, ax, self.core) for g, r, ax in zip(grads, recv, axes)])

    def exchanged(self, outs):
        for (k, l, _), r in zip(self.flying, outs):
            self.out[k] = adam_from_partials(r, self.w[k], self.m[k], self.v[k], l, self.out[k])
        self.flying = []

    def finish(self):
        plan = self.exchange_plan()
        if plan is not None:
            self.exchanged(run_plan(plan, "chip_exchange_tail"))
        return self.out


def train_step(x, target, norm_g, lb_logits, onorm_g, pool_scale, conv_w, conv_b, wts):
    t, d = x.shape
    depth = norm_g.shape[0]
    ng = lambda i, j: norm_g[i, j].reshape(1, d)
    lbs = lower_bounds_fwd(lb_logits)
    saved = []
    h = x
    _, u16, u32 = res_norm(h, None, None, ng(0, 0), "norm_in")

    def hosted(call, *args, **kw):
        plan = wts.gather_plan(kw["name"])
        out, extra = call(*args, plan=plan, **kw)
        if plan is not None:
            wts.gathered(extra)
        return out

    def project(a, kind, l, name):
        plan = wts.gather_plan(name)
        if plan is None:
            return matmul(a, wts.weight(kind, l), "nn", F32, name)
        out, extra = matmul(a, wts.weight(kind, l), "nn", F32, name, plan=plan)
        wts.gathered(extra)
        return out

    for i in range(depth):
        kind, j = i % 3, i // 3
        s = {"h_in": h, "u1": u16}
        if kind == 0:
            s["proj"] = project(u16, "w_in", j, f"hg_in_{i}")
            s["y"] = hosted(hgrn_fwd, s["proj"], lbs[i].reshape(1, -1), onorm_g[j].reshape(1, -1), i > 0,
                            name=f"hgrn_fwd_{i}")
            mix = project(s["y"], "w_out", j, f"hg_out_{i}")
        elif kind == 1:
            s["qkv"] = project(u16, "w_qkv", j, f"sb_qkv_{i}")
            s["o"] = hosted(sba_fwd, s["qkv"], name=f"sba_fwd_{i}")
            mix = project(s["o"], "w_so", j, f"sb_out_{i}")
        else:
            s["u1f"] = u32
            mix = pool_fwd(u32, wts.weight("w_pool", j), pool_scale[j].reshape(1, d), f"pool_fwd_{i}")
        s["mix"] = mix
        h_mid, u2, _ = res_norm(h, mix, ng(i, 1), ng(i, 2), f"norm_mid_{i}")
        s["h_mid"], s["u2"] = h_mid, u2
        s["up"] = project(u2, "w_up", i, f"ffn_up_{i}")
        s["act"] = conv_glu_fwd(s["up"], conv_w[i], conv_b[i].reshape(1, -1), f"glu_fwd_{i}")
        s["f"] = project(s["act"], "w_down", i, f"ffn_down_{i}")
        nxt = ng(i + 1, 0) if i + 1 < depth else None
        h, u16, u32 = res_norm(h_mid, s["f"], ng(i, 3), nxt, f"norm_out_{i}",
                               want_f32=(nxt is not None and (i + 1) % 3 == 2))
        saved.append(s)

    loss_acc, dh = loss_head(h, target, "loss_head")

    d_norm = [[None] * 4 for _ in range(depth)]
    d_lbs = jnp.zeros_like(lbs)
    d_onorm = [None] * onorm_g.shape[0]
    d_pscale = [None] * pool_scale.shape[0]
    d_cw, d_cb = [None] * depth, [None] * depth

    def exchanging(call, *args, **kw):
        plan = wts.exchange_plan(kw["name"])
        out, extra = call(*args, plan=plan, **kw)
        if plan is not None:
            wts.exchanged(extra)
        return out

    def bmm(a, b, mode, dtype, name):
        plan = wts.exchange_plan(name)
        if plan is None:
            return matmul(a, b, mode, dtype, name)
        out, extra = matmul(a, b, mode, dtype, name, plan=plan)
        wts.exchanged(extra)
        return out

    for i in reversed(range(depth)):
        kind, j = i % 3, i // 3
        s = saved[i]
        df, d_norm[i][3] = norm_bwd(s["f"], ng(i, 3), dh, None, BF16, f"nb_out_{i}")
        dact = bmm(df, wts.weight("w_down", i), "nt", F32, f"d_act_{i}")
        wts.grad("w_down", i, bmm(s["act"], df, "tn", BF16, f"dw_down_{i}"))
        dup, dcw, dcb = conv_glu_bwd(s["up"], conv_w[i], conv_b[i].reshape(1, -1), dact, f"glu_bwd_{i}")
        d_cw[i] = jnp.moveaxis(dcw, 0, 1).reshape(CONV_WIDTH, -1)
        d_cb[i] = dcb.reshape(-1)
        du2 = bmm(Split(dup), wts.weight("w_up", i), "nt", F32, f"d_u2_{i}")
        wts.grad("w_up", i, bmm(s["u2"], Split(dup), "tn", BF16, f"dw_up_{i}"))
        dh_mid, d_norm[i][2] = norm_bwd(s["h_mid"], ng(i, 2), du2, dh, F32, f"nb_mid_{i}")
        dm, d_norm[i][1] = norm_bwd(s["mix"], ng(i, 1), dh_mid, None, F32 if kind == 2 else BF16, f"nb_mix_{i}")
        if kind == 0:
            dy = matmul(dm, wts.weight("w_out", j), "nt", F32, f"d_y_{i}")
            wts.grad("w_out", j, matmul(s["y"], dm, "tn", BF16, f"dw_hgout_{i}"))
            dproj, d_onorm[j], dlb = exchanging(hgrn_bwd, s["proj"], lbs[i].reshape(1, -1),
                                                onorm_g[j].reshape(1, -1), dy, i > 0, name=f"hgrn_bwd_{i}")
            d_lbs = d_lbs.at[i].set(dlb[0])
            du1 = bmm(Split(dproj), wts.weight("w_in", j), "nt", F32, f"d_u1_{i}")
            wts.grad("w_in", j, bmm(s["u1"], Split(dproj), "tn", BF16, f"dw_hgin_{i}"))
        elif kind == 1:
            do = matmul(dm, wts.weight("w_so", j), "nt", F32, f"d_o_{i}")
            wts.grad("w_so", j, matmul(s["o"], dm, "tn", BF16, f"dw_sbout_{i}"))
            dqkv = exchanging(sba_bwd, s["qkv"], s["o"], do, name=f"sba_bwd_{i}")
            du1 = bmm(Split(dqkv), wts.weight("w_qkv", j), "nt", F32, f"d_u1_{i}")
            wts.grad("w_qkv", j, bmm(s["u1"], Split(dqkv), "tn", BF16, f"dw_sbqkv_{i}"))
        else:
            du1, g_pool, d_pscale[j] = pool_bwd(s["u1f"], wts.weight("w_pool", j), pool_scale[j].reshape(1, d),
                                                dm, f"pool_bwd_{i}")
            wts.grad("w_pool", j, g_pool)
        dh, d_norm[i][0] = norm_bwd(s["h_in"], ng(i, 0), du1, dh_mid, F32, f"nb_in_{i}")

    small = {
        "norm_g": jnp.stack([jnp.stack([v.reshape(d) for v in row]) for row in d_norm]),
        "lb_logits": lower_bounds_bwd(lb_logits, d_lbs),
        "onorm_g": jnp.stack([v.reshape(-1) for v in d_onorm]),
        "pool_scale": jnp.stack([v.reshape(-1) for v in d_pscale]),
        "conv_w": jnp.stack(d_cw),
        "conv_b": jnp.stack(d_cb),
    }
    return loss_acc, dh, small


SMALL_SHARDED = ("norm_g", "onorm_g", "pool_scale", "conv_w")
SMALL_ORDER = ("norm_g", "lb_logits", "onorm_g", "pool_scale", "conv_w", "conv_b")


def kernel(x, norm_g, hgrn_lb_logits, hgrn_w_in, hgrn_onorm_g, hgrn_w_out, sba_w_qkv, sba_w_out, pool_w, pool_scale, ffn_w_up, ffn_conv_w, ffn_conv_b, ffn_w_down, loss_target, m_norm_g, m_hgrn_lb_logits, m_hgrn_w_in, m_hgrn_onorm_g, m_hgrn_w_out, m_sba_w_qkv, m_sba_w_out, m_pool_w, m_pool_scale, m_ffn_w_up, m_ffn_conv_w, m_ffn_conv_b, m_ffn_w_down, v_norm_g, v_hgrn_lb_logits, v_hgrn_w_in, v_hgrn_onorm_g, v_hgrn_w_out, v_sba_w_qkv, v_sba_w_out, v_pool_w, v_pool_scale, v_ffn_w_up, v_ffn_conv_w, v_ffn_conv_b, v_ffn_w_down):
    cx, cy, cc = lax.axis_index("x"), lax.axis_index("y"), lax.axis_index("c")
    block = 4 * cx + 2 * cy + cc
    core = cc.astype(jnp.int32).reshape(1)

    pool3 = lambda a: a.reshape(a.shape[0] * a.shape[1], a.shape[2], a.shape[3])
    big_w = dict(zip(BIG_ORDER, [hgrn_w_in, hgrn_w_out, sba_w_qkv, sba_w_out, pool3(pool_w), ffn_w_up, ffn_w_down]))
    big_m = dict(zip(BIG_ORDER, [m_hgrn_w_in, m_hgrn_w_out, m_sba_w_qkv, m_sba_w_out, pool3(m_pool_w), m_ffn_w_up,
                                 m_ffn_w_down]))
    big_v = dict(zip(BIG_ORDER, [v_hgrn_w_in, v_hgrn_w_out, v_sba_w_qkv, v_sba_w_out, pool3(v_pool_w), v_ffn_w_up,
                                 v_ffn_w_down]))

    sharded = {"norm_g": norm_g, "onorm_g": hgrn_onorm_g, "pool_scale": pool_scale, "conv_w": ffn_conv_w}
    gathered = small_exchange(_pack([sharded[n] for n in SMALL_SHARDED]), False, "gather_small")
    parts = _unpack(gathered, [sharded[n].shape for n in SMALL_SHARDED])
    full = {n: _unshard_last(p) for n, p in zip(SMALL_SHARDED, parts)}

    wts = MeshWeights({k: w.astype(BF16) for k, w in big_w.items()}, big_w, big_m, big_v, core)
    loss_acc, grad_x, small_g = train_step(
        x[0], loss_target[0], full["norm_g"], hgrn_lb_logits, full["onorm_g"], full["pool_scale"],
        full["conv_w"], ffn_conv_b, wts)
    loss = lax.psum(loss_acc[0, 0], ("x", "y", "c"))

    shapes = [small_g[n].shape for n in SMALL_ORDER]
    summed = _unpack(small_exchange(_pack([small_g[n] for n in SMALL_ORDER]), True, "reduce_small"), shapes)
    sg = {n: (_shard_last(g, block) if n in SMALL_SHARDED else g) for n, g in zip(SMALL_ORDER, summed)}
    sw = {"norm_g": norm_g, "lb_logits": hgrn_lb_logits, "onorm_g": hgrn_onorm_g, "pool_scale": pool_scale,
          "conv_w": ffn_conv_w, "conv_b": ffn_conv_b}
    sm = {"norm_g": m_norm_g, "lb_logits": m_hgrn_lb_logits, "onorm_g": m_hgrn_onorm_g, "pool_scale": m_pool_scale,
          "conv_w": m_ffn_conv_w, "conv_b": m_ffn_conv_b}
    sv = {"norm_g": v_norm_g, "lb_logits": v_hgrn_lb_logits, "onorm_g": v_hgrn_onorm_g, "pool_scale": v_pool_scale,
          "conv_w": v_ffn_conv_w, "conv_b": v_ffn_conv_b}
    sshapes = [sw[n].shape for n in SMALL_ORDER]
    packed = [_pack([dct[n] for n in SMALL_ORDER]) for dct in (sw, sg, sm, sv)]
    s_delta, s_m, s_v = [dict(zip(SMALL_ORDER, _unpack(p, sshapes))) for p in adam_small(*packed)]

    upd = wts.finish()
    b_grad, b_delta, b_m, b_v = [[upd[k][n] for k in BIG_ORDER] for n in range(4)]

    def tree(small, bigs):
        bg = list(bigs)
        bg[4] = bg[4].reshape(pool_w.shape)
        return (small["norm_g"], small["lb_logits"], bg[0], small["onorm_g"], bg[1], bg[2], bg[3], bg[4],
                small["pool_scale"], bg[5], small["conv_w"], small["conv_b"], bg[6])

    return (loss, grad_x[None], *tree(sg, b_grad), *tree(s_delta, b_delta), *tree(s_m, b_m), *tree(s_v, b_v))
```

```python
import functools
import math

import jax
import jax.numpy as jnp
from jax import lax
from jax.experimental import pallas as pl
from jax.experimental.pallas import tpu as pltpu

F32 = jnp.float32
BF16 = jnp.bfloat16
HI = lax.Precision.HIGHEST
MESH = pl.DeviceIdType.MESH
ANY = pl.BlockSpec(memory_space=pl.ANY)
VMEM_FULL = pl.BlockSpec(memory_space=pltpu.VMEM)

NORM_EPS = 1e-6
HEAD = 128
HG_CHUNK = 128
HG_SUB = 32
HG_MAX_EXPONENT = 80.0
SB_BLOCK = 256
SB_QROWS = 256
POOL_WINDOWS = (2, 4, 8, 16)
CONV_WIDTH = 3
ROW_TILE = 256
N_DEV = 8
N_CHIP = 4

ADAM_LR = 0.001
ADAM_B1 = 0.9
ADAM_B2 = 0.999
ADAM_EPS = 1e-08
ADAM_WD = 0.01
ADAM_STEP = 10

VMEM_LIMIT = 48 * 1024 * 1024
LANES = 128
ELEMWISE_BLOCK_ELEMS = 256 * 1024


def _params(sem=None, vmem=VMEM_LIMIT):
    return pltpu.CompilerParams(dimension_semantics=sem, vmem_limit_bytes=vmem)


def _tile(n, prefs=(1024, 512, 256, 128)):
    for p in prefs:
        if n % p == 0:
            return p
    return n


def _dot(a, b, prec=None):
    return jnp.dot(a, b, precision=prec, preferred_element_type=F32)


def _dot_nt(a, b, prec=None):
    return lax.dot_general(a, b, (((1,), (1,)), ((), ())), precision=prec, preferred_element_type=F32)


def _dot_tn(a, b, prec=None):
    return lax.dot_general(a, b, (((0,), (0,)), ((), ())), precision=prec, preferred_element_type=F32)


def _split_dot(x, tri, parts, left=False):
    tot, rest = None, x
    for p in range(parts):
        h = rest.astype(BF16)
        d = _dot(tri, h) if left else _dot(h, tri)
        tot = d if tot is None else tot + d
        if p + 1 < parts:
            rest = rest - h.astype(F32)
    return tot


def _dot1(a, b, fn):
    return fn(a.astype(BF16), b.astype(BF16))


def _dot3(a, b, fn):
    ah, bh = a.astype(BF16), b.astype(BF16)
    al, bl = (a - ah.astype(F32)).astype(BF16), (b - bh.astype(F32)).astype(BF16)
    return fn(ah, bh) + fn(ah, bl) + fn(al, bh)


def _sigmoid(x):
    return jax.nn.sigmoid(x)


def _softplus(x):
    return jnp.maximum(x, 0.0) + jnp.log1p(jnp.exp(-jnp.abs(x)))


class Split:
    def __init__(self, arr):
        self.arr = arr
        self.shape = (arr.shape[1], arr.shape[0] * arr.shape[2])
        self.part = arr.shape[2]


class Plain:
    def __init__(self, arr):
        self.arr = arr
        self.shape = arr.shape
        self.part = None


def _wrap(op):
    return op if isinstance(op, (Split, Plain)) else Plain(op)


def _op_spec(op, br, bc, rc_of_grid):
    if isinstance(op, Split):
        per = op.part // bc

        def imap(i, j, k):
            r, c = rc_of_grid(i, j, k)
            return (lax.div(c, per), r, lax.rem(c, per))
        return pl.BlockSpec((None, br, bc), imap)
    return pl.BlockSpec((br, bc), rc_of_grid)


def _hosted(body, n_in, n_out, plan, step_of_grid):
    if plan is None:
        return body
    pi, po, ps = len(plan.args), len(plan.out_shape), len(plan.scratch)

    def wrapped(*refs):
        refs = list(refs)
        ins, pins = refs[:n_in], refs[n_in:n_in + pi]
        outs = refs[n_in + pi:n_in + pi + n_out]
        pouts = refs[n_in + pi + n_out:n_in + pi + n_out + po]
        scr, pscr = refs[n_in + pi + n_out + po:len(refs) - ps], refs[len(refs) - ps:]
        step, nsteps = step_of_grid()

        @pl.when(step == 0)
        def _():
            plan.start(pins, pouts, pscr)

        if plan.has_mid:
            @pl.when(step == min((3 * nsteps) // 4, nsteps - 1))
            def _():
                plan.mid(pins, pouts, pscr)

        body(*ins, *outs, *scr)

        @pl.when(step == nsteps - 1)
        def _():
            plan.finish(pins, pouts, pscr)

    return wrapped


def _host_call(body, n_in, plan, step_of_grid, *, out_shape, grid, in_specs, out_specs, scratch_shapes, name, sem, args):
    single = not isinstance(out_shape, (list, tuple))
    out_shape = [out_shape] if single else list(out_shape)
    out_specs = [out_specs] if single else list(out_specs)
    n_out = len(out_shape)
    in_specs, scratch_shapes, args = list(in_specs), list(scratch_shapes), list(args)
    if plan is not None:
        in_specs += [ANY] * len(plan.args)
        args += plan.args
        out_shape += plan.out_shape
        out_specs += [ANY] * len(plan.out_shape)
        scratch_shapes += plan.scratch
        sem = ("arbitrary",) * len(grid)
    outs = pl.pallas_call(
        _hosted(body, n_in, n_out, plan, step_of_grid), out_shape=out_shape, grid=grid,
        in_specs=in_specs, out_specs=out_specs, scratch_shapes=scratch_shapes,
        name=name, compiler_params=_params(sem),
    )(*args)
    host = outs[0] if single else list(outs[:n_out])
    return host, list(outs[n_out:])


def matmul(a, b, mode, out_dtype, name, plan=None):
    a, b = _wrap(a), _wrap(b)
    if mode == "nn":
        (m, kd), (kd2, n) = a.shape, b.shape
    elif mode == "nt":
        (m, kd), (n, kd2) = a.shape, b.shape
    else:
        (kd, m), (kd2, n) = a.shape, b.shape
    assert kd == kd2, (mode, a.shape, b.shape)

    def dim_tile(full, ops_on_cols, prefs=(1024, 512, 256, 128)):
        base = full
        for op in ops_on_cols:
            if op.part is not None:
                base = math.gcd(base, op.part)
        return _tile(base, prefs)

    tm = dim_tile(m, [a] if mode == "tn" else [])
    tn = dim_tile(n, [b] if mode in ("nn", "tn") else [])
    tk = dim_tile(kd, ([a] if mode in ("nn", "nt") else []) + ([b] if mode == "nt" else []),
                  prefs=(2048, 2816, 1024, 512, 256, 128))
    nk = kd // tk

    if mode == "nn":
        a_spec = _op_spec(a, tm, tk, lambda i, j, k: (i, k))
        b_spec = _op_spec(b, tk, tn, lambda i, j, k: (k, j))
        dot = _dot
    elif mode == "nt":
        a_spec = _op_spec(a, tm, tk, lambda i, j, k: (i, k))
        b_spec = _op_spec(b, tn, tk, lambda i, j, k: (j, k))
        dot = _dot_nt
    else:
        a_spec = _op_spec(a, tk, tm, lambda i, j, k: (k, i))
        b_spec = _op_spec(b, tk, tn, lambda i, j, k: (k, j))
        dot = _dot_tn

    def body(a_ref, b_ref, o_ref, *acc):
        part = dot(a_ref[...].astype(BF16), b_ref[...].astype(BF16))
        if nk == 1:
            o_ref[...] = part.astype(o_ref.dtype)
            return
        (acc_ref,) = acc
        k = pl.program_id(2)

        @pl.when(k == 0)
        def _():
            acc_ref[...] = part

        @pl.when(k > 0)
        def _():
            acc_ref[...] += part

        @pl.when(k == nk - 1)
        def _():
            o_ref[...] = acc_ref[...].astype(o_ref.dtype)

    gi, gj = m // tm, n // tn

    def step_of_grid():
        return (pl.program_id(0) * gj + pl.program_id(1)) * nk + pl.program_id(2), gi * gj * nk

    out, extra = _host_call(
        body, 2, plan, step_of_grid, out_shape=jax.ShapeDtypeStruct((m, n), out_dtype), grid=(gi, gj, nk),
        in_specs=[a_spec, b_spec], out_specs=pl.BlockSpec((tm, tn), lambda i, j, k: (i, j)),
        scratch_shapes=[pltpu.VMEM((tm, tn), F32)] if nk > 1 else [], name=name,
        sem=("parallel", "parallel", "arbitrary"), args=[a.arr, b.arr])
    return out if plan is None else (out, extra)


def _rms(x, g):
    r = lax.rsqrt(jnp.mean(x * x, axis=-1, keepdims=True) + NORM_EPS)
    return x * r * g


def res_norm(h, m, g_a, g_b, name, want_f32=False):
    t, d = h.shape
    tr = min(ROW_TILE, t)
    has_m, has_u = m is not None, g_b is not None
    row = pl.BlockSpec((tr, d), lambda i: (i, 0))
    vec = pl.BlockSpec((1, d), lambda i: (0, 0))

    def body(*refs):
        refs = list(refs)
        h_ref = refs.pop(0)
        hn = h_ref[...]
        if has_m:
            m_ref, ga_ref = refs.pop(0), refs.pop(0)
            hn = hn + _rms(m_ref[...], ga_ref[...])
        if has_u:
            gb_ref = refs.pop(0)
        if has_m:
            refs.pop(0)[...] = hn
        if has_u:
            u = _rms(hn, gb_ref[...])
            refs.pop(0)[...] = u.astype(BF16)
            if want_f32:
                refs.pop(0)[...] = u

    args, in_specs, out_shape, out_specs = [h], [row], [], []
    if has_m:
        args += [m, g_a]
        in_specs += [row, vec]
        out_shape.append(jax.ShapeDtypeStruct((t, d), F32))
        out_specs.append(row)
    if has_u:
        args.append(g_b)
        in_specs.append(vec)
        out_shape.append(jax.ShapeDtypeStruct((t, d), BF16))
        out_specs.append(row)
        if want_f32:
            out_shape.append(jax.ShapeDtypeStruct((t, d), F32))
            out_specs.append(row)
    outs = list(pl.pallas_call(
        body, out_shape=out_shape, grid=(t // tr,), in_specs=in_specs, out_specs=out_specs,
        name=name, compiler_params=_params(("parallel",)),
    )(*args))
    h_new = outs.pop(0) if has_m else None
    u16 = outs.pop(0) if has_u else None
    u32 = outs.pop(0) if (has_u and want_f32) else None
    return h_new, u16, u32


def norm_bwd(x, g, dy, add, out_dtype, name):
    t, d = x.shape
    tr = min(ROW_TILE, t)
    has_add = add is not None
    row = pl.BlockSpec((tr, d), lambda i: (i, 0))
    vec = pl.BlockSpec((1, d), lambda i: (0, 0))

    def body(*refs):
        if has_add:
            x_ref, g_ref, dy_ref, add_ref, dx_ref, dg_ref = refs
        else:
            x_ref, g_ref, dy_ref, dx_ref, dg_ref = refs
        xv = x_ref[...]
        dyv = dy_ref[...].astype(F32)
        r = lax.rsqrt(jnp.mean(xv * xv, axis=-1, keepdims=True) + NORM_EPS)
        gy = dyv * g_ref[...]
        dx = r * gy - xv * (r * r * r * jnp.mean(gy * xv, axis=-1, keepdims=True))
        if has_add:
            dx = dx + add_ref[...]
        dx_ref[...] = dx.astype(dx_ref.dtype)

        @pl.when(pl.program_id(0) == 0)
        def _():
            dg_ref[...] = jnp.zeros_like(dg_ref)

        dg_ref[...] += jnp.sum(dyv * xv * r, axis=0, keepdims=True)

    args = [x, g, dy] + ([add] if has_add else [])
    in_specs = [row, vec, row] + ([row] if has_add else [])
    return pl.pallas_call(
        body, out_shape=[jax.ShapeDtypeStruct((t, d), out_dtype), jax.ShapeDtypeStruct((1, d), F32)],
        grid=(t // tr,), in_specs=in_specs, out_specs=[row, vec],
        name=name, compiler_params=_params(("arbitrary",)),
    )(*args)


def loss_head(y, target, name):
    t, d = y.shape
    tr = min(ROW_TILE, t)
    row = pl.BlockSpec((tr, d), lambda i: (i, 0))
    acc = pl.BlockSpec((8, LANES), lambda i: (0, 0))

    def body(y_ref, t_ref, loss_ref, dy_ref):
        e = y_ref[...] - t_ref[...]
        dy_ref[...] = e * (1.0 / d)

        @pl.when(pl.program_id(0) == 0)
        def _():
            loss_ref[...] = jnp.zeros_like(loss_ref)

        loss_ref[...] += jnp.sum(e * e) * (0.5 / d)

    return pl.pallas_call(
        body, out_shape=[jax.ShapeDtypeStruct((8, LANES), F32), jax.ShapeDtypeStruct((t, d), F32)],
        grid=(t // tr,), in_specs=[row, row], out_specs=[acc, row],
        name=name, compiler_params=_params(("arbitrary",)),
    )(y, target)


def _depth_softmax(ref, depth):
    rows = [ref[i:i + 1, :] for i in range(depth)]
    mx = functools.reduce(jnp.maximum, rows)
    ex = [jnp.exp(r - mx) for r in rows]
    tot = functools.reduce(lambda p, q: p + q, ex)
    return [e / tot for e in ex]


def lower_bounds_fwd(logits):
    depth, kw = logits.shape

    def body(l_ref, o_ref):
        s = _depth_softmax(l_ref, depth)
        run = jnp.zeros_like(s[0])
        o_ref[0:1, :] = run
        for i in range(1, depth):
            run = run + s[i]
            o_ref[i:i + 1, :] = run

    return pl.pallas_call(body, out_shape=jax.ShapeDtypeStruct((depth, kw), F32), name="lb_fwd")(logits)


def lower_bounds_bwd(logits, dlb):
    depth, kw = logits.shape

    def body(l_ref, d_ref, o_ref):
        s = _depth_softmax(l_ref, depth)
        ds = [jnp.zeros_like(s[0]) for _ in range(depth)]
        run = jnp.zeros_like(s[0])
        for j in range(depth - 1, 0, -1):
            run = run + d_ref[j:j + 1, :]
            ds[j] = run
        dot = functools.reduce(lambda p, q: p + q, [s[j] * ds[j] for j in range(depth)])
        for j in range(depth):
            o_ref[j:j + 1, :] = s[j] * (ds[j] - dot)

    return pl.pallas_call(body, out_shape=jax.ShapeDtypeStruct((depth, kw), F32), name="lb_bwd")(logits, dlb)


def _hg_gates(qp, fp, lb_row, has_lb):
    sig = _sigmoid(fp)
    nsig = _sigmoid(-fp)
    ls = jnp.minimum(fp, 0.0) - jnp.log1p(jnp.exp(-jnp.abs(fp)))
    if has_lb:
        a = jnp.log(lb_row)
        bb = jnp.log1p(-lb_row) + ls
        g = jnp.maximum(a, bb) + jnp.log1p(jnp.exp(-jnp.abs(a - bb)))
        w = jnp.exp(bb - g)
        k = (1.0 - lb_row) * nsig
    else:
        g, w, k = ls, None, nsig
    q = qp * _sigmoid(qp)
    return q, k, g, sig, nsig, w


HG_HEADS_PER_STEP = 2


def _heads_per_step(nh):
    return HG_HEADS_PER_STEP if nh % HG_HEADS_PER_STEP == 0 else 1


def _head_views(refs, hh, lanes, lead):
    cs = pl.ds(hh * HEAD, HEAD)
    out = []
    for i, r in enumerate(refs):
        if i in lead:
            out.append(r.at[hh])
        elif i in lanes:
            out.append(r.at[(slice(None),) * (len(r.shape) - 1) + (cs,)])
        else:
            out.append(r)
    return out


def _hg_masks():
    c = HG_CHUNK
    row = lax.broadcasted_iota(jnp.int32, (c, c), 0)
    col = lax.broadcasted_iota(jnp.int32, (c, c), 1)
    lower = (col <= row).astype(BF16)
    upper = (col >= row).astype(BF16)
    krow = lax.broadcasted_iota(jnp.int32, (c, HEAD), 0)
    arow = lax.broadcasted_iota(jnp.int32, (HG_SUB, c), 0)
    acol = lax.broadcasted_iota(jnp.int32, (HG_SUB, c), 1)
    return lower, upper, krow, arow, acol


def _hg_sub(i, q, k, b, b_ref, krow, arow, acol):
    r0 = i * HG_SUB
    m = b_ref[r0:r0 + 1, :]
    ebq = jnp.exp(b[r0:r0 + HG_SUB, :] - m)
    qh = (q[r0:r0 + HG_SUB, :] * ebq).astype(BF16)
    ek = jnp.exp(jnp.where(krow < r0 + HG_SUB, jnp.minimum(m - b, HG_MAX_EXPONENT), 0.0))
    kh = (k * ek).astype(BF16)
    mask = acol <= arow + r0
    amat = jnp.where(mask, _dot_nt(qh, kh), 0.0).astype(BF16)
    return ebq, qh, ek, kh, mask, amat


def hgrn_fwd(proj, lb, onorm_g, has_lb, name, plan=None):
    t, w4 = proj.shape
    kw = w4 // 4
    nh, nc, c = kw // HEAD, t // HG_CHUNK, HG_CHUNK
    hpb = _heads_per_step(nh)
    wide = hpb * HEAD

    def body(*refs):
        refs[7][...] = jnp.zeros_like(refs[7])
        heads = [one_head(*_head_views(refs, hh, lanes=(0, 1, 2, 3, 4, 5, 6), lead=(7, 8))) for hh in range(hpb)]

        def chunks(ci, carry):
            for chunk in heads:
                chunk(ci)
            return carry

        lax.fori_loop(0, nc, chunks, 0)

    def one_head(qp_ref, fp_ref, iv_ref, gp_ref, lb_ref, on_ref, y_ref, st_ref, b_ref):
        lower, _, krow, arow, acol = _hg_masks()
        lb_row, gam = lb_ref[...], on_ref[...]

        def chunk(ci):
            rs = pl.ds(pl.multiple_of(ci * c, c), c)
            v = iv_ref[rs, :].astype(BF16)
            gp = gp_ref[rs, :]
            q, k, g, _, _, _ = _hg_gates(qp_ref[rs, :], fp_ref[rs, :], lb_row, has_lb)
            b = _split_dot(g, lower, 3, left=True)
            b_ref[...] = b
            st = st_ref[...]
            o = _dot_nt((q * jnp.exp(b)).astype(BF16), st.astype(BF16))
            amats = [_hg_sub(i, q, k, b, b_ref, krow, arow, acol)[-1] for i in range(c // HG_SUB)]
            o = o + _dot(jnp.concatenate(amats, axis=0), v)
            bl = b_ref[c - 1:c, :]
            st_ref[...] = jnp.exp(bl) * st + _dot_tn(v, (k * jnp.exp(bl - b)).astype(BF16))
            r = lax.rsqrt(jnp.mean(o * o, axis=-1, keepdims=True) + NORM_EPS)
            y_ref[rs, :] = (o * r * gam * (gp * _sigmoid(gp))).astype(BF16)

        return chunk

    ns = nh // hpb
    col = lambda p: pl.BlockSpec((t, wide), lambda h: (0, p * ns + h))
    vec = pl.BlockSpec((1, wide), lambda h: (0, h))
    return _host_call(
        body, 6, plan, lambda: (pl.program_id(0), ns),
        out_shape=jax.ShapeDtypeStruct((t, kw), BF16), grid=(ns,),
        in_specs=[col(0), col(1), col(2), col(3), vec, vec],
        out_specs=pl.BlockSpec((t, wide), lambda h: (0, h)),
        scratch_shapes=[pltpu.VMEM((hpb, HEAD, HEAD), F32), pltpu.VMEM((hpb, c, HEAD), F32)],
        name=name, sem=("parallel",), args=[proj, proj, proj, proj, lb, onorm_g])


def hgrn_bwd(proj, lb, onorm_g, dy, has_lb, name, plan=None):
    t, w4 = proj.shape
    kw = w4 // 4
    nh, nc, c = kw // HEAD, t // HG_CHUNK, HG_CHUNK
    hpb = _heads_per_step(nh)
    wide = hpb * HEAD

    def body(*refs):
        for i in (8, 9, 13, 14):
            refs[i][...] = jnp.zeros_like(refs[i])
        heads = [one_head(*_head_views(refs, hh, lanes=(0, 1, 2, 3, 4, 5, 6, 7, 8, 9, 11), lead=(10, 12, 13, 14)))
                 for hh in range(hpb)]

        def fwd_chunks(ci, states):
            return tuple(fwd(ci, st) for (fwd, _), st in zip(heads, states))

        lax.fori_loop(0, nc, fwd_chunks, tuple(jnp.zeros((HEAD, HEAD), F32) for _ in heads))

        def bwd_chunks(step, carry):
            for _, bwd in heads:
                bwd(nc - 1 - step)
            return carry

        lax.fori_loop(0, nc, bwd_chunks, 0)

    def one_head(qp_ref, fp_ref, iv_ref, gp_ref, lb_ref, on_ref, dy_ref,
                 dp_ref, dgam_ref, dlb_ref, sst_ref, o_ref, b_ref, dst_ref, car_ref):
        lower, upper, krow, arow, acol = _hg_masks()
        lb_row, gam = lb_ref[...], on_ref[...]
        mm = _dot3 if has_lb else _dot1

        def recompute(ci):
            rs = pl.ds(pl.multiple_of(ci * c, c), c)
            gates = _hg_gates(qp_ref[rs, :], fp_ref[rs, :], lb_row, has_lb)
            b = _split_dot(gates[2], lower, 3, left=True)
            b_ref[...] = b
            return rs, gates, b

        def fwd_chunk(ci, carry):
            rs, (q, k, g, _, _, _), b = recompute(ci)
            v = iv_ref[rs, :].astype(BF16)
            st = carry
            sst_ref[ci] = st
            o = _dot_nt((q * jnp.exp(b)).astype(BF16), st.astype(BF16))
            amats = [_hg_sub(i, q, k, b, b_ref, krow, arow, acol)[-1] for i in range(c // HG_SUB)]
            o_ref[rs, :] = o + _dot(jnp.concatenate(amats, axis=0), v)
            bl = b_ref[c - 1:c, :]
            return jnp.exp(bl) * st + mm(iv_ref[rs, :], k * jnp.exp(bl - b), _dot_tn)

        def bwd_chunk(ci):
            rs, (q, k, g, sig, nsig, w), b = recompute(ci)
            qp, gp, v32 = qp_ref[rs, :], gp_ref[rs, :], iv_ref[rs, :]
            v = v32.astype(BF16)
            dyv, o = dy_ref[rs, :], o_ref[rs, :]
            st = sst_ref[ci]
            dst_new = dst_ref[...]
            r = lax.rsqrt(jnp.mean(o * o, axis=-1, keepdims=True) + NORM_EPS)
            on = o * r
            sgm = _sigmoid(gp)
            sg = gp * sgm
            dgam_ref[...] += jnp.sum(dyv * sg * on, axis=0, keepdims=True)
            dgp = dyv * on * gam * (sgm * (1.0 + gp * (1.0 - sgm)))
            dn = dyv * gam * sg
            do = r * dn - o * (r * r * r * jnp.mean(dn * o, axis=-1, keepdims=True))
            eb = jnp.exp(b)
            bl = b_ref[c - 1:c, :]
            ekd = jnp.exp(bl - b)
            do16 = do.astype(BF16)
            dq = mm(do, st, _dot) * eb
            dv = _dot1(k * ekd, dst_new, _dot_nt)
            dk = mm(v32, dst_new, _dot) * ekd
            dst_ref[...] = jnp.exp(bl) * dst_new + mm(do, q * eb, _dot_tn)
            da_all = mm(do, v32, _dot_nt)
            dq_parts, amats = [], []
            for i in range(c // HG_SUB):
                r0 = i * HG_SUB
                ebq, _, ek, _, mask, amat = _hg_sub(i, q, k, b, b_ref, krow, arow, acol)
                da = jnp.where(mask, da_all[r0:r0 + HG_SUB, :], 0.0)
                dq_parts.append(mm(da, k * ek, _dot) * ebq)
                dk = dk + mm(da, q[r0:r0 + HG_SUB, :] * ebq, _dot_tn) * ek
                amats.append(amat)
            dq = dq + jnp.concatenate(dq_parts, axis=0)
            dv = dv + _dot_tn(jnp.concatenate(amats, axis=0), do16)
            db = q * dq - k * dk
            dg = car_ref[...] + _split_dot(db, upper, 3, left=True)
            car_ref[...] += jnp.sum(db, axis=0, keepdims=True)
            if has_lb:
                dfp = dg * nsig * w - dk * ((1.0 - lb_row) * sig * nsig)
                dlb_ref[...] += jnp.sum(dg * nsig * jnp.exp(-g) - dk * nsig, axis=0, keepdims=True)
            else:
                dfp = dg * nsig - dk * (sig * nsig)
            sq = _sigmoid(qp)
            dp_ref[0, rs, :] = (dq * (sq * (1.0 + qp * (1.0 - sq)))).astype(BF16)
            dp_ref[1, rs, :] = dfp.astype(BF16)
            dp_ref[2, rs, :] = dv.astype(BF16)
            dp_ref[3, rs, :] = dgp.astype(BF16)

        return fwd_chunk, bwd_chunk

    ns = nh // hpb
    col = lambda p: pl.BlockSpec((t, wide), lambda h: (0, p * ns + h))
    vec = pl.BlockSpec((1, wide), lambda h: (0, h))
    return _host_call(
        body, 7, plan, lambda: (pl.program_id(0), ns),
        out_shape=[jax.ShapeDtypeStruct((4, t, kw), BF16), jax.ShapeDtypeStruct((1, kw), F32),
                   jax.ShapeDtypeStruct((1, kw), F32)],
        grid=(ns,),
        in_specs=[col(0), col(1), col(2), col(3), vec, vec, pl.BlockSpec((t, wide), lambda h: (0, h))],
        out_specs=[pl.BlockSpec((4, t, wide), lambda h: (0, 0, h)), vec, vec],
        scratch_shapes=[pltpu.VMEM((hpb, nc, HEAD, HEAD), F32), pltpu.VMEM((t, wide), F32),
                        pltpu.VMEM((hpb, c, HEAD), F32), pltpu.VMEM((hpb, HEAD, HEAD), F32),
                        pltpu.VMEM((hpb, 1, HEAD), F32)],
        name=name, sem=("parallel",), args=[proj, proj, proj, proj, lb, onorm_g, dy])


def _sb_masks():
    m, n = SB_QROWS, SB_BLOCK
    row = lax.broadcasted_iota(jnp.int32, (m, n), 0)
    col = lax.broadcasted_iota(jnp.int32, (m, n), 1)
    r2 = lax.broadcasted_iota(jnp.int32, (n, n), 0)
    c2 = lax.broadcasted_iota(jnp.int32, (n, n), 1)
    after = (r2 > c2).astype(BF16)
    from_ = (r2 >= c2).astype(BF16)
    return row, col, after, from_


def sba_fwd(qkv, name, plan=None):
    t, w3 = qkv.shape
    wd = w3 // 3
    m, n = SB_QROWS, SB_BLOCK
    nh, nq, per = wd // HEAD, t // m, m // n
    scale = HEAD ** -0.5

    def body(q_ref, k_ref, v_ref, o_ref, q16, k16, v16):
        row, col, after, _ = _sb_masks()
        q16[...] = (q_ref[...] * scale).astype(BF16)
        k16[...] = k_ref[...].astype(BF16)
        v16[...] = v_ref[...].astype(BF16)

        def qblock(qi, carry):
            qs = pl.ds(pl.multiple_of(qi * m, m), m)
            q = q16[qs, :]
            last = (qi + 1) * per - 1

            def kblock(step, state):
                acc, rem0 = state
                kj = last - step
                ks = pl.ds(pl.multiple_of(kj * n, n), n)
                z = _dot_nt(q, k16[ks, :])
                strict = (col + kj * n) < (row + qi * m)
                spz = _softplus(z)
                sp = jnp.where(strict, spz, 0.0)
                rem = rem0 + _split_dot(sp, after, 2)
                a = jnp.where(strict, jnp.exp(z - spz - rem), 0.0)
                acc = acc + _dot(a.astype(BF16), v16[ks, :])
                return acc, rem0 + jnp.sum(sp, axis=1, keepdims=True)

            acc, _ = lax.fori_loop(0, last + 1, kblock,
                                   (jnp.zeros((m, HEAD), F32), jnp.zeros((m, 1), F32)))
            o_ref[qs, :] = acc
            return carry

        lax.fori_loop(0, nq, qblock, 0)

    col_spec = lambda p: pl.BlockSpec((t, HEAD), lambda h: (0, p * nh + h))
    return _host_call(
        body, 3, plan, lambda: (pl.program_id(0), nh),
        out_shape=jax.ShapeDtypeStruct((t, wd), F32), grid=(nh,),
        in_specs=[col_spec(0), col_spec(1), col_spec(2)],
        out_specs=pl.BlockSpec((t, HEAD), lambda h: (0, h)),
        scratch_shapes=[pltpu.VMEM((t, HEAD), BF16)] * 3,
        name=name, sem=("parallel",), args=[qkv, qkv, qkv])


def sba_bwd(qkv, o, do, name, plan=None):
    t, w3 = qkv.shape
    wd = w3 // 3
    m, n = SB_QROWS, SB_BLOCK
    nh, nq, per = wd // HEAD, t // m, m // n
    scale = HEAD ** -0.5

    def body(q_ref, k_ref, v_ref, o_ref, do_ref, d_ref, dk_ref, dv_ref, q16, k16, v16):
        row, col, after, from_ = _sb_masks()
        dk_ref[...] = jnp.zeros_like(dk_ref)
        dv_ref[...] = jnp.zeros_like(dv_ref)
        q16[...] = (q_ref[...] * scale).astype(BF16)
        k16[...] = k_ref[...].astype(BF16)
        v16[...] = v_ref[...].astype(BF16)

        def qblock(qi, carry):
            qs = pl.ds(pl.multiple_of(qi * m, m), m)
            q = q16[qs, :]
            dov = do_ref[qs, :]
            do16 = dov.astype(BF16)
            dsum = jnp.sum(do16.astype(F32) * o_ref[qs, :], axis=1, keepdims=True)
            last = (qi + 1) * per - 1

            def kblock(step, state):
                dq, rem0, e0 = state
                kj = last - step
                ks = pl.ds(pl.multiple_of(kj * n, n), n)
                kv, vv = k16[ks, :], v16[ks, :]
                z = _dot_nt(q, kv)
                strict = (col + kj * n) < (row + qi * m)
                spz = _softplus(z)
                sp = jnp.where(strict, spz, 0.0)
                rem = rem0 + _split_dot(sp, after, 2)
                sgz = jnp.exp(z - spz)
                a = jnp.where(strict, sgz * jnp.exp(-rem), 0.0).astype(BF16)
                e = a.astype(F32) * _dot_nt(do16, vv)
                left = dsum - (e0 + _split_dot(e, from_, 2))
                dz = jnp.where(strict, e * (1.0 - sgz) - sgz * left, 0.0).astype(BF16)
                dq = dq + _dot(dz, kv)
                dk_ref[ks, :] += _dot_tn(dz, q)
                dv_ref[ks, :] += _dot_tn(a, do16)
                return (dq, rem0 + jnp.sum(sp, axis=1, keepdims=True),
                        e0 + jnp.sum(e, axis=1, keepdims=True))

            zero1 = jnp.zeros((m, 1), F32)
            dq, _, _ = lax.fori_loop(0, last + 1, kblock, (jnp.zeros((m, HEAD), F32), zero1, zero1))
            d_ref[0, qs, :] = (dq * scale).astype(BF16)
            return carry

        lax.fori_loop(0, nq, qblock, 0)
        d_ref[1, :, :] = dk_ref[...].astype(BF16)
        d_ref[2, :, :] = dv_ref[...].astype(BF16)

    col_spec = lambda p: pl.BlockSpec((t, HEAD), lambda h: (0, p * nh + h))
    head = pl.BlockSpec((t, HEAD), lambda h: (0, h))
    return _host_call(
        body, 5, plan, lambda: (pl.program_id(0), nh),
        out_shape=jax.ShapeDtypeStruct((3, t, wd), BF16), grid=(nh,),
        in_specs=[col_spec(0), col_spec(1), col_spec(2), head, head],
        out_specs=pl.BlockSpec((3, t, HEAD), lambda h: (0, 0, h)),
        scratch_shapes=[pltpu.VMEM((t, HEAD), F32)] * 2 + [pltpu.VMEM((t, HEAD), BF16)] * 3,
        name=name, sem=("parallel",), args=[qkv, qkv, qkv, o, do])


def _pool_band(i_out, i_in, tr, win, transpose):
    r = lax.broadcasted_iota(jnp.int32, (tr, tr), 0) + i_out * tr
    c = lax.broadcasted_iota(jnp.int32, (tr, tr), 1) + i_in * tr
    if transpose:
        return ((r <= c) & (r > c - win)).astype(F32)
    return ((c <= r) & (c > r - win)).astype(F32)


def _pool_p(u_ref, i, tr, win):
    cur = u_ref[i * tr:(i + 1) * tr, :]
    ws = _dot(_pool_band(i, i, tr, win, False), cur, HI)
    if i > 0:
        ws = ws + _dot(_pool_band(i, i - 1, tr, win, False), u_ref[(i - 1) * tr:i * tr, :], HI)
    pos = lax.broadcasted_iota(jnp.int32, (tr, 1), 0) + (i * tr + 1)
    count = jnp.minimum(pos, win).astype(F32)
    return ws / count - cur, count


def pool_fwd(u, pool_w, pool_scale, name):
    t, d = u.shape
    ng = len(POOL_WINDOWS)
    gs = d // ng
    tr = min(ROW_TILE, t)

    def body(u_ref, w_ref, s_ref, y_ref):
        win = jnp.left_shift(2, pl.program_id(0))
        for i in range(t // tr):
            p, _ = _pool_p(u_ref, i, tr, win)
            y_ref[i * tr:(i + 1) * tr, :] = _dot(p.astype(BF16), w_ref[...]) * s_ref[...]

    grp = pl.BlockSpec((t, gs), lambda g: (0, g))
    return pl.pallas_call(
        body, out_shape=jax.ShapeDtypeStruct((t, d), F32), grid=(ng,),
        in_specs=[grp, pl.BlockSpec((None, gs, gs), lambda g: (g, 0, 0)), pl.BlockSpec((1, gs), lambda g: (0, g))],
        out_specs=grp, name=name, compiler_params=_params(("parallel",)),
    )(u, pool_w, pool_scale)


def pool_bwd(u, pool_w, pool_scale, dy, name):
    t, d = u.shape
    ng = len(POOL_WINDOWS)
    gs = d // ng
    tr = min(ROW_TILE, t)
    nt = t // tr

    def body(u_ref, w_ref, s_ref, dy_ref, du_ref, dw_ref, ds_ref, dpc_ref, dp_ref):
        win = jnp.left_shift(2, pl.program_id(0))
        wv = w_ref[...]
        dw = jnp.zeros((gs, gs), F32)
        dsc = jnp.zeros((1, gs), F32)
        for i in range(nt):
            rows = slice(i * tr, (i + 1) * tr)
            p, count = _pool_p(u_ref, i, tr, win)
            p16 = p.astype(BF16)
            dyv = dy_ref[rows, :]
            dsc = dsc + jnp.sum(dyv * _dot(p16, wv), axis=0, keepdims=True)
            dyp = (dyv * s_ref[...]).astype(BF16)
            dw = dw + _dot_tn(p16, dyp)
            dp = _dot_nt(dyp, wv)
            dp_ref[rows, :] = dp
            dpc_ref[rows, :] = dp / count
        dw_ref[...] = dw.astype(BF16)
        ds_ref[...] = dsc
        for i in range(nt):
            rows = slice(i * tr, (i + 1) * tr)
            acc = _dot(_pool_band(i, i, tr, win, True), dpc_ref[rows, :], HI)
            if i + 1 < nt:
                acc = acc + _dot(_pool_band(i, i + 1, tr, win, True), dpc_ref[(i + 1) * tr:(i + 2) * tr, :], HI)
            du_ref[rows, :] = acc - dp_ref[rows, :]

    grp = pl.BlockSpec((t, gs), lambda g: (0, g))
    wspec = pl.BlockSpec((None, gs, gs), lambda g: (g, 0, 0))
    vec = pl.BlockSpec((1, gs), lambda g: (0, g))
    return pl.pallas_call(
        body,
        out_shape=[jax.ShapeDtypeStruct((t, d), F32), jax.ShapeDtypeStruct((ng, gs, gs), BF16),
                   jax.ShapeDtypeStruct((1, d), F32)],
        grid=(ng,), in_specs=[grp, wspec, vec, grp], out_specs=[grp, wspec, vec],
        scratch_shapes=[pltpu.VMEM((t, gs), F32), pltpu.VMEM((t, gs), F32)],
        name=name, compiler_params=_params(("parallel",)),
    )(u, pool_w, pool_scale, dy)


CONV_COLS = 256
HALO = 8


def _conv_taps(ref, r0, tr):
    x = ref[r0:r0 + tr, :]
    prev = ref[r0 - HALO:r0, :] if r0 > 0 else jnp.zeros((HALO, x.shape[1]), F32)
    xx = jnp.concatenate([prev, x], axis=0)
    return x, pltpu.roll(xx, 1, 0)[HALO:, :], pltpu.roll(xx, 2, 0)[HALO:, :]


def _conv_out(taps, w_ref, b_ref):
    x, s1, s2 = taps
    return w_ref[0:1, :] * s2 + w_ref[1:2, :] * s1 + w_ref[2:3, :] * x + b_ref[...]


def conv_glu_fwd(up, conv_w, conv_b, name):
    t, f2 = up.shape
    f = f2 // 2
    tc = min(CONV_COLS, f)
    nj = f // tc
    tr = min(ROW_TILE, t)

    def body(ug_ref, uv_ref, wg_ref, wv_ref, bg_ref, bv_ref, o_ref):
        for i in range(t // tr):
            r0 = i * tr
            gate = _conv_out(_conv_taps(ug_ref, r0, tr), wg_ref, bg_ref)
            val = _conv_out(_conv_taps(uv_ref, r0, tr), wv_ref, bv_ref)
            o_ref[r0:r0 + tr, :] = (gate * _sigmoid(gate) * val).astype(BF16)

    blk = lambda rows, half: pl.BlockSpec((rows, tc), lambda j: (0, half * nj + j))
    return pl.pallas_call(
        body, out_shape=jax.ShapeDtypeStruct((t, f), BF16), grid=(nj,),
        in_specs=[blk(t, 0), blk(t, 1), blk(CONV_WIDTH, 0), blk(CONV_WIDTH, 1), blk(1, 0), blk(1, 1)],
        out_specs=pl.BlockSpec((t, tc), lambda j: (0, j)),
        name=name, compiler_params=_params(("parallel",)),
    )(up, up, conv_w, conv_w, conv_b, conv_b)


def conv_glu_bwd(up, conv_w, conv_b, dact, name):
    t, f2 = up.shape
    f = f2 // 2
    tc = min(CONV_COLS, f)
    nj = f // tc
    tr = min(ROW_TILE, t)
    nt = t // tr

    def body(ug_ref, uv_ref, wg_ref, wv_ref, bg_ref, bv_ref, da_ref, du_ref, dw_ref, db_ref, dg_ref, dv_ref):
        dwg = [jnp.zeros((1, tc), F32) for _ in range(CONV_WIDTH)]
        dwv = [jnp.zeros((1, tc), F32) for _ in range(CONV_WIDTH)]
        dbg = jnp.zeros((1, tc), F32)
        dbv = jnp.zeros((1, tc), F32)
        for i in range(nt):
            r0 = i * tr
            tg, tv = _conv_taps(ug_ref, r0, tr), _conv_taps(uv_ref, r0, tr)
            gate, val = _conv_out(tg, wg_ref, bg_ref), _conv_out(tv, wv_ref, bv_ref)
            sg = _sigmoid(gate)
            da = da_ref[r0:r0 + tr, :]
            d_gate = da * val * (sg * (1.0 + gate * (1.0 - sg)))
            d_val = da * (gate * sg)
            dg_ref[r0:r0 + tr, :] = d_gate
            dv_ref[r0:r0 + tr, :] = d_val
            dbg = dbg + jnp.sum(d_gate, axis=0, keepdims=True)
            dbv = dbv + jnp.sum(d_val, axis=0, keepdims=True)
            for tap in range(CONV_WIDTH):
                dwg[tap] = dwg[tap] + jnp.sum(d_gate * tg[2 - tap], axis=0, keepdims=True)
                dwv[tap] = dwv[tap] + jnp.sum(d_val * tv[2 - tap], axis=0, keepdims=True)
        for tap in range(CONV_WIDTH):
            dw_ref[0, tap:tap + 1, :] = dwg[tap]
            dw_ref[1, tap:tap + 1, :] = dwv[tap]
        db_ref[0, :, :] = dbg
        db_ref[1, :, :] = dbv
        for half, (d_ref, w_ref) in enumerate(((dg_ref, wg_ref), (dv_ref, wv_ref))):
            for i in range(nt):
                r0 = i * tr
                x = d_ref[r0:r0 + tr, :]
                nxt = d_ref[r0 + tr:r0 + tr + HALO, :] if i + 1 < nt else jnp.zeros((HALO, tc), F32)
                xx = jnp.concatenate([x, nxt], axis=0)
                up1 = pltpu.roll(xx, tr + HALO - 1, 0)[:tr, :]
                up2 = pltpu.roll(xx, tr + HALO - 2, 0)[:tr, :]
                du = w_ref[2:3, :] * x + w_ref[1:2, :] * up1 + w_ref[0:1, :] * up2
                du_ref[half, r0:r0 + tr, :] = du.astype(BF16)

    blk = lambda rows, half: pl.BlockSpec((rows, tc), lambda j: (0, half * nj + j))
    return pl.pallas_call(
        body,
        out_shape=[jax.ShapeDtypeStruct((2, t, f), BF16), jax.ShapeDtypeStruct((2, CONV_WIDTH, f), F32),
                   jax.ShapeDtypeStruct((2, 1, f), F32)],
        grid=(nj,),
        in_specs=[blk(t, 0), blk(t, 1), blk(CONV_WIDTH, 0), blk(CONV_WIDTH, 1), blk(1, 0), blk(1, 1),
                  pl.BlockSpec((t, tc), lambda j: (0, j))],
        out_specs=[pl.BlockSpec((2, t, tc), lambda j: (0, 0, j)),
                   pl.BlockSpec((2, CONV_WIDTH, tc), lambda j: (0, 0, j)),
                   pl.BlockSpec((2, 1, tc), lambda j: (0, 0, j))],
        scratch_shapes=[pltpu.VMEM((t, tc), F32), pltpu.VMEM((t, tc), F32)],
        name=name, compiler_params=_params(("parallel",)),
    )(up, up, conv_w, conv_w, conv_b, conv_b, dact)


def _place():
    x, y, c = lax.axis_index("x"), lax.axis_index("y"), lax.axis_index("c")
    others = [(1 - x, y), (x, 1 - y), (1 - x, 1 - y)]
    return x, y, c, others


def _window(ref, axis, b, n):
    if axis == 1:
        return ref.at[:, pl.ds(b * n, n), :]
    return ref.at[:, :, pl.ds(b * n, n)]


def _remote(src, dst, send_sems, recv_sems, k, to):
    return pltpu.make_async_remote_copy(src_ref=src, dst_ref=dst, send_sem=send_sems.at[k],
                                        recv_sem=recv_sems.at[k], device_id=to, device_id_type=MESH)


class GatherPlan:
    has_mid = True

    def __init__(self, items):
        self.items = items
        self.args = [s for s, _, _, _ in items]
        self.out_shape = []
        for s, _, nl, ax in items:
            shp = [nl, s.shape[1], s.shape[2]]
            shp[ax] *= N_DEV
            self.out_shape.append(jax.ShapeDtypeStruct(tuple(shp), s.dtype))
        n = len(items)
        self.scratch = [pltpu.SemaphoreType.DMA((7 * n,)), pltpu.SemaphoreType.DMA((7 * n,)),
                        pltpu.SemaphoreType.DMA((n,))]

    def _mine(self, ins, a):
        _, l0, nl, _ = self.items[a]
        return ins[a].at[pl.ds(l0, nl)]

    def _copy(self, ins, outs, sems, a, k, block, to, own=False):
        s, _, _, ax = self.items[a]
        px, py, pc = block
        w = _window(outs[a], ax, 4 * px + 2 * py + pc, s.shape[ax])
        return _remote(self._mine(ins, a) if own else w, w, sems[0], sems[1], 7 * a + k, to)

    def _local(self, ins, outs, sems, a, x, y, c):
        s, _, _, ax = self.items[a]
        return pltpu.make_async_copy(self._mine(ins, a), _window(outs[a], ax, 4 * x + 2 * y + c, s.shape[ax]),
                                     sems[2].at[a])

    def _first(self, ins, outs, sems, a, x, y, c, others):
        me = (x, y, c)
        return [self._copy(ins, outs, sems, a, 0, me, (x, y, 1 - c), own=True)] + [
            self._copy(ins, outs, sems, a, 1 + j, me, (*chip, c), own=True) for j, chip in enumerate(others)]

    def start(self, ins, outs, sems):
        x, y, c, others = _place()
        for a in range(len(self.items)):
            self._local(ins, outs, sems, a, x, y, c).start()
        for a in range(len(self.items)):
            for cp in self._first(ins, outs, sems, a, x, y, c, others):
                cp.start()

    def mid(self, ins, outs, sems):
        x, y, c, others = _place()
        for a in range(len(self.items)):
            for j, chip in enumerate(others):
                self._copy(ins, outs, sems, a, 1 + j, (*chip, c), (x, y, c)).wait_recv()
                self._copy(ins, outs, sems, a, 4 + j, (*chip, c), (x, y, 1 - c)).start()

    def finish(self, ins, outs, sems):
        x, y, c, others = _place()
        for a in range(len(self.items)):
            self._copy(ins, outs, sems, a, 0, (x, y, 1 - c), (x, y, c)).wait_recv()
            for j, chip in enumerate(others):
                self._copy(ins, outs, sems, a, 4 + j, (*chip, 1 - c), (x, y, c)).wait_recv()
        for a in range(len(self.items)):
            for cp in self._first(ins, outs, sems, a, x, y, c, others):
                cp.wait_send()
            for j, chip in enumerate(others):
                self._copy(ins, outs, sems, a, 4 + j, (*chip, c), (x, y, 1 - c)).wait_send()
            self._local(ins, outs, sems, a, x, y, c).wait()


class ExchangePlan:
    has_mid = False

    def __init__(self, partials):
        self.args = list(partials)
        self.out_shape = [jax.ShapeDtypeStruct(p.shape, p.dtype) for p in partials]
        n = len(partials)
        self.scratch = [pltpu.SemaphoreType.DMA((3 * n,)), pltpu.SemaphoreType.DMA((3 * n,)),
                        pltpu.SemaphoreType.DMA((n,))]

    def _copies(self, ins, outs, sems):
        x, y, c, others = _place()
        me = 2 * x + y
        local, sends, recvs = [], [], []
        for a in range(len(self.args)):
            local.append(pltpu.make_async_copy(ins[a].at[me], outs[a].at[me], sems[2].at[a]))
            for j, (px, py) in enumerate(others):
                sends.append(_remote(ins[a].at[2 * px + py], outs[a].at[me], sems[0], sems[1], 3 * a + j, (px, py, c)))
                slot = outs[a].at[2 * px + py]
                recvs.append(_remote(slot, slot, sems[0], sems[1], 3 * a + j, (px, py, c)))
        return local, sends, recvs

    def start(self, ins, outs, sems):
        local, sends, _ = self._copies(ins, outs, sems)
        for cp in local + sends:
            cp.start()

    def finish(self, ins, outs, sems):
        local, sends, recvs = self._copies(ins, outs, sems)
        for cp in recvs:
            cp.wait_recv()
        for cp in sends:
            cp.wait_send()
        for cp in local:
            cp.wait()


def run_plan(plan, name):
    ni, no = len(plan.args), len(plan.out_shape)

    def body(*refs):
        ins, outs, sems = refs[:ni], refs[ni:ni + no], refs[ni + no:]
        plan.start(ins, outs, sems)
        if plan.has_mid:
            plan.mid(ins, outs, sems)
        plan.finish(ins, outs, sems)

    return pl.pallas_call(
        body, out_shape=plan.out_shape, in_specs=[ANY] * ni, out_specs=[ANY] * no,
        scratch_shapes=plan.scratch, name=name,
    )(*plan.args)


class SiblingPlan:
    has_mid = False

    def __init__(self, grads, axes):
        self.args, self.axes = list(grads), list(axes)
        self.widths = [g.shape[ax] // N_DEV for g, ax in zip(grads, axes)]
        self.out_shape = []
        for g, ax, n in zip(grads, axes, self.widths):
            shp = list(g.shape)
            shp[ax] = n
            self.out_shape.append(jax.ShapeDtypeStruct((N_CHIP, *shp), g.dtype))
        n = len(grads)
        self.scratch = [pltpu.SemaphoreType.DMA((N_CHIP * n,)), pltpu.SemaphoreType.DMA((N_CHIP * n,))]

    def _copies(self, ins, outs, sems):
        x, y, c, _ = _place()
        copies = []
        for a in range(len(self.args)):
            for q in range(N_CHIP):
                src = _window(ins[a], self.axes[a], 2 * q + (1 - c), self.widths[a])
                copies.append(_remote(src, outs[a].at[q], sems[0], sems[1], N_CHIP * a + q, (x, y, 1 - c)))
        return copies

    def start(self, ins, outs, sems):
        for cp in self._copies(ins, outs, sems):
            cp.start()

    def finish(self, ins, outs, sems):
        copies = self._copies(ins, outs, sems)
        for cp in copies:
            cp.wait_recv()
        for cp in copies:
            cp.wait_send()


def _peer_of(k, x, y, c):
    return (1 - x if k & 4 else x, 1 - y if k & 2 else y, 1 - c if k & 1 else c)


def small_exchange(vec, reduce, name):
    r = vec.shape[0]

    def body(v_ref, o_ref, *scratch):
        if reduce:
            buf, send_sems, recv_sems = scratch
        else:
            buf, (send_sems, recv_sems) = o_ref, scratch
        x, y, c, _ = _place()
        me = 4 * x + 2 * y + c
        copies = []
        for k in range(1, N_DEV):
            cp = _remote(v_ref, buf.at[me], send_sems, recv_sems, k - 1, _peer_of(k, x, y, c))
            cp.start()
            copies.append(cp)
        buf[me] = v_ref[...]
        for k in range(1, N_DEV):
            px, py, pc = _peer_of(k, x, y, c)
            slot = buf.at[4 * px + 2 * py + pc]
            _remote(slot, slot, send_sems, recv_sems, k - 1, (px, py, pc)).wait_recv()
        for cp in copies:
            cp.wait_send()
        if reduce:
            tot = buf[0]
            for b in range(1, N_DEV):
                tot = tot + buf[b]
            o_ref[...] = tot

    sems = [pltpu.SemaphoreType.DMA((N_DEV - 1,)), pltpu.SemaphoreType.DMA((N_DEV - 1,))]
    if reduce:
        out_shape = jax.ShapeDtypeStruct((r, LANES), F32)
        scratch = [pltpu.VMEM((N_DEV, r, LANES), F32)] + sems
    else:
        out_shape = jax.ShapeDtypeStruct((N_DEV, r, LANES), F32)
        scratch = sems
    return pl.pallas_call(
        body, out_shape=out_shape, in_specs=[VMEM_FULL], out_specs=VMEM_FULL,
        scratch_shapes=scratch, name=name, compiler_params=_params(None),
    )(vec)


def _row_tile(rows, cols, limit=ELEMWISE_BLOCK_ELEMS):
    best = None
    for tb in range(16, rows + 1, 16):
        if rows % tb == 0 and tb * cols <= limit:
            best = tb
    return best if best is not None else rows


def add_sibling(grad, recv, axis, core):
    nl = grad.shape[0]
    _, _, r, cc = recv.shape
    tb = _row_tile(r, cc, 4 * ELEMWISE_BLOCK_ELEMS)
    per = r // tb

    def body(c_ref, g_ref, r_ref, o_ref):
        del c_ref
        o_ref[...] = (g_ref[...].astype(F32) + r_ref[...].astype(F32)).astype(BF16)

    if axis == 2:
        g_spec = pl.BlockSpec((None, tb, cc), lambda q, l, i, c_ref: (l, i, 2 * q + c_ref[0]))
    else:
        g_spec = pl.BlockSpec((None, tb, cc), lambda q, l, i, c_ref: (l, (2 * q + c_ref[0]) * per + i, 0))
    slot = pl.BlockSpec((None, None, tb, cc), lambda q, l, i, c_ref: (q, l, i, 0))
    return pl.pallas_call(
        body, out_shape=jax.ShapeDtypeStruct(recv.shape, BF16),
        grid_spec=pltpu.PrefetchScalarGridSpec(
            num_scalar_prefetch=1, grid=(N_CHIP, nl, per), in_specs=[g_spec, slot], out_specs=slot),
        name="add_sibling", compiler_params=_params(("parallel", "parallel", "parallel")),
    )(core, grad, recv)


def _adamw(w, g, m, v):
    m = ADAM_B1 * m + (1.0 - ADAM_B1) * g
    v = ADAM_B2 * v + (1.0 - ADAM_B2) * (g * g)
    m_hat = m / (1.0 - ADAM_B1 ** ADAM_STEP)
    v_hat = v / (1.0 - ADAM_B2 ** ADAM_STEP)
    delta = -ADAM_LR * (m_hat / (jnp.sqrt(v_hat) + ADAM_EPS) + ADAM_WD * w)
    return delta, m, v


def adam_from_partials(recv, w, m, v, l0, bufs):
    nl = recv.shape[1]
    _, r, cc = w.shape
    tb = _row_tile(r, cc)

    def body(p0, p1, p2, p3, w_ref, m_ref, v_ref, b0, b1, b2, b3, g_out, d_out, m_out, v_out):
        del b0, b1, b2, b3
        g = p0[...].astype(F32) + p1[...].astype(F32) + p2[...].astype(F32) + p3[...].astype(F32)
        d, mn, vn = _adamw(w_ref[...], g, m_ref[...], v_ref[...])
        g_out[...], d_out[...], m_out[...], v_out[...] = g, d, mn, vn

    slot = lambda q: pl.BlockSpec((None, None, tb, cc), lambda l, i: (q, l, i, 0))
    blk = pl.BlockSpec((None, tb, cc), lambda l, i: (l0 + l, i, 0))
    shp = jax.ShapeDtypeStruct(w.shape, F32)
    return pl.pallas_call(
        body, out_shape=[shp] * 4, grid=(nl, r // tb),
        in_specs=[slot(0), slot(1), slot(2), slot(3), blk, blk, blk] + [ANY] * 4, out_specs=[blk] * 4,
        input_output_aliases={7: 0, 8: 1, 9: 2, 10: 3},
        name="adam_big", compiler_params=_params(("parallel", "parallel")),
    )(recv, recv, recv, recv, w, m, v, *bufs)


def adam_small(w, g, m, v):
    def body(w_ref, g_ref, m_ref, v_ref, d_out, m_out, v_out):
        d_out[...], m_out[...], v_out[...] = _adamw(w_ref[...], g_ref[...], m_ref[...], v_ref[...])

    shp = jax.ShapeDtypeStruct(w.shape, F32)
    return pl.pallas_call(body, out_shape=[shp] * 3, name="adam_small")(w, g, m, v)


def _pack(arrays, multiple=8 * LANES):
    flat = jnp.concatenate([a.reshape(-1) for a in arrays])
    pad = (-flat.shape[0]) % multiple
    if pad:
        flat = jnp.concatenate([flat, jnp.zeros((pad,), flat.dtype)])
    return flat.reshape(-1, LANES)


def _unpack(packed, shapes):
    flat = packed.reshape(packed.shape[:-2] + (-1,))
    out, off = [], 0
    for shp in shapes:
        n = math.prod(shp)
        out.append(flat[..., off:off + n].reshape(packed.shape[:-2] + tuple(shp)))
        off += n
    return out


def _unshard_last(stacked):
    moved = jnp.moveaxis(stacked, 0, -2)
    return moved.reshape(moved.shape[:-2] + (-1,))


def _shard_last(full, block):
    n = full.shape[-1] // N_DEV
    return lax.dynamic_slice_in_dim(full, block * n, n, axis=full.ndim - 1)


BIG_AXIS = {"w_in": 2, "w_out": 1, "w_qkv": 2, "w_so": 1, "w_pool": 1, "w_up": 2, "w_down": 1}
BIG_ORDER = ("w_in", "w_out", "w_qkv", "w_so", "w_pool", "w_up", "w_down")

GATHER_HOSTS = {
    "hg_in_0": (("w_out", 0),),
    "hgrn_fwd_0": (("w_up", 0),),
    "ffn_up_0": (("w_down", 0),),
    "ffn_down_0": (("w_qkv", 0),),
    "sb_qkv_1": (("w_so", 0),),
    "sba_fwd_1": (("w_up", 1), ("w_pool", 0), ("w_up", 2)),
    "ffn_up_1": (("w_down", 1),),
    "ffn_down_1": (("w_down", 2),),
    "ffn_up_2": (("w_in", 1),),
    "ffn_down_2": (("w_out", 1),),
    "hgrn_fwd_3": (("w_up", 3),),
    "ffn_up_3": (("w_down", 3),),
}


MEGA = 1 << 20
EXCHANGE_BUDGET = {"d_u2": 13 * MEGA, "dw_up": 17 * MEGA, "hgrn_bwd": 30 * MEGA, "sba_bwd": 70 * MEGA,
                   "d_u1_0": 17 * MEGA}


class LocalWeights:
    def __init__(self, full):
        self.full, self.grads = full, {}

    def gather_plan(self, host):
        return None

    def gathered(self, outs):
        pass

    def weight(self, kind, l):
        return self.full[kind][l] if kind != "w_pool" else self.full[kind]

    def grad(self, kind, l, g):
        self.grads[(kind, l)] = g

    def exchange_plan(self, host=None):
        return None

    def exchanged(self, outs):
        pass


class MeshWeights:
    def __init__(self, shards16, w, m, v, core):
        self.shards, self.w, self.m, self.v, self.core = shards16, w, m, v, core
        self.full, self.pending, self.flying, self.keys = {}, [], [], None
        self.out = {k: [lax.empty(w[k].shape, F32) for _ in range(4)] for k in BIG_ORDER}

    def _items(self, keys):
        return [(self.shards[k], 0 if k == "w_pool" else l, 4 if k == "w_pool" else 1, BIG_AXIS[k]) for k, l in keys]

    def gather_plan(self, host):
        self.keys = [key for key in GATHER_HOSTS.get(host, ()) if key[0] in self.shards
                     and key[1] < (1 if key[0] == "w_pool" else self.shards[key[0]].shape[0])]
        return GatherPlan(self._items(self.keys)) if self.keys else None

    def gathered(self, outs):
        for (k, l), o in zip(self.keys, outs):
            self.full[(k, l)] = o if k == "w_pool" else o[0]
        self.keys = None

    def weight(self, kind, l):
        if (kind, l) not in self.full:
            (out,) = run_plan(GatherPlan(self._items([(kind, l)])), f"gather_{kind}_{l}")
            self.full[(kind, l)] = out if kind == "w_pool" else out[0]
        return self.full[(kind, l)]

    def grad(self, kind, l, g):
        self.pending.append((kind, l, g if g.ndim == 3 else g[None]))

    def exchange_plan(self, host=None):
        budget = next((b for prefix, b in EXCHANGE_BUDGET.items() if host.startswith(prefix)), 0) if host else None
        take, keep, used = [], [], 0
        for item in self.pending:
            if budget is None or used + item[2].size <= budget:
                take.append(item)
                used += item[2].size
            else:
                keep.append(item)
        self.pending = keep
        if not take:
            return None
        self.flying = take
        grads = [g for _, _, g in take]
        axes = [BIG_AXIS[k] for k, _, _ in take]
        recv = run_plan(SiblingPlan(grads, axes), "sibling_exchange")
        return ExchangePlan([add_sibling(g, r, ax, self.core) for g, r, ax in zip(grads, recv, axes)])

    def exchanged(self, outs):
        for (k, l, _), r in zip(self.flying, outs):
            self.out[k] = adam_from_partials(r, self.w[k], self.m[k], self.v[k], l, self.out[k])
        self.flying = []

    def finish(self):
        plan = self.exchange_plan()
        if plan is not None:
            self.exchanged(run_plan(plan, "chip_exchange_tail"))
        return self.out


def train_step(x, target, norm_g, lb_logits, onorm_g, pool_scale, conv_w, conv_b, wts):
    t, d = x.shape
    depth = norm_g.shape[0]
    ng = lambda i, j: norm_g[i, j].reshape(1, d)
    lbs = lower_bounds_fwd(lb_logits)
    saved = []
    h = x
    _, u16, u32 = res_norm(h, None, None, ng(0, 0), "norm_in")

    def hosted(call, *args, **kw):
        plan = wts.gather_plan(kw["name"])
        out, extra = call(*args, plan=plan, **kw)
        if plan is not None:
            wts.gathered(extra)
        return out

    def project(a, kind, l, name):
        plan = wts.gather_plan(name)
        if plan is None:
            return matmul(a, wts.weight(kind, l), "nn", F32, name)
        out, extra = matmul(a, wts.weight(kind, l), "nn", F32, name, plan=plan)
        wts.gathered(extra)
        return out

    for i in range(depth):
        kind, j = i % 3, i // 3
        s = {"h_in": h, "u1": u16}
        if kind == 0:
            s["proj"] = project(u16, "w_in", j, f"hg_in_{i}")
            s["y"] = hosted(hgrn_fwd, s["proj"], lbs[i].reshape(1, -1), onorm_g[j].reshape(1, -1), i > 0,
                            name=f"hgrn_fwd_{i}")
            mix = project(s["y"], "w_out", j, f"hg_out_{i}")
        elif kind == 1:
            s["qkv"] = project(u16, "w_qkv", j, f"sb_qkv_{i}")
            s["o"] = hosted(sba_fwd, s["qkv"], name=f"sba_fwd_{i}")
            mix = project(s["o"], "w_so", j, f"sb_out_{i}")
        else:
            s["u1f"] = u32
            mix = pool_fwd(u32, wts.weight("w_pool", j), pool_scale[j].reshape(1, d), f"pool_fwd_{i}")
        s["mix"] = mix
        h_mid, u2, _ = res_norm(h, mix, ng(i, 1), ng(i, 2), f"norm_mid_{i}")
        s["h_mid"], s["u2"] = h_mid, u2
        s["up"] = project(u2, "w_up", i, f"ffn_up_{i}")
        s["act"] = conv_glu_fwd(s["up"], conv_w[i], conv_b[i].reshape(1, -1), f"glu_fwd_{i}")
        s["f"] = project(s["act"], "w_down", i, f"ffn_down_{i}")
        nxt = ng(i + 1, 0) if i + 1 < depth else None
        h, u16, u32 = res_norm(h_mid, s["f"], ng(i, 3), nxt, f"norm_out_{i}",
                               want_f32=(nxt is not None and (i + 1) % 3 == 2))
        saved.append(s)

    loss_acc, dh = loss_head(h, target, "loss_head")

    d_norm = [[None] * 4 for _ in range(depth)]
    d_lbs = jnp.zeros_like(lbs)
    d_onorm = [None] * onorm_g.shape[0]
    d_pscale = [None] * pool_scale.shape[0]
    d_cw, d_cb = [None] * depth, [None] * depth

    def exchanging(call, *args, **kw):
        plan = wts.exchange_plan(kw["name"])
        out, extra = call(*args, plan=plan, **kw)
        if plan is not None:
            wts.exchanged(extra)
        return out

    def bmm(a, b, mode, dtype, name):
        plan = wts.exchange_plan(name)
        if plan is None:
            return matmul(a, b, mode, dtype, name)
        out, extra = matmul(a, b, mode, dtype, name, plan=plan)
        wts.exchanged(extra)
        return out

    for i in reversed(range(depth)):
        kind, j = i % 3, i // 3
        s = saved[i]
        df, d_norm[i][3] = norm_bwd(s["f"], ng(i, 3), dh, None, BF16, f"nb_out_{i}")
        dact = bmm(df, wts.weight("w_down", i), "nt", F32, f"d_act_{i}")
        wts.grad("w_down", i, bmm(s["act"], df, "tn", BF16, f"dw_down_{i}"))
        dup, dcw, dcb = conv_glu_bwd(s["up"], conv_w[i], conv_b[i].reshape(1, -1), dact, f"glu_bwd_{i}")
        d_cw[i] = jnp.moveaxis(dcw, 0, 1).reshape(CONV_WIDTH, -1)
        d_cb[i] = dcb.reshape(-1)
        du2 = bmm(Split(dup), wts.weight("w_up", i), "nt", F32, f"d_u2_{i}")
        wts.grad("w_up", i, bmm(s["u2"], Split(dup), "tn", BF16, f"dw_up_{i}"))
        dh_mid, d_norm[i][2] = norm_bwd(s["h_mid"], ng(i, 2), du2, dh, F32, f"nb_mid_{i}")
        dm, d_norm[i][1] = norm_bwd(s["mix"], ng(i, 1), dh_mid, None, F32 if kind == 2 else BF16, f"nb_mix_{i}")
        if kind == 0:
            dy = matmul(dm, wts.weight("w_out", j), "nt", F32, f"d_y_{i}")
            wts.grad("w_out", j, matmul(s["y"], dm, "tn", BF16, f"dw_hgout_{i}"))
            dproj, d_onorm[j], dlb = exchanging(hgrn_bwd, s["proj"], lbs[i].reshape(1, -1),
                                                onorm_g[j].reshape(1, -1), dy, i > 0, name=f"hgrn_bwd_{i}")
            d_lbs = d_lbs.at[i].set(dlb[0])
            wts.grad("w_in", j, bmm(s["u1"], Split(dproj), "tn", BF16, f"dw_hgin_{i}"))
            du1 = bmm(Split(dproj), wts.weight("w_in", j), "nt", F32, f"d_u1_{i}")
        elif kind == 1:
            do = matmul(dm, wts.weight("w_so", j), "nt", F32, f"d_o_{i}")
            wts.grad("w_so", j, matmul(s["o"], dm, "tn", BF16, f"dw_sbout_{i}"))
            dqkv = exchanging(sba_bwd, s["qkv"], s["o"], do, name=f"sba_bwd_{i}")
            du1 = bmm(Split(dqkv), wts.weight("w_qkv", j), "nt", F32, f"d_u1_{i}")
            wts.grad("w_qkv", j, bmm(s["u1"], Split(dqkv), "tn", BF16, f"dw_sbqkv_{i}"))
        else:
            du1, g_pool, d_pscale[j] = pool_bwd(s["u1f"], wts.weight("w_pool", j), pool_scale[j].reshape(1, d),
                                                dm, f"pool_bwd_{i}")
            wts.grad("w_pool", j, g_pool)
        dh, d_norm[i][0] = norm_bwd(s["h_in"], ng(i, 0), du1, dh_mid, F32, f"nb_in_{i}")

    small = {
        "norm_g": jnp.stack([jnp.stack([v.reshape(d) for v in row]) for row in d_norm]),
        "lb_logits": lower_bounds_bwd(lb_logits, d_lbs),
        "onorm_g": jnp.stack([v.reshape(-1) for v in d_onorm]),
        "pool_scale": jnp.stack([v.reshape(-1) for v in d_pscale]),
        "conv_w": jnp.stack(d_cw),
        "conv_b": jnp.stack(d_cb),
    }
    return loss_acc, dh, small


SMALL_SHARDED = ("norm_g", "onorm_g", "pool_scale", "conv_w")
SMALL_ORDER = ("norm_g", "lb_logits", "onorm_g", "pool_scale", "conv_w", "conv_b")


def kernel(x, norm_g, hgrn_lb_logits, hgrn_w_in, hgrn_onorm_g, hgrn_w_out, sba_w_qkv, sba_w_out, pool_w, pool_scale, ffn_w_up, ffn_conv_w, ffn_conv_b, ffn_w_down, loss_target, m_norm_g, m_hgrn_lb_logits, m_hgrn_w_in, m_hgrn_onorm_g, m_hgrn_w_out, m_sba_w_qkv, m_sba_w_out, m_pool_w, m_pool_scale, m_ffn_w_up, m_ffn_conv_w, m_ffn_conv_b, m_ffn_w_down, v_norm_g, v_hgrn_lb_logits, v_hgrn_w_in, v_hgrn_onorm_g, v_hgrn_w_out, v_sba_w_qkv, v_sba_w_out, v_pool_w, v_pool_scale, v_ffn_w_up, v_ffn_conv_w, v_ffn_conv_b, v_ffn_w_down):
    cx, cy, cc = lax.axis_index("x"), lax.axis_index("y"), lax.axis_index("c")
    block = 4 * cx + 2 * cy + cc
    core = cc.astype(jnp.int32).reshape(1)

    pool3 = lambda a: a.reshape(a.shape[0] * a.shape[1], a.shape[2], a.shape[3])
    big_w = dict(zip(BIG_ORDER, [hgrn_w_in, hgrn_w_out, sba_w_qkv, sba_w_out, pool3(pool_w), ffn_w_up, ffn_w_down]))
    big_m = dict(zip(BIG_ORDER, [m_hgrn_w_in, m_hgrn_w_out, m_sba_w_qkv, m_sba_w_out, pool3(m_pool_w), m_ffn_w_up,
                                 m_ffn_w_down]))
    big_v = dict(zip(BIG_ORDER, [v_hgrn_w_in, v_hgrn_w_out, v_sba_w_qkv, v_sba_w_out, pool3(v_pool_w), v_ffn_w_up,
                                 v_ffn_w_down]))

    sharded = {"norm_g": norm_g, "onorm_g": hgrn_onorm_g, "pool_scale": pool_scale, "conv_w": ffn_conv_w}
    gathered = small_exchange(_pack([sharded[n] for n in SMALL_SHARDED]), False, "gather_small")
    parts = _unpack(gathered, [sharded[n].shape for n in SMALL_SHARDED])
    full = {n: _unshard_last(p) for n, p in zip(SMALL_SHARDED, parts)}

    wts = MeshWeights({k: w.astype(BF16) for k, w in big_w.items()}, big_w, big_m, big_v, core)
    loss_acc, grad_x, small_g = train_step(
        x[0], loss_target[0], full["norm_g"], hgrn_lb_logits, full["onorm_g"], full["pool_scale"],
        full["conv_w"], ffn_conv_b, wts)
    loss = lax.psum(loss_acc[0, 0], ("x", "y", "c"))

    shapes = [small_g[n].shape for n in SMALL_ORDER]
    summed = _unpack(small_exchange(_pack([small_g[n] for n in SMALL_ORDER]), True, "reduce_small"), shapes)
    sg = {n: (_shard_last(g, block) if n in SMALL_SHARDED else g) for n, g in zip(SMALL_ORDER, summed)}
    sw = {"norm_g": norm_g, "lb_logits": hgrn_lb_logits, "onorm_g": hgrn_onorm_g, "pool_scale": pool_scale,
          "conv_w": ffn_conv_w, "conv_b": ffn_conv_b}
    sm = {"norm_g": m_norm_g, "lb_logits": m_hgrn_lb_logits, "onorm_g": m_hgrn_onorm_g, "pool_scale": m_pool_scale,
          "conv_w": m_ffn_conv_w, "conv_b": m_ffn_conv_b}
    sv = {"norm_g": v_norm_g, "lb_logits": v_hgrn_lb_logits, "onorm_g": v_hgrn_onorm_g, "pool_scale": v_pool_scale,
          "conv_w": v_ffn_conv_w, "conv_b": v_ffn_conv_b}
    sshapes = [sw[n].shape for n in SMALL_ORDER]
    packed = [_pack([dct[n] for n in SMALL_ORDER]) for dct in (sw, sg, sm, sv)]
    s_delta, s_m, s_v = [dict(zip(SMALL_ORDER, _unpack(p, sshapes))) for p in adam_small(*packed)]

    upd = wts.finish()
    b_grad, b_delta, b_m, b_v = [[upd[k][n] for k in BIG_ORDER] for n in range(4)]

    def tree(small, bigs):
        bg = list(bigs)
        bg[4] = bg[4].reshape(pool_w.shape)
        return (small["norm_g"], small["lb_logits"], bg[0], small["onorm_g"], bg[1], bg[2], bg[3], bg[4],
                small["pool_scale"], bg[5], small["conv_w"], small["conv_b"], bg[6])

    return (loss, grad_x[None], *tree(sg, b_grad), *tree(s_delta, b_delta), *tree(s_m, b_m), *tree(s_v, b_v))
```

```python
import functools
import math

import jax
import jax.numpy as jnp
from jax import lax
from jax.experimental import pallas as pl
from jax.experimental.pallas import tpu as pltpu

F32 = jnp.float32
BF16 = jnp.bfloat16
HI = lax.Precision.HIGHEST
MESH = pl.DeviceIdType.MESH
ANY = pl.BlockSpec(memory_space=pl.ANY)
VMEM_FULL = pl.BlockSpec(memory_space=pltpu.VMEM)

NORM_EPS = 1e-6
HEAD = 128
HG_CHUNK = 128
HG_SUB = 32
HG_MAX_EXPONENT = 80.0
SB_BLOCK = 256
SB_QROWS = 256
POOL_WINDOWS = (2, 4, 8, 16)
CONV_WIDTH = 3
ROW_TILE = 256
N_DEV = 8
N_CHIP = 4

ADAM_LR = 0.001
ADAM_B1 = 0.9
ADAM_B2 = 0.999
ADAM_EPS = 1e-08
ADAM_WD = 0.01
ADAM_STEP = 10

VMEM_LIMIT = 48 * 1024 * 1024
LANES = 128
ELEMWISE_BLOCK_ELEMS = 256 * 1024


def _params(sem=None, vmem=VMEM_LIMIT):
    return pltpu.CompilerParams(dimension_semantics=sem, vmem_limit_bytes=vmem)


def _tile(n, prefs=(1024, 512, 256, 128)):
    for p in prefs:
        if n % p == 0:
            return p
    return n


def _dot(a, b, prec=None):
    return jnp.dot(a, b, precision=prec, preferred_element_type=F32)


def _dot_nt(a, b, prec=None):
    return lax.dot_general(a, b, (((1,), (1,)), ((), ())), precision=prec, preferred_element_type=F32)


def _dot_tn(a, b, prec=None):
    return lax.dot_general(a, b, (((0,), (0,)), ((), ())), precision=prec, preferred_element_type=F32)


def _split_dot(x, tri, parts, left=False):
    tot, rest = None, x
    for p in range(parts):
        h = rest.astype(BF16)
        d = _dot(tri, h) if left else _dot(h, tri)
        tot = d if tot is None else tot + d
        if p + 1 < parts:
            rest = rest - h.astype(F32)
    return tot


def _dot1(a, b, fn):
    return fn(a.astype(BF16), b.astype(BF16))


def _dot3(a, b, fn):
    ah, bh = a.astype(BF16), b.astype(BF16)
    al, bl = (a - ah.astype(F32)).astype(BF16), (b - bh.astype(F32)).astype(BF16)
    return fn(ah, bh) + fn(ah, bl) + fn(al, bh)


def _sigmoid(x):
    return jax.nn.sigmoid(x)


def _softplus(x):
    return jnp.maximum(x, 0.0) + jnp.log1p(jnp.exp(-jnp.abs(x)))


class Split:
    def __init__(self, arr):
        self.arr = arr
        self.shape = (arr.shape[1], arr.shape[0] * arr.shape[2])
        self.part = arr.shape[2]


class Plain:
    def __init__(self, arr):
        self.arr = arr
        self.shape = arr.shape
        self.part = None


def _wrap(op):
    return op if isinstance(op, (Split, Plain)) else Plain(op)


def _op_spec(op, br, bc, rc_of_grid):
    if isinstance(op, Split):
        per = op.part // bc

        def imap(i, j, k):
            r, c = rc_of_grid(i, j, k)
            return (lax.div(c, per), r, lax.rem(c, per))
        return pl.BlockSpec((None, br, bc), imap)
    return pl.BlockSpec((br, bc), rc_of_grid)


def _hosted(body, n_in, n_out, plan, step_of_grid):
    if plan is None:
        return body
    pi, po, ps = len(plan.args), len(plan.out_shape), len(plan.scratch)

    def wrapped(*refs):
        refs = list(refs)
        ins, pins = refs[:n_in], refs[n_in:n_in + pi]
        outs = refs[n_in + pi:n_in + pi + n_out]
        pouts = refs[n_in + pi + n_out:n_in + pi + n_out + po]
        scr, pscr = refs[n_in + pi + n_out + po:len(refs) - ps], refs[len(refs) - ps:]
        step, nsteps = step_of_grid()

        @pl.when(step == 0)
        def _():
            plan.start(pins, pouts, pscr)

        if plan.has_mid:
            @pl.when(step == min((2 * nsteps) // 3, nsteps - 1))
            def _():
                plan.mid(pins, pouts, pscr)

        body(*ins, *outs, *scr)

        @pl.when(step == nsteps - 1)
        def _():
            plan.finish(pins, pouts, pscr)

    return wrapped


def _host_call(body, n_in, plan, step_of_grid, *, out_shape, grid, in_specs, out_specs, scratch_shapes, name, sem, args):
    single = not isinstance(out_shape, (list, tuple))
    out_shape = [out_shape] if single else list(out_shape)
    out_specs = [out_specs] if single else list(out_specs)
    n_out = len(out_shape)
    in_specs, scratch_shapes, args = list(in_specs), list(scratch_shapes), list(args)
    if plan is not None:
        in_specs += [ANY] * len(plan.args)
        args += plan.args
        out_shape += plan.out_shape
        out_specs += [ANY] * len(plan.out_shape)
        scratch_shapes += plan.scratch
        sem = ("arbitrary",) * len(grid)
    outs = pl.pallas_call(
        _hosted(body, n_in, n_out, plan, step_of_grid), out_shape=out_shape, grid=grid,
        in_specs=in_specs, out_specs=out_specs, scratch_shapes=scratch_shapes,
        name=name, compiler_params=_params(sem),
    )(*args)
    host = outs[0] if single else list(outs[:n_out])
    return host, list(outs[n_out:])


def matmul(a, b, mode, out_dtype, name, plan=None):
    a, b = _wrap(a), _wrap(b)
    if mode == "nn":
        (m, kd), (kd2, n) = a.shape, b.shape
    elif mode == "nt":
        (m, kd), (n, kd2) = a.shape, b.shape
    else:
        (kd, m), (kd2, n) = a.shape, b.shape
    assert kd == kd2, (mode, a.shape, b.shape)

    def dim_tile(full, ops_on_cols, prefs=(1024, 512, 256, 128)):
        base = full
        for op in ops_on_cols:
            if op.part is not None:
                base = math.gcd(base, op.part)
        return _tile(base, prefs)

    tm = dim_tile(m, [a] if mode == "tn" else [])
    tn = dim_tile(n, [b] if mode in ("nn", "tn") else [])
    tk = dim_tile(kd, ([a] if mode in ("nn", "nt") else []) + ([b] if mode == "nt" else []),
                  prefs=(2048, 2816, 1024, 512, 256, 128))
    nk = kd // tk

    if mode == "nn":
        a_spec = _op_spec(a, tm, tk, lambda i, j, k: (i, k))
        b_spec = _op_spec(b, tk, tn, lambda i, j, k: (k, j))
        dot = _dot
    elif mode == "nt":
        a_spec = _op_spec(a, tm, tk, lambda i, j, k: (i, k))
        b_spec = _op_spec(b, tn, tk, lambda i, j, k: (j, k))
        dot = _dot_nt
    else:
        a_spec = _op_spec(a, tk, tm, lambda i, j, k: (k, i))
        b_spec = _op_spec(b, tk, tn, lambda i, j, k: (k, j))
        dot = _dot_tn

    def body(a_ref, b_ref, o_ref, *acc):
        part = dot(a_ref[...].astype(BF16), b_ref[...].astype(BF16))
        if nk == 1:
            o_ref[...] = part.astype(o_ref.dtype)
            return
        (acc_ref,) = acc
        k = pl.program_id(2)

        @pl.when(k == 0)
        def _():
            acc_ref[...] = part

        @pl.when(k > 0)
        def _():
            acc_ref[...] += part

        @pl.when(k == nk - 1)
        def _():
            o_ref[...] = acc_ref[...].astype(o_ref.dtype)

    gi, gj = m // tm, n // tn

    def step_of_grid():
        return (pl.program_id(0) * gj + pl.program_id(1)) * nk + pl.program_id(2), gi * gj * nk

    out, extra = _host_call(
        body, 2, plan, step_of_grid, out_shape=jax.ShapeDtypeStruct((m, n), out_dtype), grid=(gi, gj, nk),
        in_specs=[a_spec, b_spec], out_specs=pl.BlockSpec((tm, tn), lambda i, j, k: (i, j)),
        scratch_shapes=[pltpu.VMEM((tm, tn), F32)] if nk > 1 else [], name=name,
        sem=("parallel", "parallel", "arbitrary"), args=[a.arr, b.arr])
    return out if plan is None else (out, extra)


def _rms(x, g):
    r = lax.rsqrt(jnp.mean(x * x, axis=-1, keepdims=True) + NORM_EPS)
    return x * r * g


def res_norm(h, m, g_a, g_b, name, want_f32=False):
    t, d = h.shape
    tr = min(ROW_TILE, t)
    has_m, has_u = m is not None, g_b is not None
    row = pl.BlockSpec((tr, d), lambda i: (i, 0))
    vec = pl.BlockSpec((1, d), lambda i: (0, 0))

    def body(*refs):
        refs = list(refs)
        h_ref = refs.pop(0)
        hn = h_ref[...]
        if has_m:
            m_ref, ga_ref = refs.pop(0), refs.pop(0)
            hn = hn + _rms(m_ref[...], ga_ref[...])
        if has_u:
            gb_ref = refs.pop(0)
        if has_m:
            refs.pop(0)[...] = hn
        if has_u:
            u = _rms(hn, gb_ref[...])
            refs.pop(0)[...] = u.astype(BF16)
            if want_f32:
                refs.pop(0)[...] = u

    args, in_specs, out_shape, out_specs = [h], [row], [], []
    if has_m:
        args += [m, g_a]
        in_specs += [row, vec]
        out_shape.append(jax.ShapeDtypeStruct((t, d), F32))
        out_specs.append(row)
    if has_u:
        args.append(g_b)
        in_specs.append(vec)
        out_shape.append(jax.ShapeDtypeStruct((t, d), BF16))
        out_specs.append(row)
        if want_f32:
            out_shape.append(jax.ShapeDtypeStruct((t, d), F32))
            out_specs.append(row)
    outs = list(pl.pallas_call(
        body, out_shape=out_shape, grid=(t // tr,), in_specs=in_specs, out_specs=out_specs,
        name=name, compiler_params=_params(("parallel",)),
    )(*args))
    h_new = outs.pop(0) if has_m else None
    u16 = outs.pop(0) if has_u else None
    u32 = outs.pop(0) if (has_u and want_f32) else None
    return h_new, u16, u32


def norm_bwd(x, g, dy, add, out_dtype, name):
    t, d = x.shape
    tr = min(ROW_TILE, t)
    has_add = add is not None
    row = pl.BlockSpec((tr, d), lambda i: (i, 0))
    vec = pl.BlockSpec((1, d), lambda i: (0, 0))

    def body(*refs):
        if has_add:
            x_ref, g_ref, dy_ref, add_ref, dx_ref, dg_ref = refs
        else:
            x_ref, g_ref, dy_ref, dx_ref, dg_ref = refs
        xv = x_ref[...]
        dyv = dy_ref[...].astype(F32)
        r = lax.rsqrt(jnp.mean(xv * xv, axis=-1, keepdims=True) + NORM_EPS)
        gy = dyv * g_ref[...]
        dx = r * gy - xv * (r * r * r * jnp.mean(gy * xv, axis=-1, keepdims=True))
        if has_add:
            dx = dx + add_ref[...]
        dx_ref[...] = dx.astype(dx_ref.dtype)

        @pl.when(pl.program_id(0) == 0)
        def _():
            dg_ref[...] = jnp.zeros_like(dg_ref)

        dg_ref[...] += jnp.sum(dyv * xv * r, axis=0, keepdims=True)

    args = [x, g, dy] + ([add] if has_add else [])
    in_specs = [row, vec, row] + ([row] if has_add else [])
    return pl.pallas_call(
        body, out_shape=[jax.ShapeDtypeStruct((t, d), out_dtype), jax.ShapeDtypeStruct((1, d), F32)],
        grid=(t // tr,), in_specs=in_specs, out_specs=[row, vec],
        name=name, compiler_params=_params(("arbitrary",)),
    )(*args)


def loss_head(y, target, name):
    t, d = y.shape
    tr = min(ROW_TILE, t)
    row = pl.BlockSpec((tr, d), lambda i: (i, 0))
    acc = pl.BlockSpec((8, LANES), lambda i: (0, 0))

    def body(y_ref, t_ref, loss_ref, dy_ref):
        e = y_ref[...] - t_ref[...]
        dy_ref[...] = e * (1.0 / d)

        @pl.when(pl.program_id(0) == 0)
        def _():
            loss_ref[...] = jnp.zeros_like(loss_ref)

        loss_ref[...] += jnp.sum(e * e) * (0.5 / d)

    return pl.pallas_call(
        body, out_shape=[jax.ShapeDtypeStruct((8, LANES), F32), jax.ShapeDtypeStruct((t, d), F32)],
        grid=(t // tr,), in_specs=[row, row], out_specs=[acc, row],
        name=name, compiler_params=_params(("arbitrary",)),
    )(y, target)


def _depth_softmax(ref, depth):
    rows = [ref[i:i + 1, :] for i in range(depth)]
    mx = functools.reduce(jnp.maximum, rows)
    ex = [jnp.exp(r - mx) for r in rows]
    tot = functools.reduce(lambda p, q: p + q, ex)
    return [e / tot for e in ex]


def lower_bounds_fwd(logits):
    depth, kw = logits.shape

    def body(l_ref, o_ref):
        s = _depth_softmax(l_ref, depth)
        run = jnp.zeros_like(s[0])
        o_ref[0:1, :] = run
        for i in range(1, depth):
            run = run + s[i]
            o_ref[i:i + 1, :] = run

    return pl.pallas_call(body, out_shape=jax.ShapeDtypeStruct((depth, kw), F32), name="lb_fwd")(logits)


def lower_bounds_bwd(logits, dlb):
    depth, kw = logits.shape

    def body(l_ref, d_ref, o_ref):
        s = _depth_softmax(l_ref, depth)
        ds = [jnp.zeros_like(s[0]) for _ in range(depth)]
        run = jnp.zeros_like(s[0])
        for j in range(depth - 1, 0, -1):
            run = run + d_ref[j:j + 1, :]
            ds[j] = run
        dot = functools.reduce(lambda p, q: p + q, [s[j] * ds[j] for j in range(depth)])
        for j in range(depth):
            o_ref[j:j + 1, :] = s[j] * (ds[j] - dot)

    return pl.pallas_call(body, out_shape=jax.ShapeDtypeStruct((depth, kw), F32), name="lb_bwd")(logits, dlb)


def _hg_gates(qp, fp, lb_row, has_lb):
    sig = _sigmoid(fp)
    nsig = _sigmoid(-fp)
    ls = jnp.minimum(fp, 0.0) - jnp.log1p(jnp.exp(-jnp.abs(fp)))
    if has_lb:
        a = jnp.log(lb_row)
        bb = jnp.log1p(-lb_row) + ls
        g = jnp.maximum(a, bb) + jnp.log1p(jnp.exp(-jnp.abs(a - bb)))
        w = jnp.exp(bb - g)
        k = (1.0 - lb_row) * nsig
    else:
        g, w, k = ls, None, nsig
    q = qp * _sigmoid(qp)
    return q, k, g, sig, nsig, w


HG_HEADS_PER_STEP = 2


def _heads_per_step(nh):
    return HG_HEADS_PER_STEP if nh % HG_HEADS_PER_STEP == 0 else 1


def _head_views(refs, hh, lanes, lead):
    cs = pl.ds(hh * HEAD, HEAD)
    out = []
    for i, r in enumerate(refs):
        if i in lead:
            out.append(r.at[hh])
        elif i in lanes:
            out.append(r.at[(slice(None),) * (len(r.shape) - 1) + (cs,)])
        else:
            out.append(r)
    return out


def _hg_masks():
    c = HG_CHUNK
    row = lax.broadcasted_iota(jnp.int32, (c, c), 0)
    col = lax.broadcasted_iota(jnp.int32, (c, c), 1)
    lower = (col <= row).astype(BF16)
    upper = (col >= row).astype(BF16)
    krow = lax.broadcasted_iota(jnp.int32, (c, HEAD), 0)
    arow = lax.broadcasted_iota(jnp.int32, (HG_SUB, c), 0)
    acol = lax.broadcasted_iota(jnp.int32, (HG_SUB, c), 1)
    return lower, upper, krow, arow, acol


def _hg_sub(i, q, k, b, b_ref, krow, arow, acol):
    r0 = i * HG_SUB
    m = b_ref[r0:r0 + 1, :]
    ebq = jnp.exp(b[r0:r0 + HG_SUB, :] - m)
    qh = (q[r0:r0 + HG_SUB, :] * ebq).astype(BF16)
    ek = jnp.exp(jnp.where(krow < r0 + HG_SUB, jnp.minimum(m - b, HG_MAX_EXPONENT), 0.0))
    kh = (k * ek).astype(BF16)
    mask = acol <= arow + r0
    amat = jnp.where(mask, _dot_nt(qh, kh), 0.0).astype(BF16)
    return ebq, qh, ek, kh, mask, amat


def hgrn_fwd(proj, lb, onorm_g, has_lb, name, plan=None):
    t, w4 = proj.shape
    kw = w4 // 4
    nh, nc, c = kw // HEAD, t // HG_CHUNK, HG_CHUNK
    hpb = _heads_per_step(nh)
    wide = hpb * HEAD

    def body(*refs):
        refs[7][...] = jnp.zeros_like(refs[7])
        heads = [one_head(*_head_views(refs, hh, lanes=(0, 1, 2, 3, 4, 5, 6), lead=(7, 8))) for hh in range(hpb)]

        def chunks(ci, carry):
            for chunk in heads:
                chunk(ci)
            return carry

        lax.fori_loop(0, nc, chunks, 0)

    def one_head(qp_ref, fp_ref, iv_ref, gp_ref, lb_ref, on_ref, y_ref, st_ref, b_ref):
        lower, _, krow, arow, acol = _hg_masks()
        lb_row, gam = lb_ref[...], on_ref[...]

        def chunk(ci):
            rs = pl.ds(pl.multiple_of(ci * c, c), c)
            v = iv_ref[rs, :].astype(BF16)
            gp = gp_ref[rs, :]
            q, k, g, _, _, _ = _hg_gates(qp_ref[rs, :], fp_ref[rs, :], lb_row, has_lb)
            b = _split_dot(g, lower, 3, left=True)
            b_ref[...] = b
            st = st_ref[...]
            o = _dot_nt((q * jnp.exp(b)).astype(BF16), st.astype(BF16))
            amats = [_hg_sub(i, q, k, b, b_ref, krow, arow, acol)[-1] for i in range(c // HG_SUB)]
            o = o + _dot(jnp.concatenate(amats, axis=0), v)
            bl = b_ref[c - 1:c, :]
            st_ref[...] = jnp.exp(bl) * st + _dot_tn(v, (k * jnp.exp(bl - b)).astype(BF16))
            r = lax.rsqrt(jnp.mean(o * o, axis=-1, keepdims=True) + NORM_EPS)
            y_ref[rs, :] = (o * r * gam * (gp * _sigmoid(gp))).astype(BF16)

        return chunk

    ns = nh // hpb
    col = lambda p: pl.BlockSpec((t, wide), lambda h: (0, p * ns + h))
    vec = pl.BlockSpec((1, wide), lambda h: (0, h))
    return _host_call(
        body, 6, plan, lambda: (pl.program_id(0), ns),
        out_shape=jax.ShapeDtypeStruct((t, kw), BF16), grid=(ns,),
        in_specs=[col(0), col(1), col(2), col(3), vec, vec],
        out_specs=pl.BlockSpec((t, wide), lambda h: (0, h)),
        scratch_shapes=[pltpu.VMEM((hpb, HEAD, HEAD), F32), pltpu.VMEM((hpb, c, HEAD), F32)],
        name=name, sem=("parallel",), args=[proj, proj, proj, proj, lb, onorm_g])


def hgrn_bwd(proj, lb, onorm_g, dy, has_lb, name, plan=None):
    t, w4 = proj.shape
    kw = w4 // 4
    nh, nc, c = kw // HEAD, t // HG_CHUNK, HG_CHUNK
    hpb = _heads_per_step(nh)
    wide = hpb * HEAD

    def body(*refs):
        for i in (8, 9, 13, 14):
            refs[i][...] = jnp.zeros_like(refs[i])
        heads = [one_head(*_head_views(refs, hh, lanes=(0, 1, 2, 3, 4, 5, 6, 7, 8, 9, 11), lead=(10, 12, 13, 14)))
                 for hh in range(hpb)]

        def fwd_chunks(ci, states):
            return tuple(fwd(ci, st) for (fwd, _), st in zip(heads, states))

        lax.fori_loop(0, nc, fwd_chunks, tuple(jnp.zeros((HEAD, HEAD), F32) for _ in heads))

        def bwd_chunks(step, carry):
            for _, bwd in heads:
                bwd(nc - 1 - step)
            return carry

        lax.fori_loop(0, nc, bwd_chunks, 0)

    def one_head(qp_ref, fp_ref, iv_ref, gp_ref, lb_ref, on_ref, dy_ref,
                 dp_ref, dgam_ref, dlb_ref, sst_ref, o_ref, b_ref, dst_ref, car_ref):
        lower, upper, krow, arow, acol = _hg_masks()
        lb_row, gam = lb_ref[...], on_ref[...]
        mm = _dot3 if has_lb else _dot1

        def recompute(ci):
            rs = pl.ds(pl.multiple_of(ci * c, c), c)
            gates = _hg_gates(qp_ref[rs, :], fp_ref[rs, :], lb_row, has_lb)
            b = _split_dot(gates[2], lower, 3, left=True)
            b_ref[...] = b
            return rs, gates, b

        def fwd_chunk(ci, carry):
            rs, (q, k, g, _, _, _), b = recompute(ci)
            v = iv_ref[rs, :].astype(BF16)
            st = carry
            sst_ref[ci] = st
            o = _dot_nt((q * jnp.exp(b)).astype(BF16), st.astype(BF16))
            amats = [_hg_sub(i, q, k, b, b_ref, krow, arow, acol)[-1] for i in range(c // HG_SUB)]
            o_ref[rs, :] = o + _dot(jnp.concatenate(amats, axis=0), v)
            bl = b_ref[c - 1:c, :]
            return jnp.exp(bl) * st + mm(iv_ref[rs, :], k * jnp.exp(bl - b), _dot_tn)

        def bwd_chunk(ci):
            rs, (q, k, g, sig, nsig, w), b = recompute(ci)
            qp, gp, v32 = qp_ref[rs, :], gp_ref[rs, :], iv_ref[rs, :]
            v = v32.astype(BF16)
            dyv, o = dy_ref[rs, :], o_ref[rs, :]
            st = sst_ref[ci]
            dst_new = dst_ref[...]
            r = lax.rsqrt(jnp.mean(o * o, axis=-1, keepdims=True) + NORM_EPS)
            on = o * r
            sgm = _sigmoid(gp)
            sg = gp * sgm
            dgam_ref[...] += jnp.sum(dyv * sg * on, axis=0, keepdims=True)
            dgp = dyv * on * gam * (sgm * (1.0 + gp * (1.0 - sgm)))
            dn = dyv * gam * sg
            do = r * dn - o * (r * r * r * jnp.mean(dn * o, axis=-1, keepdims=True))
            eb = jnp.exp(b)
            bl = b_ref[c - 1:c, :]
            ekd = jnp.exp(bl - b)
            do16 = do.astype(BF16)
            dq = mm(do, st, _dot) * eb
            dv = _dot1(k * ekd, dst_new, _dot_nt)
            dk = mm(v32, dst_new, _dot) * ekd
            dst_ref[...] = jnp.exp(bl) * dst_new + mm(do, q * eb, _dot_tn)
            da_all = mm(do, v32, _dot_nt)
            dq_parts, amats = [], []
            for i in range(c // HG_SUB):
                r0 = i * HG_SUB
                ebq, _, ek, _, mask, amat = _hg_sub(i, q, k, b, b_ref, krow, arow, acol)
                da = jnp.where(mask, da_all[r0:r0 + HG_SUB, :], 0.0)
                dq_parts.append(mm(da, k * ek, _dot) * ebq)
                dk = dk + mm(da, q[r0:r0 + HG_SUB, :] * ebq, _dot_tn) * ek
                amats.append(amat)
            dq = dq + jnp.concatenate(dq_parts, axis=0)
            dv = dv + _dot_tn(jnp.concatenate(amats, axis=0), do16)
            db = q * dq - k * dk
            dg = car_ref[...] + _split_dot(db, upper, 3, left=True)
            car_ref[...] += jnp.sum(db, axis=0, keepdims=True)
            if has_lb:
                dfp = dg * nsig * w - dk * ((1.0 - lb_row) * sig * nsig)
                dlb_ref[...] += jnp.sum(dg * nsig * jnp.exp(-g) - dk * nsig, axis=0, keepdims=True)
            else:
                dfp = dg * nsig - dk * (sig * nsig)
            sq = _sigmoid(qp)
            dp_ref[0, rs, :] = (dq * (sq * (1.0 + qp * (1.0 - sq)))).astype(BF16)
            dp_ref[1, rs, :] = dfp.astype(BF16)
            dp_ref[2, rs, :] = dv.astype(BF16)
            dp_ref[3, rs, :] = dgp.astype(BF16)

        return fwd_chunk, bwd_chunk

    ns = nh // hpb
    col = lambda p: pl.BlockSpec((t, wide), lambda h: (0, p * ns + h))
    vec = pl.BlockSpec((1, wide), lambda h: (0, h))
    return _host_call(
        body, 7, plan, lambda: (pl.program_id(0), ns),
        out_shape=[jax.ShapeDtypeStruct((4, t, kw), BF16), jax.ShapeDtypeStruct((1, kw), F32),
                   jax.ShapeDtypeStruct((1, kw), F32)],
        grid=(ns,),
        in_specs=[col(0), col(1), col(2), col(3), vec, vec, pl.BlockSpec((t, wide), lambda h: (0, h))],
        out_specs=[pl.BlockSpec((4, t, wide), lambda h: (0, 0, h)), vec, vec],
        scratch_shapes=[pltpu.VMEM((hpb, nc, HEAD, HEAD), F32), pltpu.VMEM((t, wide), F32),
                        pltpu.VMEM((hpb, c, HEAD), F32), pltpu.VMEM((hpb, HEAD, HEAD), F32),
                        pltpu.VMEM((hpb, 1, HEAD), F32)],
        name=name, sem=("parallel",), args=[proj, proj, proj, proj, lb, onorm_g, dy])


def _sb_masks():
    m, n = SB_QROWS, SB_BLOCK
    row = lax.broadcasted_iota(jnp.int32, (m, n), 0)
    col = lax.broadcasted_iota(jnp.int32, (m, n), 1)
    r2 = lax.broadcasted_iota(jnp.int32, (n, n), 0)
    c2 = lax.broadcasted_iota(jnp.int32, (n, n), 1)
    after = (r2 > c2).astype(BF16)
    from_ = (r2 >= c2).astype(BF16)
    return row, col, after, from_


def sba_fwd(qkv, name, plan=None):
    t, w3 = qkv.shape
    wd = w3 // 3
    m, n = SB_QROWS, SB_BLOCK
    nh, nq, per = wd // HEAD, t // m, m // n
    scale = HEAD ** -0.5

    def body(q_ref, k_ref, v_ref, o_ref, q16, k16, v16):
        row, col, after, _ = _sb_masks()
        q16[...] = (q_ref[...] * scale).astype(BF16)
        k16[...] = k_ref[...].astype(BF16)
        v16[...] = v_ref[...].astype(BF16)

        def qblock(qi, carry):
            qs = pl.ds(pl.multiple_of(qi * m, m), m)
            q = q16[qs, :]
            last = (qi + 1) * per - 1

            def kblock(step, state):
                acc, rem0 = state
                kj = last - step
                ks = pl.ds(pl.multiple_of(kj * n, n), n)
                z = _dot_nt(q, k16[ks, :])
                strict = (col + kj * n) < (row + qi * m)
                spz = _softplus(z)
                sp = jnp.where(strict, spz, 0.0)
                rem = rem0 + _split_dot(sp, after, 2)
                a = jnp.where(strict, jnp.exp(z - spz - rem), 0.0)
                acc = acc + _dot(a.astype(BF16), v16[ks, :])
                return acc, rem0 + jnp.sum(sp, axis=1, keepdims=True)

            acc, _ = lax.fori_loop(0, last + 1, kblock,
                                   (jnp.zeros((m, HEAD), F32), jnp.zeros((m, 1), F32)))
            o_ref[qs, :] = acc
            return carry

        lax.fori_loop(0, nq, qblock, 0)

    col_spec = lambda p: pl.BlockSpec((t, HEAD), lambda h: (0, p * nh + h))
    return _host_call(
        body, 3, plan, lambda: (pl.program_id(0), nh),
        out_shape=jax.ShapeDtypeStruct((t, wd), F32), grid=(nh,),
        in_specs=[col_spec(0), col_spec(1), col_spec(2)],
        out_specs=pl.BlockSpec((t, HEAD), lambda h: (0, h)),
        scratch_shapes=[pltpu.VMEM((t, HEAD), BF16)] * 3,
        name=name, sem=("parallel",), args=[qkv, qkv, qkv])


def sba_bwd(qkv, o, do, name, plan=None):
    t, w3 = qkv.shape
    wd = w3 // 3
    m, n = SB_QROWS, SB_BLOCK
    nh, nq, per = wd // HEAD, t // m, m // n
    scale = HEAD ** -0.5

    def body(q_ref, k_ref, v_ref, o_ref, do_ref, d_ref, dk_ref, dv_ref, q16, k16, v16):
        row, col, after, from_ = _sb_masks()
        dk_ref[...] = jnp.zeros_like(dk_ref)
        dv_ref[...] = jnp.zeros_like(dv_ref)
        q16[...] = (q_ref[...] * scale).astype(BF16)
        k16[...] = k_ref[...].astype(BF16)
        v16[...] = v_ref[...].astype(BF16)

        def qblock(qi, carry):
            qs = pl.ds(pl.multiple_of(qi * m, m), m)
            q = q16[qs, :]
            dov = do_ref[qs, :]
            do16 = dov.astype(BF16)
            dsum = jnp.sum(do16.astype(F32) * o_ref[qs, :], axis=1, keepdims=True)
            last = (qi + 1) * per - 1

            def kblock(step, state):
                dq, rem0, e0 = state
                kj = last - step
                ks = pl.ds(pl.multiple_of(kj * n, n), n)
                kv, vv = k16[ks, :], v16[ks, :]
                z = _dot_nt(q, kv)
                strict = (col + kj * n) < (row + qi * m)
                spz = _softplus(z)
                sp = jnp.where(strict, spz, 0.0)
                rem = rem0 + _split_dot(sp, after, 2)
                sgz = jnp.exp(z - spz)
                a = jnp.where(strict, sgz * jnp.exp(-rem), 0.0).astype(BF16)
                e = a.astype(F32) * _dot_nt(do16, vv)
                left = dsum - (e0 + _split_dot(e, from_, 2))
                dz = jnp.where(strict, e * (1.0 - sgz) - sgz * left, 0.0).astype(BF16)
                dq = dq + _dot(dz, kv)
                dk_ref[ks, :] += _dot_tn(dz, q)
                dv_ref[ks, :] += _dot_tn(a, do16)
                return (dq, rem0 + jnp.sum(sp, axis=1, keepdims=True),
                        e0 + jnp.sum(e, axis=1, keepdims=True))

            zero1 = jnp.zeros((m, 1), F32)
            dq, _, _ = lax.fori_loop(0, last + 1, kblock, (jnp.zeros((m, HEAD), F32), zero1, zero1))
            d_ref[0, qs, :] = (dq * scale).astype(BF16)
            return carry

        lax.fori_loop(0, nq, qblock, 0)
        d_ref[1, :, :] = dk_ref[...].astype(BF16)
        d_ref[2, :, :] = dv_ref[...].astype(BF16)

    col_spec = lambda p: pl.BlockSpec((t, HEAD), lambda h: (0, p * nh + h))
    head = pl.BlockSpec((t, HEAD), lambda h: (0, h))
    return _host_call(
        body, 5, plan, lambda: (pl.program_id(0), nh),
        out_shape=jax.ShapeDtypeStruct((3, t, wd), BF16), grid=(nh,),
        in_specs=[col_spec(0), col_spec(1), col_spec(2), head, head],
        out_specs=pl.BlockSpec((3, t, HEAD), lambda h: (0, 0, h)),
        scratch_shapes=[pltpu.VMEM((t, HEAD), F32)] * 2 + [pltpu.VMEM((t, HEAD), BF16)] * 3,
        name=name, sem=("parallel",), args=[qkv, qkv, qkv, o, do])


def _pool_band(i_out, i_in, tr, win, transpose):
    r = lax.broadcasted_iota(jnp.int32, (tr, tr), 0) + i_out * tr
    c = lax.broadcasted_iota(jnp.int32, (tr, tr), 1) + i_in * tr
    if transpose:
        return ((r <= c) & (r > c - win)).astype(F32)
    return ((c <= r) & (c > r - win)).astype(F32)


def _pool_p(u_ref, i, tr, win):
    cur = u_ref[i * tr:(i + 1) * tr, :]
    ws = _dot(_pool_band(i, i, tr, win, False), cur, HI)
    if i > 0:
        ws = ws + _dot(_pool_band(i, i - 1, tr, win, False), u_ref[(i - 1) * tr:i * tr, :], HI)
    pos = lax.broadcasted_iota(jnp.int32, (tr, 1), 0) + (i * tr + 1)
    count = jnp.minimum(pos, win).astype(F32)
    return ws / count - cur, count


def pool_fwd(u, pool_w, pool_scale, name):
    t, d = u.shape
    ng = len(POOL_WINDOWS)
    gs = d // ng
    tr = min(ROW_TILE, t)

    def body(u_ref, w_ref, s_ref, y_ref):
        win = jnp.left_shift(2, pl.program_id(0))
        for i in range(t // tr):
            p, _ = _pool_p(u_ref, i, tr, win)
            y_ref[i * tr:(i + 1) * tr, :] = _dot(p.astype(BF16), w_ref[...]) * s_ref[...]

    grp = pl.BlockSpec((t, gs), lambda g: (0, g))
    return pl.pallas_call(
        body, out_shape=jax.ShapeDtypeStruct((t, d), F32), grid=(ng,),
        in_specs=[grp, pl.BlockSpec((None, gs, gs), lambda g: (g, 0, 0)), pl.BlockSpec((1, gs), lambda g: (0, g))],
        out_specs=grp, name=name, compiler_params=_params(("parallel",)),
    )(u, pool_w, pool_scale)


def pool_bwd(u, pool_w, pool_scale, dy, name):
    t, d = u.shape
    ng = len(POOL_WINDOWS)
    gs = d // ng
    tr = min(ROW_TILE, t)
    nt = t // tr

    def body(u_ref, w_ref, s_ref, dy_ref, du_ref, dw_ref, ds_ref, dpc_ref, dp_ref):
        win = jnp.left_shift(2, pl.program_id(0))
        wv = w_ref[...]
        dw = jnp.zeros((gs, gs), F32)
        dsc = jnp.zeros((1, gs), F32)
        for i in range(nt):
            rows = slice(i * tr, (i + 1) * tr)
            p, count = _pool_p(u_ref, i, tr, win)
            p16 = p.astype(BF16)
            dyv = dy_ref[rows, :]
            dsc = dsc + jnp.sum(dyv * _dot(p16, wv), axis=0, keepdims=True)
            dyp = (dyv * s_ref[...]).astype(BF16)
            dw = dw + _dot_tn(p16, dyp)
            dp = _dot_nt(dyp, wv)
            dp_ref[rows, :] = dp
            dpc_ref[rows, :] = dp / count
        dw_ref[...] = dw.astype(BF16)
        ds_ref[...] = dsc
        for i in range(nt):
            rows = slice(i * tr, (i + 1) * tr)
            acc = _dot(_pool_band(i, i, tr, win, True), dpc_ref[rows, :], HI)
            if i + 1 < nt:
                acc = acc + _dot(_pool_band(i, i + 1, tr, win, True), dpc_ref[(i + 1) * tr:(i + 2) * tr, :], HI)
            du_ref[rows, :] = acc - dp_ref[rows, :]

    grp = pl.BlockSpec((t, gs), lambda g: (0, g))
    wspec = pl.BlockSpec((None, gs, gs), lambda g: (g, 0, 0))
    vec = pl.BlockSpec((1, gs), lambda g: (0, g))
    return pl.pallas_call(
        body,
        out_shape=[jax.ShapeDtypeStruct((t, d), F32), jax.ShapeDtypeStruct((ng, gs, gs), BF16),
                   jax.ShapeDtypeStruct((1, d), F32)],
        grid=(ng,), in_specs=[grp, wspec, vec, grp], out_specs=[grp, wspec, vec],
        scratch_shapes=[pltpu.VMEM((t, gs), F32), pltpu.VMEM((t, gs), F32)],
        name=name, compiler_params=_params(("parallel",)),
    )(u, pool_w, pool_scale, dy)


CONV_COLS = 256
HALO = 8


def _conv_taps(ref, r0, tr):
    x = ref[r0:r0 + tr, :]
    prev = ref[r0 - HALO:r0, :] if r0 > 0 else jnp.zeros((HALO, x.shape[1]), F32)
    xx = jnp.concatenate([prev, x], axis=0)
    return x, pltpu.roll(xx, 1, 0)[HALO:, :], pltpu.roll(xx, 2, 0)[HALO:, :]


def _conv_out(taps, w_ref, b_ref):
    x, s1, s2 = taps
    return w_ref[0:1, :] * s2 + w_ref[1:2, :] * s1 + w_ref[2:3, :] * x + b_ref[...]


def conv_glu_fwd(up, conv_w, conv_b, name):
    t, f2 = up.shape
    f = f2 // 2
    tc = min(CONV_COLS, f)
    nj = f // tc
    tr = min(ROW_TILE, t)

    def body(ug_ref, uv_ref, wg_ref, wv_ref, bg_ref, bv_ref, o_ref):
        for i in range(t // tr):
            r0 = i * tr
            gate = _conv_out(_conv_taps(ug_ref, r0, tr), wg_ref, bg_ref)
            val = _conv_out(_conv_taps(uv_ref, r0, tr), wv_ref, bv_ref)
            o_ref[r0:r0 + tr, :] = (gate * _sigmoid(gate) * val).astype(BF16)

    blk = lambda rows, half: pl.BlockSpec((rows, tc), lambda j: (0, half * nj + j))
    return pl.pallas_call(
        body, out_shape=jax.ShapeDtypeStruct((t, f), BF16), grid=(nj,),
        in_specs=[blk(t, 0), blk(t, 1), blk(CONV_WIDTH, 0), blk(CONV_WIDTH, 1), blk(1, 0), blk(1, 1)],
        out_specs=pl.BlockSpec((t, tc), lambda j: (0, j)),
        name=name, compiler_params=_params(("parallel",)),
    )(up, up, conv_w, conv_w, conv_b, conv_b)


def conv_glu_bwd(up, conv_w, conv_b, dact, name):
    t, f2 = up.shape
    f = f2 // 2
    tc = min(CONV_COLS, f)
    nj = f // tc
    tr = min(ROW_TILE, t)
    nt = t // tr

    def body(ug_ref, uv_ref, wg_ref, wv_ref, bg_ref, bv_ref, da_ref, du_ref, dw_ref, db_ref, dg_ref, dv_ref):
        dwg = [jnp.zeros((1, tc), F32) for _ in range(CONV_WIDTH)]
        dwv = [jnp.zeros((1, tc), F32) for _ in range(CONV_WIDTH)]
        dbg = jnp.zeros((1, tc), F32)
        dbv = jnp.zeros((1, tc), F32)
        for i in range(nt):
            r0 = i * tr
            tg, tv = _conv_taps(ug_ref, r0, tr), _conv_taps(uv_ref, r0, tr)
            gate, val = _conv_out(tg, wg_ref, bg_ref), _conv_out(tv, wv_ref, bv_ref)
            sg = _sigmoid(gate)
            da = da_ref[r0:r0 + tr, :]
            d_gate = da * val * (sg * (1.0 + gate * (1.0 - sg)))
            d_val = da * (gate * sg)
            dg_ref[r0:r0 + tr, :] = d_gate
            dv_ref[r0:r0 + tr, :] = d_val
            dbg = dbg + jnp.sum(d_gate, axis=0, keepdims=True)
            dbv = dbv + jnp.sum(d_val, axis=0, keepdims=True)
            for tap in range(CONV_WIDTH):
                dwg[tap] = dwg[tap] + jnp.sum(d_gate * tg[2 - tap], axis=0, keepdims=True)
                dwv[tap] = dwv[tap] + jnp.sum(d_val * tv[2 - tap], axis=0, keepdims=True)
        for tap in range(CONV_WIDTH):
            dw_ref[0, tap:tap + 1, :] = dwg[tap]
            dw_ref[1, tap:tap + 1, :] = dwv[tap]
        db_ref[0, :, :] = dbg
        db_ref[1, :, :] = dbv
        for half, (d_ref, w_ref) in enumerate(((dg_ref, wg_ref), (dv_ref, wv_ref))):
            for i in range(nt):
                r0 = i * tr
                x = d_ref[r0:r0 + tr, :]
                nxt = d_ref[r0 + tr:r0 + tr + HALO, :] if i + 1 < nt else jnp.zeros((HALO, tc), F32)
                xx = jnp.concatenate([x, nxt], axis=0)
                up1 = pltpu.roll(xx, tr + HALO - 1, 0)[:tr, :]
                up2 = pltpu.roll(xx, tr + HALO - 2, 0)[:tr, :]
                du = w_ref[2:3, :] * x + w_ref[1:2, :] * up1 + w_ref[0:1, :] * up2
                du_ref[half, r0:r0 + tr, :] = du.astype(BF16)

    blk = lambda rows, half: pl.BlockSpec((rows, tc), lambda j: (0, half * nj + j))
    return pl.pallas_call(
        body,
        out_shape=[jax.ShapeDtypeStruct((2, t, f), BF16), jax.ShapeDtypeStruct((2, CONV_WIDTH, f), F32),
                   jax.ShapeDtypeStruct((2, 1, f), F32)],
        grid=(nj,),
        in_specs=[blk(t, 0), blk(t, 1), blk(CONV_WIDTH, 0), blk(CONV_WIDTH, 1), blk(1, 0), blk(1, 1),
                  pl.BlockSpec((t, tc), lambda j: (0, j))],
        out_specs=[pl.BlockSpec((2, t, tc), lambda j: (0, 0, j)),
                   pl.BlockSpec((2, CONV_WIDTH, tc), lambda j: (0, 0, j)),
                   pl.BlockSpec((2, 1, tc), lambda j: (0, 0, j))],
        scratch_shapes=[pltpu.VMEM((t, tc), F32), pltpu.VMEM((t, tc), F32)],
        name=name, compiler_params=_params(("parallel",)),
    )(up, up, conv_w, conv_w, conv_b, conv_b, dact)


def _place():
    x, y, c = lax.axis_index("x"), lax.axis_index("y"), lax.axis_index("c")
    others = [(1 - x, y), (x, 1 - y), (1 - x, 1 - y)]
    return x, y, c, others


def _window(ref, axis, b, n):
    if axis == 1:
        return ref.at[:, pl.ds(b * n, n), :]
    return ref.at[:, :, pl.ds(b * n, n)]


def _remote(src, dst, send_sems, recv_sems, k, to):
    return pltpu.make_async_remote_copy(src_ref=src, dst_ref=dst, send_sem=send_sems.at[k],
                                        recv_sem=recv_sems.at[k], device_id=to, device_id_type=MESH)


class GatherPlan:
    has_mid = True

    def __init__(self, items):
        self.items = items
        self.args = [s for s, _, _, _ in items]
        self.out_shape = []
        for s, _, nl, ax in items:
            shp = [nl, s.shape[1], s.shape[2]]
            shp[ax] *= N_DEV
            self.out_shape.append(jax.ShapeDtypeStruct(tuple(shp), s.dtype))
        n = len(items)
        self.scratch = [pltpu.SemaphoreType.DMA((7 * n,)), pltpu.SemaphoreType.DMA((7 * n,)),
                        pltpu.SemaphoreType.DMA((n,))]

    def _mine(self, ins, a):
        _, l0, nl, _ = self.items[a]
        return ins[a].at[pl.ds(l0, nl)]

    def _copy(self, ins, outs, sems, a, k, block, to, own=False):
        s, _, _, ax = self.items[a]
        px, py, pc = block
        w = _window(outs[a], ax, 4 * px + 2 * py + pc, s.shape[ax])
        return _remote(self._mine(ins, a) if own else w, w, sems[0], sems[1], 7 * a + k, to)

    def _local(self, ins, outs, sems, a, x, y, c):
        s, _, _, ax = self.items[a]
        return pltpu.make_async_copy(self._mine(ins, a), _window(outs[a], ax, 4 * x + 2 * y + c, s.shape[ax]),
                                     sems[2].at[a])


    def _first(self, ins, outs, sems, a, x, y, c, others):
        me = (x, y, c)
        return [self._copy(ins, outs, sems, a, 0, me, (x, y, 1 - c), own=True)] + [
            self._copy(ins, outs, sems, a, 1 + j, me, (*chip, c), own=True) for j, chip in enumerate(others[:2])]

    def _relay(self, ins, outs, sems, a, x, y, c):
        block = ((1 - c) * (1 - x) + c * x, (1 - c) * y + c * (1 - y), c)
        to = ((1 - c) * x + c * (1 - x), (1 - c) * (1 - y) + c * y, c)
        return self._copy(ins, outs, sems, a, 3, block, to)

    def _passed(self, ins, outs, sems, a, j, x, y, c, others):
        return self._copy(ins, outs, sems, a, 4 + j, (*others[j], c), (x, y, 1 - c))

    def start(self, ins, outs, sems):
        x, y, c, others = _place()
        for a in range(len(self.items)):
            self._local(ins, outs, sems, a, x, y, c).start()
        for a in range(len(self.items)):
            for cp in self._first(ins, outs, sems, a, x, y, c, others):
                cp.start()

    def mid(self, ins, outs, sems):
        x, y, c, others = _place()
        for a in range(len(self.items)):
            for j, chip in enumerate(others[:2]):
                self._copy(ins, outs, sems, a, 1 + j, (*chip, c), (x, y, c)).wait_recv()
            self._relay(ins, outs, sems, a, x, y, c).start()
            for j in range(2):
                self._passed(ins, outs, sems, a, j, x, y, c, others).start()

    def finish(self, ins, outs, sems):
        x, y, c, others = _place()
        for a in range(len(self.items)):
            self._copy(ins, outs, sems, a, 3, (*others[2], c), (x, y, c)).wait_recv()
            self._passed(ins, outs, sems, a, 2, x, y, c, others).start()
        for a in range(len(self.items)):
            self._copy(ins, outs, sems, a, 0, (x, y, 1 - c), (x, y, c)).wait_recv()
            for j, chip in enumerate(others):
                self._copy(ins, outs, sems, a, 4 + j, (*chip, 1 - c), (x, y, c)).wait_recv()
        for a in range(len(self.items)):
            for cp in self._first(ins, outs, sems, a, x, y, c, others):
                cp.wait_send()
            self._relay(ins, outs, sems, a, x, y, c).wait_send()
            for j in range(3):
                self._passed(ins, outs, sems, a, j, x, y, c, others).wait_send()
            self._local(ins, outs, sems, a, x, y, c).wait()


class ExchangePlan:
    has_mid = False

    def __init__(self, partials):
        self.args = list(partials)
        self.out_shape = [jax.ShapeDtypeStruct(p.shape, p.dtype) for p in partials]
        n = len(partials)
        self.scratch = [pltpu.SemaphoreType.DMA((3 * n,)), pltpu.SemaphoreType.DMA((3 * n,)),
                        pltpu.SemaphoreType.DMA((n,))]

    def _copies(self, ins, outs, sems):
        x, y, c, others = _place()
        me = 2 * x + y
        local, sends, recvs = [], [], []
        for a in range(len(self.args)):
            local.append(pltpu.make_async_copy(ins[a].at[me], outs[a].at[me], sems[2].at[a]))
            for j, (px, py) in enumerate(others):
                sends.append(_remote(ins[a].at[2 * px + py], outs[a].at[me], sems[0], sems[1], 3 * a + j, (px, py, c)))
                slot = outs[a].at[2 * px + py]
                recvs.append(_remote(slot, slot, sems[0], sems[1], 3 * a + j, (px, py, c)))
        return local, sends, recvs

    def start(self, ins, outs, sems):
        local, sends, _ = self._copies(ins, outs, sems)
        for cp in local + sends:
            cp.start()

    def finish(self, ins, outs, sems):
        local, sends, recvs = self._copies(ins, outs, sems)
        for cp in recvs:
            cp.wait_recv()
        for cp in sends:
            cp.wait_send()
        for cp in local:
            cp.wait()


def run_plan(plan, name):
    ni, no = len(plan.args), len(plan.out_shape)

    def body(*refs):
        ins, outs, sems = refs[:ni], refs[ni:ni + no], refs[ni + no:]
        plan.start(ins, outs, sems)
        if plan.has_mid:
            plan.mid(ins, outs, sems)
        plan.finish(ins, outs, sems)

    return pl.pallas_call(
        body, out_shape=plan.out_shape, in_specs=[ANY] * ni, out_specs=[ANY] * no,
        scratch_shapes=plan.scratch, name=name,
    )(*plan.args)


class SiblingPlan:
    has_mid = False

    def __init__(self, grads, axes):
        self.args, self.axes = list(grads), list(axes)
        self.widths = [g.shape[ax] // N_DEV for g, ax in zip(grads, axes)]
        self.out_shape = []
        for g, ax, n in zip(grads, axes, self.widths):
            shp = list(g.shape)
            shp[ax] = n
            self.out_shape.append(jax.ShapeDtypeStruct((N_CHIP, *shp), g.dtype))
        n = len(grads)
        self.scratch = [pltpu.SemaphoreType.DMA((N_CHIP * n,)), pltpu.SemaphoreType.DMA((N_CHIP * n,))]

    def _copies(self, ins, outs, sems):
        x, y, c, _ = _place()
        copies = []
        for a in range(len(self.args)):
            for q in range(N_CHIP):
                src = _window(ins[a], self.axes[a], 2 * q + (1 - c), self.widths[a])
                copies.append(_remote(src, outs[a].at[q], sems[0], sems[1], N_CHIP * a + q, (x, y, 1 - c)))
        return copies

    def start(self, ins, outs, sems):
        for cp in self._copies(ins, outs, sems):
            cp.start()

    def finish(self, ins, outs, sems):
        copies = self._copies(ins, outs, sems)
        for cp in copies:
            cp.wait_recv()
        for cp in copies:
            cp.wait_send()


def _peer_of(k, x, y, c):
    return (1 - x if k & 4 else x, 1 - y if k & 2 else y, 1 - c if k & 1 else c)


def small_exchange(vec, reduce, name):
    r = vec.shape[0]

    def body(v_ref, o_ref, *scratch):
        if reduce:
            buf, send_sems, recv_sems = scratch
        else:
            buf, (send_sems, recv_sems) = o_ref, scratch
        x, y, c, _ = _place()
        me = 4 * x + 2 * y + c
        copies = []
        for k in range(1, N_DEV):
            cp = _remote(v_ref, buf.at[me], send_sems, recv_sems, k - 1, _peer_of(k, x, y, c))
            cp.start()
            copies.append(cp)
        buf[me] = v_ref[...]
        for k in range(1, N_DEV):
            px, py, pc = _peer_of(k, x, y, c)
            slot = buf.at[4 * px + 2 * py + pc]
            _remote(slot, slot, send_sems, recv_sems, k - 1, (px, py, pc)).wait_recv()
        for cp in copies:
            cp.wait_send()
        if reduce:
            tot = buf[0]
            for b in range(1, N_DEV):
                tot = tot + buf[b]
            o_ref[...] = tot

    sems = [pltpu.SemaphoreType.DMA((N_DEV - 1,)), pltpu.SemaphoreType.DMA((N_DEV - 1,))]
    if reduce:
        out_shape = jax.ShapeDtypeStruct((r, LANES), F32)
        scratch = [pltpu.VMEM((N_DEV, r, LANES), F32)] + sems
    else:
        out_shape = jax.ShapeDtypeStruct((N_DEV, r, LANES), F32)
        scratch = sems
    return pl.pallas_call(
        body, out_shape=out_shape, in_specs=[VMEM_FULL], out_specs=VMEM_FULL,
        scratch_shapes=scratch, name=name, compiler_params=_params(None),
    )(vec)


def _row_tile(rows, cols, limit=ELEMWISE_BLOCK_ELEMS):
    best = None
    for tb in range(16, rows + 1, 16):
        if rows % tb == 0 and tb * cols <= limit:
            best = tb
    return best if best is not None else rows


def add_sibling(grad, recv, axis, core):
    nl = grad.shape[0]
    _, _, r, cc = recv.shape
    tb = _row_tile(r, cc, 4 * ELEMWISE_BLOCK_ELEMS)
    per = r // tb

    def body(c_ref, g_ref, r_ref, o_ref):
        del c_ref
        o_ref[...] = (g_ref[...].astype(F32) + r_ref[...].astype(F32)).astype(BF16)

    if axis == 2:
        g_spec = pl.BlockSpec((None, tb, cc), lambda q, l, i, c_ref: (l, i, 2 * q + c_ref[0]))
    else:
        g_spec = pl.BlockSpec((None, tb, cc), lambda q, l, i, c_ref: (l, (2 * q + c_ref[0]) * per + i, 0))
    slot = pl.BlockSpec((None, None, tb, cc), lambda q, l, i, c_ref: (q, l, i, 0))
    return pl.pallas_call(
        body, out_shape=jax.ShapeDtypeStruct(recv.shape, BF16),
        grid_spec=pltpu.PrefetchScalarGridSpec(
            num_scalar_prefetch=1, grid=(N_CHIP, nl, per), in_specs=[g_spec, slot], out_specs=slot),
        name="add_sibling", compiler_params=_params(("parallel", "parallel", "parallel")),
    )(core, grad, recv)


def _adamw(w, g, m, v):
    m = ADAM_B1 * m + (1.0 - ADAM_B1) * g
    v = ADAM_B2 * v + (1.0 - ADAM_B2) * (g * g)
    m_hat = m / (1.0 - ADAM_B1 ** ADAM_STEP)
    v_hat = v / (1.0 - ADAM_B2 ** ADAM_STEP)
    delta = -ADAM_LR * (m_hat / (jnp.sqrt(v_hat) + ADAM_EPS) + ADAM_WD * w)
    return delta, m, v


def adam_from_partials(recv, w, m, v, l0, bufs):
    nl = recv.shape[1]
    _, r, cc = w.shape
    tb = _row_tile(r, cc)

    def body(p0, p1, p2, p3, w_ref, m_ref, v_ref, b0, b1, b2, b3, g_out, d_out, m_out, v_out):
        del b0, b1, b2, b3
        g = p0[...].astype(F32) + p1[...].astype(F32) + p2[...].astype(F32) + p3[...].astype(F32)
        d, mn, vn = _adamw(w_ref[...], g, m_ref[...], v_ref[...])
        g_out[...], d_out[...], m_out[...], v_out[...] = g, d, mn, vn

    slot = lambda q: pl.BlockSpec((None, None, tb, cc), lambda l, i: (q, l, i, 0))
    blk = pl.BlockSpec((None, tb, cc), lambda l, i: (l0 + l, i, 0))
    shp = jax.ShapeDtypeStruct(w.shape, F32)
    return pl.pallas_call(
        body, out_shape=[shp] * 4, grid=(nl, r // tb),
        in_specs=[slot(0), slot(1), slot(2), slot(3), blk, blk, blk] + [ANY] * 4, out_specs=[blk] * 4,
        input_output_aliases={7: 0, 8: 1, 9: 2, 10: 3},
        name="adam_big", compiler_params=_params(("parallel", "parallel")),
    )(recv, recv, recv, recv, w, m, v, *bufs)


def adam_small(w, g, m, v):
    def body(w_ref, g_ref, m_ref, v_ref, d_out, m_out, v_out):
        d_out[...], m_out[...], v_out[...] = _adamw(w_ref[...], g_ref[...], m_ref[...], v_ref[...])

    shp = jax.ShapeDtypeStruct(w.shape, F32)
    return pl.pallas_call(body, out_shape=[shp] * 3, name="adam_small")(w, g, m, v)


def _pack(arrays, multiple=8 * LANES):
    flat = jnp.concatenate([a.reshape(-1) for a in arrays])
    pad = (-flat.shape[0]) % multiple
    if pad:
        flat = jnp.concatenate([flat, jnp.zeros((pad,), flat.dtype)])
    return flat.reshape(-1, LANES)


def _unpack(packed, shapes):
    flat = packed.reshape(packed.shape[:-2] + (-1,))
    out, off = [], 0
    for shp in shapes:
        n = math.prod(shp)
        out.append(flat[..., off:off + n].reshape(packed.shape[:-2] + tuple(shp)))
        off += n
    return out


def _unshard_last(stacked):
    moved = jnp.moveaxis(stacked, 0, -2)
    return moved.reshape(moved.shape[:-2] + (-1,))


def _shard_last(full, block):
    n = full.shape[-1] // N_DEV
    return lax.dynamic_slice_in_dim(full, block * n, n, axis=full.ndim - 1)


BIG_AXIS = {"w_in": 2, "w_out": 1, "w_qkv": 2, "w_so": 1, "w_pool": 1, "w_up": 2, "w_down": 1}
BIG_ORDER = ("w_in", "w_out", "w_qkv", "w_so", "w_pool", "w_up", "w_down")

GATHER_HOSTS = {
    "hg_in_0": (("w_out", 0),),
    "hgrn_fwd_0": (("w_up", 0),),
    "ffn_up_0": (("w_down", 0),),
    "ffn_down_0": (("w_qkv", 0),),
    "sb_qkv_1": (("w_so", 0),),
    "sba_fwd_1": (("w_up", 1), ("w_pool", 0), ("w_up", 2)),
    "ffn_up_1": (("w_down", 1),),
    "ffn_down_1": (("w_down", 2),),
    "ffn_up_2": (("w_in", 1),),
    "ffn_down_2": (("w_out", 1),),
    "hgrn_fwd_3": (("w_up", 3),),
    "ffn_up_3": (("w_down", 3),),
}


MEGA = 1 << 20
EXCHANGE_BUDGET = {"d_u2": 13 * MEGA, "dw_up": 17 * MEGA, "hgrn_bwd": 30 * MEGA, "sba_bwd": 70 * MEGA,
                   "d_u1_0": 17 * MEGA}


class LocalWeights:
    def __init__(self, full):
        self.full, self.grads = full, {}

    def gather_plan(self, host):
        return None

    def gathered(self, outs):
        pass

    def weight(self, kind, l):
        return self.full[kind][l] if kind != "w_pool" else self.full[kind]

    def grad(self, kind, l, g):
        self.grads[(kind, l)] = g

    def exchange_plan(self, host=None):
        return None

    def exchanged(self, outs):
        pass


class MeshWeights:
    def __init__(self, shards16, w, m, v, core):
        self.shards, self.w, self.m, self.v, self.core = shards16, w, m, v, core
        self.full, self.pending, self.flying, self.keys = {}, [], [], None
        self.out = {k: [lax.empty(w[k].shape, F32) for _ in range(4)] for k in BIG_ORDER}

    def _items(self, keys):
        return [(self.shards[k], 0 if k == "w_pool" else l, 4 if k == "w_pool" else 1, BIG_AXIS[k]) for k, l in keys]

    def gather_plan(self, host):
        self.keys = [key for key in GATHER_HOSTS.get(host, ()) if key[0] in self.shards
                     and key[1] < (1 if key[0] == "w_pool" else self.shards[key[0]].shape[0])]
        return GatherPlan(self._items(self.keys)) if self.keys else None

    def gathered(self, outs):
        for (k, l), o in zip(self.keys, outs):
            self.full[(k, l)] = o if k == "w_pool" else o[0]
        self.keys = None

    def weight(self, kind, l):
        if (kind, l) not in self.full:
            (out,) = run_plan(GatherPlan(self._items([(kind, l)])), f"gather_{kind}_{l}")
            self.full[(kind, l)] = out if kind == "w_pool" else out[0]
        return self.full[(kind, l)]

    def grad(self, kind, l, g):
        self.pending.append((kind, l, g if g.ndim == 3 else g[None]))

    def exchange_plan(self, host=None):
        budget = next((b for prefix, b in EXCHANGE_BUDGET.items() if host.startswith(prefix)), 0) if host else None
        take, keep, used = [], [], 0
        for item in self.pending:
            if budget is None or used + item[2].size <= budget:
                take.append(item)
                used += item[2].size
            else:
                keep.append(item)
        self.pending = keep
        if not take:
            return None
        self.flying = take
        grads = [g for _, _, g in take]
        axes = [BIG_AXIS[k] for k, _, _ in take]
        recv = run_plan(SiblingPlan(grads, axes), "sibling_exchange")
        return ExchangePlan([add_sibling(g, r, ax, self.core) for g, r, ax in zip(grads, recv, axes)])

    def exchanged(self, outs):
        for (k, l, _), r in zip(self.flying, outs):
            self.out[k] = adam_from_partials(r, self.w[k], self.m[k], self.v[k], l, self.out[k])
        self.flying = []

    def finish(self):
        plan = self.exchange_plan()
        if plan is not None:
            self.exchanged(run_plan(plan, "chip_exchange_tail"))
        return self.out


def train_step(x, target, norm_g, lb_logits, onorm_g, pool_scale, conv_w, conv_b, wts):
    t, d = x.shape
    depth = norm_g.shape[0]
    ng = lambda i, j: norm_g[i, j].reshape(1, d)
    lbs = lower_bounds_fwd(lb_logits)
    saved = []
    h = x
    _, u16, u32 = res_norm(h, None, None, ng(0, 0), "norm_in")

    def hosted(call, *args, **kw):
        plan = wts.gather_plan(kw["name"])
        out, extra = call(*args, plan=plan, **kw)
        if plan is not None:
            wts.gathered(extra)
        return out

    def project(a, kind, l, name):
        plan = wts.gather_plan(name)
        if plan is None:
            return matmul(a, wts.weight(kind, l), "nn", F32, name)
        out, extra = matmul(a, wts.weight(kind, l), "nn", F32, name, plan=plan)
        wts.gathered(extra)
        return out

    for i in range(depth):
        kind, j = i % 3, i // 3
        s = {"h_in": h, "u1": u16}
        if kind == 0:
            s["proj"] = project(u16, "w_in", j, f"hg_in_{i}")
            s["y"] = hosted(hgrn_fwd, s["proj"], lbs[i].reshape(1, -1), onorm_g[j].reshape(1, -1), i > 0,
                            name=f"hgrn_fwd_{i}")
            mix = project(s["y"], "w_out", j, f"hg_out_{i}")
        elif kind == 1:
            s["qkv"] = project(u16, "w_qkv", j, f"sb_qkv_{i}")
            s["o"] = hosted(sba_fwd, s["qkv"], name=f"sba_fwd_{i}")
            mix = project(s["o"], "w_so", j, f"sb_out_{i}")
        else:
            s["u1f"] = u32
            mix = pool_fwd(u32, wts.weight("w_pool", j), pool_scale[j].reshape(1, d), f"pool_fwd_{i}")
        s["mix"] = mix
        h_mid, u2, _ = res_norm(h, mix, ng(i, 1), ng(i, 2), f"norm_mid_{i}")
        s["h_mid"], s["u2"] = h_mid, u2
        s["up"] = project(u2, "w_up", i, f"ffn_up_{i}")
        s["act"] = conv_glu_fwd(s["up"], conv_w[i], conv_b[i].reshape(1, -1), f"glu_fwd_{i}")
        s["f"] = project(s["act"], "w_down", i, f"ffn_down_{i}")
        nxt = ng(i + 1, 0) if i + 1 < depth else None
        h, u16, u32 = res_norm(h_mid, s["f"], ng(i, 3), nxt, f"norm_out_{i}",
                               want_f32=(nxt is not None and (i + 1) % 3 == 2))
        saved.append(s)

    loss_acc, dh = loss_head(h, target, "loss_head")

    d_norm = [[None] * 4 for _ in range(depth)]
    d_lbs = jnp.zeros_like(lbs)
    d_onorm = [None] * onorm_g.shape[0]
    d_pscale = [None] * pool_scale.shape[0]
    d_cw, d_cb = [None] * depth, [None] * depth

    def exchanging(call, *args, **kw):
        plan = wts.exchange_plan(kw["name"])
        out, extra = call(*args, plan=plan, **kw)
        if plan is not None:
            wts.exchanged(extra)
        return out

    def bmm(a, b, mode, dtype, name):
        plan = wts.exchange_plan(name)
        if plan is None:
            return matmul(a, b, mode, dtype, name)
        out, extra = matmul(a, b, mode, dtype, name, plan=plan)
        wts.exchanged(extra)
        return out

    for i in reversed(range(depth)):
        kind, j = i % 3, i // 3
        s = saved[i]
        df, d_norm[i][3] = norm_bwd(s["f"], ng(i, 3), dh, None, BF16, f"nb_out_{i}")
        dact = bmm(df, wts.weight("w_down", i), "nt", F32, f"d_act_{i}")
        wts.grad("w_down", i, bmm(s["act"], df, "tn", BF16, f"dw_down_{i}"))
        dup, dcw, dcb = conv_glu_bwd(s["up"], conv_w[i], conv_b[i].reshape(1, -1), dact, f"glu_bwd_{i}")
        d_cw[i] = jnp.moveaxis(dcw, 0, 1).reshape(CONV_WIDTH, -1)
        d_cb[i] = dcb.reshape(-1)
        du2 = bmm(Split(dup), wts.weight("w_up", i), "nt", F32, f"d_u2_{i}")
        wts.grad("w_up", i, bmm(s["u2"], Split(dup), "tn", BF16, f"dw_up_{i}"))
        dh_mid, d_norm[i][2] = norm_bwd(s["h_mid"], ng(i, 2), du2, dh, F32, f"nb_mid_{i}")
        dm, d_norm[i][1] = norm_bwd(s["mix"], ng(i, 1), dh_mid, None, F32 if kind == 2 else BF16, f"nb_mix_{i}")
        if kind == 0:
            dy = matmul(dm, wts.weight("w_out", j), "nt", F32, f"d_y_{i}")
            wts.grad("w_out", j, matmul(s["y"], dm, "tn", BF16, f"dw_hgout_{i}"))
            dproj, d_onorm[j], dlb = exchanging(hgrn_bwd, s["proj"], lbs[i].reshape(1, -1),
                                                onorm_g[j].reshape(1, -1), dy, i > 0, name=f"hgrn_bwd_{i}")
            d_lbs = d_lbs.at[i].set(dlb[0])
            wts.grad("w_in", j, bmm(s["u1"], Split(dproj), "tn", BF16, f"dw_hgin_{i}"))
            du1 = bmm(Split(dproj), wts.weight("w_in", j), "nt", F32, f"d_u1_{i}")
        elif kind == 1:
            do = matmul(dm, wts.weight("w_so", j), "nt", F32, f"d_o_{i}")
            wts.grad("w_so", j, matmul(s["o"], dm, "tn", BF16, f"dw_sbout_{i}"))
            dqkv = exchanging(sba_bwd, s["qkv"], s["o"], do, name=f"sba_bwd_{i}")
            du1 = bmm(Split(dqkv), wts.weight("w_qkv", j), "nt", F32, f"d_u1_{i}")
            wts.grad("w_qkv", j, bmm(s["u1"], Split(dqkv), "tn", BF16, f"dw_sbqkv_{i}"))
        else:
            du1, g_pool, d_pscale[j] = pool_bwd(s["u1f"], wts.weight("w_pool", j), pool_scale[j].reshape(1, d),
                                                dm, f"pool_bwd_{i}")
            wts.grad("w_pool", j, g_pool)
        dh, d_norm[i][0] = norm_bwd(s["h_in"], ng(i, 0), du1, dh_mid, F32, f"nb_in_{i}")

    small = {
        "norm_g": jnp.stack([jnp.stack([v.reshape(d) for v in row]) for row in d_norm]),
        "lb_logits": lower_bounds_bwd(lb_logits, d_lbs),
        "onorm_g": jnp.stack([v.reshape(-1) for v in d_onorm]),
        "pool_scale": jnp.stack([v.reshape(-1) for v in d_pscale]),
        "conv_w": jnp.stack(d_cw),
        "conv_b": jnp.stack(d_cb),
    }
    return loss_acc, dh, small


SMALL_SHARDED = ("norm_g", "onorm_g", "pool_scale", "conv_w")
SMALL_ORDER = ("norm_g", "lb_logits", "onorm_g", "pool_scale", "conv_w", "conv_b")


def kernel(x, norm_g, hgrn_lb_logits, hgrn_w_in, hgrn_onorm_g, hgrn_w_out, sba_w_qkv, sba_w_out, pool_w, pool_scale, ffn_w_up, ffn_conv_w, ffn_conv_b, ffn_w_down, loss_target, m_norm_g, m_hgrn_lb_logits, m_hgrn_w_in, m_hgrn_onorm_g, m_hgrn_w_out, m_sba_w_qkv, m_sba_w_out, m_pool_w, m_pool_scale, m_ffn_w_up, m_ffn_conv_w, m_ffn_conv_b, m_ffn_w_down, v_norm_g, v_hgrn_lb_logits, v_hgrn_w_in, v_hgrn_onorm_g, v_hgrn_w_out, v_sba_w_qkv, v_sba_w_out, v_pool_w, v_pool_scale, v_ffn_w_up, v_ffn_conv_w, v_ffn_conv_b, v_ffn_w_down):
    cx, cy, cc = lax.axis_index("x"), lax.axis_index("y"), lax.axis_index("c")
    block = 4 * cx + 2 * cy + cc
    core = cc.astype(jnp.int32).reshape(1)

    pool3 = lambda a: a.reshape(a.shape[0] * a.shape[1], a.shape[2], a.shape[3])
    big_w = dict(zip(BIG_ORDER, [hgrn_w_in, hgrn_w_out, sba_w_qkv, sba_w_out, pool3(pool_w), ffn_w_up, ffn_w_down]))
    big_m = dict(zip(BIG_ORDER, [m_hgrn_w_in, m_hgrn_w_out, m_sba_w_qkv, m_sba_w_out, pool3(m_pool_w), m_ffn_w_up,
                                 m_ffn_w_down]))
    big_v = dict(zip(BIG_ORDER, [v_hgrn_w_in, v_hgrn_w_out, v_sba_w_qkv, v_sba_w_out, pool3(v_pool_w), v_ffn_w_up,
                                 v_ffn_w_down]))

    sharded = {"norm_g": norm_g, "onorm_g": hgrn_onorm_g, "pool_scale": pool_scale, "conv_w": ffn_conv_w}
    gathered = small_exchange(_pack([sharded[n] for n in SMALL_SHARDED]), False, "gather_small")
    parts = _unpack(gathered, [sharded[n].shape for n in SMALL_SHARDED])
    full = {n: _unshard_last(p) for n, p in zip(SMALL_SHARDED, parts)}

    wts = MeshWeights({k: w.astype(BF16) for k, w in big_w.items()}, big_w, big_m, big_v, core)
    loss_acc, grad_x, small_g = train_step(
        x[0], loss_target[0], full["norm_g"], hgrn_lb_logits, full["onorm_g"], full["pool_scale"],
        full["conv_w"], ffn_conv_b, wts)
    loss = lax.psum(loss_acc[0, 0], ("x", "y", "c"))

    shapes = [small_g[n].shape for n in SMALL_ORDER]
    summed = _unpack(small_exchange(_pack([small_g[n] for n in SMALL_ORDER]), True, "reduce_small"), shapes)
    sg = {n: (_shard_last(g, block) if n in SMALL_SHARDED else g) for n, g in zip(SMALL_ORDER, summed)}
    sw = {"norm_g": norm_g, "lb_logits": hgrn_lb_logits, "onorm_g": hgrn_onorm_g, "pool_scale": pool_scale,
          "conv_w": ffn_conv_w, "conv_b": ffn_conv_b}
    sm = {"norm_g": m_norm_g, "lb_logits": m_hgrn_lb_logits, "onorm_g": m_hgrn_onorm_g, "pool_scale": m_pool_scale,
          "conv_w": m_ffn_conv_w, "conv_b": m_ffn_conv_b}
    sv = {"norm_g": v_norm_g, "lb_logits": v_hgrn_lb_logits, "onorm_g": v_hgrn_onorm_g, "pool_scale": v_pool_scale,
          "conv_w": v_ffn_conv_w, "conv_b": v_ffn_conv_b}
    sshapes = [sw[n].shape for n in SMALL_ORDER]
    packed = [_pack([dct[n] for n in SMALL_ORDER]) for dct in (sw, sg, sm, sv)]
    s_delta, s_m, s_v = [dict(zip(SMALL_ORDER, _unpack(p, sshapes))) for p in adam_small(*packed)]

    upd = wts.finish()
    b_grad, b_delta, b_m, b_v = [[upd[k][n] for k in BIG_ORDER] for n in range(4)]

    def tree(small, bigs):
        bg = list(bigs)
        bg[4] = bg[4].reshape(pool_w.shape)
        return (small["norm_g"], small["lb_logits"], bg[0], small["onorm_g"], bg[1], bg[2], bg[3], bg[4],
                small["pool_scale"], bg[5], small["conv_w"], small["conv_b"], bg[6])

    return (loss, grad_x[None], *tree(sg, b_grad), *tree(s_delta, b_delta), *tree(s_m, b_m), *tree(s_v, b_v))
```

```python
import functools
import math

import jax
import jax.numpy as jnp
from jax import lax
from jax.experimental import pallas as pl
from jax.experimental.pallas import tpu as pltpu

F32 = jnp.float32
BF16 = jnp.bfloat16
HI = lax.Precision.HIGHEST
MESH = pl.DeviceIdType.MESH
ANY = pl.BlockSpec(memory_space=pl.ANY)
VMEM_FULL = pl.BlockSpec(memory_space=pltpu.VMEM)

NORM_EPS = 1e-6
HEAD = 128
HG_CHUNK = 128
HG_SUB = 32
HG_MAX_EXPONENT = 80.0
SB_BLOCK = 256
SB_QROWS = 256
POOL_WINDOWS = (2, 4, 8, 16)
CONV_WIDTH = 3
ROW_TILE = 256
N_DEV = 8
N_CHIP = 4

ADAM_LR = 0.001
ADAM_B1 = 0.9
ADAM_B2 = 0.999
ADAM_EPS = 1e-08
ADAM_WD = 0.01
ADAM_STEP = 10

VMEM_LIMIT = 48 * 1024 * 1024
LANES = 128
ELEMWISE_BLOCK_ELEMS = 256 * 1024


def _params(sem=None, vmem=VMEM_LIMIT):
    return pltpu.CompilerParams(dimension_semantics=sem, vmem_limit_bytes=vmem)


def _tile(n, prefs=(1024, 512, 256, 128)):
    for p in prefs:
        if n % p == 0:
            return p
    return n


def _dot(a, b, prec=None):
    return jnp.dot(a, b, precision=prec, preferred_element_type=F32)


def _dot_nt(a, b, prec=None):
    return lax.dot_general(a, b, (((1,), (1,)), ((), ())), precision=prec, preferred_element_type=F32)


def _dot_tn(a, b, prec=None):
    return lax.dot_general(a, b, (((0,), (0,)), ((), ())), precision=prec, preferred_element_type=F32)


def _split_dot(x, tri, parts, left=False):
    tot, rest = None, x
    for p in range(parts):
        h = rest.astype(BF16)
        d = _dot(tri, h) if left else _dot(h, tri)
        tot = d if tot is None else tot + d
        if p + 1 < parts:
            rest = rest - h.astype(F32)
    return tot


def _dot1(a, b, fn):
    return fn(a.astype(BF16), b.astype(BF16))


def _dot3(a, b, fn):
    ah, bh = a.astype(BF16), b.astype(BF16)
    al, bl = (a - ah.astype(F32)).astype(BF16), (b - bh.astype(F32)).astype(BF16)
    return fn(ah, bh) + fn(ah, bl) + fn(al, bh)


def _sigmoid(x):
    return jax.nn.sigmoid(x)


def _softplus(x):
    return jnp.maximum(x, 0.0) + jnp.log1p(jnp.exp(-jnp.abs(x)))


class Split:
    def __init__(self, arr):
        self.arr = arr
        self.shape = (arr.shape[1], arr.shape[0] * arr.shape[2])
        self.part = arr.shape[2]


class Plain:
    def __init__(self, arr):
        self.arr = arr
        self.shape = arr.shape
        self.part = None


def _wrap(op):
    return op if isinstance(op, (Split, Plain)) else Plain(op)


def _op_spec(op, br, bc, rc_of_grid):
    if isinstance(op, Split):
        per = op.part // bc

        def imap(i, j, k):
            r, c = rc_of_grid(i, j, k)
            return (lax.div(c, per), r, lax.rem(c, per))
        return pl.BlockSpec((None, br, bc), imap)
    return pl.BlockSpec((br, bc), rc_of_grid)


def _hosted(body, n_in, n_out, plan, step_of_grid):
    if plan is None:
        return body
    pi, po, ps = len(plan.args), len(plan.out_shape), len(plan.scratch)

    def wrapped(*refs):
        refs = list(refs)
        ins, pins = refs[:n_in], refs[n_in:n_in + pi]
        outs = refs[n_in + pi:n_in + pi + n_out]
        pouts = refs[n_in + pi + n_out:n_in + pi + n_out + po]
        scr, pscr = refs[n_in + pi + n_out + po:len(refs) - ps], refs[len(refs) - ps:]
        step, nsteps = step_of_grid()

        @pl.when(step == 0)
        def _():
            plan.start(pins, pouts, pscr)

        if plan.has_mid:
            @pl.when(step == min((2 * nsteps) // 3, nsteps - 1))
            def _():
                plan.mid(pins, pouts, pscr)

        body(*ins, *outs, *scr)

        @pl.when(step == nsteps - 1)
        def _():
            plan.finish(pins, pouts, pscr)

    return wrapped


def _host_call(body, n_in, plan, step_of_grid, *, out_shape, grid, in_specs, out_specs, scratch_shapes, name, sem, args):
    single = not isinstance(out_shape, (list, tuple))
    out_shape = [out_shape] if single else list(out_shape)
    out_specs = [out_specs] if single else list(out_specs)
    n_out = len(out_shape)
    in_specs, scratch_shapes, args = list(in_specs), list(scratch_shapes), list(args)
    if plan is not None:
        in_specs += [ANY] * len(plan.args)
        args += plan.args
        out_shape += plan.out_shape
        out_specs += [ANY] * len(plan.out_shape)
        scratch_shapes += plan.scratch
        sem = ("arbitrary",) * len(grid)
    outs = pl.pallas_call(
        _hosted(body, n_in, n_out, plan, step_of_grid), out_shape=out_shape, grid=grid,
        in_specs=in_specs, out_specs=out_specs, scratch_shapes=scratch_shapes,
        name=name, compiler_params=_params(sem),
    )(*args)
    host = outs[0] if single else list(outs[:n_out])
    return host, list(outs[n_out:])


def matmul(a, b, mode, out_dtype, name, plan=None):
    a, b = _wrap(a), _wrap(b)
    if mode == "nn":
        (m, kd), (kd2, n) = a.shape, b.shape
    elif mode == "nt":
        (m, kd), (n, kd2) = a.shape, b.shape
    else:
        (kd, m), (kd2, n) = a.shape, b.shape
    assert kd == kd2, (mode, a.shape, b.shape)

    def dim_tile(full, ops_on_cols, prefs=(1024, 512, 256, 128)):
        base = full
        for op in ops_on_cols:
            if op.part is not None:
                base = math.gcd(base, op.part)
        return _tile(base, prefs)

    tm = dim_tile(m, [a] if mode == "tn" else [])
    tn = dim_tile(n, [b] if mode in ("nn", "tn") else [])
    tk = dim_tile(kd, ([a] if mode in ("nn", "nt") else []) + ([b] if mode == "nt" else []),
                  prefs=(2048, 2816, 1024, 512, 256, 128))
    nk = kd // tk

    if mode == "nn":
        a_spec = _op_spec(a, tm, tk, lambda i, j, k: (i, k))
        b_spec = _op_spec(b, tk, tn, lambda i, j, k: (k, j))
        dot = _dot
    elif mode == "nt":
        a_spec = _op_spec(a, tm, tk, lambda i, j, k: (i, k))
        b_spec = _op_spec(b, tn, tk, lambda i, j, k: (j, k))
        dot = _dot_nt
    else:
        a_spec = _op_spec(a, tk, tm, lambda i, j, k: (k, i))
        b_spec = _op_spec(b, tk, tn, lambda i, j, k: (k, j))
        dot = _dot_tn

    def body(a_ref, b_ref, o_ref, *acc):
        part = dot(a_ref[...].astype(BF16), b_ref[...].astype(BF16))
        if nk == 1:
            o_ref[...] = part.astype(o_ref.dtype)
            return
        (acc_ref,) = acc
        k = pl.program_id(2)

        @pl.when(k == 0)
        def _():
            acc_ref[...] = part

        @pl.when(k > 0)
        def _():
            acc_ref[...] += part

        @pl.when(k == nk - 1)
        def _():
            o_ref[...] = acc_ref[...].astype(o_ref.dtype)

    gi, gj = m // tm, n // tn

    def step_of_grid():
        return (pl.program_id(0) * gj + pl.program_id(1)) * nk + pl.program_id(2), gi * gj * nk

    out, extra = _host_call(
        body, 2, plan, step_of_grid, out_shape=jax.ShapeDtypeStruct((m, n), out_dtype), grid=(gi, gj, nk),
        in_specs=[a_spec, b_spec], out_specs=pl.BlockSpec((tm, tn), lambda i, j, k: (i, j)),
        scratch_shapes=[pltpu.VMEM((tm, tn), F32)] if nk > 1 else [], name=name,
        sem=("parallel", "parallel", "arbitrary"), args=[a.arr, b.arr])
    return out if plan is None else (out, extra)


def _rms(x, g):
    r = lax.rsqrt(jnp.mean(x * x, axis=-1, keepdims=True) + NORM_EPS)
    return x * r * g


def res_norm(h, m, g_a, g_b, name, want_f32=False):
    t, d = h.shape
    tr = min(ROW_TILE, t)
    has_m, has_u = m is not None, g_b is not None
    row = pl.BlockSpec((tr, d), lambda i: (i, 0))
    vec = pl.BlockSpec((1, d), lambda i: (0, 0))

    def body(*refs):
        refs = list(refs)
        h_ref = refs.pop(0)
        hn = h_ref[...]
        if has_m:
            m_ref, ga_ref = refs.pop(0), refs.pop(0)
            hn = hn + _rms(m_ref[...], ga_ref[...])
        if has_u:
            gb_ref = refs.pop(0)
        if has_m:
            refs.pop(0)[...] = hn
        if has_u:
            u = _rms(hn, gb_ref[...])
            refs.pop(0)[...] = u.astype(BF16)
            if want_f32:
                refs.pop(0)[...] = u

    args, in_specs, out_shape, out_specs = [h], [row], [], []
    if has_m:
        args += [m, g_a]
        in_specs += [row, vec]
        out_shape.append(jax.ShapeDtypeStruct((t, d), F32))
        out_specs.append(row)
    if has_u:
        args.append(g_b)
        in_specs.append(vec)
        out_shape.append(jax.ShapeDtypeStruct((t, d), BF16))
        out_specs.append(row)
        if want_f32:
            out_shape.append(jax.ShapeDtypeStruct((t, d), F32))
            out_specs.append(row)
    outs = list(pl.pallas_call(
        body, out_shape=out_shape, grid=(t // tr,), in_specs=in_specs, out_specs=out_specs,
        name=name, compiler_params=_params(("parallel",)),
    )(*args))
    h_new = outs.pop(0) if has_m else None
    u16 = outs.pop(0) if has_u else None
    u32 = outs.pop(0) if (has_u and want_f32) else None
    return h_new, u16, u32


def norm_bwd(x, g, dy, add, out_dtype, name):
    t, d = x.shape
    tr = min(ROW_TILE, t)
    has_add = add is not None
    row = pl.BlockSpec((tr, d), lambda i: (i, 0))
    vec = pl.BlockSpec((1, d), lambda i: (0, 0))

    def body(*refs):
        if has_add:
            x_ref, g_ref, dy_ref, add_ref, dx_ref, dg_ref = refs
        else:
            x_ref, g_ref, dy_ref, dx_ref, dg_ref = refs
        xv = x_ref[...]
        dyv = dy_ref[...].astype(F32)
        r = lax.rsqrt(jnp.mean(xv * xv, axis=-1, keepdims=True) + NORM_EPS)
        gy = dyv * g_ref[...]
        dx = r * gy - xv * (r * r * r * jnp.mean(gy * xv, axis=-1, keepdims=True))
        if has_add:
            dx = dx + add_ref[...]
        dx_ref[...] = dx.astype(dx_ref.dtype)

        @pl.when(pl.program_id(0) == 0)
        def _():
            dg_ref[...] = jnp.zeros_like(dg_ref)

        dg_ref[...] += jnp.sum(dyv * xv * r, axis=0, keepdims=True)

    args = [x, g, dy] + ([add] if has_add else [])
    in_specs = [row, vec, row] + ([row] if has_add else [])
    return pl.pallas_call(
        body, out_shape=[jax.ShapeDtypeStruct((t, d), out_dtype), jax.ShapeDtypeStruct((1, d), F32)],
        grid=(t // tr,), in_specs=in_specs, out_specs=[row, vec],
        name=name, compiler_params=_params(("arbitrary",)),
    )(*args)


def loss_head(y, target, name):
    t, d = y.shape
    tr = min(ROW_TILE, t)
    row = pl.BlockSpec((tr, d), lambda i: (i, 0))
    acc = pl.BlockSpec((8, LANES), lambda i: (0, 0))

    def body(y_ref, t_ref, loss_ref, dy_ref):
        e = y_ref[...] - t_ref[...]
        dy_ref[...] = e * (1.0 / d)

        @pl.when(pl.program_id(0) == 0)
        def _():
            loss_ref[...] = jnp.zeros_like(loss_ref)

        loss_ref[...] += jnp.sum(e * e) * (0.5 / d)

    return pl.pallas_call(
        body, out_shape=[jax.ShapeDtypeStruct((8, LANES), F32), jax.ShapeDtypeStruct((t, d), F32)],
        grid=(t // tr,), in_specs=[row, row], out_specs=[acc, row],
        name=name, compiler_params=_params(("arbitrary",)),
    )(y, target)


def _depth_softmax(ref, depth):
    rows = [ref[i:i + 1, :] for i in range(depth)]
    mx = functools.reduce(jnp.maximum, rows)
    ex = [jnp.exp(r - mx) for r in rows]
    tot = functools.reduce(lambda p, q: p + q, ex)
    return [e / tot for e in ex]


def lower_bounds_fwd(logits):
    depth, kw = logits.shape

    def body(l_ref, o_ref):
        s = _depth_softmax(l_ref, depth)
        run = jnp.zeros_like(s[0])
        o_ref[0:1, :] = run
        for i in range(1, depth):
            run = run + s[i]
            o_ref[i:i + 1, :] = run

    return pl.pallas_call(body, out_shape=jax.ShapeDtypeStruct((depth, kw), F32), name="lb_fwd")(logits)


def lower_bounds_bwd(logits, dlb):
    depth, kw = logits.shape

    def body(l_ref, d_ref, o_ref):
        s = _depth_softmax(l_ref, depth)
        ds = [jnp.zeros_like(s[0]) for _ in range(depth)]
        run = jnp.zeros_like(s[0])
        for j in range(depth - 1, 0, -1):
            run = run + d_ref[j:j + 1, :]
            ds[j] = run
        dot = functools.reduce(lambda p, q: p + q, [s[j] * ds[j] for j in range(depth)])
        for j in range(depth):
            o_ref[j:j + 1, :] = s[j] * (ds[j] - dot)

    return pl.pallas_call(body, out_shape=jax.ShapeDtypeStruct((depth, kw), F32), name="lb_bwd")(logits, dlb)


def _hg_gates(qp, fp, lb_row, has_lb):
    sig = _sigmoid(fp)
    nsig = _sigmoid(-fp)
    ls = jnp.minimum(fp, 0.0) - jnp.log1p(jnp.exp(-jnp.abs(fp)))
    if has_lb:
        a = jnp.log(lb_row)
        bb = jnp.log1p(-lb_row) + ls
        g = jnp.maximum(a, bb) + jnp.log1p(jnp.exp(-jnp.abs(a - bb)))
        w = jnp.exp(bb - g)
        k = (1.0 - lb_row) * nsig
    else:
        g, w, k = ls, None, nsig
    q = qp * _sigmoid(qp)
    return q, k, g, sig, nsig, w


HG_HEADS_PER_STEP = 2


def _heads_per_step(nh):
    return HG_HEADS_PER_STEP if nh % HG_HEADS_PER_STEP == 0 else 1


def _head_views(refs, hh, lanes, lead):
    cs = pl.ds(hh * HEAD, HEAD)
    out = []
    for i, r in enumerate(refs):
        if i in lead:
            out.append(r.at[hh])
        elif i in lanes:
            out.append(r.at[(slice(None),) * (len(r.shape) - 1) + (cs,)])
        else:
            out.append(r)
    return out


def _hg_masks():
    c = HG_CHUNK
    row = lax.broadcasted_iota(jnp.int32, (c, c), 0)
    col = lax.broadcasted_iota(jnp.int32, (c, c), 1)
    lower = (col <= row).astype(BF16)
    upper = (col >= row).astype(BF16)
    krow = lax.broadcasted_iota(jnp.int32, (c, HEAD), 0)
    arow = lax.broadcasted_iota(jnp.int32, (HG_SUB, c), 0)
    acol = lax.broadcasted_iota(jnp.int32, (HG_SUB, c), 1)
    return lower, upper, krow, arow, acol


def _hg_sub(i, q, k, b, b_ref, krow, arow, acol):
    r0 = i * HG_SUB
    m = b_ref[r0:r0 + 1, :]
    ebq = jnp.exp(b[r0:r0 + HG_SUB, :] - m)
    qh = (q[r0:r0 + HG_SUB, :] * ebq).astype(BF16)
    ek = jnp.exp(jnp.where(krow < r0 + HG_SUB, jnp.minimum(m - b, HG_MAX_EXPONENT), 0.0))
    kh = (k * ek).astype(BF16)
    mask = acol <= arow + r0
    amat = jnp.where(mask, _dot_nt(qh, kh), 0.0).astype(BF16)
    return ebq, qh, ek, kh, mask, amat


def hgrn_fwd(proj, lb, onorm_g, has_lb, name, plan=None):
    t, w4 = proj.shape
    kw = w4 // 4
    nh, nc, c = kw // HEAD, t // HG_CHUNK, HG_CHUNK
    hpb = _heads_per_step(nh)
    wide = hpb * HEAD

    def body(*refs):
        refs[7][...] = jnp.zeros_like(refs[7])
        heads = [one_head(*_head_views(refs, hh, lanes=(0, 1, 2, 3, 4, 5, 6), lead=(7, 8))) for hh in range(hpb)]

        def chunks(ci, carry):
            for chunk in heads:
                chunk(ci)
            return carry

        lax.fori_loop(0, nc, chunks, 0)

    def one_head(qp_ref, fp_ref, iv_ref, gp_ref, lb_ref, on_ref, y_ref, st_ref, b_ref):
        lower, _, krow, arow, acol = _hg_masks()
        lb_row, gam = lb_ref[...], on_ref[...]

        def chunk(ci):
            rs = pl.ds(pl.multiple_of(ci * c, c), c)
            v = iv_ref[rs, :].astype(BF16)
            gp = gp_ref[rs, :]
            q, k, g, _, _, _ = _hg_gates(qp_ref[rs, :], fp_ref[rs, :], lb_row, has_lb)
            b = _split_dot(g, lower, 3, left=True)
            b_ref[...] = b
            st = st_ref[...]
            o = _dot_nt((q * jnp.exp(b)).astype(BF16), st.astype(BF16))
            amats = [_hg_sub(i, q, k, b, b_ref, krow, arow, acol)[-1] for i in range(c // HG_SUB)]
            o = o + _dot(jnp.concatenate(amats, axis=0), v)
            bl = b_ref[c - 1:c, :]
            st_ref[...] = jnp.exp(bl) * st + _dot_tn(v, (k * jnp.exp(bl - b)).astype(BF16))
            r = lax.rsqrt(jnp.mean(o * o, axis=-1, keepdims=True) + NORM_EPS)
            y_ref[rs, :] = (o * r * gam * (gp * _sigmoid(gp))).astype(BF16)

        return chunk

    ns = nh // hpb
    col = lambda p: pl.BlockSpec((t, wide), lambda h: (0, p * ns + h))
    vec = pl.BlockSpec((1, wide), lambda h: (0, h))
    return _host_call(
        body, 6, plan, lambda: (pl.program_id(0), ns),
        out_shape=jax.ShapeDtypeStruct((t, kw), BF16), grid=(ns,),
        in_specs=[col(0), col(1), col(2), col(3), vec, vec],
        out_specs=pl.BlockSpec((t, wide), lambda h: (0, h)),
        scratch_shapes=[pltpu.VMEM((hpb, HEAD, HEAD), F32), pltpu.VMEM((hpb, c, HEAD), F32)],
        name=name, sem=("parallel",), args=[proj, proj, proj, proj, lb, onorm_g])


def hgrn_bwd(proj, lb, onorm_g, dy, has_lb, name, plan=None):
    t, w4 = proj.shape
    kw = w4 // 4
    nh, nc, c = kw // HEAD, t // HG_CHUNK, HG_CHUNK
    hpb = _heads_per_step(nh)
    wide = hpb * HEAD

    def body(*refs):
        for i in (8, 9, 13, 14):
            refs[i][...] = jnp.zeros_like(refs[i])
        heads = [one_head(*_head_views(refs, hh, lanes=(0, 1, 2, 3, 4, 5, 6, 7, 8, 9, 11), lead=(10, 12, 13, 14)))
                 for hh in range(hpb)]

        def fwd_chunks(ci, states):
            return tuple(fwd(ci, st) for (fwd, _), st in zip(heads, states))

        lax.fori_loop(0, nc, fwd_chunks, tuple(jnp.zeros((HEAD, HEAD), F32) for _ in heads))

        def bwd_chunks(step, carry):
            for _, bwd in heads:
                bwd(nc - 1 - step)
            return carry

        lax.fori_loop(0, nc, bwd_chunks, 0)

    def one_head(qp_ref, fp_ref, iv_ref, gp_ref, lb_ref, on_ref, dy_ref,
                 dp_ref, dgam_ref, dlb_ref, sst_ref, o_ref, b_ref, dst_ref, car_ref):
        lower, upper, krow, arow, acol = _hg_masks()
        lb_row, gam = lb_ref[...], on_ref[...]
        mm = _dot3 if has_lb else _dot1

        def recompute(ci):
            rs = pl.ds(pl.multiple_of(ci * c, c), c)
            gates = _hg_gates(qp_ref[rs, :], fp_ref[rs, :], lb_row, has_lb)
            b = _split_dot(gates[2], lower, 3, left=True)
            b_ref[...] = b
            return rs, gates, b

        def fwd_chunk(ci, carry):
            rs, (q, k, g, _, _, _), b = recompute(ci)
            v = iv_ref[rs, :].astype(BF16)
            st = carry
            sst_ref[ci] = st
            o = _dot_nt((q * jnp.exp(b)).astype(BF16), st.astype(BF16))
            amats = [_hg_sub(i, q, k, b, b_ref, krow, arow, acol)[-1] for i in range(c // HG_SUB)]
            o_ref[rs, :] = o + _dot(jnp.concatenate(amats, axis=0), v)
            bl = b_ref[c - 1:c, :]
            return jnp.exp(bl) * st + mm(iv_ref[rs, :], k * jnp.exp(bl - b), _dot_tn)

        def bwd_chunk(ci):
            rs, (q, k, g, sig, nsig, w), b = recompute(ci)
            qp, gp, v32 = qp_ref[rs, :], gp_ref[rs, :], iv_ref[rs, :]
            v = v32.astype(BF16)
            dyv, o = dy_ref[rs, :], o_ref[rs, :]
            st = sst_ref[ci]
            dst_new = dst_ref[...]
            r = lax.rsqrt(jnp.mean(o * o, axis=-1, keepdims=True) + NORM_EPS)
            on = o * r
            sgm = _sigmoid(gp)
            sg = gp * sgm
            dgam_ref[...] += jnp.sum(dyv * sg * on, axis=0, keepdims=True)
            dgp = dyv * on * gam * (sgm * (1.0 + gp * (1.0 - sgm)))
            dn = dyv * gam * sg
            do = r * dn - o * (r * r * r * jnp.mean(dn * o, axis=-1, keepdims=True))
            eb = jnp.exp(b)
            bl = b_ref[c - 1:c, :]
            ekd = jnp.exp(bl - b)
            do16 = do.astype(BF16)
            dq = mm(do, st, _dot) * eb
            dv = _dot1(k * ekd, dst_new, _dot_nt)
            dk = mm(v32, dst_new, _dot) * ekd
            dst_ref[...] = jnp.exp(bl) * dst_new + mm(do, q * eb, _dot_tn)
            da_all = mm(do, v32, _dot_nt)
            dq_parts, amats = [], []
            for i in range(c // HG_SUB):
                r0 = i * HG_SUB
                ebq, _, ek, _, mask, amat = _hg_sub(i, q, k, b, b_ref, krow, arow, acol)
                da = jnp.where(mask, da_all[r0:r0 + HG_SUB, :], 0.0)
                dq_parts.append(mm(da, k * ek, _dot) * ebq)
                dk = dk + mm(da, q[r0:r0 + HG_SUB, :] * ebq, _dot_tn) * ek
                amats.append(amat)
            dq = dq + jnp.concatenate(dq_parts, axis=0)
            dv = dv + _dot_tn(jnp.concatenate(amats, axis=0), do16)
            db = q * dq - k * dk
            dg = car_ref[...] + _split_dot(db, upper, 3, left=True)
            car_ref[...] += jnp.sum(db, axis=0, keepdims=True)
            if has_lb:
                dfp = dg * nsig * w - dk * ((1.0 - lb_row) * sig * nsig)
                dlb_ref[...] += jnp.sum(dg * nsig * jnp.exp(-g) - dk * nsig, axis=0, keepdims=True)
            else:
                dfp = dg * nsig - dk * (sig * nsig)
            sq = _sigmoid(qp)
            dp_ref[0, rs, :] = (dq * (sq * (1.0 + qp * (1.0 - sq)))).astype(BF16)
            dp_ref[1, rs, :] = dfp.astype(BF16)
            dp_ref[2, rs, :] = dv.astype(BF16)
            dp_ref[3, rs, :] = dgp.astype(BF16)

        return fwd_chunk, bwd_chunk

    ns = nh // hpb
    col = lambda p: pl.BlockSpec((t, wide), lambda h: (0, p * ns + h))
    vec = pl.BlockSpec((1, wide), lambda h: (0, h))
    return _host_call(
        body, 7, plan, lambda: (pl.program_id(0), ns),
        out_shape=[jax.ShapeDtypeStruct((4, t, kw), BF16), jax.ShapeDtypeStruct((1, kw), F32),
                   jax.ShapeDtypeStruct((1, kw), F32)],
        grid=(ns,),
        in_specs=[col(0), col(1), col(2), col(3), vec, vec, pl.BlockSpec((t, wide), lambda h: (0, h))],
        out_specs=[pl.BlockSpec((4, t, wide), lambda h: (0, 0, h)), vec, vec],
        scratch_shapes=[pltpu.VMEM((hpb, nc, HEAD, HEAD), F32), pltpu.VMEM((t, wide), F32),
                        pltpu.VMEM((hpb, c, HEAD), F32), pltpu.VMEM((hpb, HEAD, HEAD), F32),
                        pltpu.VMEM((hpb, 1, HEAD), F32)],
        name=name, sem=("parallel",), args=[proj, proj, proj, proj, lb, onorm_g, dy])


def _sb_masks():
    m, n = SB_QROWS, SB_BLOCK
    row = lax.broadcasted_iota(jnp.int32, (m, n), 0)
    col = lax.broadcasted_iota(jnp.int32, (m, n), 1)
    r2 = lax.broadcasted_iota(jnp.int32, (n, n), 0)
    c2 = lax.broadcasted_iota(jnp.int32, (n, n), 1)
    after = (r2 > c2).astype(BF16)
    from_ = (r2 >= c2).astype(BF16)
    return row, col, after, from_


def sba_fwd(qkv, name, plan=None):
    t, w3 = qkv.shape
    wd = w3 // 3
    m, n = SB_QROWS, SB_BLOCK
    nh, nq, per = wd // HEAD, t // m, m // n
    scale = HEAD ** -0.5

    def body(q_ref, k_ref, v_ref, o_ref, q16, k16, v16):
        row, col, after, _ = _sb_masks()
        q16[...] = (q_ref[...] * scale).astype(BF16)
        k16[...] = k_ref[...].astype(BF16)
        v16[...] = v_ref[...].astype(BF16)

        def qblock(qi, carry):
            qs = pl.ds(pl.multiple_of(qi * m, m), m)
            q = q16[qs, :]
            last = (qi + 1) * per - 1

            def kblock(step, state):
                acc, rem0 = state
                kj = last - step
                ks = pl.ds(pl.multiple_of(kj * n, n), n)
                z = _dot_nt(q, k16[ks, :])
                strict = (col + kj * n) < (row + qi * m)
                spz = _softplus(z)
                sp = jnp.where(strict, spz, 0.0)
                rem = rem0 + _split_dot(sp, after, 2)
                a = jnp.where(strict, jnp.exp(z - spz - rem), 0.0)
                acc = acc + _dot(a.astype(BF16), v16[ks, :])
                return acc, rem0 + jnp.sum(sp, axis=1, keepdims=True)

            acc, _ = lax.fori_loop(0, last + 1, kblock,
                                   (jnp.zeros((m, HEAD), F32), jnp.zeros((m, 1), F32)))
            o_ref[qs, :] = acc
            return carry

        lax.fori_loop(0, nq, qblock, 0)

    col_spec = lambda p: pl.BlockSpec((t, HEAD), lambda h: (0, p * nh + h))
    return _host_call(
        body, 3, plan, lambda: (pl.program_id(0), nh),
        out_shape=jax.ShapeDtypeStruct((t, wd), F32), grid=(nh,),
        in_specs=[col_spec(0), col_spec(1), col_spec(2)],
        out_specs=pl.BlockSpec((t, HEAD), lambda h: (0, h)),
        scratch_shapes=[pltpu.VMEM((t, HEAD), BF16)] * 3,
        name=name, sem=("parallel",), args=[qkv, qkv, qkv])


def sba_bwd(qkv, o, do, name, plan=None):
    t, w3 = qkv.shape
    wd = w3 // 3
    m, n = SB_QROWS, SB_BLOCK
    nh, nq, per = wd // HEAD, t // m, m // n
    scale = HEAD ** -0.5

    def body(q_ref, k_ref, v_ref, o_ref, do_ref, d_ref, dk_ref, dv_ref, q16, k16, v16):
        row, col, after, from_ = _sb_masks()
        dk_ref[...] = jnp.zeros_like(dk_ref)
        dv_ref[...] = jnp.zeros_like(dv_ref)
        q16[...] = (q_ref[...] * scale).astype(BF16)
        k16[...] = k_ref[...].astype(BF16)
        v16[...] = v_ref[...].astype(BF16)

        def qblock(qi, carry):
            qs = pl.ds(pl.multiple_of(qi * m, m), m)
            q = q16[qs, :]
            dov = do_ref[qs, :]
            do16 = dov.astype(BF16)
            dsum = jnp.sum(do16.astype(F32) * o_ref[qs, :], axis=1, keepdims=True)
            last = (qi + 1) * per - 1

            def kblock(step, state):
                dq, rem0, e0 = state
                kj = last - step
                ks = pl.ds(pl.multiple_of(kj * n, n), n)
                kv, vv = k16[ks, :], v16[ks, :]
                z = _dot_nt(q, kv)
                strict = (col + kj * n) < (row + qi * m)
                spz = _softplus(z)
                sp = jnp.where(strict, spz, 0.0)
                rem = rem0 + _split_dot(sp, after, 2)
                sgz = jnp.exp(z - spz)
                a = jnp.where(strict, sgz * jnp.exp(-rem), 0.0).astype(BF16)
                e = a.astype(F32) * _dot_nt(do16, vv)
                left = dsum - (e0 + _split_dot(e, from_, 2))
                dz = jnp.where(strict, e * (1.0 - sgz) - sgz * left, 0.0).astype(BF16)
                dq = dq + _dot(dz, kv)
                dk_ref[ks, :] += _dot_tn(dz, q)
                dv_ref[ks, :] += _dot_tn(a, do16)
                return (dq, rem0 + jnp.sum(sp, axis=1, keepdims=True),
                        e0 + jnp.sum(e, axis=1, keepdims=True))

            zero1 = jnp.zeros((m, 1), F32)
            dq, _, _ = lax.fori_loop(0, last + 1, kblock, (jnp.zeros((m, HEAD), F32), zero1, zero1))
            d_ref[0, qs, :] = (dq * scale).astype(BF16)
            return carry

        lax.fori_loop(0, nq, qblock, 0)
        d_ref[1, :, :] = dk_ref[...].astype(BF16)
        d_ref[2, :, :] = dv_ref[...].astype(BF16)

    col_spec = lambda p: pl.BlockSpec((t, HEAD), lambda h: (0, p * nh + h))
    head = pl.BlockSpec((t, HEAD), lambda h: (0, h))
    return _host_call(
        body, 5, plan, lambda: (pl.program_id(0), nh),
        out_shape=jax.ShapeDtypeStruct((3, t, wd), BF16), grid=(nh,),
        in_specs=[col_spec(0), col_spec(1), col_spec(2), head, head],
        out_specs=pl.BlockSpec((3, t, HEAD), lambda h: (0, 0, h)),
        scratch_shapes=[pltpu.VMEM((t, HEAD), F32)] * 2 + [pltpu.VMEM((t, HEAD), BF16)] * 3,
        name=name, sem=("parallel",), args=[qkv, qkv, qkv, o, do])


def _pool_band(i_out, i_in, tr, win, transpose):
    r = lax.broadcasted_iota(jnp.int32, (tr, tr), 0) + i_out * tr
    c = lax.broadcasted_iota(jnp.int32, (tr, tr), 1) + i_in * tr
    if transpose:
        return ((r <= c) & (r > c - win)).astype(F32)
    return ((c <= r) & (c > r - win)).astype(F32)


def _pool_p(u_ref, i, tr, win):
    cur = u_ref[i * tr:(i + 1) * tr, :]
    ws = _dot(_pool_band(i, i, tr, win, False), cur, HI)
    if i > 0:
        ws = ws + _dot(_pool_band(i, i - 1, tr, win, False), u_ref[(i - 1) * tr:i * tr, :], HI)
    pos = lax.broadcasted_iota(jnp.int32, (tr, 1), 0) + (i * tr + 1)
    count = jnp.minimum(pos, win).astype(F32)
    return ws / count - cur, count


def pool_fwd(u, pool_w, pool_scale, name):
    t, d = u.shape
    ng = len(POOL_WINDOWS)
    gs = d // ng
    tr = min(ROW_TILE, t)

    def body(u_ref, w_ref, s_ref, y_ref):
        win = jnp.left_shift(2, pl.program_id(0))
        for i in range(t // tr):
            p, _ = _pool_p(u_ref, i, tr, win)
            y_ref[i * tr:(i + 1) * tr, :] = _dot(p.astype(BF16), w_ref[...]) * s_ref[...]

    grp = pl.BlockSpec((t, gs), lambda g: (0, g))
    return pl.pallas_call(
        body, out_shape=jax.ShapeDtypeStruct((t, d), F32), grid=(ng,),
        in_specs=[grp, pl.BlockSpec((None, gs, gs), lambda g: (g, 0, 0)), pl.BlockSpec((1, gs), lambda g: (0, g))],
        out_specs=grp, name=name, compiler_params=_params(("parallel",)),
    )(u, pool_w, pool_scale)


def pool_bwd(u, pool_w, pool_scale, dy, name):
    t, d = u.shape
    ng = len(POOL_WINDOWS)
    gs = d // ng
    tr = min(ROW_TILE, t)
    nt = t // tr

    def body(u_ref, w_ref, s_ref, dy_ref, du_ref, dw_ref, ds_ref, dpc_ref, dp_ref):
        win = jnp.left_shift(2, pl.program_id(0))
        wv = w_ref[...]
        dw = jnp.zeros((gs, gs), F32)
        dsc = jnp.zeros((1, gs), F32)
        for i in range(nt):
            rows = slice(i * tr, (i + 1) * tr)
            p, count = _pool_p(u_ref, i, tr, win)
            p16 = p.astype(BF16)
            dyv = dy_ref[rows, :]
            dsc = dsc + jnp.sum(dyv * _dot(p16, wv), axis=0, keepdims=True)
            dyp = (dyv * s_ref[...]).astype(BF16)
            dw = dw + _dot_tn(p16, dyp)
            dp = _dot_nt(dyp, wv)
            dp_ref[rows, :] = dp
            dpc_ref[rows, :] = dp / count
        dw_ref[...] = dw.astype(BF16)
        ds_ref[...] = dsc
        for i in range(nt):
            rows = slice(i * tr, (i + 1) * tr)
            acc = _dot(_pool_band(i, i, tr, win, True), dpc_ref[rows, :], HI)
            if i + 1 < nt:
                acc = acc + _dot(_pool_band(i, i + 1, tr, win, True), dpc_ref[(i + 1) * tr:(i + 2) * tr, :], HI)
            du_ref[rows, :] = acc - dp_ref[rows, :]

    grp = pl.BlockSpec((t, gs), lambda g: (0, g))
    wspec = pl.BlockSpec((None, gs, gs), lambda g: (g, 0, 0))
    vec = pl.BlockSpec((1, gs), lambda g: (0, g))
    return pl.pallas_call(
        body,
        out_shape=[jax.ShapeDtypeStruct((t, d), F32), jax.ShapeDtypeStruct((ng, gs, gs), BF16),
                   jax.ShapeDtypeStruct((1, d), F32)],
        grid=(ng,), in_specs=[grp, wspec, vec, grp], out_specs=[grp, wspec, vec],
        scratch_shapes=[pltpu.VMEM((t, gs), F32), pltpu.VMEM((t, gs), F32)],
        name=name, compiler_params=_params(("parallel",)),
    )(u, pool_w, pool_scale, dy)


CONV_COLS = 256
HALO = 8


def _conv_taps(ref, r0, tr):
    x = ref[r0:r0 + tr, :]
    prev = ref[r0 - HALO:r0, :] if r0 > 0 else jnp.zeros((HALO, x.shape[1]), F32)
    xx = jnp.concatenate([prev, x], axis=0)
    return x, pltpu.roll(xx, 1, 0)[HALO:, :], pltpu.roll(xx, 2, 0)[HALO:, :]


def _conv_out(taps, w_ref, b_ref):
    x, s1, s2 = taps
    return w_ref[0:1, :] * s2 + w_ref[1:2, :] * s1 + w_ref[2:3, :] * x + b_ref[...]


def conv_glu_fwd(up, conv_w, conv_b, name):
    t, f2 = up.shape
    f = f2 // 2
    tc = min(CONV_COLS, f)
    nj = f // tc
    tr = min(ROW_TILE, t)

    def body(ug_ref, uv_ref, wg_ref, wv_ref, bg_ref, bv_ref, o_ref):
        for i in range(t // tr):
            r0 = i * tr
            gate = _conv_out(_conv_taps(ug_ref, r0, tr), wg_ref, bg_ref)
            val = _conv_out(_conv_taps(uv_ref, r0, tr), wv_ref, bv_ref)
            o_ref[r0:r0 + tr, :] = (gate * _sigmoid(gate) * val).astype(BF16)

    blk = lambda rows, half: pl.BlockSpec((rows, tc), lambda j: (0, half * nj + j))
    return pl.pallas_call(
        body, out_shape=jax.ShapeDtypeStruct((t, f), BF16), grid=(nj,),
        in_specs=[blk(t, 0), blk(t, 1), blk(CONV_WIDTH, 0), blk(CONV_WIDTH, 1), blk(1, 0), blk(1, 1)],
        out_specs=pl.BlockSpec((t, tc), lambda j: (0, j)),
        name=name, compiler_params=_params(("parallel",)),
    )(up, up, conv_w, conv_w, conv_b, conv_b)


def conv_glu_bwd(up, conv_w, conv_b, dact, name, plan=None):
    t, f2 = up.shape
    f = f2 // 2
    tc = min(CONV_COLS, f)
    nj = f // tc
    tr = min(ROW_TILE, t)
    nt = t // tr

    def body(ug_ref, uv_ref, wg_ref, wv_ref, bg_ref, bv_ref, da_ref, du_ref, dw_ref, db_ref, dg_ref, dv_ref):
        dwg = [jnp.zeros((1, tc), F32) for _ in range(CONV_WIDTH)]
        dwv = [jnp.zeros((1, tc), F32) for _ in range(CONV_WIDTH)]
        dbg = jnp.zeros((1, tc), F32)
        dbv = jnp.zeros((1, tc), F32)
        for i in range(nt):
            r0 = i * tr
            tg, tv = _conv_taps(ug_ref, r0, tr), _conv_taps(uv_ref, r0, tr)
            gate, val = _conv_out(tg, wg_ref, bg_ref), _conv_out(tv, wv_ref, bv_ref)
            sg = _sigmoid(gate)
            da = da_ref[r0:r0 + tr, :]
            d_gate = da * val * (sg * (1.0 + gate * (1.0 - sg)))
            d_val = da * (gate * sg)
            dg_ref[r0:r0 + tr, :] = d_gate
            dv_ref[r0:r0 + tr, :] = d_val
            dbg = dbg + jnp.sum(d_gate, axis=0, keepdims=True)
            dbv = dbv + jnp.sum(d_val, axis=0, keepdims=True)
            for tap in range(CONV_WIDTH):
                dwg[tap] = dwg[tap] + jnp.sum(d_gate * tg[2 - tap], axis=0, keepdims=True)
                dwv[tap] = dwv[tap] + jnp.sum(d_val * tv[2 - tap], axis=0, keepdims=True)
        for tap in range(CONV_WIDTH):
            dw_ref[0, tap:tap + 1, :] = dwg[tap]
            dw_ref[1, tap:tap + 1, :] = dwv[tap]
        db_ref[0, :, :] = dbg
        db_ref[1, :, :] = dbv
        for half, (d_ref, w_ref) in enumerate(((dg_ref, wg_ref), (dv_ref, wv_ref))):
            for i in range(nt):
                r0 = i * tr
                x = d_ref[r0:r0 + tr, :]
                nxt = d_ref[r0 + tr:r0 + tr + HALO, :] if i + 1 < nt else jnp.zeros((HALO, tc), F32)
                xx = jnp.concatenate([x, nxt], axis=0)
                up1 = pltpu.roll(xx, tr + HALO - 1, 0)[:tr, :]
                up2 = pltpu.roll(xx, tr + HALO - 2, 0)[:tr, :]
                du = w_ref[2:3, :] * x + w_ref[1:2, :] * up1 + w_ref[0:1, :] * up2
                du_ref[half, r0:r0 + tr, :] = du.astype(BF16)

    blk = lambda rows, half: pl.BlockSpec((rows, tc), lambda j: (0, half * nj + j))
    return _host_call(
        body, 7, plan, lambda: (pl.program_id(0), nj),
        out_shape=[jax.ShapeDtypeStruct((2, t, f), BF16), jax.ShapeDtypeStruct((2, CONV_WIDTH, f), F32),
                   jax.ShapeDtypeStruct((2, 1, f), F32)],
        grid=(nj,),
        in_specs=[blk(t, 0), blk(t, 1), blk(CONV_WIDTH, 0), blk(CONV_WIDTH, 1), blk(1, 0), blk(1, 1),
                  pl.BlockSpec((t, tc), lambda j: (0, j))],
        out_specs=[pl.BlockSpec((2, t, tc), lambda j: (0, 0, j)),
                   pl.BlockSpec((2, CONV_WIDTH, tc), lambda j: (0, 0, j)),
                   pl.BlockSpec((2, 1, tc), lambda j: (0, 0, j))],
        scratch_shapes=[pltpu.VMEM((t, tc), F32), pltpu.VMEM((t, tc), F32)],
        name=name, sem=("parallel",), args=[up, up, conv_w, conv_w, conv_b, conv_b, dact])


def _place():
    x, y, c = lax.axis_index("x"), lax.axis_index("y"), lax.axis_index("c")
    others = [(1 - x, y), (x, 1 - y), (1 - x, 1 - y)]
    return x, y, c, others


def _window(ref, axis, b, n):
    if axis == 1:
        return ref.at[:, pl.ds(b * n, n), :]
    return ref.at[:, :, pl.ds(b * n, n)]


def _remote(src, dst, send_sems, recv_sems, k, to):
    return pltpu.make_async_remote_copy(src_ref=src, dst_ref=dst, send_sem=send_sems.at[k],
                                        recv_sem=recv_sems.at[k], device_id=to, device_id_type=MESH)


class GatherPlan:
    has_mid = True

    def __init__(self, items):
        self.items = items
        self.args = [s for s, _, _, _ in items]
        self.out_shape = []
        for s, _, nl, ax in items:
            shp = [nl, s.shape[1], s.shape[2]]
            shp[ax] *= N_DEV
            self.out_shape.append(jax.ShapeDtypeStruct(tuple(shp), s.dtype))
        n = len(items)
        self.scratch = [pltpu.SemaphoreType.DMA((7 * n,)), pltpu.SemaphoreType.DMA((7 * n,)),
                        pltpu.SemaphoreType.DMA((n,))]

    def _mine(self, ins, a):
        _, l0, nl, _ = self.items[a]
        return ins[a].at[pl.ds(l0, nl)]

    def _copy(self, ins, outs, sems, a, k, block, to, own=False):
        s, _, _, ax = self.items[a]
        px, py, pc = block
        w = _window(outs[a], ax, 4 * px + 2 * py + pc, s.shape[ax])
        return _remote(self._mine(ins, a) if own else w, w, sems[0], sems[1], 7 * a + k, to)

    def _local(self, ins, outs, sems, a, x, y, c):
        s, _, _, ax = self.items[a]
        return pltpu.make_async_copy(self._mine(ins, a), _window(outs[a], ax, 4 * x + 2 * y + c, s.shape[ax]),
                                     sems[2].at[a])


    def _first(self, ins, outs, sems, a, x, y, c, others):
        me = (x, y, c)
        return [self._copy(ins, outs, sems, a, 0, me, (x, y, 1 - c), own=True)] + [
            self._copy(ins, outs, sems, a, 1 + j, me, (*chip, c), own=True) for j, chip in enumerate(others[:2])]

    def _relay(self, ins, outs, sems, a, x, y, c):
        block = ((1 - c) * (1 - x) + c * x, (1 - c) * y + c * (1 - y), c)
        to = ((1 - c) * x + c * (1 - x), (1 - c) * (1 - y) + c * y, c)
        return self._copy(ins, outs, sems, a, 3, block, to)

    def _passed(self, ins, outs, sems, a, j, x, y, c, others):
        return self._copy(ins, outs, sems, a, 4 + j, (*others[j], c), (x, y, 1 - c))

    def start(self, ins, outs, sems):
        x, y, c, others = _place()
        for a in range(len(self.items)):
            self._local(ins, outs, sems, a, x, y, c).start()
        for a in range(len(self.items)):
            for cp in self._first(ins, outs, sems, a, x, y, c, others):
                cp.start()

    def mid(self, ins, outs, sems):
        x, y, c, others = _place()
        for a in range(len(self.items)):
            for j, chip in enumerate(others[:2]):
                self._copy(ins, outs, sems, a, 1 + j, (*chip, c), (x, y, c)).wait_recv()
            self._relay(ins, outs, sems, a, x, y, c).start()
            for j in range(2):
                self._passed(ins, outs, sems, a, j, x, y, c, others).start()

    def finish(self, ins, outs, sems):
        x, y, c, others = _place()
        for a in range(len(self.items)):
            self._copy(ins, outs, sems, a, 3, (*others[2], c), (x, y, c)).wait_recv()
            self._passed(ins, outs, sems, a, 2, x, y, c, others).start()
        for a in range(len(self.items)):
            self._copy(ins, outs, sems, a, 0, (x, y, 1 - c), (x, y, c)).wait_recv()
            for j, chip in enumerate(others):
                self._copy(ins, outs, sems, a, 4 + j, (*chip, 1 - c), (x, y, c)).wait_recv()
        for a in range(len(self.items)):
            for cp in self._first(ins, outs, sems, a, x, y, c, others):
                cp.wait_send()
            self._relay(ins, outs, sems, a, x, y, c).wait_send()
            for j in range(3):
                self._passed(ins, outs, sems, a, j, x, y, c, others).wait_send()
            self._local(ins, outs, sems, a, x, y, c).wait()


class ExchangePlan:
    has_mid = False

    def __init__(self, partials):
        self.args = list(partials)
        self.out_shape = [jax.ShapeDtypeStruct(p.shape, p.dtype) for p in partials]
        n = len(partials)
        self.scratch = [pltpu.SemaphoreType.DMA((3 * n,)), pltpu.SemaphoreType.DMA((3 * n,)),
                        pltpu.SemaphoreType.DMA((n,))]

    def _copies(self, ins, outs, sems):
        x, y, c, others = _place()
        me = 2 * x + y
        local, sends, recvs = [], [], []
        for a in range(len(self.args)):
            local.append(pltpu.make_async_copy(ins[a].at[me], outs[a].at[me], sems[2].at[a]))
            for j, (px, py) in enumerate(others):
                sends.append(_remote(ins[a].at[2 * px + py], outs[a].at[me], sems[0], sems[1], 3 * a + j, (px, py, c)))
                slot = outs[a].at[2 * px + py]
                recvs.append(_remote(slot, slot, sems[0], sems[1], 3 * a + j, (px, py, c)))
        return local, sends, recvs

    def start(self, ins, outs, sems):
        local, sends, _ = self._copies(ins, outs, sems)
        for cp in local + sends:
            cp.start()

    def finish(self, ins, outs, sems):
        local, sends, recvs = self._copies(ins, outs, sems)
        for cp in recvs:
            cp.wait_recv()
        for cp in sends:
            cp.wait_send()
        for cp in local:
            cp.wait()


def run_plan(plan, name):
    ni, no = len(plan.args), len(plan.out_shape)

    def body(*refs):
        ins, outs, sems = refs[:ni], refs[ni:ni + no], refs[ni + no:]
        plan.start(ins, outs, sems)
        if plan.has_mid:
            plan.mid(ins, outs, sems)
        plan.finish(ins, outs, sems)

    return pl.pallas_call(
        body, out_shape=plan.out_shape, in_specs=[ANY] * ni, out_specs=[ANY] * no,
        scratch_shapes=plan.scratch, name=name,
    )(*plan.args)


class SiblingPlan:
    has_mid = False

    def __init__(self, grads, axes):
        self.args, self.axes = list(grads), list(axes)
        self.widths = [g.shape[ax] // N_DEV for g, ax in zip(grads, axes)]
        self.out_shape = []
        for g, ax, n in zip(grads, axes, self.widths):
            shp = list(g.shape)
            shp[ax] = n
            self.out_shape.append(jax.ShapeDtypeStruct((N_CHIP, *shp), g.dtype))
        n = len(grads)
        self.scratch = [pltpu.SemaphoreType.DMA((N_CHIP * n,)), pltpu.SemaphoreType.DMA((N_CHIP * n,))]

    def _copies(self, ins, outs, sems):
        x, y, c, _ = _place()
        copies = []
        for a in range(len(self.args)):
            for q in range(N_CHIP):
                src = _window(ins[a], self.axes[a], 2 * q + (1 - c), self.widths[a])
                copies.append(_remote(src, outs[a].at[q], sems[0], sems[1], N_CHIP * a + q, (x, y, 1 - c)))
        return copies

    def start(self, ins, outs, sems):
        for cp in self._copies(ins, outs, sems):
            cp.start()

    def finish(self, ins, outs, sems):
        copies = self._copies(ins, outs, sems)
        for cp in copies:
            cp.wait_recv()
        for cp in copies:
            cp.wait_send()


def _peer_of(k, x, y, c):
    return (1 - x if k & 4 else x, 1 - y if k & 2 else y, 1 - c if k & 1 else c)


def small_exchange(vec, reduce, name):
    r = vec.shape[0]

    def body(v_ref, o_ref, *scratch):
        if reduce:
            buf, send_sems, recv_sems = scratch
        else:
            buf, (send_sems, recv_sems) = o_ref, scratch
        x, y, c, _ = _place()
        me = 4 * x + 2 * y + c
        copies = []
        for k in range(1, N_DEV):
            cp = _remote(v_ref, buf.at[me], send_sems, recv_sems, k - 1, _peer_of(k, x, y, c))
            cp.start()
            copies.append(cp)
        buf[me] = v_ref[...]
        for k in range(1, N_DEV):
            px, py, pc = _peer_of(k, x, y, c)
            slot = buf.at[4 * px + 2 * py + pc]
            _remote(slot, slot, send_sems, recv_sems, k - 1, (px, py, pc)).wait_recv()
        for cp in copies:
            cp.wait_send()
        if reduce:
            tot = buf[0]
            for b in range(1, N_DEV):
                tot = tot + buf[b]
            o_ref[...] = tot

    sems = [pltpu.SemaphoreType.DMA((N_DEV - 1,)), pltpu.SemaphoreType.DMA((N_DEV - 1,))]
    if reduce:
        out_shape = jax.ShapeDtypeStruct((r, LANES), F32)
        scratch = [pltpu.VMEM((N_DEV, r, LANES), F32)] + sems
    else:
        out_shape = jax.ShapeDtypeStruct((N_DEV, r, LANES), F32)
        scratch = sems
    return pl.pallas_call(
        body, out_shape=out_shape, in_specs=[VMEM_FULL], out_specs=VMEM_FULL,
        scratch_shapes=scratch, name=name, compiler_params=_params(None),
    )(vec)


def _row_tile(rows, cols, limit=ELEMWISE_BLOCK_ELEMS):
    best = None
    for tb in range(16, rows + 1, 16):
        if rows % tb == 0 and tb * cols <= limit:
            best = tb
    return best if best is not None else rows


def add_sibling(grad, recv, axis, core):
    nl = grad.shape[0]
    _, _, r, cc = recv.shape
    tb = _row_tile(r, cc, 4 * ELEMWISE_BLOCK_ELEMS)
    per = r // tb

    def body(c_ref, g_ref, r_ref, o_ref):
        del c_ref
        o_ref[...] = (g_ref[...].astype(F32) + r_ref[...].astype(F32)).astype(BF16)

    if axis == 2:
        g_spec = pl.BlockSpec((None, tb, cc), lambda q, l, i, c_ref: (l, i, 2 * q + c_ref[0]))
    else:
        g_spec = pl.BlockSpec((None, tb, cc), lambda q, l, i, c_ref: (l, (2 * q + c_ref[0]) * per + i, 0))
    slot = pl.BlockSpec((None, None, tb, cc), lambda q, l, i, c_ref: (q, l, i, 0))
    return pl.pallas_call(
        body, out_shape=jax.ShapeDtypeStruct(recv.shape, BF16),
        grid_spec=pltpu.PrefetchScalarGridSpec(
            num_scalar_prefetch=1, grid=(N_CHIP, nl, per), in_specs=[g_spec, slot], out_specs=slot),
        name="add_sibling", compiler_params=_params(("parallel", "parallel", "parallel")),
    )(core, grad, recv)


def _adamw(w, g, m, v):
    m = ADAM_B1 * m + (1.0 - ADAM_B1) * g
    v = ADAM_B2 * v + (1.0 - ADAM_B2) * (g * g)
    m_hat = m / (1.0 - ADAM_B1 ** ADAM_STEP)
    v_hat = v / (1.0 - ADAM_B2 ** ADAM_STEP)
    delta = -ADAM_LR * (m_hat / (jnp.sqrt(v_hat) + ADAM_EPS) + ADAM_WD * w)
    return delta, m, v


def adam_from_partials(recv, w, m, v, l0, bufs):
    nl = recv.shape[1]
    _, r, cc = w.shape
    tb = _row_tile(r, cc)

    def body(p0, p1, p2, p3, w_ref, m_ref, v_ref, b0, b1, b2, b3, g_out, d_out, m_out, v_out):
        del b0, b1, b2, b3
        g = p0[...].astype(F32) + p1[...].astype(F32) + p2[...].astype(F32) + p3[...].astype(F32)
        d, mn, vn = _adamw(w_ref[...], g, m_ref[...], v_ref[...])
        g_out[...], d_out[...], m_out[...], v_out[...] = g, d, mn, vn

    slot = lambda q: pl.BlockSpec((None, None, tb, cc), lambda l, i: (q, l, i, 0))
    blk = pl.BlockSpec((None, tb, cc), lambda l, i: (l0 + l, i, 0))
    shp = jax.ShapeDtypeStruct(w.shape, F32)
    return pl.pallas_call(
        body, out_shape=[shp] * 4, grid=(nl, r // tb),
        in_specs=[slot(0), slot(1), slot(2), slot(3), blk, blk, blk] + [ANY] * 4, out_specs=[blk] * 4,
        input_output_aliases={7: 0, 8: 1, 9: 2, 10: 3},
        name="adam_big", compiler_params=_params(("parallel", "parallel")),
    )(recv, recv, recv, recv, w, m, v, *bufs)


def adam_small(w, g, m, v):
    def body(w_ref, g_ref, m_ref, v_ref, d_out, m_out, v_out):
        d_out[...], m_out[...], v_out[...] = _adamw(w_ref[...], g_ref[...], m_ref[...], v_ref[...])

    shp = jax.ShapeDtypeStruct(w.shape, F32)
    return pl.pallas_call(body, out_shape=[shp] * 3, name="adam_small")(w, g, m, v)


def _pack(arrays, multiple=8 * LANES):
    flat = jnp.concatenate([a.reshape(-1) for a in arrays])
    pad = (-flat.shape[0]) % multiple
    if pad:
        flat = jnp.concatenate([flat, jnp.zeros((pad,), flat.dtype)])
    return flat.reshape(-1, LANES)


def _unpack(packed, shapes):
    flat = packed.reshape(packed.shape[:-2] + (-1,))
    out, off = [], 0
    for shp in shapes:
        n = math.prod(shp)
        out.append(flat[..., off:off + n].reshape(packed.shape[:-2] + tuple(shp)))
        off += n
    return out


def _unshard_last(stacked):
    moved = jnp.moveaxis(stacked, 0, -2)
    return moved.reshape(moved.shape[:-2] + (-1,))


def _shard_last(full, block):
    n = full.shape[-1] // N_DEV
    return lax.dynamic_slice_in_dim(full, block * n, n, axis=full.ndim - 1)


BIG_AXIS = {"w_in": 2, "w_out": 1, "w_qkv": 2, "w_so": 1, "w_pool": 1, "w_up": 2, "w_down": 1}
BIG_ORDER = ("w_in", "w_out", "w_qkv", "w_so", "w_pool", "w_up", "w_down")

GATHER_HOSTS = {
    "hg_in_0": (("w_out", 0),),
    "hgrn_fwd_0": (("w_up", 0),),
    "ffn_up_0": (("w_down", 0),),
    "ffn_down_0": (("w_qkv", 0),),
    "sb_qkv_1": (("w_so", 0),),
    "sba_fwd_1": (("w_up", 1), ("w_pool", 0), ("w_up", 2)),
    "ffn_up_1": (("w_down", 1),),
    "ffn_down_1": (("w_down", 2),),
    "ffn_up_2": (("w_in", 1),),
    "ffn_down_2": (("w_out", 1),),
    "hgrn_fwd_3": (("w_up", 3),),
    "ffn_up_3": (("w_down", 3),),
}


MEGA = 1 << 20
EXCHANGE_BUDGET = {"d_u2": 13 * MEGA, "dw_up": 17 * MEGA, "hgrn_bwd": 30 * MEGA, "sba_bwd": 70 * MEGA,
                   "d_u1_0": 17 * MEGA}


class LocalWeights:
    def __init__(self, full):
        self.full, self.grads = full, {}

    def gather_plan(self, host):
        return None

    def gathered(self, outs):
        pass

    def weight(self, kind, l):
        return self.full[kind][l] if kind != "w_pool" else self.full[kind]

    def grad(self, kind, l, g):
        self.grads[(kind, l)] = g

    def exchange_plan(self, host=None):
        return None

    def exchanged(self, outs):
        pass

    def sibling_plan(self):
        return None


class MeshWeights:
    def __init__(self, shards16, w, m, v, core):
        self.shards, self.w, self.m, self.v, self.core = shards16, w, m, v, core
        self.full, self.pending, self.ready, self.flying, self.sib_flying, self.keys = {}, [], [], [], [], None
        self.out = {k: [lax.empty(w[k].shape, F32) for _ in range(4)] for k in BIG_ORDER}

    def _items(self, keys):
        return [(self.shards[k], 0 if k == "w_pool" else l, 4 if k == "w_pool" else 1, BIG_AXIS[k]) for k, l in keys]

    def gather_plan(self, host):
        self.keys = [key for key in GATHER_HOSTS.get(host, ()) if key[0] in self.shards
                     and key[1] < (1 if key[0] == "w_pool" else self.shards[key[0]].shape[0])]
        return GatherPlan(self._items(self.keys)) if self.keys else None

    def gathered(self, outs):
        for (k, l), o in zip(self.keys, outs):
            self.full[(k, l)] = o if k == "w_pool" else o[0]
        self.keys = None

    def weight(self, kind, l):
        if (kind, l) not in self.full:
            (out,) = run_plan(GatherPlan(self._items([(kind, l)])), f"gather_{kind}_{l}")
            self.full[(kind, l)] = out if kind == "w_pool" else out[0]
        return self.full[(kind, l)]

    def grad(self, kind, l, g):
        self.pending.append((kind, l, g if g.ndim == 3 else g[None]))

    def exchange_plan(self, host=None):
        budget = next((b for prefix, b in EXCHANGE_BUDGET.items() if host.startswith(prefix)), 0) if host else None
        used = 0

        def fits(queue):
            nonlocal used
            take, keep = [], []
            for item in queue:
                size = item[2].size // (N_CHIP if queue is self.ready else N_DEV)
                if budget is None or (used + size) * N_DEV <= budget:
                    take.append(item)
                    used += size
                else:
                    keep.append(item)
            return take, keep

        done, self.ready = fits(self.ready)
        raw, self.pending = fits(self.pending)
        if raw:
            done = done + self._partial_sums(raw, run_plan(self._sibling_plan(raw), "sibling_exchange"))
        if not done:
            return None
        self.flying = done
        return ExchangePlan([p for _, _, p in done])

    def exchanged(self, outs):
        for (k, l, _), r in zip(self.flying, outs):
            self.out[k] = adam_from_partials(r, self.w[k], self.m[k], self.v[k], l, self.out[k])
        self.flying = []

    def _sibling_plan(self, items):
        return SiblingPlan([g for _, _, g in items], [BIG_AXIS[k] for k, _, _ in items])

    def _partial_sums(self, items, recv):
        return [(k, l, add_sibling(g, r, BIG_AXIS[k], self.core)) for (k, l, g), r in zip(items, recv)]

    def sibling_plan(self):
        self.sib_flying, self.pending = self.pending, []
        return self._sibling_plan(self.sib_flying) if self.sib_flying else None

    def sibling_done(self, outs):
        self.ready += self._partial_sums(self.sib_flying, outs)
        self.sib_flying = []

    def finish(self):
        plan = self.exchange_plan()
        if plan is not None:
            self.exchanged(run_plan(plan, "chip_exchange_tail"))
        return self.out


def train_step(x, target, norm_g, lb_logits, onorm_g, pool_scale, conv_w, conv_b, wts):
    t, d = x.shape
    depth = norm_g.shape[0]
    ng = lambda i, j: norm_g[i, j].reshape(1, d)
    lbs = lower_bounds_fwd(lb_logits)
    saved = []
    h = x
    _, u16, u32 = res_norm(h, None, None, ng(0, 0), "norm_in")

    def hosted(call, *args, **kw):
        plan = wts.gather_plan(kw["name"])
        out, extra = call(*args, plan=plan, **kw)
        if plan is not None:
            wts.gathered(extra)
        return out

    def project(a, kind, l, name):
        plan = wts.gather_plan(name)
        if plan is None:
            return matmul(a, wts.weight(kind, l), "nn", F32, name)
        out, extra = matmul(a, wts.weight(kind, l), "nn", F32, name, plan=plan)
        wts.gathered(extra)
        return out

    for i in range(depth):
        kind, j = i % 3, i // 3
        s = {"h_in": h, "u1": u16}
        if kind == 0:
            s["proj"] = project(u16, "w_in", j, f"hg_in_{i}")
            s["y"] = hosted(hgrn_fwd, s["proj"], lbs[i].reshape(1, -1), onorm_g[j].reshape(1, -1), i > 0,
                            name=f"hgrn_fwd_{i}")
            mix = project(s["y"], "w_out", j, f"hg_out_{i}")
        elif kind == 1:
            s["qkv"] = project(u16, "w_qkv", j, f"sb_qkv_{i}")
            s["o"] = hosted(sba_fwd, s["qkv"], name=f"sba_fwd_{i}")
            mix = project(s["o"], "w_so", j, f"sb_out_{i}")
        else:
            s["u1f"] = u32
            mix = pool_fwd(u32, wts.weight("w_pool", j), pool_scale[j].reshape(1, d), f"pool_fwd_{i}")
        s["mix"] = mix
        h_mid, u2, _ = res_norm(h, mix, ng(i, 1), ng(i, 2), f"norm_mid_{i}")
        s["h_mid"], s["u2"] = h_mid, u2
        s["up"] = project(u2, "w_up", i, f"ffn_up_{i}")
        s["act"] = conv_glu_fwd(s["up"], conv_w[i], conv_b[i].reshape(1, -1), f"glu_fwd_{i}")
        s["f"] = project(s["act"], "w_down", i, f"ffn_down_{i}")
        nxt = ng(i + 1, 0) if i + 1 < depth else None
        h, u16, u32 = res_norm(h_mid, s["f"], ng(i, 3), nxt, f"norm_out_{i}",
                               want_f32=(nxt is not None and (i + 1) % 3 == 2))
        saved.append(s)

    loss_acc, dh = loss_head(h, target, "loss_head")

    d_norm = [[None] * 4 for _ in range(depth)]
    d_lbs = jnp.zeros_like(lbs)
    d_onorm = [None] * onorm_g.shape[0]
    d_pscale = [None] * pool_scale.shape[0]
    d_cw, d_cb = [None] * depth, [None] * depth

    def exchanging(call, *args, **kw):
        plan = wts.exchange_plan(kw["name"])
        out, extra = call(*args, plan=plan, **kw)
        if plan is not None:
            wts.exchanged(extra)
        return out

    def bmm(a, b, mode, dtype, name):
        plan = wts.exchange_plan(name)
        if plan is None:
            return matmul(a, b, mode, dtype, name)
        out, extra = matmul(a, b, mode, dtype, name, plan=plan)
        wts.exchanged(extra)
        return out

    for i in reversed(range(depth)):
        kind, j = i % 3, i // 3
        s = saved[i]
        df, d_norm[i][3] = norm_bwd(s["f"], ng(i, 3), dh, None, BF16, f"nb_out_{i}")
        dact = bmm(df, wts.weight("w_down", i), "nt", F32, f"d_act_{i}")
        wts.grad("w_down", i, bmm(s["act"], df, "tn", BF16, f"dw_down_{i}"))
        sib = wts.sibling_plan()
        (dup, dcw, dcb), extra = conv_glu_bwd(s["up"], conv_w[i], conv_b[i].reshape(1, -1), dact, f"glu_bwd_{i}",
                                              plan=sib)
        if sib is not None:
            wts.sibling_done(extra)
        d_cw[i] = jnp.moveaxis(dcw, 0, 1).reshape(CONV_WIDTH, -1)
        d_cb[i] = dcb.reshape(-1)
        du2 = bmm(Split(dup), wts.weight("w_up", i), "nt", F32, f"d_u2_{i}")
        wts.grad("w_up", i, bmm(s["u2"], Split(dup), "tn", BF16, f"dw_up_{i}"))
        dh_mid, d_norm[i][2] = norm_bwd(s["h_mid"], ng(i, 2), du2, dh, F32, f"nb_mid_{i}")
        dm, d_norm[i][1] = norm_bwd(s["mix"], ng(i, 1), dh_mid, None, F32 if kind == 2 else BF16, f"nb_mix_{i}")
        if kind == 0:
            dy = matmul(dm, wts.weight("w_out", j), "nt", F32, f"d_y_{i}")
            wts.grad("w_out", j, matmul(s["y"], dm, "tn", BF16, f"dw_hgout_{i}"))
            dproj, d_onorm[j], dlb = exchanging(hgrn_bwd, s["proj"], lbs[i].reshape(1, -1),
                                                onorm_g[j].reshape(1, -1), dy, i > 0, name=f"hgrn_bwd_{i}")
            d_lbs = d_lbs.at[i].set(dlb[0])
            wts.grad("w_in", j, bmm(s["u1"], Split(dproj), "tn", BF16, f"dw_hgin_{i}"))
            du1 = bmm(Split(dproj), wts.weight("w_in", j), "nt", F32, f"d_u1_{i}")
        elif kind == 1:
            do = matmul(dm, wts.weight("w_so", j), "nt", F32, f"d_o_{i}")
            wts.grad("w_so", j, matmul(s["o"], dm, "tn", BF16, f"dw_sbout_{i}"))
            dqkv = exchanging(sba_bwd, s["qkv"], s["o"], do, name=f"sba_bwd_{i}")
            du1 = bmm(Split(dqkv), wts.weight("w_qkv", j), "nt", F32, f"d_u1_{i}")
            wts.grad("w_qkv", j, bmm(s["u1"], Split(dqkv), "tn", BF16, f"dw_sbqkv_{i}"))
        else:
            du1, g_pool, d_pscale[j] = pool_bwd(s["u1f"], wts.weight("w_pool", j), pool_scale[j].reshape(1, d),
                                                dm, f"pool_bwd_{i}")
            wts.grad("w_pool", j, g_pool)
        dh, d_norm[i][0] = norm_bwd(s["h_in"], ng(i, 0), du1, dh_mid, F32, f"nb_in_{i}")

    small = {
        "norm_g": jnp.stack([jnp.stack([v.reshape(d) for v in row]) for row in d_norm]),
        "lb_logits": lower_bounds_bwd(lb_logits, d_lbs),
        "onorm_g": jnp.stack([v.reshape(-1) for v in d_onorm]),
        "pool_scale": jnp.stack([v.reshape(-1) for v in d_pscale]),
        "conv_w": jnp.stack(d_cw),
        "conv_b": jnp.stack(d_cb),
    }
    return loss_acc, dh, small


SMALL_SHARDED = ("norm_g", "onorm_g", "pool_scale", "conv_w")
SMALL_ORDER = ("norm_g", "lb_logits", "onorm_g", "pool_scale", "conv_w", "conv_b")


def kernel(x, norm_g, hgrn_lb_logits, hgrn_w_in, hgrn_onorm_g, hgrn_w_out, sba_w_qkv, sba_w_out, pool_w, pool_scale, ffn_w_up, ffn_conv_w, ffn_conv_b, ffn_w_down, loss_target, m_norm_g, m_hgrn_lb_logits, m_hgrn_w_in, m_hgrn_onorm_g, m_hgrn_w_out, m_sba_w_qkv, m_sba_w_out, m_pool_w, m_pool_scale, m_ffn_w_up, m_ffn_conv_w, m_ffn_conv_b, m_ffn_w_down, v_norm_g, v_hgrn_lb_logits, v_hgrn_w_in, v_hgrn_onorm_g, v_hgrn_w_out, v_sba_w_qkv, v_sba_w_out, v_pool_w, v_pool_scale, v_ffn_w_up, v_ffn_conv_w, v_ffn_conv_b, v_ffn_w_down):
    cx, cy, cc = lax.axis_index("x"), lax.axis_index("y"), lax.axis_index("c")
    block = 4 * cx + 2 * cy + cc
    core = cc.astype(jnp.int32).reshape(1)

    pool3 = lambda a: a.reshape(a.shape[0] * a.shape[1], a.shape[2], a.shape[3])
    big_w = dict(zip(BIG_ORDER, [hgrn_w_in, hgrn_w_out, sba_w_qkv, sba_w_out, pool3(pool_w), ffn_w_up, ffn_w_down]))
    big_m = dict(zip(BIG_ORDER, [m_hgrn_w_in, m_hgrn_w_out, m_sba_w_qkv, m_sba_w_out, pool3(m_pool_w), m_ffn_w_up,
                                 m_ffn_w_down]))
    big_v = dict(zip(BIG_ORDER, [v_hgrn_w_in, v_hgrn_w_out, v_sba_w_qkv, v_sba_w_out, pool3(v_pool_w), v_ffn_w_up,
                                 v_ffn_w_down]))

    sharded = {"norm_g": norm_g, "onorm_g": hgrn_onorm_g, "pool_scale": pool_scale, "conv_w": ffn_conv_w}
    gathered = small_exchange(_pack([sharded[n] for n in SMALL_SHARDED]), False, "gather_small")
    parts = _unpack(gathered, [sharded[n].shape for n in SMALL_SHARDED])
    full = {n: _unshard_last(p) for n, p in zip(SMALL_SHARDED, parts)}

    wts = MeshWeights({k: w.astype(BF16) for k, w in big_w.items()}, big_w, big_m, big_v, core)
    loss_acc, grad_x, small_g = train_step(
        x[0], loss_target[0], full["norm_g"], hgrn_lb_logits, full["onorm_g"], full["pool_scale"],
        full["conv_w"], ffn_conv_b, wts)
    loss = lax.psum(loss_acc[0, 0], ("x", "y", "c"))

    shapes = [small_g[n].shape for n in SMALL_ORDER]
    summed = _unpack(small_exchange(_pack([small_g[n] for n in SMALL_ORDER]), True, "reduce_small"), shapes)
    sg = {n: (_shard_last(g, block) if n in SMALL_SHARDED else g) for n, g in zip(SMALL_ORDER, summed)}
    sw = {"norm_g": norm_g, "lb_logits": hgrn_lb_logits, "onorm_g": hgrn_onorm_g, "pool_scale": pool_scale,
          "conv_w": ffn_conv_w, "conv_b": ffn_conv_b}
    sm = {"norm_g": m_norm_g, "lb_logits": m_hgrn_lb_logits, "onorm_g": m_hgrn_onorm_g, "pool_scale": m_pool_scale,
          "conv_w": m_ffn_conv_w, "conv_b": m_ffn_conv_b}
    sv = {"norm_g": v_norm_g, "lb_logits": v_hgrn_lb_logits, "onorm_g": v_hgrn_onorm_g, "pool_scale": v_pool_scale,
          "conv_w": v_ffn_conv_w, "conv_b": v_ffn_conv_b}
    sshapes = [sw[n].shape for n in SMALL_ORDER]
    packed = [_pack([dct[n] for n in SMALL_ORDER]) for dct in (sw, sg, sm, sv)]
    s_delta, s_m, s_v = [dict(zip(SMALL_ORDER, _unpack(p, sshapes))) for p in adam_small(*packed)]

    upd = wts.finish()
    b_grad, b_delta, b_m, b_v = [[upd[k][n] for k in BIG_ORDER] for n in range(4)]

    def tree(small, bigs):
        bg = list(bigs)
        bg[4] = bg[4].reshape(pool_w.shape)
        return (small["norm_g"], small["lb_logits"], bg[0], small["onorm_g"], bg[1], bg[2], bg[3], bg[4],
                small["pool_scale"], bg[5], small["conv_w"], small["conv_b"], bg[6])

    return (loss, grad_x[None], *tree(sg, b_grad), *tree(s_delta, b_delta), *tree(s_m, b_m), *tree(s_v, b_v))
```

```python
import functools
import math

import jax
import jax.numpy as jnp
from jax import lax
from jax.experimental import pallas as pl
from jax.experimental.pallas import tpu as pltpu

F32 = jnp.float32
BF16 = jnp.bfloat16
HI = lax.Precision.HIGHEST
MESH = pl.DeviceIdType.MESH
ANY = pl.BlockSpec(memory_space=pl.ANY)
VMEM_FULL = pl.BlockSpec(memory_space=pltpu.VMEM)

NORM_EPS = 1e-6
HEAD = 128
HG_CHUNK = 128
HG_SUB = 32
HG_MAX_EXPONENT = 80.0
SB_BLOCK = 256
SB_QROWS = 256
POOL_WINDOWS = (2, 4, 8, 16)
CONV_WIDTH = 3
ROW_TILE = 256
N_DEV = 8
N_CHIP = 4

ADAM_LR = 0.001
ADAM_B1 = 0.9
ADAM_B2 = 0.999
ADAM_EPS = 1e-08
ADAM_WD = 0.01
ADAM_STEP = 10

VMEM_LIMIT = 48 * 1024 * 1024
LANES = 128
ELEMWISE_BLOCK_ELEMS = 256 * 1024


def _params(sem=None, vmem=VMEM_LIMIT):
    return pltpu.CompilerParams(dimension_semantics=sem, vmem_limit_bytes=vmem)


def _tile(n, prefs=(1024, 512, 256, 128)):
    for p in prefs:
        if n % p == 0:
            return p
    return n


def _dot(a, b, prec=None):
    return jnp.dot(a, b, precision=prec, preferred_element_type=F32)


def _dot_nt(a, b, prec=None):
    return lax.dot_general(a, b, (((1,), (1,)), ((), ())), precision=prec, preferred_element_type=F32)


def _dot_tn(a, b, prec=None):
    return lax.dot_general(a, b, (((0,), (0,)), ((), ())), precision=prec, preferred_element_type=F32)


def _split_dot(x, tri, parts, left=False):
    tot, rest = None, x
    for p in range(parts):
        h = rest.astype(BF16)
        d = _dot(tri, h) if left else _dot(h, tri)
        tot = d if tot is None else tot + d
        if p + 1 < parts:
            rest = rest - h.astype(F32)
    return tot


def _dot1(a, b, fn):
    return fn(a.astype(BF16), b.astype(BF16))


def _dot3(a, b, fn):
    ah, bh = a.astype(BF16), b.astype(BF16)
    al, bl = (a - ah.astype(F32)).astype(BF16), (b - bh.astype(F32)).astype(BF16)
    return fn(ah, bh) + fn(ah, bl) + fn(al, bh)


def _sigmoid(x):
    return jax.nn.sigmoid(x)


def _softplus(x):
    return jnp.maximum(x, 0.0) + jnp.log1p(jnp.exp(-jnp.abs(x)))


class Split:
    def __init__(self, arr):
        self.arr = arr
        self.shape = (arr.shape[1], arr.shape[0] * arr.shape[2])
        self.part = arr.shape[2]


class Plain:
    def __init__(self, arr):
        self.arr = arr
        self.shape = arr.shape
        self.part = None


def _wrap(op):
    return op if isinstance(op, (Split, Plain)) else Plain(op)


def _op_spec(op, br, bc, rc_of_grid):
    if isinstance(op, Split):
        per = op.part // bc

        def imap(i, j, k):
            r, c = rc_of_grid(i, j, k)
            return (lax.div(c, per), r, lax.rem(c, per))
        return pl.BlockSpec((None, br, bc), imap)
    return pl.BlockSpec((br, bc), rc_of_grid)


def _hosted(body, n_in, n_out, plan, step_of_grid):
    if plan is None:
        return body
    pi, po, ps = len(plan.args), len(plan.out_shape), len(plan.scratch)

    def wrapped(*refs):
        refs = list(refs)
        ins, pins = refs[:n_in], refs[n_in:n_in + pi]
        outs = refs[n_in + pi:n_in + pi + n_out]
        pouts = refs[n_in + pi + n_out:n_in + pi + n_out + po]
        scr, pscr = refs[n_in + pi + n_out + po:len(refs) - ps], refs[len(refs) - ps:]
        step, nsteps = step_of_grid()

        @pl.when(step == 0)
        def _():
            plan.start(pins, pouts, pscr)

        if plan.has_mid:
            @pl.when(step == min((2 * nsteps) // 3, nsteps - 1))
            def _():
                plan.mid(pins, pouts, pscr)

        body(*ins, *outs, *scr)

        @pl.when(step == nsteps - 1)
        def _():
            plan.finish(pins, pouts, pscr)

    return wrapped


def _host_call(body, n_in, plan, step_of_grid, *, out_shape, grid, in_specs, out_specs, scratch_shapes, name, sem, args):
    single = not isinstance(out_shape, (list, tuple))
    out_shape = [out_shape] if single else list(out_shape)
    out_specs = [out_specs] if single else list(out_specs)
    n_out = len(out_shape)
    in_specs, scratch_shapes, args = list(in_specs), list(scratch_shapes), list(args)
    if plan is not None:
        in_specs += [ANY] * len(plan.args)
        args += plan.args
        out_shape += plan.out_shape
        out_specs += [ANY] * len(plan.out_shape)
        scratch_shapes += plan.scratch
        sem = ("arbitrary",) * len(grid)
    outs = pl.pallas_call(
        _hosted(body, n_in, n_out, plan, step_of_grid), out_shape=out_shape, grid=grid,
        in_specs=in_specs, out_specs=out_specs, scratch_shapes=scratch_shapes,
        name=name, compiler_params=_params(sem),
    )(*args)
    host = outs[0] if single else list(outs[:n_out])
    return host, list(outs[n_out:])


def matmul(a, b, mode, out_dtype, name, plan=None):
    a, b = _wrap(a), _wrap(b)
    if mode == "nn":
        (m, kd), (kd2, n) = a.shape, b.shape
    elif mode == "nt":
        (m, kd), (n, kd2) = a.shape, b.shape
    else:
        (kd, m), (kd2, n) = a.shape, b.shape
    assert kd == kd2, (mode, a.shape, b.shape)

    def dim_tile(full, ops_on_cols, prefs=(1024, 512, 256, 128)):
        base = full
        for op in ops_on_cols:
            if op.part is not None:
                base = math.gcd(base, op.part)
        return _tile(base, prefs)

    tm = dim_tile(m, [a] if mode == "tn" else [])
    tn = dim_tile(n, [b] if mode in ("nn", "tn") else [])
    tk = dim_tile(kd, ([a] if mode in ("nn", "nt") else []) + ([b] if mode == "nt" else []),
                  prefs=(2048, 2816, 1024, 512, 256, 128))
    nk = kd // tk

    if mode == "nn":
        a_spec = _op_spec(a, tm, tk, lambda i, j, k: (i, k))
        b_spec = _op_spec(b, tk, tn, lambda i, j, k: (k, j))
        dot = _dot
    elif mode == "nt":
        a_spec = _op_spec(a, tm, tk, lambda i, j, k: (i, k))
        b_spec = _op_spec(b, tn, tk, lambda i, j, k: (j, k))
        dot = _dot_nt
    else:
        a_spec = _op_spec(a, tk, tm, lambda i, j, k: (k, i))
        b_spec = _op_spec(b, tk, tn, lambda i, j, k: (k, j))
        dot = _dot_tn

    def body(a_ref, b_ref, o_ref, *acc):
        part = dot(a_ref[...].astype(BF16), b_ref[...].astype(BF16))
        if nk == 1:
            o_ref[...] = part.astype(o_ref.dtype)
            return
        (acc_ref,) = acc
        k = pl.program_id(2)

        @pl.when(k == 0)
        def _():
            acc_ref[...] = part

        @pl.when(k > 0)
        def _():
            acc_ref[...] += part

        @pl.when(k == nk - 1)
        def _():
            o_ref[...] = acc_ref[...].astype(o_ref.dtype)

    gi, gj = m // tm, n // tn

    def step_of_grid():
        return (pl.program_id(0) * gj + pl.program_id(1)) * nk + pl.program_id(2), gi * gj * nk

    out, extra = _host_call(
        body, 2, plan, step_of_grid, out_shape=jax.ShapeDtypeStruct((m, n), out_dtype), grid=(gi, gj, nk),
        in_specs=[a_spec, b_spec], out_specs=pl.BlockSpec((tm, tn), lambda i, j, k: (i, j)),
        scratch_shapes=[pltpu.VMEM((tm, tn), F32)] if nk > 1 else [], name=name,
        sem=("parallel", "parallel", "arbitrary"), args=[a.arr, b.arr])
    return out if plan is None else (out, extra)


def _rms(x, g):
    r = lax.rsqrt(jnp.mean(x * x, axis=-1, keepdims=True) + NORM_EPS)
    return x * r * g


def res_norm(h, m, g_a, g_b, name, want_f32=False):
    t, d = h.shape
    tr = min(ROW_TILE, t)
    has_m, has_u = m is not None, g_b is not None
    row = pl.BlockSpec((tr, d), lambda i: (i, 0))
    vec = pl.BlockSpec((1, d), lambda i: (0, 0))

    def body(*refs):
        refs = list(refs)
        h_ref = refs.pop(0)
        hn = h_ref[...]
        if has_m:
            m_ref, ga_ref = refs.pop(0), refs.pop(0)
            hn = hn + _rms(m_ref[...], ga_ref[...])
        if has_u:
            gb_ref = refs.pop(0)
        if has_m:
            refs.pop(0)[...] = hn
        if has_u:
            u = _rms(hn, gb_ref[...])
            refs.pop(0)[...] = u.astype(BF16)
            if want_f32:
                refs.pop(0)[...] = u

    args, in_specs, out_shape, out_specs = [h], [row], [], []
    if has_m:
        args += [m, g_a]
        in_specs += [row, vec]
        out_shape.append(jax.ShapeDtypeStruct((t, d), F32))
        out_specs.append(row)
    if has_u:
        args.append(g_b)
        in_specs.append(vec)
        out_shape.append(jax.ShapeDtypeStruct((t, d), BF16))
        out_specs.append(row)
        if want_f32:
            out_shape.append(jax.ShapeDtypeStruct((t, d), F32))
            out_specs.append(row)
    outs = list(pl.pallas_call(
        body, out_shape=out_shape, grid=(t // tr,), in_specs=in_specs, out_specs=out_specs,
        name=name, compiler_params=_params(("parallel",)),
    )(*args))
    h_new = outs.pop(0) if has_m else None
    u16 = outs.pop(0) if has_u else None
    u32 = outs.pop(0) if (has_u and want_f32) else None
    return h_new, u16, u32


def norm_bwd(x, g, dy, add, out_dtype, name):
    t, d = x.shape
    tr = min(ROW_TILE, t)
    has_add = add is not None
    row = pl.BlockSpec((tr, d), lambda i: (i, 0))
    vec = pl.BlockSpec((1, d), lambda i: (0, 0))

    def body(*refs):
        if has_add:
            x_ref, g_ref, dy_ref, add_ref, dx_ref, dg_ref = refs
        else:
            x_ref, g_ref, dy_ref, dx_ref, dg_ref = refs
        xv = x_ref[...]
        dyv = dy_ref[...].astype(F32)
        r = lax.rsqrt(jnp.mean(xv * xv, axis=-1, keepdims=True) + NORM_EPS)
        gy = dyv * g_ref[...]
        dx = r * gy - xv * (r * r * r * jnp.mean(gy * xv, axis=-1, keepdims=True))
        if has_add:
            dx = dx + add_ref[...]
        dx_ref[...] = dx.astype(dx_ref.dtype)

        @pl.when(pl.program_id(0) == 0)
        def _():
            dg_ref[...] = jnp.zeros_like(dg_ref)

        dg_ref[...] += jnp.sum(dyv * xv * r, axis=0, keepdims=True)

    args = [x, g, dy] + ([add] if has_add else [])
    in_specs = [row, vec, row] + ([row] if has_add else [])
    return pl.pallas_call(
        body, out_shape=[jax.ShapeDtypeStruct((t, d), out_dtype), jax.ShapeDtypeStruct((1, d), F32)],
        grid=(t // tr,), in_specs=in_specs, out_specs=[row, vec],
        name=name, compiler_params=_params(("arbitrary",)),
    )(*args)


def loss_head(y, target, name):
    t, d = y.shape
    tr = min(ROW_TILE, t)
    row = pl.BlockSpec((tr, d), lambda i: (i, 0))
    acc = pl.BlockSpec((8, LANES), lambda i: (0, 0))

    def body(y_ref, t_ref, loss_ref, dy_ref):
        e = y_ref[...] - t_ref[...]
        dy_ref[...] = e * (1.0 / d)

        @pl.when(pl.program_id(0) == 0)
        def _():
            loss_ref[...] = jnp.zeros_like(loss_ref)

        loss_ref[...] += jnp.sum(e * e) * (0.5 / d)

    return pl.pallas_call(
        body, out_shape=[jax.ShapeDtypeStruct((8, LANES), F32), jax.ShapeDtypeStruct((t, d), F32)],
        grid=(t // tr,), in_specs=[row, row], out_specs=[acc, row],
        name=name, compiler_params=_params(("arbitrary",)),
    )(y, target)


def _depth_softmax(ref, depth):
    rows = [ref[i:i + 1, :] for i in range(depth)]
    mx = functools.reduce(jnp.maximum, rows)
    ex = [jnp.exp(r - mx) for r in rows]
    tot = functools.reduce(lambda p, q: p + q, ex)
    return [e / tot for e in ex]


def lower_bounds_fwd(logits):
    depth, kw = logits.shape

    def body(l_ref, o_ref):
        s = _depth_softmax(l_ref, depth)
        run = jnp.zeros_like(s[0])
        o_ref[0:1, :] = run
        for i in range(1, depth):
            run = run + s[i]
            o_ref[i:i + 1, :] = run

    return pl.pallas_call(body, out_shape=jax.ShapeDtypeStruct((depth, kw), F32), name="lb_fwd")(logits)


def lower_bounds_bwd(logits, dlb):
    depth, kw = logits.shape

    def body(l_ref, d_ref, o_ref):
        s = _depth_softmax(l_ref, depth)
        ds = [jnp.zeros_like(s[0]) for _ in range(depth)]
        run = jnp.zeros_like(s[0])
        for j in range(depth - 1, 0, -1):
            run = run + d_ref[j:j + 1, :]
            ds[j] = run
        dot = functools.reduce(lambda p, q: p + q, [s[j] * ds[j] for j in range(depth)])
        for j in range(depth):
            o_ref[j:j + 1, :] = s[j] * (ds[j] - dot)

    return pl.pallas_call(body, out_shape=jax.ShapeDtypeStruct((depth, kw), F32), name="lb_bwd")(logits, dlb)


def _hg_gates(qp, fp, lb_row, has_lb):
    sig = _sigmoid(fp)
    nsig = _sigmoid(-fp)
    ls = jnp.minimum(fp, 0.0) - jnp.log1p(jnp.exp(-jnp.abs(fp)))
    if has_lb:
        a = jnp.log(lb_row)
        bb = jnp.log1p(-lb_row) + ls
        g = jnp.maximum(a, bb) + jnp.log1p(jnp.exp(-jnp.abs(a - bb)))
        w = jnp.exp(bb - g)
        k = (1.0 - lb_row) * nsig
    else:
        g, w, k = ls, None, nsig
    q = qp * _sigmoid(qp)
    return q, k, g, sig, nsig, w


HG_HEADS_PER_STEP = 2


def _heads_per_step(nh):
    return HG_HEADS_PER_STEP if nh % HG_HEADS_PER_STEP == 0 else 1


def _head_views(refs, hh, lanes, lead):
    cs = pl.ds(hh * HEAD, HEAD)
    out = []
    for i, r in enumerate(refs):
        if i in lead:
            out.append(r.at[hh])
        elif i in lanes:
            out.append(r.at[(slice(None),) * (len(r.shape) - 1) + (cs,)])
        else:
            out.append(r)
    return out


def _hg_masks():
    c = HG_CHUNK
    row = lax.broadcasted_iota(jnp.int32, (c, c), 0)
    col = lax.broadcasted_iota(jnp.int32, (c, c), 1)
    lower = (col <= row).astype(BF16)
    upper = (col >= row).astype(BF16)
    krow = lax.broadcasted_iota(jnp.int32, (c, HEAD), 0)
    arow = lax.broadcasted_iota(jnp.int32, (HG_SUB, c), 0)
    acol = lax.broadcasted_iota(jnp.int32, (HG_SUB, c), 1)
    return lower, upper, krow, arow, acol


def _hg_sub(i, q, k, b, b_ref, krow, arow, acol):
    r0 = i * HG_SUB
    m = b_ref[r0:r0 + 1, :]
    ebq = jnp.exp(b[r0:r0 + HG_SUB, :] - m)
    qh = (q[r0:r0 + HG_SUB, :] * ebq).astype(BF16)
    ek = jnp.exp(jnp.where(krow < r0 + HG_SUB, jnp.minimum(m - b, HG_MAX_EXPONENT), 0.0))
    kh = (k * ek).astype(BF16)
    mask = acol <= arow + r0
    amat = jnp.where(mask, _dot_nt(qh, kh), 0.0).astype(BF16)
    return ebq, qh, ek, kh, mask, amat


def hgrn_fwd(proj, lb, onorm_g, has_lb, name, plan=None):
    t, w4 = proj.shape
    kw = w4 // 4
    nh, nc, c = kw // HEAD, t // HG_CHUNK, HG_CHUNK
    hpb = _heads_per_step(nh)
    wide = hpb * HEAD

    def body(*refs):
        refs[7][...] = jnp.zeros_like(refs[7])
        heads = [one_head(*_head_views(refs, hh, lanes=(0, 1, 2, 3, 4, 5, 6), lead=(7, 8))) for hh in range(hpb)]

        def chunks(ci, carry):
            for chunk in heads:
                chunk(ci)
            return carry

        lax.fori_loop(0, nc, chunks, 0)

    def one_head(qp_ref, fp_ref, iv_ref, gp_ref, lb_ref, on_ref, y_ref, st_ref, b_ref):
        lower, _, krow, arow, acol = _hg_masks()
        lb_row, gam = lb_ref[...], on_ref[...]

        def chunk(ci):
            rs = pl.ds(pl.multiple_of(ci * c, c), c)
            v = iv_ref[rs, :].astype(BF16)
            gp = gp_ref[rs, :]
            q, k, g, _, _, _ = _hg_gates(qp_ref[rs, :], fp_ref[rs, :], lb_row, has_lb)
            b = _split_dot(g, lower, 3, left=True)
            b_ref[...] = b
            st = st_ref[...]
            o = _dot_nt((q * jnp.exp(b)).astype(BF16), st.astype(BF16))
            amats = [_hg_sub(i, q, k, b, b_ref, krow, arow, acol)[-1] for i in range(c // HG_SUB)]
            o = o + _dot(jnp.concatenate(amats, axis=0), v)
            bl = b_ref[c - 1:c, :]
            st_ref[...] = jnp.exp(bl) * st + _dot_tn(v, (k * jnp.exp(bl - b)).astype(BF16))
            r = lax.rsqrt(jnp.mean(o * o, axis=-1, keepdims=True) + NORM_EPS)
            y_ref[rs, :] = (o * r * gam * (gp * _sigmoid(gp))).astype(BF16)

        return chunk

    ns = nh // hpb
    col = lambda p: pl.BlockSpec((t, wide), lambda h: (0, p * ns + h))
    vec = pl.BlockSpec((1, wide), lambda h: (0, h))
    return _host_call(
        body, 6, plan, lambda: (pl.program_id(0), ns),
        out_shape=jax.ShapeDtypeStruct((t, kw), BF16), grid=(ns,),
        in_specs=[col(0), col(1), col(2), col(3), vec, vec],
        out_specs=pl.BlockSpec((t, wide), lambda h: (0, h)),
        scratch_shapes=[pltpu.VMEM((hpb, HEAD, HEAD), F32), pltpu.VMEM((hpb, c, HEAD), F32)],
        name=name, sem=("parallel",), args=[proj, proj, proj, proj, lb, onorm_g])


def hgrn_bwd(proj, lb, onorm_g, dy, has_lb, name, plan=None):
    t, w4 = proj.shape
    kw = w4 // 4
    nh, nc, c = kw // HEAD, t // HG_CHUNK, HG_CHUNK
    hpb = _heads_per_step(nh)
    wide = hpb * HEAD

    def body(*refs):
        for i in (8, 9, 13, 14):
            refs[i][...] = jnp.zeros_like(refs[i])
        heads = [one_head(*_head_views(refs, hh, lanes=(0, 1, 2, 3, 4, 5, 6, 7, 8, 9, 11), lead=(10, 12, 13, 14)))
                 for hh in range(hpb)]

        def fwd_chunks(ci, states):
            return tuple(fwd(ci, st) for (fwd, _), st in zip(heads, states))

        lax.fori_loop(0, nc, fwd_chunks, tuple(jnp.zeros((HEAD, HEAD), F32) for _ in heads))

        def bwd_chunks(step, carry):
            for _, bwd in heads:
                bwd(nc - 1 - step)
            return carry

        lax.fori_loop(0, nc, bwd_chunks, 0)

    def one_head(qp_ref, fp_ref, iv_ref, gp_ref, lb_ref, on_ref, dy_ref,
                 dp_ref, dgam_ref, dlb_ref, sst_ref, o_ref, b_ref, dst_ref, car_ref):
        lower, upper, krow, arow, acol = _hg_masks()
        lb_row, gam = lb_ref[...], on_ref[...]
        mm = _dot3 if has_lb else _dot1

        def recompute(ci):
            rs = pl.ds(pl.multiple_of(ci * c, c), c)
            gates = _hg_gates(qp_ref[rs, :], fp_ref[rs, :], lb_row, has_lb)
            b = _split_dot(gates[2], lower, 3, left=True)
            b_ref[...] = b
            return rs, gates, b

        def fwd_chunk(ci, carry):
            rs, (q, k, g, _, _, _), b = recompute(ci)
            v = iv_ref[rs, :].astype(BF16)
            st = carry
            sst_ref[ci] = st
            o = _dot_nt((q * jnp.exp(b)).astype(BF16), st.astype(BF16))
            amats = [_hg_sub(i, q, k, b, b_ref, krow, arow, acol)[-1] for i in range(c // HG_SUB)]
            o_ref[rs, :] = o + _dot(jnp.concatenate(amats, axis=0), v)
            bl = b_ref[c - 1:c, :]
            return jnp.exp(bl) * st + mm(iv_ref[rs, :], k * jnp.exp(bl - b), _dot_tn)

        def bwd_chunk(ci):
            rs, (q, k, g, sig, nsig, w), b = recompute(ci)
            qp, gp, v32 = qp_ref[rs, :], gp_ref[rs, :], iv_ref[rs, :]
            v = v32.astype(BF16)
            dyv, o = dy_ref[rs, :], o_ref[rs, :]
            st = sst_ref[ci]
            dst_new = dst_ref[...]
            r = lax.rsqrt(jnp.mean(o * o, axis=-1, keepdims=True) + NORM_EPS)
            on = o * r
            sgm = _sigmoid(gp)
            sg = gp * sgm
            dgam_ref[...] += jnp.sum(dyv * sg * on, axis=0, keepdims=True)
            dgp = dyv * on * gam * (sgm * (1.0 + gp * (1.0 - sgm)))
            dn = dyv * gam * sg
            do = r * dn - o * (r * r * r * jnp.mean(dn * o, axis=-1, keepdims=True))
            eb = jnp.exp(b)
            bl = b_ref[c - 1:c, :]
            ekd = jnp.exp(bl - b)
            do16 = do.astype(BF16)
            dq = mm(do, st, _dot) * eb
            dv = _dot1(k * ekd, dst_new, _dot_nt)
            dk = mm(v32, dst_new, _dot) * ekd
            dst_ref[...] = jnp.exp(bl) * dst_new + mm(do, q * eb, _dot_tn)
            da_all = mm(do, v32, _dot_nt)
            dq_parts, amats = [], []
            for i in range(c // HG_SUB):
                r0 = i * HG_SUB
                ebq, _, ek, _, mask, amat = _hg_sub(i, q, k, b, b_ref, krow, arow, acol)
                da = jnp.where(mask, da_all[r0:r0 + HG_SUB, :], 0.0)
                dq_parts.append(mm(da, k * ek, _dot) * ebq)
                dk = dk + mm(da, q[r0:r0 + HG_SUB, :] * ebq, _dot_tn) * ek
                amats.append(amat)
            dq = dq + jnp.concatenate(dq_parts, axis=0)
            dv = dv + _dot_tn(jnp.concatenate(amats, axis=0), do16)
            db = q * dq - k * dk
            dg = car_ref[...] + _split_dot(db, upper, 3, left=True)
            car_ref[...] += jnp.sum(db, axis=0, keepdims=True)
            if has_lb:
                dfp = dg * nsig * w - dk * ((1.0 - lb_row) * sig * nsig)
                dlb_ref[...] += jnp.sum(dg * nsig * jnp.exp(-g) - dk * nsig, axis=0, keepdims=True)
            else:
                dfp = dg * nsig - dk * (sig * nsig)
            sq = _sigmoid(qp)
            dp_ref[0, rs, :] = (dq * (sq * (1.0 + qp * (1.0 - sq)))).astype(BF16)
            dp_ref[1, rs, :] = dfp.astype(BF16)
            dp_ref[2, rs, :] = dv.astype(BF16)
            dp_ref[3, rs, :] = dgp.astype(BF16)

        return fwd_chunk, bwd_chunk

    ns = nh // hpb
    col = lambda p: pl.BlockSpec((t, wide), lambda h: (0, p * ns + h))
    vec = pl.BlockSpec((1, wide), lambda h: (0, h))
    return _host_call(
        body, 7, plan, lambda: (pl.program_id(0), ns),
        out_shape=[jax.ShapeDtypeStruct((4, t, kw), BF16), jax.ShapeDtypeStruct((1, kw), F32),
                   jax.ShapeDtypeStruct((1, kw), F32)],
        grid=(ns,),
        in_specs=[col(0), col(1), col(2), col(3), vec, vec, pl.BlockSpec((t, wide), lambda h: (0, h))],
        out_specs=[pl.BlockSpec((4, t, wide), lambda h: (0, 0, h)), vec, vec],
        scratch_shapes=[pltpu.VMEM((hpb, nc, HEAD, HEAD), F32), pltpu.VMEM((t, wide), F32),
                        pltpu.VMEM((hpb, c, HEAD), F32), pltpu.VMEM((hpb, HEAD, HEAD), F32),
                        pltpu.VMEM((hpb, 1, HEAD), F32)],
        name=name, sem=("parallel",), args=[proj, proj, proj, proj, lb, onorm_g, dy])


def _sb_masks():
    m, n = SB_QROWS, SB_BLOCK
    row = lax.broadcasted_iota(jnp.int32, (m, n), 0)
    col = lax.broadcasted_iota(jnp.int32, (m, n), 1)
    r2 = lax.broadcasted_iota(jnp.int32, (n, n), 0)
    c2 = lax.broadcasted_iota(jnp.int32, (n, n), 1)
    after = (r2 > c2).astype(BF16)
    from_ = (r2 >= c2).astype(BF16)
    return row, col, after, from_


def sba_fwd(qkv, name, plan=None):
    t, w3 = qkv.shape
    wd = w3 // 3
    m, n = SB_QROWS, SB_BLOCK
    nh, nq, per = wd // HEAD, t // m, m // n
    scale = HEAD ** -0.5

    def body(q_ref, k_ref, v_ref, o_ref, q16, k16, v16):
        row, col, after, _ = _sb_masks()
        q16[...] = (q_ref[...] * scale).astype(BF16)
        k16[...] = k_ref[...].astype(BF16)
        v16[...] = v_ref[...].astype(BF16)

        def qblock(qi, carry):
            qs = pl.ds(pl.multiple_of(qi * m, m), m)
            q = q16[qs, :]
            last = (qi + 1) * per - 1

            def kblock(step, state, masked):
                acc, rem0 = state
                kj = last - step
                ks = pl.ds(pl.multiple_of(kj * n, n), n)
                z = _dot_nt(q, k16[ks, :])
                spz = _softplus(z)
                strict = (col + kj * n) < (row + qi * m)
                sp = jnp.where(strict, spz, 0.0) if masked else spz
                rem = rem0 + _split_dot(sp, after, 2)
                a = jnp.exp(z - spz - rem)
                if masked:
                    a = jnp.where(strict, a, 0.0)
                acc = acc + _dot(a.astype(BF16), v16[ks, :])
                return acc, rem0 + jnp.sum(sp, axis=1, keepdims=True)

            state = (jnp.zeros((m, HEAD), F32), jnp.zeros((m, 1), F32))
            for step in range(per):
                state = kblock(step, state, True)
            acc, _ = lax.fori_loop(per, last + 1, functools.partial(kblock, masked=False), state)
            o_ref[qs, :] = acc
            return carry

        lax.fori_loop(0, nq, qblock, 0)

    col_spec = lambda p: pl.BlockSpec((t, HEAD), lambda h: (0, p * nh + h))
    return _host_call(
        body, 3, plan, lambda: (pl.program_id(0), nh),
        out_shape=jax.ShapeDtypeStruct((t, wd), F32), grid=(nh,),
        in_specs=[col_spec(0), col_spec(1), col_spec(2)],
        out_specs=pl.BlockSpec((t, HEAD), lambda h: (0, h)),
        scratch_shapes=[pltpu.VMEM((t, HEAD), BF16)] * 3,
        name=name, sem=("parallel",), args=[qkv, qkv, qkv])


def sba_bwd(qkv, o, do, name, plan=None):
    t, w3 = qkv.shape
    wd = w3 // 3
    m, n = SB_QROWS, SB_BLOCK
    nh, nq, per = wd // HEAD, t // m, m // n
    scale = HEAD ** -0.5

    def body(q_ref, k_ref, v_ref, o_ref, do_ref, d_ref, dk_ref, dv_ref, q16, k16, v16):
        row, col, after, from_ = _sb_masks()
        dk_ref[...] = jnp.zeros_like(dk_ref)
        dv_ref[...] = jnp.zeros_like(dv_ref)
        q16[...] = (q_ref[...] * scale).astype(BF16)
        k16[...] = k_ref[...].astype(BF16)
        v16[...] = v_ref[...].astype(BF16)

        def qblock(qi, carry):
            qs = pl.ds(pl.multiple_of(qi * m, m), m)
            q = q16[qs, :]
            dov = do_ref[qs, :]
            do16 = dov.astype(BF16)
            dsum = jnp.sum(do16.astype(F32) * o_ref[qs, :], axis=1, keepdims=True)
            last = (qi + 1) * per - 1

            def kblock(step, state, masked):
                dq, rem0, e0 = state
                kj = last - step
                ks = pl.ds(pl.multiple_of(kj * n, n), n)
                kv, vv = k16[ks, :], v16[ks, :]
                z = _dot_nt(q, kv)
                strict = (col + kj * n) < (row + qi * m)
                spz = _softplus(z)
                sp = jnp.where(strict, spz, 0.0) if masked else spz
                rem = rem0 + _split_dot(sp, after, 2)
                sgz = jnp.exp(z - spz)
                a = sgz * jnp.exp(-rem)
                if masked:
                    a = jnp.where(strict, a, 0.0)
                a = a.astype(BF16)
                e = a.astype(F32) * _dot_nt(do16, vv)
                left = dsum - (e0 + _split_dot(e, from_, 2))
                dz = e * (1.0 - sgz) - sgz * left
                if masked:
                    dz = jnp.where(strict, dz, 0.0)
                dz = dz.astype(BF16)
                dq = dq + _dot(dz, kv)
                dk_ref[ks, :] += _dot_tn(dz, q)
                dv_ref[ks, :] += _dot_tn(a, do16)
                return (dq, rem0 + jnp.sum(sp, axis=1, keepdims=True),
                        e0 + jnp.sum(e, axis=1, keepdims=True))

            zero1 = jnp.zeros((m, 1), F32)
            state = (jnp.zeros((m, HEAD), F32), zero1, zero1)
            for step in range(per):
                state = kblock(step, state, True)
            dq, _, _ = lax.fori_loop(per, last + 1, functools.partial(kblock, masked=False), state)
            d_ref[0, qs, :] = (dq * scale).astype(BF16)
            return carry

        lax.fori_loop(0, nq, qblock, 0)
        d_ref[1, :, :] = dk_ref[...].astype(BF16)
        d_ref[2, :, :] = dv_ref[...].astype(BF16)

    col_spec = lambda p: pl.BlockSpec((t, HEAD), lambda h: (0, p * nh + h))
    head = pl.BlockSpec((t, HEAD), lambda h: (0, h))
    return _host_call(
        body, 5, plan, lambda: (pl.program_id(0), nh),
        out_shape=jax.ShapeDtypeStruct((3, t, wd), BF16), grid=(nh,),
        in_specs=[col_spec(0), col_spec(1), col_spec(2), head, head],
        out_specs=pl.BlockSpec((3, t, HEAD), lambda h: (0, 0, h)),
        scratch_shapes=[pltpu.VMEM((t, HEAD), F32)] * 2 + [pltpu.VMEM((t, HEAD), BF16)] * 3,
        name=name, sem=("parallel",), args=[qkv, qkv, qkv, o, do])


def _pool_band(i_out, i_in, tr, win, transpose):
    r = lax.broadcasted_iota(jnp.int32, (tr, tr), 0) + i_out * tr
    c = lax.broadcasted_iota(jnp.int32, (tr, tr), 1) + i_in * tr
    if transpose:
        return ((r <= c) & (r > c - win)).astype(F32)
    return ((c <= r) & (c > r - win)).astype(F32)


def _pool_p(u_ref, i, tr, win):
    cur = u_ref[i * tr:(i + 1) * tr, :]
    ws = _dot(_pool_band(i, i, tr, win, False), cur, HI)
    if i > 0:
        ws = ws + _dot(_pool_band(i, i - 1, tr, win, False), u_ref[(i - 1) * tr:i * tr, :], HI)
    pos = lax.broadcasted_iota(jnp.int32, (tr, 1), 0) + (i * tr + 1)
    count = jnp.minimum(pos, win).astype(F32)
    return ws / count - cur, count


def pool_fwd(u, pool_w, pool_scale, name):
    t, d = u.shape
    ng = len(POOL_WINDOWS)
    gs = d // ng
    tr = min(ROW_TILE, t)

    def body(u_ref, w_ref, s_ref, y_ref):
        win = jnp.left_shift(2, pl.program_id(0))
        for i in range(t // tr):
            p, _ = _pool_p(u_ref, i, tr, win)
            y_ref[i * tr:(i + 1) * tr, :] = _dot(p.astype(BF16), w_ref[...]) * s_ref[...]

    grp = pl.BlockSpec((t, gs), lambda g: (0, g))
    return pl.pallas_call(
        body, out_shape=jax.ShapeDtypeStruct((t, d), F32), grid=(ng,),
        in_specs=[grp, pl.BlockSpec((None, gs, gs), lambda g: (g, 0, 0)), pl.BlockSpec((1, gs), lambda g: (0, g))],
        out_specs=grp, name=name, compiler_params=_params(("parallel",)),
    )(u, pool_w, pool_scale)


def pool_bwd(u, pool_w, pool_scale, dy, name):
    t, d = u.shape
    ng = len(POOL_WINDOWS)
    gs = d // ng
    tr = min(ROW_TILE, t)
    nt = t // tr

    def body(u_ref, w_ref, s_ref, dy_ref, du_ref, dw_ref, ds_ref, dpc_ref, dp_ref):
        win = jnp.left_shift(2, pl.program_id(0))
        wv = w_ref[...]
        dw = jnp.zeros((gs, gs), F32)
        dsc = jnp.zeros((1, gs), F32)
        for i in range(nt):
            rows = slice(i * tr, (i + 1) * tr)
            p, count = _pool_p(u_ref, i, tr, win)
            p16 = p.astype(BF16)
            dyv = dy_ref[rows, :]
            dsc = dsc + jnp.sum(dyv * _dot(p16, wv), axis=0, keepdims=True)
            dyp = (dyv * s_ref[...]).astype(BF16)
            dw = dw + _dot_tn(p16, dyp)
            dp = _dot_nt(dyp, wv)
            dp_ref[rows, :] = dp
            dpc_ref[rows, :] = dp / count
        dw_ref[...] = dw.astype(BF16)
        ds_ref[...] = dsc
        for i in range(nt):
            rows = slice(i * tr, (i + 1) * tr)
            acc = _dot(_pool_band(i, i, tr, win, True), dpc_ref[rows, :], HI)
            if i + 1 < nt:
                acc = acc + _dot(_pool_band(i, i + 1, tr, win, True), dpc_ref[(i + 1) * tr:(i + 2) * tr, :], HI)
            du_ref[rows, :] = acc - dp_ref[rows, :]

    grp = pl.BlockSpec((t, gs), lambda g: (0, g))
    wspec = pl.BlockSpec((None, gs, gs), lambda g: (g, 0, 0))
    vec = pl.BlockSpec((1, gs), lambda g: (0, g))
    return pl.pallas_call(
        body,
        out_shape=[jax.ShapeDtypeStruct((t, d), F32), jax.ShapeDtypeStruct((ng, gs, gs), BF16),
                   jax.ShapeDtypeStruct((1, d), F32)],
        grid=(ng,), in_specs=[grp, wspec, vec, grp], out_specs=[grp, wspec, vec],
        scratch_shapes=[pltpu.VMEM((t, gs), F32), pltpu.VMEM((t, gs), F32)],
        name=name, compiler_params=_params(("parallel",)),
    )(u, pool_w, pool_scale, dy)


CONV_COLS = 256
HALO = 8


def _conv_taps(ref, r0, tr):
    x = ref[r0:r0 + tr, :]
    prev = ref[r0 - HALO:r0, :] if r0 > 0 else jnp.zeros((HALO, x.shape[1]), F32)
    xx = jnp.concatenate([prev, x], axis=0)
    return x, pltpu.roll(xx, 1, 0)[HALO:, :], pltpu.roll(xx, 2, 0)[HALO:, :]


def _conv_out(taps, w_ref, b_ref):
    x, s1, s2 = taps
    return w_ref[0:1, :] * s2 + w_ref[1:2, :] * s1 + w_ref[2:3, :] * x + b_ref[...]


def conv_glu_fwd(up, conv_w, conv_b, name):
    t, f2 = up.shape
    f = f2 // 2
    tc = min(CONV_COLS, f)
    nj = f // tc
    tr = min(ROW_TILE, t)

    def body(ug_ref, uv_ref, wg_ref, wv_ref, bg_ref, bv_ref, o_ref):
        for i in range(t // tr):
            r0 = i * tr
            gate = _conv_out(_conv_taps(ug_ref, r0, tr), wg_ref, bg_ref)
            val = _conv_out(_conv_taps(uv_ref, r0, tr), wv_ref, bv_ref)
            o_ref[r0:r0 + tr, :] = (gate * _sigmoid(gate) * val).astype(BF16)

    blk = lambda rows, half: pl.BlockSpec((rows, tc), lambda j: (0, half * nj + j))
    return pl.pallas_call(
        body, out_shape=jax.ShapeDtypeStruct((t, f), BF16), grid=(nj,),
        in_specs=[blk(t, 0), blk(t, 1), blk(CONV_WIDTH, 0), blk(CONV_WIDTH, 1), blk(1, 0), blk(1, 1)],
        out_specs=pl.BlockSpec((t, tc), lambda j: (0, j)),
        name=name, compiler_params=_params(("parallel",)),
    )(up, up, conv_w, conv_w, conv_b, conv_b)


def conv_glu_bwd(up, conv_w, conv_b, dact, name, plan=None):
    t, f2 = up.shape
    f = f2 // 2
    tc = min(CONV_COLS, f)
    nj = f // tc
    tr = min(ROW_TILE, t)
    nt = t // tr

    def body(ug_ref, uv_ref, wg_ref, wv_ref, bg_ref, bv_ref, da_ref, du_ref, dw_ref, db_ref, dg_ref, dv_ref):
        dwg = [jnp.zeros((1, tc), F32) for _ in range(CONV_WIDTH)]
        dwv = [jnp.zeros((1, tc), F32) for _ in range(CONV_WIDTH)]
        dbg = jnp.zeros((1, tc), F32)
        dbv = jnp.zeros((1, tc), F32)
        for i in range(nt):
            r0 = i * tr
            tg, tv = _conv_taps(ug_ref, r0, tr), _conv_taps(uv_ref, r0, tr)
            gate, val = _conv_out(tg, wg_ref, bg_ref), _conv_out(tv, wv_ref, bv_ref)
            sg = _sigmoid(gate)
            da = da_ref[r0:r0 + tr, :]
            d_gate = da * val * (sg * (1.0 + gate * (1.0 - sg)))
            d_val = da * (gate * sg)
            dg_ref[r0:r0 + tr, :] = d_gate
            dv_ref[r0:r0 + tr, :] = d_val
            dbg = dbg + jnp.sum(d_gate, axis=0, keepdims=True)
            dbv = dbv + jnp.sum(d_val, axis=0, keepdims=True)
            for tap in range(CONV_WIDTH):
                dwg[tap] = dwg[tap] + jnp.sum(d_gate * tg[2 - tap], axis=0, keepdims=True)
                dwv[tap] = dwv[tap] + jnp.sum(d_val * tv[2 - tap], axis=0, keepdims=True)
        for tap in range(CONV_WIDTH):
            dw_ref[0, tap:tap + 1, :] = dwg[tap]
            dw_ref[1, tap:tap + 1, :] = dwv[tap]
        db_ref[0, :, :] = dbg
        db_ref[1, :, :] = dbv
        for half, (d_ref, w_ref) in enumerate(((dg_ref, wg_ref), (dv_ref, wv_ref))):
            for i in range(nt):
                r0 = i * tr
                x = d_ref[r0:r0 + tr, :]
                nxt = d_ref[r0 + tr:r0 + tr + HALO, :] if i + 1 < nt else jnp.zeros((HALO, tc), F32)
                xx = jnp.concatenate([x, nxt], axis=0)
                up1 = pltpu.roll(xx, tr + HALO - 1, 0)[:tr, :]
                up2 = pltpu.roll(xx, tr + HALO - 2, 0)[:tr, :]
                du = w_ref[2:3, :] * x + w_ref[1:2, :] * up1 + w_ref[0:1, :] * up2
                du_ref[half, r0:r0 + tr, :] = du.astype(BF16)

    blk = lambda rows, half: pl.BlockSpec((rows, tc), lambda j: (0, half * nj + j))
    return _host_call(
        body, 7, plan, lambda: (pl.program_id(0), nj),
        out_shape=[jax.ShapeDtypeStruct((2, t, f), BF16), jax.ShapeDtypeStruct((2, CONV_WIDTH, f), F32),
                   jax.ShapeDtypeStruct((2, 1, f), F32)],
        grid=(nj,),
        in_specs=[blk(t, 0), blk(t, 1), blk(CONV_WIDTH, 0), blk(CONV_WIDTH, 1), blk(1, 0), blk(1, 1),
                  pl.BlockSpec((t, tc), lambda j: (0, j))],
        out_specs=[pl.BlockSpec((2, t, tc), lambda j: (0, 0, j)),
                   pl.BlockSpec((2, CONV_WIDTH, tc), lambda j: (0, 0, j)),
                   pl.BlockSpec((2, 1, tc), lambda j: (0, 0, j))],
        scratch_shapes=[pltpu.VMEM((t, tc), F32), pltpu.VMEM((t, tc), F32)],
        name=name, sem=("parallel",), args=[up, up, conv_w, conv_w, conv_b, conv_b, dact])


def _place():
    x, y, c = lax.axis_index("x"), lax.axis_index("y"), lax.axis_index("c")
    others = [(1 - x, y), (x, 1 - y), (1 - x, 1 - y)]
    return x, y, c, others


def _window(ref, axis, b, n):
    if axis == 1:
        return ref.at[:, pl.ds(b * n, n), :]
    return ref.at[:, :, pl.ds(b * n, n)]


def _remote(src, dst, send_sems, recv_sems, k, to):
    return pltpu.make_async_remote_copy(src_ref=src, dst_ref=dst, send_sem=send_sems.at[k],
                                        recv_sem=recv_sems.at[k], device_id=to, device_id_type=MESH)


class GatherPlan:
    has_mid = True

    def __init__(self, items):
        self.items = items
        self.args = [s for s, _, _, _ in items]
        self.out_shape = []
        for s, _, nl, ax in items:
            shp = [nl, s.shape[1], s.shape[2]]
            shp[ax] *= N_DEV
            self.out_shape.append(jax.ShapeDtypeStruct(tuple(shp), s.dtype))
        n = len(items)
        self.scratch = [pltpu.SemaphoreType.DMA((7 * n,)), pltpu.SemaphoreType.DMA((7 * n,)),
                        pltpu.SemaphoreType.DMA((n,))]

    def _mine(self, ins, a):
        _, l0, nl, _ = self.items[a]
        return ins[a].at[pl.ds(l0, nl)]

    def _copy(self, ins, outs, sems, a, k, block, to, own=False):
        s, _, _, ax = self.items[a]
        px, py, pc = block
        w = _window(outs[a], ax, 4 * px + 2 * py + pc, s.shape[ax])
        return _remote(self._mine(ins, a) if own else w, w, sems[0], sems[1], 7 * a + k, to)

    def _local(self, ins, outs, sems, a, x, y, c):
        s, _, _, ax = self.items[a]
        return pltpu.make_async_copy(self._mine(ins, a), _window(outs[a], ax, 4 * x + 2 * y + c, s.shape[ax]),
                                     sems[2].at[a])


    def _first(self, ins, outs, sems, a, x, y, c, others):
        me = (x, y, c)
        return [self._copy(ins, outs, sems, a, 0, me, (x, y, 1 - c), own=True)] + [
            self._copy(ins, outs, sems, a, 1 + j, me, (*chip, c), own=True) for j, chip in enumerate(others[:2])]

    def _relay(self, ins, outs, sems, a, x, y, c):
        block = ((1 - c) * (1 - x) + c * x, (1 - c) * y + c * (1 - y), c)
        to = ((1 - c) * x + c * (1 - x), (1 - c) * (1 - y) + c * y, c)
        return self._copy(ins, outs, sems, a, 3, block, to)

    def _passed(self, ins, outs, sems, a, j, x, y, c, others):
        return self._copy(ins, outs, sems, a, 4 + j, (*others[j], c), (x, y, 1 - c))

    def start(self, ins, outs, sems):
        x, y, c, others = _place()
        for a in range(len(self.items)):
            self._local(ins, outs, sems, a, x, y, c).start()
        for a in range(len(self.items)):
            for cp in self._first(ins, outs, sems, a, x, y, c, others):
                cp.start()

    def mid(self, ins, outs, sems):
        x, y, c, others = _place()
        for a in range(len(self.items)):
            for j, chip in enumerate(others[:2]):
                self._copy(ins, outs, sems, a, 1 + j, (*chip, c), (x, y, c)).wait_recv()
            self._relay(ins, outs, sems, a, x, y, c).start()
            for j in range(2):
                self._passed(ins, outs, sems, a, j, x, y, c, others).start()

    def finish(self, ins, outs, sems):
        x, y, c, others = _place()
        for a in range(len(self.items)):
            self._copy(ins, outs, sems, a, 3, (*others[2], c), (x, y, c)).wait_recv()
            self._passed(ins, outs, sems, a, 2, x, y, c, others).start()
        for a in range(len(self.items)):
            self._copy(ins, outs, sems, a, 0, (x, y, 1 - c), (x, y, c)).wait_recv()
            for j, chip in enumerate(others):
                self._copy(ins, outs, sems, a, 4 + j, (*chip, 1 - c), (x, y, c)).wait_recv()
        for a in range(len(self.items)):
            for cp in self._first(ins, outs, sems, a, x, y, c, others):
                cp.wait_send()
            self._relay(ins, outs, sems, a, x, y, c).wait_send()
            for j in range(3):
                self._passed(ins, outs, sems, a, j, x, y, c, others).wait_send()
            self._local(ins, outs, sems, a, x, y, c).wait()


class ExchangePlan:
    has_mid = False

    def __init__(self, partials):
        self.args = list(partials)
        self.out_shape = [jax.ShapeDtypeStruct(p.shape, p.dtype) for p in partials]
        n = len(partials)
        self.scratch = [pltpu.SemaphoreType.DMA((3 * n,)), pltpu.SemaphoreType.DMA((3 * n,)),
                        pltpu.SemaphoreType.DMA((n,))]

    def _copies(self, ins, outs, sems):
        x, y, c, others = _place()
        me = 2 * x + y
        local, sends, recvs = [], [], []
        for a in range(len(self.args)):
            local.append(pltpu.make_async_copy(ins[a].at[me], outs[a].at[me], sems[2].at[a]))
            for j, (px, py) in enumerate(others):
                sends.append(_remote(ins[a].at[2 * px + py], outs[a].at[me], sems[0], sems[1], 3 * a + j, (px, py, c)))
                slot = outs[a].at[2 * px + py]
                recvs.append(_remote(slot, slot, sems[0], sems[1], 3 * a + j, (px, py, c)))
        return local, sends, recvs

    def start(self, ins, outs, sems):
        local, sends, _ = self._copies(ins, outs, sems)
        for cp in local + sends:
            cp.start()

    def finish(self, ins, outs, sems):
        local, sends, recvs = self._copies(ins, outs, sems)
        for cp in recvs:
            cp.wait_recv()
        for cp in sends:
            cp.wait_send()
        for cp in local:
            cp.wait()


def run_plan(plan, name):
    ni, no = len(plan.args), len(plan.out_shape)

    def body(*refs):
        ins, outs, sems = refs[:ni], refs[ni:ni + no], refs[ni + no:]
        plan.start(ins, outs, sems)
        if plan.has_mid:
            plan.mid(ins, outs, sems)
        plan.finish(ins, outs, sems)

    return pl.pallas_call(
        body, out_shape=plan.out_shape, in_specs=[ANY] * ni, out_specs=[ANY] * no,
        scratch_shapes=plan.scratch, name=name,
    )(*plan.args)


class SiblingPlan:
    has_mid = False

    def __init__(self, grads, axes):
        self.args, self.axes = list(grads), list(axes)
        self.widths = [g.shape[ax] // N_DEV for g, ax in zip(grads, axes)]
        self.out_shape = []
        for g, ax, n in zip(grads, axes, self.widths):
            shp = list(g.shape)
            shp[ax] = n
            self.out_shape.append(jax.ShapeDtypeStruct((N_CHIP, *shp), g.dtype))
        n = len(grads)
        self.scratch = [pltpu.SemaphoreType.DMA((N_CHIP * n,)), pltpu.SemaphoreType.DMA((N_CHIP * n,))]

    def _copies(self, ins, outs, sems):
        x, y, c, _ = _place()
        copies = []
        for a in range(len(self.args)):
            for q in range(N_CHIP):
                src = _window(ins[a], self.axes[a], 2 * q + (1 - c), self.widths[a])
                copies.append(_remote(src, outs[a].at[q], sems[0], sems[1], N_CHIP * a + q, (x, y, 1 - c)))
        return copies

    def start(self, ins, outs, sems):
        for cp in self._copies(ins, outs, sems):
            cp.start()

    def finish(self, ins, outs, sems):
        copies = self._copies(ins, outs, sems)
        for cp in copies:
            cp.wait_recv()
        for cp in copies:
            cp.wait_send()


def _peer_of(k, x, y, c):
    return (1 - x if k & 4 else x, 1 - y if k & 2 else y, 1 - c if k & 1 else c)


def small_exchange(vec, reduce, name):
    r = vec.shape[0]

    def body(v_ref, o_ref, *scratch):
        if reduce:
            buf, send_sems, recv_sems = scratch
        else:
            buf, (send_sems, recv_sems) = o_ref, scratch
        x, y, c, _ = _place()
        me = 4 * x + 2 * y + c
        copies = []
        for k in range(1, N_DEV):
            cp = _remote(v_ref, buf.at[me], send_sems, recv_sems, k - 1, _peer_of(k, x, y, c))
            cp.start()
            copies.append(cp)
        buf[me] = v_ref[...]
        for k in range(1, N_DEV):
            px, py, pc = _peer_of(k, x, y, c)
            slot = buf.at[4 * px + 2 * py + pc]
            _remote(slot, slot, send_sems, recv_sems, k - 1, (px, py, pc)).wait_recv()
        for cp in copies:
            cp.wait_send()
        if reduce:
            tot = buf[0]
            for b in range(1, N_DEV):
                tot = tot + buf[b]
            o_ref[...] = tot

    sems = [pltpu.SemaphoreType.DMA((N_DEV - 1,)), pltpu.SemaphoreType.DMA((N_DEV - 1,))]
    if reduce:
        out_shape = jax.ShapeDtypeStruct((r, LANES), F32)
        scratch = [pltpu.VMEM((N_DEV, r, LANES), F32)] + sems
    else:
        out_shape = jax.ShapeDtypeStruct((N_DEV, r, LANES), F32)
        scratch = sems
    return pl.pallas_call(
        body, out_shape=out_shape, in_specs=[VMEM_FULL], out_specs=VMEM_FULL,
        scratch_shapes=scratch, name=name, compiler_params=_params(None),
    )(vec)


def _row_tile(rows, cols, limit=ELEMWISE_BLOCK_ELEMS):
    best = None
    for tb in range(16, rows + 1, 16):
        if rows % tb == 0 and tb * cols <= limit:
            best = tb
    return best if best is not None else rows


def add_sibling(grad, recv, axis, core):
    nl = grad.shape[0]
    _, _, r, cc = recv.shape
    tb = _row_tile(r, cc, 4 * ELEMWISE_BLOCK_ELEMS)
    per = r // tb

    def body(c_ref, g_ref, r_ref, o_ref):
        del c_ref
        o_ref[...] = (g_ref[...].astype(F32) + r_ref[...].astype(F32)).astype(BF16)

    if axis == 2:
        g_spec = pl.BlockSpec((None, tb, cc), lambda q, l, i, c_ref: (l, i, 2 * q + c_ref[0]))
    else:
        g_spec = pl.BlockSpec((None, tb, cc), lambda q, l, i, c_ref: (l, (2 * q + c_ref[0]) * per + i, 0))
    slot = pl.BlockSpec((None, None, tb, cc), lambda q, l, i, c_ref: (q, l, i, 0))
    return pl.pallas_call(
        body, out_shape=jax.ShapeDtypeStruct(recv.shape, BF16),
        grid_spec=pltpu.PrefetchScalarGridSpec(
            num_scalar_prefetch=1, grid=(N_CHIP, nl, per), in_specs=[g_spec, slot], out_specs=slot),
        name="add_sibling", compiler_params=_params(("parallel", "parallel", "parallel")),
    )(core, grad, recv)


def _adamw(w, g, m, v):
    m = ADAM_B1 * m + (1.0 - ADAM_B1) * g
    v = ADAM_B2 * v + (1.0 - ADAM_B2) * (g * g)
    m_hat = m / (1.0 - ADAM_B1 ** ADAM_STEP)
    v_hat = v / (1.0 - ADAM_B2 ** ADAM_STEP)
    delta = -ADAM_LR * (m_hat / (jnp.sqrt(v_hat) + ADAM_EPS) + ADAM_WD * w)
    return delta, m, v


def adam_from_partials(recv, w, m, v, l0, bufs):
    nl = recv.shape[1]
    _, r, cc = w.shape
    tb = _row_tile(r, cc)

    def body(p0, p1, p2, p3, w_ref, m_ref, v_ref, b0, b1, b2, b3, g_out, d_out, m_out, v_out):
        del b0, b1, b2, b3
        g = p0[...].astype(F32) + p1[...].astype(F32) + p2[...].astype(F32) + p3[...].astype(F32)
        d, mn, vn = _adamw(w_ref[...], g, m_ref[...], v_ref[...])
        g_out[...], d_out[...], m_out[...], v_out[...] = g, d, mn, vn

    slot = lambda q: pl.BlockSpec((None, None, tb, cc), lambda l, i: (q, l, i, 0))
    blk = pl.BlockSpec((None, tb, cc), lambda l, i: (l0 + l, i, 0))
    shp = jax.ShapeDtypeStruct(w.shape, F32)
    return pl.pallas_call(
        body, out_shape=[shp] * 4, grid=(nl, r // tb),
        in_specs=[slot(0), slot(1), slot(2), slot(3), blk, blk, blk] + [ANY] * 4, out_specs=[blk] * 4,
        input_output_aliases={7: 0, 8: 1, 9: 2, 10: 3},
        name="adam_big", compiler_params=_params(("parallel", "parallel")),
    )(recv, recv, recv, recv, w, m, v, *bufs)


def adam_small(w, g, m, v):
    def body(w_ref, g_ref, m_ref, v_ref, d_out, m_out, v_out):
        d_out[...], m_out[...], v_out[...] = _adamw(w_ref[...], g_ref[...], m_ref[...], v_ref[...])

    shp = jax.ShapeDtypeStruct(w.shape, F32)
    return pl.pallas_call(body, out_shape=[shp] * 3, name="adam_small")(w, g, m, v)


def _pack(arrays, multiple=8 * LANES):
    flat = jnp.concatenate([a.reshape(-1) for a in arrays])
    pad = (-flat.shape[0]) % multiple
    if pad:
        flat = jnp.concatenate([flat, jnp.zeros((pad,), flat.dtype)])
    return flat.reshape(-1, LANES)


def _unpack(packed, shapes):
    flat = packed.reshape(packed.shape[:-2] + (-1,))
    out, off = [], 0
    for shp in shapes:
        n = math.prod(shp)
        out.append(flat[..., off:off + n].reshape(packed.shape[:-2] + tuple(shp)))
        off += n
    return out


def _unshard_last(stacked):
    moved = jnp.moveaxis(stacked, 0, -2)
    return moved.reshape(moved.shape[:-2] + (-1,))


def _shard_last(full, block):
    n = full.shape[-1] // N_DEV
    return lax.dynamic_slice_in_dim(full, block * n, n, axis=full.ndim - 1)


BIG_AXIS = {"w_in": 2, "w_out": 1, "w_qkv": 2, "w_so": 1, "w_pool": 1, "w_up": 2, "w_down": 1}
BIG_ORDER = ("w_in", "w_out", "w_qkv", "w_so", "w_pool", "w_up", "w_down")

GATHER_HOSTS = {
    "hg_in_0": (("w_out", 0),),
    "hgrn_fwd_0": (("w_up", 0),),
    "ffn_up_0": (("w_down", 0),),
    "ffn_down_0": (("w_qkv", 0),),
    "sb_qkv_1": (("w_so", 0),),
    "sba_fwd_1": (("w_up", 1), ("w_pool", 0), ("w_up", 2)),
    "ffn_up_1": (("w_down", 1),),
    "ffn_down_1": (("w_down", 2),),
    "ffn_up_2": (("w_in", 1),),
    "ffn_down_2": (("w_out", 1),),
    "hgrn_fwd_3": (("w_up", 3),),
    "ffn_up_3": (("w_down", 3),),
}


MEGA = 1 << 20
EXCHANGE_BUDGET = {"d_u2": 13 * MEGA, "dw_up": 13 * MEGA, "hgrn_bwd": 30 * MEGA, "sba_bwd": 70 * MEGA,
                   "d_u1_0": 17 * MEGA}


class MeshWeights:
    def __init__(self, shards16, w, m, v, core):
        self.shards, self.w, self.m, self.v, self.core = shards16, w, m, v, core
        self.full, self.pending, self.ready, self.flying, self.sib_flying, self.keys = {}, [], [], [], [], None
        self.out = {k: [lax.empty(w[k].shape, F32) for _ in range(4)] for k in BIG_ORDER}

    def _items(self, keys):
        return [(self.shards[k], 0 if k == "w_pool" else l, 4 if k == "w_pool" else 1, BIG_AXIS[k]) for k, l in keys]

    def gather_plan(self, host):
        self.keys = [key for key in GATHER_HOSTS.get(host, ()) if key[0] in self.shards
                     and key[1] < (1 if key[0] == "w_pool" else self.shards[key[0]].shape[0])]
        return GatherPlan(self._items(self.keys)) if self.keys else None

    def gathered(self, outs):
        for (k, l), o in zip(self.keys, outs):
            self.full[(k, l)] = o if k == "w_pool" else o[0]
        self.keys = None

    def weight(self, kind, l):
        if (kind, l) not in self.full:
            (out,) = run_plan(GatherPlan(self._items([(kind, l)])), f"gather_{kind}_{l}")
            self.full[(kind, l)] = out if kind == "w_pool" else out[0]
        return self.full[(kind, l)]

    def grad(self, kind, l, g):
        self.pending.append((kind, l, g if g.ndim == 3 else g[None]))

    def exchange_plan(self, host=None):
        budget = next((b for prefix, b in EXCHANGE_BUDGET.items() if host.startswith(prefix)), 0) if host else None
        used = 0

        def fits(queue):
            nonlocal used
            take, keep = [], []
            for item in queue:
                size = item[2].size // (N_CHIP if queue is self.ready else N_DEV)
                if budget is None or (used + size) * N_DEV <= budget:
                    take.append(item)
                    used += size
                else:
                    keep.append(item)
            return take, keep

        done, self.ready = fits(self.ready)
        raw, self.pending = fits(self.pending)
        if raw:
            done = done + self._partial_sums(raw, run_plan(self._sibling_plan(raw), "sibling_exchange"))
        if not done:
            return None
        self.flying = done
        return ExchangePlan([p for _, _, p in done])

    def exchanged(self, outs):
        for (k, l, _), r in zip(self.flying, outs):
            self.out[k] = adam_from_partials(r, self.w[k], self.m[k], self.v[k], l, self.out[k])
        self.flying = []

    def _sibling_plan(self, items):
        return SiblingPlan([g for _, _, g in items], [BIG_AXIS[k] for k, _, _ in items])

    def _partial_sums(self, items, recv):
        return [(k, l, add_sibling(g, r, BIG_AXIS[k], self.core)) for (k, l, g), r in zip(items, recv)]

    def sibling_plan(self):
        self.sib_flying, self.pending = self.pending, []
        return self._sibling_plan(self.sib_flying) if self.sib_flying else None

    def sibling_done(self, outs):
        self.ready += self._partial_sums(self.sib_flying, outs)
        self.sib_flying = []

    def finish(self):
        plan = self.exchange_plan()
        if plan is not None:
            self.exchanged(run_plan(plan, "chip_exchange_tail"))
        return self.out


def train_step(x, target, norm_g, lb_logits, onorm_g, pool_scale, conv_w, conv_b, wts):
    t, d = x.shape
    depth = norm_g.shape[0]
    ng = lambda i, j: norm_g[i, j].reshape(1, d)
    lbs = lower_bounds_fwd(lb_logits)
    saved = []
    h = x
    _, u16, u32 = res_norm(h, None, None, ng(0, 0), "norm_in")

    def hosted(call, *args, **kw):
        plan = wts.gather_plan(kw["name"])
        out, extra = call(*args, plan=plan, **kw)
        if plan is not None:
            wts.gathered(extra)
        return out

    def project(a, kind, l, name):
        plan = wts.gather_plan(name)
        if plan is None:
            return matmul(a, wts.weight(kind, l), "nn", F32, name)
        out, extra = matmul(a, wts.weight(kind, l), "nn", F32, name, plan=plan)
        wts.gathered(extra)
        return out

    for i in range(depth):
        kind, j = i % 3, i // 3
        s = {"h_in": h, "u1": u16}
        if kind == 0:
            s["proj"] = project(u16, "w_in", j, f"hg_in_{i}")
            s["y"] = hosted(hgrn_fwd, s["proj"], lbs[i].reshape(1, -1), onorm_g[j].reshape(1, -1), i > 0,
                            name=f"hgrn_fwd_{i}")
            mix = project(s["y"], "w_out", j, f"hg_out_{i}")
        elif kind == 1:
            s["qkv"] = project(u16, "w_qkv", j, f"sb_qkv_{i}")
            s["o"] = hosted(sba_fwd, s["qkv"], name=f"sba_fwd_{i}")
            mix = project(s["o"], "w_so", j, f"sb_out_{i}")
        else:
            s["u1f"] = u32
            mix = pool_fwd(u32, wts.weight("w_pool", j), pool_scale[j].reshape(1, d), f"pool_fwd_{i}")
        s["mix"] = mix
        h_mid, u2, _ = res_norm(h, mix, ng(i, 1), ng(i, 2), f"norm_mid_{i}")
        s["h_mid"], s["u2"] = h_mid, u2
        s["up"] = project(u2, "w_up", i, f"ffn_up_{i}")
        s["act"] = conv_glu_fwd(s["up"], conv_w[i], conv_b[i].reshape(1, -1), f"glu_fwd_{i}")
        s["f"] = project(s["act"], "w_down", i, f"ffn_down_{i}")
        nxt = ng(i + 1, 0) if i + 1 < depth else None
        h, u16, u32 = res_norm(h_mid, s["f"], ng(i, 3), nxt, f"norm_out_{i}",
                               want_f32=(nxt is not None and (i + 1) % 3 == 2))
        saved.append(s)

    loss_acc, dh = loss_head(h, target, "loss_head")

    d_norm = [[None] * 4 for _ in range(depth)]
    d_lbs = jnp.zeros_like(lbs)
    d_onorm = [None] * onorm_g.shape[0]
    d_pscale = [None] * pool_scale.shape[0]
    d_cw, d_cb = [None] * depth, [None] * depth

    def exchanging(call, *args, **kw):
        plan = wts.exchange_plan(kw["name"])
        out, extra = call(*args, plan=plan, **kw)
        if plan is not None:
            wts.exchanged(extra)
        return out

    def bmm(a, b, mode, dtype, name):
        plan = wts.exchange_plan(name)
        if plan is None:
            return matmul(a, b, mode, dtype, name)
        out, extra = matmul(a, b, mode, dtype, name, plan=plan)
        wts.exchanged(extra)
        return out

    for i in reversed(range(depth)):
        kind, j = i % 3, i // 3
        s = saved[i]
        df, d_norm[i][3] = norm_bwd(s["f"], ng(i, 3), dh, None, BF16, f"nb_out_{i}")
        dact = bmm(df, wts.weight("w_down", i), "nt", F32, f"d_act_{i}")
        wts.grad("w_down", i, bmm(s["act"], df, "tn", BF16, f"dw_down_{i}"))
        sib = wts.sibling_plan()
        (dup, dcw, dcb), extra = conv_glu_bwd(s["up"], conv_w[i], conv_b[i].reshape(1, -1), dact, f"glu_bwd_{i}",
                                              plan=sib)
        if sib is not None:
            wts.sibling_done(extra)
        d_cw[i] = jnp.moveaxis(dcw, 0, 1).reshape(CONV_WIDTH, -1)
        d_cb[i] = dcb.reshape(-1)
        du2 = bmm(Split(dup), wts.weight("w_up", i), "nt", F32, f"d_u2_{i}")
        wts.grad("w_up", i, bmm(s["u2"], Split(dup), "tn", BF16, f"dw_up_{i}"))
        dh_mid, d_norm[i][2] = norm_bwd(s["h_mid"], ng(i, 2), du2, dh, F32, f"nb_mid_{i}")
        dm, d_norm[i][1] = norm_bwd(s["mix"], ng(i, 1), dh_mid, None, F32 if kind == 2 else BF16, f"nb_mix_{i}")
        if kind == 0:
            dy = matmul(dm, wts.weight("w_out", j), "nt", F32, f"d_y_{i}")
            wts.grad("w_out", j, matmul(s["y"], dm, "tn", BF16, f"dw_hgout_{i}"))
            dproj, d_onorm[j], dlb = exchanging(hgrn_bwd, s["proj"], lbs[i].reshape(1, -1),
                                                onorm_g[j].reshape(1, -1), dy, i > 0, name=f"hgrn_bwd_{i}")
            d_lbs = d_lbs.at[i].set(dlb[0])
            wts.grad("w_in", j, bmm(s["u1"], Split(dproj), "tn", BF16, f"dw_hgin_{i}"))
            du1 = bmm(Split(dproj), wts.weight("w_in", j), "nt", F32, f"d_u1_{i}")
        elif kind == 1:
            do = matmul(dm, wts.weight("w_so", j), "nt", F32, f"d_o_{i}")
            wts.grad("w_so", j, matmul(s["o"], dm, "tn", BF16, f"dw_sbout_{i}"))
            dqkv = exchanging(sba_bwd, s["qkv"], s["o"], do, name=f"sba_bwd_{i}")
            du1 = bmm(Split(dqkv), wts.weight("w_qkv", j), "nt", F32, f"d_u1_{i}")
            wts.grad("w_qkv", j, bmm(s["u1"], Split(dqkv), "tn", BF16, f"dw_sbqkv_{i}"))
        else:
            du1, g_pool, d_pscale[j] = pool_bwd(s["u1f"], wts.weight("w_pool", j), pool_scale[j].reshape(1, d),
                                                dm, f"pool_bwd_{i}")
            wts.grad("w_pool", j, g_pool)
        dh, d_norm[i][0] = norm_bwd(s["h_in"], ng(i, 0), du1, dh_mid, F32, f"nb_in_{i}")

    small = {
        "norm_g": jnp.stack([jnp.stack([v.reshape(d) for v in row]) for row in d_norm]),
        "lb_logits": lower_bounds_bwd(lb_logits, d_lbs),
        "onorm_g": jnp.stack([v.reshape(-1) for v in d_onorm]),
        "pool_scale": jnp.stack([v.reshape(-1) for v in d_pscale]),
        "conv_w": jnp.stack(d_cw),
        "conv_b": jnp.stack(d_cb),
    }
    return loss_acc, dh, small


SMALL_SHARDED = ("norm_g", "onorm_g", "pool_scale", "conv_w")
SMALL_ORDER = ("norm_g", "lb_logits", "onorm_g", "pool_scale", "conv_w", "conv_b")


def kernel(x, norm_g, hgrn_lb_logits, hgrn_w_in, hgrn_onorm_g, hgrn_w_out, sba_w_qkv, sba_w_out, pool_w, pool_scale, ffn_w_up, ffn_conv_w, ffn_conv_b, ffn_w_down, loss_target, m_norm_g, m_hgrn_lb_logits, m_hgrn_w_in, m_hgrn_onorm_g, m_hgrn_w_out, m_sba_w_qkv, m_sba_w_out, m_pool_w, m_pool_scale, m_ffn_w_up, m_ffn_conv_w, m_ffn_conv_b, m_ffn_w_down, v_norm_g, v_hgrn_lb_logits, v_hgrn_w_in, v_hgrn_onorm_g, v_hgrn_w_out, v_sba_w_qkv, v_sba_w_out, v_pool_w, v_pool_scale, v_ffn_w_up, v_ffn_conv_w, v_ffn_conv_b, v_ffn_w_down):
    cx, cy, cc = lax.axis_index("x"), lax.axis_index("y"), lax.axis_index("c")
    block = 4 * cx + 2 * cy + cc
    core = cc.astype(jnp.int32).reshape(1)

    pool3 = lambda a: a.reshape(a.shape[0] * a.shape[1], a.shape[2], a.shape[3])
    big_w = dict(zip(BIG_ORDER, [hgrn_w_in, hgrn_w_out, sba_w_qkv, sba_w_out, pool3(pool_w), ffn_w_up, ffn_w_down]))
    big_m = dict(zip(BIG_ORDER, [m_hgrn_w_in, m_hgrn_w_out, m_sba_w_qkv, m_sba_w_out, pool3(m_pool_w), m_ffn_w_up,
                                 m_ffn_w_down]))
    big_v = dict(zip(BIG_ORDER, [v_hgrn_w_in, v_hgrn_w_out, v_sba_w_qkv, v_sba_w_out, pool3(v_pool_w), v_ffn_w_up,
                                 v_ffn_w_down]))

    sharded = {"norm_g": norm_g, "onorm_g": hgrn_onorm_g, "pool_scale": pool_scale, "conv_w": ffn_conv_w}
    gathered = small_exchange(_pack([sharded[n] for n in SMALL_SHARDED]), False, "gather_small")
    parts = _unpack(gathered, [sharded[n].shape for n in SMALL_SHARDED])
    full = {n: _unshard_last(p) for n, p in zip(SMALL_SHARDED, parts)}

    wts = MeshWeights({k: w.astype(BF16) for k, w in big_w.items()}, big_w, big_m, big_v, core)
    loss_acc, grad_x, small_g = train_step(
        x[0], loss_target[0], full["norm_g"], hgrn_lb_logits, full["onorm_g"], full["pool_scale"],
        full["conv_w"], ffn_conv_b, wts)
    loss = lax.psum(loss_acc[0, 0], ("x", "y", "c"))

    shapes = [small_g[n].shape for n in SMALL_ORDER]
    summed = _unpack(small_exchange(_pack([small_g[n] for n in SMALL_ORDER]), True, "reduce_small"), shapes)
    sg = {n: (_shard_last(g, block) if n in SMALL_SHARDED else g) for n, g in zip(SMALL_ORDER, summed)}
    sw = {"norm_g": norm_g, "lb_logits": hgrn_lb_logits, "onorm_g": hgrn_onorm_g, "pool_scale": pool_scale,
          "conv_w": ffn_conv_w, "conv_b": ffn_conv_b}
    sm = {"norm_g": m_norm_g, "lb_logits": m_hgrn_lb_logits, "onorm_g": m_hgrn_onorm_g, "pool_scale": m_pool_scale,
          "conv_w": m_ffn_conv_w, "conv_b": m_ffn_conv_b}
    sv = {"norm_g": v_norm_g, "lb_logits": v_hgrn_lb_logits, "onorm_g": v_hgrn_onorm_g, "pool_scale": v_pool_scale,
          "conv_w": v_ffn_conv_w, "conv_b": v_ffn_conv_b}
    sshapes = [sw[n].shape for n in SMALL_ORDER]
    packed = [_pack([dct[n] for n in SMALL_ORDER]) for dct in (sw, sg, sm, sv)]
    s_delta, s_m, s_v = [dict(zip(SMALL_ORDER, _unpack(p, sshapes))) for p in adam_small(*packed)]

    upd = wts.finish()
    b_grad, b_delta, b_m, b_v = [[upd[k][n] for k in BIG_ORDER] for n in range(4)]

    def tree(small, bigs):
        bg = list(bigs)
        bg[4] = bg[4].reshape(pool_w.shape)
        return (small["norm_g"], small["lb_logits"], bg[0], small["onorm_g"], bg[1], bg[2], bg[3], bg[4],
                small["pool_scale"], bg[5], small["conv_w"], small["conv_b"], bg[6])

    return (loss, grad_x[None], *tree(sg, b_grad), *tree(s_delta, b_delta), *tree(s_m, b_m), *tree(s_v, b_v))
```

```python
import functools
import math

import jax
import jax.numpy as jnp
from jax import lax
from jax.experimental import pallas as pl
from jax.experimental.pallas import tpu as pltpu

F32 = jnp.float32
BF16 = jnp.bfloat16
HI = lax.Precision.HIGHEST
MESH = pl.DeviceIdType.MESH
ANY = pl.BlockSpec(memory_space=pl.ANY)
VMEM_FULL = pl.BlockSpec(memory_space=pltpu.VMEM)

NORM_EPS = 1e-6
HEAD = 128
HG_CHUNK = 128
HG_SUB = 32
HG_MAX_EXPONENT = 80.0
SB_BLOCK = 256
SB_QROWS = 256
POOL_WINDOWS = (2, 4, 8, 16)
CONV_WIDTH = 3
ROW_TILE = 256
N_DEV = 8
N_CHIP = 4

ADAM_LR = 0.001
ADAM_B1 = 0.9
ADAM_B2 = 0.999
ADAM_EPS = 1e-08
ADAM_WD = 0.01
ADAM_STEP = 10

VMEM_LIMIT = 48 * 1024 * 1024
LANES = 128
ELEMWISE_BLOCK_ELEMS = 256 * 1024


def _params(sem=None, vmem=VMEM_LIMIT):
    return pltpu.CompilerParams(dimension_semantics=sem, vmem_limit_bytes=vmem)


def _tile(n, prefs=(1024, 512, 256, 128)):
    for p in prefs:
        if n % p == 0:
            return p
    return n


def _dot(a, b, prec=None):
    return jnp.dot(a, b, precision=prec, preferred_element_type=F32)


def _dot_nt(a, b, prec=None):
    return lax.dot_general(a, b, (((1,), (1,)), ((), ())), precision=prec, preferred_element_type=F32)


def _dot_tn(a, b, prec=None):
    return lax.dot_general(a, b, (((0,), (0,)), ((), ())), precision=prec, preferred_element_type=F32)


def _split_dot(x, tri, parts, left=False):
    tot, rest = None, x
    for p in range(parts):
        h = rest.astype(BF16)
        d = _dot(tri, h) if left else _dot(h, tri)
        tot = d if tot is None else tot + d
        if p + 1 < parts:
            rest = rest - h.astype(F32)
    return tot


def _dot1(a, b, fn):
    return fn(a.astype(BF16), b.astype(BF16))


def _dot3(a, b, fn):
    ah, bh = a.astype(BF16), b.astype(BF16)
    al, bl = (a - ah.astype(F32)).astype(BF16), (b - bh.astype(F32)).astype(BF16)
    return fn(ah, bh) + fn(ah, bl) + fn(al, bh)


def _sigmoid(x):
    return jax.nn.sigmoid(x)


def _softplus(x):
    return jnp.maximum(x, 0.0) + jnp.log1p(jnp.exp(-jnp.abs(x)))


class Split:
    def __init__(self, arr):
        self.arr = arr
        self.shape = (arr.shape[1], arr.shape[0] * arr.shape[2])
        self.part = arr.shape[2]


class Plain:
    def __init__(self, arr):
        self.arr = arr
        self.shape = arr.shape
        self.part = None


def _wrap(op):
    return op if isinstance(op, (Split, Plain)) else Plain(op)


def _op_spec(op, br, bc, rc_of_grid):
    if isinstance(op, Split):
        per = op.part // bc

        def imap(i, j, k):
            r, c = rc_of_grid(i, j, k)
            return (lax.div(c, per), r, lax.rem(c, per))
        return pl.BlockSpec((None, br, bc), imap)
    return pl.BlockSpec((br, bc), rc_of_grid)


def _hosted(body, n_in, n_out, plan, step_of_grid):
    if plan is None:
        return body
    pi, po, ps = len(plan.args), len(plan.out_shape), len(plan.scratch)

    def wrapped(*refs):
        refs = list(refs)
        ins, pins = refs[:n_in], refs[n_in:n_in + pi]
        outs = refs[n_in + pi:n_in + pi + n_out]
        pouts = refs[n_in + pi + n_out:n_in + pi + n_out + po]
        scr, pscr = refs[n_in + pi + n_out + po:len(refs) - ps], refs[len(refs) - ps:]
        step, nsteps = step_of_grid()

        @pl.when(step == 0)
        def _():
            plan.start(pins, pouts, pscr)

        if plan.has_mid:
            @pl.when(step == min((2 * nsteps) // 3, nsteps - 1))
            def _():
                plan.mid(pins, pouts, pscr)

        body(*ins, *outs, *scr)

        @pl.when(step == nsteps - 1)
        def _():
            plan.finish(pins, pouts, pscr)

    return wrapped


def _host_call(body, n_in, plan, step_of_grid, *, out_shape, grid, in_specs, out_specs, scratch_shapes, name, sem, args):
    single = not isinstance(out_shape, (list, tuple))
    out_shape = [out_shape] if single else list(out_shape)
    out_specs = [out_specs] if single else list(out_specs)
    n_out = len(out_shape)
    in_specs, scratch_shapes, args = list(in_specs), list(scratch_shapes), list(args)
    if plan is not None:
        in_specs += [ANY] * len(plan.args)
        args += plan.args
        out_shape += plan.out_shape
        out_specs += [ANY] * len(plan.out_shape)
        scratch_shapes += plan.scratch
        sem = ("arbitrary",) * len(grid)
    outs = pl.pallas_call(
        _hosted(body, n_in, n_out, plan, step_of_grid), out_shape=out_shape, grid=grid,
        in_specs=in_specs, out_specs=out_specs, scratch_shapes=scratch_shapes,
        name=name, compiler_params=_params(sem),
    )(*args)
    host = outs[0] if single else list(outs[:n_out])
    return host, list(outs[n_out:])


def matmul(a, b, mode, out_dtype, name, plan=None):
    a, b = _wrap(a), _wrap(b)
    if mode == "nn":
        (m, kd), (kd2, n) = a.shape, b.shape
    elif mode == "nt":
        (m, kd), (n, kd2) = a.shape, b.shape
    else:
        (kd, m), (kd2, n) = a.shape, b.shape
    assert kd == kd2, (mode, a.shape, b.shape)

    def dim_tile(full, ops_on_cols, prefs=(1024, 512, 256, 128)):
        base = full
        for op in ops_on_cols:
            if op.part is not None:
                base = math.gcd(base, op.part)
        return _tile(base, prefs)

    tm = dim_tile(m, [a] if mode == "tn" else [])
    tn = dim_tile(n, [b] if mode in ("nn", "tn") else [])
    tk = dim_tile(kd, ([a] if mode in ("nn", "nt") else []) + ([b] if mode == "nt" else []),
                  prefs=(2048, 2816, 1024, 512, 256, 128))
    nk = kd // tk

    if mode == "nn":
        a_spec = _op_spec(a, tm, tk, lambda i, j, k: (i, k))
        b_spec = _op_spec(b, tk, tn, lambda i, j, k: (k, j))
        dot = _dot
    elif mode == "nt":
        a_spec = _op_spec(a, tm, tk, lambda i, j, k: (i, k))
        b_spec = _op_spec(b, tn, tk, lambda i, j, k: (j, k))
        dot = _dot_nt
    else:
        a_spec = _op_spec(a, tk, tm, lambda i, j, k: (k, i))
        b_spec = _op_spec(b, tk, tn, lambda i, j, k: (k, j))
        dot = _dot_tn

    def body(a_ref, b_ref, o_ref, *acc):
        part = dot(a_ref[...].astype(BF16), b_ref[...].astype(BF16))
        if nk == 1:
            o_ref[...] = part.astype(o_ref.dtype)
            return
        (acc_ref,) = acc
        k = pl.program_id(2)

        @pl.when(k == 0)
        def _():
            acc_ref[...] = part

        @pl.when(k > 0)
        def _():
            acc_ref[...] += part

        @pl.when(k == nk - 1)
        def _():
            o_ref[...] = acc_ref[...].astype(o_ref.dtype)

    gi, gj = m // tm, n // tn

    def step_of_grid():
        return (pl.program_id(0) * gj + pl.program_id(1)) * nk + pl.program_id(2), gi * gj * nk

    out, extra = _host_call(
        body, 2, plan, step_of_grid, out_shape=jax.ShapeDtypeStruct((m, n), out_dtype), grid=(gi, gj, nk),
        in_specs=[a_spec, b_spec], out_specs=pl.BlockSpec((tm, tn), lambda i, j, k: (i, j)),
        scratch_shapes=[pltpu.VMEM((tm, tn), F32)] if nk > 1 else [], name=name,
        sem=("parallel", "parallel", "arbitrary"), args=[a.arr, b.arr])
    return out if plan is None else (out, extra)


def _rms(x, g):
    r = lax.rsqrt(jnp.mean(x * x, axis=-1, keepdims=True) + NORM_EPS)
    return x * r * g


def res_norm(h, m, g_a, g_b, name, want_f32=False):
    t, d = h.shape
    tr = min(ROW_TILE, t)
    has_m, has_u = m is not None, g_b is not None
    row = pl.BlockSpec((tr, d), lambda i: (i, 0))
    vec = pl.BlockSpec((1, d), lambda i: (0, 0))

    def body(*refs):
        refs = list(refs)
        h_ref = refs.pop(0)
        hn = h_ref[...]
        if has_m:
            m_ref, ga_ref = refs.pop(0), refs.pop(0)
            hn = hn + _rms(m_ref[...], ga_ref[...])
        if has_u:
            gb_ref = refs.pop(0)
        if has_m:
            refs.pop(0)[...] = hn
        if has_u:
            u = _rms(hn, gb_ref[...])
            refs.pop(0)[...] = u.astype(BF16)
            if want_f32:
                refs.pop(0)[...] = u

    args, in_specs, out_shape, out_specs = [h], [row], [], []
    if has_m:
        args += [m, g_a]
        in_specs += [row, vec]
        out_shape.append(jax.ShapeDtypeStruct((t, d), F32))
        out_specs.append(row)
    if has_u:
        args.append(g_b)
        in_specs.append(vec)
        out_shape.append(jax.ShapeDtypeStruct((t, d), BF16))
        out_specs.append(row)
        if want_f32:
            out_shape.append(jax.ShapeDtypeStruct((t, d), F32))
            out_specs.append(row)
    outs = list(pl.pallas_call(
        body, out_shape=out_shape, grid=(t // tr,), in_specs=in_specs, out_specs=out_specs,
        name=name, compiler_params=_params(("parallel",)),
    )(*args))
    h_new = outs.pop(0) if has_m else None
    u16 = outs.pop(0) if has_u else None
    u32 = outs.pop(0) if (has_u and want_f32) else None
    return h_new, u16, u32


def norm_bwd(x, g, dy, add, out_dtype, name):
    t, d = x.shape
    tr = min(ROW_TILE, t)
    has_add = add is not None
    row = pl.BlockSpec((tr, d), lambda i: (i, 0))
    vec = pl.BlockSpec((1, d), lambda i: (0, 0))

    def body(*refs):
        if has_add:
            x_ref, g_ref, dy_ref, add_ref, dx_ref, dg_ref = refs
        else:
            x_ref, g_ref, dy_ref, dx_ref, dg_ref = refs
        xv = x_ref[...]
        dyv = dy_ref[...].astype(F32)
        r = lax.rsqrt(jnp.mean(xv * xv, axis=-1, keepdims=True) + NORM_EPS)
        gy = dyv * g_ref[...]
        dx = r * gy - xv * (r * r * r * jnp.mean(gy * xv, axis=-1, keepdims=True))
        if has_add:
            dx = dx + add_ref[...]
        dx_ref[...] = dx.astype(dx_ref.dtype)

        @pl.when(pl.program_id(0) == 0)
        def _():
            dg_ref[...] = jnp.zeros_like(dg_ref)

        dg_ref[...] += jnp.sum(dyv * xv * r, axis=0, keepdims=True)

    args = [x, g, dy] + ([add] if has_add else [])
    in_specs = [row, vec, row] + ([row] if has_add else [])
    return pl.pallas_call(
        body, out_shape=[jax.ShapeDtypeStruct((t, d), out_dtype), jax.ShapeDtypeStruct((1, d), F32)],
        grid=(t // tr,), in_specs=in_specs, out_specs=[row, vec],
        name=name, compiler_params=_params(("arbitrary",)),
    )(*args)


def loss_head(y, target, name):
    t, d = y.shape
    tr = min(ROW_TILE, t)
    row = pl.BlockSpec((tr, d), lambda i: (i, 0))
    acc = pl.BlockSpec((8, LANES), lambda i: (0, 0))

    def body(y_ref, t_ref, loss_ref, dy_ref):
        e = y_ref[...] - t_ref[...]
        dy_ref[...] = e * (1.0 / d)

        @pl.when(pl.program_id(0) == 0)
        def _():
            loss_ref[...] = jnp.zeros_like(loss_ref)

        loss_ref[...] += jnp.sum(e * e) * (0.5 / d)

    return pl.pallas_call(
        body, out_shape=[jax.ShapeDtypeStruct((8, LANES), F32), jax.ShapeDtypeStruct((t, d), F32)],
        grid=(t // tr,), in_specs=[row, row], out_specs=[acc, row],
        name=name, compiler_params=_params(("arbitrary",)),
    )(y, target)


def _depth_softmax(ref, depth):
    rows = [ref[i:i + 1, :] for i in range(depth)]
    mx = functools.reduce(jnp.maximum, rows)
    ex = [jnp.exp(r - mx) for r in rows]
    tot = functools.reduce(lambda p, q: p + q, ex)
    return [e / tot for e in ex]


def lower_bounds_fwd(logits):
    depth, kw = logits.shape

    def body(l_ref, o_ref):
        s = _depth_softmax(l_ref, depth)
        run = jnp.zeros_like(s[0])
        o_ref[0:1, :] = run
        for i in range(1, depth):
            run = run + s[i]
            o_ref[i:i + 1, :] = run

    return pl.pallas_call(body, out_shape=jax.ShapeDtypeStruct((depth, kw), F32), name="lb_fwd")(logits)


def lower_bounds_bwd(logits, dlb):
    depth, kw = logits.shape

    def body(l_ref, d_ref, o_ref):
        s = _depth_softmax(l_ref, depth)
        ds = [jnp.zeros_like(s[0]) for _ in range(depth)]
        run = jnp.zeros_like(s[0])
        for j in range(depth - 1, 0, -1):
            run = run + d_ref[j:j + 1, :]
            ds[j] = run
        dot = functools.reduce(lambda p, q: p + q, [s[j] * ds[j] for j in range(depth)])
        for j in range(depth):
            o_ref[j:j + 1, :] = s[j] * (ds[j] - dot)

    return pl.pallas_call(body, out_shape=jax.ShapeDtypeStruct((depth, kw), F32), name="lb_bwd")(logits, dlb)


def _hg_gates(qp, fp, lb_row, has_lb):
    sig = _sigmoid(fp)
    nsig = _sigmoid(-fp)
    ls = jnp.minimum(fp, 0.0) - jnp.log1p(jnp.exp(-jnp.abs(fp)))
    if has_lb:
        a = jnp.log(lb_row)
        bb = jnp.log1p(-lb_row) + ls
        g = jnp.maximum(a, bb) + jnp.log1p(jnp.exp(-jnp.abs(a - bb)))
        w = jnp.exp(bb - g)
        k = (1.0 - lb_row) * nsig
    else:
        g, w, k = ls, None, nsig
    q = qp * _sigmoid(qp)
    return q, k, g, sig, nsig, w


HG_HEADS_PER_STEP = 2


def _heads_per_step(nh):
    return HG_HEADS_PER_STEP if nh % HG_HEADS_PER_STEP == 0 else 1


def _head_views(refs, hh, lanes, lead):
    cs = pl.ds(hh * HEAD, HEAD)
    out = []
    for i, r in enumerate(refs):
        if i in lead:
            out.append(r.at[hh])
        elif i in lanes:
            out.append(r.at[(slice(None),) * (len(r.shape) - 1) + (cs,)])
        else:
            out.append(r)
    return out


def _hg_masks():
    c = HG_CHUNK
    row = lax.broadcasted_iota(jnp.int32, (c, c), 0)
    col = lax.broadcasted_iota(jnp.int32, (c, c), 1)
    lower = (col <= row).astype(BF16)
    upper = (col >= row).astype(BF16)
    krow = lax.broadcasted_iota(jnp.int32, (c, HEAD), 0)
    arow = lax.broadcasted_iota(jnp.int32, (HG_SUB, c), 0)
    acol = lax.broadcasted_iota(jnp.int32, (HG_SUB, c), 1)
    return lower, upper, krow, arow, acol


def _hg_sub(i, q, k, b, b_ref, krow, arow, acol):
    r0 = i * HG_SUB
    m = b_ref[r0:r0 + 1, :]
    ebq = jnp.exp(b[r0:r0 + HG_SUB, :] - m)
    qh = (q[r0:r0 + HG_SUB, :] * ebq).astype(BF16)
    ek = jnp.exp(jnp.where(krow < r0 + HG_SUB, jnp.minimum(m - b, HG_MAX_EXPONENT), 0.0))
    kh = (k * ek).astype(BF16)
    mask = acol <= arow + r0
    amat = jnp.where(mask, _dot_nt(qh, kh), 0.0).astype(BF16)
    return ebq, qh, ek, kh, mask, amat


def hgrn_fwd(proj, lb, onorm_g, has_lb, name, plan=None):
    t, w4 = proj.shape
    kw = w4 // 4
    nh, nc, c = kw // HEAD, t // HG_CHUNK, HG_CHUNK
    hpb = _heads_per_step(nh)
    wide = hpb * HEAD

    def body(*refs):
        refs[7][...] = jnp.zeros_like(refs[7])
        heads = [one_head(*_head_views(refs, hh, lanes=(0, 1, 2, 3, 4, 5, 6), lead=(7, 8))) for hh in range(hpb)]

        def chunks(ci, carry):
            for chunk in heads:
                chunk(ci)
            return carry

        lax.fori_loop(0, nc, chunks, 0)

    def one_head(qp_ref, fp_ref, iv_ref, gp_ref, lb_ref, on_ref, y_ref, st_ref, b_ref):
        lower, _, krow, arow, acol = _hg_masks()
        lb_row, gam = lb_ref[...], on_ref[...]

        def chunk(ci):
            rs = pl.ds(pl.multiple_of(ci * c, c), c)
            v = iv_ref[rs, :].astype(BF16)
            gp = gp_ref[rs, :]
            q, k, g, _, _, _ = _hg_gates(qp_ref[rs, :], fp_ref[rs, :], lb_row, has_lb)
            b = _split_dot(g, lower, 3, left=True)
            b_ref[...] = b
            st = st_ref[...]
            o = _dot_nt((q * jnp.exp(b)).astype(BF16), st.astype(BF16))
            amats = [_hg_sub(i, q, k, b, b_ref, krow, arow, acol)[-1] for i in range(c // HG_SUB)]
            o = o + _dot(jnp.concatenate(amats, axis=0), v)
            bl = b_ref[c - 1:c, :]
            st_ref[...] = jnp.exp(bl) * st + _dot_tn(v, (k * jnp.exp(bl - b)).astype(BF16))
            r = lax.rsqrt(jnp.mean(o * o, axis=-1, keepdims=True) + NORM_EPS)
            y_ref[rs, :] = (o * r * gam * (gp * _sigmoid(gp))).astype(BF16)

        return chunk

    ns = nh // hpb
    col = lambda p: pl.BlockSpec((t, wide), lambda h: (0, p * ns + h))
    vec = pl.BlockSpec((1, wide), lambda h: (0, h))
    return _host_call(
        body, 6, plan, lambda: (pl.program_id(0), ns),
        out_shape=jax.ShapeDtypeStruct((t, kw), BF16), grid=(ns,),
        in_specs=[col(0), col(1), col(2), col(3), vec, vec],
        out_specs=pl.BlockSpec((t, wide), lambda h: (0, h)),
        scratch_shapes=[pltpu.VMEM((hpb, HEAD, HEAD), F32), pltpu.VMEM((hpb, c, HEAD), F32)],
        name=name, sem=("parallel",), args=[proj, proj, proj, proj, lb, onorm_g])


def hgrn_bwd(proj, lb, onorm_g, dy, has_lb, name, plan=None):
    t, w4 = proj.shape
    kw = w4 // 4
    nh, nc, c = kw // HEAD, t // HG_CHUNK, HG_CHUNK
    hpb = _heads_per_step(nh)
    wide = hpb * HEAD

    def body(*refs):
        for i in (8, 9, 13, 14):
            refs[i][...] = jnp.zeros_like(refs[i])
        heads = [one_head(*_head_views(refs, hh, lanes=(0, 1, 2, 3, 4, 5, 6, 7, 8, 9, 11), lead=(10, 12, 13, 14)))
                 for hh in range(hpb)]

        def fwd_chunks(ci, states):
            return tuple(fwd(ci, st) for (fwd, _), st in zip(heads, states))

        lax.fori_loop(0, nc, fwd_chunks, tuple(jnp.zeros((HEAD, HEAD), F32) for _ in heads))

        def bwd_chunks(step, carry):
            for _, bwd in heads:
                bwd(nc - 1 - step)
            return carry

        lax.fori_loop(0, nc, bwd_chunks, 0)

    def one_head(qp_ref, fp_ref, iv_ref, gp_ref, lb_ref, on_ref, dy_ref,
                 dp_ref, dgam_ref, dlb_ref, sst_ref, o_ref, b_ref, dst_ref, car_ref):
        lower, upper, krow, arow, acol = _hg_masks()
        lb_row, gam = lb_ref[...], on_ref[...]
        mm = _dot3 if has_lb else _dot1

        def recompute(ci):
            rs = pl.ds(pl.multiple_of(ci * c, c), c)
            gates = _hg_gates(qp_ref[rs, :], fp_ref[rs, :], lb_row, has_lb)
            b = _split_dot(gates[2], lower, 3, left=True)
            b_ref[...] = b
            return rs, gates, b

        def fwd_chunk(ci, carry):
            rs, (q, k, g, _, _, _), b = recompute(ci)
            v = iv_ref[rs, :].astype(BF16)
            st = carry
            sst_ref[ci] = st
            o = _dot_nt((q * jnp.exp(b)).astype(BF16), st.astype(BF16))
            amats = [_hg_sub(i, q, k, b, b_ref, krow, arow, acol)[-1] for i in range(c // HG_SUB)]
            o_ref[rs, :] = o + _dot(jnp.concatenate(amats, axis=0), v)
            bl = b_ref[c - 1:c, :]
            return jnp.exp(bl) * st + mm(iv_ref[rs, :], k * jnp.exp(bl - b), _dot_tn)

        def bwd_chunk(ci):
            rs, (q, k, g, sig, nsig, w), b = recompute(ci)
            qp, gp, v32 = qp_ref[rs, :], gp_ref[rs, :], iv_ref[rs, :]
            v = v32.astype(BF16)
            dyv, o = dy_ref[rs, :], o_ref[rs, :]
            st = sst_ref[ci]
            dst_new = dst_ref[...]
            r = lax.rsqrt(jnp.mean(o * o, axis=-1, keepdims=True) + NORM_EPS)
            on = o * r
            sgm = _sigmoid(gp)
            sg = gp * sgm
            dgam_ref[...] += jnp.sum(dyv * sg * on, axis=0, keepdims=True)
            dgp = dyv * on * gam * (sgm * (1.0 + gp * (1.0 - sgm)))
            dn = dyv * gam * sg
            do = r * dn - o * (r * r * r * jnp.mean(dn * o, axis=-1, keepdims=True))
            eb = jnp.exp(b)
            bl = b_ref[c - 1:c, :]
            ekd = jnp.exp(bl - b)
            do16 = do.astype(BF16)
            dq = mm(do, st, _dot) * eb
            dv = _dot1(k * ekd, dst_new, _dot_nt)
            dk = mm(v32, dst_new, _dot) * ekd
            dst_ref[...] = jnp.exp(bl) * dst_new + mm(do, q * eb, _dot_tn)
            da_all = mm(do, v32, _dot_nt)
            dq_parts, amats = [], []
            for i in range(c // HG_SUB):
                r0 = i * HG_SUB
                ebq, _, ek, _, mask, amat = _hg_sub(i, q, k, b, b_ref, krow, arow, acol)
                da = jnp.where(mask, da_all[r0:r0 + HG_SUB, :], 0.0)
                dq_parts.append(mm(da, k * ek, _dot) * ebq)
                dk = dk + mm(da, q[r0:r0 + HG_SUB, :] * ebq, _dot_tn) * ek
                amats.append(amat)
            dq = dq + jnp.concatenate(dq_parts, axis=0)
            dv = dv + _dot_tn(jnp.concatenate(amats, axis=0), do16)
            db = q * dq - k * dk
            dg = car_ref[...] + _split_dot(db, upper, 3, left=True)
            car_ref[...] += jnp.sum(db, axis=0, keepdims=True)
            if has_lb:
                dfp = dg * nsig * w - dk * ((1.0 - lb_row) * sig * nsig)
                dlb_ref[...] += jnp.sum(dg * nsig * jnp.exp(-g) - dk * nsig, axis=0, keepdims=True)
            else:
                dfp = dg * nsig - dk * (sig * nsig)
            sq = _sigmoid(qp)
            dp_ref[0, rs, :] = (dq * (sq * (1.0 + qp * (1.0 - sq)))).astype(BF16)
            dp_ref[1, rs, :] = dfp.astype(BF16)
            dp_ref[2, rs, :] = dv.astype(BF16)
            dp_ref[3, rs, :] = dgp.astype(BF16)

        return fwd_chunk, bwd_chunk

    ns = nh // hpb
    col = lambda p: pl.BlockSpec((t, wide), lambda h: (0, p * ns + h))
    vec = pl.BlockSpec((1, wide), lambda h: (0, h))
    return _host_call(
        body, 7, plan, lambda: (pl.program_id(0), ns),
        out_shape=[jax.ShapeDtypeStruct((4, t, kw), BF16), jax.ShapeDtypeStruct((1, kw), F32),
                   jax.ShapeDtypeStruct((1, kw), F32)],
        grid=(ns,),
        in_specs=[col(0), col(1), col(2), col(3), vec, vec, pl.BlockSpec((t, wide), lambda h: (0, h))],
        out_specs=[pl.BlockSpec((4, t, wide), lambda h: (0, 0, h)), vec, vec],
        scratch_shapes=[pltpu.VMEM((hpb, nc, HEAD, HEAD), F32), pltpu.VMEM((t, wide), F32),
                        pltpu.VMEM((hpb, c, HEAD), F32), pltpu.VMEM((hpb, HEAD, HEAD), F32),
                        pltpu.VMEM((hpb, 1, HEAD), F32)],
        name=name, sem=("parallel",), args=[proj, proj, proj, proj, lb, onorm_g, dy])


def _sb_masks():
    m, n = SB_QROWS, SB_BLOCK
    row = lax.broadcasted_iota(jnp.int32, (m, n), 0)
    col = lax.broadcasted_iota(jnp.int32, (m, n), 1)
    r2 = lax.broadcasted_iota(jnp.int32, (n, n), 0)
    c2 = lax.broadcasted_iota(jnp.int32, (n, n), 1)
    after = (r2 > c2).astype(BF16)
    from_ = (r2 >= c2).astype(BF16)
    return row, col, after, from_


def sba_fwd(qkv, name, plan=None):
    t, w3 = qkv.shape
    wd = w3 // 3
    m, n = SB_QROWS, SB_BLOCK
    nh, nq, per = wd // HEAD, t // m, m // n
    scale = HEAD ** -0.5

    def body(q_ref, k_ref, v_ref, o_ref, q16, k16, v16):
        row, col, after, _ = _sb_masks()
        q16[...] = (q_ref[...] * scale).astype(BF16)
        k16[...] = k_ref[...].astype(BF16)
        v16[...] = v_ref[...].astype(BF16)

        def qblock(qi, carry):
            qs = pl.ds(pl.multiple_of(qi * m, m), m)
            q = q16[qs, :]
            last = (qi + 1) * per - 1

            def kblock(step, state, masked):
                acc, rem0 = state
                kj = last - step
                ks = pl.ds(pl.multiple_of(kj * n, n), n)
                z = _dot_nt(q, k16[ks, :])
                spz = _softplus(z)
                strict = (col + kj * n) < (row + qi * m)
                sp = jnp.where(strict, spz, 0.0) if masked else spz
                rem = rem0 + _split_dot(sp, after, 2)
                a = jnp.exp(z - spz - rem)
                if masked:
                    a = jnp.where(strict, a, 0.0)
                acc = acc + _dot(a.astype(BF16), v16[ks, :])
                return acc, rem0 + jnp.sum(sp, axis=1, keepdims=True)

            state = (jnp.zeros((m, HEAD), F32), jnp.zeros((m, 1), F32))
            for step in range(per):
                state = kblock(step, state, True)
            acc, _ = lax.fori_loop(per, last + 1, functools.partial(kblock, masked=False), state)
            o_ref[qs, :] = acc
            return carry

        lax.fori_loop(0, nq, qblock, 0)

    col_spec = lambda p: pl.BlockSpec((t, HEAD), lambda h: (0, p * nh + h))
    return _host_call(
        body, 3, plan, lambda: (pl.program_id(0), nh),
        out_shape=jax.ShapeDtypeStruct((t, wd), F32), grid=(nh,),
        in_specs=[col_spec(0), col_spec(1), col_spec(2)],
        out_specs=pl.BlockSpec((t, HEAD), lambda h: (0, h)),
        scratch_shapes=[pltpu.VMEM((t, HEAD), BF16)] * 3,
        name=name, sem=("parallel",), args=[qkv, qkv, qkv])


def sba_bwd(qkv, o, do, name, plan=None):
    t, w3 = qkv.shape
    wd = w3 // 3
    m, n = SB_QROWS, SB_BLOCK
    nh, nq, per = wd // HEAD, t // m, m // n
    scale = HEAD ** -0.5

    def body(q_ref, k_ref, v_ref, o_ref, do_ref, d_ref, dk_ref, dv_ref, q16, k16, v16):
        row, col, after, from_ = _sb_masks()
        dk_ref[...] = jnp.zeros_like(dk_ref)
        dv_ref[...] = jnp.zeros_like(dv_ref)
        q16[...] = (q_ref[...] * scale).astype(BF16)
        k16[...] = k_ref[...].astype(BF16)
        v16[...] = v_ref[...].astype(BF16)

        def qblock(qi, carry):
            qs = pl.ds(pl.multiple_of(qi * m, m), m)
            q = q16[qs, :]
            dov = do_ref[qs, :]
            do16 = dov.astype(BF16)
            dsum = jnp.sum(do16.astype(F32) * o_ref[qs, :], axis=1, keepdims=True)
            last = (qi + 1) * per - 1

            def kblock(step, state, masked):
                dq, rem0, e0 = state
                kj = last - step
                ks = pl.ds(pl.multiple_of(kj * n, n), n)
                kv, vv = k16[ks, :], v16[ks, :]
                z = _dot_nt(q, kv)
                strict = (col + kj * n) < (row + qi * m)
                spz = _softplus(z)
                sp = jnp.where(strict, spz, 0.0) if masked else spz
                rem = rem0 + _split_dot(sp, after, 2)
                sgz = jnp.exp(z - spz)
                a = sgz * jnp.exp(-rem)
                if masked:
                    a = jnp.where(strict, a, 0.0)
                a = a.astype(BF16)
                e = a.astype(F32) * _dot_nt(do16, vv)
                left = dsum - (e0 + _split_dot(e, from_, 2))
                dz = e * (1.0 - sgz) - sgz * left
                if masked:
                    dz = jnp.where(strict, dz, 0.0)
                dz = dz.astype(BF16)
                dq = dq + _dot(dz, kv)
                dk_ref[ks, :] += _dot_tn(dz, q)
                dv_ref[ks, :] += _dot_tn(a, do16)
                return (dq, rem0 + jnp.sum(sp, axis=1, keepdims=True),
                        e0 + jnp.sum(e, axis=1, keepdims=True))

            zero1 = jnp.zeros((m, 1), F32)
            state = (jnp.zeros((m, HEAD), F32), zero1, zero1)
            for step in range(per):
                state = kblock(step, state, True)
            dq, _, _ = lax.fori_loop(per, last + 1, functools.partial(kblock, masked=False), state)
            d_ref[0, qs, :] = (dq * scale).astype(BF16)
            return carry

        lax.fori_loop(0, nq, qblock, 0)
        d_ref[1, :, :] = dk_ref[...].astype(BF16)
        d_ref[2, :, :] = dv_ref[...].astype(BF16)

    col_spec = lambda p: pl.BlockSpec((t, HEAD), lambda h: (0, p * nh + h))
    head = pl.BlockSpec((t, HEAD), lambda h: (0, h))
    return _host_call(
        body, 5, plan, lambda: (pl.program_id(0), nh),
        out_shape=jax.ShapeDtypeStruct((3, t, wd), BF16), grid=(nh,),
        in_specs=[col_spec(0), col_spec(1), col_spec(2), head, head],
        out_specs=pl.BlockSpec((3, t, HEAD), lambda h: (0, 0, h)),
        scratch_shapes=[pltpu.VMEM((t, HEAD), F32)] * 2 + [pltpu.VMEM((t, HEAD), BF16)] * 3,
        name=name, sem=("parallel",), args=[qkv, qkv, qkv, o, do])


def _pool_band(i_out, i_in, tr, win, transpose):
    r = lax.broadcasted_iota(jnp.int32, (tr, tr), 0) + i_out * tr
    c = lax.broadcasted_iota(jnp.int32, (tr, tr), 1) + i_in * tr
    if transpose:
        return ((r <= c) & (r > c - win)).astype(F32)
    return ((c <= r) & (c > r - win)).astype(F32)


def _pool_p(u_ref, i, tr, win):
    cur = u_ref[i * tr:(i + 1) * tr, :]
    ws = _dot(_pool_band(i, i, tr, win, False), cur, HI)
    if i > 0:
        ws = ws + _dot(_pool_band(i, i - 1, tr, win, False), u_ref[(i - 1) * tr:i * tr, :], HI)
    pos = lax.broadcasted_iota(jnp.int32, (tr, 1), 0) + (i * tr + 1)
    count = jnp.minimum(pos, win).astype(F32)
    return ws / count - cur, count


def pool_fwd(u, pool_w, pool_scale, name):
    t, d = u.shape
    ng = len(POOL_WINDOWS)
    gs = d // ng
    tr = min(ROW_TILE, t)

    def body(u_ref, w_ref, s_ref, y_ref):
        win = jnp.left_shift(2, pl.program_id(0))
        for i in range(t // tr):
            p, _ = _pool_p(u_ref, i, tr, win)
            y_ref[i * tr:(i + 1) * tr, :] = _dot(p.astype(BF16), w_ref[...]) * s_ref[...]

    grp = pl.BlockSpec((t, gs), lambda g: (0, g))
    return pl.pallas_call(
        body, out_shape=jax.ShapeDtypeStruct((t, d), F32), grid=(ng,),
        in_specs=[grp, pl.BlockSpec((None, gs, gs), lambda g: (g, 0, 0)), pl.BlockSpec((1, gs), lambda g: (0, g))],
        out_specs=grp, name=name, compiler_params=_params(("parallel",)),
    )(u, pool_w, pool_scale)


def pool_bwd(u, pool_w, pool_scale, dy, name):
    t, d = u.shape
    ng = len(POOL_WINDOWS)
    gs = d // ng
    tr = min(ROW_TILE, t)
    nt = t // tr

    def body(u_ref, w_ref, s_ref, dy_ref, du_ref, dw_ref, ds_ref, dpc_ref, dp_ref):
        win = jnp.left_shift(2, pl.program_id(0))
        wv = w_ref[...]
        dw = jnp.zeros((gs, gs), F32)
        dsc = jnp.zeros((1, gs), F32)
        for i in range(nt):
            rows = slice(i * tr, (i + 1) * tr)
            p, count = _pool_p(u_ref, i, tr, win)
            p16 = p.astype(BF16)
            dyv = dy_ref[rows, :]
            dsc = dsc + jnp.sum(dyv * _dot(p16, wv), axis=0, keepdims=True)
            dyp = (dyv * s_ref[...]).astype(BF16)
            dw = dw + _dot_tn(p16, dyp)
            dp = _dot_nt(dyp, wv)
            dp_ref[rows, :] = dp
            dpc_ref[rows, :] = dp / count
        dw_ref[...] = dw.astype(BF16)
        ds_ref[...] = dsc
        for i in range(nt):
            rows = slice(i * tr, (i + 1) * tr)
            acc = _dot(_pool_band(i, i, tr, win, True), dpc_ref[rows, :], HI)
            if i + 1 < nt:
                acc = acc + _dot(_pool_band(i, i + 1, tr, win, True), dpc_ref[(i + 1) * tr:(i + 2) * tr, :], HI)
            du_ref[rows, :] = acc - dp_ref[rows, :]

    grp = pl.BlockSpec((t, gs), lambda g: (0, g))
    wspec = pl.BlockSpec((None, gs, gs), lambda g: (g, 0, 0))
    vec = pl.BlockSpec((1, gs), lambda g: (0, g))
    return pl.pallas_call(
        body,
        out_shape=[jax.ShapeDtypeStruct((t, d), F32), jax.ShapeDtypeStruct((ng, gs, gs), BF16),
                   jax.ShapeDtypeStruct((1, d), F32)],
        grid=(ng,), in_specs=[grp, wspec, vec, grp], out_specs=[grp, wspec, vec],
        scratch_shapes=[pltpu.VMEM((t, gs), F32), pltpu.VMEM((t, gs), F32)],
        name=name, compiler_params=_params(("parallel",)),
    )(u, pool_w, pool_scale, dy)


CONV_COLS = 256
HALO = 8


def _conv_taps(ref, r0, tr):
    x = ref[r0:r0 + tr, :]
    prev = ref[r0 - HALO:r0, :] if r0 > 0 else jnp.zeros((HALO, x.shape[1]), F32)
    xx = jnp.concatenate([prev, x], axis=0)
    return x, pltpu.roll(xx, 1, 0)[HALO:, :], pltpu.roll(xx, 2, 0)[HALO:, :]


def _conv_out(taps, w_ref, b_ref):
    x, s1, s2 = taps
    return w_ref[0:1, :] * s2 + w_ref[1:2, :] * s1 + w_ref[2:3, :] * x + b_ref[...]


def conv_glu_fwd(up, conv_w, conv_b, name, plan=None):
    t, f2 = up.shape
    f = f2 // 2
    tc = min(CONV_COLS, f)
    nj = f // tc
    tr = min(ROW_TILE, t)

    def body(ug_ref, uv_ref, wg_ref, wv_ref, bg_ref, bv_ref, o_ref):
        for i in range(t // tr):
            r0 = i * tr
            gate = _conv_out(_conv_taps(ug_ref, r0, tr), wg_ref, bg_ref)
            val = _conv_out(_conv_taps(uv_ref, r0, tr), wv_ref, bv_ref)
            o_ref[r0:r0 + tr, :] = (gate * _sigmoid(gate) * val).astype(BF16)

    blk = lambda rows, half: pl.BlockSpec((rows, tc), lambda j: (0, half * nj + j))
    return _host_call(
        body, 6, plan, lambda: (pl.program_id(0), nj),
        out_shape=jax.ShapeDtypeStruct((t, f), BF16), grid=(nj,),
        in_specs=[blk(t, 0), blk(t, 1), blk(CONV_WIDTH, 0), blk(CONV_WIDTH, 1), blk(1, 0), blk(1, 1)],
        out_specs=pl.BlockSpec((t, tc), lambda j: (0, j)), scratch_shapes=[],
        name=name, sem=("parallel",), args=[up, up, conv_w, conv_w, conv_b, conv_b])


def conv_glu_bwd(up, conv_w, conv_b, dact, name, plan=None):
    t, f2 = up.shape
    f = f2 // 2
    tc = min(CONV_COLS, f)
    nj = f // tc
    tr = min(ROW_TILE, t)
    nt = t // tr

    def body(ug_ref, uv_ref, wg_ref, wv_ref, bg_ref, bv_ref, da_ref, du_ref, dw_ref, db_ref, dg_ref, dv_ref):
        dwg = [jnp.zeros((1, tc), F32) for _ in range(CONV_WIDTH)]
        dwv = [jnp.zeros((1, tc), F32) for _ in range(CONV_WIDTH)]
        dbg = jnp.zeros((1, tc), F32)
        dbv = jnp.zeros((1, tc), F32)
        for i in range(nt):
            r0 = i * tr
            tg, tv = _conv_taps(ug_ref, r0, tr), _conv_taps(uv_ref, r0, tr)
            gate, val = _conv_out(tg, wg_ref, bg_ref), _conv_out(tv, wv_ref, bv_ref)
            sg = _sigmoid(gate)
            da = da_ref[r0:r0 + tr, :]
            d_gate = da * val * (sg * (1.0 + gate * (1.0 - sg)))
            d_val = da * (gate * sg)
            dg_ref[r0:r0 + tr, :] = d_gate
            dv_ref[r0:r0 + tr, :] = d_val
            dbg = dbg + jnp.sum(d_gate, axis=0, keepdims=True)
            dbv = dbv + jnp.sum(d_val, axis=0, keepdims=True)
            for tap in range(CONV_WIDTH):
                dwg[tap] = dwg[tap] + jnp.sum(d_gate * tg[2 - tap], axis=0, keepdims=True)
                dwv[tap] = dwv[tap] + jnp.sum(d_val * tv[2 - tap], axis=0, keepdims=True)
        for tap in range(CONV_WIDTH):
            dw_ref[0, tap:tap + 1, :] = dwg[tap]
            dw_ref[1, tap:tap + 1, :] = dwv[tap]
        db_ref[0, :, :] = dbg
        db_ref[1, :, :] = dbv
        for half, (d_ref, w_ref) in enumerate(((dg_ref, wg_ref), (dv_ref, wv_ref))):
            for i in range(nt):
                r0 = i * tr
                x = d_ref[r0:r0 + tr, :]
                nxt = d_ref[r0 + tr:r0 + tr + HALO, :] if i + 1 < nt else jnp.zeros((HALO, tc), F32)
                xx = jnp.concatenate([x, nxt], axis=0)
                up1 = pltpu.roll(xx, tr + HALO - 1, 0)[:tr, :]
                up2 = pltpu.roll(xx, tr + HALO - 2, 0)[:tr, :]
                du = w_ref[2:3, :] * x + w_ref[1:2, :] * up1 + w_ref[0:1, :] * up2
                du_ref[half, r0:r0 + tr, :] = du.astype(BF16)

    blk = lambda rows, half: pl.BlockSpec((rows, tc), lambda j: (0, half * nj + j))
    return _host_call(
        body, 7, plan, lambda: (pl.program_id(0), nj),
        out_shape=[jax.ShapeDtypeStruct((2, t, f), BF16), jax.ShapeDtypeStruct((2, CONV_WIDTH, f), F32),
                   jax.ShapeDtypeStruct((2, 1, f), F32)],
        grid=(nj,),
        in_specs=[blk(t, 0), blk(t, 1), blk(CONV_WIDTH, 0), blk(CONV_WIDTH, 1), blk(1, 0), blk(1, 1),
                  pl.BlockSpec((t, tc), lambda j: (0, j))],
        out_specs=[pl.BlockSpec((2, t, tc), lambda j: (0, 0, j)),
                   pl.BlockSpec((2, CONV_WIDTH, tc), lambda j: (0, 0, j)),
                   pl.BlockSpec((2, 1, tc), lambda j: (0, 0, j))],
        scratch_shapes=[pltpu.VMEM((t, tc), F32), pltpu.VMEM((t, tc), F32)],
        name=name, sem=("parallel",), args=[up, up, conv_w, conv_w, conv_b, conv_b, dact])


def _place():
    x, y, c = lax.axis_index("x"), lax.axis_index("y"), lax.axis_index("c")
    others = [(1 - x, y), (x, 1 - y), (1 - x, 1 - y)]
    return x, y, c, others


def _window(ref, axis, b, n):
    if axis == 1:
        return ref.at[:, pl.ds(b * n, n), :]
    return ref.at[:, :, pl.ds(b * n, n)]


def _remote(src, dst, send_sems, recv_sems, k, to):
    return pltpu.make_async_remote_copy(src_ref=src, dst_ref=dst, send_sem=send_sems.at[k],
                                        recv_sem=recv_sems.at[k], device_id=to, device_id_type=MESH)


class GatherPlan:
    has_mid = True

    def __init__(self, items):
        self.items = items
        self.args = [s for s, _, _, _ in items]
        self.out_shape = []
        for s, _, nl, ax in items:
            shp = [nl, s.shape[1], s.shape[2]]
            shp[ax] *= N_DEV
            self.out_shape.append(jax.ShapeDtypeStruct(tuple(shp), s.dtype))
        n = len(items)
        self.scratch = [pltpu.SemaphoreType.DMA((7 * n,)), pltpu.SemaphoreType.DMA((7 * n,)),
                        pltpu.SemaphoreType.DMA((n,))]

    def _mine(self, ins, a):
        _, l0, nl, _ = self.items[a]
        return ins[a].at[pl.ds(l0, nl)]

    def _copy(self, ins, outs, sems, a, k, block, to, own=False):
        s, _, _, ax = self.items[a]
        px, py, pc = block
        w = _window(outs[a], ax, 4 * px + 2 * py + pc, s.shape[ax])
        return _remote(self._mine(ins, a) if own else w, w, sems[0], sems[1], 7 * a + k, to)

    def _local(self, ins, outs, sems, a, x, y, c):
        s, _, _, ax = self.items[a]
        return pltpu.make_async_copy(self._mine(ins, a), _window(outs[a], ax, 4 * x + 2 * y + c, s.shape[ax]),
                                     sems[2].at[a])


    def _first(self, ins, outs, sems, a, x, y, c, others):
        me = (x, y, c)
        return [self._copy(ins, outs, sems, a, 0, me, (x, y, 1 - c), own=True)] + [
            self._copy(ins, outs, sems, a, 1 + j, me, (*chip, c), own=True) for j, chip in enumerate(others[:2])]

    def _relay(self, ins, outs, sems, a, x, y, c):
        block = ((1 - c) * (1 - x) + c * x, (1 - c) * y + c * (1 - y), c)
        to = ((1 - c) * x + c * (1 - x), (1 - c) * (1 - y) + c * y, c)
        return self._copy(ins, outs, sems, a, 3, block, to)

    def _passed(self, ins, outs, sems, a, j, x, y, c, others):
        return self._copy(ins, outs, sems, a, 4 + j, (*others[j], c), (x, y, 1 - c))

    def start(self, ins, outs, sems):
        x, y, c, others = _place()
        for a in range(len(self.items)):
            self._local(ins, outs, sems, a, x, y, c).start()
        for a in range(len(self.items)):
            for cp in self._first(ins, outs, sems, a, x, y, c, others):
                cp.start()

    def mid(self, ins, outs, sems):
        x, y, c, others = _place()
        for a in range(len(self.items)):
            for j, chip in enumerate(others[:2]):
                self._copy(ins, outs, sems, a, 1 + j, (*chip, c), (x, y, c)).wait_recv()
            self._relay(ins, outs, sems, a, x, y, c).start()
            for j in range(2):
                self._passed(ins, outs, sems, a, j, x, y, c, others).start()

    def finish(self, ins, outs, sems):
        x, y, c, others = _place()
        for a in range(len(self.items)):
            self._copy(ins, outs, sems, a, 3, (*others[2], c), (x, y, c)).wait_recv()
            self._passed(ins, outs, sems, a, 2, x, y, c, others).start()
        for a in range(len(self.items)):
            self._copy(ins, outs, sems, a, 0, (x, y, 1 - c), (x, y, c)).wait_recv()
            for j, chip in enumerate(others):
                self._copy(ins, outs, sems, a, 4 + j, (*chip, 1 - c), (x, y, c)).wait_recv()
        for a in range(len(self.items)):
            for cp in self._first(ins, outs, sems, a, x, y, c, others):
                cp.wait_send()
            self._relay(ins, outs, sems, a, x, y, c).wait_send()
            for j in range(3):
                self._passed(ins, outs, sems, a, j, x, y, c, others).wait_send()
            self._local(ins, outs, sems, a, x, y, c).wait()


class ExchangePlan:
    has_mid = False

    def __init__(self, partials):
        self.args = list(partials)
        self.out_shape = [jax.ShapeDtypeStruct(p.shape, p.dtype) for p in partials]
        n = len(partials)
        self.scratch = [pltpu.SemaphoreType.DMA((3 * n,)), pltpu.SemaphoreType.DMA((3 * n,)),
                        pltpu.SemaphoreType.DMA((n,))]

    def _copies(self, ins, outs, sems):
        x, y, c, others = _place()
        me = 2 * x + y
        local, sends, recvs = [], [], []
        for a in range(len(self.args)):
            local.append(pltpu.make_async_copy(ins[a].at[me], outs[a].at[me], sems[2].at[a]))
            for j, (px, py) in enumerate(others):
                sends.append(_remote(ins[a].at[2 * px + py], outs[a].at[me], sems[0], sems[1], 3 * a + j, (px, py, c)))
                slot = outs[a].at[2 * px + py]
                recvs.append(_remote(slot, slot, sems[0], sems[1], 3 * a + j, (px, py, c)))
        return local, sends, recvs

    def start(self, ins, outs, sems):
        local, sends, _ = self._copies(ins, outs, sems)
        for cp in local + sends:
            cp.start()

    def finish(self, ins, outs, sems):
        local, sends, recvs = self._copies(ins, outs, sems)
        for cp in recvs:
            cp.wait_recv()
        for cp in sends:
            cp.wait_send()
        for cp in local:
            cp.wait()


def run_plan(plan, name):
    ni, no = len(plan.args), len(plan.out_shape)

    def body(*refs):
        ins, outs, sems = refs[:ni], refs[ni:ni + no], refs[ni + no:]
        plan.start(ins, outs, sems)
        if plan.has_mid:
            plan.mid(ins, outs, sems)
        plan.finish(ins, outs, sems)

    return pl.pallas_call(
        body, out_shape=plan.out_shape, in_specs=[ANY] * ni, out_specs=[ANY] * no,
        scratch_shapes=plan.scratch, name=name,
    )(*plan.args)


class SiblingPlan:
    has_mid = False

    def __init__(self, grads, axes):
        self.args, self.axes = list(grads), list(axes)
        self.widths = [g.shape[ax] // N_DEV for g, ax in zip(grads, axes)]
        self.out_shape = []
        for g, ax, n in zip(grads, axes, self.widths):
            shp = list(g.shape)
            shp[ax] = n
            self.out_shape.append(jax.ShapeDtypeStruct((N_CHIP, *shp), g.dtype))
        n = len(grads)
        self.scratch = [pltpu.SemaphoreType.DMA((N_CHIP * n,)), pltpu.SemaphoreType.DMA((N_CHIP * n,))]

    def _copies(self, ins, outs, sems):
        x, y, c, _ = _place()
        copies = []
        for a in range(len(self.args)):
            for q in range(N_CHIP):
                src = _window(ins[a], self.axes[a], 2 * q + (1 - c), self.widths[a])
                copies.append(_remote(src, outs[a].at[q], sems[0], sems[1], N_CHIP * a + q, (x, y, 1 - c)))
        return copies

    def start(self, ins, outs, sems):
        for cp in self._copies(ins, outs, sems):
            cp.start()

    def finish(self, ins, outs, sems):
        copies = self._copies(ins, outs, sems)
        for cp in copies:
            cp.wait_recv()
        for cp in copies:
            cp.wait_send()


def _peer_of(k, x, y, c):
    return (1 - x if k & 4 else x, 1 - y if k & 2 else y, 1 - c if k & 1 else c)


def small_exchange(vec, reduce, name):
    r = vec.shape[0]

    def body(v_ref, o_ref, *scratch):
        if reduce:
            buf, send_sems, recv_sems = scratch
        else:
            buf, (send_sems, recv_sems) = o_ref, scratch
        x, y, c, _ = _place()
        me = 4 * x + 2 * y + c
        copies = []
        for k in range(1, N_DEV):
            cp = _remote(v_ref, buf.at[me], send_sems, recv_sems, k - 1, _peer_of(k, x, y, c))
            cp.start()
            copies.append(cp)
        buf[me] = v_ref[...]
        for k in range(1, N_DEV):
            px, py, pc = _peer_of(k, x, y, c)
            slot = buf.at[4 * px + 2 * py + pc]
            _remote(slot, slot, send_sems, recv_sems, k - 1, (px, py, pc)).wait_recv()
        for cp in copies:
            cp.wait_send()
        if reduce:
            tot = buf[0]
            for b in range(1, N_DEV):
                tot = tot + buf[b]
            o_ref[...] = tot

    sems = [pltpu.SemaphoreType.DMA((N_DEV - 1,)), pltpu.SemaphoreType.DMA((N_DEV - 1,))]
    if reduce:
        out_shape = jax.ShapeDtypeStruct((r, LANES), F32)
        scratch = [pltpu.VMEM((N_DEV, r, LANES), F32)] + sems
    else:
        out_shape = jax.ShapeDtypeStruct((N_DEV, r, LANES), F32)
        scratch = sems
    return pl.pallas_call(
        body, out_shape=out_shape, in_specs=[VMEM_FULL], out_specs=VMEM_FULL,
        scratch_shapes=scratch, name=name, compiler_params=_params(None),
    )(vec)


def _row_tile(rows, cols, limit=ELEMWISE_BLOCK_ELEMS):
    best = None
    for tb in range(16, rows + 1, 16):
        if rows % tb == 0 and tb * cols <= limit:
            best = tb
    return best if best is not None else rows


def add_sibling(grad, recv, axis, core):
    nl = grad.shape[0]
    _, _, r, cc = recv.shape
    tb = _row_tile(r, cc, 4 * ELEMWISE_BLOCK_ELEMS)
    per = r // tb

    def body(c_ref, g_ref, r_ref, o_ref):
        del c_ref
        o_ref[...] = (g_ref[...].astype(F32) + r_ref[...].astype(F32)).astype(BF16)

    if axis == 2:
        g_spec = pl.BlockSpec((None, tb, cc), lambda q, l, i, c_ref: (l, i, 2 * q + c_ref[0]))
    else:
        g_spec = pl.BlockSpec((None, tb, cc), lambda q, l, i, c_ref: (l, (2 * q + c_ref[0]) * per + i, 0))
    slot = pl.BlockSpec((None, None, tb, cc), lambda q, l, i, c_ref: (q, l, i, 0))
    return pl.pallas_call(
        body, out_shape=jax.ShapeDtypeStruct(recv.shape, BF16),
        grid_spec=pltpu.PrefetchScalarGridSpec(
            num_scalar_prefetch=1, grid=(N_CHIP, nl, per), in_specs=[g_spec, slot], out_specs=slot),
        name="add_sibling", compiler_params=_params(("parallel", "parallel", "parallel")),
    )(core, grad, recv)


def _adamw(w, g, m, v):
    m = ADAM_B1 * m + (1.0 - ADAM_B1) * g
    v = ADAM_B2 * v + (1.0 - ADAM_B2) * (g * g)
    m_hat = m / (1.0 - ADAM_B1 ** ADAM_STEP)
    v_hat = v / (1.0 - ADAM_B2 ** ADAM_STEP)
    delta = -ADAM_LR * (m_hat / (jnp.sqrt(v_hat) + ADAM_EPS) + ADAM_WD * w)
    return delta, m, v


def adam_from_partials(recv, w, m, v, l0, bufs):
    nl = recv.shape[1]
    _, r, cc = w.shape
    tb = _row_tile(r, cc)

    def body(p0, p1, p2, p3, w_ref, m_ref, v_ref, b0, b1, b2, b3, g_out, d_out, m_out, v_out):
        del b0, b1, b2, b3
        g = p0[...].astype(F32) + p1[...].astype(F32) + p2[...].astype(F32) + p3[...].astype(F32)
        d, mn, vn = _adamw(w_ref[...], g, m_ref[...], v_ref[...])
        g_out[...], d_out[...], m_out[...], v_out[...] = g, d, mn, vn

    slot = lambda q: pl.BlockSpec((None, None, tb, cc), lambda l, i: (q, l, i, 0))
    blk = pl.BlockSpec((None, tb, cc), lambda l, i: (l0 + l, i, 0))
    shp = jax.ShapeDtypeStruct(w.shape, F32)
    return pl.pallas_call(
        body, out_shape=[shp] * 4, grid=(nl, r // tb),
        in_specs=[slot(0), slot(1), slot(2), slot(3), blk, blk, blk] + [ANY] * 4, out_specs=[blk] * 4,
        input_output_aliases={7: 0, 8: 1, 9: 2, 10: 3},
        name="adam_big", compiler_params=_params(("parallel", "parallel")),
    )(recv, recv, recv, recv, w, m, v, *bufs)


def adam_small(w, g, m, v):
    def body(w_ref, g_ref, m_ref, v_ref, d_out, m_out, v_out):
        d_out[...], m_out[...], v_out[...] = _adamw(w_ref[...], g_ref[...], m_ref[...], v_ref[...])

    shp = jax.ShapeDtypeStruct(w.shape, F32)
    return pl.pallas_call(body, out_shape=[shp] * 3, name="adam_small")(w, g, m, v)


def _pack(arrays, multiple=8 * LANES):
    flat = jnp.concatenate([a.reshape(-1) for a in arrays])
    pad = (-flat.shape[0]) % multiple
    if pad:
        flat = jnp.concatenate([flat, jnp.zeros((pad,), flat.dtype)])
    return flat.reshape(-1, LANES)


def _unpack(packed, shapes):
    flat = packed.reshape(packed.shape[:-2] + (-1,))
    out, off = [], 0
    for shp in shapes:
        n = math.prod(shp)
        out.append(flat[..., off:off + n].reshape(packed.shape[:-2] + tuple(shp)))
        off += n
    return out


def _unshard_last(stacked):
    moved = jnp.moveaxis(stacked, 0, -2)
    return moved.reshape(moved.shape[:-2] + (-1,))


def _shard_last(full, block):
    n = full.shape[-1] // N_DEV
    return lax.dynamic_slice_in_dim(full, block * n, n, axis=full.ndim - 1)


BIG_AXIS = {"w_in": 2, "w_out": 1, "w_qkv": 2, "w_so": 1, "w_pool": 1, "w_up": 2, "w_down": 1}
BIG_ORDER = ("w_in", "w_out", "w_qkv", "w_so", "w_pool", "w_up", "w_down")

GATHER_HOSTS = {
    "hg_in_0": (("w_out", 0),),
    "hgrn_fwd_0": (("w_up", 0),),
    "ffn_up_0": (("w_down", 0),),
    "glu_fwd_0": (("w_qkv", 0),),
    "sb_qkv_1": (("w_so", 0),),
    "sba_fwd_1": (("w_up", 1), ("w_pool", 0), ("w_up", 2)),
    "ffn_up_1": (("w_down", 1),),
    "glu_fwd_1": (("w_down", 2),),
    "ffn_up_2": (("w_in", 1),),
    "ffn_down_2": (("w_out", 1),),
    "hgrn_fwd_3": (("w_up", 3),),
    "ffn_up_3": (("w_down", 3),),
}


MEGA = 1 << 20
EXCHANGE_BUDGET = {"d_u2": 13 * MEGA, "dw_up": 13 * MEGA, "hgrn_bwd": 30 * MEGA, "sba_bwd": 70 * MEGA,
                   "d_u1_0": 17 * MEGA}


class MeshWeights:
    def __init__(self, shards16, w, m, v, core):
        self.shards, self.w, self.m, self.v, self.core = shards16, w, m, v, core
        self.full, self.pending, self.ready, self.flying, self.sib_flying, self.keys = {}, [], [], [], [], None
        self.out = {k: [lax.empty(w[k].shape, F32) for _ in range(4)] for k in BIG_ORDER}

    def _items(self, keys):
        return [(self.shards[k], 0 if k == "w_pool" else l, 4 if k == "w_pool" else 1, BIG_AXIS[k]) for k, l in keys]

    def gather_plan(self, host):
        self.keys = [key for key in GATHER_HOSTS.get(host, ()) if key[0] in self.shards
                     and key[1] < (1 if key[0] == "w_pool" else self.shards[key[0]].shape[0])]
        return GatherPlan(self._items(self.keys)) if self.keys else None

    def gathered(self, outs):
        for (k, l), o in zip(self.keys, outs):
            self.full[(k, l)] = o if k == "w_pool" else o[0]
        self.keys = None

    def weight(self, kind, l):
        if (kind, l) not in self.full:
            (out,) = run_plan(GatherPlan(self._items([(kind, l)])), f"gather_{kind}_{l}")
            self.full[(kind, l)] = out if kind == "w_pool" else out[0]
        return self.full[(kind, l)]

    def grad(self, kind, l, g):
        self.pending.append((kind, l, g if g.ndim == 3 else g[None]))

    def exchange_plan(self, host=None):
        budget = next((b for prefix, b in EXCHANGE_BUDGET.items() if host.startswith(prefix)), 0) if host else None
        used = 0

        def fits(queue):
            nonlocal used
            take, keep = [], []
            for item in queue:
                size = item[2].size // (N_CHIP if queue is self.ready else N_DEV)
                if budget is None or (used + size) * N_DEV <= budget:
                    take.append(item)
                    used += size
                else:
                    keep.append(item)
            return take, keep

        done, self.ready = fits(self.ready)
        raw, self.pending = fits(self.pending)
        if raw:
            done = done + self._partial_sums(raw, run_plan(self._sibling_plan(raw), "sibling_exchange"))
        if not done:
            return None
        self.flying = done
        return ExchangePlan([p for _, _, p in done])

    def exchanged(self, outs):
        for (k, l, _), r in zip(self.flying, outs):
            self.out[k] = adam_from_partials(r, self.w[k], self.m[k], self.v[k], l, self.out[k])
        self.flying = []

    def _sibling_plan(self, items):
        return SiblingPlan([g for _, _, g in items], [BIG_AXIS[k] for k, _, _ in items])

    def _partial_sums(self, items, recv):
        return [(k, l, add_sibling(g, r, BIG_AXIS[k], self.core)) for (k, l, g), r in zip(items, recv)]

    def sibling_plan(self):
        self.sib_flying, self.pending = self.pending, []
        return self._sibling_plan(self.sib_flying) if self.sib_flying else None

    def sibling_done(self, outs):
        self.ready += self._partial_sums(self.sib_flying, outs)
        self.sib_flying = []

    def finish(self):
        plan = self.exchange_plan()
        if plan is not None:
            self.exchanged(run_plan(plan, "chip_exchange_tail"))
        return self.out


def train_step(x, target, norm_g, lb_logits, onorm_g, pool_scale, conv_w, conv_b, wts):
    t, d = x.shape
    depth = norm_g.shape[0]
    ng = lambda i, j: norm_g[i, j].reshape(1, d)
    lbs = lower_bounds_fwd(lb_logits)
    saved = []
    h = x
    _, u16, u32 = res_norm(h, None, None, ng(0, 0), "norm_in")

    def hosted(call, *args, **kw):
        plan = wts.gather_plan(kw["name"])
        out, extra = call(*args, plan=plan, **kw)
        if plan is not None:
            wts.gathered(extra)
        return out

    def project(a, kind, l, name):
        plan = wts.gather_plan(name)
        if plan is None:
            return matmul(a, wts.weight(kind, l), "nn", F32, name)
        out, extra = matmul(a, wts.weight(kind, l), "nn", F32, name, plan=plan)
        wts.gathered(extra)
        return out

    for i in range(depth):
        kind, j = i % 3, i // 3
        s = {"h_in": h, "u1": u16}
        if kind == 0:
            s["proj"] = project(u16, "w_in", j, f"hg_in_{i}")
            s["y"] = hosted(hgrn_fwd, s["proj"], lbs[i].reshape(1, -1), onorm_g[j].reshape(1, -1), i > 0,
                            name=f"hgrn_fwd_{i}")
            mix = project(s["y"], "w_out", j, f"hg_out_{i}")
        elif kind == 1:
            s["qkv"] = project(u16, "w_qkv", j, f"sb_qkv_{i}")
            s["o"] = hosted(sba_fwd, s["qkv"], name=f"sba_fwd_{i}")
            mix = project(s["o"], "w_so", j, f"sb_out_{i}")
        else:
            s["u1f"] = u32
            mix = pool_fwd(u32, wts.weight("w_pool", j), pool_scale[j].reshape(1, d), f"pool_fwd_{i}")
        s["mix"] = mix
        h_mid, u2, _ = res_norm(h, mix, ng(i, 1), ng(i, 2), f"norm_mid_{i}")
        s["h_mid"], s["u2"] = h_mid, u2
        s["up"] = project(u2, "w_up", i, f"ffn_up_{i}")
        s["act"] = hosted(conv_glu_fwd, s["up"], conv_w[i], conv_b[i].reshape(1, -1), name=f"glu_fwd_{i}")
        s["f"] = project(s["act"], "w_down", i, f"ffn_down_{i}")
        nxt = ng(i + 1, 0) if i + 1 < depth else None
        h, u16, u32 = res_norm(h_mid, s["f"], ng(i, 3), nxt, f"norm_out_{i}",
                               want_f32=(nxt is not None and (i + 1) % 3 == 2))
        saved.append(s)

    loss_acc, dh = loss_head(h, target, "loss_head")

    d_norm = [[None] * 4 for _ in range(depth)]
    d_lbs = jnp.zeros_like(lbs)
    d_onorm = [None] * onorm_g.shape[0]
    d_pscale = [None] * pool_scale.shape[0]
    d_cw, d_cb = [None] * depth, [None] * depth

    def exchanging(call, *args, **kw):
        plan = wts.exchange_plan(kw["name"])
        out, extra = call(*args, plan=plan, **kw)
        if plan is not None:
            wts.exchanged(extra)
        return out

    def bmm(a, b, mode, dtype, name):
        plan = wts.exchange_plan(name)
        if plan is None:
            return matmul(a, b, mode, dtype, name)
        out, extra = matmul(a, b, mode, dtype, name, plan=plan)
        wts.exchanged(extra)
        return out

    for i in reversed(range(depth)):
        kind, j = i % 3, i // 3
        s = saved[i]
        df, d_norm[i][3] = norm_bwd(s["f"], ng(i, 3), dh, None, BF16, f"nb_out_{i}")
        dact = bmm(df, wts.weight("w_down", i), "nt", F32, f"d_act_{i}")
        wts.grad("w_down", i, bmm(s["act"], df, "tn", BF16, f"dw_down_{i}"))
        sib = wts.sibling_plan()
        (dup, dcw, dcb), extra = conv_glu_bwd(s["up"], conv_w[i], conv_b[i].reshape(1, -1), dact, f"glu_bwd_{i}",
                                              plan=sib)
        if sib is not None:
            wts.sibling_done(extra)
        d_cw[i] = jnp.moveaxis(dcw, 0, 1).reshape(CONV_WIDTH, -1)
        d_cb[i] = dcb.reshape(-1)
        du2 = bmm(Split(dup), wts.weight("w_up", i), "nt", F32, f"d_u2_{i}")
        wts.grad("w_up", i, bmm(s["u2"], Split(dup), "tn", BF16, f"dw_up_{i}"))
        dh_mid, d_norm[i][2] = norm_bwd(s["h_mid"], ng(i, 2), du2, dh, F32, f"nb_mid_{i}")
        dm, d_norm[i][1] = norm_bwd(s["mix"], ng(i, 1), dh_mid, None, F32 if kind == 2 else BF16, f"nb_mix_{i}")
        if kind == 0:
            dy = matmul(dm, wts.weight("w_out", j), "nt", F32, f"d_y_{i}")
            wts.grad("w_out", j, matmul(s["y"], dm, "tn", BF16, f"dw_hgout_{i}"))
            dproj, d_onorm[j], dlb = exchanging(hgrn_bwd, s["proj"], lbs[i].reshape(1, -1),
                                                onorm_g[j].reshape(1, -1), dy, i > 0, name=f"hgrn_bwd_{i}")
            d_lbs = d_lbs.at[i].set(dlb[0])
            wts.grad("w_in", j, bmm(s["u1"], Split(dproj), "tn", BF16, f"dw_hgin_{i}"))
            du1 = bmm(Split(dproj), wts.weight("w_in", j), "nt", F32, f"d_u1_{i}")
        elif kind == 1:
            do = matmul(dm, wts.weight("w_so", j), "nt", F32, f"d_o_{i}")
            wts.grad("w_so", j, matmul(s["o"], dm, "tn", BF16, f"dw_sbout_{i}"))
            dqkv = exchanging(sba_bwd, s["qkv"], s["o"], do, name=f"sba_bwd_{i}")
            du1 = bmm(Split(dqkv), wts.weight("w_qkv", j), "nt", F32, f"d_u1_{i}")
            wts.grad("w_qkv", j, bmm(s["u1"], Split(dqkv), "tn", BF16, f"dw_sbqkv_{i}"))
        else:
            du1, g_pool, d_pscale[j] = pool_bwd(s["u1f"], wts.weight("w_pool", j), pool_scale[j].reshape(1, d),
                                                dm, f"pool_bwd_{i}")
            wts.grad("w_pool", j, g_pool)
        dh, d_norm[i][0] = norm_bwd(s["h_in"], ng(i, 0), du1, dh_mid, F32, f"nb_in_{i}")

    small = {
        "norm_g": jnp.stack([jnp.stack([v.reshape(d) for v in row]) for row in d_norm]),
        "lb_logits": lower_bounds_bwd(lb_logits, d_lbs),
        "onorm_g": jnp.stack([v.reshape(-1) for v in d_onorm]),
        "pool_scale": jnp.stack([v.reshape(-1) for v in d_pscale]),
        "conv_w": jnp.stack(d_cw),
        "conv_b": jnp.stack(d_cb),
    }
    return loss_acc, dh, small


SMALL_SHARDED = ("norm_g", "onorm_g", "pool_scale", "conv_w")
SMALL_ORDER = ("norm_g", "lb_logits", "onorm_g", "pool_scale", "conv_w", "conv_b")


def kernel(x, norm_g, hgrn_lb_logits, hgrn_w_in, hgrn_onorm_g, hgrn_w_out, sba_w_qkv, sba_w_out, pool_w, pool_scale, ffn_w_up, ffn_conv_w, ffn_conv_b, ffn_w_down, loss_target, m_norm_g, m_hgrn_lb_logits, m_hgrn_w_in, m_hgrn_onorm_g, m_hgrn_w_out, m_sba_w_qkv, m_sba_w_out, m_pool_w, m_pool_scale, m_ffn_w_up, m_ffn_conv_w, m_ffn_conv_b, m_ffn_w_down, v_norm_g, v_hgrn_lb_logits, v_hgrn_w_in, v_hgrn_onorm_g, v_hgrn_w_out, v_sba_w_qkv, v_sba_w_out, v_pool_w, v_pool_scale, v_ffn_w_up, v_ffn_conv_w, v_ffn_conv_b, v_ffn_w_down):
    cx, cy, cc = lax.axis_index("x"), lax.axis_index("y"), lax.axis_index("c")
    block = 4 * cx + 2 * cy + cc
    core = cc.astype(jnp.int32).reshape(1)

    pool3 = lambda a: a.reshape(a.shape[0] * a.shape[1], a.shape[2], a.shape[3])
    big_w = dict(zip(BIG_ORDER, [hgrn_w_in, hgrn_w_out, sba_w_qkv, sba_w_out, pool3(pool_w), ffn_w_up, ffn_w_down]))
    big_m = dict(zip(BIG_ORDER, [m_hgrn_w_in, m_hgrn_w_out, m_sba_w_qkv, m_sba_w_out, pool3(m_pool_w), m_ffn_w_up,
                                 m_ffn_w_down]))
    big_v = dict(zip(BIG_ORDER, [v_hgrn_w_in, v_hgrn_w_out, v_sba_w_qkv, v_sba_w_out, pool3(v_pool_w), v_ffn_w_up,
                                 v_ffn_w_down]))

    sharded = {"norm_g": norm_g, "onorm_g": hgrn_onorm_g, "pool_scale": pool_scale, "conv_w": ffn_conv_w}
    gathered = small_exchange(_pack([sharded[n] for n in SMALL_SHARDED]), False, "gather_small")
    parts = _unpack(gathered, [sharded[n].shape for n in SMALL_SHARDED])
    full = {n: _unshard_last(p) for n, p in zip(SMALL_SHARDED, parts)}

    wts = MeshWeights({k: w.astype(BF16) for k, w in big_w.items()}, big_w, big_m, big_v, core)
    loss_acc, grad_x, small_g = train_step(
        x[0], loss_target[0], full["norm_g"], hgrn_lb_logits, full["onorm_g"], full["pool_scale"],
        full["conv_w"], ffn_conv_b, wts)
    loss = lax.psum(loss_acc[0, 0], ("x", "y", "c"))

    shapes = [small_g[n].shape for n in SMALL_ORDER]
    summed = _unpack(small_exchange(_pack([small_g[n] for n in SMALL_ORDER]), True, "reduce_small"), shapes)
    sg = {n: (_shard_last(g, block) if n in SMALL_SHARDED else g) for n, g in zip(SMALL_ORDER, summed)}
    sw = {"norm_g": norm_g, "lb_logits": hgrn_lb_logits, "onorm_g": hgrn_onorm_g, "pool_scale": pool_scale,
          "conv_w": ffn_conv_w, "conv_b": ffn_conv_b}
    sm = {"norm_g": m_norm_g, "lb_logits": m_hgrn_lb_logits, "onorm_g": m_hgrn_onorm_g, "pool_scale": m_pool_scale,
          "conv_w": m_ffn_conv_w, "conv_b": m_ffn_conv_b}
    sv = {"norm_g": v_norm_g, "lb_logits": v_hgrn_lb_logits, "onorm_g": v_hgrn_onorm_g, "pool_scale": v_pool_scale,
          "conv_w": v_ffn_conv_w, "conv_b": v_ffn_conv_b}
    sshapes = [sw[n].shape for n in SMALL_ORDER]
    packed = [_pack([dct[n] for n in SMALL_ORDER]) for dct in (sw, sg, sm, sv)]
    s_delta, s_m, s_v = [dict(zip(SMALL_ORDER, _unpack(p, sshapes))) for p in adam_small(*packed)]

    upd = wts.finish()
    b_grad, b_delta, b_m, b_v = [[upd[k][n] for k in BIG_ORDER] for n in range(4)]

    def tree(small, bigs):
        bg = list(bigs)
        bg[4] = bg[4].reshape(pool_w.shape)
        return (small["norm_g"], small["lb_logits"], bg[0], small["onorm_g"], bg[1], bg[2], bg[3], bg[4],
                small["pool_scale"], bg[5], small["conv_w"], small["conv_b"], bg[6])

    return (loss, grad_x[None], *tree(sg, b_grad), *tree(s_delta, b_delta), *tree(s_m, b_m), *tree(s_v, b_v))
```
